```python
import jax, jax.numpy as jnp
from jax import lax
import numpy as np

D_MODEL = 1024
BATCH = 2
SEQ = 16384
DEPTH = 2
DEC_BATCH = 8
DEC_SEQ = 64
PAST_LEN = 4096

CHUNK = 64
N_EVEN = (DEPTH + 1) // 2
N_ODD = DEPTH // 2
HA = 4
DVA = D_MODEL // (2 * HA)
DKA = DVA // 2
GATE_RANK = 16
GATE_TAU = 16.0
HB = 8
DHB = D_MODEL // (2 * HB)
N_PREV_B = 8
MAX_REL = 128
HC = 16
KVC = 2
GC = HC // KVC
DHC = D_MODEL // HC
WINDOW = 128
N_PREV_C = WINDOW // CHUNK
ROPE_THETA = 10000.0
N_EXPERTS = 16
N_GROUPS = 4
EXP_PER_GROUP = N_EXPERTS // N_GROUPS
TOP_K = 2
D_FF = 512
EPS = 1e-6
EVEN_SIZES = (HA * DKA, HA * DKA, HA * DVA, HA * DVA, GATE_RANK, HB * DHB, HB * DHB, HB * DHB)
ODD_SIZES = (HC * DHC, KVC * DHC, KVC * DHC)

kernel_name = 'hybrid_streaming_gla_band_swa_moe_step'

F32 = jnp.float32


def rmsnorm(x, g):
    xf = x.astype(F32)
    y = xf * lax.rsqrt(jnp.mean(xf * xf, axis=-1, keepdims=True) + EPS)
    return (y * g.astype(F32)).astype(x.dtype)


def adaln(c, w, b):
    mod = jax.nn.silu(c) @ w + b
    return jnp.split(mod, 6, axis=-1)


def modulate(x, g, shift, scale):
    return rmsnorm(x, g) * (1 + scale[:, None]) + shift[:, None]


def split_cols(z, sizes):
    idx = [int(s) for s in np.cumsum(sizes)[:-1]]
    return jnp.split(z, idx, axis=-1)


def rope(x, pos):
    half = x.shape[-1] // 2
    inv = ROPE_THETA ** (-jnp.arange(half, dtype=F32) / half)
    ang = pos.astype(F32)[:, None] * inv[None, :]
    shp = (1, pos.shape[0]) + (1,) * (x.ndim - 3) + (half,)
    cos, sin = jnp.cos(ang).reshape(shp), jnp.sin(ang).reshape(shp)
    x1, x2 = x[..., :half].astype(F32), x[..., half:].astype(F32)
    return jnp.concatenate([x1 * cos - x2 * sin, x2 * cos + x1 * sin], axis=-1).astype(x.dtype)


def gla_scan(q, k, v, g, s0, block):
    B, L, H, dk = q.shape
    dv = v.shape[-1]
    n = L // block

    def blocks(x):
        return jnp.moveaxis(x.astype(F32).reshape((B, n, block) + x.shape[2:]), 1, 0)

    causal = jnp.tril(jnp.ones((block, block), dtype=bool))[None, :, :, None, None]

    def step(S, inp):
        qc, kc, vc, gc = inp
        b = jnp.cumsum(gc, axis=1)
        o_inter = jnp.einsum('bthk,bhkv->bthv', qc * jnp.exp(b), S)
        dec = jnp.exp(jnp.where(causal, b[:, :, None] - b[:, None, :], -jnp.inf))
        att = jnp.einsum('bthk,bshk,btshk->btsh', qc, kc, dec)
        o = o_inter + jnp.einsum('btsh,bshv->bthv', att, vc)
        bl = b[:, -1]
        S = jnp.exp(bl)[..., None] * S + jnp.einsum('bshk,bshv->bhkv', kc * jnp.exp(bl[:, None] - b), vc)
        return S, o

    S, o = lax.scan(step, s0.astype(F32), (blocks(q), blocks(k), blocks(v), blocks(g)))
    return jnp.moveaxis(o, 0, 1).reshape(B, L, H, dv), S


def chunk_band_mask(qpos, kpos, n_prev):
    d = qpos[..., :, None] // CHUNK - kpos[..., None, :] // CHUNK
    return (d >= 0) & (d <= n_prev) & (kpos[..., None, :] >= 0)


def rel_bias(table, rel):
    idx = jnp.clip(rel, -MAX_REL, MAX_REL) + MAX_REL
    return table[:, idx][:, None]


def band_attend(q, k, v, mask, bias=None, sink=None):
    s = jnp.einsum('...qhgd,...khd->...hgqk', q, k).astype(F32) * (q.shape[-1] ** -0.5)
    if bias is not None:
        s = s + bias.astype(F32)
    s = jnp.where(mask, s, -1e30)
    if sink is not None:
        sl = jnp.broadcast_to(sink.astype(F32)[:, :, None, None], s.shape[:-1] + (1,))
        p = jax.nn.softmax(jnp.concatenate([s, sl], axis=-1), axis=-1)[..., :-1]
    else:
        p = jax.nn.softmax(s, axis=-1)
    return jnp.einsum('...hgqk,...khd->...qhgd', p.astype(v.dtype), v)


def gather_band(k, n_prev):
    B, L = k.shape[:2]
    n = L // CHUNK
    kp = jnp.pad(k, ((0, 0), (n_prev * CHUNK, 0)) + ((0, 0),) * (k.ndim - 2))
    kp = kp.reshape((B, n + n_prev, CHUNK) + k.shape[2:])
    band = jnp.stack([kp[:, j:j + n] for j in range(n_prev + 1)], axis=2)
    return band.reshape((B, n, (n_prev + 1) * CHUNK) + k.shape[2:])


def band_prompt(q, k, v, n_prev, rel_table=None, sink=None):
    B, L = q.shape[:2]
    n = L // CHUNK
    klen = (n_prev + 1) * CHUNK
    qb = q.reshape((B, n, CHUNK) + q.shape[2:])
    kb, vb = gather_band(k, n_prev), gather_band(v, n_prev)
    qpos = jnp.arange(L, dtype=jnp.int32).reshape(n, CHUNK)
    kpos = jnp.arange(n, dtype=jnp.int32)[:, None] * CHUNK + jnp.arange(klen, dtype=jnp.int32)[None, :] - n_prev * CHUNK
    mask = chunk_band_mask(qpos, kpos, n_prev)[:, None, None]
    bias = None if rel_table is None else rel_bias(rel_table, kpos[0][None, :] - qpos[0][:, None])
    o = band_attend(qb, kb, vb, mask, bias, sink)
    return o.reshape(q.shape)


def band_sample(q, k, v, k_cache, v_cache, n_prev, rel_table=None, sink=None):
    T = q.shape[1]
    R = k_cache.shape[1]
    kf = jnp.concatenate([k_cache.astype(k.dtype), k], axis=1)
    vf = jnp.concatenate([v_cache.astype(v.dtype), v], axis=1)
    qpos = PAST_LEN + jnp.arange(T, dtype=jnp.int32)
    kpos = PAST_LEN - R + jnp.arange(R + T, dtype=jnp.int32)
    mask = chunk_band_mask(qpos, kpos, n_prev)[None, None]
    bias = None if rel_table is None else rel_bias(rel_table, kpos[None, :] - qpos[:, None])
    o = band_attend(q, kf, vf, mask, bias, sink)
    return o, kf[:, T:], vf[:, T:]


def even_mixer(h, w_in, w_gate, b_gate, gla_g, rel_table, w_out, gla_state=None, k_cache=None, v_cache=None):
    B, L, _ = h.shape
    prompt = gla_state is None
    qa, ka, va, ra, la, qb, kb, vb = split_cols(h @ w_in, EVEN_SIZES)
    qa = qa.reshape(B, L, HA, DKA) * (DKA ** -0.5)
    ka = ka.reshape(B, L, HA, DKA)
    va = va.reshape(B, L, HA, DVA)
    ga = (jax.nn.log_sigmoid((la @ w_gate + b_gate).astype(F32)) / GATE_TAU).reshape(B, L, HA, DKA)
    s0 = jnp.zeros((B, HA, DKA, DVA), F32) if prompt else gla_state
    oa, S = gla_scan(qa, ka, va, ga, s0, CHUNK if prompt else L)
    oa = rmsnorm(oa.astype(h.dtype), gla_g) * jax.nn.silu(ra.reshape(B, L, HA, DVA))
    qb = qb.reshape(B, L, HB, 1, DHB)
    kb = kb.reshape(B, L, HB, DHB)
    vb = vb.reshape(B, L, HB, DHB)
    if prompt:
        ob = band_prompt(qb, kb, vb, N_PREV_B, rel_table=rel_table)
        rows = min(N_PREV_B * CHUNK, L)
        kn, vn = kb[:, L - rows:], vb[:, L - rows:]
    else:
        ob, kn, vn = band_sample(qb, kb, vb, k_cache, v_cache, N_PREV_B, rel_table=rel_table)
    out = jnp.concatenate([oa.reshape(B, L, -1), ob.reshape(B, L, -1)], axis=-1) @ w_out
    return out, S, kn, vn


def odd_mixer(h, w_in, sinks, w_out, k_cache=None, v_cache=None):
    B, L, _ = h.shape
    prompt = k_cache is None
    q, k, v = split_cols(h @ w_in, ODD_SIZES)
    pos = jnp.arange(L, dtype=jnp.int32) + (0 if prompt else PAST_LEN)
    q = rope(q.reshape(B, L, KVC, GC, DHC), pos)
    k = rope(k.reshape(B, L, KVC, DHC), pos)
    v = v.reshape(B, L, KVC, DHC)
    sink = sinks.reshape(KVC, GC)
    if prompt:
        o = band_prompt(q, k, v, N_PREV_C, sink=sink)
        rows = min(WINDOW, L)
        kn, vn = k[:, L - rows:], v[:, L - rows:]
    else:
        o, kn, vn = band_sample(q, k, v, k_cache, v_cache, N_PREV_C, sink=sink)
    return o.reshape(B, L, -1) @ w_out, kn, vn


def moe(h, w_router, b_router, w_gu, w_dn):
    logits = (h @ w_router).astype(F32) + b_router.astype(F32)
    probs = jax.nn.softmax(logits, axis=-1)
    pg = probs.reshape(-1, N_GROUPS, EXP_PER_GROUP)
    gscore = jnp.sum(lax.top_k(pg, TOP_K)[0], axis=-1)
    gsel = jnp.argmax(gscore, axis=-1)
    pin = jnp.take_along_axis(pg, gsel[:, None, None], axis=1)[:, 0]
    topv, topi = lax.top_k(pin, TOP_K)
    wts = topv / jnp.sum(topv, axis=-1, keepdims=True)
    eidx = gsel[:, None] * EXP_PER_GROUP + topi
    combine = jnp.sum(jax.nn.one_hot(eidx, N_EXPERTS, dtype=F32) * wts[..., None], axis=1).astype(h.dtype)
    y = jnp.zeros_like(h)
    for e in range(N_EXPERTS):
        a, b = jnp.split(h @ w_gu[e], 2, axis=-1)
        y = y + combine[:, e:e + 1] * ((jax.nn.silu(a) * b) @ w_dn[e])
    return y


def setup_inputs(seed: int = 0) -> dict:
    key = jax.random.key(seed)
    ks = jax.random.split(key, 32)

    def nrm(k, shape, scale=1.0):
        return scale * jax.random.normal(k, shape, F32)

    rb = min(N_PREV_B * CHUNK, PAST_LEN)
    rw = min(WINDOW, PAST_LEN)
    d_even = sum(EVEN_SIZES)
    d_odd = sum(ODD_SIZES)
    return {
        'x_prompt': nrm(ks[0], (BATCH, SEQ, D_MODEL)),
        'x_sample': nrm(ks[1], (DEC_BATCH, DEC_SEQ, D_MODEL)),
        'c_prompt': nrm(ks[2], (BATCH, D_MODEL)),
        'c_sample': nrm(ks[3], (DEC_BATCH, D_MODEL)),
        'state_gla': nrm(ks[4], (N_EVEN, DEC_BATCH, HA, DKA, DVA)),
        'cache_band_k': nrm(ks[5], (N_EVEN, DEC_BATCH, rb, HB, DHB)),
        'cache_band_v': nrm(ks[6], (N_EVEN, DEC_BATCH, rb, HB, DHB)),
        'cache_swa_k': nrm(ks[7], (N_ODD, DEC_BATCH, rw, KVC, DHC)),
        'cache_swa_v': nrm(ks[8], (N_ODD, DEC_BATCH, rw, KVC, DHC)),
        'w_ada': nrm(ks[9], (DEPTH, D_MODEL, 6 * D_MODEL), 0.5 * D_MODEL ** -0.5),
        'b_ada': nrm(ks[10], (DEPTH, 6 * D_MODEL), 0.02),
        'norm_mix': 1.0 + nrm(ks[11], (DEPTH, D_MODEL), 0.05),
        'norm_ffn': 1.0 + nrm(ks[12], (DEPTH, D_MODEL), 0.05),
        'norm_final': 1.0 + nrm(ks[13], (D_MODEL,), 0.05),
        'w_in_even': nrm(ks[14], (N_EVEN, D_MODEL, d_even), D_MODEL ** -0.5),
        'w_gate_a': nrm(ks[15], (N_EVEN, GATE_RANK, HA * DKA), GATE_RANK ** -0.5),
        'b_gate_a': nrm(ks[16], (N_EVEN, HA * DKA), 0.1),
        'gla_norm': 1.0 + nrm(ks[17], (N_EVEN, DVA), 0.05),
        'rel_bias_b': nrm(ks[18], (N_EVEN, HB, 2 * MAX_REL + 1), 0.5),
        'w_out_even': nrm(ks[19], (N_EVEN, HA * DVA + HB * DHB, D_MODEL), (HA * DVA + HB * DHB) ** -0.5),
        'w_in_odd': nrm(ks[20], (N_ODD, D_MODEL, d_odd), D_MODEL ** -0.5),
        'sinks_c': nrm(ks[21], (N_ODD, HC), 1.0),
        'w_out_odd': nrm(ks[22], (N_ODD, HC * DHC, D_MODEL), (HC * DHC) ** -0.5),
        'w_router': nrm(ks[23], (D_MODEL, N_EXPERTS), D_MODEL ** -0.5),
        'b_router': nrm(ks[24], (N_EXPERTS,), 0.01),
        'w_gate_up': nrm(ks[25], (DEPTH, N_EXPERTS, D_MODEL, 2 * D_FF), D_MODEL ** -0.5),
        'w_down': nrm(ks[26], (DEPTH, N_EXPERTS, D_FF, D_MODEL), D_FF ** -0.5),
    }


def reference(x_prompt, x_sample, c_prompt, c_sample, state_gla, cache_band_k, cache_band_v, cache_swa_k, cache_swa_v,
              w_ada, b_ada, norm_mix, norm_ffn, norm_final, w_in_even, w_gate_a, b_gate_a, gla_norm, rel_bias_b,
              w_out_even, w_in_odd, sinks_c, w_out_odd, w_router, b_router, w_gate_up, w_down):
    xp, xs = x_prompt, x_sample
    B, L, D = xp.shape
    Bd, T, _ = xs.shape
    gla_p, gla_s, bk_p, bv_p, bk_s, bv_s = [], [], [], [], [], []
    sk_p, sv_p, sk_s, sv_s = [], [], [], []
    for l in range(DEPTH):
        mp = adaln(c_prompt, w_ada[l], b_ada[l])
        ms = adaln(c_sample, w_ada[l], b_ada[l])
        hp = modulate(xp, norm_mix[l], mp[0], mp[1])
        hs = modulate(xs, norm_mix[l], ms[0], ms[1])
        if l % 2 == 0:
            i = l // 2
            prm = (w_in_even[i], w_gate_a[i], b_gate_a[i], gla_norm[i], rel_bias_b[i], w_out_even[i])
            op, s_p, k_p, v_p = even_mixer(hp, *prm)
            os_, s_s, k_s, v_s = even_mixer(hs, *prm, state_gla[i], cache_band_k[i], cache_band_v[i])
            gla_p.append(s_p); gla_s.append(s_s)
            bk_p.append(k_p); bv_p.append(v_p); bk_s.append(k_s); bv_s.append(v_s)
        else:
            i = l // 2
            op, k_p, v_p = odd_mixer(hp, w_in_odd[i], sinks_c[i], w_out_odd[i])
            os_, k_s, v_s = odd_mixer(hs, w_in_odd[i], sinks_c[i], w_out_odd[i], cache_swa_k[i], cache_swa_v[i])
            sk_p.append(k_p); sv_p.append(v_p); sk_s.append(k_s); sv_s.append(v_s)
        xp = xp + mp[2][:, None] * op
        xs = xs + ms[2][:, None] * os_
        hp = modulate(xp, norm_ffn[l], mp[3], mp[4])
        hs = modulate(xs, norm_ffn[l], ms[3], ms[4])
        f = moe(jnp.concatenate([hp.reshape(B * L, D), hs.reshape(Bd * T, D)], axis=0),
                w_router, b_router, w_gate_up[l], w_down[l])
        xp = xp + mp[5][:, None] * f[:B * L].reshape(B, L, D)
        xs = xs + ms[5][:, None] * f[B * L:].reshape(Bd, T, D)
    y_prompt = rmsnorm(xp, norm_final)
    y_sample = rmsnorm(xs, norm_final)
    return (y_prompt, y_sample, jnp.stack(gla_p), jnp.stack(gla_s), jnp.stack(bk_p), jnp.stack(bv_p),
            jnp.stack(bk_s), jnp.stack(bv_s), jnp.stack(sk_p), jnp.stack(sv_p), jnp.stack(sk_s), jnp.stack(sv_s))
```

```python
import functools

import numpy as np
import jax
import jax.numpy as jnp
from jax import lax
from jax.experimental import pallas as pl
from jax.experimental.pallas import tpu as pltpu

F32 = jnp.float32
BF16 = jnp.bfloat16

D_MODEL = 1024
DEPTH = 2
CHUNK = 64
PAST_LEN = 4096
HA, DKA, DVA = 4, 64, 128
GATE_RANK = 16
GATE_TAU = 16.0
HB, DHB = 8, 64
N_PREV_B = 8
MAX_REL = 128
HC, KVC, DHC = 16, 2, 64
WINDOW = 128
ROPE_THETA = 10000.0
N_EXPERTS = 16
N_GROUPS = 4
EXP_PER_GROUP = 4
D_FF = 512
EPS = 1e-6

TM = 512
SEQ_ROWS = 16
SUB = 16
NEG = -1e30


def _cparams(sem, vmem_mb=48):
    return pltpu.CompilerParams(dimension_semantics=sem, vmem_limit_bytes=vmem_mb * 1024 * 1024)


def _dot(a, b):
    return jnp.dot(a, b, preferred_element_type=F32)


def _dot_nt(a, b):
    return lax.dot_general(a, b, (((1,), (1,)), ((), ())), preferred_element_type=F32)


def _split(a):
    hi = a.astype(BF16)
    lo = (a - hi.astype(F32)).astype(BF16)
    return hi, lo


def _dot3(a, b):
    ah, al = _split(a)
    bh, bl = _split(b)
    return _dot(ah, bh) + _dot(ah, bl) + _dot(al, bh)


def _sigmoid(x):
    return 1.0 / (1.0 + jnp.exp(-x))


def _norm_mod(x, g, shift, scale):
    t, d = x.shape
    ms = jnp.mean(x * x, axis=-1, keepdims=True)
    y = x * lax.rsqrt(ms + EPS) * g
    y = y.reshape(t // CHUNK, CHUNK, d) * (1.0 + scale) + shift
    return y.reshape(t, d)


def _ada_kernel(c_ref, w_ref, b_ref, o_ref):
    c = c_ref[...]
    o_ref[0] = _dot3(c * _sigmoid(c), w_ref[0]) + b_ref[0]


def _ada_call(c16, w_ada, b_ada):
    d = D_MODEL
    tn = 1024
    return pl.pallas_call(
        _ada_kernel,
        grid=(DEPTH, 6 * d // tn),
        in_specs=[pl.BlockSpec((SEQ_ROWS, d), lambda l, j: (0, 0)),
                  pl.BlockSpec((1, d, tn), lambda l, j: (l, 0, j)),
                  pl.BlockSpec((1, 1, tn), lambda l, j: (l, 0, j))],
        out_specs=pl.BlockSpec((1, SEQ_ROWS, tn), lambda l, j: (l, 0, j)),
        out_shape=jax.ShapeDtypeStruct((DEPTH, SEQ_ROWS, 6 * d), F32),
        compiler_params=_cparams(("arbitrary", "arbitrary")),
        name="ada",
    )(c16, w_ada, b_ada.reshape(DEPTH, 1, 6 * d))


def _inproj_even_kernel(x_ref, sh_ref, sc_ref, g_ref, w_ref, wla_ref, wg_ref, bg_ref,
                        qa_ref, ka_ref, va_ref, ra_ref, qb_ref, kb_ref, vb_ref, ga_ref):
    hb = _norm_mod(x_ref[...], g_ref[...], sh_ref[...], sc_ref[...]).astype(BF16)
    qa_ref[...] = (_dot(hb, w_ref[:, 0:256]) * (DKA ** -0.5)).astype(BF16)
    ka_ref[...] = _dot(hb, w_ref[:, 256:512]).astype(BF16)
    va_ref[...] = _dot(hb, w_ref[:, 512:1024]).astype(BF16)
    ra_ref[...] = _dot(hb, w_ref[:, 1024:1536]).astype(BF16)
    qb_ref[...] = (_dot(hb, w_ref[:, 1536:2048]) * (DHB ** -0.5)).astype(BF16)
    kb_ref[...] = _dot(hb, w_ref[:, 2048:2560]).astype(BF16)
    vb_ref[...] = _dot(hb, w_ref[:, 2560:3072]).astype(BF16)
    la = _dot(hb, wla_ref[...])
    gl = _dot3(la, wg_ref[...]) + bg_ref[...]
    ga_ref[...] = -(jnp.maximum(-gl, 0.0) + jnp.log(1.0 + jnp.exp(-jnp.abs(gl)))) * (1.0 / GATE_TAU)


def _inproj_even_call(x, shift, scale, g, w_main, w_la, w_gate, b_gate):
    n, d = x.shape
    grp = TM // CHUNK
    row = lambda i: (i, 0)
    const = lambda i: (0, 0)
    mod = lambda i: (i, 0, 0)
    widths = (256, 256, 512, 512, 512, 512, 512)
    out_shape = [jax.ShapeDtypeStruct((n, w), BF16) for w in widths] + [jax.ShapeDtypeStruct((n, 256), F32)]
    out_specs = [pl.BlockSpec((TM, w), row) for w in widths] + [pl.BlockSpec((TM, 256), row)]
    return pl.pallas_call(
        _inproj_even_kernel,
        grid=(n // TM,),
        in_specs=[pl.BlockSpec((TM, d), row),
                  pl.BlockSpec((grp, 1, d), mod), pl.BlockSpec((grp, 1, d), mod),
                  pl.BlockSpec((1, d), const),
                  pl.BlockSpec(w_main.shape, const), pl.BlockSpec(w_la.shape, const),
                  pl.BlockSpec(w_gate.shape, const), pl.BlockSpec(b_gate.shape, const)],
        out_specs=out_specs, out_shape=out_shape,
        compiler_params=_cparams(("parallel",)),
        name="inproj_even",
    )(x, shift, scale, g, w_main, w_la, w_gate, b_gate)


def _rope(x, cos, sin_signed):
    t, w = x.shape
    lane = lax.broadcasted_iota(jnp.int32, (1, w), 1)
    first_half = (lane & 63) < 32
    rot = jnp.where(first_half, pltpu.roll(x, w - 32, 1), pltpu.roll(x, 32, 1))
    reps = w // 128
    return x * jnp.tile(cos, (1, reps)) + rot * jnp.tile(sin_signed, (1, reps))


def _inproj_odd_kernel(x_ref, sh_ref, sc_ref, g_ref, cos_ref, sin_ref, w_ref, q_ref, k_ref, v_ref):
    hb = _norm_mod(x_ref[...], g_ref[...], sh_ref[...], sc_ref[...]).astype(BF16)
    cos, sin = cos_ref[...], sin_ref[...]
    q = _rope(_dot(hb, w_ref[:, 0:1024]), cos, sin)
    q_ref[...] = (q * (DHC ** -0.5)).astype(BF16)
    k_ref[...] = _rope(_dot(hb, w_ref[:, 1024:1280]), cos, sin).astype(BF16)
    v_ref[...] = _dot(hb, w_ref[:, 1280:1536]).astype(BF16)


def _inproj_odd_call(x, shift, scale, g, cos, sin, w):
    n, d = x.shape
    grp = TM // CHUNK
    row = lambda i: (i, 0)
    const = lambda i: (0, 0)
    mod = lambda i: (i, 0, 0)
    widths = (1024, 256, 256)
    return pl.pallas_call(
        _inproj_odd_kernel,
        grid=(n // TM,),
        in_specs=[pl.BlockSpec((TM, d), row),
                  pl.BlockSpec((grp, 1, d), mod), pl.BlockSpec((grp, 1, d), mod),
                  pl.BlockSpec((1, d), const),
                  pl.BlockSpec((TM, 128), row), pl.BlockSpec((TM, 128), row),
                  pl.BlockSpec(w.shape, const)],
        out_specs=[pl.BlockSpec((TM, wd), row) for wd in widths],
        out_shape=[jax.ShapeDtypeStruct((n, wd), BF16) for wd in widths],
        compiler_params=_cparams(("parallel",)),
        name="inproj_odd",
    )(x, shift, scale, g, cos, sin, w)


def _gla_tri():
    t = np.arange(CHUNK)[:, None]
    s = np.arange(CHUNK)[None, :]
    cum = s <= t
    start = s < (t // SUB) * SUB
    end = s < (t // SUB + 1) * SUB
    return jnp.asarray(np.concatenate([cum, start, end], axis=0).astype(np.float32), dtype=BF16)


def _gla_kernel(q_ref, k_ref, v_ref, g_ref, r_ref, s0_ref, gn_ref, tri_ref, o_ref, sout_ref, s_ref, *, nb):
    c_ = CHUNK
    nsub = c_ // SUB

    @pl.when(pl.program_id(1) == 0)
    def _():
        s_ref[...] = s0_ref[0]

    tri = tri_ref[...]
    lane = lax.broadcasted_iota(jnp.int32, (1, 128), 1)
    hmask = [jnp.where(lane < DKA, 1.0, 0.0), jnp.where(lane >= DKA, 1.0, 0.0)]
    ti = lax.broadcasted_iota(jnp.int32, (c_, c_), 0)
    si = lax.broadcasted_iota(jnp.int32, (c_, c_), 1)
    rb, cb = ti >> 4, si >> 4
    m_diag = (rb == cb) & (si <= ti)
    m_off = [(cb == j) & (rb > j) for j in range(nsub - 1)]
    hk = HA * DKA
    eye = lax.broadcasted_iota(jnp.int32, (hk, hk), 0) == lax.broadcasted_iota(jnp.int32, (hk, hk), 1)
    gn = gn_ref[...]

    for c in range(nb):
        rows = slice(c * c_, (c + 1) * c_)
        g_hi, g_lo = _split(g_ref[rows, :])
        cs = _dot(tri, g_hi) + _dot(tri, g_lo)
        b, rs, re = cs[0:c_], cs[c_:2 * c_], cs[2 * c_:3 * c_]
        q = q_ref[rows, :].astype(F32)
        k = k_ref[rows, :].astype(F32)
        bl = b[c_ - 1:c_, :]
        qd = q * jnp.exp(b - rs)
        kd = k * jnp.exp(rs - b)
        ke = k * jnp.exp(re - b)
        qi = q * jnp.exp(b)
        kl = k * jnp.exp(bl - b)
        ql = [q * jnp.exp(jnp.minimum(b - b[SUB * (j + 1) - 1:SUB * (j + 1), :], 0.0)) for j in range(nsub - 1)]
        dcol = jnp.sum(jnp.where(eye, jnp.broadcast_to(jnp.exp(bl), (hk, hk)), 0.0), axis=1, keepdims=True)
        s_old = s_ref[...]
        s_old_b = s_old.astype(BF16)
        upd = []
        for p in range(HA // 2):
            ls = slice(128 * p, 128 * (p + 1))
            kd_p = kd[:, ls].astype(BF16)
            ke_p = ke[:, ls].astype(BF16)
            klt = kl[:, ls].T
            sp_b = s_old_b[ls, :]
            for hh in range(2):
                h = 2 * p + hh
                msk = hmask[hh]
                a1 = _dot_nt((qd[:, ls] * msk).astype(BF16), kd_p)
                lhs2 = jnp.concatenate([ql[j][:, ls] * msk for j in range(nsub - 1)], axis=0).astype(BF16)
                a2 = _dot_nt(lhs2, ke_p)
                att = jnp.zeros((c_, c_), F32)
                for j in reversed(range(nsub - 1)):
                    att = jnp.where(m_off[j], a2[j * c_:(j + 1) * c_], att)
                att = jnp.where(m_diag, a1, att)
                vs = slice(DVA * h, DVA * (h + 1))
                v_h = v_ref[rows, vs]
                o = _dot(att.astype(BF16), v_h) + _dot((qi[:, ls] * msk).astype(BF16), sp_b)
                ms = jnp.mean(o * o, axis=-1, keepdims=True)
                rr = r_ref[rows, vs].astype(F32)
                o_ref[rows, vs] = (o * lax.rsqrt(ms + EPS) * gn * (rr * _sigmoid(rr))).astype(BF16)
                upd.append(_dot(klt[DKA * hh:DKA * (hh + 1)].astype(BF16), v_h))
        s_ref[...] = dcol * s_old + jnp.concatenate(upd, axis=0)
    sout_ref[0] = s_ref[...]


def _gla_call(q, k, v, g, r, s0, gn, o_prev, *, n_seq, seq_rows, row0, nb):
    tq = nb * CHUNK
    steps = seq_rows // tq
    blk0 = row0 // tq
    row = lambda b, j: (blk0 + b * steps + j, 0)
    const = lambda b, j: (0, 0)
    tri = _gla_tri()
    in_specs = [pl.BlockSpec((tq, 256), row), pl.BlockSpec((tq, 256), row), pl.BlockSpec((tq, 512), row),
                pl.BlockSpec((tq, 256), row), pl.BlockSpec((tq, 512), row),
                pl.BlockSpec((1, 256, 128), lambda b, j: (b, 0, 0)),
                pl.BlockSpec((1, 128), const), pl.BlockSpec(tri.shape, const)]
    args = [q, k, v, g, r, s0, gn, tri]
    aliases = {}
    if o_prev is not None:
        in_specs.append(pl.BlockSpec(memory_space=pl.ANY))
        args.append(o_prev)
        aliases = {len(args) - 1: 0}
    kern = functools.partial(_gla_kernel, nb=nb)
    if o_prev is not None:
        kern = _drop_arg(kern, 8)
    return pl.pallas_call(
        kern,
        grid=(n_seq, steps),
        in_specs=in_specs,
        out_specs=[pl.BlockSpec((tq, 512), row), pl.BlockSpec((1, 256, 128), lambda b, j: (b, 0, 0))],
        out_shape=[jax.ShapeDtypeStruct((q.shape[0], 512), BF16), jax.ShapeDtypeStruct((n_seq, 256, 128), F32)],
        scratch_shapes=[pltpu.VMEM((256, 128), F32)],
        input_output_aliases=aliases,
        compiler_params=_cparams(("arbitrary", "arbitrary")),
        name="gla",
    )(*args)


def _drop_arg(fn, idx):
    def wrapped(*refs):
        return fn(*refs[:idx], *refs[idx + 1:])
    return wrapped


def _window(prev_ref, cur_ref, lo, hi, pb, ls):
    if lo < pb:
        return jnp.concatenate([prev_ref[lo:pb, ls], cur_ref[0:hi - pb, ls]], axis=0)
    return cur_ref[lo - pb:hi - pb, ls]


def _band_kernel(q_ref, kp_ref, kc_ref, vp_ref, vc_ref, bias_ref, o_ref, *, g, n_sub, pb, blocks_per_seq, mask_first):
    qs = CHUNK * g
    kw_rows = pb + qs
    row = lax.broadcasted_iota(jnp.int32, (qs, kw_rows), 0)
    col = lax.broadcasted_iota(jnp.int32, (qs, kw_rows), 1)
    dd = (col >> 6) - (row >> 6)
    band = (dd >= 0) & (dd <= pb // CHUNK)
    lane = lax.broadcasted_iota(jnp.int32, (1, 128), 1)
    low = lane < DHB
    hmask = [jnp.where(low, 1.0, 0.0), jnp.where(low, 0.0, 1.0)]
    first = (pl.program_id(0) % blocks_per_seq) == 0
    for s in range(n_sub):
        valid = band
        if mask_first:
            valid = band & (col >= jnp.where(first, pb - qs * s, 0))
        for p in range(HB // 2):
            ls = slice(128 * p, 128 * (p + 1))
            qp = q_ref[qs * s:qs * (s + 1), ls].astype(F32)
            kw = _window(kp_ref, kc_ref, qs * s, qs * s + kw_rows, pb, ls)
            vw = _window(vp_ref, vc_ref, qs * s, qs * s + kw_rows, pb, ls)
            outs = []
            for hh in range(2):
                sc = _dot_nt((qp * hmask[hh]).astype(BF16), kw)
                sc = jnp.where(valid, sc + bias_ref[2 * p + hh], NEG)
                m = jnp.max(sc, axis=-1, keepdims=True)
                pe = jnp.exp(sc - m)
                l = jnp.sum(pe, axis=-1, keepdims=True)
                outs.append(_dot(pe.astype(BF16), vw) / l)
            o_ref[qs * s:qs * (s + 1), ls] = jnp.where(low, outs[0], outs[1]).astype(BF16)


def _band_bias(table, g, pb):
    r = np.arange(CHUNK * g)[:, None]
    c = np.arange(pb + CHUNK * g)[None, :]
    idx = np.clip((c - pb) - r, -MAX_REL, MAX_REL) + MAX_REL
    return table[:, idx]


def _attn_call(kernel, q, kp, kc, vp, vc, extra, extra_specs, o_prev, *, width, kv_width, tq, pb,
               n_blocks, blk0, prev_map, name, n_prefetch=0):
    row = lambda i, *_: (blk0 + i, 0)
    in_specs = [pl.BlockSpec((tq, width), row),
                pl.BlockSpec((pb, kv_width), prev_map), pl.BlockSpec((tq, kv_width), row),
                pl.BlockSpec((pb, kv_width), prev_map), pl.BlockSpec((tq, kv_width), row)] + extra_specs
    args = [q, kp, kc, vp, vc] + extra
    aliases = {}
    if o_prev is not None:
        in_specs.append(pl.BlockSpec(memory_space=pl.ANY))
        args.append(o_prev)
        aliases = {len(args) - 1: 0}
        kernel = _drop_arg(kernel, len(args) - 1)
    return pl.pallas_call(
        kernel,
        grid=(n_blocks,),
        in_specs=in_specs,
        out_specs=pl.BlockSpec((tq, width), row),
        out_shape=jax.ShapeDtypeStruct((q.shape[0], width), BF16),
        input_output_aliases=aliases,
        compiler_params=_cparams(("parallel",)),
        name=name,
    )(*args)


def _swa_kernel(q_ref, kp_ref, kc_ref, vp_ref, vc_ref, sink_ref, o_ref, *, g, n_sub, pb, blocks_per_seq, mask_first):
    qs = CHUNK * g
    kw_rows = pb + qs
    row = lax.broadcasted_iota(jnp.int32, (qs, kw_rows), 0)
    col = lax.broadcasted_iota(jnp.int32, (qs, kw_rows), 1)
    dd = (col >> 6) - (row >> 6)
    band = (dd >= 0) & (dd <= pb // CHUNK)
    lane = lax.broadcasted_iota(jnp.int32, (1, 128), 1)
    low = lane < DHC
    hmask = [jnp.where(low, 1.0, 0.0), jnp.where(low, 0.0, 1.0)]
    first = (pl.program_id(0) % blocks_per_seq) == 0
    pairs_per_kv = HC // KVC // 2
    for s in range(n_sub):
        valid = band
        if mask_first:
            valid = band & (col >= jnp.where(first, pb - qs * s, 0))
        for kv in range(KVC):
            kvs = slice(128 * kv, 128 * (kv + 1))
            kw = _window(kp_ref, kc_ref, qs * s, qs * s + kw_rows, pb, kvs)
            vw = _window(vp_ref, vc_ref, qs * s, qs * s + kw_rows, pb, kvs)
            for jj in range(pairs_per_kv):
                j = kv * pairs_per_kv + jj
                ls = slice(128 * j, 128 * (j + 1))
                qp = q_ref[qs * s:qs * (s + 1), ls].astype(F32)
                outs = []
                for hh in range(2):
                    sc = _dot_nt((qp * hmask[hh]).astype(BF16), kw)
                    sc = jnp.where(valid, sc, NEG)
                    sk = sink_ref[0, 2 * j + hh]
                    m = jnp.maximum(jnp.max(sc, axis=-1, keepdims=True), sk)
                    pe = jnp.exp(sc - m)
                    l = jnp.sum(pe, axis=-1, keepdims=True) + jnp.exp(sk - m)
                    outs.append(_dot(pe.astype(BF16), vw) / l)
                o_ref[qs * s:qs * (s + 1), ls] = jnp.where(low, outs[0], outs[1]).astype(BF16)


def _route(logits_t):
    a = [logits_t[4 * j:4 * j + 4] for j in range(EXP_PER_GROUP)]

    def first_argmax(vals, m):
        idx = jnp.full(m.shape, float(len(vals) - 1), F32)
        for j in reversed(range(len(vals) - 1)):
            idx = jnp.where(vals[j] == m, float(j), idx)
        return idx

    m1 = functools.reduce(jnp.maximum, a)
    i1 = first_argmax(a, m1)
    bsec = [jnp.where(i1 == float(j), -jnp.inf, a[j]) for j in range(EXP_PER_GROUP)]
    m2 = functools.reduce(jnp.maximum, bsec)
    i2 = first_argmax(bsec, m2)
    rows = lambda x: [x[gi:gi + 1] for gi in range(N_GROUPS)]
    gm = functools.reduce(jnp.maximum, rows(m1))
    gscore = jnp.exp(m1 - gm) + jnp.exp(m2 - gm)
    gs = rows(gscore)
    gsel = first_argmax(gs, functools.reduce(jnp.maximum, gs))

    def pick(x):
        xr = rows(x)
        out = xr[N_GROUPS - 1]
        for gi in reversed(range(N_GROUPS - 1)):
            out = jnp.where(gsel == float(gi), xr[gi], out)
        return out

    p1 = jnp.exp(pick(m1) - gm)
    p2 = jnp.exp(pick(m2) - gm)
    w1 = p1 / (p1 + p2)
    w2 = p2 / (p1 + p2)
    e1 = gsel * float(EXP_PER_GROUP) + pick(i1)
    e2 = gsel * float(EXP_PER_GROUP) + pick(i2)
    t = logits_t.shape[1]
    erow = lax.broadcasted_iota(jnp.int32, (N_EXPERTS, t), 0).astype(F32)
    return jnp.where(erow == e1, w1, 0.0) + jnp.where(erow == e2, w2, 0.0)


def _outproj_kernel(*refs, n_o):
    x_ref = refs[0]
    o_refs = refs[1:1 + n_o]
    w_refs = refs[1 + n_o:1 + 2 * n_o]
    gate_ref, nf_ref, sh_ref, sc_ref, wr_ref, br_ref, xn_ref, h_ref, comb_ref = refs[1 + 2 * n_o:]
    t, d = x_ref.shape
    y = _dot(o_refs[0][...], w_refs[0][...])
    for i in range(1, n_o):
        y = y + _dot(o_refs[i][...], w_refs[i][...])
    xn = (x_ref[...].reshape(t // CHUNK, CHUNK, d) + gate_ref[...] * y.reshape(t // CHUNK, CHUNK, d)).reshape(t, d)
    xn_ref[...] = xn
    h = _norm_mod(xn, nf_ref[...], sh_ref[...], sc_ref[...])
    h_ref[...] = h.astype(BF16)
    logits = _dot3(h, wr_ref[...]) + br_ref[...]
    comb_ref[...] = _route(logits.T[0:N_EXPERTS])


def _outproj_call(x, os_, ws, gate, nf, shift, scale, wr, br):
    n, d = x.shape
    grp = TM // CHUNK
    row = lambda i: (i, 0)
    const = lambda i: (0, 0)
    mod = lambda i: (i, 0, 0)
    n_o = len(os_)
    in_specs = ([pl.BlockSpec((TM, d), row)]
                + [pl.BlockSpec((TM, o.shape[1]), row) for o in os_]
                + [pl.BlockSpec(w.shape, const) for w in ws]
                + [pl.BlockSpec((grp, 1, d), mod), pl.BlockSpec((1, d), const),
                   pl.BlockSpec((grp, 1, d), mod), pl.BlockSpec((grp, 1, d), mod),
                   pl.BlockSpec(wr.shape, const), pl.BlockSpec(br.shape, const)])
    return pl.pallas_call(
        functools.partial(_outproj_kernel, n_o=n_o),
        grid=(n // TM,),
        in_specs=in_specs,
        out_specs=[pl.BlockSpec((TM, d), row), pl.BlockSpec((TM, d), row), pl.BlockSpec((N_EXPERTS, TM), lambda i: (0, i))],
        out_shape=[jax.ShapeDtypeStruct((n, d), F32), jax.ShapeDtypeStruct((n, d), BF16),
                   jax.ShapeDtypeStruct((N_EXPERTS, n), F32)],
        compiler_params=_cparams(("parallel",)),
        name="outproj_router",
    )(x, *os_, *ws, gate, nf, shift, scale, wr, br)


def _moe_dense_kernel(h_ref, comb_ref, wgu_ref, wdn_ref, x_ref, gate_ref, o_ref, acc_ref):
    e = pl.program_id(1)
    t, d = x_ref.shape

    @pl.when(e == 0)
    def _():
        acc_ref[...] = jnp.zeros_like(acc_ref)

    ab = _dot(h_ref[...], wgu_ref[0])
    a, b = ab[:, :D_FF], ab[:, D_FF:]
    y = _dot((a * _sigmoid(a) * b).astype(BF16), wdn_ref[0])
    cw = jnp.broadcast_to(comb_ref[pl.ds(e, 1), :], (8, t)).T[:, 0:1]
    acc_ref[...] += cw * y

    @pl.when(e == N_EXPERTS - 1)
    def _():
        acc = acc_ref[...].reshape(t // CHUNK, CHUNK, d)
        o_ref[...] = (x_ref[...].reshape(t // CHUNK, CHUNK, d) + gate_ref[...] * acc).reshape(t, d)


def _moe_dense_call(h, comb, wgu, wdn, x, gate):
    n, d = x.shape
    grp = TM // CHUNK
    return pl.pallas_call(
        _moe_dense_kernel,
        grid=(n // TM, N_EXPERTS),
        in_specs=[pl.BlockSpec((TM, d), lambda i, e: (i, 0)),
                  pl.BlockSpec((N_EXPERTS, TM), lambda i, e: (0, i)),
                  pl.BlockSpec((1, d, 2 * D_FF), lambda i, e: (e, 0, 0)),
                  pl.BlockSpec((1, D_FF, d), lambda i, e: (e, 0, 0)),
                  pl.BlockSpec((TM, d), lambda i, e: (i, 0)),
                  pl.BlockSpec((grp, 1, d), lambda i, e: (i, 0, 0))],
        out_specs=pl.BlockSpec((TM, d), lambda i, e: (i, 0)),
        out_shape=jax.ShapeDtypeStruct((n, d), F32),
        scratch_shapes=[pltpu.VMEM((TM, d), F32)],
        compiler_params=_cparams(("parallel", "arbitrary")),
        name="moe_dense",
    )(h, comb, wgu, wdn, x, gate)


def _final_kernel(x_ref, g_ref, o_ref):
    x = x_ref[...]
    ms = jnp.mean(x * x, axis=-1, keepdims=True)
    o_ref[...] = x * lax.rsqrt(ms + EPS) * g_ref[...]


def _final_call(x, g):
    n, d = x.shape
    return pl.pallas_call(
        _final_kernel,
        grid=(n // TM,),
        in_specs=[pl.BlockSpec((TM, d), lambda i: (i, 0)), pl.BlockSpec((1, d), lambda i: (0, 0))],
        out_specs=pl.BlockSpec((TM, d), lambda i: (i, 0)),
        out_shape=jax.ShapeDtypeStruct((n, d), F32),
        compiler_params=_cparams(("parallel",)),
        name="final_norm",
    )(x, g)


def kernel(x_prompt, x_sample, c_prompt, c_sample, state_gla, cache_band_k, cache_band_v, cache_swa_k, cache_swa_v,
           w_ada, b_ada, norm_mix, norm_ffn, norm_final, w_in_even, w_gate_a, b_gate_a, gla_norm, rel_bias_b,
           w_out_even, w_in_odd, sinks_c, w_out_odd, w_router, b_router, w_gate_up, w_down):
    bp, lp, d = x_prompt.shape
    bs, ls_, _ = x_sample.shape
    n_p, n_s = bp * lp, bs * ls_
    n = n_p + n_s
    assert ls_ == CHUNK and n_s == TM and lp % TM == 0 and PAST_LEN % CHUNK == 0

    x = jnp.concatenate([x_prompt.reshape(n_p, d), x_sample.reshape(n_s, d)], axis=0)

    c16 = jnp.zeros((SEQ_ROWS, d), F32).at[:bp].set(c_prompt).at[bp:bp + bs].set(c_sample)
    mods = _ada_call(c16, w_ada, b_ada)
    seq_of_group = np.concatenate([np.repeat(np.arange(bp), lp // CHUNK), bp + np.arange(bs)])
    n_grp = n // CHUNK

    def mod_parts(l):
        m = mods[l][seq_of_group].reshape(n_grp, 6, 1, d)
        return [m[:, i] for i in range(6)]

    perm = np.array([4 * (c % 4) + c // 4 for c in range(N_EXPERTS)])
    wr = jnp.zeros((d, 128), F32).at[:, :N_EXPERTS].set(w_router[:, perm])
    br = jnp.zeros((1, 128), F32).at[0, :N_EXPERTS].set(b_router[perm])

    wgu = w_gate_up.astype(BF16)
    wdn = w_down.astype(BF16)

    gla_p = gla_s = bk_p = bv_p = bk_s = bv_s = sk_p = sv_p = sk_s = sv_s = None
    for l in range(DEPTH):
        sh_m, sc_m, g_m, sh_f, sc_f, g_f = mod_parts(l)
        i = l // 2
        if l % 2 == 0:
            w = w_in_even[i]
            w_main = jnp.concatenate([w[:, :1536], w[:, 1552:]], axis=1).astype(BF16)
            w_la = jnp.zeros((d, 128), F32).at[:, :GATE_RANK].set(w[:, 1536:1552]).astype(BF16)
            w_gate = jnp.zeros((128, HA * DKA), F32).at[:GATE_RANK].set(w_gate_a[i])
            qa, ka, va, ra, qb, kb, vb, ga = _inproj_even_call(
                x, sh_m, sc_m, norm_mix[l][None], w_main, w_la, w_gate, b_gate_a[i][None])
            gn = gla_norm[i][None]
            oa, s_p = _gla_call(qa, ka, va, ga, ra, jnp.zeros((bp, 256, 128), F32), gn, None,
                                n_seq=bp, seq_rows=lp, row0=0, nb=8)
            oa, s_s = _gla_call(qa, ka, va, ga, ra, state_gla[i].reshape(bs, 256, 128), gn, oa,
                                n_seq=bs, seq_rows=ls_, row0=n_p, nb=1)
            gla_p, gla_s = s_p.reshape(1, bp, HA, DKA, DVA), s_s.reshape(1, bs, HA, DKA, DVA)
            pb = N_PREV_B * CHUNK
            tq, g = 512, 2
            bps = lp // tq
            bias = _band_bias(rel_bias_b[i], g, pb)
            ob = _attn_call(
                functools.partial(_band_kernel, g=g, n_sub=tq // (CHUNK * g), pb=pb, blocks_per_seq=bps, mask_first=True),
                qb, kb, kb, vb, vb, [bias], [pl.BlockSpec(bias.shape, lambda i_: (0, 0, 0))], None,
                width=512, kv_width=512, tq=tq, pb=pb, n_blocks=n_p // tq, blk0=0,
                prev_map=lambda i_: (jnp.maximum(i_ * (tq // pb) - 1, 0), 0), name="band_prompt")
            ck = cache_band_k[i].reshape(bs * pb, HB * DHB).astype(BF16)
            cv = cache_band_v[i].reshape(bs * pb, HB * DHB).astype(BF16)
            bias1 = _band_bias(rel_bias_b[i], 1, pb)
            ob = _attn_call(
                functools.partial(_band_kernel, g=1, n_sub=1, pb=pb, blocks_per_seq=1, mask_first=False),
                qb, ck, kb, cv, vb, [bias1], [pl.BlockSpec(bias1.shape, lambda i_: (0, 0, 0))], ob,
                width=512, kv_width=512, tq=CHUNK, pb=pb, n_blocks=bs, blk0=n_p // CHUNK,
                prev_map=lambda i_: (i_, 0), name="band_sample")
            kbf, vbf = kb.astype(F32), vb.astype(F32)
            bk_p = kbf[:n_p].reshape(bp, lp, HB, DHB)[:, lp - pb:][None]
            bv_p = vbf[:n_p].reshape(bp, lp, HB, DHB)[:, lp - pb:][None]
            bk_s = jnp.concatenate([cache_band_k[i][:, ls_:], kbf[n_p:].reshape(bs, ls_, HB, DHB)], axis=1)[None]
            bv_s = jnp.concatenate([cache_band_v[i][:, ls_:], vbf[n_p:].reshape(bs, ls_, HB, DHB)], axis=1)[None]
            wo = w_out_even[i].astype(BF16)
            os_, ws = [oa, ob], [wo[:HA * DVA], wo[HA * DVA:]]
        else:
            w = w_in_odd[i]
            wk, wv = w[:, 1024:1152], w[:, 1152:1280]
            dup = lambda a: jnp.concatenate([a[:, :64], a[:, :64], a[:, 64:], a[:, 64:]], axis=1)
            w_all = jnp.concatenate([w[:, :1024], dup(wk), dup(wv)], axis=1).astype(BF16)
            pos = np.concatenate([np.tile(np.arange(lp), bp), np.tile(PAST_LEN + np.arange(ls_), bs)])
            half = DHC // 2
            inv = ROPE_THETA ** (-jnp.arange(half, dtype=F32) / half)
            ang = jnp.asarray(pos, F32)[:, None] * inv[None, :]
            cs_, sn_ = jnp.cos(ang), jnp.sin(ang)
            cos = jnp.concatenate([cs_, cs_, cs_, cs_], axis=1)
            sin = jnp.concatenate([-sn_, sn_, -sn_, sn_], axis=1)
            q, k, v = _inproj_odd_call(x, sh_m, sc_m, norm_mix[l][None], cos, sin, w_all)
            pb = WINDOW
            tq, g = 512, 2
            bps = lp // tq
            sink = sinks_c[i][None]
            sink_spec = [pl.BlockSpec(memory_space=pltpu.SMEM)]
            o = _attn_call(
                functools.partial(_swa_kernel, g=g, n_sub=tq // (CHUNK * g), pb=pb, blocks_per_seq=bps, mask_first=True),
                q, k, k, v, v, [sink], sink_spec, None,
                width=1024, kv_width=256, tq=tq, pb=pb, n_blocks=n_p // tq, blk0=0,
                prev_map=lambda i_: (jnp.maximum(i_ * (tq // pb) - 1, 0), 0), name="swa_prompt")
            dupc = lambda c: jnp.concatenate([c[:, :, 0], c[:, :, 0], c[:, :, 1], c[:, :, 1]], axis=-1).reshape(bs * pb, 256).astype(BF16)
            ck, cv = dupc(cache_swa_k[i]), dupc(cache_swa_v[i])
            o = _attn_call(
                functools.partial(_swa_kernel, g=1, n_sub=1, pb=pb, blocks_per_seq=1, mask_first=False),
                q, ck, k, cv, v, [sink], sink_spec, o,
                width=1024, kv_width=256, tq=CHUNK, pb=pb, n_blocks=bs, blk0=n_p // CHUNK,
                prev_map=lambda i_: (i_, 0), name="swa_sample")
            undup = lambda a: jnp.concatenate([a[:, 0:64], a[:, 128:192]], axis=1).astype(F32)
            kf, vf = undup(k), undup(v)
            sk_p = kf[:n_p].reshape(bp, lp, KVC, DHC)[:, lp - pb:][None]
            sv_p = vf[:n_p].reshape(bp, lp, KVC, DHC)[:, lp - pb:][None]
            sk_s = jnp.concatenate([cache_swa_k[i][:, ls_:], kf[n_p:].reshape(bs, ls_, KVC, DHC)], axis=1)[None]
            sv_s = jnp.concatenate([cache_swa_v[i][:, ls_:], vf[n_p:].reshape(bs, ls_, KVC, DHC)], axis=1)[None]
            os_, ws = [o], [w_out_odd[i].astype(BF16)]
        xn, h2, comb = _outproj_call(x, os_, ws, g_m, norm_ffn[l][None], sh_f, sc_f, wr, br)
        x = _moe_dense_call(h2, comb, wgu[l], wdn[l], xn, g_f)

    y = _final_call(x, norm_final[None])
    y_prompt = y[:n_p].reshape(bp, lp, d)
    y_sample = y[n_p:].reshape(bs, ls_, d)
    return (y_prompt, y_sample, gla_p, gla_s, bk_p, bv_p, bk_s, bv_s, sk_p, sv_p, sk_s, sv_s)
```

```python
import functools

import numpy as np
import jax
import jax.numpy as jnp
from jax import lax
from jax.experimental import pallas as pl
from jax.experimental.pallas import tpu as pltpu
from jax.experimental.pallas import tpu_sc as plsc

F32 = jnp.float32
BF16 = jnp.bfloat16

D_MODEL = 1024
DEPTH = 2
CHUNK = 64
PAST_LEN = 4096
HA, DKA, DVA = 4, 64, 128
GATE_RANK = 16
GATE_TAU = 16.0
HB, DHB = 8, 64
N_PREV_B = 8
MAX_REL = 128
HC, KVC, DHC = 16, 2, 64
WINDOW = 128
ROPE_THETA = 10000.0
N_EXPERTS = 16
N_GROUPS = 4
EXP_PER_GROUP = 4
D_FF = 512
EPS = 1e-6

N_PAIRS = 6
N_BUCKETS = N_GROUPS * N_PAIRS
BUCKET_ROWS = 32
Y_SLABS = 4
DISP_SLABS = Y_SLABS + 1
TMO = 256
SC_WINDOW = 128
SC_WORKERS = 32

TM = 512
SEQ_ROWS = 16
SUB = 16
NEG = -1e30


def _cparams(sem, vmem_mb=48):
    return pltpu.CompilerParams(dimension_semantics=sem, vmem_limit_bytes=vmem_mb * 1024 * 1024)


def _dot(a, b):
    return jnp.dot(a, b, preferred_element_type=F32)


def _dot_nt(a, b):
    return lax.dot_general(a, b, (((1,), (1,)), ((), ())), preferred_element_type=F32)


def _split(a):
    hi = a.astype(BF16)
    lo = (a - hi.astype(F32)).astype(BF16)
    return hi, lo


def _dot3(a, b):
    ah, al = _split(a)
    bh, bl = _split(b)
    return _dot(ah, bh) + _dot(ah, bl) + _dot(al, bh)


def _sigmoid(x):
    return 1.0 / (1.0 + jnp.exp(-x))


def _norm_mod(x, g, shift, scale):
    t, d = x.shape
    ms = jnp.mean(x * x, axis=-1, keepdims=True)
    y = x * lax.rsqrt(ms + EPS) * g
    y = y.reshape(t // CHUNK, CHUNK, d) * (1.0 + scale) + shift
    return y.reshape(t, d)


def _ada_kernel(c_ref, w_ref, b_ref, o_ref):
    c = c_ref[...]
    o_ref[0] = _dot3(c * _sigmoid(c), w_ref[0]) + b_ref[0]


def _ada_call(c16, w_ada, b_ada):
    d = D_MODEL
    tn = 1024
    return pl.pallas_call(
        _ada_kernel,
        grid=(DEPTH, 6 * d // tn),
        in_specs=[pl.BlockSpec((SEQ_ROWS, d), lambda l, j: (0, 0)),
                  pl.BlockSpec((1, d, tn), lambda l, j: (l, 0, j)),
                  pl.BlockSpec((1, 1, tn), lambda l, j: (l, 0, j))],
        out_specs=pl.BlockSpec((1, SEQ_ROWS, tn), lambda l, j: (l, 0, j)),
        out_shape=jax.ShapeDtypeStruct((DEPTH, SEQ_ROWS, 6 * d), F32),
        compiler_params=_cparams(("arbitrary", "arbitrary")),
        name="ada",
    )(c16, w_ada, b_ada.reshape(DEPTH, 1, 6 * d))


def _inproj_even_kernel(x_ref, sh_ref, sc_ref, g_ref, w_ref, wla_ref, wg_ref, bg_ref,
                        qa_ref, ka_ref, va_ref, ra_ref, qb_ref, kb_ref, vb_ref, ga_ref):
    hb = _norm_mod(x_ref[...], g_ref[...], sh_ref[...], sc_ref[...]).astype(BF16)
    qa_ref[...] = (_dot(hb, w_ref[:, 0:256]) * (DKA ** -0.5)).astype(BF16)
    ka_ref[...] = _dot(hb, w_ref[:, 256:512]).astype(BF16)
    va_ref[...] = _dot(hb, w_ref[:, 512:1024]).astype(BF16)
    ra_ref[...] = _dot(hb, w_ref[:, 1024:1536]).astype(BF16)
    qb_ref[...] = (_dot(hb, w_ref[:, 1536:2048]) * (DHB ** -0.5)).astype(BF16)
    kb_ref[...] = _dot(hb, w_ref[:, 2048:2560]).astype(BF16)
    vb_ref[...] = _dot(hb, w_ref[:, 2560:3072]).astype(BF16)
    la = _dot(hb, wla_ref[...])
    gl = _dot3(la, wg_ref[...]) + bg_ref[...]
    ga_ref[...] = -(jnp.maximum(-gl, 0.0) + jnp.log(1.0 + jnp.exp(-jnp.abs(gl)))) * (1.0 / GATE_TAU)


def _inproj_even_call(x, shift, scale, g, w_main, w_la, w_gate, b_gate):
    n, d = x.shape
    grp = TM // CHUNK
    row = lambda i: (i, 0)
    const = lambda i: (0, 0)
    mod = lambda i: (i, 0, 0)
    widths = (256, 256, 512, 512, 512, 512, 512)
    out_shape = [jax.ShapeDtypeStruct((n, w), BF16) for w in widths] + [jax.ShapeDtypeStruct((n, 256), F32)]
    out_specs = [pl.BlockSpec((TM, w), row) for w in widths] + [pl.BlockSpec((TM, 256), row)]
    return pl.pallas_call(
        _inproj_even_kernel,
        grid=(n // TM,),
        in_specs=[pl.BlockSpec((TM, d), row),
                  pl.BlockSpec((grp, 1, d), mod), pl.BlockSpec((grp, 1, d), mod),
                  pl.BlockSpec((1, d), const),
                  pl.BlockSpec(w_main.shape, const), pl.BlockSpec(w_la.shape, const),
                  pl.BlockSpec(w_gate.shape, const), pl.BlockSpec(b_gate.shape, const)],
        out_specs=out_specs, out_shape=out_shape,
        compiler_params=_cparams(("parallel",)),
        name="inproj_even",
    )(x, shift, scale, g, w_main, w_la, w_gate, b_gate)


def _rope(x, cos, sin_signed):
    t, w = x.shape
    lane = lax.broadcasted_iota(jnp.int32, (1, w), 1)
    first_half = (lane & 63) < 32
    rot = jnp.where(first_half, pltpu.roll(x, w - 32, 1), pltpu.roll(x, 32, 1))
    reps = w // 128
    return x * jnp.tile(cos, (1, reps)) + rot * jnp.tile(sin_signed, (1, reps))


def _unpack_pairs(slabs, dtype):
    lo = [pltpu.bitcast(s << 16, F32) for s in slabs]
    hi = [pltpu.bitcast(s & jnp.int32(-65536), F32) for s in slabs]
    return jnp.concatenate(lo + hi, axis=1).astype(dtype)


def _pack_pairs(x):
    bits = pltpu.bitcast(x.astype(BF16).astype(F32), jnp.int32)
    half = x.shape[1] // 2
    packed = ((bits[:, :half] >> 16) & jnp.int32(0xFFFF)) | (bits[:, half:] & jnp.int32(-65536))
    return [packed[:, 128 * s:128 * (s + 1)] for s in range(half // 128)]


def _add_moe(xn_ref, z_ref, gate_ref):
    t, d = xn_ref.shape
    y = _unpack_pairs([z_ref[s] for s in range(z_ref.shape[0])], F32)
    return (xn_ref[...].reshape(t // CHUNK, CHUNK, d) + gate_ref[...] * y.reshape(t // CHUNK, CHUNK, d)).reshape(t, d)


def _inproj_odd_kernel(xn_ref, z_ref, gate_ref, sh_ref, sc_ref, g_ref, cos_ref, sin_ref, w_ref,
                       x_ref, q_ref, k_ref, v_ref):
    x = _add_moe(xn_ref, z_ref, gate_ref)
    x_ref[...] = x
    hb = _norm_mod(x, g_ref[...], sh_ref[...], sc_ref[...]).astype(BF16)
    cos, sin = cos_ref[...], sin_ref[...]
    q = _rope(_dot(hb, w_ref[:, 0:1024]), cos, sin)
    q_ref[...] = (q * (DHC ** -0.5)).astype(BF16)
    k_ref[...] = _rope(_dot(hb, w_ref[:, 1024:1280]), cos, sin).astype(BF16)
    v_ref[...] = _dot(hb, w_ref[:, 1280:1536]).astype(BF16)


def _inproj_odd_call(xn, z, gate, shift, scale, g, cos, sin, w):
    n, d = xn.shape
    grp = TM // CHUNK
    row = lambda i: (i, 0)
    const = lambda i: (0, 0)
    mod = lambda i: (i, 0, 0)
    widths = (1024, 256, 256)
    return pl.pallas_call(
        _inproj_odd_kernel,
        grid=(n // TM,),
        in_specs=[pl.BlockSpec((TM, d), row), pl.BlockSpec((z.shape[0], TM, 128), lambda i: (0, i, 0)),
                  pl.BlockSpec((grp, 1, d), mod),
                  pl.BlockSpec((grp, 1, d), mod), pl.BlockSpec((grp, 1, d), mod),
                  pl.BlockSpec((1, d), const),
                  pl.BlockSpec((TM, 128), row), pl.BlockSpec((TM, 128), row),
                  pl.BlockSpec(w.shape, const)],
        out_specs=[pl.BlockSpec((TM, d), row)] + [pl.BlockSpec((TM, wd), row) for wd in widths],
        out_shape=[jax.ShapeDtypeStruct((n, d), F32)] + [jax.ShapeDtypeStruct((n, wd), BF16) for wd in widths],
        compiler_params=_cparams(("parallel",)),
        name="inproj_odd",
    )(xn, z, gate, shift, scale, g, cos, sin, w)


def _gla_tri():
    t = np.arange(CHUNK)[:, None]
    s = np.arange(CHUNK)[None, :]
    cum = s <= t
    start = s < (t // SUB) * SUB
    end = s < (t // SUB + 1) * SUB
    return jnp.asarray(np.concatenate([cum, start, end], axis=0).astype(np.float32), dtype=BF16)


def _gla_kernel(q_ref, k_ref, v_ref, g_ref, r_ref, s0_ref, gn_ref, tri_ref, o_ref, sout_ref, s_ref, *, nb):
    c_ = CHUNK
    nsub = c_ // SUB

    @pl.when(pl.program_id(1) == 0)
    def _():
        s_ref[...] = s0_ref[0]

    tri = tri_ref[...]
    lane = lax.broadcasted_iota(jnp.int32, (1, 128), 1)
    hmask = [jnp.where(lane < DKA, 1.0, 0.0), jnp.where(lane >= DKA, 1.0, 0.0)]
    ti = lax.broadcasted_iota(jnp.int32, (c_, c_), 0)
    si = lax.broadcasted_iota(jnp.int32, (c_, c_), 1)
    rb, cb = ti >> 4, si >> 4
    m_diag = (rb == cb) & (si <= ti)
    m_off = [(cb == j) & (rb > j) for j in range(nsub - 1)]
    hk = HA * DKA
    eye = lax.broadcasted_iota(jnp.int32, (hk, hk), 0) == lax.broadcasted_iota(jnp.int32, (hk, hk), 1)
    gn = gn_ref[...]

    for c in range(nb):
        rows = slice(c * c_, (c + 1) * c_)
        g_hi, g_lo = _split(g_ref[rows, :])
        cs = _dot(tri, g_hi) + _dot(tri, g_lo)
        b, rs, re = cs[0:c_], cs[c_:2 * c_], cs[2 * c_:3 * c_]
        q = q_ref[rows, :].astype(F32)
        k = k_ref[rows, :].astype(F32)
        bl = b[c_ - 1:c_, :]
        qd = q * jnp.exp(b - rs)
        kd = k * jnp.exp(rs - b)
        ke = k * jnp.exp(re - b)
        qi = q * jnp.exp(b)
        kl = k * jnp.exp(bl - b)
        ql = [q * jnp.exp(jnp.minimum(b - b[SUB * (j + 1) - 1:SUB * (j + 1), :], 0.0)) for j in range(nsub - 1)]
        dcol = jnp.sum(jnp.where(eye, jnp.broadcast_to(jnp.exp(bl), (hk, hk)), 0.0), axis=1, keepdims=True)
        s_old = s_ref[...]
        s_old_b = s_old.astype(BF16)
        upd = []
        for p in range(HA // 2):
            ls = slice(128 * p, 128 * (p + 1))
            kd_p = kd[:, ls].astype(BF16)
            ke_p = ke[:, ls].astype(BF16)
            klt = kl[:, ls].T
            sp_b = s_old_b[ls, :]
            for hh in range(2):
                h = 2 * p + hh
                msk = hmask[hh]
                a1 = _dot_nt((qd[:, ls] * msk).astype(BF16), kd_p)
                lhs2 = jnp.concatenate([ql[j][:, ls] * msk for j in range(nsub - 1)], axis=0).astype(BF16)
                a2 = _dot_nt(lhs2, ke_p)
                att = jnp.zeros((c_, c_), F32)
                for j in reversed(range(nsub - 1)):
                    att = jnp.where(m_off[j], a2[j * c_:(j + 1) * c_], att)
                att = jnp.where(m_diag, a1, att)
                vs = slice(DVA * h, DVA * (h + 1))
                v_h = v_ref[rows, vs]
                o = _dot(att.astype(BF16), v_h) + _dot((qi[:, ls] * msk).astype(BF16), sp_b)
                ms = jnp.mean(o * o, axis=-1, keepdims=True)
                rr = r_ref[rows, vs].astype(F32)
                o_ref[rows, vs] = (o * lax.rsqrt(ms + EPS) * gn * (rr * _sigmoid(rr))).astype(BF16)
                upd.append(_dot(klt[DKA * hh:DKA * (hh + 1)].astype(BF16), v_h))
        s_ref[...] = dcol * s_old + jnp.concatenate(upd, axis=0)
    sout_ref[0] = s_ref[...]


def _gla_call(q, k, v, g, r, s0, gn, o_prev, *, n_seq, seq_rows, row0, nb):
    tq = nb * CHUNK
    steps = seq_rows // tq
    blk0 = row0 // tq
    row = lambda b, j: (blk0 + b * steps + j, 0)
    const = lambda b, j: (0, 0)
    tri = _gla_tri()
    in_specs = [pl.BlockSpec((tq, 256), row), pl.BlockSpec((tq, 256), row), pl.BlockSpec((tq, 512), row),
                pl.BlockSpec((tq, 256), row), pl.BlockSpec((tq, 512), row),
                pl.BlockSpec((1, 256, 128), lambda b, j: (b, 0, 0)),
                pl.BlockSpec((1, 128), const), pl.BlockSpec(tri.shape, const)]
    args = [q, k, v, g, r, s0, gn, tri]
    aliases = {}
    if o_prev is not None:
        in_specs.append(pl.BlockSpec(memory_space=pl.ANY))
        args.append(o_prev)
        aliases = {len(args) - 1: 0}
    kern = functools.partial(_gla_kernel, nb=nb)
    if o_prev is not None:
        kern = _drop_arg(kern, 8)
    return pl.pallas_call(
        kern,
        grid=(n_seq, steps),
        in_specs=in_specs,
        out_specs=[pl.BlockSpec((tq, 512), row), pl.BlockSpec((1, 256, 128), lambda b, j: (b, 0, 0))],
        out_shape=[jax.ShapeDtypeStruct((q.shape[0], 512), BF16), jax.ShapeDtypeStruct((n_seq, 256, 128), F32)],
        scratch_shapes=[pltpu.VMEM((256, 128), F32)],
        input_output_aliases=aliases,
        compiler_params=_cparams(("arbitrary", "arbitrary")),
        name="gla",
    )(*args)


def _drop_arg(fn, idx):
    def wrapped(*refs):
        return fn(*refs[:idx], *refs[idx + 1:])
    return wrapped


def _window(prev_ref, cur_ref, lo, hi, pb, ls):
    if lo < pb:
        return jnp.concatenate([prev_ref[lo:pb, ls], cur_ref[0:hi - pb, ls]], axis=0)
    return cur_ref[lo - pb:hi - pb, ls]


def _band_kernel(q_ref, kp_ref, kc_ref, vp_ref, vc_ref, bias_ref, o_ref, *, g, n_sub, pb, blocks_per_seq, mask_first):
    qs = CHUNK * g
    kw_rows = pb + qs
    row = lax.broadcasted_iota(jnp.int32, (qs, kw_rows), 0)
    col = lax.broadcasted_iota(jnp.int32, (qs, kw_rows), 1)
    dd = (col >> 6) - (row >> 6)
    band = (dd >= 0) & (dd <= pb // CHUNK)
    lane = lax.broadcasted_iota(jnp.int32, (1, 128), 1)
    low = lane < DHB
    hmask = [jnp.where(low, 1.0, 0.0), jnp.where(low, 0.0, 1.0)]
    first = (pl.program_id(0) % blocks_per_seq) == 0
    for s in range(n_sub):
        valid = band
        if mask_first:
            valid = band & (col >= jnp.where(first, pb - qs * s, 0))
        for p in range(HB // 2):
            ls = slice(128 * p, 128 * (p + 1))
            qp = q_ref[qs * s:qs * (s + 1), ls].astype(F32)
            kw = _window(kp_ref, kc_ref, qs * s, qs * s + kw_rows, pb, ls)
            vw = _window(vp_ref, vc_ref, qs * s, qs * s + kw_rows, pb, ls)
            outs = []
            for hh in range(2):
                sc = _dot_nt((qp * hmask[hh]).astype(BF16), kw)
                sc = jnp.where(valid, sc + bias_ref[2 * p + hh], NEG)
                m = jnp.max(sc, axis=-1, keepdims=True)
                pe = jnp.exp(sc - m)
                l = jnp.sum(pe, axis=-1, keepdims=True)
                outs.append(_dot(pe.astype(BF16), vw) / l)
            o_ref[qs * s:qs * (s + 1), ls] = jnp.where(low, outs[0], outs[1]).astype(BF16)


def _band_bias(table, g, pb):
    rows, kw = CHUNK * g, pb + CHUNK * g
    period = kw + rows
    m = np.arange(period)
    m = np.where(m < kw, m, m - period)
    ext = table[:, np.clip(m - pb, -MAX_REL, MAX_REL) + MAX_REL]
    flat = jnp.tile(ext, (1, rows))[:, :rows * (period - 1)]
    return flat.reshape(table.shape[0], rows, period - 1)[:, :, :kw]


def _attn_call(kernel, q, kp, kc, vp, vc, extra, extra_specs, o_prev, *, width, kv_width, tq, pb,
               n_blocks, blk0, prev_map, name, n_prefetch=0):
    row = lambda i, *_: (blk0 + i, 0)
    in_specs = [pl.BlockSpec((tq, width), row),
                pl.BlockSpec((pb, kv_width), prev_map), pl.BlockSpec((tq, kv_width), row),
                pl.BlockSpec((pb, kv_width), prev_map), pl.BlockSpec((tq, kv_width), row)] + extra_specs
    args = [q, kp, kc, vp, vc] + extra
    aliases = {}
    if o_prev is not None:
        in_specs.append(pl.BlockSpec(memory_space=pl.ANY))
        args.append(o_prev)
        aliases = {len(args) - 1: 0}
        kernel = _drop_arg(kernel, len(args) - 1)
    return pl.pallas_call(
        kernel,
        grid=(n_blocks,),
        in_specs=in_specs,
        out_specs=pl.BlockSpec((tq, width), row),
        out_shape=jax.ShapeDtypeStruct((q.shape[0], width), BF16),
        input_output_aliases=aliases,
        compiler_params=_cparams(("parallel",)),
        name=name,
    )(*args)


def _swa_kernel(q_ref, kp_ref, kc_ref, vp_ref, vc_ref, sink_ref, o_ref, *, g, n_sub, pb, blocks_per_seq, mask_first):
    qs = CHUNK * g
    kw_rows = pb + qs
    row = lax.broadcasted_iota(jnp.int32, (qs, kw_rows), 0)
    col = lax.broadcasted_iota(jnp.int32, (qs, kw_rows), 1)
    dd = (col >> 6) - (row >> 6)
    band = (dd >= 0) & (dd <= pb // CHUNK)
    lane = lax.broadcasted_iota(jnp.int32, (1, 128), 1)
    low = lane < DHC
    hmask = [jnp.where(low, 1.0, 0.0), jnp.where(low, 0.0, 1.0)]
    first = (pl.program_id(0) % blocks_per_seq) == 0
    pairs_per_kv = HC // KVC // 2
    for s in range(n_sub):
        valid = band
        if mask_first:
            valid = band & (col >= jnp.where(first, pb - qs * s, 0))
        for kv in range(KVC):
            kvs = slice(128 * kv, 128 * (kv + 1))
            kw = _window(kp_ref, kc_ref, qs * s, qs * s + kw_rows, pb, kvs)
            vw = _window(vp_ref, vc_ref, qs * s, qs * s + kw_rows, pb, kvs)
            for jj in range(pairs_per_kv):
                j = kv * pairs_per_kv + jj
                ls = slice(128 * j, 128 * (j + 1))
                qp = q_ref[qs * s:qs * (s + 1), ls].astype(F32)
                outs = []
                for hh in range(2):
                    sc = _dot_nt((qp * hmask[hh]).astype(BF16), kw)
                    sc = jnp.where(valid, sc, NEG)
                    sk = sink_ref[0, 2 * j + hh]
                    m = jnp.maximum(jnp.max(sc, axis=-1, keepdims=True), sk)
                    pe = jnp.exp(sc - m)
                    l = jnp.sum(pe, axis=-1, keepdims=True) + jnp.exp(sk - m)
                    outs.append(_dot(pe.astype(BF16), vw) / l)
                o_ref[qs * s:qs * (s + 1), ls] = jnp.where(low, outs[0], outs[1]).astype(BF16)


def _route(logits_t):
    a = [logits_t[4 * j:4 * j + 4] for j in range(EXP_PER_GROUP)]

    def first_argmax(vals, m):
        idx = jnp.full(m.shape, float(len(vals) - 1), F32)
        for j in reversed(range(len(vals) - 1)):
            idx = jnp.where(vals[j] == m, float(j), idx)
        return idx

    m1 = functools.reduce(jnp.maximum, a)
    i1 = first_argmax(a, m1)
    bsec = [jnp.where(i1 == float(j), -jnp.inf, a[j]) for j in range(EXP_PER_GROUP)]
    m2 = functools.reduce(jnp.maximum, bsec)
    i2 = first_argmax(bsec, m2)
    rows = lambda x: [x[gi:gi + 1] for gi in range(N_GROUPS)]
    gm = functools.reduce(jnp.maximum, rows(m1))
    gscore = jnp.exp(m1 - gm) + jnp.exp(m2 - gm)
    gs = rows(gscore)
    gsel = first_argmax(gs, functools.reduce(jnp.maximum, gs))

    def pick(x):
        xr = rows(x)
        out = xr[N_GROUPS - 1]
        for gi in reversed(range(N_GROUPS - 1)):
            out = jnp.where(gsel == float(gi), xr[gi], out)
        return out

    p1 = jnp.exp(pick(m1) - gm)
    p2 = jnp.exp(pick(m2) - gm)
    w1 = p1 / (p1 + p2)
    w2 = p2 / (p1 + p2)
    s1, s2 = pick(i1), pick(i2)
    lo, hi = jnp.minimum(s1, s2), jnp.maximum(s1, s2)
    pair = jnp.where(lo == 0.0, hi - 1.0, jnp.where(lo == 1.0, hi + 1.0, 5.0))
    bucket = gsel * float(N_PAIRS) + pair
    first_is_lo = s1 < s2
    return bucket, jnp.where(first_is_lo, w1, w2), jnp.where(first_is_lo, w2, w1)


def _outproj_kernel(*refs, n_o):
    x_ref = refs[0]
    o_refs = refs[1:1 + n_o]
    w_refs = refs[1 + n_o:1 + 2 * n_o]
    (gate_ref, nf_ref, sh_ref, sc_ref, wr_ref, br_ref, tri_ref,
     xn_ref, disp_ref, meta_ref, cnt_ref, run_ref) = refs[1 + 2 * n_o:]
    t, d = x_ref.shape

    @pl.when(pl.program_id(0) == 0)
    def _():
        run_ref[...] = jnp.zeros_like(run_ref)

    y = _dot(o_refs[0][...], w_refs[0][...])
    for i in range(1, n_o):
        y = y + _dot(o_refs[i][...], w_refs[i][...])
    xn = (x_ref[...].reshape(t // CHUNK, CHUNK, d) + gate_ref[...] * y.reshape(t // CHUNK, CHUNK, d)).reshape(t, d)
    xn_ref[...] = xn
    h = _norm_mod(xn, nf_ref[...], sh_ref[...], sc_ref[...])
    for s, slab in enumerate(_pack_pairs(h)):
        disp_ref[s] = slab
    logits = _dot3(h, wr_ref[...]) + br_ref[...]
    bucket, w_lo, w_hi = _route(logits.T[0:N_EXPERTS])
    r128 = lax.broadcasted_iota(jnp.int32, (128, t), 0)
    aux = jnp.where(r128 == 0, w_lo, jnp.where(r128 == 1, w_hi, 0.0)).T
    disp_ref[disp_ref.shape[0] - 1] = pltpu.bitcast(aux, jnp.int32)
    brow = lax.broadcasted_iota(jnp.int32, (BUCKET_ROWS, t), 0).astype(F32)
    onehot = jnp.where(brow == bucket, 1.0, 0.0)
    before = _dot(onehot.astype(BF16), tri_ref[...]) + run_ref[:, 0:1]
    rank = jnp.sum(onehot * before, axis=0, keepdims=True)
    run_ref[...] = run_ref[...] + jnp.sum(onehot, axis=1, keepdims=True)
    cnt_ref[...] = run_ref[...]
    r8 = lax.broadcasted_iota(jnp.int32, (8, t), 0)
    meta_ref[...] = jnp.where(r8 == 0, bucket, jnp.where(r8 == 1, rank, 0.0)).astype(jnp.int32)


def _outproj_call(x, os_, ws, gate, nf, shift, scale, wr, br, n_pad):
    n, d = x.shape
    grp = TM // CHUNK
    row = lambda i: (i, 0)
    const = lambda i: (0, 0)
    mod = lambda i: (i, 0, 0)
    n_o = len(os_)
    in_specs = ([pl.BlockSpec((TM, d), row)]
                + [pl.BlockSpec((TM, o.shape[1]), row) for o in os_]
                + [pl.BlockSpec(w.shape, const) for w in ws]
                + [pl.BlockSpec((grp, 1, d), mod), pl.BlockSpec((1, d), const),
                   pl.BlockSpec((grp, 1, d), mod), pl.BlockSpec((grp, 1, d), mod),
                   pl.BlockSpec(wr.shape, const), pl.BlockSpec(br.shape, const),
                   pl.BlockSpec((TM, TM), const)])
    tri = jnp.asarray(np.triu(np.ones((TM, TM), np.float32), k=1), dtype=BF16)
    return pl.pallas_call(
        functools.partial(_outproj_kernel, n_o=n_o),
        grid=(n // TM,),
        in_specs=in_specs,
        out_specs=[pl.BlockSpec((TM, d), row), pl.BlockSpec((DISP_SLABS, TM, 128), lambda i: (0, i, 0)),
                   pl.BlockSpec((8, TM), lambda i: (0, i)), pl.BlockSpec((BUCKET_ROWS, 128), const)],
        out_shape=[jax.ShapeDtypeStruct((n, d), F32), jax.ShapeDtypeStruct((DISP_SLABS, n_pad, 128), jnp.int32),
                   jax.ShapeDtypeStruct((8, n), jnp.int32), jax.ShapeDtypeStruct((BUCKET_ROWS, 128), F32)],
        scratch_shapes=[pltpu.VMEM((BUCKET_ROWS, 128), F32)],
        compiler_params=_cparams(("arbitrary",)),
        name="outproj_router",
    )(x, *os_, *ws, gate, nf, shift, scale, wr, br, tri)


def _sc_mesh():
    return plsc.VectorSubcoreMesh(core_axis_name="core", subcore_axis_name="subcore")


def _sc_scatter_rows(src, idx, n_out):
    r = src.shape[0]
    assert r % (SC_WINDOW * SC_WORKERS) == 0 and idx.shape == (r,)

    @functools.partial(pl.kernel, out_type=jax.ShapeDtypeStruct((n_out, 128), src.dtype), mesh=_sc_mesh(),
                       scratch_types=[])
    def scatter(x_hbm, i_hbm, o_hbm):
        def body(x_vmem, i_vmem):
            pltpu.sync_copy(x_vmem, o_hbm.at[i_vmem.at[0]])

        pltpu.emit_pipeline(
            body,
            grid=(r // SC_WINDOW,),
            in_specs=[pl.BlockSpec((SC_WINDOW, 128), lambda i: (i, 0)),
                      pl.BlockSpec((1, SC_WINDOW), lambda i: (0, i))],
            out_specs=[],
            core_axis_name=("core", "subcore"),
            dimension_semantics=(pltpu.PARALLEL,),
        )(x_hbm, i_hbm)

    return scatter(src, idx.reshape(1, r))


def _sc_gather_rows(table, idx):
    r = idx.shape[0]
    assert r % (SC_WINDOW * SC_WORKERS) == 0

    @functools.partial(pl.kernel, out_type=jax.ShapeDtypeStruct((r, 128), table.dtype), mesh=_sc_mesh(),
                       scratch_types=[])
    def gather(x_hbm, i_hbm, o_hbm):
        def body(i_vmem, o_vmem):
            pltpu.sync_copy(x_hbm.at[i_vmem.at[0]], o_vmem)

        pltpu.emit_pipeline(
            body,
            grid=(r // SC_WINDOW,),
            in_specs=[pl.BlockSpec((1, SC_WINDOW), lambda i: (0, i))],
            out_specs=[pl.BlockSpec((SC_WINDOW, 128), lambda i: (i, 0))],
            core_axis_name=("core", "subcore"),
            dimension_semantics=(pltpu.PARALLEL,),
        )(i_hbm, o_hbm)

    return gather(table, idx.reshape(1, r))


def _moe_kernel(elo_ref, ehi_ref, valid_ref, xs_ref, gu_lo_ref, gu_hi_ref, dn_lo_ref, dn_hi_ref, y_ref):
    i = pl.program_id(0)

    @pl.when(valid_ref[i] == 1)
    def _():
        h = _unpack_pairs([xs_ref[s] for s in range(Y_SLABS)], BF16)
        aux = pltpu.bitcast(xs_ref[Y_SLABS], F32)
        acc = None
        for lane, gu_ref, dn_ref in ((0, gu_lo_ref, dn_lo_ref), (1, gu_hi_ref, dn_hi_ref)):
            ab = _dot(h, gu_ref[0])
            a, b = ab[:, :D_FF], ab[:, D_FF:]
            y = aux[:, lane:lane + 1] * _dot((a * _sigmoid(a) * b).astype(BF16), dn_ref[0])
            acc = y if acc is None else acc + y
        for s, slab in enumerate(_pack_pairs(acc)):
            y_ref[s] = slab

    @pl.when(valid_ref[i] == 0)
    def _():
        y_ref[...] = jnp.zeros_like(y_ref)


def _moe_call(xs, elo, ehi, valid, wgu, wdn, n_tiles):
    d = wgu.shape[1]
    gu = lambda sel: pl.BlockSpec((1, d, 2 * D_FF), lambda i, lo, hi, v: ((lo, hi)[sel][i], 0, 0))
    dn = lambda sel: pl.BlockSpec((1, D_FF, d), lambda i, lo, hi, v: ((lo, hi)[sel][i], 0, 0))
    return pl.pallas_call(
        _moe_kernel,
        grid_spec=pltpu.PrefetchScalarGridSpec(
            num_scalar_prefetch=3,
            grid=(n_tiles,),
            in_specs=[pl.BlockSpec((DISP_SLABS, TMO, 128), lambda i, lo, hi, v: (0, i, 0)),
                      gu(0), gu(1), dn(0), dn(1)],
            out_specs=pl.BlockSpec((Y_SLABS, TMO, 128), lambda i, lo, hi, v: (0, i, 0))),
        out_shape=jax.ShapeDtypeStruct((Y_SLABS, n_tiles * TMO, 128), jnp.int32),
        compiler_params=_cparams(("arbitrary",)),
        name="moe_grouped",
    )(elo, ehi, valid, xs, wgu, wgu, wdn, wdn)


def _moe_layer(disp, meta, counts, wgu, wdn, n, n_pad, sort_rows):
    n_tiles = sort_rows // TMO
    cnt = counts[:N_BUCKETS, 0].astype(jnp.int32)
    padded = ((cnt + TMO - 1) // TMO) * TMO
    ends = jnp.cumsum(padded)
    offs = ends - padded
    bucket, rank = meta[0], meta[1]
    pos = rank + jnp.sum(jnp.where(bucket[None, :] == jnp.arange(N_BUCKETS, dtype=jnp.int32)[:, None],
                                   offs[:, None], 0), axis=0)
    tile_start = jnp.arange(n_tiles, dtype=jnp.int32) * TMO
    tile_bucket = jnp.minimum(jnp.sum((tile_start[:, None] >= ends[None, :]).astype(jnp.int32), axis=1), N_BUCKETS - 1)
    valid = (tile_start < ends[-1]).astype(jnp.int32)
    pair_lo = np.array([0, 0, 0, 1, 1, 2], np.int32)
    pair_hi = np.array([1, 2, 3, 2, 3, 3], np.int32)
    b_lo = jnp.asarray(np.repeat(np.arange(N_GROUPS), N_PAIRS) * EXP_PER_GROUP + np.tile(pair_lo, N_GROUPS), jnp.int32)
    b_hi = jnp.asarray(np.repeat(np.arange(N_GROUPS), N_PAIRS) * EXP_PER_GROUP + np.tile(pair_hi, N_GROUPS), jnp.int32)
    onehot_tb = (tile_bucket[:, None] == jnp.arange(N_BUCKETS, dtype=jnp.int32)[None, :]).astype(jnp.int32)
    elo = jnp.sum(onehot_tb * b_lo[None, :], axis=1)
    ehi = jnp.sum(onehot_tb * b_hi[None, :], axis=1)
    dump = sort_rows + jnp.arange(n_pad - n, dtype=jnp.int32)
    pos_sc = jnp.concatenate([pos, dump])
    total = sort_rows + n_pad - n
    sc_idx = (pos_sc[None, :] + (jnp.arange(DISP_SLABS, dtype=jnp.int32) * total)[:, None]).reshape(-1)
    xs = _sc_scatter_rows(disp.reshape(DISP_SLABS * n_pad, 128), sc_idx, DISP_SLABS * total)
    ys = _moe_call(xs.reshape(DISP_SLABS, total, 128), elo, ehi, valid, wgu, wdn, n_tiles)
    pos_g = jnp.concatenate([pos, jnp.zeros((n_pad - n,), jnp.int32)])
    g_idx = (pos_g[None, :] + (jnp.arange(Y_SLABS, dtype=jnp.int32) * sort_rows)[:, None]).reshape(-1)
    z = _sc_gather_rows(ys.reshape(Y_SLABS * sort_rows, 128), g_idx)
    return z.reshape(Y_SLABS, n_pad, 128)


def _final_kernel(xn_ref, z_ref, gate_ref, g_ref, o_ref):
    x = _add_moe(xn_ref, z_ref, gate_ref)
    ms = jnp.mean(x * x, axis=-1, keepdims=True)
    o_ref[...] = x * lax.rsqrt(ms + EPS) * g_ref[...]


def _final_call(xn, z, gate, g):
    n, d = xn.shape
    grp = TM // CHUNK
    return pl.pallas_call(
        _final_kernel,
        grid=(n // TM,),
        in_specs=[pl.BlockSpec((TM, d), lambda i: (i, 0)), pl.BlockSpec((z.shape[0], TM, 128), lambda i: (0, i, 0)),
                  pl.BlockSpec((grp, 1, d), lambda i: (i, 0, 0)), pl.BlockSpec((1, d), lambda i: (0, 0))],
        out_specs=pl.BlockSpec((TM, d), lambda i: (i, 0)),
        out_shape=jax.ShapeDtypeStruct((n, d), F32),
        compiler_params=_cparams(("parallel",)),
        name="final_norm",
    )(xn, z, gate, g)


def kernel(x_prompt, x_sample, c_prompt, c_sample, state_gla, cache_band_k, cache_band_v, cache_swa_k, cache_swa_v,
           w_ada, b_ada, norm_mix, norm_ffn, norm_final, w_in_even, w_gate_a, b_gate_a, gla_norm, rel_bias_b,
           w_out_even, w_in_odd, sinks_c, w_out_odd, w_router, b_router, w_gate_up, w_down):
    bp, lp, d = x_prompt.shape
    bs, ls_, _ = x_sample.shape
    n_p, n_s = bp * lp, bs * ls_
    n = n_p + n_s
    assert ls_ == CHUNK and n_s == TM and lp % TM == 0 and PAST_LEN % CHUNK == 0

    x = jnp.concatenate([x_prompt.reshape(n_p, d), x_sample.reshape(n_s, d)], axis=0)

    c16 = jnp.zeros((SEQ_ROWS, d), F32).at[:bp].set(c_prompt).at[bp:bp + bs].set(c_sample)
    mods = _ada_call(c16, w_ada, b_ada)
    seq_of_group = np.concatenate([np.repeat(np.arange(bp), lp // CHUNK), bp + np.arange(bs)])
    n_grp = n // CHUNK

    def mod_parts(l):
        m = mods[l][seq_of_group].reshape(n_grp, 6, 1, d)
        return [m[:, i] for i in range(6)]

    perm = np.array([4 * (c % 4) + c // 4 for c in range(N_EXPERTS)])
    wr = jnp.zeros((d, 128), F32).at[:, :N_EXPERTS].set(w_router[:, perm])
    br = jnp.zeros((1, 128), F32).at[0, :N_EXPERTS].set(b_router[perm])

    wgu = w_gate_up.astype(BF16)
    wdn = w_down.astype(BF16)

    sc_unit = SC_WINDOW * SC_WORKERS
    n_pad = -(-n // sc_unit) * sc_unit
    sort_rows = -(-(n + N_BUCKETS * TMO) // sc_unit) * sc_unit

    gla_p = gla_s = bk_p = bv_p = bk_s = bv_s = sk_p = sv_p = sk_s = sv_s = None
    xn = z = g_prev = None
    for l in range(DEPTH):
        sh_m, sc_m, g_m, sh_f, sc_f, g_f = mod_parts(l)
        i = l // 2
        if l % 2 == 0:
            w = w_in_even[i]
            w_main = jnp.concatenate([w[:, :1536], w[:, 1552:]], axis=1).astype(BF16)
            w_la = jnp.zeros((d, 128), F32).at[:, :GATE_RANK].set(w[:, 1536:1552]).astype(BF16)
            w_gate = jnp.zeros((128, HA * DKA), F32).at[:GATE_RANK].set(w_gate_a[i])
            qa, ka, va, ra, qb, kb, vb, ga = _inproj_even_call(
                x, sh_m, sc_m, norm_mix[l][None], w_main, w_la, w_gate, b_gate_a[i][None])
            gn = gla_norm[i][None]
            oa, s_p = _gla_call(qa, ka, va, ga, ra, jnp.zeros((bp, 256, 128), F32), gn, None,
                                n_seq=bp, seq_rows=lp, row0=0, nb=8)
            oa, s_s = _gla_call(qa, ka, va, ga, ra, state_gla[i].reshape(bs, 256, 128), gn, oa,
                                n_seq=bs, seq_rows=ls_, row0=n_p, nb=1)
            gla_p, gla_s = s_p.reshape(1, bp, HA, DKA, DVA), s_s.reshape(1, bs, HA, DKA, DVA)
            pb = N_PREV_B * CHUNK
            tq, g = 512, 2
            bps = lp // tq
            bias = _band_bias(rel_bias_b[i], g, pb)
            ob = _attn_call(
                functools.partial(_band_kernel, g=g, n_sub=tq // (CHUNK * g), pb=pb, blocks_per_seq=bps, mask_first=True),
                qb, kb, kb, vb, vb, [bias], [pl.BlockSpec(bias.shape, lambda i_: (0, 0, 0))], None,
                width=512, kv_width=512, tq=tq, pb=pb, n_blocks=n_p // tq, blk0=0,
                prev_map=lambda i_: (jnp.maximum(i_ * (tq // pb) - 1, 0), 0), name="band_prompt")
            ck = cache_band_k[i].reshape(bs * pb, HB * DHB).astype(BF16)
            cv = cache_band_v[i].reshape(bs * pb, HB * DHB).astype(BF16)
            bias1 = _band_bias(rel_bias_b[i], 1, pb)
            ob = _attn_call(
                functools.partial(_band_kernel, g=1, n_sub=1, pb=pb, blocks_per_seq=1, mask_first=False),
                qb, ck, kb, cv, vb, [bias1], [pl.BlockSpec(bias1.shape, lambda i_: (0, 0, 0))], ob,
                width=512, kv_width=512, tq=CHUNK, pb=pb, n_blocks=bs, blk0=n_p // CHUNK,
                prev_map=lambda i_: (i_, 0), name="band_sample")
            tail = lambda a: jnp.stack([a[(b + 1) * lp - pb:(b + 1) * lp] for b in range(bp)]).astype(F32).reshape(1, bp, pb, HB, DHB)
            new = lambda a: a[n_p:].astype(F32).reshape(bs, ls_, HB, DHB)
            bk_p, bv_p = tail(kb), tail(vb)
            bk_s = jnp.concatenate([cache_band_k[i][:, ls_:], new(kb)], axis=1)[None]
            bv_s = jnp.concatenate([cache_band_v[i][:, ls_:], new(vb)], axis=1)[None]
            wo = w_out_even[i].astype(BF16)
            os_, ws = [oa, ob], [wo[:HA * DVA], wo[HA * DVA:]]
        else:
            w = w_in_odd[i]
            wk, wv = w[:, 1024:1152], w[:, 1152:1280]
            dup = lambda a: jnp.concatenate([a[:, :64], a[:, :64], a[:, 64:], a[:, 64:]], axis=1)
            w_all = jnp.concatenate([w[:, :1024], dup(wk), dup(wv)], axis=1).astype(BF16)
            pos = np.concatenate([np.tile(np.arange(lp), bp), np.tile(PAST_LEN + np.arange(ls_), bs)])
            half = DHC // 2
            inv = ROPE_THETA ** (-jnp.arange(half, dtype=F32) / half)
            ang = jnp.asarray(pos, F32)[:, None] * inv[None, :]
            cs_, sn_ = jnp.cos(ang), jnp.sin(ang)
            cos = jnp.concatenate([cs_, cs_, cs_, cs_], axis=1)
            sin = jnp.concatenate([-sn_, sn_, -sn_, sn_], axis=1)
            x, q, k, v = _inproj_odd_call(xn, z, g_prev, sh_m, sc_m, norm_mix[l][None], cos, sin, w_all)
            pb = WINDOW
            tq, g = 512, 2
            bps = lp // tq
            sink = sinks_c[i][None]
            sink_spec = [pl.BlockSpec(memory_space=pltpu.SMEM)]
            o = _attn_call(
                functools.partial(_swa_kernel, g=g, n_sub=tq // (CHUNK * g), pb=pb, blocks_per_seq=bps, mask_first=True),
                q, k, k, v, v, [sink], sink_spec, None,
                width=1024, kv_width=256, tq=tq, pb=pb, n_blocks=n_p // tq, blk0=0,
                prev_map=lambda i_: (jnp.maximum(i_ * (tq // pb) - 1, 0), 0), name="swa_prompt")
            dupc = lambda c: jnp.concatenate([c[:, :, 0], c[:, :, 0], c[:, :, 1], c[:, :, 1]], axis=-1).reshape(bs * pb, 256).astype(BF16)
            ck, cv = dupc(cache_swa_k[i]), dupc(cache_swa_v[i])
            o = _attn_call(
                functools.partial(_swa_kernel, g=1, n_sub=1, pb=pb, blocks_per_seq=1, mask_first=False),
                q, ck, k, cv, v, [sink], sink_spec, o,
                width=1024, kv_width=256, tq=CHUNK, pb=pb, n_blocks=bs, blk0=n_p // CHUNK,
                prev_map=lambda i_: (i_, 0), name="swa_sample")
            undup = lambda a: jnp.concatenate([a[:, 0:64], a[:, 128:192]], axis=1).astype(F32)
            tail = lambda a: jnp.stack([undup(a[(b + 1) * lp - pb:(b + 1) * lp]) for b in range(bp)]).reshape(1, bp, pb, KVC, DHC)
            new = lambda a: undup(a[n_p:]).reshape(bs, ls_, KVC, DHC)
            sk_p, sv_p = tail(k), tail(v)
            sk_s = jnp.concatenate([cache_swa_k[i][:, ls_:], new(k)], axis=1)[None]
            sv_s = jnp.concatenate([cache_swa_v[i][:, ls_:], new(v)], axis=1)[None]
            os_, ws = [o], [w_out_odd[i].astype(BF16)]
        xn, disp, meta, counts = _outproj_call(x, os_, ws, g_m, norm_ffn[l][None], sh_f, sc_f, wr, br, n_pad)
        z = _moe_layer(disp, meta, counts, wgu[l], wdn[l], n, n_pad, sort_rows)
        g_prev = g_f

    y = _final_call(xn, z, g_prev, norm_final[None])
    y_prompt = y[:n_p].reshape(bp, lp, d)
    y_sample = y[n_p:].reshape(bs, ls_, d)
    return (y_prompt, y_sample, gla_p, gla_s, bk_p, bv_p, bk_s, bv_s, sk_p, sv_p, sk_s, sv_s)
```

```python
import functools

import numpy as np
import jax
import jax.numpy as jnp
from jax import lax
from jax.experimental import pallas as pl
from jax.experimental.pallas import tpu as pltpu
from jax.experimental.pallas import tpu_sc as plsc

F32 = jnp.float32
BF16 = jnp.bfloat16

D_MODEL = 1024
DEPTH = 2
CHUNK = 64
PAST_LEN = 4096
HA, DKA, DVA = 4, 64, 128
GATE_RANK = 16
GATE_TAU = 16.0
HB, DHB = 8, 64
N_PREV_B = 8
MAX_REL = 128
HC, KVC, DHC = 16, 2, 64
WINDOW = 128
ROPE_THETA = 10000.0
N_EXPERTS = 16
N_GROUPS = 4
EXP_PER_GROUP = 4
D_FF = 512
EPS = 1e-6

N_PAIRS = 6
N_BUCKETS = N_GROUPS * N_PAIRS
BUCKET_ROWS = 32
Y_SLABS = 4
DISP_SLABS = Y_SLABS + 1
TMO = 256
SC_WINDOW = 128
SC_WORKERS = 32
SC_GROUP = 3

TM = 512
SEQ_ROWS = 16
SUB = 16
NEG = -1e30


def _cparams(sem, vmem_mb=48):
    return pltpu.CompilerParams(dimension_semantics=sem, vmem_limit_bytes=vmem_mb * 1024 * 1024)


def _dot(a, b):
    return jnp.dot(a, b, preferred_element_type=F32)


def _dot_nt(a, b):
    return lax.dot_general(a, b, (((1,), (1,)), ((), ())), preferred_element_type=F32)


def _split(a):
    hi = a.astype(BF16)
    lo = (a - hi.astype(F32)).astype(BF16)
    return hi, lo


def _dot3(a, b):
    ah, al = _split(a)
    bh, bl = _split(b)
    return _dot(ah, bh) + _dot(ah, bl) + _dot(al, bh)


def _sigmoid(x):
    return 1.0 / (1.0 + jnp.exp(-x))


def _group_affine(y, mul, add):
    parts = []
    for gi in range(y.shape[0] // CHUNK):
        p = y[gi * CHUNK:(gi + 1) * CHUNK]
        if mul is not None:
            p = p * mul[gi:gi + 1]
        if add is not None:
            p = p + add[gi:gi + 1]
        parts.append(p)
    return jnp.concatenate(parts, axis=0)


def _norm_mod(x, g, shift, scale):
    ms = jnp.mean(x * x, axis=-1, keepdims=True)
    return _group_affine(x * lax.rsqrt(ms + EPS) * g, 1.0 + scale, shift)


def _mod_spec(part):
    return pl.BlockSpec((TM // CHUNK, D_MODEL), lambda i: (i, part))


SHIFT_MIX, SCALE_MIX, GATE_MIX, SHIFT_FFN, SCALE_FFN, GATE_FFN = range(6)


def _load_tokens(xp_ref, xs_ref, n_prompt_tiles):
    return lax.cond(pl.program_id(0) < n_prompt_tiles, lambda: xp_ref[...], lambda: xs_ref[...])


def _token_specs(n_prompt_tiles, d):
    return [pl.BlockSpec((TM, d), lambda i: (jnp.minimum(i, n_prompt_tiles - 1), 0)),
            pl.BlockSpec((TM, d), lambda i: (0, 0))]


def _ada_kernel(c_ref, w_ref, b_ref, o_ref):
    c = c_ref[...]
    o_ref[0] = _dot3(c * _sigmoid(c), w_ref[0]) + b_ref[0]


def _ada_call(c16, w_ada, b_ada):
    d = D_MODEL
    tn = 1024
    return pl.pallas_call(
        _ada_kernel,
        grid=(DEPTH, 6 * d // tn),
        in_specs=[pl.BlockSpec((SEQ_ROWS, d), lambda l, j: (0, 0)),
                  pl.BlockSpec((1, d, tn), lambda l, j: (l, 0, j)),
                  pl.BlockSpec((1, 1, tn), lambda l, j: (l, 0, j))],
        out_specs=pl.BlockSpec((1, SEQ_ROWS, tn), lambda l, j: (l, 0, j)),
        out_shape=jax.ShapeDtypeStruct((DEPTH, SEQ_ROWS, 6 * d), F32),
        compiler_params=_cparams(("arbitrary", "arbitrary")),
        name="ada",
    )(c16, w_ada, b_ada.reshape(DEPTH, 1, 6 * d))


def _inproj_even_kernel(xp_ref, xs_ref, sh_ref, sc_ref, g_ref, w_ref, wla_ref, wg_ref, bg_ref,
                        qa_ref, ka_ref, va_ref, ra_ref, qb_ref, kb_ref, vb_ref, ga_ref, *, n_prompt_tiles):
    x = _load_tokens(xp_ref, xs_ref, n_prompt_tiles)
    hb = _norm_mod(x, g_ref[...], sh_ref[...], sc_ref[...]).astype(BF16)
    qa_ref[...] = (_dot(hb, w_ref[:, 0:256]) * (DKA ** -0.5)).astype(BF16)
    ka_ref[...] = _dot(hb, w_ref[:, 256:512]).astype(BF16)
    va_ref[...] = _dot(hb, w_ref[:, 512:1024]).astype(BF16)
    ra_ref[...] = _dot(hb, w_ref[:, 1024:1536]).astype(BF16)
    qb_ref[...] = (_dot(hb, w_ref[:, 1536:2048]) * (DHB ** -0.5)).astype(BF16)
    kb_ref[...] = _dot(hb, w_ref[:, 2048:2560]).astype(BF16)
    vb_ref[...] = _dot(hb, w_ref[:, 2560:3072]).astype(BF16)
    la = _dot(hb, wla_ref[...])
    gl = _dot3(la, wg_ref[...]) + bg_ref[...]
    ga_ref[...] = -(jnp.maximum(-gl, 0.0) + jnp.log(1.0 + jnp.exp(-jnp.abs(gl)))) * (1.0 / GATE_TAU)


def _inproj_even_call(xp, xs, mods, g, w_main, w_la, w_gate, b_gate):
    d = xp.shape[1]
    npt = xp.shape[0] // TM
    n = xp.shape[0] + xs.shape[0]
    row = lambda i: (i, 0)
    const = lambda i: (0, 0)
    widths = (256, 256, 512, 512, 512, 512, 512)
    out_shape = [jax.ShapeDtypeStruct((n, w), BF16) for w in widths] + [jax.ShapeDtypeStruct((n, 256), F32)]
    out_specs = [pl.BlockSpec((TM, w), row) for w in widths] + [pl.BlockSpec((TM, 256), row)]
    return pl.pallas_call(
        functools.partial(_inproj_even_kernel, n_prompt_tiles=npt),
        grid=(n // TM,),
        in_specs=_token_specs(npt, d) + [
            _mod_spec(SHIFT_MIX), _mod_spec(SCALE_MIX),
            pl.BlockSpec((1, d), const),
            pl.BlockSpec(w_main.shape, const), pl.BlockSpec(w_la.shape, const),
            pl.BlockSpec(w_gate.shape, const), pl.BlockSpec(b_gate.shape, const)],
        out_specs=out_specs, out_shape=out_shape,
        compiler_params=_cparams(("parallel",)),
        name="inproj_even",
    )(xp, xs, mods, mods, g, w_main, w_la, w_gate, b_gate)


def _rope(x, cos, sin_signed):
    t, w = x.shape
    lane = lax.broadcasted_iota(jnp.int32, (1, w), 1)
    first_half = (lane & 63) < 32
    rot = jnp.where(first_half, pltpu.roll(x, w - 32, 1), pltpu.roll(x, 32, 1))
    reps = w // 128
    return x * jnp.tile(cos, (1, reps)) + rot * jnp.tile(sin_signed, (1, reps))


def _unpack_pairs(slabs, dtype):
    lo = [pltpu.bitcast(s << 16, F32) for s in slabs]
    hi = [pltpu.bitcast(s & jnp.int32(-65536), F32) for s in slabs]
    return jnp.concatenate(lo + hi, axis=1).astype(dtype)


def _pack_pairs(x):
    bits = pltpu.bitcast(x.astype(BF16).astype(F32), jnp.int32)
    half = x.shape[1] // 2
    packed = ((bits[:, :half] >> 16) & jnp.int32(0xFFFF)) | (bits[:, half:] & jnp.int32(-65536))
    return [packed[:, 128 * s:128 * (s + 1)] for s in range(half // 128)]


def _add_moe(xn_ref, z_ref, gate_ref):
    y = _unpack_pairs([z_ref[s] for s in range(z_ref.shape[0])], F32)
    return xn_ref[...] + _group_affine(y, gate_ref[...], None)


def _rope_tables(lp, ls_, bp, bs):
    assert PAST_LEN + ls_ <= lp and lp % 128 == 0 and bs * ls_ == TM
    half = DHC // 2
    inv = ROPE_THETA ** (-jnp.arange(half, dtype=F32) / half)
    a = jnp.asarray(np.arange(lp // 128) * 128, F32)[:, None] * inv[None, :]
    b = jnp.asarray(np.arange(128), F32)[:, None] * inv[None, :]
    ca, sa, cb, sb = jnp.cos(a)[:, None], jnp.sin(a)[:, None], jnp.cos(b)[None], jnp.sin(b)[None]
    cos_p = (ca * cb - sa * sb).reshape(lp, half)
    sin_p = (sa * cb + ca * sb).reshape(lp, half)
    with_sample = lambda t: jnp.concatenate([t, jnp.tile(t[PAST_LEN:PAST_LEN + ls_], (bs, 1))], axis=0)
    cos_p, sin_p = with_sample(cos_p), with_sample(sin_p)
    cos = jnp.concatenate([cos_p] * 4, axis=1)
    sin = jnp.concatenate([-sin_p, sin_p, -sin_p, sin_p], axis=1)
    tiles = lp // TM
    return cos, sin, lambda i: (jnp.where(i < bp * tiles, i % tiles, tiles), 0)


def _inproj_odd_kernel(xn_ref, z_ref, gate_ref, sh_ref, sc_ref, g_ref, cos_ref, sin_ref, w_ref,
                       x_ref, q_ref, k_ref, v_ref):
    x = _add_moe(xn_ref, z_ref, gate_ref)
    x_ref[...] = x
    hb = _norm_mod(x, g_ref[...], sh_ref[...], sc_ref[...]).astype(BF16)
    cos, sin = cos_ref[...], sin_ref[...]
    q = _rope(_dot(hb, w_ref[:, 0:1024]), cos, sin)
    q_ref[...] = (q * (DHC ** -0.5)).astype(BF16)
    k_ref[...] = _rope(_dot(hb, w_ref[:, 1024:1280]), cos, sin).astype(BF16)
    v_ref[...] = _dot(hb, w_ref[:, 1280:1536]).astype(BF16)


def _inproj_odd_call(xn, z, mods_prev, mods, g, cos, sin, rope_map, w):
    n, d = xn.shape
    row = lambda i: (i, 0)
    const = lambda i: (0, 0)
    widths = (1024, 256, 256)
    return pl.pallas_call(
        _inproj_odd_kernel,
        grid=(n // TM,),
        in_specs=[pl.BlockSpec((TM, d), row), pl.BlockSpec((z.shape[0], TM, 128), lambda i: (0, i, 0)),
                  _mod_spec(GATE_FFN), _mod_spec(SHIFT_MIX), _mod_spec(SCALE_MIX),
                  pl.BlockSpec((1, d), const),
                  pl.BlockSpec((TM, 128), rope_map), pl.BlockSpec((TM, 128), rope_map),
                  pl.BlockSpec(w.shape, const)],
        out_specs=[pl.BlockSpec((TM, d), row)] + [pl.BlockSpec((TM, wd), row) for wd in widths],
        out_shape=[jax.ShapeDtypeStruct((n, d), F32)] + [jax.ShapeDtypeStruct((n, wd), BF16) for wd in widths],
        compiler_params=_cparams(("parallel",)),
        name="inproj_odd",
    )(xn, z, mods_prev, mods, mods, g, cos, sin, w)


def _gla_tri():
    t = np.arange(CHUNK)[:, None]
    s = np.arange(CHUNK)[None, :]
    cum = s <= t
    start = s < (t // SUB) * SUB
    end = s < (t // SUB + 1) * SUB
    return jnp.asarray(np.concatenate([cum, start, end], axis=0).astype(np.float32), dtype=BF16)


def _gla_kernel(q_ref, k_ref, v_ref, g_ref, r_ref, s0_ref, gn_ref, tri_ref, o_ref, sout_ref, s_ref, *, nb):
    c_ = CHUNK
    nsub = c_ // SUB

    @pl.when(pl.program_id(1) == 0)
    def _():
        s_ref[...] = s0_ref[0]

    tri = tri_ref[...]
    lane = lax.broadcasted_iota(jnp.int32, (1, 128), 1)
    hmask = [jnp.where(lane < DKA, 1.0, 0.0), jnp.where(lane >= DKA, 1.0, 0.0)]
    ti = lax.broadcasted_iota(jnp.int32, (c_, c_), 0)
    si = lax.broadcasted_iota(jnp.int32, (c_, c_), 1)
    rb, cb = ti >> 4, si >> 4
    m_diag = (rb == cb) & (si <= ti)
    m_off = [(cb == j) & (rb > j) for j in range(nsub - 1)]
    hk = HA * DKA
    eye = lax.broadcasted_iota(jnp.int32, (hk, hk), 0) == lax.broadcasted_iota(jnp.int32, (hk, hk), 1)
    gn = gn_ref[...]

    for c in range(nb):
        rows = slice(c * c_, (c + 1) * c_)
        g_hi, g_lo = _split(g_ref[rows, :])
        cs = _dot(tri, g_hi) + _dot(tri, g_lo)
        b, rs, re = cs[0:c_], cs[c_:2 * c_], cs[2 * c_:3 * c_]
        q = q_ref[rows, :].astype(F32)
        k = k_ref[rows, :].astype(F32)
        bl = b[c_ - 1:c_, :]
        qd = q * jnp.exp(b - rs)
        kd = k * jnp.exp(rs - b)
        ke = k * jnp.exp(re - b)
        qi = q * jnp.exp(b)
        kl = k * jnp.exp(bl - b)
        ql = [q * jnp.exp(jnp.minimum(b - b[SUB * (j + 1) - 1:SUB * (j + 1), :], 0.0)) for j in range(nsub - 1)]
        dcol = jnp.sum(jnp.where(eye, jnp.broadcast_to(jnp.exp(bl), (hk, hk)), 0.0), axis=1, keepdims=True)
        s_old = s_ref[...]
        s_old_b = s_old.astype(BF16)
        upd = []
        for p in range(HA // 2):
            ls = slice(128 * p, 128 * (p + 1))
            kd_p = kd[:, ls].astype(BF16)
            ke_p = ke[:, ls].astype(BF16)
            klt = kl[:, ls].T
            sp_b = s_old_b[ls, :]
            for hh in range(2):
                h = 2 * p + hh
                msk = hmask[hh]
                a1 = _dot_nt((qd[:, ls] * msk).astype(BF16), kd_p)
                lhs2 = jnp.concatenate([ql[j][:, ls] * msk for j in range(nsub - 1)], axis=0).astype(BF16)
                a2 = _dot_nt(lhs2, ke_p)
                att = jnp.zeros((c_, c_), F32)
                for j in reversed(range(nsub - 1)):
                    att = jnp.where(m_off[j], a2[j * c_:(j + 1) * c_], att)
                att = jnp.where(m_diag, a1, att)
                vs = slice(DVA * h, DVA * (h + 1))
                v_h = v_ref[rows, vs]
                o = _dot(att.astype(BF16), v_h) + _dot((qi[:, ls] * msk).astype(BF16), sp_b)
                ms = jnp.mean(o * o, axis=-1, keepdims=True)
                rr = r_ref[rows, vs].astype(F32)
                o_ref[rows, vs] = (o * lax.rsqrt(ms + EPS) * gn * (rr * _sigmoid(rr))).astype(BF16)
                upd.append(_dot(klt[DKA * hh:DKA * (hh + 1)].astype(BF16), v_h))
        s_ref[...] = dcol * s_old + jnp.concatenate(upd, axis=0)
    sout_ref[0] = s_ref[...]


def _gla_call(q, k, v, g, r, s0, gn, o_prev, *, n_seq, seq_rows, row0, nb):
    tq = nb * CHUNK
    steps = seq_rows // tq
    blk0 = row0 // tq
    row = lambda b, j: (blk0 + b * steps + j, 0)
    const = lambda b, j: (0, 0)
    tri = _gla_tri()
    in_specs = [pl.BlockSpec((tq, 256), row), pl.BlockSpec((tq, 256), row), pl.BlockSpec((tq, 512), row),
                pl.BlockSpec((tq, 256), row), pl.BlockSpec((tq, 512), row),
                pl.BlockSpec((1, 256, 128), lambda b, j: (b, 0, 0)),
                pl.BlockSpec((1, 128), const), pl.BlockSpec(tri.shape, const)]
    args = [q, k, v, g, r, s0, gn, tri]
    aliases = {}
    if o_prev is not None:
        in_specs.append(pl.BlockSpec(memory_space=pl.ANY))
        args.append(o_prev)
        aliases = {len(args) - 1: 0}
    kern = functools.partial(_gla_kernel, nb=nb)
    if o_prev is not None:
        kern = _drop_arg(kern, 8)
    return pl.pallas_call(
        kern,
        grid=(n_seq, steps),
        in_specs=in_specs,
        out_specs=[pl.BlockSpec((tq, 512), row), pl.BlockSpec((1, 256, 128), lambda b, j: (b, 0, 0))],
        out_shape=[jax.ShapeDtypeStruct((q.shape[0], 512), BF16), jax.ShapeDtypeStruct((n_seq, 256, 128), F32)],
        scratch_shapes=[pltpu.VMEM((256, 128), F32)],
        input_output_aliases=aliases,
        compiler_params=_cparams(("arbitrary", "arbitrary")),
        name="gla",
    )(*args)


def _drop_arg(fn, idx):
    def wrapped(*refs):
        return fn(*refs[:idx], *refs[idx + 1:])
    return wrapped


def _window(prev_ref, cur_ref, lo, hi, pb, ls):
    if lo < pb:
        return jnp.concatenate([prev_ref[lo:pb, ls], cur_ref[0:hi - pb, ls]], axis=0)
    return cur_ref[lo - pb:hi - pb, ls]


def _band_kernel(q_ref, kp_ref, kc_ref, vp_ref, vc_ref, bias_ref, o_ref, *, g, n_sub, pb, blocks_per_seq, mask_first):
    qs = CHUNK * g
    kw_rows = pb + qs
    row = lax.broadcasted_iota(jnp.int32, (qs, kw_rows), 0)
    col = lax.broadcasted_iota(jnp.int32, (qs, kw_rows), 1)
    dd = (col >> 6) - (row >> 6)
    band = (dd >= 0) & (dd <= pb // CHUNK)
    lane = lax.broadcasted_iota(jnp.int32, (1, 128), 1)
    low = lane < DHB
    hmask = [jnp.where(low, 1.0, 0.0), jnp.where(low, 0.0, 1.0)]
    first = (pl.program_id(0) % blocks_per_seq) == 0
    for s in range(n_sub):
        valid = band
        if mask_first:
            valid = band & (col >= jnp.where(first, pb - qs * s, 0))
        for p in range(HB // 2):
            ls = slice(128 * p, 128 * (p + 1))
            qp = q_ref[qs * s:qs * (s + 1), ls].astype(F32)
            kw = _window(kp_ref, kc_ref, qs * s, qs * s + kw_rows, pb, ls)
            vw = _window(vp_ref, vc_ref, qs * s, qs * s + kw_rows, pb, ls)
            outs = []
            for hh in range(2):
                sc = _dot_nt((qp * hmask[hh]).astype(BF16), kw)
                sc = jnp.where(valid, sc + bias_ref[2 * p + hh], NEG)
                m = jnp.max(sc, axis=-1, keepdims=True)
                pe = jnp.exp(sc - m)
                l = jnp.sum(pe, axis=-1, keepdims=True)
                outs.append(_dot(pe.astype(BF16), vw) / l)
            o_ref[qs * s:qs * (s + 1), ls] = jnp.where(low, outs[0], outs[1]).astype(BF16)


def _band_bias(table, g, pb):
    rows, kw = CHUNK * g, pb + CHUNK * g
    period = kw + rows
    m = np.arange(period)
    m = np.where(m < kw, m, m - period)
    ext = table[:, np.clip(m - pb, -MAX_REL, MAX_REL) + MAX_REL]
    flat = jnp.tile(ext, (1, rows))[:, :rows * (period - 1)]
    return flat.reshape(table.shape[0], rows, period - 1)[:, :, :kw]


def _attn_call(kernel, q, kp, kc, vp, vc, extra, extra_specs, o_prev, *, width, kv_width, tq, pb,
               n_blocks, blk0, prev_map, name, n_prefetch=0):
    row = lambda i, *_: (blk0 + i, 0)
    in_specs = [pl.BlockSpec((tq, width), row),
                pl.BlockSpec((pb, kv_width), prev_map), pl.BlockSpec((tq, kv_width), row),
                pl.BlockSpec((pb, kv_width), prev_map), pl.BlockSpec((tq, kv_width), row)] + extra_specs
    args = [q, kp, kc, vp, vc] + extra
    aliases = {}
    if o_prev is not None:
        in_specs.append(pl.BlockSpec(memory_space=pl.ANY))
        args.append(o_prev)
        aliases = {len(args) - 1: 0}
        kernel = _drop_arg(kernel, len(args) - 1)
    return pl.pallas_call(
        kernel,
        grid=(n_blocks,),
        in_specs=in_specs,
        out_specs=pl.BlockSpec((tq, width), row),
        out_shape=jax.ShapeDtypeStruct((q.shape[0], width), BF16),
        input_output_aliases=aliases,
        compiler_params=_cparams(("parallel",)),
        name=name,
    )(*args)


def _swa_kernel(q_ref, kp_ref, kc_ref, vp_ref, vc_ref, sink_ref, o_ref, *, g, n_sub, pb, blocks_per_seq, mask_first):
    qs = CHUNK * g
    kw_rows = pb + qs
    row = lax.broadcasted_iota(jnp.int32, (qs, kw_rows), 0)
    col = lax.broadcasted_iota(jnp.int32, (qs, kw_rows), 1)
    dd = (col >> 6) - (row >> 6)
    band = (dd >= 0) & (dd <= pb // CHUNK)
    lane = lax.broadcasted_iota(jnp.int32, (1, 128), 1)
    low = lane < DHC
    hmask = [jnp.where(low, 1.0, 0.0), jnp.where(low, 0.0, 1.0)]
    first = (pl.program_id(0) % blocks_per_seq) == 0
    pairs_per_kv = HC // KVC // 2
    for s in range(n_sub):
        valid = band
        if mask_first:
            valid = band & (col >= jnp.where(first, pb - qs * s, 0))
        for kv in range(KVC):
            kvs = slice(128 * kv, 128 * (kv + 1))
            kw = _window(kp_ref, kc_ref, qs * s, qs * s + kw_rows, pb, kvs)
            vw = _window(vp_ref, vc_ref, qs * s, qs * s + kw_rows, pb, kvs)
            for jj in range(pairs_per_kv):
                j = kv * pairs_per_kv + jj
                ls = slice(128 * j, 128 * (j + 1))
                qp = q_ref[qs * s:qs * (s + 1), ls].astype(F32)
                outs = []
                for hh in range(2):
                    sc = _dot_nt((qp * hmask[hh]).astype(BF16), kw)
                    sc = jnp.where(valid, sc, NEG)
                    sk = sink_ref[0, 2 * j + hh]
                    m = jnp.maximum(jnp.max(sc, axis=-1, keepdims=True), sk)
                    pe = jnp.exp(sc - m)
                    l = jnp.sum(pe, axis=-1, keepdims=True) + jnp.exp(sk - m)
                    outs.append(_dot(pe.astype(BF16), vw) / l)
                o_ref[qs * s:qs * (s + 1), ls] = jnp.where(low, outs[0], outs[1]).astype(BF16)


def _route(logits_t):
    a = [logits_t[4 * j:4 * j + 4] for j in range(EXP_PER_GROUP)]

    def first_argmax(vals, m):
        idx = jnp.full(m.shape, float(len(vals) - 1), F32)
        for j in reversed(range(len(vals) - 1)):
            idx = jnp.where(vals[j] == m, float(j), idx)
        return idx

    m1 = functools.reduce(jnp.maximum, a)
    i1 = first_argmax(a, m1)
    bsec = [jnp.where(i1 == float(j), -jnp.inf, a[j]) for j in range(EXP_PER_GROUP)]
    m2 = functools.reduce(jnp.maximum, bsec)
    i2 = first_argmax(bsec, m2)
    rows = lambda x: [x[gi:gi + 1] for gi in range(N_GROUPS)]
    gm = functools.reduce(jnp.maximum, rows(m1))
    gscore = jnp.exp(m1 - gm) + jnp.exp(m2 - gm)
    gs = rows(gscore)
    gsel = first_argmax(gs, functools.reduce(jnp.maximum, gs))

    def pick(x):
        xr = rows(x)
        out = xr[N_GROUPS - 1]
        for gi in reversed(range(N_GROUPS - 1)):
            out = jnp.where(gsel == float(gi), xr[gi], out)
        return out

    p1 = jnp.exp(pick(m1) - gm)
    p2 = jnp.exp(pick(m2) - gm)
    w1 = p1 / (p1 + p2)
    w2 = p2 / (p1 + p2)
    s1, s2 = pick(i1), pick(i2)
    lo, hi = jnp.minimum(s1, s2), jnp.maximum(s1, s2)
    pair = jnp.where(lo == 0.0, hi - 1.0, jnp.where(lo == 1.0, hi + 1.0, 5.0))
    bucket = gsel * float(N_PAIRS) + pair
    first_is_lo = s1 < s2
    return bucket, jnp.where(first_is_lo, w1, w2), jnp.where(first_is_lo, w2, w1)


def _outproj_kernel(*refs, n_x, n_o, n_prompt_tiles):
    x_refs = refs[:n_x]
    o_refs = refs[n_x:n_x + n_o]
    w_refs = refs[n_x + n_o:n_x + 2 * n_o]
    (gate_ref, nf_ref, sh_ref, sc_ref, wr_ref, br_ref, tri_ref,
     xn_ref, disp_ref, meta_ref, cnt_ref, run_ref) = refs[n_x + 2 * n_o:]
    t = xn_ref.shape[0]

    @pl.when(pl.program_id(0) == 0)
    def _():
        run_ref[...] = jnp.zeros_like(run_ref)

    x = x_refs[0][...] if n_x == 1 else _load_tokens(x_refs[0], x_refs[1], n_prompt_tiles)
    y = _dot(o_refs[0][...], w_refs[0][...])
    for i in range(1, n_o):
        y = y + _dot(o_refs[i][...], w_refs[i][...])
    xn = x + _group_affine(y, gate_ref[...], None)
    xn_ref[...] = xn
    h = _norm_mod(xn, nf_ref[...], sh_ref[...], sc_ref[...])
    for s, slab in enumerate(_pack_pairs(h)):
        disp_ref[s] = slab
    logits = _dot3(h, wr_ref[...]) + br_ref[...]
    bucket, w_lo, w_hi = _route(logits.T[0:N_EXPERTS])
    r128 = lax.broadcasted_iota(jnp.int32, (128, t), 0)
    aux = jnp.where(r128 == 0, w_lo, jnp.where(r128 == 1, w_hi, 0.0)).T
    disp_ref[disp_ref.shape[0] - 1] = pltpu.bitcast(aux, jnp.int32)
    brow = lax.broadcasted_iota(jnp.int32, (BUCKET_ROWS, t), 0).astype(F32)
    onehot = jnp.where(brow == bucket, 1.0, 0.0)
    before = _dot(onehot.astype(BF16), tri_ref[...]) + run_ref[:, 0:1]
    rank = jnp.sum(onehot * before, axis=0, keepdims=True)
    run_ref[...] = run_ref[...] + jnp.sum(onehot, axis=1, keepdims=True)
    cnt_ref[...] = run_ref[...]
    r8 = lax.broadcasted_iota(jnp.int32, (8, t), 0)
    meta_ref[...] = jnp.where(r8 == 0, bucket, jnp.where(r8 == 1, rank, 0.0)).astype(jnp.int32)


def _outproj_call(xs_, os_, ws, mods, nf, wr, br, n_pad):
    d = xs_[0].shape[1]
    n = sum(a.shape[0] for a in xs_)
    npt = xs_[0].shape[0] // TM
    row = lambda i: (i, 0)
    const = lambda i: (0, 0)
    n_o = len(os_)
    in_specs = ((_token_specs(npt, d) if len(xs_) == 2 else [pl.BlockSpec((TM, d), row)])
                + [pl.BlockSpec((TM, o.shape[1]), row) for o in os_]
                + [pl.BlockSpec(w.shape, const) for w in ws]
                + [_mod_spec(GATE_MIX), pl.BlockSpec((1, d), const),
                   _mod_spec(SHIFT_FFN), _mod_spec(SCALE_FFN),
                   pl.BlockSpec(wr.shape, const), pl.BlockSpec(br.shape, const),
                   pl.BlockSpec((TM, TM), const)])
    tri = jnp.asarray(np.triu(np.ones((TM, TM), np.float32), k=1), dtype=BF16)
    return pl.pallas_call(
        functools.partial(_outproj_kernel, n_x=len(xs_), n_o=n_o, n_prompt_tiles=npt),
        grid=(n // TM,),
        in_specs=in_specs,
        out_specs=[pl.BlockSpec((TM, d), row), pl.BlockSpec((DISP_SLABS, TM, 128), lambda i: (0, i, 0)),
                   pl.BlockSpec((8, TM), lambda i: (0, i)), pl.BlockSpec((BUCKET_ROWS, 128), const)],
        out_shape=[jax.ShapeDtypeStruct((n, d), F32), jax.ShapeDtypeStruct((DISP_SLABS, n_pad, 128), jnp.int32),
                   jax.ShapeDtypeStruct((8, n), jnp.int32), jax.ShapeDtypeStruct((BUCKET_ROWS, 128), F32)],
        scratch_shapes=[pltpu.VMEM((BUCKET_ROWS, 128), F32)],
        compiler_params=_cparams(("arbitrary",)),
        name="outproj_router",
    )(*xs_, *os_, *ws, mods, nf, mods, mods, wr, br, tri)


def _sc_mesh():
    return plsc.VectorSubcoreMesh(core_axis_name="core", subcore_axis_name="subcore")


def _sc_row_copy(src, idx, n_out, scatter):
    r = idx.shape[0]
    k = SC_GROUP
    w_per = r // (SC_WINDOW * SC_WORKERS)
    assert r % (SC_WINDOW * SC_WORKERS) == 0 and w_per % k == 0
    n_groups = w_per // k

    @functools.partial(
        pl.kernel, out_type=jax.ShapeDtypeStruct((n_out, 128), src.dtype), mesh=_sc_mesh(),
        scratch_types=[pltpu.VMEM((w_per, SC_WINDOW), jnp.int32),
                       pltpu.VMEM((2 * k, SC_WINDOW, 128), src.dtype),
                       pltpu.SemaphoreType.DMA((2,)), pltpu.SemaphoreType.DMA((2,))])
    def copy(x_hbm, i_hbm, o_hbm, ibuf, xbuf, in_sem, out_sem):
        wid = lax.axis_index("core") * (SC_WORKERS // 2) + lax.axis_index("subcore")
        pltpu.sync_copy(i_hbm.at[wid], ibuf)
        first = wid * w_per

        def rows(j):
            return pl.ds((first + j) * SC_WINDOW, SC_WINDOW)

        def start_in(g, slot):
            cps = []
            for c in range(k):
                j = g * k + c
                s = x_hbm.at[rows(j)] if scatter else x_hbm.at[ibuf.at[j]]
                cps.append(pltpu.async_copy(s, xbuf.at[slot * k + c], in_sem.at[slot]))
            return cps

        def start_out(g, slot):
            cps = []
            for c in range(k):
                j = g * k + c
                dst = o_hbm.at[ibuf.at[j]] if scatter else o_hbm.at[rows(j)]
                cps.append(pltpu.async_copy(xbuf.at[slot * k + c], dst, out_sem.at[slot]))
            return cps

        pending_in = start_in(0, 0)
        for g in range(n_groups):
            slot = g % 2
            for cp in pending_in:
                cp.wait()
            pending_out = start_out(g, slot)
            if g + 1 < n_groups:
                pending_in = start_in(g + 1, 1 - slot)
            for cp in pending_out:
                cp.wait()

    return copy(src, idx.reshape(SC_WORKERS, w_per, SC_WINDOW))


def _sc_scatter_rows(src, idx, n_out):
    assert idx.shape == (src.shape[0],)
    return _sc_row_copy(src, idx, n_out, scatter=True)


def _sc_gather_rows(table, idx):
    return _sc_row_copy(table, idx, idx.shape[0], scatter=False)


def _moe_kernel(elo_ref, ehi_ref, valid_ref, xs_ref, gu_lo_ref, gu_hi_ref, dn_lo_ref, dn_hi_ref, y_ref):
    i = pl.program_id(0)

    @pl.when(valid_ref[i] == 1)
    def _():
        h = _unpack_pairs([xs_ref[s] for s in range(Y_SLABS)], BF16)
        aux = pltpu.bitcast(xs_ref[Y_SLABS], F32)
        acc = None
        for lane, gu_ref, dn_ref in ((0, gu_lo_ref, dn_lo_ref), (1, gu_hi_ref, dn_hi_ref)):
            ab = _dot(h, gu_ref[0, 0])
            a, b = ab[:, :D_FF], ab[:, D_FF:]
            y = aux[:, lane:lane + 1] * _dot((a * _sigmoid(a) * b).astype(BF16), dn_ref[0, 0])
            acc = y if acc is None else acc + y
        for s, slab in enumerate(_pack_pairs(acc)):
            y_ref[s] = slab

    @pl.when(valid_ref[i] == 0)
    def _():
        y_ref[...] = jnp.zeros_like(y_ref)


def _moe_call(xs, elo, ehi, valid, wgu, wdn, layer, n_tiles):
    d = wgu.shape[2]
    gu = lambda sel: pl.BlockSpec((1, 1, d, 2 * D_FF), lambda i, lo, hi, v: (layer, (lo, hi)[sel][i], 0, 0))
    dn = lambda sel: pl.BlockSpec((1, 1, D_FF, d), lambda i, lo, hi, v: (layer, (lo, hi)[sel][i], 0, 0))
    return pl.pallas_call(
        _moe_kernel,
        grid_spec=pltpu.PrefetchScalarGridSpec(
            num_scalar_prefetch=3,
            grid=(n_tiles,),
            in_specs=[pl.BlockSpec((DISP_SLABS, TMO, 128), lambda i, lo, hi, v: (0, i, 0)),
                      gu(0), gu(1), dn(0), dn(1)],
            out_specs=pl.BlockSpec((Y_SLABS, TMO, 128), lambda i, lo, hi, v: (0, i, 0))),
        out_shape=jax.ShapeDtypeStruct((Y_SLABS, n_tiles * TMO, 128), jnp.int32),
        compiler_params=_cparams(("arbitrary",)),
        name="moe_grouped",
    )(elo, ehi, valid, xs, wgu, wgu, wdn, wdn)


def _moe_layer(disp, meta, counts, wgu, wdn, layer, n, n_pad, sort_rows):
    n_tiles = sort_rows // TMO
    cnt = counts[:N_BUCKETS, 0].astype(jnp.int32)
    padded = ((cnt + TMO - 1) // TMO) * TMO
    ends = jnp.cumsum(padded)
    offs = ends - padded
    bucket, rank = meta[0], meta[1]
    pos = rank + jnp.sum(jnp.where(bucket[None, :] == jnp.arange(N_BUCKETS, dtype=jnp.int32)[:, None],
                                   offs[:, None], 0), axis=0)
    tile_start = jnp.arange(n_tiles, dtype=jnp.int32) * TMO
    tile_bucket = jnp.minimum(jnp.sum((tile_start[:, None] >= ends[None, :]).astype(jnp.int32), axis=1), N_BUCKETS - 1)
    valid = (tile_start < ends[-1]).astype(jnp.int32)
    pair_lo = np.array([0, 0, 0, 1, 1, 2], np.int32)
    pair_hi = np.array([1, 2, 3, 2, 3, 3], np.int32)
    b_lo = jnp.asarray(np.repeat(np.arange(N_GROUPS), N_PAIRS) * EXP_PER_GROUP + np.tile(pair_lo, N_GROUPS), jnp.int32)
    b_hi = jnp.asarray(np.repeat(np.arange(N_GROUPS), N_PAIRS) * EXP_PER_GROUP + np.tile(pair_hi, N_GROUPS), jnp.int32)
    onehot_tb = (tile_bucket[:, None] == jnp.arange(N_BUCKETS, dtype=jnp.int32)[None, :]).astype(jnp.int32)
    elo = jnp.sum(onehot_tb * b_lo[None, :], axis=1)
    ehi = jnp.sum(onehot_tb * b_hi[None, :], axis=1)
    dump = sort_rows + jnp.arange(n_pad - n, dtype=jnp.int32)
    pos_sc = jnp.concatenate([pos, dump])
    total = sort_rows + n_pad - n
    sc_idx = (pos_sc[None, :] + (jnp.arange(DISP_SLABS, dtype=jnp.int32) * total)[:, None]).reshape(-1)
    xs = _sc_scatter_rows(disp.reshape(DISP_SLABS * n_pad, 128), sc_idx, DISP_SLABS * total)
    ys = _moe_call(xs.reshape(DISP_SLABS, total, 128), elo, ehi, valid, wgu, wdn, layer, n_tiles)
    pos_g = jnp.concatenate([pos, jnp.zeros((n_pad - n,), jnp.int32)])
    g_idx = (pos_g[None, :] + (jnp.arange(Y_SLABS, dtype=jnp.int32) * sort_rows)[:, None]).reshape(-1)
    z = _sc_gather_rows(ys.reshape(Y_SLABS * sort_rows, 128), g_idx)
    return z.reshape(Y_SLABS, n_pad, 128)


def _final_kernel(xn_ref, z_ref, gate_ref, g_ref, yp_ref, ys_ref, *, n_prompt_tiles):
    x = _add_moe(xn_ref, z_ref, gate_ref)
    ms = jnp.mean(x * x, axis=-1, keepdims=True)
    y = x * lax.rsqrt(ms + EPS) * g_ref[...]
    i = pl.program_id(0)

    @pl.when(i < n_prompt_tiles)
    def _():
        yp_ref[...] = y

    @pl.when(i >= n_prompt_tiles)
    def _():
        ys_ref[...] = y


def _final_call(xn, z, mods, g, n_prompt):
    n, d = xn.shape
    npt = n_prompt // TM
    assert n - n_prompt == TM
    return pl.pallas_call(
        functools.partial(_final_kernel, n_prompt_tiles=npt),
        grid=(n // TM,),
        in_specs=[pl.BlockSpec((TM, d), lambda i: (i, 0)), pl.BlockSpec((z.shape[0], TM, 128), lambda i: (0, i, 0)),
                  _mod_spec(GATE_FFN), pl.BlockSpec((1, d), lambda i: (0, 0))],
        out_specs=_token_specs(npt, d),
        out_shape=[jax.ShapeDtypeStruct((n_prompt, d), F32), jax.ShapeDtypeStruct((TM, d), F32)],
        compiler_params=_cparams(("arbitrary",)),
        name="final_norm",
    )(xn, z, mods, g)


def kernel(x_prompt, x_sample, c_prompt, c_sample, state_gla, cache_band_k, cache_band_v, cache_swa_k, cache_swa_v,
           w_ada, b_ada, norm_mix, norm_ffn, norm_final, w_in_even, w_gate_a, b_gate_a, gla_norm, rel_bias_b,
           w_out_even, w_in_odd, sinks_c, w_out_odd, w_router, b_router, w_gate_up, w_down):
    bp, lp, d = x_prompt.shape
    bs, ls_, _ = x_sample.shape
    n_p, n_s = bp * lp, bs * ls_
    n = n_p + n_s
    assert ls_ == CHUNK and n_s == TM and lp % TM == 0 and PAST_LEN % CHUNK == 0

    xp2, xs2 = x_prompt.reshape(n_p, d), x_sample.reshape(n_s, d)

    c16 = jnp.zeros((SEQ_ROWS, d), F32).at[:bp].set(c_prompt).at[bp:bp + bs].set(c_sample)
    mods = _ada_call(c16, w_ada, b_ada)
    seq_of_group = np.concatenate([np.repeat(np.arange(bp), lp // CHUNK), bp + np.arange(bs)])
    mods_g = [mods[l][seq_of_group] for l in range(DEPTH)]

    perm = np.array([4 * (c % 4) + c // 4 for c in range(N_EXPERTS)])
    wr = jnp.zeros((d, 128), F32).at[:, :N_EXPERTS].set(w_router[:, perm])
    br = jnp.zeros((1, 128), F32).at[0, :N_EXPERTS].set(b_router[perm])

    wgu = w_gate_up.astype(BF16)
    wdn = w_down.astype(BF16)

    sc_unit = SC_WINDOW * SC_WORKERS
    n_pad = -(-n // sc_unit) * sc_unit
    sort_rows = -(-(n + N_BUCKETS * TMO) // sc_unit) * sc_unit

    gla_p = gla_s = bk_p = bv_p = bk_s = bv_s = sk_p = sv_p = sk_s = sv_s = None
    xn = z = None
    for l in range(DEPTH):
        i = l // 2
        if l % 2 == 0:
            w = w_in_even[i]
            w_main = jnp.concatenate([w[:, :1536], w[:, 1552:]], axis=1).astype(BF16)
            w_la = jnp.zeros((d, 128), F32).at[:, :GATE_RANK].set(w[:, 1536:1552]).astype(BF16)
            w_gate = jnp.zeros((128, HA * DKA), F32).at[:GATE_RANK].set(w_gate_a[i])
            qa, ka, va, ra, qb, kb, vb, ga = _inproj_even_call(
                xp2, xs2, mods_g[l], norm_mix[l][None], w_main, w_la, w_gate, b_gate_a[i][None])
            xres = [xp2, xs2]
            gn = gla_norm[i][None]
            oa, s_p = _gla_call(qa, ka, va, ga, ra, jnp.zeros((bp, 256, 128), F32), gn, None,
                                n_seq=bp, seq_rows=lp, row0=0, nb=8)
            oa, s_s = _gla_call(qa, ka, va, ga, ra, state_gla[i].reshape(bs, 256, 128), gn, oa,
                                n_seq=bs, seq_rows=ls_, row0=n_p, nb=1)
            gla_p, gla_s = s_p.reshape(1, bp, HA, DKA, DVA), s_s.reshape(1, bs, HA, DKA, DVA)
            pb = N_PREV_B * CHUNK
            tq, g = 512, 2
            bps = lp // tq
            bias = _band_bias(rel_bias_b[i], g, pb)
            ob = _attn_call(
                functools.partial(_band_kernel, g=g, n_sub=tq // (CHUNK * g), pb=pb, blocks_per_seq=bps, mask_first=True),
                qb, kb, kb, vb, vb, [bias], [pl.BlockSpec(bias.shape, lambda i_: (0, 0, 0))], None,
                width=512, kv_width=512, tq=tq, pb=pb, n_blocks=n_p // tq, blk0=0,
                prev_map=lambda i_: (jnp.maximum(i_ * (tq // pb) - 1, 0), 0), name="band_prompt")
            ck = cache_band_k[i].reshape(bs * pb, HB * DHB).astype(BF16)
            cv = cache_band_v[i].reshape(bs * pb, HB * DHB).astype(BF16)
            bias1 = _band_bias(rel_bias_b[i], 1, pb)
            ob = _attn_call(
                functools.partial(_band_kernel, g=1, n_sub=1, pb=pb, blocks_per_seq=1, mask_first=False),
                qb, ck, kb, cv, vb, [bias1], [pl.BlockSpec(bias1.shape, lambda i_: (0, 0, 0))], ob,
                width=512, kv_width=512, tq=CHUNK, pb=pb, n_blocks=bs, blk0=n_p // CHUNK,
                prev_map=lambda i_: (i_, 0), name="band_sample")
            tail = lambda a: jnp.stack([a[(b + 1) * lp - pb:(b + 1) * lp] for b in range(bp)]).astype(F32).reshape(1, bp, pb, HB, DHB)
            new = lambda a: a[n_p:].astype(F32).reshape(bs, ls_, HB, DHB)
            bk_p, bv_p = tail(kb), tail(vb)
            bk_s = jnp.concatenate([cache_band_k[i][:, ls_:], new(kb)], axis=1)[None]
            bv_s = jnp.concatenate([cache_band_v[i][:, ls_:], new(vb)], axis=1)[None]
            wo = w_out_even[i].astype(BF16)
            os_, ws = [oa, ob], [wo[:HA * DVA], wo[HA * DVA:]]
        else:
            w = w_in_odd[i]
            wk, wv = w[:, 1024:1152], w[:, 1152:1280]
            dup = lambda a: jnp.concatenate([a[:, :64], a[:, :64], a[:, 64:], a[:, 64:]], axis=1)
            w_all = jnp.concatenate([w[:, :1024], dup(wk), dup(wv)], axis=1).astype(BF16)
            cos, sin, rope_map = _rope_tables(lp, ls_, bp, bs)
            x, q, k, v = _inproj_odd_call(xn, z, mods_g[l - 1], mods_g[l], norm_mix[l][None], cos, sin, rope_map, w_all)
            xres = [x]
            pb = WINDOW
            tq, g = 512, 2
            bps = lp // tq
            sink = sinks_c[i][None]
            sink_spec = [pl.BlockSpec(memory_space=pltpu.SMEM)]
            o = _attn_call(
                functools.partial(_swa_kernel, g=g, n_sub=tq // (CHUNK * g), pb=pb, blocks_per_seq=bps, mask_first=True),
                q, k, k, v, v, [sink], sink_spec, None,
                width=1024, kv_width=256, tq=tq, pb=pb, n_blocks=n_p // tq, blk0=0,
                prev_map=lambda i_: (jnp.maximum(i_ * (tq // pb) - 1, 0), 0), name="swa_prompt")
            dupc = lambda c: jnp.concatenate([c[:, :, 0], c[:, :, 0], c[:, :, 1], c[:, :, 1]], axis=-1).reshape(bs * pb, 256).astype(BF16)
            ck, cv = dupc(cache_swa_k[i]), dupc(cache_swa_v[i])
            o = _attn_call(
                functools.partial(_swa_kernel, g=1, n_sub=1, pb=pb, blocks_per_seq=1, mask_first=False),
                q, ck, k, cv, v, [sink], sink_spec, o,
                width=1024, kv_width=256, tq=CHUNK, pb=pb, n_blocks=bs, blk0=n_p // CHUNK,
                prev_map=lambda i_: (i_, 0), name="swa_sample")
            undup = lambda a: jnp.concatenate([a[:, 0:64], a[:, 128:192]], axis=1).astype(F32)
            tail = lambda a: jnp.stack([undup(a[(b + 1) * lp - pb:(b + 1) * lp]) for b in range(bp)]).reshape(1, bp, pb, KVC, DHC)
            new = lambda a: undup(a[n_p:]).reshape(bs, ls_, KVC, DHC)
            sk_p, sv_p = tail(k), tail(v)
            sk_s = jnp.concatenate([cache_swa_k[i][:, ls_:], new(k)], axis=1)[None]
            sv_s = jnp.concatenate([cache_swa_v[i][:, ls_:], new(v)], axis=1)[None]
            os_, ws = [o], [w_out_odd[i].astype(BF16)]
        xn, disp, meta, counts = _outproj_call(xres, os_, ws, mods_g[l], norm_ffn[l][None], wr, br, n_pad)
        z = _moe_layer(disp, meta, counts, wgu, wdn, l, n, n_pad, sort_rows)

    y_prompt, y_sample = _final_call(xn, z, mods_g[DEPTH - 1], norm_final[None], n_p)
    return (y_prompt.reshape(bp, lp, d), y_sample.reshape(bs, ls_, d),
            gla_p, gla_s, bk_p, bv_p, bk_s, bv_s, sk_p, sv_p, sk_s, sv_s)
```

```python
import functools

import numpy as np
import jax
import jax.numpy as jnp
from jax import lax
from jax.experimental import pallas as pl
from jax.experimental.pallas import tpu as pltpu
from jax.experimental.pallas import tpu_sc as plsc

F32 = jnp.float32
BF16 = jnp.bfloat16

D_MODEL = 1024
DEPTH = 2
CHUNK = 64
PAST_LEN = 4096
HA, DKA, DVA = 4, 64, 128
GATE_RANK = 16
GATE_TAU = 16.0
HB, DHB = 8, 64
N_PREV_B = 8
MAX_REL = 128
HC, KVC, DHC = 16, 2, 64
WINDOW = 128
ROPE_THETA = 10000.0
N_EXPERTS = 16
N_GROUPS = 4
EXP_PER_GROUP = 4
D_FF = 512
EPS = 1e-6

N_PAIRS = 6
N_BUCKETS = N_GROUPS * N_PAIRS
BUCKET_ROWS = 32
Y_SLABS = 4
DISP_SLABS = Y_SLABS + 1
TMO = 256
SC_WINDOW = 128
SC_WORKERS = 32
SC_GROUP = 3

TM = 512
SEQ_ROWS = 16
SUB = 16
NEG = -1e30


def _cparams(sem, vmem_mb=48):
    return pltpu.CompilerParams(dimension_semantics=sem, vmem_limit_bytes=vmem_mb * 1024 * 1024)


def _dot(a, b):
    return jnp.dot(a, b, preferred_element_type=F32)


def _dot_nt(a, b):
    return lax.dot_general(a, b, (((1,), (1,)), ((), ())), preferred_element_type=F32)


def _split(a):
    hi = a.astype(BF16)
    lo = (a - hi.astype(F32)).astype(BF16)
    return hi, lo


def _dot3(a, b):
    ah, al = _split(a)
    bh, bl = _split(b)
    return _dot(ah, bh) + _dot(ah, bl) + _dot(al, bh)


def _sigmoid(x):
    return 1.0 / (1.0 + jnp.exp(-x))


def _group_affine(y, mul, add):
    parts = []
    for gi in range(y.shape[0] // CHUNK):
        p = y[gi * CHUNK:(gi + 1) * CHUNK]
        if mul is not None:
            p = p * mul[gi:gi + 1]
        if add is not None:
            p = p + add[gi:gi + 1]
        parts.append(p)
    return jnp.concatenate(parts, axis=0)


def _norm_mod(x, g, shift, scale):
    ms = jnp.mean(x * x, axis=-1, keepdims=True)
    return _group_affine(x * lax.rsqrt(ms + EPS) * g, 1.0 + scale, shift)


def _mod_spec(part):
    return pl.BlockSpec((TM // CHUNK, D_MODEL), lambda i: (i, part))


SHIFT_MIX, SCALE_MIX, GATE_MIX, SHIFT_FFN, SCALE_FFN, GATE_FFN = range(6)


def _on_token_tile(xp_ref, xs_ref, n_prompt_tiles, body):
    @pl.when(pl.program_id(0) < n_prompt_tiles)
    def _():
        body(xp_ref)

    @pl.when(pl.program_id(0) >= n_prompt_tiles)
    def _():
        body(xs_ref)


def _token_specs(n_prompt_tiles, d):
    return [pl.BlockSpec((TM, d), lambda i: (jnp.minimum(i, n_prompt_tiles - 1), 0)),
            pl.BlockSpec((TM, d), lambda i: (0, 0))]


def _ada_kernel(c_ref, w_ref, b_ref, o_ref):
    c = c_ref[...]
    o_ref[0] = _dot3(c * _sigmoid(c), w_ref[0]) + b_ref[0]


def _ada_call(c16, w_ada, b_ada):
    d = D_MODEL
    tn = 1024
    return pl.pallas_call(
        _ada_kernel,
        grid=(DEPTH, 6 * d // tn),
        in_specs=[pl.BlockSpec((SEQ_ROWS, d), lambda l, j: (0, 0)),
                  pl.BlockSpec((1, d, tn), lambda l, j: (l, 0, j)),
                  pl.BlockSpec((1, 1, tn), lambda l, j: (l, 0, j))],
        out_specs=pl.BlockSpec((1, SEQ_ROWS, tn), lambda l, j: (l, 0, j)),
        out_shape=jax.ShapeDtypeStruct((DEPTH, SEQ_ROWS, 6 * d), F32),
        compiler_params=_cparams(("arbitrary", "arbitrary")),
        name="ada",
    )(c16, w_ada, b_ada.reshape(DEPTH, 1, 6 * d))


def _inproj_even_kernel(xp_ref, xs_ref, sh_ref, sc_ref, g_ref, w_ref, wla_ref, wg_ref, bg_ref,
                        qa_ref, ka_ref, va_ref, ra_ref, qb_ref, kb_ref, vb_ref, ga_ref, *, n_prompt_tiles):
    def body(x_ref):
        hb = _norm_mod(x_ref[...], g_ref[...], sh_ref[...], sc_ref[...]).astype(BF16)
        qa_ref[...] = (_dot(hb, w_ref[:, 0:256]) * (DKA ** -0.5)).astype(BF16)
        ka_ref[...] = _dot(hb, w_ref[:, 256:512]).astype(BF16)
        va_ref[...] = _dot(hb, w_ref[:, 512:1024]).astype(BF16)
        ra_ref[...] = _dot(hb, w_ref[:, 1024:1536]).astype(BF16)
        qb_ref[...] = (_dot(hb, w_ref[:, 1536:2048]) * (DHB ** -0.5)).astype(BF16)
        kb_ref[...] = _dot(hb, w_ref[:, 2048:2560]).astype(BF16)
        vb_ref[...] = _dot(hb, w_ref[:, 2560:3072]).astype(BF16)
        la = _dot(hb, wla_ref[...])
        gl = _dot3(la, wg_ref[...]) + bg_ref[...]
        ga_ref[...] = -(jnp.maximum(-gl, 0.0) + jnp.log(1.0 + jnp.exp(-jnp.abs(gl)))) * (1.0 / GATE_TAU)

    _on_token_tile(xp_ref, xs_ref, n_prompt_tiles, body)


def _inproj_even_call(xp, xs, mods, g, w_main, w_la, w_gate, b_gate):
    d = xp.shape[1]
    npt = xp.shape[0] // TM
    n = xp.shape[0] + xs.shape[0]
    row = lambda i: (i, 0)
    const = lambda i: (0, 0)
    widths = (256, 256, 512, 512, 512, 512, 512)
    out_shape = [jax.ShapeDtypeStruct((n, w), BF16) for w in widths] + [jax.ShapeDtypeStruct((n, 256), F32)]
    out_specs = [pl.BlockSpec((TM, w), row) for w in widths] + [pl.BlockSpec((TM, 256), row)]
    return pl.pallas_call(
        functools.partial(_inproj_even_kernel, n_prompt_tiles=npt),
        grid=(n // TM,),
        in_specs=_token_specs(npt, d) + [
            _mod_spec(SHIFT_MIX), _mod_spec(SCALE_MIX),
            pl.BlockSpec((1, d), const),
            pl.BlockSpec(w_main.shape, const), pl.BlockSpec(w_la.shape, const),
            pl.BlockSpec(w_gate.shape, const), pl.BlockSpec(b_gate.shape, const)],
        out_specs=out_specs, out_shape=out_shape,
        compiler_params=_cparams(("parallel",)),
        name="inproj_even",
    )(xp, xs, mods, mods, g, w_main, w_la, w_gate, b_gate)


def _rope(x, cos, sin_signed):
    t, w = x.shape
    lane = lax.broadcasted_iota(jnp.int32, (1, w), 1)
    first_half = (lane & 63) < 32
    rot = jnp.where(first_half, pltpu.roll(x, w - 32, 1), pltpu.roll(x, 32, 1))
    reps = w // 128
    return x * jnp.tile(cos, (1, reps)) + rot * jnp.tile(sin_signed, (1, reps))


def _unpack_pairs(slabs, dtype):
    lo = [pltpu.bitcast(s << 16, F32) for s in slabs]
    hi = [pltpu.bitcast(s & jnp.int32(-65536), F32) for s in slabs]
    return jnp.concatenate(lo + hi, axis=1).astype(dtype)


def _pack_pairs(x):
    bits = pltpu.bitcast(x.astype(BF16).astype(F32), jnp.int32)
    half = x.shape[1] // 2
    packed = ((bits[:, :half] >> 16) & jnp.int32(0xFFFF)) | (bits[:, half:] & jnp.int32(-65536))
    return [packed[:, 128 * s:128 * (s + 1)] for s in range(half // 128)]


def _add_moe(xn_ref, z_ref, gate_ref):
    y = _unpack_pairs([z_ref[s] for s in range(z_ref.shape[0])], F32)
    return xn_ref[...] + _group_affine(y, gate_ref[...], None)


def _rope_tables(lp, ls_, bp, bs):
    assert PAST_LEN + ls_ <= lp and lp % 128 == 0 and bs * ls_ == TM
    half = DHC // 2
    inv = ROPE_THETA ** (-jnp.arange(half, dtype=F32) / half)
    inv = jnp.tile(inv, 128 // half)
    sign = jnp.asarray(np.tile(np.repeat([-1.0, 1.0], half), 128 // DHC), F32)
    a = jnp.asarray(np.arange(lp // 128) * 128, F32)[:, None] * inv[None, :]
    b = jnp.asarray(np.arange(128), F32)[:, None] * inv[None, :]
    ca, sa, cb, sb = jnp.cos(a)[:, None], jnp.sin(a)[:, None], jnp.cos(b)[None], jnp.sin(b)[None]
    cos = (ca * cb - sa * sb).reshape(lp, 128)
    sin = ((sa * cb + ca * sb) * sign).reshape(lp, 128)
    with_sample = lambda t: jnp.concatenate([t, jnp.tile(t[PAST_LEN:PAST_LEN + ls_], (bs, 1))], axis=0)
    tiles = lp // TM
    return with_sample(cos), with_sample(sin), lambda i: (jnp.where(i < bp * tiles, i % tiles, tiles), 0)


def _inproj_odd_kernel(xn_ref, z_ref, gate_ref, sh_ref, sc_ref, g_ref, cos_ref, sin_ref, w_ref,
                       x_ref, q_ref, k_ref, v_ref):
    x = _add_moe(xn_ref, z_ref, gate_ref)
    x_ref[...] = x
    hb = _norm_mod(x, g_ref[...], sh_ref[...], sc_ref[...]).astype(BF16)
    cos, sin = cos_ref[...], sin_ref[...]
    q = _rope(_dot(hb, w_ref[:, 0:1024]), cos, sin)
    q_ref[...] = (q * (DHC ** -0.5)).astype(BF16)
    k_ref[...] = _rope(_dot(hb, w_ref[:, 1024:1280]), cos, sin).astype(BF16)
    v_ref[...] = _dot(hb, w_ref[:, 1280:1536]).astype(BF16)


def _inproj_odd_call(xn, z, mods_prev, mods, g, cos, sin, rope_map, w):
    n, d = xn.shape
    row = lambda i: (i, 0)
    const = lambda i: (0, 0)
    widths = (1024, 256, 256)
    return pl.pallas_call(
        _inproj_odd_kernel,
        grid=(n // TM,),
        in_specs=[pl.BlockSpec((TM, d), row), pl.BlockSpec((z.shape[0], TM, 128), lambda i: (0, i, 0)),
                  _mod_spec(GATE_FFN), _mod_spec(SHIFT_MIX), _mod_spec(SCALE_MIX),
                  pl.BlockSpec((1, d), const),
                  pl.BlockSpec((TM, 128), rope_map), pl.BlockSpec((TM, 128), rope_map),
                  pl.BlockSpec(w.shape, const)],
        out_specs=[pl.BlockSpec((TM, d), row)] + [pl.BlockSpec((TM, wd), row) for wd in widths],
        out_shape=[jax.ShapeDtypeStruct((n, d), F32)] + [jax.ShapeDtypeStruct((n, wd), BF16) for wd in widths],
        compiler_params=_cparams(("parallel",)),
        name="inproj_odd",
    )(xn, z, mods_prev, mods, mods, g, cos, sin, w)


def _gla_tri():
    t = np.arange(CHUNK)[:, None]
    s = np.arange(CHUNK)[None, :]
    cum = s <= t
    start = s < (t // SUB) * SUB
    end = s < (t // SUB + 1) * SUB
    return jnp.asarray(np.concatenate([cum, start, end], axis=0).astype(np.float32), dtype=BF16)


def _gla_kernel(q_ref, k_ref, v_ref, g_ref, r_ref, s0_ref, gn_ref, tri_ref, o_ref, sout_ref, s_ref, *, nb):
    c_ = CHUNK
    nsub = c_ // SUB

    @pl.when(pl.program_id(1) == 0)
    def _():
        s_ref[...] = s0_ref[0]

    tri = tri_ref[...]
    lane = lax.broadcasted_iota(jnp.int32, (1, 128), 1)
    hmask = [jnp.where(lane < DKA, 1.0, 0.0), jnp.where(lane >= DKA, 1.0, 0.0)]
    ti = lax.broadcasted_iota(jnp.int32, (c_, c_), 0)
    si = lax.broadcasted_iota(jnp.int32, (c_, c_), 1)
    rb, cb = ti >> 4, si >> 4
    m_diag = (rb == cb) & (si <= ti)
    m_off = [(cb == j) & (rb > j) for j in range(nsub - 1)]
    hk = HA * DKA
    eye = lax.broadcasted_iota(jnp.int32, (hk, hk), 0) == lax.broadcasted_iota(jnp.int32, (hk, hk), 1)
    gn = gn_ref[...]

    for c in range(nb):
        rows = slice(c * c_, (c + 1) * c_)
        g_hi, g_lo = _split(g_ref[rows, :])
        cs = _dot(tri, g_hi) + _dot(tri, g_lo)
        b, rs, re = cs[0:c_], cs[c_:2 * c_], cs[2 * c_:3 * c_]
        q = q_ref[rows, :].astype(F32)
        k = k_ref[rows, :].astype(F32)
        bl = b[c_ - 1:c_, :]
        qd = q * jnp.exp(b - rs)
        kd = k * jnp.exp(rs - b)
        ke = k * jnp.exp(re - b)
        qi = q * jnp.exp(b)
        kl = k * jnp.exp(bl - b)
        ql = [q * jnp.exp(jnp.minimum(b - b[SUB * (j + 1) - 1:SUB * (j + 1), :], 0.0)) for j in range(nsub - 1)]
        dcol = jnp.sum(jnp.where(eye, jnp.broadcast_to(jnp.exp(bl), (hk, hk)), 0.0), axis=1, keepdims=True)
        s_old = s_ref[...]
        s_old_b = s_old.astype(BF16)
        upd = []
        for p in range(HA // 2):
            ls = slice(128 * p, 128 * (p + 1))
            kd_p = kd[:, ls].astype(BF16)
            ke_p = ke[:, ls].astype(BF16)
            klt = kl[:, ls].T
            sp_b = s_old_b[ls, :]
            for hh in range(2):
                h = 2 * p + hh
                msk = hmask[hh]
                a1 = _dot_nt((qd[:, ls] * msk).astype(BF16), kd_p)
                lhs2 = jnp.concatenate([ql[j][:, ls] * msk for j in range(nsub - 1)], axis=0).astype(BF16)
                a2 = _dot_nt(lhs2, ke_p)
                att = jnp.zeros((c_, c_), F32)
                for j in reversed(range(nsub - 1)):
                    att = jnp.where(m_off[j], a2[j * c_:(j + 1) * c_], att)
                att = jnp.where(m_diag, a1, att)
                vs = slice(DVA * h, DVA * (h + 1))
                v_h = v_ref[rows, vs]
                o = _dot(att.astype(BF16), v_h) + _dot((qi[:, ls] * msk).astype(BF16), sp_b)
                ms = jnp.mean(o * o, axis=-1, keepdims=True)
                rr = r_ref[rows, vs].astype(F32)
                o_ref[rows, vs] = (o * lax.rsqrt(ms + EPS) * gn * (rr * _sigmoid(rr))).astype(BF16)
                upd.append(_dot(klt[DKA * hh:DKA * (hh + 1)].astype(BF16), v_h))
        s_ref[...] = dcol * s_old + jnp.concatenate(upd, axis=0)
    sout_ref[0] = s_ref[...]


def _gla_call(q, k, v, g, r, s0, gn, o_prev, *, n_seq, seq_rows, row0, nb):
    tq = nb * CHUNK
    steps = seq_rows // tq
    blk0 = row0 // tq
    row = lambda b, j: (blk0 + b * steps + j, 0)
    const = lambda b, j: (0, 0)
    tri = _gla_tri()
    in_specs = [pl.BlockSpec((tq, 256), row), pl.BlockSpec((tq, 256), row), pl.BlockSpec((tq, 512), row),
                pl.BlockSpec((tq, 256), row), pl.BlockSpec((tq, 512), row),
                pl.BlockSpec((1, 256, 128), lambda b, j: (b, 0, 0)),
                pl.BlockSpec((1, 128), const), pl.BlockSpec(tri.shape, const)]
    args = [q, k, v, g, r, s0, gn, tri]
    aliases = {}
    if o_prev is not None:
        in_specs.append(pl.BlockSpec(memory_space=pl.ANY))
        args.append(o_prev)
        aliases = {len(args) - 1: 0}
    kern = functools.partial(_gla_kernel, nb=nb)
    if o_prev is not None:
        kern = _drop_arg(kern, 8)
    return pl.pallas_call(
        kern,
        grid=(n_seq, steps),
        in_specs=in_specs,
        out_specs=[pl.BlockSpec((tq, 512), row), pl.BlockSpec((1, 256, 128), lambda b, j: (b, 0, 0))],
        out_shape=[jax.ShapeDtypeStruct((q.shape[0], 512), BF16), jax.ShapeDtypeStruct((n_seq, 256, 128), F32)],
        scratch_shapes=[pltpu.VMEM((256, 128), F32)],
        input_output_aliases=aliases,
        compiler_params=_cparams(("arbitrary", "arbitrary")),
        name="gla",
    )(*args)


def _drop_arg(fn, idx):
    def wrapped(*refs):
        return fn(*refs[:idx], *refs[idx + 1:])
    return wrapped


def _window(prev_ref, cur_ref, lo, hi, pb, ls):
    if lo < pb:
        return jnp.concatenate([prev_ref[lo:pb, ls], cur_ref[0:hi - pb, ls]], axis=0)
    return cur_ref[lo - pb:hi - pb, ls]


def _band_kernel(q_ref, kp_ref, kc_ref, vp_ref, vc_ref, bias_ref, o_ref, *, g, n_sub, pb, blocks_per_seq, mask_first):
    qs = CHUNK * g
    kw_rows = pb + qs
    row = lax.broadcasted_iota(jnp.int32, (qs, kw_rows), 0)
    col = lax.broadcasted_iota(jnp.int32, (qs, kw_rows), 1)
    dd = (col >> 6) - (row >> 6)
    band = (dd >= 0) & (dd <= pb // CHUNK)
    lane = lax.broadcasted_iota(jnp.int32, (1, 128), 1)
    low = lane < DHB
    hmask = [jnp.where(low, 1.0, 0.0), jnp.where(low, 0.0, 1.0)]
    first = (pl.program_id(0) % blocks_per_seq) == 0
    for s in range(n_sub):
        valid = band
        if mask_first:
            valid = band & (col >= jnp.where(first, pb - qs * s, 0))
        for p in range(HB // 2):
            ls = slice(128 * p, 128 * (p + 1))
            qp = q_ref[qs * s:qs * (s + 1), ls].astype(F32)
            kw = _window(kp_ref, kc_ref, qs * s, qs * s + kw_rows, pb, ls)
            vw = _window(vp_ref, vc_ref, qs * s, qs * s + kw_rows, pb, ls)
            outs = []
            for hh in range(2):
                sc = _dot_nt((qp * hmask[hh]).astype(BF16), kw)
                sc = jnp.where(valid, sc + bias_ref[2 * p + hh], NEG)
                m = jnp.max(sc, axis=-1, keepdims=True)
                pe = jnp.exp(sc - m)
                l = jnp.sum(pe, axis=-1, keepdims=True)
                outs.append(_dot(pe.astype(BF16), vw) / l)
            o_ref[qs * s:qs * (s + 1), ls] = jnp.where(low, outs[0], outs[1]).astype(BF16)


def _band_bias(table, g, pb):
    rows, kw = CHUNK * g, pb + CHUNK * g
    period = kw + rows
    m = np.arange(period)
    m = np.where(m < kw, m, m - period)
    ext = table[:, np.clip(m - pb, -MAX_REL, MAX_REL) + MAX_REL]
    flat = jnp.tile(ext, (1, rows))[:, :rows * (period - 1)]
    return flat.reshape(table.shape[0], rows, period - 1)[:, :, :kw]


def _attn_call(kernel, q, kp, kc, vp, vc, extra, extra_specs, o_prev, *, width, kv_width, tq, pb,
               n_blocks, blk0, prev_map, name, n_prefetch=0):
    row = lambda i, *_: (blk0 + i, 0)
    in_specs = [pl.BlockSpec((tq, width), row),
                pl.BlockSpec((pb, kv_width), prev_map), pl.BlockSpec((tq, kv_width), row),
                pl.BlockSpec((pb, kv_width), prev_map), pl.BlockSpec((tq, kv_width), row)] + extra_specs
    args = [q, kp, kc, vp, vc] + extra
    aliases = {}
    if o_prev is not None:
        in_specs.append(pl.BlockSpec(memory_space=pl.ANY))
        args.append(o_prev)
        aliases = {len(args) - 1: 0}
        kernel = _drop_arg(kernel, len(args) - 1)
    return pl.pallas_call(
        kernel,
        grid=(n_blocks,),
        in_specs=in_specs,
        out_specs=pl.BlockSpec((tq, width), row),
        out_shape=jax.ShapeDtypeStruct((q.shape[0], width), BF16),
        input_output_aliases=aliases,
        compiler_params=_cparams(("parallel",)),
        name=name,
    )(*args)


def _swa_kernel(q_ref, kp_ref, kc_ref, vp_ref, vc_ref, sink_ref, o_ref, *, g, n_sub, pb, blocks_per_seq, mask_first):
    qs = CHUNK * g
    kw_rows = pb + qs
    row = lax.broadcasted_iota(jnp.int32, (qs, kw_rows), 0)
    col = lax.broadcasted_iota(jnp.int32, (qs, kw_rows), 1)
    dd = (col >> 6) - (row >> 6)
    band = (dd >= 0) & (dd <= pb // CHUNK)
    lane = lax.broadcasted_iota(jnp.int32, (1, 128), 1)
    low = lane < DHC
    hmask = [jnp.where(low, 1.0, 0.0), jnp.where(low, 0.0, 1.0)]
    first = (pl.program_id(0) % blocks_per_seq) == 0
    pairs_per_kv = HC // KVC // 2
    for s in range(n_sub):
        valid = band
        if mask_first:
            valid = band & (col >= jnp.where(first, pb - qs * s, 0))
        for kv in range(KVC):
            kvs = slice(128 * kv, 128 * (kv + 1))
            kw = _window(kp_ref, kc_ref, qs * s, qs * s + kw_rows, pb, kvs)
            vw = _window(vp_ref, vc_ref, qs * s, qs * s + kw_rows, pb, kvs)
            for jj in range(pairs_per_kv):
                j = kv * pairs_per_kv + jj
                ls = slice(128 * j, 128 * (j + 1))
                qp = q_ref[qs * s:qs * (s + 1), ls].astype(F32)
                outs = []
                for hh in range(2):
                    sc = _dot_nt((qp * hmask[hh]).astype(BF16), kw)
                    sc = jnp.where(valid, sc, NEG)
                    sk = sink_ref[0, 2 * j + hh]
                    m = jnp.maximum(jnp.max(sc, axis=-1, keepdims=True), sk)
                    pe = jnp.exp(sc - m)
                    l = jnp.sum(pe, axis=-1, keepdims=True) + jnp.exp(sk - m)
                    outs.append(_dot(pe.astype(BF16), vw) / l)
                o_ref[qs * s:qs * (s + 1), ls] = jnp.where(low, outs[0], outs[1]).astype(BF16)


def _route(logits_t):
    a = [logits_t[4 * j:4 * j + 4] for j in range(EXP_PER_GROUP)]

    def first_argmax(vals, m):
        idx = jnp.full(m.shape, float(len(vals) - 1), F32)
        for j in reversed(range(len(vals) - 1)):
            idx = jnp.where(vals[j] == m, float(j), idx)
        return idx

    m1 = functools.reduce(jnp.maximum, a)
    i1 = first_argmax(a, m1)
    bsec = [jnp.where(i1 == float(j), -jnp.inf, a[j]) for j in range(EXP_PER_GROUP)]
    m2 = functools.reduce(jnp.maximum, bsec)
    i2 = first_argmax(bsec, m2)
    rows = lambda x: [x[gi:gi + 1] for gi in range(N_GROUPS)]
    gm = functools.reduce(jnp.maximum, rows(m1))
    gscore = jnp.exp(m1 - gm) + jnp.exp(m2 - gm)
    gs = rows(gscore)
    gsel = first_argmax(gs, functools.reduce(jnp.maximum, gs))

    def pick(x):
        xr = rows(x)
        out = xr[N_GROUPS - 1]
        for gi in reversed(range(N_GROUPS - 1)):
            out = jnp.where(gsel == float(gi), xr[gi], out)
        return out

    p1 = jnp.exp(pick(m1) - gm)
    p2 = jnp.exp(pick(m2) - gm)
    w1 = p1 / (p1 + p2)
    w2 = p2 / (p1 + p2)
    s1, s2 = pick(i1), pick(i2)
    lo, hi = jnp.minimum(s1, s2), jnp.maximum(s1, s2)
    pair = jnp.where(lo == 0.0, hi - 1.0, jnp.where(lo == 1.0, hi + 1.0, 5.0))
    bucket = gsel * float(N_PAIRS) + pair
    first_is_lo = s1 < s2
    return bucket, jnp.where(first_is_lo, w1, w2), jnp.where(first_is_lo, w2, w1)


def _outproj_kernel(*refs, n_x, n_o, n_prompt_tiles):
    x_refs = refs[:n_x]
    o_refs = refs[n_x:n_x + n_o]
    w_refs = refs[n_x + n_o:n_x + 2 * n_o]
    (gate_ref, nf_ref, sh_ref, sc_ref, wr_ref, br_ref, tri_ref,
     xn_ref, disp_ref, meta_ref, cnt_ref, run_ref) = refs[n_x + 2 * n_o:]
    t = xn_ref.shape[0]

    @pl.when(pl.program_id(0) == 0)
    def _():
        run_ref[...] = jnp.zeros_like(run_ref)

    y = _dot(o_refs[0][...], w_refs[0][...])
    for i in range(1, n_o):
        y = y + _dot(o_refs[i][...], w_refs[i][...])
    gy = _group_affine(y, gate_ref[...], None)

    def residual(x_ref):
        xn_ref[...] = x_ref[...] + gy

    if n_x == 1:
        residual(x_refs[0])
    else:
        _on_token_tile(x_refs[0], x_refs[1], n_prompt_tiles, residual)
    xn = xn_ref[...]
    h = _norm_mod(xn, nf_ref[...], sh_ref[...], sc_ref[...])
    for s, slab in enumerate(_pack_pairs(h)):
        disp_ref[s] = slab
    logits = _dot3(h, wr_ref[...]) + br_ref[...]
    bucket, w_lo, w_hi = _route(logits.T[0:N_EXPERTS])
    r128 = lax.broadcasted_iota(jnp.int32, (128, t), 0)
    tok = (pl.program_id(0) * t + lax.broadcasted_iota(jnp.int32, (1, t), 1)).astype(F32)
    aux = jnp.where(r128 == 0, w_lo, jnp.where(r128 == 1, w_hi, jnp.where(r128 == 2, tok, 0.0))).T
    disp_ref[disp_ref.shape[0] - 1] = pltpu.bitcast(aux, jnp.int32)
    brow = lax.broadcasted_iota(jnp.int32, (BUCKET_ROWS, t), 0).astype(F32)
    onehot = jnp.where(brow == bucket, 1.0, 0.0)
    before = _dot(onehot.astype(BF16), tri_ref[...]) + run_ref[:, 0:1]
    rank = jnp.sum(onehot * before, axis=0, keepdims=True)
    run_ref[...] = run_ref[...] + jnp.sum(onehot, axis=1, keepdims=True)
    cnt_ref[...] = run_ref[...]
    r8 = lax.broadcasted_iota(jnp.int32, (8, t), 0)
    meta_ref[...] = jnp.where(r8 == 0, bucket, jnp.where(r8 == 1, rank, 0.0)).astype(jnp.int32)


def _outproj_call(xs_, os_, ws, mods, nf, wr, br, n_pad):
    d = xs_[0].shape[1]
    n = sum(a.shape[0] for a in xs_)
    npt = xs_[0].shape[0] // TM
    row = lambda i: (i, 0)
    const = lambda i: (0, 0)
    n_o = len(os_)
    in_specs = ((_token_specs(npt, d) if len(xs_) == 2 else [pl.BlockSpec((TM, d), row)])
                + [pl.BlockSpec((TM, o.shape[1]), row) for o in os_]
                + [pl.BlockSpec(w.shape, const) for w in ws]
                + [_mod_spec(GATE_MIX), pl.BlockSpec((1, d), const),
                   _mod_spec(SHIFT_FFN), _mod_spec(SCALE_FFN),
                   pl.BlockSpec(wr.shape, const), pl.BlockSpec(br.shape, const),
                   pl.BlockSpec((TM, TM), const)])
    tri = jnp.asarray(np.triu(np.ones((TM, TM), np.float32), k=1), dtype=BF16)
    return pl.pallas_call(
        functools.partial(_outproj_kernel, n_x=len(xs_), n_o=n_o, n_prompt_tiles=npt),
        grid=(n // TM,),
        in_specs=in_specs,
        out_specs=[pl.BlockSpec((TM, d), row), pl.BlockSpec((DISP_SLABS, TM, 128), lambda i: (0, i, 0)),
                   pl.BlockSpec((8, TM), lambda i: (0, i)), pl.BlockSpec((BUCKET_ROWS, 128), const)],
        out_shape=[jax.ShapeDtypeStruct((n, d), F32), jax.ShapeDtypeStruct((DISP_SLABS, n_pad, 128), jnp.int32),
                   jax.ShapeDtypeStruct((8, n), jnp.int32), jax.ShapeDtypeStruct((BUCKET_ROWS, 128), F32)],
        scratch_shapes=[pltpu.VMEM((BUCKET_ROWS, 128), F32)],
        compiler_params=_cparams(("arbitrary",)),
        name="outproj_router",
    )(*xs_, *os_, *ws, mods, nf, mods, mods, wr, br, tri)


def _sc_mesh():
    return plsc.VectorSubcoreMesh(core_axis_name="core", subcore_axis_name="subcore")


def _sc_row_copy(src, idx, n_out, scatter):
    r = idx.shape[0]
    k = SC_GROUP
    w_per = r // (SC_WINDOW * SC_WORKERS)
    assert r % (SC_WINDOW * SC_WORKERS) == 0 and w_per % k == 0
    n_groups = w_per // k

    @functools.partial(
        pl.kernel, out_type=jax.ShapeDtypeStruct((n_out, 128), src.dtype), mesh=_sc_mesh(),
        scratch_types=[pltpu.VMEM((w_per, SC_WINDOW), jnp.int32),
                       pltpu.VMEM((2 * k, SC_WINDOW, 128), src.dtype),
                       pltpu.SemaphoreType.DMA((2,)), pltpu.SemaphoreType.DMA((2,))])
    def copy(x_hbm, i_hbm, o_hbm, ibuf, xbuf, in_sem, out_sem):
        wid = lax.axis_index("core") * (SC_WORKERS // 2) + lax.axis_index("subcore")
        pltpu.sync_copy(i_hbm.at[wid], ibuf)
        first = wid * w_per

        def rows(j):
            return pl.ds((first + j) * SC_WINDOW, SC_WINDOW)

        def start_in(g, slot):
            cps = []
            for c in range(k):
                j = g * k + c
                s = x_hbm.at[rows(j)] if scatter else x_hbm.at[ibuf.at[j]]
                cps.append(pltpu.async_copy(s, xbuf.at[slot * k + c], in_sem.at[slot]))
            return cps

        def start_out(g, slot):
            cps = []
            for c in range(k):
                j = g * k + c
                dst = o_hbm.at[ibuf.at[j]] if scatter else o_hbm.at[rows(j)]
                cps.append(pltpu.async_copy(xbuf.at[slot * k + c], dst, out_sem.at[slot]))
            return cps

        pending_in = start_in(0, 0)
        for g in range(n_groups):
            slot = g % 2
            for cp in pending_in:
                cp.wait()
            pending_out = start_out(g, slot)
            if g + 1 < n_groups:
                pending_in = start_in(g + 1, 1 - slot)
            for cp in pending_out:
                cp.wait()

    return copy(src, idx.reshape(SC_WORKERS, w_per, SC_WINDOW))


def _sc_scatter_rows(src, idx, n_out):
    assert idx.shape == (src.shape[0],)
    return _sc_row_copy(src, idx, n_out, scatter=True)


def _moe_kernel(elo_ref, ehi_ref, nvalid_ref, xs_ref, gu_lo_ref, gu_hi_ref, dn_lo_ref, dn_hi_ref, y_ref, tok_ref,
                *, n_tok, dump_tiles):
    i = pl.program_id(0)
    t = xs_ref.shape[1]
    aux = pltpu.bitcast(xs_ref[Y_SLABS], F32)
    r = lax.broadcasted_iota(jnp.int32, (1, t), 1)
    spare = n_tok + (i % dump_tiles) * t + r
    tok = jnp.where(r < nvalid_ref[i], aux.T[2:3, :].astype(jnp.int32), spare)
    for c in range(t // 128):
        tok_ref[0, c:c + 1, :] = tok[:, 128 * c:128 * (c + 1)]

    @pl.when(nvalid_ref[i] > 0)
    def _():
        h = _unpack_pairs([xs_ref[s] for s in range(Y_SLABS)], BF16)
        acc = None
        for lane, gu_ref, dn_ref in ((0, gu_lo_ref, dn_lo_ref), (1, gu_hi_ref, dn_hi_ref)):
            ab = _dot(h, gu_ref[0, 0])
            a, b = ab[:, :D_FF], ab[:, D_FF:]
            y = aux[:, lane:lane + 1] * _dot((a * _sigmoid(a) * b).astype(BF16), dn_ref[0, 0])
            acc = y if acc is None else acc + y
        for s, slab in enumerate(_pack_pairs(acc)):
            y_ref[s] = slab

    @pl.when(nvalid_ref[i] == 0)
    def _():
        y_ref[...] = jnp.zeros_like(y_ref)


def _moe_call(xs, elo, ehi, nvalid, wgu, wdn, layer, n_tiles, n_tok, dump_tiles):
    d = wgu.shape[2]
    gu = lambda sel: pl.BlockSpec((1, 1, d, 2 * D_FF), lambda i, lo, hi, v: (layer, (lo, hi)[sel][i], 0, 0))
    dn = lambda sel: pl.BlockSpec((1, 1, D_FF, d), lambda i, lo, hi, v: (layer, (lo, hi)[sel][i], 0, 0))
    return pl.pallas_call(
        functools.partial(_moe_kernel, n_tok=n_tok, dump_tiles=dump_tiles),
        grid_spec=pltpu.PrefetchScalarGridSpec(
            num_scalar_prefetch=3,
            grid=(n_tiles,),
            in_specs=[pl.BlockSpec((DISP_SLABS, TMO, 128), lambda i, lo, hi, v: (0, i, 0)),
                      gu(0), gu(1), dn(0), dn(1)],
            out_specs=[pl.BlockSpec((Y_SLABS, TMO, 128), lambda i, lo, hi, v: (0, i, 0)),
                       pl.BlockSpec((1, TMO // 128, 128), lambda i, lo, hi, v: (i, 0, 0))]),
        out_shape=[jax.ShapeDtypeStruct((Y_SLABS, n_tiles * TMO, 128), jnp.int32),
                   jax.ShapeDtypeStruct((n_tiles, TMO // 128, 128), jnp.int32)],
        compiler_params=_cparams(("arbitrary",)),
        name="moe_grouped",
    )(elo, ehi, nvalid, xs, wgu, wgu, wdn, wdn)


def _moe_layer(disp, meta, counts, wgu, wdn, layer, n, n_pad, sort_rows):
    n_tiles = sort_rows // TMO
    cnt = counts[:N_BUCKETS, 0].astype(jnp.int32)
    padded = ((cnt + TMO - 1) // TMO) * TMO
    ends = jnp.cumsum(padded)
    offs = ends - padded
    bucket, rank = meta[0], meta[1]
    pos = rank + jnp.sum(jnp.where(bucket[None, :] == jnp.arange(N_BUCKETS, dtype=jnp.int32)[:, None],
                                   offs[:, None], 0), axis=0)
    tile_start = jnp.arange(n_tiles, dtype=jnp.int32) * TMO
    tile_bucket = jnp.minimum(jnp.sum((tile_start[:, None] >= ends[None, :]).astype(jnp.int32), axis=1), N_BUCKETS - 1)
    pair_lo = np.array([0, 0, 0, 1, 1, 2], np.int32)
    pair_hi = np.array([1, 2, 3, 2, 3, 3], np.int32)
    b_lo = jnp.asarray(np.repeat(np.arange(N_GROUPS), N_PAIRS) * EXP_PER_GROUP + np.tile(pair_lo, N_GROUPS), jnp.int32)
    b_hi = jnp.asarray(np.repeat(np.arange(N_GROUPS), N_PAIRS) * EXP_PER_GROUP + np.tile(pair_hi, N_GROUPS), jnp.int32)
    onehot_tb = (tile_bucket[:, None] == jnp.arange(N_BUCKETS, dtype=jnp.int32)[None, :]).astype(jnp.int32)
    elo = jnp.sum(onehot_tb * b_lo[None, :], axis=1)
    ehi = jnp.sum(onehot_tb * b_hi[None, :], axis=1)
    bucket_end = jnp.sum(onehot_tb * (offs + cnt)[None, :], axis=1)
    nvalid = jnp.where(tile_start < ends[-1], jnp.clip(bucket_end - tile_start, 0, TMO), 0)
    dump = sort_rows + jnp.arange(n_pad - n, dtype=jnp.int32)
    pos_sc = jnp.concatenate([pos, dump])
    total = sort_rows + n_pad - n
    sc_idx = (pos_sc[None, :] + (jnp.arange(DISP_SLABS, dtype=jnp.int32) * total)[:, None]).reshape(-1)
    xs = _sc_scatter_rows(disp.reshape(DISP_SLABS * n_pad, 128), sc_idx, DISP_SLABS * total)
    ys, tok = _moe_call(xs.reshape(DISP_SLABS, total, 128), elo, ehi, nvalid, wgu, wdn, layer, n_tiles,
                        n, (n_pad - n) // TMO)
    back_idx = (tok.reshape(1, sort_rows) + (jnp.arange(Y_SLABS, dtype=jnp.int32) * n_pad)[:, None]).reshape(-1)
    z = _sc_scatter_rows(ys.reshape(Y_SLABS * sort_rows, 128), back_idx, Y_SLABS * n_pad)
    return z.reshape(Y_SLABS, n_pad, 128)


def _final_kernel(xn_ref, z_ref, gate_ref, g_ref, yp_ref, ys_ref, *, n_prompt_tiles):
    x = _add_moe(xn_ref, z_ref, gate_ref)
    ms = jnp.mean(x * x, axis=-1, keepdims=True)
    y = x * lax.rsqrt(ms + EPS) * g_ref[...]
    i = pl.program_id(0)

    @pl.when(i < n_prompt_tiles)
    def _():
        yp_ref[...] = y

    @pl.when(i >= n_prompt_tiles)
    def _():
        ys_ref[...] = y


def _final_call(xn, z, mods, g, n_prompt):
    n, d = xn.shape
    npt = n_prompt // TM
    assert n - n_prompt == TM
    return pl.pallas_call(
        functools.partial(_final_kernel, n_prompt_tiles=npt),
        grid=(n // TM,),
        in_specs=[pl.BlockSpec((TM, d), lambda i: (i, 0)), pl.BlockSpec((z.shape[0], TM, 128), lambda i: (0, i, 0)),
                  _mod_spec(GATE_FFN), pl.BlockSpec((1, d), lambda i: (0, 0))],
        out_specs=_token_specs(npt, d),
        out_shape=[jax.ShapeDtypeStruct((n_prompt, d), F32), jax.ShapeDtypeStruct((TM, d), F32)],
        compiler_params=_cparams(("arbitrary",)),
        name="final_norm",
    )(xn, z, mods, g)


def kernel(x_prompt, x_sample, c_prompt, c_sample, state_gla, cache_band_k, cache_band_v, cache_swa_k, cache_swa_v,
           w_ada, b_ada, norm_mix, norm_ffn, norm_final, w_in_even, w_gate_a, b_gate_a, gla_norm, rel_bias_b,
           w_out_even, w_in_odd, sinks_c, w_out_odd, w_router, b_router, w_gate_up, w_down):
    bp, lp, d = x_prompt.shape
    bs, ls_, _ = x_sample.shape
    n_p, n_s = bp * lp, bs * ls_
    n = n_p + n_s
    assert ls_ == CHUNK and n_s == TM and lp % TM == 0 and PAST_LEN % CHUNK == 0

    xp2, xs2 = x_prompt.reshape(n_p, d), x_sample.reshape(n_s, d)

    c16 = jnp.zeros((SEQ_ROWS, d), F32).at[:bp].set(c_prompt).at[bp:bp + bs].set(c_sample)
    mods = _ada_call(c16, w_ada, b_ada)
    seq_of_group = np.concatenate([np.repeat(np.arange(bp), lp // CHUNK), bp + np.arange(bs)])
    mods_g = [mods[l][seq_of_group] for l in range(DEPTH)]

    perm = np.array([4 * (c % 4) + c // 4 for c in range(N_EXPERTS)])
    wr = jnp.zeros((d, 128), F32).at[:, :N_EXPERTS].set(w_router[:, perm])
    br = jnp.zeros((1, 128), F32).at[0, :N_EXPERTS].set(b_router[perm])

    wgu = w_gate_up.astype(BF16)
    wdn = w_down.astype(BF16)

    sc_unit = SC_WINDOW * SC_WORKERS * SC_GROUP
    n_pad = n + TMO
    while (DISP_SLABS * n_pad) % sc_unit or (Y_SLABS * n_pad) % TMO or (n_pad - n) % TMO:
        n_pad += TMO
    sort_rows = n + N_BUCKETS * TMO
    while (Y_SLABS * sort_rows) % sc_unit:
        sort_rows += TMO

    gla_p = gla_s = bk_p = bv_p = bk_s = bv_s = sk_p = sv_p = sk_s = sv_s = None
    xn = z = None
    for l in range(DEPTH):
        i = l // 2
        if l % 2 == 0:
            w = w_in_even[i]
            w_main = jnp.concatenate([w[:, :1536], w[:, 1552:]], axis=1).astype(BF16)
            w_la = jnp.zeros((d, 128), F32).at[:, :GATE_RANK].set(w[:, 1536:1552]).astype(BF16)
            w_gate = jnp.zeros((128, HA * DKA), F32).at[:GATE_RANK].set(w_gate_a[i])
            qa, ka, va, ra, qb, kb, vb, ga = _inproj_even_call(
                xp2, xs2, mods_g[l], norm_mix[l][None], w_main, w_la, w_gate, b_gate_a[i][None])
            xres = [xp2, xs2]
            gn = gla_norm[i][None]
            oa, s_p = _gla_call(qa, ka, va, ga, ra, jnp.zeros((bp, 256, 128), F32), gn, None,
                                n_seq=bp, seq_rows=lp, row0=0, nb=8)
            oa, s_s = _gla_call(qa, ka, va, ga, ra, state_gla[i].reshape(bs, 256, 128), gn, oa,
                                n_seq=bs, seq_rows=ls_, row0=n_p, nb=1)
            gla_p, gla_s = s_p.reshape(1, bp, HA, DKA, DVA), s_s.reshape(1, bs, HA, DKA, DVA)
            pb = N_PREV_B * CHUNK
            tq, g = 512, 2
            bps = lp // tq
            bias = _band_bias(rel_bias_b[i], g, pb)
            ob = _attn_call(
                functools.partial(_band_kernel, g=g, n_sub=tq // (CHUNK * g), pb=pb, blocks_per_seq=bps, mask_first=True),
                qb, kb, kb, vb, vb, [bias], [pl.BlockSpec(bias.shape, lambda i_: (0, 0, 0))], None,
                width=512, kv_width=512, tq=tq, pb=pb, n_blocks=n_p // tq, blk0=0,
                prev_map=lambda i_: (jnp.maximum(i_ * (tq // pb) - 1, 0), 0), name="band_prompt")
            ck = cache_band_k[i].reshape(bs * pb, HB * DHB).astype(BF16)
            cv = cache_band_v[i].reshape(bs * pb, HB * DHB).astype(BF16)
            bias1 = _band_bias(rel_bias_b[i], 1, pb)
            ob = _attn_call(
                functools.partial(_band_kernel, g=1, n_sub=1, pb=pb, blocks_per_seq=1, mask_first=False),
                qb, ck, kb, cv, vb, [bias1], [pl.BlockSpec(bias1.shape, lambda i_: (0, 0, 0))], ob,
                width=512, kv_width=512, tq=CHUNK, pb=pb, n_blocks=bs, blk0=n_p // CHUNK,
                prev_map=lambda i_: (i_, 0), name="band_sample")
            tail = lambda a: jnp.stack([a[(b + 1) * lp - pb:(b + 1) * lp] for b in range(bp)]).astype(F32).reshape(1, bp, pb, HB, DHB)
            new = lambda a: a[n_p:].astype(F32).reshape(bs, ls_, HB, DHB)
            bk_p, bv_p = tail(kb), tail(vb)
            bk_s = jnp.concatenate([cache_band_k[i][:, ls_:], new(kb)], axis=1)[None]
            bv_s = jnp.concatenate([cache_band_v[i][:, ls_:], new(vb)], axis=1)[None]
            wo = w_out_even[i].astype(BF16)
            os_, ws = [oa, ob], [wo[:HA * DVA], wo[HA * DVA:]]
        else:
            w = w_in_odd[i]
            wk, wv = w[:, 1024:1152], w[:, 1152:1280]
            dup = lambda a: jnp.concatenate([a[:, :64], a[:, :64], a[:, 64:], a[:, 64:]], axis=1)
            w_all = jnp.concatenate([w[:, :1024], dup(wk), dup(wv)], axis=1).astype(BF16)
            cos, sin, rope_map = _rope_tables(lp, ls_, bp, bs)
            x, q, k, v = _inproj_odd_call(xn, z, mods_g[l - 1], mods_g[l], norm_mix[l][None], cos, sin, rope_map, w_all)
            xres = [x]
            pb = WINDOW
            tq, g = 512, 2
            bps = lp // tq
            sink = sinks_c[i][None]
            sink_spec = [pl.BlockSpec(memory_space=pltpu.SMEM)]
            o = _attn_call(
                functools.partial(_swa_kernel, g=g, n_sub=tq // (CHUNK * g), pb=pb, blocks_per_seq=bps, mask_first=True),
                q, k, k, v, v, [sink], sink_spec, None,
                width=1024, kv_width=256, tq=tq, pb=pb, n_blocks=n_p // tq, blk0=0,
                prev_map=lambda i_: (jnp.maximum(i_ * (tq // pb) - 1, 0), 0), name="swa_prompt")
            dupc = lambda c: jnp.concatenate([c[:, :, 0], c[:, :, 0], c[:, :, 1], c[:, :, 1]], axis=-1).reshape(bs * pb, 256).astype(BF16)
            ck, cv = dupc(cache_swa_k[i]), dupc(cache_swa_v[i])
            o = _attn_call(
                functools.partial(_swa_kernel, g=1, n_sub=1, pb=pb, blocks_per_seq=1, mask_first=False),
                q, ck, k, cv, v, [sink], sink_spec, o,
                width=1024, kv_width=256, tq=CHUNK, pb=pb, n_blocks=bs, blk0=n_p // CHUNK,
                prev_map=lambda i_: (i_, 0), name="swa_sample")
            undup = lambda a: jnp.concatenate([a[:, 0:64], a[:, 128:192]], axis=1).astype(F32)
            tail = lambda a: jnp.stack([undup(a[(b + 1) * lp - pb:(b + 1) * lp]) for b in range(bp)]).reshape(1, bp, pb, KVC, DHC)
            new = lambda a: undup(a[n_p:]).reshape(bs, ls_, KVC, DHC)
            sk_p, sv_p = tail(k), tail(v)
            sk_s = jnp.concatenate([cache_swa_k[i][:, ls_:], new(k)], axis=1)[None]
            sv_s = jnp.concatenate([cache_swa_v[i][:, ls_:], new(v)], axis=1)[None]
            os_, ws = [o], [w_out_odd[i].astype(BF16)]
        xn, disp, meta, counts = _outproj_call(xres, os_, ws, mods_g[l], norm_ffn[l][None], wr, br, n_pad)
        z = _moe_layer(disp, meta, counts, wgu, wdn, l, n, n_pad, sort_rows)

    y_prompt, y_sample = _final_call(xn, z, mods_g[DEPTH - 1], norm_final[None], n_p)
    return (y_prompt.reshape(bp, lp, d), y_sample.reshape(bs, ls_, d),
            gla_p, gla_s, bk_p, bv_p, bk_s, bv_s, sk_p, sv_p, sk_s, sv_s)
```

```python
import functools

import numpy as np
import jax
import jax.numpy as jnp
from jax import lax
from jax.experimental import pallas as pl
from jax.experimental.pallas import tpu as pltpu
from jax.experimental.pallas import tpu_sc as plsc

F32 = jnp.float32
BF16 = jnp.bfloat16

D_MODEL = 1024
DEPTH = 2
CHUNK = 64
PAST_LEN = 4096
HA, DKA, DVA = 4, 64, 128
GATE_RANK = 16
GATE_TAU = 16.0
HB, DHB = 8, 64
N_PREV_B = 8
MAX_REL = 128
HC, KVC, DHC = 16, 2, 64
WINDOW = 128
ROPE_THETA = 10000.0
N_EXPERTS = 16
N_GROUPS = 4
EXP_PER_GROUP = 4
D_FF = 512
EPS = 1e-6

N_PAIRS = 6
N_BUCKETS = N_GROUPS * N_PAIRS
BUCKET_ROWS = 32
Y_SLABS = 4
DISP_SLABS = Y_SLABS + 1
TMO = 256
SC_WINDOW = 128
SC_WORKERS = 32
SC_GROUP = 3

TM = 512
SEQ_ROWS = 16
SUB = 16
LOG2E = 1.4426950408889634


def _cparams(sem, vmem_mb=48):
    return pltpu.CompilerParams(dimension_semantics=sem, vmem_limit_bytes=vmem_mb * 1024 * 1024)


def _dot(a, b):
    return jnp.dot(a, b, preferred_element_type=F32)


def _dot_nt(a, b):
    return lax.dot_general(a, b, (((1,), (1,)), ((), ())), preferred_element_type=F32)


def _split(a):
    hi = a.astype(BF16)
    lo = (a - hi.astype(F32)).astype(BF16)
    return hi, lo


def _dot3(a, b):
    ah, al = _split(a)
    bh, bl = _split(b)
    return _dot(ah, bh) + _dot(ah, bl) + _dot(al, bh)


def _sigmoid(x):
    return 1.0 / (1.0 + jnp.exp(-x))


def _group_affine(y, mul, add):
    parts = []
    for gi in range(y.shape[0] // CHUNK):
        p = y[gi * CHUNK:(gi + 1) * CHUNK]
        if mul is not None:
            p = p * mul[gi:gi + 1]
        if add is not None:
            p = p + add[gi:gi + 1]
        parts.append(p)
    return jnp.concatenate(parts, axis=0)


def _norm_mod(x, g, shift, scale):
    ms = jnp.mean(x * x, axis=-1, keepdims=True)
    return _group_affine(x * lax.rsqrt(ms + EPS) * g, 1.0 + scale, shift)


def _mod_spec(part):
    return pl.BlockSpec((TM // CHUNK, D_MODEL), lambda i: (i, part))


SHIFT_MIX, SCALE_MIX, GATE_MIX, SHIFT_FFN, SCALE_FFN, GATE_FFN = range(6)


def _on_token_tile(xp_ref, xs_ref, n_prompt_tiles, body):
    @pl.when(pl.program_id(0) < n_prompt_tiles)
    def _():
        body(xp_ref)

    @pl.when(pl.program_id(0) >= n_prompt_tiles)
    def _():
        body(xs_ref)


def _token_specs(n_prompt_tiles, d):
    return [pl.BlockSpec((TM, d), lambda i: (jnp.minimum(i, n_prompt_tiles - 1), 0)),
            pl.BlockSpec((TM, d), lambda i: (0, 0))]


def _ada_kernel(c_ref, w_ref, b_ref, o_ref):
    c = c_ref[...]
    o_ref[0] = _dot3(c * _sigmoid(c), w_ref[0]) + b_ref[0]


def _ada_call(c16, w_ada, b_ada):
    d = D_MODEL
    tn = 1024
    return pl.pallas_call(
        _ada_kernel,
        grid=(DEPTH, 6 * d // tn),
        in_specs=[pl.BlockSpec((SEQ_ROWS, d), lambda l, j: (0, 0)),
                  pl.BlockSpec((1, d, tn), lambda l, j: (l, 0, j)),
                  pl.BlockSpec((1, 1, tn), lambda l, j: (l, 0, j))],
        out_specs=pl.BlockSpec((1, SEQ_ROWS, tn), lambda l, j: (l, 0, j)),
        out_shape=jax.ShapeDtypeStruct((DEPTH, SEQ_ROWS, 6 * d), F32),
        compiler_params=_cparams(("arbitrary", "arbitrary")),
        name="ada",
    )(c16, w_ada, b_ada.reshape(DEPTH, 1, 6 * d))


def _inproj_even_kernel(xp_ref, xs_ref, sh_ref, sc_ref, g_ref, w_ref, wla_ref, wg_ref, bg_ref,
                        qa_ref, ka_ref, va_ref, ra_ref, qb_ref, kb_ref, vb_ref, ga_ref, *, n_prompt_tiles):
    def body(x_ref):
        hb = _norm_mod(x_ref[...], g_ref[...], sh_ref[...], sc_ref[...]).astype(BF16)
        qa_ref[...] = (_dot(hb, w_ref[:, 0:256]) * (DKA ** -0.5)).astype(BF16)
        ka_ref[...] = _dot(hb, w_ref[:, 256:512]).astype(BF16)
        va_ref[...] = _dot(hb, w_ref[:, 512:1024]).astype(BF16)
        ra_ref[...] = _dot(hb, w_ref[:, 1024:1536]).astype(BF16)
        qb_ref[...] = (_dot(hb, w_ref[:, 1536:2048]) * (DHB ** -0.5 * LOG2E)).astype(BF16)
        kb_ref[...] = _dot(hb, w_ref[:, 2048:2560]).astype(BF16)
        vb_ref[...] = _dot(hb, w_ref[:, 2560:3072]).astype(BF16)
        la = _dot(hb, wla_ref[...])
        gl = _dot3(la, wg_ref[...]) + bg_ref[...]
        ga_ref[...] = -(jnp.maximum(-gl, 0.0) + jnp.log(1.0 + jnp.exp(-jnp.abs(gl)))) * (1.0 / GATE_TAU)

    _on_token_tile(xp_ref, xs_ref, n_prompt_tiles, body)


def _inproj_even_call(xp, xs, mods, g, w_main, w_la, w_gate, b_gate):
    d = xp.shape[1]
    npt = xp.shape[0] // TM
    n = xp.shape[0] + xs.shape[0]
    row = lambda i: (i, 0)
    const = lambda i: (0, 0)
    widths = (256, 256, 512, 512, 512, 512, 512)
    out_shape = [jax.ShapeDtypeStruct((n, w), BF16) for w in widths] + [jax.ShapeDtypeStruct((n, 256), F32)]
    out_specs = [pl.BlockSpec((TM, w), row) for w in widths] + [pl.BlockSpec((TM, 256), row)]
    return pl.pallas_call(
        functools.partial(_inproj_even_kernel, n_prompt_tiles=npt),
        grid=(n // TM,),
        in_specs=_token_specs(npt, d) + [
            _mod_spec(SHIFT_MIX), _mod_spec(SCALE_MIX),
            pl.BlockSpec((1, d), const),
            pl.BlockSpec(w_main.shape, const), pl.BlockSpec(w_la.shape, const),
            pl.BlockSpec(w_gate.shape, const), pl.BlockSpec(b_gate.shape, const)],
        out_specs=out_specs, out_shape=out_shape,
        compiler_params=_cparams(("parallel",)),
        name="inproj_even",
    )(xp, xs, mods, mods, g, w_main, w_la, w_gate, b_gate)


def _rope(x, cos, sin_signed):
    t, w = x.shape
    lane = lax.broadcasted_iota(jnp.int32, (1, w), 1)
    first_half = (lane & 63) < 32
    rot = jnp.where(first_half, pltpu.roll(x, w - 32, 1), pltpu.roll(x, 32, 1))
    reps = w // 128
    return x * jnp.tile(cos, (1, reps)) + rot * jnp.tile(sin_signed, (1, reps))


def _unpack_pairs(slabs, dtype):
    lo = [pltpu.bitcast(s << 16, F32) for s in slabs]
    hi = [pltpu.bitcast(s & jnp.int32(-65536), F32) for s in slabs]
    return jnp.concatenate(lo + hi, axis=1).astype(dtype)


def _pack_pairs(x):
    bits = pltpu.bitcast(x.astype(BF16).astype(F32), jnp.int32)
    half = x.shape[1] // 2
    packed = ((bits[:, :half] >> 16) & jnp.int32(0xFFFF)) | (bits[:, half:] & jnp.int32(-65536))
    return [packed[:, 128 * s:128 * (s + 1)] for s in range(half // 128)]


def _add_moe(xn_ref, z_ref, gate_ref):
    y = _unpack_pairs([z_ref[s] for s in range(z_ref.shape[0])], F32)
    return xn_ref[...] + _group_affine(y, gate_ref[...], None)


def _rope_tables(lp, ls_, bp, bs):
    assert PAST_LEN + ls_ <= lp and lp % 128 == 0 and bs * ls_ == TM
    half = DHC // 2
    inv = ROPE_THETA ** (-jnp.arange(half, dtype=F32) / half)
    inv = jnp.tile(inv, 128 // half)
    sign = jnp.asarray(np.tile(np.repeat([-1.0, 1.0], half), 128 // DHC), F32)
    a = jnp.asarray(np.arange(lp // 128) * 128, F32)[:, None] * inv[None, :]
    b = jnp.asarray(np.arange(128), F32)[:, None] * inv[None, :]
    ca, sa, cb, sb = jnp.cos(a)[:, None], jnp.sin(a)[:, None], jnp.cos(b)[None], jnp.sin(b)[None]
    cos = (ca * cb - sa * sb).reshape(lp, 128)
    sin = ((sa * cb + ca * sb) * sign).reshape(lp, 128)
    with_sample = lambda t: jnp.concatenate([t, jnp.tile(t[PAST_LEN:PAST_LEN + ls_], (bs, 1))], axis=0)
    tiles = lp // TM
    return with_sample(cos), with_sample(sin), lambda i: (jnp.where(i < bp * tiles, i % tiles, tiles), 0)


def _inproj_odd_kernel(xn_ref, z_ref, gate_ref, sh_ref, sc_ref, g_ref, cos_ref, sin_ref, w_ref,
                       x_ref, q_ref, k_ref, v_ref):
    x = _add_moe(xn_ref, z_ref, gate_ref)
    x_ref[...] = x
    hb = _norm_mod(x, g_ref[...], sh_ref[...], sc_ref[...]).astype(BF16)
    cos, sin = cos_ref[...], sin_ref[...]
    q = _rope(_dot(hb, w_ref[:, 0:1024]), cos, sin)
    q_ref[...] = (q * (DHC ** -0.5 * LOG2E)).astype(BF16)
    k_ref[...] = _rope(_dot(hb, w_ref[:, 1024:1280]), cos, sin).astype(BF16)
    v_ref[...] = _dot(hb, w_ref[:, 1280:1536]).astype(BF16)


def _inproj_odd_call(xn, z, mods_prev, mods, g, cos, sin, rope_map, w):
    n, d = xn.shape
    row = lambda i: (i, 0)
    const = lambda i: (0, 0)
    widths = (1024, 256, 256)
    return pl.pallas_call(
        _inproj_odd_kernel,
        grid=(n // TM,),
        in_specs=[pl.BlockSpec((TM, d), row), pl.BlockSpec((z.shape[0], TM, 128), lambda i: (0, i, 0)),
                  _mod_spec(GATE_FFN), _mod_spec(SHIFT_MIX), _mod_spec(SCALE_MIX),
                  pl.BlockSpec((1, d), const),
                  pl.BlockSpec((TM, 128), rope_map), pl.BlockSpec((TM, 128), rope_map),
                  pl.BlockSpec(w.shape, const)],
        out_specs=[pl.BlockSpec((TM, d), row)] + [pl.BlockSpec((TM, wd), row) for wd in widths],
        out_shape=[jax.ShapeDtypeStruct((n, d), F32)] + [jax.ShapeDtypeStruct((n, wd), BF16) for wd in widths],
        compiler_params=_cparams(("parallel",)),
        name="inproj_odd",
    )(xn, z, mods_prev, mods, mods, g, cos, sin, w)


def _gla_tri():
    t = np.arange(CHUNK)[:, None]
    s = np.arange(CHUNK)[None, :]
    cum = s <= t
    start = s < (t // SUB) * SUB
    end = s < (t // SUB + 1) * SUB
    return jnp.asarray(np.concatenate([cum, start, end], axis=0).astype(np.float32), dtype=BF16)


def _gla_kernel(q_ref, k_ref, v_ref, g_ref, r_ref, s0_ref, gn_ref, tri_ref, o_ref, sout_ref, s_ref, *, nb):
    c_ = CHUNK
    nsub = c_ // SUB

    @pl.when(pl.program_id(1) == 0)
    def _():
        s_ref[...] = s0_ref[0]

    tri = tri_ref[...]
    lane = lax.broadcasted_iota(jnp.int32, (1, 128), 1)
    hmask = [jnp.where(lane < DKA, 1.0, 0.0), jnp.where(lane >= DKA, 1.0, 0.0)]
    ti = lax.broadcasted_iota(jnp.int32, (c_, c_), 0)
    si = lax.broadcasted_iota(jnp.int32, (c_, c_), 1)
    rb, cb = ti >> 4, si >> 4
    m_diag = (rb == cb) & (si <= ti)
    m_off = [(cb == j) & (rb > j) for j in range(nsub - 1)]
    hk = HA * DKA
    eye = lax.broadcasted_iota(jnp.int32, (hk, hk), 0) == lax.broadcasted_iota(jnp.int32, (hk, hk), 1)
    gn = gn_ref[...]

    local = []
    for c in range(nb):
        rows = slice(c * c_, (c + 1) * c_)
        g_hi, g_lo = _split(g_ref[rows, :])
        cs = _dot(tri, g_hi) + _dot(tri, g_lo)
        b, rs, re = cs[0:c_], cs[c_:2 * c_], cs[2 * c_:3 * c_]
        q = q_ref[rows, :].astype(F32)
        k = k_ref[rows, :].astype(F32)
        bl = b[c_ - 1:c_, :]
        qd = q * jnp.exp(b - rs)
        kd = k * jnp.exp(rs - b)
        ke = k * jnp.exp(re - b)
        qi = q * jnp.exp(b)
        kl = k * jnp.exp(bl - b)
        ql = [q * jnp.exp(jnp.minimum(b - b[SUB * (j + 1) - 1:SUB * (j + 1), :], 0.0)) for j in range(nsub - 1)]
        dcol = jnp.sum(jnp.where(eye, jnp.broadcast_to(jnp.exp(bl), (hk, hk)), 0.0), axis=1, keepdims=True)
        upd, o_intra, q_inter = [], [], []
        for p in range(HA // 2):
            ls = slice(128 * p, 128 * (p + 1))
            kd_p = kd[:, ls].astype(BF16)
            ke_p = ke[:, ls].astype(BF16)
            klt = kl[:, ls].T
            for hh in range(2):
                h = 2 * p + hh
                msk = hmask[hh]
                a1 = _dot_nt((qd[:, ls] * msk).astype(BF16), kd_p)
                lhs2 = jnp.concatenate([ql[j][:, ls] * msk for j in range(nsub - 1)], axis=0).astype(BF16)
                a2 = _dot_nt(lhs2, ke_p)
                att = jnp.zeros((c_, c_), F32)
                for j in reversed(range(nsub - 1)):
                    att = jnp.where(m_off[j], a2[j * c_:(j + 1) * c_], att)
                att = jnp.where(m_diag, a1, att)
                v_h = v_ref[rows, DVA * h:DVA * (h + 1)]
                o_intra.append(_dot(att.astype(BF16), v_h))
                q_inter.append((qi[:, ls] * msk).astype(BF16))
                upd.append(_dot(klt[DKA * hh:DKA * (hh + 1)].astype(BF16), v_h))
        local.append((dcol, jnp.concatenate(upd, axis=0), o_intra, q_inter))

    s_cur = s_ref[...]
    s_in = []
    for dcol, upd, _, _ in local:
        s_in.append(s_cur.astype(BF16))
        s_cur = dcol * s_cur + upd
    s_ref[...] = s_cur
    sout_ref[0] = s_cur

    for c, (_, _, o_intra, q_inter) in enumerate(local):
        rows = slice(c * c_, (c + 1) * c_)
        for h in range(HA):
            pair_rows = slice(128 * (h // 2), 128 * (h // 2 + 1))
            o = o_intra[h] + _dot(q_inter[h], s_in[c][pair_rows, :])
            ms = jnp.mean(o * o, axis=-1, keepdims=True)
            vs = slice(DVA * h, DVA * (h + 1))
            rr = r_ref[rows, vs].astype(F32)
            o_ref[rows, vs] = (o * lax.rsqrt(ms + EPS) * gn * (rr * _sigmoid(rr))).astype(BF16)


def _gla_call(q, k, v, g, r, s0, gn, o_prev, *, n_seq, seq_rows, row0, nb):
    tq = nb * CHUNK
    steps = seq_rows // tq
    blk0 = row0 // tq
    row = lambda b, j: (blk0 + b * steps + j, 0)
    const = lambda b, j: (0, 0)
    tri = _gla_tri()
    in_specs = [pl.BlockSpec((tq, 256), row), pl.BlockSpec((tq, 256), row), pl.BlockSpec((tq, 512), row),
                pl.BlockSpec((tq, 256), row), pl.BlockSpec((tq, 512), row),
                pl.BlockSpec((1, 256, 128), lambda b, j: (b, 0, 0)),
                pl.BlockSpec((1, 128), const), pl.BlockSpec(tri.shape, const)]
    args = [q, k, v, g, r, s0, gn, tri]
    aliases = {}
    if o_prev is not None:
        in_specs.append(pl.BlockSpec(memory_space=pl.ANY))
        args.append(o_prev)
        aliases = {len(args) - 1: 0}
    kern = functools.partial(_gla_kernel, nb=nb)
    if o_prev is not None:
        kern = _drop_arg(kern, 8)
    return pl.pallas_call(
        kern,
        grid=(n_seq, steps),
        in_specs=in_specs,
        out_specs=[pl.BlockSpec((tq, 512), row), pl.BlockSpec((1, 256, 128), lambda b, j: (b, 0, 0))],
        out_shape=[jax.ShapeDtypeStruct((q.shape[0], 512), BF16), jax.ShapeDtypeStruct((n_seq, 256, 128), F32)],
        scratch_shapes=[pltpu.VMEM((256, 128), F32)],
        input_output_aliases=aliases,
        compiler_params=_cparams(("arbitrary", "arbitrary")),
        name="gla",
    )(*args)


def _drop_arg(fn, idx):
    def wrapped(*refs):
        return fn(*refs[:idx], *refs[idx + 1:])
    return wrapped


def _window(prev_ref, cur_ref, lo, hi, pb, ls):
    if lo < pb:
        return jnp.concatenate([prev_ref[lo:pb, ls], cur_ref[0:hi - pb, ls]], axis=0)
    return cur_ref[lo - pb:hi - pb, ls]


def _band_kernel(q_ref, kp_ref, kc_ref, vp_ref, vc_ref, bias_ref, o_ref, *, g, n_sub, pb):
    qs = CHUNK * g
    kw_rows = pb + qs
    lane = lax.broadcasted_iota(jnp.int32, (1, 128), 1)
    low = lane < DHB
    hmask = [jnp.where(low, 1.0, 0.0), jnp.where(low, 0.0, 1.0)]
    for s in range(n_sub):
        sb = s if bias_ref.shape[0] > 1 else 0
        rows = slice(qs * s, qs * (s + 1))
        lanes = [slice(128 * p, 128 * (p + 1)) for p in range(HB // 2)]
        heads = [(p, hh) for p in range(HB // 2) for hh in range(2)]
        qps = [q_ref[rows, ls].astype(F32) for ls in lanes]
        kws = [_window(kp_ref, kc_ref, qs * s, qs * s + kw_rows, pb, ls) for ls in lanes]
        vws = [_window(vp_ref, vc_ref, qs * s, qs * s + kw_rows, pb, ls) for ls in lanes]
        scs = [_dot_nt((qps[p] * hmask[hh]).astype(BF16), kws[p]) + bias_ref[sb, 2 * p + hh] for p, hh in heads]
        pes = [jnp.exp2(sc - jnp.max(sc, axis=-1, keepdims=True)) for sc in scs]
        outs = [_dot(pe.astype(BF16), vws[p]) / jnp.sum(pe, axis=-1, keepdims=True) for pe, (p, hh) in zip(pes, heads)]
        for p, ls in enumerate(lanes):
            o_ref[rows, ls] = jnp.where(low, outs[2 * p], outs[2 * p + 1]).astype(BF16)


def _band_valid(g, pb, n_sub=None):
    rows, kw = CHUNK * g, pb + CHUNK * g
    r = np.arange(rows)[:, None]
    c = np.arange(kw)[None, :]
    dd = c // CHUNK - r // CHUNK
    band = (dd >= 0) & (dd <= pb // CHUNK)
    if n_sub is None:
        return band[None]
    return np.stack([band & (c >= pb - rows * s) for s in range(n_sub)])


def _band_bias(table, g, pb, valid):
    rows, kw = CHUNK * g, pb + CHUNK * g
    period = kw + rows
    m = np.arange(period)
    m = np.where(m < kw, m, m - period)
    ext = table[:, np.clip(m - pb, -MAX_REL, MAX_REL) + MAX_REL] * LOG2E
    flat = jnp.tile(ext, (1, rows))[:, :rows * (period - 1)]
    bias = flat.reshape(table.shape[0], rows, period - 1)[:, :, :kw]
    return jnp.where(valid[:, None], bias[None], -jnp.inf)


def _attn_call(kernel, q, kp, kc, vp, vc, extra, extra_specs, o_prev, *, width, kv_width, tq, pb,
               n_blocks, blk_map, prev_map, name):
    row = lambda i: (blk_map(i), 0)
    prev = lambda i: (prev_map(i), 0)
    in_specs = [pl.BlockSpec((tq, width), row),
                pl.BlockSpec((pb, kv_width), prev), pl.BlockSpec((tq, kv_width), row),
                pl.BlockSpec((pb, kv_width), prev), pl.BlockSpec((tq, kv_width), row)] + extra_specs
    args = [q, kp, kc, vp, vc] + extra
    aliases = {}
    if o_prev is not None:
        in_specs.append(pl.BlockSpec(memory_space=pl.ANY))
        args.append(o_prev)
        aliases = {len(args) - 1: 0}
        kernel = _drop_arg(kernel, len(args) - 1)
    return pl.pallas_call(
        kernel,
        grid=(n_blocks,),
        in_specs=in_specs,
        out_specs=pl.BlockSpec((tq, width), row),
        out_shape=jax.ShapeDtypeStruct((q.shape[0], width), BF16),
        input_output_aliases=aliases,
        compiler_params=_cparams(("parallel",)),
        name=name,
    )(*args)


def _attention(kernel_fn, q, k, v, cache_k, cache_v, masks, extra, extra_specs, *, width, kv_width, pb, tq, g,
               bp, lp, bs, name):
    bps = lp // tq
    n_sub = tq // (CHUNK * g)
    spec = lambda a: [pl.BlockSpec(a.shape, lambda i: (0,) * a.ndim)]
    kern = functools.partial(kernel_fn, g=g, n_sub=n_sub, pb=pb)
    common = dict(width=width, kv_width=kv_width, pb=pb)
    main = lambda i: (i // (bps - 1)) * bps + i % (bps - 1) + 1
    o = _attn_call(kern, q, k, k, v, v, [masks[0]] + extra, spec(masks[0]) + extra_specs, None, tq=tq,
                   n_blocks=bp * (bps - 1), blk_map=main, prev_map=lambda i: main(i) * (tq // pb) - 1,
                   name=name + "_main", **common)
    first = lambda i: i * bps
    o = _attn_call(kern, q, k, k, v, v, [masks[1]] + extra, spec(masks[1]) + extra_specs, o, tq=tq,
                   n_blocks=bp, blk_map=first, prev_map=lambda i: jnp.maximum(first(i) * (tq // pb) - 1, 0),
                   name=name + "_first", **common)
    samp = functools.partial(kernel_fn, g=1, n_sub=1, pb=pb)
    return _attn_call(samp, q, cache_k, k, cache_v, v, [masks[2]] + extra, spec(masks[2]) + extra_specs, o, tq=CHUNK,
                      n_blocks=bs, blk_map=lambda i: bp * lp // CHUNK + i, prev_map=lambda i: i,
                      name=name + "_sample", **common)


def _swa_kernel(q_ref, kp_ref, kc_ref, vp_ref, vc_ref, mask_ref, sink_ref, o_ref, *, g, n_sub, pb):
    qs = CHUNK * g
    kw_rows = pb + qs
    lane = lax.broadcasted_iota(jnp.int32, (1, 128), 1)
    low = lane < DHC
    hmask = [jnp.where(low, 1.0, 0.0), jnp.where(low, 0.0, 1.0)]
    pairs_per_kv = HC // KVC // 2
    for s in range(n_sub):
        msk = mask_ref[s if mask_ref.shape[0] > 1 else 0]
        rows = slice(qs * s, qs * (s + 1))
        kws = [_window(kp_ref, kc_ref, qs * s, qs * s + kw_rows, pb, slice(128 * kv, 128 * (kv + 1))) for kv in range(KVC)]
        vws = [_window(vp_ref, vc_ref, qs * s, qs * s + kw_rows, pb, slice(128 * kv, 128 * (kv + 1))) for kv in range(KVC)]
        heads = [(j, hh) for j in range(HC // 2) for hh in range(2)]
        qps = [q_ref[rows, 128 * j:128 * (j + 1)].astype(F32) for j in range(HC // 2)]
        scs = [_dot_nt((qps[j] * hmask[hh]).astype(BF16), kws[j // pairs_per_kv]) + msk for j, hh in heads]
        sks = [sink_ref[0, 2 * j + hh] for j, hh in heads]
        ms = [jnp.maximum(jnp.max(sc, axis=-1, keepdims=True), sk) for sc, sk in zip(scs, sks)]
        pes = [jnp.exp2(sc - m) for sc, m in zip(scs, ms)]
        outs = [_dot(pe.astype(BF16), vws[j // pairs_per_kv]) / (jnp.sum(pe, axis=-1, keepdims=True) + jnp.exp2(sk - m))
                for pe, sk, m, (j, hh) in zip(pes, sks, ms, heads)]
        for j in range(HC // 2):
            o_ref[rows, 128 * j:128 * (j + 1)] = jnp.where(low, outs[2 * j], outs[2 * j + 1]).astype(BF16)


def _route(logits_t):
    a = [logits_t[4 * j:4 * j + 4] for j in range(EXP_PER_GROUP)]

    def first_argmax(vals, m):
        idx = jnp.full(m.shape, float(len(vals) - 1), F32)
        for j in reversed(range(len(vals) - 1)):
            idx = jnp.where(vals[j] == m, float(j), idx)
        return idx

    m1 = functools.reduce(jnp.maximum, a)
    i1 = first_argmax(a, m1)
    bsec = [jnp.where(i1 == float(j), -jnp.inf, a[j]) for j in range(EXP_PER_GROUP)]
    m2 = functools.reduce(jnp.maximum, bsec)
    i2 = first_argmax(bsec, m2)
    rows = lambda x: [x[gi:gi + 1] for gi in range(N_GROUPS)]
    gm = functools.reduce(jnp.maximum, rows(m1))
    gscore = jnp.exp(m1 - gm) + jnp.exp(m2 - gm)
    gs = rows(gscore)
    gsel = first_argmax(gs, functools.reduce(jnp.maximum, gs))

    def pick(x):
        xr = rows(x)
        out = xr[N_GROUPS - 1]
        for gi in reversed(range(N_GROUPS - 1)):
            out = jnp.where(gsel == float(gi), xr[gi], out)
        return out

    p1 = jnp.exp(pick(m1) - gm)
    p2 = jnp.exp(pick(m2) - gm)
    w1 = p1 / (p1 + p2)
    w2 = p2 / (p1 + p2)
    s1, s2 = pick(i1), pick(i2)
    lo, hi = jnp.minimum(s1, s2), jnp.maximum(s1, s2)
    pair = jnp.where(lo == 0.0, hi - 1.0, jnp.where(lo == 1.0, hi + 1.0, 5.0))
    bucket = gsel * float(N_PAIRS) + pair
    first_is_lo = s1 < s2
    return bucket, jnp.where(first_is_lo, w1, w2), jnp.where(first_is_lo, w2, w1)


def _outproj_kernel(*refs, n_x, n_o, n_prompt_tiles):
    x_refs = refs[:n_x]
    o_refs = refs[n_x:n_x + n_o]
    w_refs = refs[n_x + n_o:n_x + 2 * n_o]
    (gate_ref, nf_ref, sh_ref, sc_ref, wr_ref, br_ref, tri_ref,
     xn_ref, disp_ref, meta_ref, cnt_ref, run_ref) = refs[n_x + 2 * n_o:]
    t = xn_ref.shape[0]

    @pl.when(pl.program_id(0) == 0)
    def _():
        run_ref[...] = jnp.zeros_like(run_ref)

    y = _dot(o_refs[0][...], w_refs[0][...])
    for i in range(1, n_o):
        y = y + _dot(o_refs[i][...], w_refs[i][...])
    gy = _group_affine(y, gate_ref[...], None)

    def residual(x_ref):
        xn_ref[...] = x_ref[...] + gy

    if n_x == 1:
        residual(x_refs[0])
    else:
        _on_token_tile(x_refs[0], x_refs[1], n_prompt_tiles, residual)
    xn = xn_ref[...]
    h = _norm_mod(xn, nf_ref[...], sh_ref[...], sc_ref[...])
    for s, slab in enumerate(_pack_pairs(h)):
        disp_ref[s] = slab
    logits = _dot3(h, wr_ref[...]) + br_ref[...]
    bucket, w_lo, w_hi = _route(logits.T[0:N_EXPERTS])
    r128 = lax.broadcasted_iota(jnp.int32, (128, t), 0)
    tok = (pl.program_id(0) * t + lax.broadcasted_iota(jnp.int32, (1, t), 1)).astype(F32)
    aux = jnp.where(r128 == 0, w_lo, jnp.where(r128 == 1, w_hi, jnp.where(r128 == 2, tok, 0.0))).T
    disp_ref[disp_ref.shape[0] - 1] = pltpu.bitcast(aux, jnp.int32)
    brow = lax.broadcasted_iota(jnp.int32, (BUCKET_ROWS, t), 0).astype(F32)
    onehot = jnp.where(brow == bucket, 1.0, 0.0)
    before = _dot(onehot.astype(BF16), tri_ref[...]) + run_ref[:, 0:1]
    rank = jnp.sum(onehot * before, axis=0, keepdims=True)
    run_ref[...] = run_ref[...] + jnp.sum(onehot, axis=1, keepdims=True)
    cnt_ref[...] = run_ref[...]
    r8 = lax.broadcasted_iota(jnp.int32, (8, t), 0)
    meta_ref[...] = jnp.where(r8 == 0, bucket, jnp.where(r8 == 1, rank, 0.0)).astype(jnp.int32)


def _outproj_call(xs_, os_, ws, mods, nf, wr, br, n_pad):
    d = xs_[0].shape[1]
    n = sum(a.shape[0] for a in xs_)
    npt = xs_[0].shape[0] // TM
    row = lambda i: (i, 0)
    const = lambda i: (0, 0)
    n_o = len(os_)
    in_specs = ((_token_specs(npt, d) if len(xs_) == 2 else [pl.BlockSpec((TM, d), row)])
                + [pl.BlockSpec((TM, o.shape[1]), row) for o in os_]
                + [pl.BlockSpec(w.shape, const) for w in ws]
                + [_mod_spec(GATE_MIX), pl.BlockSpec((1, d), const),
                   _mod_spec(SHIFT_FFN), _mod_spec(SCALE_FFN),
                   pl.BlockSpec(wr.shape, const), pl.BlockSpec(br.shape, const),
                   pl.BlockSpec((TM, TM), const)])
    tri = jnp.asarray(np.triu(np.ones((TM, TM), np.float32), k=1), dtype=BF16)
    return pl.pallas_call(
        functools.partial(_outproj_kernel, n_x=len(xs_), n_o=n_o, n_prompt_tiles=npt),
        grid=(n // TM,),
        in_specs=in_specs,
        out_specs=[pl.BlockSpec((TM, d), row), pl.BlockSpec((DISP_SLABS, TM, 128), lambda i: (0, i, 0)),
                   pl.BlockSpec((8, TM), lambda i: (0, i)), pl.BlockSpec((BUCKET_ROWS, 128), const)],
        out_shape=[jax.ShapeDtypeStruct((n, d), F32), jax.ShapeDtypeStruct((DISP_SLABS, n_pad, 128), jnp.int32),
                   jax.ShapeDtypeStruct((8, n), jnp.int32), jax.ShapeDtypeStruct((BUCKET_ROWS, 128), F32)],
        scratch_shapes=[pltpu.VMEM((BUCKET_ROWS, 128), F32)],
        compiler_params=_cparams(("arbitrary",)),
        name="outproj_router",
    )(*xs_, *os_, *ws, mods, nf, mods, mods, wr, br, tri)


def _sc_mesh():
    return plsc.VectorSubcoreMesh(core_axis_name="core", subcore_axis_name="subcore")


def _sc_row_copy(src, idx, n_out, scatter):
    r = idx.shape[0]
    k = SC_GROUP
    w_per = r // (SC_WINDOW * SC_WORKERS)
    assert r % (SC_WINDOW * SC_WORKERS) == 0 and w_per % k == 0
    n_groups = w_per // k

    @functools.partial(
        pl.kernel, out_type=jax.ShapeDtypeStruct((n_out, 128), src.dtype), mesh=_sc_mesh(),
        scratch_types=[pltpu.VMEM((w_per, SC_WINDOW), jnp.int32),
                       pltpu.VMEM((2 * k, SC_WINDOW, 128), src.dtype),
                       pltpu.SemaphoreType.DMA((2,)), pltpu.SemaphoreType.DMA((2,))])
    def copy(x_hbm, i_hbm, o_hbm, ibuf, xbuf, in_sem, out_sem):
        wid = lax.axis_index("core") * (SC_WORKERS // 2) + lax.axis_index("subcore")
        pltpu.sync_copy(i_hbm.at[wid], ibuf)
        first = wid * w_per

        def rows(j):
            return pl.ds((first + j) * SC_WINDOW, SC_WINDOW)

        def start_in(g, slot):
            cps = []
            for c in range(k):
                j = g * k + c
                s = x_hbm.at[rows(j)] if scatter else x_hbm.at[ibuf.at[j]]
                cps.append(pltpu.async_copy(s, xbuf.at[slot * k + c], in_sem.at[slot]))
            return cps

        def start_out(g, slot):
            cps = []
            for c in range(k):
                j = g * k + c
                dst = o_hbm.at[ibuf.at[j]] if scatter else o_hbm.at[rows(j)]
                cps.append(pltpu.async_copy(xbuf.at[slot * k + c], dst, out_sem.at[slot]))
            return cps

        pending_in = start_in(0, 0)
        for g in range(n_groups):
            slot = g % 2
            for cp in pending_in:
                cp.wait()
            pending_out = start_out(g, slot)
            if g + 1 < n_groups:
                pending_in = start_in(g + 1, 1 - slot)
            for cp in pending_out:
                cp.wait()

    return copy(src, idx.reshape(SC_WORKERS, w_per, SC_WINDOW))


def _sc_scatter_rows(src, idx, n_out):
    assert idx.shape == (src.shape[0],)
    return _sc_row_copy(src, idx, n_out, scatter=True)


def _moe_kernel(elo_ref, ehi_ref, nvalid_ref, xs_ref, gu_lo_ref, gu_hi_ref, dn_lo_ref, dn_hi_ref, y_ref, tok_ref,
                *, n_tok, dump_tiles):
    i = pl.program_id(0)
    t = xs_ref.shape[1]
    aux = pltpu.bitcast(xs_ref[Y_SLABS], F32)
    r = lax.broadcasted_iota(jnp.int32, (1, t), 1)
    spare = n_tok + (i % dump_tiles) * t + r
    tok = jnp.where(r < nvalid_ref[i], aux.T[2:3, :].astype(jnp.int32), spare)
    for c in range(t // 128):
        tok_ref[0, c:c + 1, :] = tok[:, 128 * c:128 * (c + 1)]

    @pl.when(nvalid_ref[i] > 0)
    def _():
        h = _unpack_pairs([xs_ref[s] for s in range(Y_SLABS)], BF16)
        abs_ = [_dot(h, gu_ref[0, 0]) for gu_ref in (gu_lo_ref, gu_hi_ref)]
        acts = [(ab[:, :D_FF] * _sigmoid(ab[:, :D_FF]) * ab[:, D_FF:]).astype(BF16) for ab in abs_]
        ys = [_dot(act, dn_ref[0, 0]) for act, dn_ref in zip(acts, (dn_lo_ref, dn_hi_ref))]
        acc = aux[:, 0:1] * ys[0] + aux[:, 1:2] * ys[1]
        for s, slab in enumerate(_pack_pairs(acc)):
            y_ref[s] = slab

    @pl.when(nvalid_ref[i] == 0)
    def _():
        y_ref[...] = jnp.zeros_like(y_ref)


def _moe_call(xs, elo, ehi, nvalid, wgu, wdn, layer, n_tiles, n_tok, dump_tiles):
    d = wgu.shape[2]
    gu = lambda sel: pl.BlockSpec((1, 1, d, 2 * D_FF), lambda i, lo, hi, v: (layer, (lo, hi)[sel][i], 0, 0))
    dn = lambda sel: pl.BlockSpec((1, 1, D_FF, d), lambda i, lo, hi, v: (layer, (lo, hi)[sel][i], 0, 0))
    return pl.pallas_call(
        functools.partial(_moe_kernel, n_tok=n_tok, dump_tiles=dump_tiles),
        grid_spec=pltpu.PrefetchScalarGridSpec(
            num_scalar_prefetch=3,
            grid=(n_tiles,),
            in_specs=[pl.BlockSpec((DISP_SLABS, TMO, 128), lambda i, lo, hi, v: (0, i, 0)),
                      gu(0), gu(1), dn(0), dn(1)],
            out_specs=[pl.BlockSpec((Y_SLABS, TMO, 128), lambda i, lo, hi, v: (0, i, 0)),
                       pl.BlockSpec((1, TMO // 128, 128), lambda i, lo, hi, v: (i, 0, 0))]),
        out_shape=[jax.ShapeDtypeStruct((Y_SLABS, n_tiles * TMO, 128), jnp.int32),
                   jax.ShapeDtypeStruct((n_tiles, TMO // 128, 128), jnp.int32)],
        compiler_params=_cparams(("arbitrary",)),
        name="moe_grouped",
    )(elo, ehi, nvalid, xs, wgu, wgu, wdn, wdn)


def _moe_layer(disp, meta, counts, wgu, wdn, layer, n, n_pad, sort_rows):
    n_tiles = sort_rows // TMO
    cnt = counts[:N_BUCKETS, 0].astype(jnp.int32)
    padded = ((cnt + TMO - 1) // TMO) * TMO
    ends = jnp.cumsum(padded)
    offs = ends - padded
    bucket, rank = meta[0], meta[1]
    pos = rank + jnp.sum(jnp.where(bucket[None, :] == jnp.arange(N_BUCKETS, dtype=jnp.int32)[:, None],
                                   offs[:, None], 0), axis=0)
    tile_start = jnp.arange(n_tiles, dtype=jnp.int32) * TMO
    tile_bucket = jnp.minimum(jnp.sum((tile_start[:, None] >= ends[None, :]).astype(jnp.int32), axis=1), N_BUCKETS - 1)
    pair_lo = np.array([0, 0, 0, 1, 1, 2], np.int32)
    pair_hi = np.array([1, 2, 3, 2, 3, 3], np.int32)
    b_lo = jnp.asarray(np.repeat(np.arange(N_GROUPS), N_PAIRS) * EXP_PER_GROUP + np.tile(pair_lo, N_GROUPS), jnp.int32)
    b_hi = jnp.asarray(np.repeat(np.arange(N_GROUPS), N_PAIRS) * EXP_PER_GROUP + np.tile(pair_hi, N_GROUPS), jnp.int32)
    onehot_tb = (tile_bucket[:, None] == jnp.arange(N_BUCKETS, dtype=jnp.int32)[None, :]).astype(jnp.int32)
    elo = jnp.sum(onehot_tb * b_lo[None, :], axis=1)
    ehi = jnp.sum(onehot_tb * b_hi[None, :], axis=1)
    bucket_end = jnp.sum(onehot_tb * (offs + cnt)[None, :], axis=1)
    nvalid = jnp.where(tile_start < ends[-1], jnp.clip(bucket_end - tile_start, 0, TMO), 0)
    dump = sort_rows + jnp.arange(n_pad - n, dtype=jnp.int32)
    pos_sc = jnp.concatenate([pos, dump])
    total = sort_rows + n_pad - n
    sc_idx = (pos_sc[None, :] + (jnp.arange(DISP_SLABS, dtype=jnp.int32) * total)[:, None]).reshape(-1)
    xs = _sc_scatter_rows(disp.reshape(DISP_SLABS * n_pad, 128), sc_idx, DISP_SLABS * total)
    ys, tok = _moe_call(xs.reshape(DISP_SLABS, total, 128), elo, ehi, nvalid, wgu, wdn, layer, n_tiles,
                        n, (n_pad - n) // TMO)
    back_idx = (tok.reshape(1, sort_rows) + (jnp.arange(Y_SLABS, dtype=jnp.int32) * n_pad)[:, None]).reshape(-1)
    z = _sc_scatter_rows(ys.reshape(Y_SLABS * sort_rows, 128), back_idx, Y_SLABS * n_pad)
    return z.reshape(Y_SLABS, n_pad, 128)


def _final_kernel(xn_ref, z_ref, gate_ref, g_ref, yp_ref, ys_ref, *, n_prompt_tiles):
    x = _add_moe(xn_ref, z_ref, gate_ref)
    ms = jnp.mean(x * x, axis=-1, keepdims=True)
    y = x * lax.rsqrt(ms + EPS) * g_ref[...]
    i = pl.program_id(0)

    @pl.when(i < n_prompt_tiles)
    def _():
        yp_ref[...] = y

    @pl.when(i >= n_prompt_tiles)
    def _():
        ys_ref[...] = y


def _final_call(xn, z, mods, g, n_prompt):
    n, d = xn.shape
    npt = n_prompt // TM
    assert n - n_prompt == TM
    return pl.pallas_call(
        functools.partial(_final_kernel, n_prompt_tiles=npt),
        grid=(n // TM,),
        in_specs=[pl.BlockSpec((TM, d), lambda i: (i, 0)), pl.BlockSpec((z.shape[0], TM, 128), lambda i: (0, i, 0)),
                  _mod_spec(GATE_FFN), pl.BlockSpec((1, d), lambda i: (0, 0))],
        out_specs=_token_specs(npt, d),
        out_shape=[jax.ShapeDtypeStruct((n_prompt, d), F32), jax.ShapeDtypeStruct((TM, d), F32)],
        compiler_params=_cparams(("arbitrary",)),
        name="final_norm",
    )(xn, z, mods, g)


def kernel(x_prompt, x_sample, c_prompt, c_sample, state_gla, cache_band_k, cache_band_v, cache_swa_k, cache_swa_v,
           w_ada, b_ada, norm_mix, norm_ffn, norm_final, w_in_even, w_gate_a, b_gate_a, gla_norm, rel_bias_b,
           w_out_even, w_in_odd, sinks_c, w_out_odd, w_router, b_router, w_gate_up, w_down):
    bp, lp, d = x_prompt.shape
    bs, ls_, _ = x_sample.shape
    n_p, n_s = bp * lp, bs * ls_
    n = n_p + n_s
    assert ls_ == CHUNK and n_s == TM and lp % TM == 0 and PAST_LEN % CHUNK == 0

    xp2, xs2 = x_prompt.reshape(n_p, d), x_sample.reshape(n_s, d)

    c16 = jnp.zeros((SEQ_ROWS, d), F32).at[:bp].set(c_prompt).at[bp:bp + bs].set(c_sample)
    mods = _ada_call(c16, w_ada, b_ada)
    seq_of_group = np.concatenate([np.repeat(np.arange(bp), lp // CHUNK), bp + np.arange(bs)])
    mods_g = [mods[l][seq_of_group] for l in range(DEPTH)]

    perm = np.array([4 * (c % 4) + c // 4 for c in range(N_EXPERTS)])
    wr = jnp.zeros((d, 128), F32).at[:, :N_EXPERTS].set(w_router[:, perm])
    br = jnp.zeros((1, 128), F32).at[0, :N_EXPERTS].set(b_router[perm])

    wgu = w_gate_up.astype(BF16)
    wdn = w_down.astype(BF16)

    sc_unit = SC_WINDOW * SC_WORKERS * SC_GROUP
    n_pad = n + TMO
    while (DISP_SLABS * n_pad) % sc_unit or (Y_SLABS * n_pad) % TMO or (n_pad - n) % TMO:
        n_pad += TMO
    sort_rows = n + N_BUCKETS * TMO
    while (Y_SLABS * sort_rows) % sc_unit:
        sort_rows += TMO

    gla_p = gla_s = bk_p = bv_p = bk_s = bv_s = sk_p = sv_p = sk_s = sv_s = None
    xn = z = None
    for l in range(DEPTH):
        i = l // 2
        if l % 2 == 0:
            w = w_in_even[i]
            w_main = jnp.concatenate([w[:, :1536], w[:, 1552:]], axis=1).astype(BF16)
            w_la = jnp.zeros((d, 128), F32).at[:, :GATE_RANK].set(w[:, 1536:1552]).astype(BF16)
            w_gate = jnp.zeros((128, HA * DKA), F32).at[:GATE_RANK].set(w_gate_a[i])
            qa, ka, va, ra, qb, kb, vb, ga = _inproj_even_call(
                xp2, xs2, mods_g[l], norm_mix[l][None], w_main, w_la, w_gate, b_gate_a[i][None])
            xres = [xp2, xs2]
            gn = gla_norm[i][None]
            oa, s_p = _gla_call(qa, ka, va, ga, ra, jnp.zeros((bp, 256, 128), F32), gn, None,
                                n_seq=bp, seq_rows=lp, row0=0, nb=8)
            oa, s_s = _gla_call(qa, ka, va, ga, ra, state_gla[i].reshape(bs, 256, 128), gn, oa,
                                n_seq=bs, seq_rows=ls_, row0=n_p, nb=1)
            gla_p, gla_s = s_p.reshape(1, bp, HA, DKA, DVA), s_s.reshape(1, bs, HA, DKA, DVA)
            pb = N_PREV_B * CHUNK
            tq, g = 512, 2
            ck = cache_band_k[i].reshape(bs * pb, HB * DHB).astype(BF16)
            cv = cache_band_v[i].reshape(bs * pb, HB * DHB).astype(BF16)
            biases = (_band_bias(rel_bias_b[i], g, pb, _band_valid(g, pb)),
                      _band_bias(rel_bias_b[i], g, pb, _band_valid(g, pb, tq // (CHUNK * g))),
                      _band_bias(rel_bias_b[i], 1, pb, _band_valid(1, pb)))
            ob = _attention(_band_kernel, qb, kb, vb, ck, cv, biases, [], [], width=512, kv_width=512, pb=pb,
                            tq=tq, g=g, bp=bp, lp=lp, bs=bs, name="band")
            tail = lambda a: jnp.stack([a[(b + 1) * lp - pb:(b + 1) * lp] for b in range(bp)]).astype(F32).reshape(1, bp, pb, HB, DHB)
            new = lambda a: a[n_p:].astype(F32).reshape(bs, ls_, HB, DHB)
            bk_p, bv_p = tail(kb), tail(vb)
            bk_s = jnp.concatenate([cache_band_k[i][:, ls_:], new(kb)], axis=1)[None]
            bv_s = jnp.concatenate([cache_band_v[i][:, ls_:], new(vb)], axis=1)[None]
            wo = w_out_even[i].astype(BF16)
            os_, ws = [oa, ob], [wo[:HA * DVA], wo[HA * DVA:]]
        else:
            w = w_in_odd[i]
            wk, wv = w[:, 1024:1152], w[:, 1152:1280]
            dup = lambda a: jnp.concatenate([a[:, :64], a[:, :64], a[:, 64:], a[:, 64:]], axis=1)
            w_all = jnp.concatenate([w[:, :1024], dup(wk), dup(wv)], axis=1).astype(BF16)
            cos, sin, rope_map = _rope_tables(lp, ls_, bp, bs)
            x, q, k, v = _inproj_odd_call(xn, z, mods_g[l - 1], mods_g[l], norm_mix[l][None], cos, sin, rope_map, w_all)
            xres = [x]
            pb = WINDOW
            tq, g = 512, 2
            sink = sinks_c[i][None] * LOG2E
            sink_spec = [pl.BlockSpec(memory_space=pltpu.SMEM)]
            dupc = lambda c: jnp.concatenate([c[:, :, 0], c[:, :, 0], c[:, :, 1], c[:, :, 1]], axis=-1).reshape(bs * pb, 256).astype(BF16)
            ck, cv = dupc(cache_swa_k[i]), dupc(cache_swa_v[i])
            additive = lambda valid: jnp.asarray(np.where(valid, 0.0, -np.inf), F32)
            masks = (additive(_band_valid(g, pb)), additive(_band_valid(g, pb, tq // (CHUNK * g))),
                     additive(_band_valid(1, pb)))
            o = _attention(_swa_kernel, q, k, v, ck, cv, masks, [sink], sink_spec, width=1024, kv_width=256, pb=pb,
                           tq=tq, g=g, bp=bp, lp=lp, bs=bs, name="swa")
            undup = lambda a: jnp.concatenate([a[:, 0:64], a[:, 128:192]], axis=1).astype(F32)
            tail = lambda a: jnp.stack([undup(a[(b + 1) * lp - pb:(b + 1) * lp]) for b in range(bp)]).reshape(1, bp, pb, KVC, DHC)
            new = lambda a: undup(a[n_p:]).reshape(bs, ls_, KVC, DHC)
            sk_p, sv_p = tail(k), tail(v)
            sk_s = jnp.concatenate([cache_swa_k[i][:, ls_:], new(k)], axis=1)[None]
            sv_s = jnp.concatenate([cache_swa_v[i][:, ls_:], new(v)], axis=1)[None]
            os_, ws = [o], [w_out_odd[i].astype(BF16)]
        xn, disp, meta, counts = _outproj_call(xres, os_, ws, mods_g[l], norm_ffn[l][None], wr, br, n_pad)
        z = _moe_layer(disp, meta, counts, wgu, wdn, l, n, n_pad, sort_rows)

    y_prompt, y_sample = _final_call(xn, z, mods_g[DEPTH - 1], norm_final[None], n_p)
    return (y_prompt.reshape(bp, lp, d), y_sample.reshape(bs, ls_, d),
            gla_p, gla_s, bk_p, bv_p, bk_s, bv_s, sk_p, sv_p, sk_s, sv_s)
```

```python
import functools

import numpy as np
import jax
import jax.numpy as jnp
from jax import lax
from jax.experimental import pallas as pl
from jax.experimental.pallas import tpu as pltpu
from jax.experimental.pallas import tpu_sc as plsc

F32 = jnp.float32
BF16 = jnp.bfloat16

D_MODEL = 1024
DEPTH = 2
CHUNK = 64
PAST_LEN = 4096
HA, DKA, DVA = 4, 64, 128
GATE_RANK = 16
GATE_TAU = 16.0
HB, DHB = 8, 64
N_PREV_B = 8
MAX_REL = 128
HC, KVC, DHC = 16, 2, 64
WINDOW = 128
ROPE_THETA = 10000.0
N_EXPERTS = 16
N_GROUPS = 4
EXP_PER_GROUP = 4
D_FF = 512
EPS = 1e-6

N_PAIRS = 6
N_BUCKETS = N_GROUPS * N_PAIRS
BUCKET_ROWS = 32
Y_SLABS = 4
DISP_SLABS = Y_SLABS + 1
TMO = 256
SC_WINDOW = 128
SC_WORKERS = 32
SC_GROUP = 3

TM = 512
SEQ_ROWS = 16
SUB = 16
LOG2E = 1.4426950408889634


def _cparams(sem, vmem_mb=48):
    return pltpu.CompilerParams(dimension_semantics=sem, vmem_limit_bytes=vmem_mb * 1024 * 1024)


def _dot(a, b):
    return jnp.dot(a, b, preferred_element_type=F32)


def _dot_nt(a, b):
    return lax.dot_general(a, b, (((1,), (1,)), ((), ())), preferred_element_type=F32)


def _split(a):
    hi = a.astype(BF16)
    lo = (a - hi.astype(F32)).astype(BF16)
    return hi, lo


def _dot3(a, b):
    ah, al = _split(a)
    bh, bl = _split(b)
    return _dot(ah, bh) + _dot(ah, bl) + _dot(al, bh)


def _sigmoid(x):
    return 1.0 / (1.0 + jnp.exp(-x))


def _group_affine(y, mul, add):
    parts = []
    for gi in range(y.shape[0] // CHUNK):
        p = y[gi * CHUNK:(gi + 1) * CHUNK]
        if mul is not None:
            p = p * mul[gi:gi + 1]
        if add is not None:
            p = p + add[gi:gi + 1]
        parts.append(p)
    return jnp.concatenate(parts, axis=0)


def _norm_mod(x, g, shift, scale):
    ms = jnp.mean(x * x, axis=-1, keepdims=True)
    return _group_affine(x * lax.rsqrt(ms + EPS) * g, 1.0 + scale, shift)


def _mod_spec(part):
    return pl.BlockSpec((TM // CHUNK, D_MODEL), lambda i: (i, part))


SHIFT_MIX, SCALE_MIX, GATE_MIX, SHIFT_FFN, SCALE_FFN, GATE_FFN = range(6)


def _on_token_tile(xp_ref, xs_ref, n_prompt_tiles, body):
    @pl.when(pl.program_id(0) < n_prompt_tiles)
    def _():
        body(xp_ref)

    @pl.when(pl.program_id(0) >= n_prompt_tiles)
    def _():
        body(xs_ref)


def _token_specs(n_prompt_tiles, d):
    return [pl.BlockSpec((TM, d), lambda i: (jnp.minimum(i, n_prompt_tiles - 1), 0)),
            pl.BlockSpec((TM, d), lambda i: (0, 0))]


def _ada_kernel(c_ref, w_ref, b_ref, o_ref):
    c = c_ref[...]
    o_ref[0] = _dot3(c * _sigmoid(c), w_ref[0]) + b_ref[0]


def _ada_call(c16, w_ada, b_ada):
    d = D_MODEL
    tn = 1024
    return pl.pallas_call(
        _ada_kernel,
        grid=(DEPTH, 6 * d // tn),
        in_specs=[pl.BlockSpec((SEQ_ROWS, d), lambda l, j: (0, 0)),
                  pl.BlockSpec((1, d, tn), lambda l, j: (l, 0, j)),
                  pl.BlockSpec((1, 1, tn), lambda l, j: (l, 0, j))],
        out_specs=pl.BlockSpec((1, SEQ_ROWS, tn), lambda l, j: (l, 0, j)),
        out_shape=jax.ShapeDtypeStruct((DEPTH, SEQ_ROWS, 6 * d), F32),
        compiler_params=_cparams(("arbitrary", "arbitrary")),
        name="ada",
    )(c16, w_ada, b_ada.reshape(DEPTH, 1, 6 * d))


def _inproj_even_kernel(xp_ref, xs_ref, sh_ref, sc_ref, g_ref, w_ref, wla_ref, wg_ref, bg_ref,
                        qa_ref, ka_ref, va_ref, ra_ref, qb_ref, kb_ref, vb_ref, ga_ref, *, n_prompt_tiles):
    def body(x_ref):
        hb = _norm_mod(x_ref[...], g_ref[...], sh_ref[...], sc_ref[...]).astype(BF16)
        outs = ((qa_ref, 0, 256, DKA ** -0.5), (ka_ref, 256, 512, None), (va_ref, 512, 1024, None),
                (ra_ref, 1024, 1536, None), (qb_ref, 1536, 2048, DHB ** -0.5 * LOG2E), (kb_ref, 2048, 2560, None),
                (vb_ref, 2560, 3072, None))
        zs = [_dot(hb, w_ref[:, lo:hi]) for _, lo, hi, _ in outs]
        la = _dot(hb, wla_ref[...])
        for z, (o_ref, _, _, scale) in zip(zs, outs):
            o_ref[...] = (z if scale is None else z * scale).astype(BF16)
        gl = _dot3(la, wg_ref[...]) + bg_ref[...]
        ga_ref[...] = -(jnp.maximum(-gl, 0.0) + jnp.log(1.0 + jnp.exp(-jnp.abs(gl)))) * (1.0 / GATE_TAU)

    _on_token_tile(xp_ref, xs_ref, n_prompt_tiles, body)


def _inproj_even_call(xp, xs, mods, g, w_main, w_la, w_gate, b_gate):
    d = xp.shape[1]
    npt = xp.shape[0] // TM
    n = xp.shape[0] + xs.shape[0]
    row = lambda i: (i, 0)
    const = lambda i: (0, 0)
    widths = (256, 256, 512, 512, 512, 512, 512)
    out_shape = [jax.ShapeDtypeStruct((n, w), BF16) for w in widths] + [jax.ShapeDtypeStruct((n, 256), F32)]
    out_specs = [pl.BlockSpec((TM, w), row) for w in widths] + [pl.BlockSpec((TM, 256), row)]
    return pl.pallas_call(
        functools.partial(_inproj_even_kernel, n_prompt_tiles=npt),
        grid=(n // TM,),
        in_specs=_token_specs(npt, d) + [
            _mod_spec(SHIFT_MIX), _mod_spec(SCALE_MIX),
            pl.BlockSpec((1, d), const),
            pl.BlockSpec(w_main.shape, const), pl.BlockSpec(w_la.shape, const),
            pl.BlockSpec(w_gate.shape, const), pl.BlockSpec(b_gate.shape, const)],
        out_specs=out_specs, out_shape=out_shape,
        compiler_params=_cparams(("parallel",)),
        name="inproj_even",
    )(xp, xs, mods, mods, g, w_main, w_la, w_gate, b_gate)


def _rope(x, cos, sin_signed):
    t, w = x.shape
    lane = lax.broadcasted_iota(jnp.int32, (1, w), 1)
    first_half = (lane & 63) < 32
    rot = jnp.where(first_half, pltpu.roll(x, w - 32, 1), pltpu.roll(x, 32, 1))
    reps = w // 128
    return x * jnp.tile(cos, (1, reps)) + rot * jnp.tile(sin_signed, (1, reps))


def _unpack_pairs(slabs, dtype):
    lo = [pltpu.bitcast(s << 16, F32) for s in slabs]
    hi = [pltpu.bitcast(s & jnp.int32(-65536), F32) for s in slabs]
    return jnp.concatenate(lo + hi, axis=1).astype(dtype)


def _pack_pairs(x):
    bits = pltpu.bitcast(x.astype(BF16).astype(F32), jnp.int32)
    half = x.shape[1] // 2
    packed = ((bits[:, :half] >> 16) & jnp.int32(0xFFFF)) | (bits[:, half:] & jnp.int32(-65536))
    return [packed[:, 128 * s:128 * (s + 1)] for s in range(half // 128)]


def _add_moe(xn_ref, z_ref, gate_ref):
    y = _unpack_pairs([z_ref[s] for s in range(z_ref.shape[0])], F32)
    return xn_ref[...] + _group_affine(y, gate_ref[...], None)


def _rope_tables(lp, ls_, bp, bs):
    assert PAST_LEN + ls_ <= lp and lp % 128 == 0 and bs * ls_ == TM
    half = DHC // 2
    inv = ROPE_THETA ** (-jnp.arange(half, dtype=F32) / half)
    inv = jnp.tile(inv, 128 // half)
    sign = jnp.asarray(np.tile(np.repeat([-1.0, 1.0], half), 128 // DHC), F32)
    a = jnp.asarray(np.arange(lp // 128) * 128, F32)[:, None] * inv[None, :]
    b = jnp.asarray(np.arange(128), F32)[:, None] * inv[None, :]
    ca, sa, cb, sb = jnp.cos(a)[:, None], jnp.sin(a)[:, None], jnp.cos(b)[None], jnp.sin(b)[None]
    cos = (ca * cb - sa * sb).reshape(lp, 128)
    sin = ((sa * cb + ca * sb) * sign).reshape(lp, 128)
    with_sample = lambda t: jnp.concatenate([t, jnp.tile(t[PAST_LEN:PAST_LEN + ls_], (bs, 1))], axis=0)
    tiles = lp // TM
    return with_sample(cos), with_sample(sin), lambda i: (jnp.where(i < bp * tiles, i % tiles, tiles), 0)


def _inproj_odd_kernel(xn_ref, z_ref, gate_ref, sh_ref, sc_ref, g_ref, cos_ref, sin_ref, w_ref,
                       x_ref, q_ref, k_ref, v_ref):
    x = _add_moe(xn_ref, z_ref, gate_ref)
    x_ref[...] = x
    hb = _norm_mod(x, g_ref[...], sh_ref[...], sc_ref[...]).astype(BF16)
    cos, sin = cos_ref[...], sin_ref[...]
    q = _rope(_dot(hb, w_ref[:, 0:1024]), cos, sin)
    q_ref[...] = (q * (DHC ** -0.5 * LOG2E)).astype(BF16)
    k_ref[...] = _rope(_dot(hb, w_ref[:, 1024:1280]), cos, sin).astype(BF16)
    v_ref[...] = _dot(hb, w_ref[:, 1280:1536]).astype(BF16)


def _inproj_odd_call(xn, z, mods_prev, mods, g, cos, sin, rope_map, w):
    n, d = xn.shape
    row = lambda i: (i, 0)
    const = lambda i: (0, 0)
    widths = (1024, 256, 256)
    return pl.pallas_call(
        _inproj_odd_kernel,
        grid=(n // TM,),
        in_specs=[pl.BlockSpec((TM, d), row), pl.BlockSpec((z.shape[0], TM, 128), lambda i: (0, i, 0)),
                  _mod_spec(GATE_FFN), _mod_spec(SHIFT_MIX), _mod_spec(SCALE_MIX),
                  pl.BlockSpec((1, d), const),
                  pl.BlockSpec((TM, 128), rope_map), pl.BlockSpec((TM, 128), rope_map),
                  pl.BlockSpec(w.shape, const)],
        out_specs=[pl.BlockSpec((TM, d), row)] + [pl.BlockSpec((TM, wd), row) for wd in widths],
        out_shape=[jax.ShapeDtypeStruct((n, d), F32)] + [jax.ShapeDtypeStruct((n, wd), BF16) for wd in widths],
        compiler_params=_cparams(("parallel",)),
        name="inproj_odd",
    )(xn, z, mods_prev, mods, mods, g, cos, sin, w)


def _gla_tri():
    t = np.arange(CHUNK)[:, None]
    s = np.arange(CHUNK)[None, :]
    cum = s <= t
    start = s < (t // SUB) * SUB
    end = s < (t // SUB + 1) * SUB
    return jnp.asarray(np.concatenate([cum, start, end], axis=0).astype(np.float32), dtype=BF16)


def _gla_kernel(q_ref, k_ref, v_ref, g_ref, r_ref, s0_ref, gn_ref, tri_ref, o_ref, sout_ref, s_ref, *, nb):
    c_ = CHUNK
    nsub = c_ // SUB

    @pl.when(pl.program_id(1) == 0)
    def _():
        s_ref[...] = s0_ref[0]

    tri = tri_ref[...]
    lane = lax.broadcasted_iota(jnp.int32, (1, 128), 1)
    hmask = [jnp.where(lane < DKA, 1.0, 0.0), jnp.where(lane >= DKA, 1.0, 0.0)]
    ti = lax.broadcasted_iota(jnp.int32, (c_, c_), 0)
    si = lax.broadcasted_iota(jnp.int32, (c_, c_), 1)
    rb, cb = ti >> 4, si >> 4
    m_diag = (rb == cb) & (si <= ti)
    m_off = [(cb == j) & (rb > j) for j in range(nsub - 1)]
    hk = HA * DKA
    eye = lax.broadcasted_iota(jnp.int32, (hk, hk), 0) == lax.broadcasted_iota(jnp.int32, (hk, hk), 1)
    gn = gn_ref[...]

    chunks = range(nb)
    heads = [(p, hh) for p in range(HA // 2) for hh in range(2)]
    rows = [slice(c * c_, (c + 1) * c_) for c in chunks]
    pair = [slice(128 * p, 128 * (p + 1)) for p in range(HA // 2)]
    css = []
    for c in chunks:
        g_hi, g_lo = _split(g_ref[rows[c], :])
        css.append(_dot(tri, g_hi) + _dot(tri, g_lo))
    lhs1, lhs2, kds, kes, q_inter, klts, dcols = [], [], [], [], [], [], []
    for c in chunks:
        b, rs, re = css[c][0:c_], css[c][c_:2 * c_], css[c][2 * c_:3 * c_]
        q = q_ref[rows[c], :].astype(F32)
        k = k_ref[rows[c], :].astype(F32)
        bl = b[c_ - 1:c_, :]
        qd = q * jnp.exp(b - rs)
        kd = k * jnp.exp(rs - b)
        ke = k * jnp.exp(re - b)
        qi = q * jnp.exp(b)
        kl = k * jnp.exp(bl - b)
        ql = [q * jnp.exp(jnp.minimum(b - b[SUB * (j + 1) - 1:SUB * (j + 1), :], 0.0)) for j in range(nsub - 1)]
        dcols.append(jnp.sum(jnp.where(eye, jnp.broadcast_to(jnp.exp(bl), (hk, hk)), 0.0), axis=1, keepdims=True))
        kds.append([kd[:, ls].astype(BF16) for ls in pair])
        kes.append([ke[:, ls].astype(BF16) for ls in pair])
        klts.append([kl[:, ls].T.astype(BF16) for ls in pair])
        lhs1.append([(qd[:, pair[p]] * hmask[hh]).astype(BF16) for p, hh in heads])
        lhs2.append([jnp.concatenate([ql[j][:, pair[p]] * hmask[hh] for j in range(nsub - 1)], axis=0).astype(BF16)
                     for p, hh in heads])
        q_inter.append([(qi[:, pair[p]] * hmask[hh]).astype(BF16) for p, hh in heads])
    a1s = [[_dot_nt(lhs1[c][h], kds[c][p]) for h, (p, hh) in enumerate(heads)] for c in chunks]
    a2s = [[_dot_nt(lhs2[c][h], kes[c][p]) for h, (p, hh) in enumerate(heads)] for c in chunks]
    atts = []
    for c in chunks:
        per_head = []
        for h in range(HA):
            att = jnp.zeros((c_, c_), F32)
            for j in reversed(range(nsub - 1)):
                att = jnp.where(m_off[j], a2s[c][h][j * c_:(j + 1) * c_], att)
            per_head.append(jnp.where(m_diag, a1s[c][h], att).astype(BF16))
        atts.append(per_head)
    vs_ = [[v_ref[rows[c], DVA * h:DVA * (h + 1)] for h in range(HA)] for c in chunks]
    o_intra = [[_dot(atts[c][h], vs_[c][h]) for h in range(HA)] for c in chunks]
    upds = [jnp.concatenate([_dot(klts[c][p][DKA * hh:DKA * (hh + 1)], vs_[c][2 * p + hh]) for p, hh in heads], axis=0)
            for c in chunks]

    s_cur = s_ref[...]
    s_in = []
    for c in chunks:
        s_in.append(s_cur.astype(BF16))
        s_cur = dcols[c] * s_cur + upds[c]
    s_ref[...] = s_cur
    sout_ref[0] = s_cur

    for c in chunks:
        for h in range(HA):
            o = o_intra[c][h] + _dot(q_inter[c][h], s_in[c][pair[h // 2], :])
            ms = jnp.mean(o * o, axis=-1, keepdims=True)
            vs = slice(DVA * h, DVA * (h + 1))
            rr = r_ref[rows[c], vs].astype(F32)
            o_ref[rows[c], vs] = (o * lax.rsqrt(ms + EPS) * gn * (rr * _sigmoid(rr))).astype(BF16)


def _gla_call(q, k, v, g, r, s0, gn, o_prev, *, n_seq, seq_rows, row0, nb):
    tq = nb * CHUNK
    steps = seq_rows // tq
    blk0 = row0 // tq
    row = lambda b, j: (blk0 + b * steps + j, 0)
    const = lambda b, j: (0, 0)
    tri = _gla_tri()
    in_specs = [pl.BlockSpec((tq, 256), row), pl.BlockSpec((tq, 256), row), pl.BlockSpec((tq, 512), row),
                pl.BlockSpec((tq, 256), row), pl.BlockSpec((tq, 512), row),
                pl.BlockSpec((1, 256, 128), lambda b, j: (b, 0, 0)),
                pl.BlockSpec((1, 128), const), pl.BlockSpec(tri.shape, const)]
    args = [q, k, v, g, r, s0, gn, tri]
    aliases = {}
    if o_prev is not None:
        in_specs.append(pl.BlockSpec(memory_space=pl.ANY))
        args.append(o_prev)
        aliases = {len(args) - 1: 0}
    kern = functools.partial(_gla_kernel, nb=nb)
    if o_prev is not None:
        kern = _drop_arg(kern, 8)
    return pl.pallas_call(
        kern,
        grid=(n_seq, steps),
        in_specs=in_specs,
        out_specs=[pl.BlockSpec((tq, 512), row), pl.BlockSpec((1, 256, 128), lambda b, j: (b, 0, 0))],
        out_shape=[jax.ShapeDtypeStruct((q.shape[0], 512), BF16), jax.ShapeDtypeStruct((n_seq, 256, 128), F32)],
        scratch_shapes=[pltpu.VMEM((256, 128), F32)],
        input_output_aliases=aliases,
        compiler_params=_cparams(("arbitrary", "arbitrary")),
        name="gla",
    )(*args)


def _drop_arg(fn, idx):
    def wrapped(*refs):
        return fn(*refs[:idx], *refs[idx + 1:])
    return wrapped


def _window(prev_ref, cur_ref, lo, hi, pb, ls):
    if lo < pb:
        return jnp.concatenate([prev_ref[lo:pb, ls], cur_ref[0:hi - pb, ls]], axis=0)
    return cur_ref[lo - pb:hi - pb, ls]


def _band_kernel(q_ref, kp_ref, kc_ref, vp_ref, vc_ref, bias_ref, o_ref, *, g, n_sub, pb):
    qs = CHUNK * g
    kw_rows = pb + qs
    lane = lax.broadcasted_iota(jnp.int32, (1, 128), 1)
    low = lane < DHB
    hmask = [jnp.where(low, 1.0, 0.0), jnp.where(low, 0.0, 1.0)]
    for s in range(n_sub):
        sb = s if bias_ref.shape[0] > 1 else 0
        rows = slice(qs * s, qs * (s + 1))
        lanes = [slice(128 * p, 128 * (p + 1)) for p in range(HB // 2)]
        heads = [(p, hh) for p in range(HB // 2) for hh in range(2)]
        qps = [q_ref[rows, ls].astype(F32) for ls in lanes]
        kws = [_window(kp_ref, kc_ref, qs * s, qs * s + kw_rows, pb, ls) for ls in lanes]
        vws = [_window(vp_ref, vc_ref, qs * s, qs * s + kw_rows, pb, ls) for ls in lanes]
        scs = [_dot_nt((qps[p] * hmask[hh]).astype(BF16), kws[p]) + bias_ref[sb, 2 * p + hh] for p, hh in heads]
        pes = [jnp.exp2(sc - jnp.max(sc, axis=-1, keepdims=True)) for sc in scs]
        outs = [_dot(pe.astype(BF16), vws[p]) / jnp.sum(pe, axis=-1, keepdims=True) for pe, (p, hh) in zip(pes, heads)]
        for p, ls in enumerate(lanes):
            o_ref[rows, ls] = jnp.where(low, outs[2 * p], outs[2 * p + 1]).astype(BF16)


def _band_valid(g, pb, n_sub=None):
    rows, kw = CHUNK * g, pb + CHUNK * g
    r = np.arange(rows)[:, None]
    c = np.arange(kw)[None, :]
    dd = c // CHUNK - r // CHUNK
    band = (dd >= 0) & (dd <= pb // CHUNK)
    if n_sub is None:
        return band[None]
    return np.stack([band & (c >= pb - rows * s) for s in range(n_sub)])


def _band_bias(table, g, pb, valid):
    rows, kw = CHUNK * g, pb + CHUNK * g
    period = kw + rows
    m = np.arange(period)
    m = np.where(m < kw, m, m - period)
    ext = table[:, np.clip(m - pb, -MAX_REL, MAX_REL) + MAX_REL] * LOG2E
    flat = jnp.tile(ext, (1, rows))[:, :rows * (period - 1)]
    bias = flat.reshape(table.shape[0], rows, period - 1)[:, :, :kw]
    return jnp.where(valid[:, None], bias[None], -jnp.inf)


def _attn_call(kernel, q, kp, kc, vp, vc, extra, extra_specs, o_prev, *, width, kv_width, tq, pb,
               n_blocks, blk_map, prev_map, name):
    row = lambda i: (blk_map(i), 0)
    prev = lambda i: (prev_map(i), 0)
    in_specs = [pl.BlockSpec((tq, width), row),
                pl.BlockSpec((pb, kv_width), prev), pl.BlockSpec((tq, kv_width), row),
                pl.BlockSpec((pb, kv_width), prev), pl.BlockSpec((tq, kv_width), row)] + extra_specs
    args = [q, kp, kc, vp, vc] + extra
    aliases = {}
    if o_prev is not None:
        in_specs.append(pl.BlockSpec(memory_space=pl.ANY))
        args.append(o_prev)
        aliases = {len(args) - 1: 0}
        kernel = _drop_arg(kernel, len(args) - 1)
    return pl.pallas_call(
        kernel,
        grid=(n_blocks,),
        in_specs=in_specs,
        out_specs=pl.BlockSpec((tq, width), row),
        out_shape=jax.ShapeDtypeStruct((q.shape[0], width), BF16),
        input_output_aliases=aliases,
        compiler_params=_cparams(("parallel",)),
        name=name,
    )(*args)


def _attention(kernel_fn, q, k, v, cache_k, cache_v, masks, extra, extra_specs, *, width, kv_width, pb, tq, g,
               bp, lp, bs, name):
    bps = lp // tq
    n_sub = tq // (CHUNK * g)
    spec = lambda a: [pl.BlockSpec(a.shape, lambda i: (0,) * a.ndim)]
    kern = functools.partial(kernel_fn, g=g, n_sub=n_sub, pb=pb)
    common = dict(width=width, kv_width=kv_width, pb=pb)
    main = lambda i: (i // (bps - 1)) * bps + i % (bps - 1) + 1
    o = _attn_call(kern, q, k, k, v, v, [masks[0]] + extra, spec(masks[0]) + extra_specs, None, tq=tq,
                   n_blocks=bp * (bps - 1), blk_map=main, prev_map=lambda i: main(i) * (tq // pb) - 1,
                   name=name + "_main", **common)
    first = lambda i: i * bps
    o = _attn_call(kern, q, k, k, v, v, [masks[1]] + extra, spec(masks[1]) + extra_specs, o, tq=tq,
                   n_blocks=bp, blk_map=first, prev_map=lambda i: jnp.maximum(first(i) * (tq // pb) - 1, 0),
                   name=name + "_first", **common)
    samp = functools.partial(kernel_fn, g=1, n_sub=1, pb=pb)
    return _attn_call(samp, q, cache_k, k, cache_v, v, [masks[2]] + extra, spec(masks[2]) + extra_specs, o, tq=CHUNK,
                      n_blocks=bs, blk_map=lambda i: bp * lp // CHUNK + i, prev_map=lambda i: i,
                      name=name + "_sample", **common)


def _swa_kernel(q_ref, kp_ref, kc_ref, vp_ref, vc_ref, mask_ref, sink_ref, o_ref, *, g, n_sub, pb):
    qs = CHUNK * g
    kw_rows = pb + qs
    lane = lax.broadcasted_iota(jnp.int32, (1, 128), 1)
    low = lane < DHC
    hmask = [jnp.where(low, 1.0, 0.0), jnp.where(low, 0.0, 1.0)]
    pairs_per_kv = HC // KVC // 2
    for s in range(n_sub):
        msk = mask_ref[s if mask_ref.shape[0] > 1 else 0]
        rows = slice(qs * s, qs * (s + 1))
        kws = [_window(kp_ref, kc_ref, qs * s, qs * s + kw_rows, pb, slice(128 * kv, 128 * (kv + 1))) for kv in range(KVC)]
        vws = [_window(vp_ref, vc_ref, qs * s, qs * s + kw_rows, pb, slice(128 * kv, 128 * (kv + 1))) for kv in range(KVC)]
        heads = [(j, hh) for j in range(HC // 2) for hh in range(2)]
        qps = [q_ref[rows, 128 * j:128 * (j + 1)].astype(F32) for j in range(HC // 2)]
        scs = [_dot_nt((qps[j] * hmask[hh]).astype(BF16), kws[j // pairs_per_kv]) + msk for j, hh in heads]
        sks = [sink_ref[0, 2 * j + hh] for j, hh in heads]
        ms = [jnp.maximum(jnp.max(sc, axis=-1, keepdims=True), sk) for sc, sk in zip(scs, sks)]
        pes = [jnp.exp2(sc - m) for sc, m in zip(scs, ms)]
        outs = [_dot(pe.astype(BF16), vws[j // pairs_per_kv]) / (jnp.sum(pe, axis=-1, keepdims=True) + jnp.exp2(sk - m))
                for pe, sk, m, (j, hh) in zip(pes, sks, ms, heads)]
        for j in range(HC // 2):
            o_ref[rows, 128 * j:128 * (j + 1)] = jnp.where(low, outs[2 * j], outs[2 * j + 1]).astype(BF16)


def _route(logits_t):
    a = [logits_t[4 * j:4 * j + 4] for j in range(EXP_PER_GROUP)]

    def first_argmax(vals, m):
        idx = jnp.full(m.shape, float(len(vals) - 1), F32)
        for j in reversed(range(len(vals) - 1)):
            idx = jnp.where(vals[j] == m, float(j), idx)
        return idx

    m1 = functools.reduce(jnp.maximum, a)
    i1 = first_argmax(a, m1)
    bsec = [jnp.where(i1 == float(j), -jnp.inf, a[j]) for j in range(EXP_PER_GROUP)]
    m2 = functools.reduce(jnp.maximum, bsec)
    i2 = first_argmax(bsec, m2)
    rows = lambda x: [x[gi:gi + 1] for gi in range(N_GROUPS)]
    gm = functools.reduce(jnp.maximum, rows(m1))
    gscore = jnp.exp(m1 - gm) + jnp.exp(m2 - gm)
    gs = rows(gscore)
    gsel = first_argmax(gs, functools.reduce(jnp.maximum, gs))

    def pick(x):
        xr = rows(x)
        out = xr[N_GROUPS - 1]
        for gi in reversed(range(N_GROUPS - 1)):
            out = jnp.where(gsel == float(gi), xr[gi], out)
        return out

    p1 = jnp.exp(pick(m1) - gm)
    p2 = jnp.exp(pick(m2) - gm)
    w1 = p1 / (p1 + p2)
    w2 = p2 / (p1 + p2)
    s1, s2 = pick(i1), pick(i2)
    lo, hi = jnp.minimum(s1, s2), jnp.maximum(s1, s2)
    pair = jnp.where(lo == 0.0, hi - 1.0, jnp.where(lo == 1.0, hi + 1.0, 5.0))
    bucket = gsel * float(N_PAIRS) + pair
    first_is_lo = s1 < s2
    return bucket, jnp.where(first_is_lo, w1, w2), jnp.where(first_is_lo, w2, w1)


def _outproj_kernel(*refs, n_x, n_o, n_prompt_tiles):
    x_refs = refs[:n_x]
    o_refs = refs[n_x:n_x + n_o]
    w_refs = refs[n_x + n_o:n_x + 2 * n_o]
    (gate_ref, nf_ref, sh_ref, sc_ref, wr_ref, br_ref, tri_ref,
     xn_ref, disp_ref, meta_ref, cnt_ref, run_ref) = refs[n_x + 2 * n_o:]
    t = xn_ref.shape[0]

    @pl.when(pl.program_id(0) == 0)
    def _():
        run_ref[...] = jnp.zeros_like(run_ref)

    halves = [slice(0, t // 2), slice(t // 2, t)]
    grp = [slice(0, t // (2 * CHUNK)), slice(t // (2 * CHUNK), t // CHUNK)]
    ys = []
    for rs in halves:
        y = _dot(o_refs[0][rs, :], w_refs[0][...])
        for i in range(1, n_o):
            y = y + _dot(o_refs[i][rs, :], w_refs[i][...])
        ys.append(y)
    gate, shift, scale = gate_ref[...], sh_ref[...], sc_ref[...]
    gys = [_group_affine(y, gate[gs], None) for y, gs in zip(ys, grp)]

    def residual(x_ref):
        for rs, gy in zip(halves, gys):
            xn_ref[rs, :] = x_ref[rs, :] + gy

    if n_x == 1:
        residual(x_refs[0])
    else:
        _on_token_tile(x_refs[0], x_refs[1], n_prompt_tiles, residual)
    hs = [_norm_mod(xn_ref[rs, :], nf_ref[...], shift[gs], scale[gs]) for rs, gs in zip(halves, grp)]
    for rs, h in zip(halves, hs):
        for s, slab in enumerate(_pack_pairs(h)):
            disp_ref[s, rs, :] = slab
    logits_t = [(_dot3(h, wr_ref[...]) + br_ref[...]).T[0:N_EXPERTS] for h in hs]
    bucket, w_lo, w_hi = _route(jnp.concatenate(logits_t, axis=1))
    r128 = lax.broadcasted_iota(jnp.int32, (128, t), 0)
    tok = (pl.program_id(0) * t + lax.broadcasted_iota(jnp.int32, (1, t), 1)).astype(F32)
    aux = jnp.where(r128 == 0, w_lo, jnp.where(r128 == 1, w_hi, jnp.where(r128 == 2, tok, 0.0))).T
    disp_ref[disp_ref.shape[0] - 1] = pltpu.bitcast(aux, jnp.int32)
    brow = lax.broadcasted_iota(jnp.int32, (BUCKET_ROWS, t), 0).astype(F32)
    onehot = jnp.where(brow == bucket, 1.0, 0.0)
    before = _dot(onehot.astype(BF16), tri_ref[...]) + run_ref[:, 0:1]
    rank = jnp.sum(onehot * before, axis=0, keepdims=True)
    run_ref[...] = run_ref[...] + jnp.sum(onehot, axis=1, keepdims=True)
    cnt_ref[...] = run_ref[...]
    r8 = lax.broadcasted_iota(jnp.int32, (8, t), 0)
    meta_ref[...] = jnp.where(r8 == 0, bucket, jnp.where(r8 == 1, rank, 0.0)).astype(jnp.int32)


def _outproj_call(xs_, os_, ws, mods, nf, wr, br, n_pad):
    d = xs_[0].shape[1]
    n = sum(a.shape[0] for a in xs_)
    npt = xs_[0].shape[0] // TM
    row = lambda i: (i, 0)
    const = lambda i: (0, 0)
    n_o = len(os_)
    in_specs = ((_token_specs(npt, d) if len(xs_) == 2 else [pl.BlockSpec((TM, d), row)])
                + [pl.BlockSpec((TM, o.shape[1]), row) for o in os_]
                + [pl.BlockSpec(w.shape, const) for w in ws]
                + [_mod_spec(GATE_MIX), pl.BlockSpec((1, d), const),
                   _mod_spec(SHIFT_FFN), _mod_spec(SCALE_FFN),
                   pl.BlockSpec(wr.shape, const), pl.BlockSpec(br.shape, const),
                   pl.BlockSpec((TM, TM), const)])
    tri = jnp.asarray(np.triu(np.ones((TM, TM), np.float32), k=1), dtype=BF16)
    return pl.pallas_call(
        functools.partial(_outproj_kernel, n_x=len(xs_), n_o=n_o, n_prompt_tiles=npt),
        grid=(n // TM,),
        in_specs=in_specs,
        out_specs=[pl.BlockSpec((TM, d), row), pl.BlockSpec((DISP_SLABS, TM, 128), lambda i: (0, i, 0)),
                   pl.BlockSpec((8, TM), lambda i: (0, i)), pl.BlockSpec((BUCKET_ROWS, 128), const)],
        out_shape=[jax.ShapeDtypeStruct((n, d), F32), jax.ShapeDtypeStruct((DISP_SLABS, n_pad, 128), jnp.int32),
                   jax.ShapeDtypeStruct((8, n), jnp.int32), jax.ShapeDtypeStruct((BUCKET_ROWS, 128), F32)],
        scratch_shapes=[pltpu.VMEM((BUCKET_ROWS, 128), F32)],
        compiler_params=_cparams(("arbitrary",)),
        name="outproj_router",
    )(*xs_, *os_, *ws, mods, nf, mods, mods, wr, br, tri)


def _sc_mesh():
    return plsc.VectorSubcoreMesh(core_axis_name="core", subcore_axis_name="subcore")


def _sc_row_copy(src, idx, n_out, scatter):
    r = idx.shape[0]
    k = SC_GROUP
    w_per = r // (SC_WINDOW * SC_WORKERS)
    assert r % (SC_WINDOW * SC_WORKERS) == 0 and w_per % k == 0
    n_groups = w_per // k

    @functools.partial(
        pl.kernel, out_type=jax.ShapeDtypeStruct((n_out, 128), src.dtype), mesh=_sc_mesh(),
        scratch_types=[pltpu.VMEM((w_per, SC_WINDOW), jnp.int32),
                       pltpu.VMEM((2 * k, SC_WINDOW, 128), src.dtype),
                       pltpu.SemaphoreType.DMA((2,)), pltpu.SemaphoreType.DMA((2,))])
    def copy(x_hbm, i_hbm, o_hbm, ibuf, xbuf, in_sem, out_sem):
        wid = lax.axis_index("core") * (SC_WORKERS // 2) + lax.axis_index("subcore")
        pltpu.sync_copy(i_hbm.at[wid], ibuf)
        first = wid * w_per

        def rows(j):
            return pl.ds((first + j) * SC_WINDOW, SC_WINDOW)

        def start_in(g, slot):
            cps = []
            for c in range(k):
                j = g * k + c
                s = x_hbm.at[rows(j)] if scatter else x_hbm.at[ibuf.at[j]]
                cps.append(pltpu.async_copy(s, xbuf.at[slot * k + c], in_sem.at[slot]))
            return cps

        def start_out(g, slot):
            cps = []
            for c in range(k):
                j = g * k + c
                dst = o_hbm.at[ibuf.at[j]] if scatter else o_hbm.at[rows(j)]
                cps.append(pltpu.async_copy(xbuf.at[slot * k + c], dst, out_sem.at[slot]))
            return cps

        pending_in = start_in(0, 0)
        for g in range(n_groups):
            slot = g % 2
            for cp in pending_in:
                cp.wait()
            pending_out = start_out(g, slot)
            if g + 1 < n_groups:
                pending_in = start_in(g + 1, 1 - slot)
            for cp in pending_out:
                cp.wait()

    return copy(src, idx.reshape(SC_WORKERS, w_per, SC_WINDOW))


def _sc_scatter_rows(src, idx, n_out):
    assert idx.shape == (src.shape[0],)
    return _sc_row_copy(src, idx, n_out, scatter=True)


def _moe_kernel(elo_ref, ehi_ref, nvalid_ref, xs_ref, gu_lo_ref, gu_hi_ref, dn_lo_ref, dn_hi_ref, y_ref, tok_ref,
                *, n_tok, dump_tiles):
    i = pl.program_id(0)
    t = xs_ref.shape[1]
    aux = pltpu.bitcast(xs_ref[Y_SLABS], F32)
    r = lax.broadcasted_iota(jnp.int32, (1, t), 1)
    spare = n_tok + (i % dump_tiles) * t + r
    tok = jnp.where(r < nvalid_ref[i], aux.T[2:3, :].astype(jnp.int32), spare)
    for c in range(t // 128):
        tok_ref[0, c:c + 1, :] = tok[:, 128 * c:128 * (c + 1)]

    @pl.when(nvalid_ref[i] > 0)
    def _():
        h = _unpack_pairs([xs_ref[s] for s in range(Y_SLABS)], BF16)
        abs_ = [_dot(h, gu_ref[0, 0]) for gu_ref in (gu_lo_ref, gu_hi_ref)]
        acts = [(ab[:, :D_FF] * _sigmoid(ab[:, :D_FF]) * ab[:, D_FF:]).astype(BF16) for ab in abs_]
        ys = [_dot(act, dn_ref[0, 0]) for act, dn_ref in zip(acts, (dn_lo_ref, dn_hi_ref))]
        acc = aux[:, 0:1] * ys[0] + aux[:, 1:2] * ys[1]
        for s, slab in enumerate(_pack_pairs(acc)):
            y_ref[s] = slab

    @pl.when(nvalid_ref[i] == 0)
    def _():
        y_ref[...] = jnp.zeros_like(y_ref)


def _moe_call(xs, elo, ehi, nvalid, wgu, wdn, layer, n_tiles, n_tok, dump_tiles):
    d = wgu.shape[2]
    gu = lambda sel: pl.BlockSpec((1, 1, d, 2 * D_FF), lambda i, lo, hi, v: (layer, (lo, hi)[sel][i], 0, 0))
    dn = lambda sel: pl.BlockSpec((1, 1, D_FF, d), lambda i, lo, hi, v: (layer, (lo, hi)[sel][i], 0, 0))
    return pl.pallas_call(
        functools.partial(_moe_kernel, n_tok=n_tok, dump_tiles=dump_tiles),
        grid_spec=pltpu.PrefetchScalarGridSpec(
            num_scalar_prefetch=3,
            grid=(n_tiles,),
            in_specs=[pl.BlockSpec((DISP_SLABS, TMO, 128), lambda i, lo, hi, v: (0, i, 0)),
                      gu(0), gu(1), dn(0), dn(1)],
            out_specs=[pl.BlockSpec((Y_SLABS, TMO, 128), lambda i, lo, hi, v: (0, i, 0)),
                       pl.BlockSpec((1, TMO // 128, 128), lambda i, lo, hi, v: (i, 0, 0))]),
        out_shape=[jax.ShapeDtypeStruct((Y_SLABS, n_tiles * TMO, 128), jnp.int32),
                   jax.ShapeDtypeStruct((n_tiles, TMO // 128, 128), jnp.int32)],
        compiler_params=_cparams(("arbitrary",)),
        name="moe_grouped",
    )(elo, ehi, nvalid, xs, wgu, wgu, wdn, wdn)


def _moe_layer(disp, meta, counts, wgu, wdn, layer, n, n_pad, sort_rows):
    n_tiles = sort_rows // TMO
    cnt = counts[:N_BUCKETS, 0].astype(jnp.int32)
    padded = ((cnt + TMO - 1) // TMO) * TMO
    ends = jnp.cumsum(padded)
    offs = ends - padded
    bucket, rank = meta[0], meta[1]
    pos = rank + jnp.sum(jnp.where(bucket[None, :] == jnp.arange(N_BUCKETS, dtype=jnp.int32)[:, None],
                                   offs[:, None], 0), axis=0)
    tile_start = jnp.arange(n_tiles, dtype=jnp.int32) * TMO
    tile_bucket = jnp.minimum(jnp.sum((tile_start[:, None] >= ends[None, :]).astype(jnp.int32), axis=1), N_BUCKETS - 1)
    pair_lo = np.array([0, 0, 0, 1, 1, 2], np.int32)
    pair_hi = np.array([1, 2, 3, 2, 3, 3], np.int32)
    b_lo = jnp.asarray(np.repeat(np.arange(N_GROUPS), N_PAIRS) * EXP_PER_GROUP + np.tile(pair_lo, N_GROUPS), jnp.int32)
    b_hi = jnp.asarray(np.repeat(np.arange(N_GROUPS), N_PAIRS) * EXP_PER_GROUP + np.tile(pair_hi, N_GROUPS), jnp.int32)
    onehot_tb = (tile_bucket[:, None] == jnp.arange(N_BUCKETS, dtype=jnp.int32)[None, :]).astype(jnp.int32)
    elo = jnp.sum(onehot_tb * b_lo[None, :], axis=1)
    ehi = jnp.sum(onehot_tb * b_hi[None, :], axis=1)
    bucket_end = jnp.sum(onehot_tb * (offs + cnt)[None, :], axis=1)
    nvalid = jnp.where(tile_start < ends[-1], jnp.clip(bucket_end - tile_start, 0, TMO), 0)
    dump = sort_rows + jnp.arange(n_pad - n, dtype=jnp.int32)
    pos_sc = jnp.concatenate([pos, dump])
    total = sort_rows + n_pad - n
    sc_idx = (pos_sc[None, :] + (jnp.arange(DISP_SLABS, dtype=jnp.int32) * total)[:, None]).reshape(-1)
    xs = _sc_scatter_rows(disp.reshape(DISP_SLABS * n_pad, 128), sc_idx, DISP_SLABS * total)
    ys, tok = _moe_call(xs.reshape(DISP_SLABS, total, 128), elo, ehi, nvalid, wgu, wdn, layer, n_tiles,
                        n, (n_pad - n) // TMO)
    back_idx = (tok.reshape(1, sort_rows) + (jnp.arange(Y_SLABS, dtype=jnp.int32) * n_pad)[:, None]).reshape(-1)
    z = _sc_scatter_rows(ys.reshape(Y_SLABS * sort_rows, 128), back_idx, Y_SLABS * n_pad)
    return z.reshape(Y_SLABS, n_pad, 128)


def _final_kernel(xn_ref, z_ref, gate_ref, g_ref, yp_ref, ys_ref, *, n_prompt_tiles):
    x = _add_moe(xn_ref, z_ref, gate_ref)
    ms = jnp.mean(x * x, axis=-1, keepdims=True)
    y = x * lax.rsqrt(ms + EPS) * g_ref[...]
    i = pl.program_id(0)

    @pl.when(i < n_prompt_tiles)
    def _():
        yp_ref[...] = y

    @pl.when(i >= n_prompt_tiles)
    def _():
        ys_ref[...] = y


def _final_call(xn, z, mods, g, n_prompt):
    n, d = xn.shape
    npt = n_prompt // TM
    assert n - n_prompt == TM
    return pl.pallas_call(
        functools.partial(_final_kernel, n_prompt_tiles=npt),
        grid=(n // TM,),
        in_specs=[pl.BlockSpec((TM, d), lambda i: (i, 0)), pl.BlockSpec((z.shape[0], TM, 128), lambda i: (0, i, 0)),
                  _mod_spec(GATE_FFN), pl.BlockSpec((1, d), lambda i: (0, 0))],
        out_specs=_token_specs(npt, d),
        out_shape=[jax.ShapeDtypeStruct((n_prompt, d), F32), jax.ShapeDtypeStruct((TM, d), F32)],
        compiler_params=_cparams(("arbitrary",)),
        name="final_norm",
    )(xn, z, mods, g)


def kernel(x_prompt, x_sample, c_prompt, c_sample, state_gla, cache_band_k, cache_band_v, cache_swa_k, cache_swa_v,
           w_ada, b_ada, norm_mix, norm_ffn, norm_final, w_in_even, w_gate_a, b_gate_a, gla_norm, rel_bias_b,
           w_out_even, w_in_odd, sinks_c, w_out_odd, w_router, b_router, w_gate_up, w_down):
    bp, lp, d = x_prompt.shape
    bs, ls_, _ = x_sample.shape
    n_p, n_s = bp * lp, bs * ls_
    n = n_p + n_s
    assert ls_ == CHUNK and n_s == TM and lp % TM == 0 and PAST_LEN % CHUNK == 0

    xp2, xs2 = x_prompt.reshape(n_p, d), x_sample.reshape(n_s, d)

    c16 = jnp.zeros((SEQ_ROWS, d), F32).at[:bp].set(c_prompt).at[bp:bp + bs].set(c_sample)
    mods = _ada_call(c16, w_ada, b_ada)
    seq_of_group = np.concatenate([np.repeat(np.arange(bp), lp // CHUNK), bp + np.arange(bs)])
    mods_g = [mods[l][seq_of_group] for l in range(DEPTH)]

    perm = np.array([4 * (c % 4) + c // 4 for c in range(N_EXPERTS)])
    wr = jnp.zeros((d, 128), F32).at[:, :N_EXPERTS].set(w_router[:, perm])
    br = jnp.zeros((1, 128), F32).at[0, :N_EXPERTS].set(b_router[perm])

    wgu = w_gate_up.astype(BF16)
    wdn = w_down.astype(BF16)

    sc_unit = SC_WINDOW * SC_WORKERS * SC_GROUP
    n_pad = n + TMO
    while (DISP_SLABS * n_pad) % sc_unit or (Y_SLABS * n_pad) % TMO or (n_pad - n) % TMO:
        n_pad += TMO
    sort_rows = n + N_BUCKETS * TMO
    while (Y_SLABS * sort_rows) % sc_unit:
        sort_rows += TMO

    gla_p = gla_s = bk_p = bv_p = bk_s = bv_s = sk_p = sv_p = sk_s = sv_s = None
    xn = z = None
    for l in range(DEPTH):
        i = l // 2
        if l % 2 == 0:
            w = w_in_even[i]
            w_main = jnp.concatenate([w[:, :1536], w[:, 1552:]], axis=1).astype(BF16)
            w_la = jnp.zeros((d, 128), F32).at[:, :GATE_RANK].set(w[:, 1536:1552]).astype(BF16)
            w_gate = jnp.zeros((128, HA * DKA), F32).at[:GATE_RANK].set(w_gate_a[i])
            qa, ka, va, ra, qb, kb, vb, ga = _inproj_even_call(
                xp2, xs2, mods_g[l], norm_mix[l][None], w_main, w_la, w_gate, b_gate_a[i][None])
            xres = [xp2, xs2]
            gn = gla_norm[i][None]
            oa, s_p = _gla_call(qa, ka, va, ga, ra, jnp.zeros((bp, 256, 128), F32), gn, None,
                                n_seq=bp, seq_rows=lp, row0=0, nb=8)
            oa, s_s = _gla_call(qa, ka, va, ga, ra, state_gla[i].reshape(bs, 256, 128), gn, oa,
                                n_seq=bs, seq_rows=ls_, row0=n_p, nb=1)
            gla_p, gla_s = s_p.reshape(1, bp, HA, DKA, DVA), s_s.reshape(1, bs, HA, DKA, DVA)
            pb = N_PREV_B * CHUNK
            tq, g = 512, 2
            ck = cache_band_k[i].reshape(bs * pb, HB * DHB).astype(BF16)
            cv = cache_band_v[i].reshape(bs * pb, HB * DHB).astype(BF16)
            biases = (_band_bias(rel_bias_b[i], g, pb, _band_valid(g, pb)),
                      _band_bias(rel_bias_b[i], g, pb, _band_valid(g, pb, tq // (CHUNK * g))),
                      _band_bias(rel_bias_b[i], 1, pb, _band_valid(1, pb)))
            ob = _attention(_band_kernel, qb, kb, vb, ck, cv, biases, [], [], width=512, kv_width=512, pb=pb,
                            tq=tq, g=g, bp=bp, lp=lp, bs=bs, name="band")
            tail = lambda a: jnp.stack([a[(b + 1) * lp - pb:(b + 1) * lp] for b in range(bp)]).astype(F32).reshape(1, bp, pb, HB, DHB)
            new = lambda a: a[n_p:].astype(F32).reshape(bs, ls_, HB, DHB)
            bk_p, bv_p = tail(kb), tail(vb)
            bk_s = jnp.concatenate([cache_band_k[i][:, ls_:], new(kb)], axis=1)[None]
            bv_s = jnp.concatenate([cache_band_v[i][:, ls_:], new(vb)], axis=1)[None]
            wo = w_out_even[i].astype(BF16)
            os_, ws = [oa, ob], [wo[:HA * DVA], wo[HA * DVA:]]
        else:
            w = w_in_odd[i]
            wk, wv = w[:, 1024:1152], w[:, 1152:1280]
            dup = lambda a: jnp.concatenate([a[:, :64], a[:, :64], a[:, 64:], a[:, 64:]], axis=1)
            w_all = jnp.concatenate([w[:, :1024], dup(wk), dup(wv)], axis=1).astype(BF16)
            cos, sin, rope_map = _rope_tables(lp, ls_, bp, bs)
            x, q, k, v = _inproj_odd_call(xn, z, mods_g[l - 1], mods_g[l], norm_mix[l][None], cos, sin, rope_map, w_all)
            xres = [x]
            pb = WINDOW
            tq, g = 512, 2
            sink = sinks_c[i][None] * LOG2E
            sink_spec = [pl.BlockSpec(memory_space=pltpu.SMEM)]
            dupc = lambda c: jnp.concatenate([c[:, :, 0], c[:, :, 0], c[:, :, 1], c[:, :, 1]], axis=-1).reshape(bs * pb, 256).astype(BF16)
            ck, cv = dupc(cache_swa_k[i]), dupc(cache_swa_v[i])
            additive = lambda valid: jnp.asarray(np.where(valid, 0.0, -np.inf), F32)
            masks = (additive(_band_valid(g, pb)), additive(_band_valid(g, pb, tq // (CHUNK * g))),
                     additive(_band_valid(1, pb)))
            o = _attention(_swa_kernel, q, k, v, ck, cv, masks, [sink], sink_spec, width=1024, kv_width=256, pb=pb,
                           tq=tq, g=g, bp=bp, lp=lp, bs=bs, name="swa")
            undup = lambda a: jnp.concatenate([a[:, 0:64], a[:, 128:192]], axis=1).astype(F32)
            tail = lambda a: jnp.stack([undup(a[(b + 1) * lp - pb:(b + 1) * lp]) for b in range(bp)]).reshape(1, bp, pb, KVC, DHC)
            new = lambda a: undup(a[n_p:]).reshape(bs, ls_, KVC, DHC)
            sk_p, sv_p = tail(k), tail(v)
            sk_s = jnp.concatenate([cache_swa_k[i][:, ls_:], new(k)], axis=1)[None]
            sv_s = jnp.concatenate([cache_swa_v[i][:, ls_:], new(v)], axis=1)[None]
            os_, ws = [o], [w_out_odd[i].astype(BF16)]
        xn, disp, meta, counts = _outproj_call(xres, os_, ws, mods_g[l], norm_ffn[l][None], wr, br, n_pad)
        z = _moe_layer(disp, meta, counts, wgu, wdn, l, n, n_pad, sort_rows)

    y_prompt, y_sample = _final_call(xn, z, mods_g[DEPTH - 1], norm_final[None], n_p)
    return (y_prompt.reshape(bp, lp, d), y_sample.reshape(bs, ls_, d),
            gla_p, gla_s, bk_p, bv_p, bk_s, bv_s, sk_p, sv_p, sk_s, sv_s)
```

```python
import functools

import numpy as np
import jax
import jax.numpy as jnp
from jax import lax
from jax.experimental import pallas as pl
from jax.experimental.pallas import tpu as pltpu
from jax.experimental.pallas import tpu_sc as plsc

F32 = jnp.float32
BF16 = jnp.bfloat16

D_MODEL = 1024
DEPTH = 2
CHUNK = 64
PAST_LEN = 4096
HA, DKA, DVA = 4, 64, 128
GATE_RANK = 16
GATE_TAU = 16.0
HB, DHB = 8, 64
N_PREV_B = 8
MAX_REL = 128
HC, KVC, DHC = 16, 2, 64
WINDOW = 128
ROPE_THETA = 10000.0
N_EXPERTS = 16
N_GROUPS = 4
EXP_PER_GROUP = 4
D_FF = 512
EPS = 1e-6

N_PAIRS = 6
N_BUCKETS = N_GROUPS * N_PAIRS
BUCKET_ROWS = 32
Y_SLABS = 4
DISP_SLABS = Y_SLABS + 1
TMO = 256
MOE_TILES = 2
SC_WINDOW = 128
SC_WORKERS = 32
SC_GROUP = 3

TM = 512
SEQ_ROWS = 16
SUB = 16
LOG2E = 1.4426950408889634


def _cparams(sem, vmem_mb=48):
    return pltpu.CompilerParams(dimension_semantics=sem, vmem_limit_bytes=vmem_mb * 1024 * 1024)


def _dot(a, b):
    return jnp.dot(a, b, preferred_element_type=F32)


def _dot_nt(a, b):
    return lax.dot_general(a, b, (((1,), (1,)), ((), ())), preferred_element_type=F32)


def _split(a):
    hi = a.astype(BF16)
    lo = (a - hi.astype(F32)).astype(BF16)
    return hi, lo


def _dot3(a, b):
    ah, al = _split(a)
    bh, bl = _split(b)
    return _dot(ah, bh) + _dot(ah, bl) + _dot(al, bh)


def _dot3_narrow(a, b):
    ah, al = _split(a)
    bh, bl = _split(b)
    n = b.shape[1]
    p = _dot(ah, jnp.concatenate([bh, bl], axis=1))
    return p[:, :n] + p[:, n:] + _dot(al, bh)


def _sigmoid(x):
    return 1.0 / (1.0 + jnp.exp(-x))


def _group_affine(y, mul, add):
    parts = []
    for gi in range(y.shape[0] // CHUNK):
        p = y[gi * CHUNK:(gi + 1) * CHUNK]
        if mul is not None:
            p = p * mul[gi:gi + 1]
        if add is not None:
            p = p + add[gi:gi + 1]
        parts.append(p)
    return jnp.concatenate(parts, axis=0)


def _norm_mod(x, g, shift, scale):
    ms = jnp.mean(x * x, axis=-1, keepdims=True)
    return _group_affine(x * lax.rsqrt(ms + EPS) * g, 1.0 + scale, shift)


def _mod_spec(part):
    return pl.BlockSpec((TM // CHUNK, D_MODEL), lambda i: (i, part))


SHIFT_MIX, SCALE_MIX, GATE_MIX, SHIFT_FFN, SCALE_FFN, GATE_FFN = range(6)


def _on_token_tile(xp_ref, xs_ref, n_prompt_tiles, body):
    @pl.when(pl.program_id(0) < n_prompt_tiles)
    def _():
        body(xp_ref)

    @pl.when(pl.program_id(0) >= n_prompt_tiles)
    def _():
        body(xs_ref)


def _token_specs(n_prompt_tiles, d):
    return [pl.BlockSpec((TM, d), lambda i: (jnp.minimum(i, n_prompt_tiles - 1), 0)),
            pl.BlockSpec((TM, d), lambda i: (0, 0))]


def _ada_kernel(c_ref, w_ref, b_ref, o_ref):
    c = c_ref[...]
    o_ref[0] = _dot3(c * _sigmoid(c), w_ref[0]) + b_ref[0]


def _ada_call(c16, w_ada, b_ada):
    d = D_MODEL
    tn = 1024
    return pl.pallas_call(
        _ada_kernel,
        grid=(DEPTH, 6 * d // tn),
        in_specs=[pl.BlockSpec((SEQ_ROWS, d), lambda l, j: (0, 0)),
                  pl.BlockSpec((1, d, tn), lambda l, j: (l, 0, j)),
                  pl.BlockSpec((1, 1, tn), lambda l, j: (l, 0, j))],
        out_specs=pl.BlockSpec((1, SEQ_ROWS, tn), lambda l, j: (l, 0, j)),
        out_shape=jax.ShapeDtypeStruct((DEPTH, SEQ_ROWS, 6 * d), F32),
        compiler_params=_cparams(("arbitrary", "arbitrary")),
        name="ada",
    )(c16, w_ada, b_ada.reshape(DEPTH, 1, 6 * d))


def _inproj_even_kernel(xp_ref, xs_ref, sh_ref, sc_ref, g_ref, w_ref, wla_ref, wg_ref, bg_ref,
                        qa_ref, ka_ref, va_ref, ra_ref, qb_ref, kb_ref, vb_ref, ga_ref, *, n_prompt_tiles):
    def body(x_ref):
        t = x_ref.shape[0]
        outs = ((qa_ref, 0, 256, DKA ** -0.5), (ka_ref, 256, 512, None), (va_ref, 512, 1024, None),
                (ra_ref, 1024, 1536, None), (qb_ref, 1536, 2048, DHB ** -0.5 * LOG2E), (kb_ref, 2048, 2560, None),
                (vb_ref, 2560, 3072, None))
        shift, scale_ = sh_ref[...], sc_ref[...]
        halves = [slice(0, t // 2), slice(t // 2, t)]
        grp = [slice(0, t // (2 * CHUNK)), slice(t // (2 * CHUNK), t // CHUNK)]
        hbs = [_norm_mod(x_ref[rs, :], g_ref[...], shift[gs], scale_[gs]).astype(BF16) for rs, gs in zip(halves, grp)]
        for rs, hb in zip(halves, hbs):
            zs = [_dot(hb, w_ref[:, lo:hi]) for _, lo, hi, _ in outs]
            la = _dot(hb, wla_ref[...])
            for z, (o_ref, _, _, scale) in zip(zs, outs):
                o_ref[rs, :] = (z if scale is None else z * scale).astype(BF16)
            gl = _dot3(la, wg_ref[...]) + bg_ref[...]
            ga_ref[rs, :] = -(jnp.maximum(-gl, 0.0) + jnp.log(1.0 + jnp.exp(-jnp.abs(gl)))) * (1.0 / GATE_TAU)

    _on_token_tile(xp_ref, xs_ref, n_prompt_tiles, body)


def _inproj_even_call(xp, xs, mods, g, w_main, w_la, w_gate, b_gate):
    d = xp.shape[1]
    npt = xp.shape[0] // TM
    n = xp.shape[0] + xs.shape[0]
    row = lambda i: (i, 0)
    const = lambda i: (0, 0)
    widths = (256, 256, 512, 512, 512, 512, 512)
    out_shape = [jax.ShapeDtypeStruct((n, w), BF16) for w in widths] + [jax.ShapeDtypeStruct((n, 256), F32)]
    out_specs = [pl.BlockSpec((TM, w), row) for w in widths] + [pl.BlockSpec((TM, 256), row)]
    return pl.pallas_call(
        functools.partial(_inproj_even_kernel, n_prompt_tiles=npt),
        grid=(n // TM,),
        in_specs=_token_specs(npt, d) + [
            _mod_spec(SHIFT_MIX), _mod_spec(SCALE_MIX),
            pl.BlockSpec((1, d), const),
            pl.BlockSpec(w_main.shape, const), pl.BlockSpec(w_la.shape, const),
            pl.BlockSpec(w_gate.shape, const), pl.BlockSpec(b_gate.shape, const)],
        out_specs=out_specs, out_shape=out_shape,
        compiler_params=_cparams(("parallel",)),
        name="inproj_even",
    )(xp, xs, mods, mods, g, w_main, w_la, w_gate, b_gate)


def _rope(x, cos, sin_signed):
    t, w = x.shape
    lane = lax.broadcasted_iota(jnp.int32, (1, w), 1)
    first_half = (lane & 63) < 32
    rot = jnp.where(first_half, pltpu.roll(x, w - 32, 1), pltpu.roll(x, 32, 1))
    reps = w // 128
    return x * jnp.tile(cos, (1, reps)) + rot * jnp.tile(sin_signed, (1, reps))


def _unpack_pairs(slabs, dtype):
    lo = [pltpu.bitcast(s << 16, F32) for s in slabs]
    hi = [pltpu.bitcast(s & jnp.int32(-65536), F32) for s in slabs]
    return jnp.concatenate(lo + hi, axis=1).astype(dtype)


def _pack_pairs(x):
    bits = pltpu.bitcast(x.astype(BF16).astype(F32), jnp.int32)
    half = x.shape[1] // 2
    packed = ((bits[:, :half] >> 16) & jnp.int32(0xFFFF)) | (bits[:, half:] & jnp.int32(-65536))
    return [packed[:, 128 * s:128 * (s + 1)] for s in range(half // 128)]


def _add_moe(xn_ref, z_ref, gate_ref):
    y = _unpack_pairs([z_ref[s] for s in range(z_ref.shape[0])], F32)
    return xn_ref[...] + _group_affine(y, gate_ref[...], None)


def _rope_tables(lp, ls_, bp, bs):
    assert PAST_LEN + ls_ <= lp and lp % 128 == 0 and bs * ls_ == TM
    half = DHC // 2
    inv = ROPE_THETA ** (-jnp.arange(half, dtype=F32) / half)
    inv = jnp.tile(inv, 128 // half)
    sign = jnp.asarray(np.tile(np.repeat([-1.0, 1.0], half), 128 // DHC), F32)
    a = jnp.asarray(np.arange(lp // 128) * 128, F32)[:, None] * inv[None, :]
    b = jnp.asarray(np.arange(128), F32)[:, None] * inv[None, :]
    ca, sa, cb, sb = jnp.cos(a)[:, None], jnp.sin(a)[:, None], jnp.cos(b)[None], jnp.sin(b)[None]
    cos = (ca * cb - sa * sb).reshape(lp, 128)
    sin = ((sa * cb + ca * sb) * sign).reshape(lp, 128)
    with_sample = lambda t: jnp.concatenate([t, jnp.tile(t[PAST_LEN:PAST_LEN + ls_], (bs, 1))], axis=0)
    tiles = lp // TM
    return with_sample(cos), with_sample(sin), lambda i: (jnp.where(i < bp * tiles, i % tiles, tiles), 0)


def _inproj_odd_kernel(xn_ref, z_ref, gate_ref, sh_ref, sc_ref, g_ref, cos_ref, sin_ref, w_ref,
                       x_ref, q_ref, k_ref, v_ref):
    x = _add_moe(xn_ref, z_ref, gate_ref)
    x_ref[...] = x
    hb = _norm_mod(x, g_ref[...], sh_ref[...], sc_ref[...]).astype(BF16)
    cos, sin = cos_ref[...], sin_ref[...]
    q = _rope(_dot(hb, w_ref[:, 0:1024]), cos, sin)
    q_ref[...] = (q * (DHC ** -0.5 * LOG2E)).astype(BF16)
    k_ref[...] = _rope(_dot(hb, w_ref[:, 1024:1280]), cos, sin).astype(BF16)
    v_ref[...] = _dot(hb, w_ref[:, 1280:1536]).astype(BF16)


def _inproj_odd_call(xn, z, mods_prev, mods, g, cos, sin, rope_map, w):
    n, d = xn.shape
    row = lambda i: (i, 0)
    const = lambda i: (0, 0)
    widths = (1024, 256, 256)
    return pl.pallas_call(
        _inproj_odd_kernel,
        grid=(n // TM,),
        in_specs=[pl.BlockSpec((TM, d), row), pl.BlockSpec((z.shape[0], TM, 128), lambda i: (0, i, 0)),
                  _mod_spec(GATE_FFN), _mod_spec(SHIFT_MIX), _mod_spec(SCALE_MIX),
                  pl.BlockSpec((1, d), const),
                  pl.BlockSpec((TM, 128), rope_map), pl.BlockSpec((TM, 128), rope_map),
                  pl.BlockSpec(w.shape, const)],
        out_specs=[pl.BlockSpec((TM, d), row)] + [pl.BlockSpec((TM, wd), row) for wd in widths],
        out_shape=[jax.ShapeDtypeStruct((n, d), F32)] + [jax.ShapeDtypeStruct((n, wd), BF16) for wd in widths],
        compiler_params=_cparams(("parallel",)),
        name="inproj_odd",
    )(xn, z, mods_prev, mods, mods, g, cos, sin, w)


def _gla_tri():
    t = np.arange(CHUNK)[:, None]
    s = np.arange(CHUNK)[None, :]
    cum = s <= t
    start = s < (t // SUB) * SUB
    end = s < (t // SUB + 1) * SUB
    return jnp.asarray(np.concatenate([cum, start, end], axis=0).astype(np.float32), dtype=BF16)


def _gla_kernel(q_ref, k_ref, v_ref, g_ref, r_ref, s0_ref, gn_ref, tri_ref, o_ref, sout_ref, s_ref, *, nb):
    c_ = CHUNK
    nsub = c_ // SUB

    @pl.when(pl.program_id(1) == 0)
    def _():
        s_ref[...] = s0_ref[0]

    tri = tri_ref[...]
    lane = lax.broadcasted_iota(jnp.int32, (1, 128), 1)
    hmask = [jnp.where(lane < DKA, 1.0, 0.0), jnp.where(lane >= DKA, 1.0, 0.0)]
    ti = lax.broadcasted_iota(jnp.int32, (c_, c_), 0)
    si = lax.broadcasted_iota(jnp.int32, (c_, c_), 1)
    rb, cb = ti >> 4, si >> 4
    m_diag = (rb == cb) & (si <= ti)
    m_off = [(cb == j) & (rb > j) for j in range(nsub - 1)]
    hk = HA * DKA
    eye = lax.broadcasted_iota(jnp.int32, (hk, hk), 0) == lax.broadcasted_iota(jnp.int32, (hk, hk), 1)
    gn = gn_ref[...]

    chunks = range(nb)
    heads = [(p, hh) for p in range(HA // 2) for hh in range(2)]
    rows = [slice(c * c_, (c + 1) * c_) for c in chunks]
    pair = [slice(128 * p, 128 * (p + 1)) for p in range(HA // 2)]
    css = []
    for c in chunks:
        g_hi, g_lo = _split(g_ref[rows[c], :])
        css.append(_dot(tri, g_hi) + _dot(tri, g_lo))
    lhs1, lhs2, kds, kes, q_inter, klts, dcols = [], [], [], [], [], [], []
    for c in chunks:
        b, rs, re = css[c][0:c_], css[c][c_:2 * c_], css[c][2 * c_:3 * c_]
        q = q_ref[rows[c], :].astype(F32)
        k = k_ref[rows[c], :].astype(F32)
        bl = b[c_ - 1:c_, :]
        qd = q * jnp.exp(b - rs)
        kd = k * jnp.exp(rs - b)
        ke = k * jnp.exp(re - b)
        qi = q * jnp.exp(b)
        kl = k * jnp.exp(bl - b)
        ql = [q * jnp.exp(jnp.minimum(b - b[SUB * (j + 1) - 1:SUB * (j + 1), :], 0.0)) for j in range(nsub - 1)]
        dcols.append(jnp.sum(jnp.where(eye, jnp.broadcast_to(jnp.exp(bl), (hk, hk)), 0.0), axis=1, keepdims=True))
        kds.append([kd[:, ls].astype(BF16) for ls in pair])
        kes.append([ke[:, ls].astype(BF16) for ls in pair])
        klts.append([kl[:, ls].T.astype(BF16) for ls in pair])
        lhs1.append([(qd[:, pair[p]] * hmask[hh]).astype(BF16) for p, hh in heads])
        lhs2.append([jnp.concatenate([ql[j][:, pair[p]] * hmask[hh] for j in range(nsub - 1)], axis=0).astype(BF16)
                     for p, hh in heads])
        q_inter.append([(qi[:, pair[p]] * hmask[hh]).astype(BF16) for p, hh in heads])
    a1s = [[_dot_nt(lhs1[c][h], kds[c][p]) for h, (p, hh) in enumerate(heads)] for c in chunks]
    a2s = [[_dot_nt(lhs2[c][h], kes[c][p]) for h, (p, hh) in enumerate(heads)] for c in chunks]
    atts = []
    for c in chunks:
        per_head = []
        for h in range(HA):
            att = jnp.zeros((c_, c_), F32)
            for j in reversed(range(nsub - 1)):
                att = jnp.where(m_off[j], a2s[c][h][j * c_:(j + 1) * c_], att)
            per_head.append(jnp.where(m_diag, a1s[c][h], att).astype(BF16))
        atts.append(per_head)
    vs_ = [[v_ref[rows[c], DVA * h:DVA * (h + 1)] for h in range(HA)] for c in chunks]
    o_intra = [[_dot(atts[c][h], vs_[c][h]) for h in range(HA)] for c in chunks]
    upds = [jnp.concatenate([_dot(klts[c][p][DKA * hh:DKA * (hh + 1)], vs_[c][2 * p + hh]) for p, hh in heads], axis=0)
            for c in chunks]

    s_cur = s_ref[...]
    s_in = []
    for c in chunks:
        s_in.append(s_cur.astype(BF16))
        s_cur = dcols[c] * s_cur + upds[c]
    s_ref[...] = s_cur
    sout_ref[0] = s_cur

    for c in chunks:
        for h in range(HA):
            o = o_intra[c][h] + _dot(q_inter[c][h], s_in[c][pair[h // 2], :])
            ms = jnp.mean(o * o, axis=-1, keepdims=True)
            vs = slice(DVA * h, DVA * (h + 1))
            rr = r_ref[rows[c], vs].astype(F32)
            o_ref[rows[c], vs] = (o * lax.rsqrt(ms + EPS) * gn * (rr * _sigmoid(rr))).astype(BF16)


def _gla_call(q, k, v, g, r, s0, gn, o_prev, *, n_seq, seq_rows, row0, nb):
    tq = nb * CHUNK
    steps = seq_rows // tq
    blk0 = row0 // tq
    row = lambda b, j: (blk0 + b * steps + j, 0)
    const = lambda b, j: (0, 0)
    tri = _gla_tri()
    in_specs = [pl.BlockSpec((tq, 256), row), pl.BlockSpec((tq, 256), row), pl.BlockSpec((tq, 512), row),
                pl.BlockSpec((tq, 256), row), pl.BlockSpec((tq, 512), row),
                pl.BlockSpec((1, 256, 128), lambda b, j: (b, 0, 0)),
                pl.BlockSpec((1, 128), const), pl.BlockSpec(tri.shape, const)]
    args = [q, k, v, g, r, s0, gn, tri]
    aliases = {}
    if o_prev is not None:
        in_specs.append(pl.BlockSpec(memory_space=pl.ANY))
        args.append(o_prev)
        aliases = {len(args) - 1: 0}
    kern = functools.partial(_gla_kernel, nb=nb)
    if o_prev is not None:
        kern = _drop_arg(kern, 8)
    return pl.pallas_call(
        kern,
        grid=(n_seq, steps),
        in_specs=in_specs,
        out_specs=[pl.BlockSpec((tq, 512), row), pl.BlockSpec((1, 256, 128), lambda b, j: (b, 0, 0))],
        out_shape=[jax.ShapeDtypeStruct((q.shape[0], 512), BF16), jax.ShapeDtypeStruct((n_seq, 256, 128), F32)],
        scratch_shapes=[pltpu.VMEM((256, 128), F32)],
        input_output_aliases=aliases,
        compiler_params=_cparams(("arbitrary", "arbitrary")),
        name="gla",
    )(*args)


def _drop_arg(fn, idx):
    def wrapped(*refs):
        return fn(*refs[:idx], *refs[idx + 1:])
    return wrapped


def _window(prev_ref, cur_ref, lo, hi, pb, ls):
    if lo < pb:
        return jnp.concatenate([prev_ref[lo:pb, ls], cur_ref[0:hi - pb, ls]], axis=0)
    return cur_ref[lo - pb:hi - pb, ls]


def _band_kernel(q_ref, kp_ref, kc_ref, vp_ref, vc_ref, bias_ref, o_ref, *, g, n_sub, pb):
    qs = CHUNK * g
    kw_rows = pb + qs
    lane = lax.broadcasted_iota(jnp.int32, (1, 128), 1)
    low = lane < DHB
    hmask = [jnp.where(low, 1.0, 0.0), jnp.where(low, 0.0, 1.0)]
    for s in range(n_sub):
        sb = s if bias_ref.shape[0] > 1 else 0
        rows = slice(qs * s, qs * (s + 1))
        lanes = [slice(128 * p, 128 * (p + 1)) for p in range(HB // 2)]
        heads = [(p, hh) for p in range(HB // 2) for hh in range(2)]
        qps = [q_ref[rows, ls].astype(F32) for ls in lanes]
        kws = [_window(kp_ref, kc_ref, qs * s, qs * s + kw_rows, pb, ls) for ls in lanes]
        vws = [_window(vp_ref, vc_ref, qs * s, qs * s + kw_rows, pb, ls) for ls in lanes]
        scs = [_dot_nt((qps[p] * hmask[hh]).astype(BF16), kws[p]) + bias_ref[sb, 2 * p + hh] for p, hh in heads]
        pes = [jnp.exp2(sc - jnp.max(sc, axis=-1, keepdims=True)) for sc in scs]
        outs = [_dot(pe.astype(BF16), vws[p]) / jnp.sum(pe, axis=-1, keepdims=True) for pe, (p, hh) in zip(pes, heads)]
        for p, ls in enumerate(lanes):
            o_ref[rows, ls] = jnp.where(low, outs[2 * p], outs[2 * p + 1]).astype(BF16)


def _band_valid(g, pb, n_sub=None):
    rows, kw = CHUNK * g, pb + CHUNK * g
    r = np.arange(rows)[:, None]
    c = np.arange(kw)[None, :]
    dd = c // CHUNK - r // CHUNK
    band = (dd >= 0) & (dd <= pb // CHUNK)
    if n_sub is None:
        return band[None]
    return np.stack([band & (c >= pb - rows * s) for s in range(n_sub)])


def _band_bias(table, g, pb, valid):
    rows, kw = CHUNK * g, pb + CHUNK * g
    period = kw + rows
    m = np.arange(period)
    m = np.where(m < kw, m, m - period)
    ext = table[:, np.clip(m - pb, -MAX_REL, MAX_REL) + MAX_REL] * LOG2E
    flat = jnp.tile(ext, (1, rows))[:, :rows * (period - 1)]
    bias = flat.reshape(table.shape[0], rows, period - 1)[:, :, :kw]
    return jnp.where(valid[:, None], bias[None], -jnp.inf)


def _attn_call(kernel, q, kp, kc, vp, vc, extra, extra_specs, o_prev, *, width, kv_width, tq, pb,
               n_blocks, blk_map, prev_map, name):
    row = lambda i: (blk_map(i), 0)
    prev = lambda i: (prev_map(i), 0)
    in_specs = [pl.BlockSpec((tq, width), row),
                pl.BlockSpec((pb, kv_width), prev), pl.BlockSpec((tq, kv_width), row),
                pl.BlockSpec((pb, kv_width), prev), pl.BlockSpec((tq, kv_width), row)] + extra_specs
    args = [q, kp, kc, vp, vc] + extra
    aliases = {}
    if o_prev is not None:
        in_specs.append(pl.BlockSpec(memory_space=pl.ANY))
        args.append(o_prev)
        aliases = {len(args) - 1: 0}
        kernel = _drop_arg(kernel, len(args) - 1)
    return pl.pallas_call(
        kernel,
        grid=(n_blocks,),
        in_specs=in_specs,
        out_specs=pl.BlockSpec((tq, width), row),
        out_shape=jax.ShapeDtypeStruct((q.shape[0], width), BF16),
        input_output_aliases=aliases,
        compiler_params=_cparams(("parallel",)),
        name=name,
    )(*args)


def _attention(kernel_fn, q, k, v, cache_k, cache_v, masks, extra, extra_specs, *, width, kv_width, pb, tq, g,
               bp, lp, bs, name):
    bps = lp // tq
    n_sub = tq // (CHUNK * g)
    spec = lambda a: [pl.BlockSpec(a.shape, lambda i: (0,) * a.ndim)]
    kern = functools.partial(kernel_fn, g=g, n_sub=n_sub, pb=pb)
    common = dict(width=width, kv_width=kv_width, pb=pb)
    main = lambda i: (i // (bps - 1)) * bps + i % (bps - 1) + 1
    o = _attn_call(kern, q, k, k, v, v, [masks[0]] + extra, spec(masks[0]) + extra_specs, None, tq=tq,
                   n_blocks=bp * (bps - 1), blk_map=main, prev_map=lambda i: main(i) * (tq // pb) - 1,
                   name=name + "_main", **common)
    first = lambda i: i * bps
    o = _attn_call(kern, q, k, k, v, v, [masks[1]] + extra, spec(masks[1]) + extra_specs, o, tq=tq,
                   n_blocks=bp, blk_map=first, prev_map=lambda i: jnp.maximum(first(i) * (tq // pb) - 1, 0),
                   name=name + "_first", **common)
    samp = functools.partial(kernel_fn, g=1, n_sub=1, pb=pb)
    return _attn_call(samp, q, cache_k, k, cache_v, v, [masks[2]] + extra, spec(masks[2]) + extra_specs, o, tq=CHUNK,
                      n_blocks=bs, blk_map=lambda i: bp * lp // CHUNK + i, prev_map=lambda i: i,
                      name=name + "_sample", **common)


def _swa_kernel(q_ref, kp_ref, kc_ref, vp_ref, vc_ref, mask_ref, sink_ref, o_ref, *, g, n_sub, pb):
    qs = CHUNK * g
    kw_rows = pb + qs
    lane = lax.broadcasted_iota(jnp.int32, (1, 128), 1)
    low = lane < DHC
    hmask = [jnp.where(low, 1.0, 0.0), jnp.where(low, 0.0, 1.0)]
    pairs_per_kv = HC // KVC // 2
    for s in range(n_sub):
        msk = mask_ref[s if mask_ref.shape[0] > 1 else 0]
        rows = slice(qs * s, qs * (s + 1))
        kws = [_window(kp_ref, kc_ref, qs * s, qs * s + kw_rows, pb, slice(128 * kv, 128 * (kv + 1))) for kv in range(KVC)]
        vws = [_window(vp_ref, vc_ref, qs * s, qs * s + kw_rows, pb, slice(128 * kv, 128 * (kv + 1))) for kv in range(KVC)]
        heads = [(j, hh) for j in range(HC // 2) for hh in range(2)]
        qps = [q_ref[rows, 128 * j:128 * (j + 1)].astype(F32) for j in range(HC // 2)]
        scs = [_dot_nt((qps[j] * hmask[hh]).astype(BF16), kws[j // pairs_per_kv]) + msk for j, hh in heads]
        sks = [sink_ref[0, 2 * j + hh] for j, hh in heads]
        ms = [jnp.maximum(jnp.max(sc, axis=-1, keepdims=True), sk) for sc, sk in zip(scs, sks)]
        pes = [jnp.exp2(sc - m) for sc, m in zip(scs, ms)]
        outs = [_dot(pe.astype(BF16), vws[j // pairs_per_kv]) / (jnp.sum(pe, axis=-1, keepdims=True) + jnp.exp2(sk - m))
                for pe, sk, m, (j, hh) in zip(pes, sks, ms, heads)]
        for j in range(HC // 2):
            o_ref[rows, 128 * j:128 * (j + 1)] = jnp.where(low, outs[2 * j], outs[2 * j + 1]).astype(BF16)


def _route(logits_t):
    a = [logits_t[4 * j:4 * j + 4] for j in range(EXP_PER_GROUP)]

    def first_argmax(vals, m):
        idx = jnp.full(m.shape, float(len(vals) - 1), F32)
        for j in reversed(range(len(vals) - 1)):
            idx = jnp.where(vals[j] == m, float(j), idx)
        return idx

    m1 = functools.reduce(jnp.maximum, a)
    i1 = first_argmax(a, m1)
    bsec = [jnp.where(i1 == float(j), -jnp.inf, a[j]) for j in range(EXP_PER_GROUP)]
    m2 = functools.reduce(jnp.maximum, bsec)
    i2 = first_argmax(bsec, m2)
    rows = lambda x: [x[gi:gi + 1] for gi in range(N_GROUPS)]
    gm = functools.reduce(jnp.maximum, rows(m1))
    gscore = jnp.exp(m1 - gm) + jnp.exp(m2 - gm)
    gs = rows(gscore)
    gsel = first_argmax(gs, functools.reduce(jnp.maximum, gs))

    def pick(x):
        xr = rows(x)
        out = xr[N_GROUPS - 1]
        for gi in reversed(range(N_GROUPS - 1)):
            out = jnp.where(gsel == float(gi), xr[gi], out)
        return out

    p1 = jnp.exp(pick(m1) - gm)
    p2 = jnp.exp(pick(m2) - gm)
    w1 = p1 / (p1 + p2)
    w2 = p2 / (p1 + p2)
    s1, s2 = pick(i1), pick(i2)
    lo, hi = jnp.minimum(s1, s2), jnp.maximum(s1, s2)
    pair = jnp.where(lo == 0.0, hi - 1.0, jnp.where(lo == 1.0, hi + 1.0, 5.0))
    bucket = gsel * float(N_PAIRS) + pair
    first_is_lo = s1 < s2
    return bucket, jnp.where(first_is_lo, w1, w2), jnp.where(first_is_lo, w2, w1)


def _outproj_kernel(*refs, n_x, n_o, n_prompt_tiles):
    x_refs = refs[:n_x]
    o_refs = refs[n_x:n_x + n_o]
    w_refs = refs[n_x + n_o:n_x + 2 * n_o]
    (gate_ref, nf_ref, sh_ref, sc_ref, wr_ref, br_ref, tri_ref,
     xn_ref, disp_ref, meta_ref, cnt_ref, run_ref) = refs[n_x + 2 * n_o:]
    t = xn_ref.shape[0]

    @pl.when(pl.program_id(0) == 0)
    def _():
        run_ref[...] = jnp.zeros_like(run_ref)

    x_src = x_refs[0]
    if n_x == 2:
        def stage(x_ref):
            xn_ref[...] = x_ref[...]

        _on_token_tile(x_refs[0], x_refs[1], n_prompt_tiles, stage)
        x_src = xn_ref

    halves = [slice(0, t // 2), slice(t // 2, t)]
    grp = [slice(0, t // (2 * CHUNK)), slice(t // (2 * CHUNK), t // CHUNK)]
    ys = []
    for rs in halves:
        y = _dot(o_refs[0][rs, :], w_refs[0][...])
        for i in range(1, n_o):
            y = y + _dot(o_refs[i][rs, :], w_refs[i][...])
        ys.append(y)
    gate, shift, scale = gate_ref[...], sh_ref[...], sc_ref[...]
    gys = [_group_affine(y, gate[gs], None) for y, gs in zip(ys, grp)]

    for rs, gy in zip(halves, gys):
        xn_ref[rs, :] = x_src[rs, :] + gy
    hs = [_norm_mod(xn_ref[rs, :], nf_ref[...], shift[gs], scale[gs]) for rs, gs in zip(halves, grp)]
    for rs, h in zip(halves, hs):
        for s, slab in enumerate(_pack_pairs(h)):
            disp_ref[s, rs, :] = slab
    logits_t = [(_dot3_narrow(h, wr_ref[...]) + br_ref[...]).T[0:N_EXPERTS] for h in hs]
    bucket, w_lo, w_hi = _route(jnp.concatenate(logits_t, axis=1))
    r128 = lax.broadcasted_iota(jnp.int32, (128, t), 0)
    tok = (pl.program_id(0) * t + lax.broadcasted_iota(jnp.int32, (1, t), 1)).astype(F32)
    aux = jnp.where(r128 == 0, w_lo, jnp.where(r128 == 1, w_hi, jnp.where(r128 == 2, tok, 0.0))).T
    disp_ref[disp_ref.shape[0] - 1] = pltpu.bitcast(aux, jnp.int32)
    brow = lax.broadcasted_iota(jnp.int32, (BUCKET_ROWS, t), 0).astype(F32)
    onehot = jnp.where(brow == bucket, 1.0, 0.0)
    before = _dot(onehot.astype(BF16), tri_ref[...]) + run_ref[:, 0:1]
    rank = jnp.sum(onehot * before, axis=0, keepdims=True)
    run_ref[...] = run_ref[...] + jnp.sum(onehot, axis=1, keepdims=True)
    cnt_ref[...] = run_ref[...]
    r8 = lax.broadcasted_iota(jnp.int32, (8, t), 0)
    meta_ref[...] = jnp.where(r8 == 0, bucket, jnp.where(r8 == 1, rank, 0.0)).astype(jnp.int32)


def _outproj_call(xs_, os_, ws, mods, nf, wr, br, n_pad):
    d = xs_[0].shape[1]
    n = sum(a.shape[0] for a in xs_)
    npt = xs_[0].shape[0] // TM
    row = lambda i: (i, 0)
    const = lambda i: (0, 0)
    n_o = len(os_)
    in_specs = ((_token_specs(npt, d) if len(xs_) == 2 else [pl.BlockSpec((TM, d), row)])
                + [pl.BlockSpec((TM, o.shape[1]), row) for o in os_]
                + [pl.BlockSpec(w.shape, const) for w in ws]
                + [_mod_spec(GATE_MIX), pl.BlockSpec((1, d), const),
                   _mod_spec(SHIFT_FFN), _mod_spec(SCALE_FFN),
                   pl.BlockSpec(wr.shape, const), pl.BlockSpec(br.shape, const),
                   pl.BlockSpec((TM, TM), const)])
    tri = jnp.asarray(np.triu(np.ones((TM, TM), np.float32), k=1), dtype=BF16)
    return pl.pallas_call(
        functools.partial(_outproj_kernel, n_x=len(xs_), n_o=n_o, n_prompt_tiles=npt),
        grid=(n // TM,),
        in_specs=in_specs,
        out_specs=[pl.BlockSpec((TM, d), row), pl.BlockSpec((DISP_SLABS, TM, 128), lambda i: (0, i, 0)),
                   pl.BlockSpec((8, TM), lambda i: (0, i)), pl.BlockSpec((BUCKET_ROWS, 128), const)],
        out_shape=[jax.ShapeDtypeStruct((n, d), F32), jax.ShapeDtypeStruct((DISP_SLABS, n_pad, 128), jnp.int32),
                   jax.ShapeDtypeStruct((8, n), jnp.int32), jax.ShapeDtypeStruct((BUCKET_ROWS, 128), F32)],
        scratch_shapes=[pltpu.VMEM((BUCKET_ROWS, 128), F32)],
        compiler_params=_cparams(("arbitrary",)),
        name="outproj_router",
    )(*xs_, *os_, *ws, mods, nf, mods, mods, wr, br, tri)


def _sc_mesh():
    return plsc.VectorSubcoreMesh(core_axis_name="core", subcore_axis_name="subcore")


def _sc_row_copy(src, idx, n_out, scatter):
    r = idx.shape[0]
    k = SC_GROUP
    w_per = r // (SC_WINDOW * SC_WORKERS)
    assert r % (SC_WINDOW * SC_WORKERS) == 0 and w_per % k == 0
    n_groups = w_per // k

    @functools.partial(
        pl.kernel, out_type=jax.ShapeDtypeStruct((n_out, 128), src.dtype), mesh=_sc_mesh(),
        scratch_types=[pltpu.VMEM((w_per, SC_WINDOW), jnp.int32),
                       pltpu.VMEM((2 * k, SC_WINDOW, 128), src.dtype),
                       pltpu.SemaphoreType.DMA((2,)), pltpu.SemaphoreType.DMA((2,))])
    def copy(x_hbm, i_hbm, o_hbm, ibuf, xbuf, in_sem, out_sem):
        wid = lax.axis_index("core") * (SC_WORKERS // 2) + lax.axis_index("subcore")
        pltpu.sync_copy(i_hbm.at[wid], ibuf)
        first = wid * w_per

        def rows(j):
            return pl.ds((first + j) * SC_WINDOW, SC_WINDOW)

        def start_in(g, slot):
            cps = []
            for c in range(k):
                j = g * k + c
                s = x_hbm.at[rows(j)] if scatter else x_hbm.at[ibuf.at[j]]
                cps.append(pltpu.async_copy(s, xbuf.at[slot * k + c], in_sem.at[slot]))
            return cps

        def start_out(g, slot):
            cps = []
            for c in range(k):
                j = g * k + c
                dst = o_hbm.at[ibuf.at[j]] if scatter else o_hbm.at[rows(j)]
                cps.append(pltpu.async_copy(xbuf.at[slot * k + c], dst, out_sem.at[slot]))
            return cps

        pending_in = start_in(0, 0)
        for g in range(n_groups):
            slot = g % 2
            for cp in pending_in:
                cp.wait()
            pending_out = start_out(g, slot)
            if g + 1 < n_groups:
                pending_in = start_in(g + 1, 1 - slot)
            for cp in pending_out:
                cp.wait()

    return copy(src, idx.reshape(SC_WORKERS, w_per, SC_WINDOW))


def _sc_scatter_rows(src, idx, n_out):
    assert idx.shape == (src.shape[0],)
    return _sc_row_copy(src, idx, n_out, scatter=True)


def _moe_kernel(elo_ref, ehi_ref, nvalid_ref, xs_ref, *refs, n_tok, dump_tiles):
    w_refs, (y_ref, tok_ref) = refs[:4 * MOE_TILES], refs[4 * MOE_TILES:]
    step = pl.program_id(0)
    t = TMO
    tiles = range(MOE_TILES)
    rows = [slice(t * j, t * (j + 1)) for j in tiles]
    auxs = [pltpu.bitcast(xs_ref[Y_SLABS, rows[j], :], F32) for j in tiles]
    r = lax.broadcasted_iota(jnp.int32, (1, t), 1)
    for j in tiles:
        i = step * MOE_TILES + j
        spare = n_tok + (i % dump_tiles) * t + r
        tok = jnp.where(r < nvalid_ref[i], auxs[j].T[2:3, :].astype(jnp.int32), spare)
        for c in range(t // 128):
            tok_ref[j, c:c + 1, :] = tok[:, 128 * c:128 * (c + 1)]

    any_tokens = nvalid_ref[step * MOE_TILES] > 0
    for j in range(1, MOE_TILES):
        any_tokens = jnp.logical_or(any_tokens, nvalid_ref[step * MOE_TILES + j] > 0)

    @pl.when(any_tokens)
    def _():
        units = [(j, e) for j in tiles for e in range(2)]
        hs = [_unpack_pairs([xs_ref[s, rows[j], :] for s in range(Y_SLABS)], BF16) for j in tiles]
        abs_ = [_dot(hs[j], w_refs[4 * j + e][0, 0]) for j, e in units]
        acts = [(ab[:, :D_FF] * _sigmoid(ab[:, :D_FF]) * ab[:, D_FF:]).astype(BF16) for ab in abs_]
        ys = [_dot(act, w_refs[4 * j + 2 + e][0, 0]) for act, (j, e) in zip(acts, units)]
        for j in tiles:
            acc = auxs[j][:, 0:1] * ys[2 * j] + auxs[j][:, 1:2] * ys[2 * j + 1]
            for s, slab in enumerate(_pack_pairs(acc)):
                y_ref[s, rows[j], :] = slab

    @pl.when(jnp.logical_not(any_tokens))
    def _():
        y_ref[...] = jnp.zeros_like(y_ref)


def _moe_call(xs, elo, ehi, nvalid, wgu, wdn, layer, n_tiles, n_tok, dump_tiles):
    d = wgu.shape[2]
    m = MOE_TILES
    assert n_tiles % m == 0
    weight_specs = []
    for j in range(m):
        for shape in ((1, 1, d, 2 * D_FF), (1, 1, D_FF, d)):
            for sel in range(2):
                weight_specs.append(pl.BlockSpec(
                    shape, lambda i, lo, hi, v, j=j, sel=sel: (layer, (lo, hi)[sel][m * i + j], 0, 0)))
    weights = [w for _ in range(m) for w in (wgu, wgu, wdn, wdn)]
    return pl.pallas_call(
        functools.partial(_moe_kernel, n_tok=n_tok, dump_tiles=dump_tiles),
        grid_spec=pltpu.PrefetchScalarGridSpec(
            num_scalar_prefetch=3,
            grid=(n_tiles // m,),
            in_specs=[pl.BlockSpec((DISP_SLABS, m * TMO, 128), lambda i, lo, hi, v: (0, i, 0))] + weight_specs,
            out_specs=[pl.BlockSpec((Y_SLABS, m * TMO, 128), lambda i, lo, hi, v: (0, i, 0)),
                       pl.BlockSpec((m, TMO // 128, 128), lambda i, lo, hi, v: (i, 0, 0))]),
        out_shape=[jax.ShapeDtypeStruct((Y_SLABS, n_tiles * TMO, 128), jnp.int32),
                   jax.ShapeDtypeStruct((n_tiles, TMO // 128, 128), jnp.int32)],
        compiler_params=_cparams(("arbitrary",), vmem_mb=56),
        name="moe_grouped",
    )(elo, ehi, nvalid, xs, *weights)


def _moe_layer(disp, meta, counts, wgu, wdn, layer, n, n_pad, sort_rows):
    n_tiles = sort_rows // TMO
    cnt = counts[:N_BUCKETS, 0].astype(jnp.int32)
    padded = ((cnt + TMO - 1) // TMO) * TMO
    ends = jnp.cumsum(padded)
    offs = ends - padded
    bucket, rank = meta[0], meta[1]
    pos = rank + jnp.sum(jnp.where(bucket[None, :] == jnp.arange(N_BUCKETS, dtype=jnp.int32)[:, None],
                                   offs[:, None], 0), axis=0)
    tile_start = jnp.arange(n_tiles, dtype=jnp.int32) * TMO
    tile_bucket = jnp.minimum(jnp.sum((tile_start[:, None] >= ends[None, :]).astype(jnp.int32), axis=1), N_BUCKETS - 1)
    pair_lo = np.array([0, 0, 0, 1, 1, 2], np.int32)
    pair_hi = np.array([1, 2, 3, 2, 3, 3], np.int32)
    b_lo = jnp.asarray(np.repeat(np.arange(N_GROUPS), N_PAIRS) * EXP_PER_GROUP + np.tile(pair_lo, N_GROUPS), jnp.int32)
    b_hi = jnp.asarray(np.repeat(np.arange(N_GROUPS), N_PAIRS) * EXP_PER_GROUP + np.tile(pair_hi, N_GROUPS), jnp.int32)
    onehot_tb = (tile_bucket[:, None] == jnp.arange(N_BUCKETS, dtype=jnp.int32)[None, :]).astype(jnp.int32)
    elo = jnp.sum(onehot_tb * b_lo[None, :], axis=1)
    ehi = jnp.sum(onehot_tb * b_hi[None, :], axis=1)
    bucket_end = jnp.sum(onehot_tb * (offs + cnt)[None, :], axis=1)
    nvalid = jnp.where(tile_start < ends[-1], jnp.clip(bucket_end - tile_start, 0, TMO), 0)
    dump = sort_rows + jnp.arange(n_pad - n, dtype=jnp.int32)
    pos_sc = jnp.concatenate([pos, dump])
    total = sort_rows + n_pad - n
    sc_idx = (pos_sc[None, :] + (jnp.arange(DISP_SLABS, dtype=jnp.int32) * total)[:, None]).reshape(-1)
    xs = _sc_scatter_rows(disp.reshape(DISP_SLABS * n_pad, 128), sc_idx, DISP_SLABS * total)
    ys, tok = _moe_call(xs.reshape(DISP_SLABS, total, 128), elo, ehi, nvalid, wgu, wdn, layer, n_tiles,
                        n, (n_pad - n) // TMO)
    back_idx = (tok.reshape(1, sort_rows) + (jnp.arange(Y_SLABS, dtype=jnp.int32) * n_pad)[:, None]).reshape(-1)
    z = _sc_scatter_rows(ys.reshape(Y_SLABS * sort_rows, 128), back_idx, Y_SLABS * n_pad)
    return z.reshape(Y_SLABS, n_pad, 128)


def _final_kernel(xn_ref, z_ref, gate_ref, g_ref, yp_ref, ys_ref, *, n_prompt_tiles):
    x = _add_moe(xn_ref, z_ref, gate_ref)
    ms = jnp.mean(x * x, axis=-1, keepdims=True)
    y = x * lax.rsqrt(ms + EPS) * g_ref[...]
    i = pl.program_id(0)

    @pl.when(i < n_prompt_tiles)
    def _():
        yp_ref[...] = y

    @pl.when(i >= n_prompt_tiles)
    def _():
        ys_ref[...] = y


def _final_call(xn, z, mods, g, n_prompt):
    n, d = xn.shape
    npt = n_prompt // TM
    assert n - n_prompt == TM
    return pl.pallas_call(
        functools.partial(_final_kernel, n_prompt_tiles=npt),
        grid=(n // TM,),
        in_specs=[pl.BlockSpec((TM, d), lambda i: (i, 0)), pl.BlockSpec((z.shape[0], TM, 128), lambda i: (0, i, 0)),
                  _mod_spec(GATE_FFN), pl.BlockSpec((1, d), lambda i: (0, 0))],
        out_specs=_token_specs(npt, d),
        out_shape=[jax.ShapeDtypeStruct((n_prompt, d), F32), jax.ShapeDtypeStruct((TM, d), F32)],
        compiler_params=_cparams(("arbitrary",)),
        name="final_norm",
    )(xn, z, mods, g)


def kernel(x_prompt, x_sample, c_prompt, c_sample, state_gla, cache_band_k, cache_band_v, cache_swa_k, cache_swa_v,
           w_ada, b_ada, norm_mix, norm_ffn, norm_final, w_in_even, w_gate_a, b_gate_a, gla_norm, rel_bias_b,
           w_out_even, w_in_odd, sinks_c, w_out_odd, w_router, b_router, w_gate_up, w_down):
    bp, lp, d = x_prompt.shape
    bs, ls_, _ = x_sample.shape
    n_p, n_s = bp * lp, bs * ls_
    n = n_p + n_s
    assert ls_ == CHUNK and n_s == TM and lp % TM == 0 and PAST_LEN % CHUNK == 0

    xp2, xs2 = x_prompt.reshape(n_p, d), x_sample.reshape(n_s, d)

    c16 = jnp.zeros((SEQ_ROWS, d), F32).at[:bp].set(c_prompt).at[bp:bp + bs].set(c_sample)
    mods = _ada_call(c16, w_ada, b_ada)
    seq_of_group = np.concatenate([np.repeat(np.arange(bp), lp // CHUNK), bp + np.arange(bs)])
    mods_g = [mods[l][seq_of_group] for l in range(DEPTH)]

    perm = np.array([4 * (c % 4) + c // 4 for c in range(N_EXPERTS)])
    wr = jnp.zeros((d, 128), F32).at[:, :N_EXPERTS].set(w_router[:, perm])
    br = jnp.zeros((1, 128), F32).at[0, :N_EXPERTS].set(b_router[perm])

    wgu = w_gate_up.astype(BF16)
    wdn = w_down.astype(BF16)

    sc_unit = SC_WINDOW * SC_WORKERS * SC_GROUP
    n_pad = n + TMO
    while (DISP_SLABS * n_pad) % sc_unit or (Y_SLABS * n_pad) % TMO or (n_pad - n) % TMO:
        n_pad += TMO
    sort_rows = n + N_BUCKETS * TMO
    while (Y_SLABS * sort_rows) % sc_unit or sort_rows % (MOE_TILES * TMO):
        sort_rows += TMO

    gla_p = gla_s = bk_p = bv_p = bk_s = bv_s = sk_p = sv_p = sk_s = sv_s = None
    xn = z = None
    for l in range(DEPTH):
        i = l // 2
        if l % 2 == 0:
            w = w_in_even[i]
            w_main = jnp.concatenate([w[:, :1536], w[:, 1552:]], axis=1).astype(BF16)
            w_la = jnp.zeros((d, 128), F32).at[:, :GATE_RANK].set(w[:, 1536:1552]).astype(BF16)
            w_gate = jnp.zeros((128, HA * DKA), F32).at[:GATE_RANK].set(w_gate_a[i])
            qa, ka, va, ra, qb, kb, vb, ga = _inproj_even_call(
                xp2, xs2, mods_g[l], norm_mix[l][None], w_main, w_la, w_gate, b_gate_a[i][None])
            xres = [xp2, xs2]
            gn = gla_norm[i][None]
            oa, s_p = _gla_call(qa, ka, va, ga, ra, jnp.zeros((bp, 256, 128), F32), gn, None,
                                n_seq=bp, seq_rows=lp, row0=0, nb=8)
            oa, s_s = _gla_call(qa, ka, va, ga, ra, state_gla[i].reshape(bs, 256, 128), gn, oa,
                                n_seq=bs, seq_rows=ls_, row0=n_p, nb=1)
            gla_p, gla_s = s_p.reshape(1, bp, HA, DKA, DVA), s_s.reshape(1, bs, HA, DKA, DVA)
            pb = N_PREV_B * CHUNK
            tq, g = 512, 2
            ck = cache_band_k[i].reshape(bs * pb, HB * DHB).astype(BF16)
            cv = cache_band_v[i].reshape(bs * pb, HB * DHB).astype(BF16)
            biases = (_band_bias(rel_bias_b[i], g, pb, _band_valid(g, pb)),
                      _band_bias(rel_bias_b[i], g, pb, _band_valid(g, pb, tq // (CHUNK * g))),
                      _band_bias(rel_bias_b[i], 1, pb, _band_valid(1, pb)))
            ob = _attention(_band_kernel, qb, kb, vb, ck, cv, biases, [], [], width=512, kv_width=512, pb=pb,
                            tq=tq, g=g, bp=bp, lp=lp, bs=bs, name="band")
            tail = lambda a: jnp.stack([a[(b + 1) * lp - pb:(b + 1) * lp] for b in range(bp)]).astype(F32).reshape(1, bp, pb, HB, DHB)
            new = lambda a: a[n_p:].astype(F32).reshape(bs, ls_, HB, DHB)
            bk_p, bv_p = tail(kb), tail(vb)
            bk_s = jnp.concatenate([cache_band_k[i][:, ls_:], new(kb)], axis=1)[None]
            bv_s = jnp.concatenate([cache_band_v[i][:, ls_:], new(vb)], axis=1)[None]
            wo = w_out_even[i].astype(BF16)
            os_, ws = [oa, ob], [wo[:HA * DVA], wo[HA * DVA:]]
        else:
            w = w_in_odd[i]
            wk, wv = w[:, 1024:1152], w[:, 1152:1280]
            dup = lambda a: jnp.concatenate([a[:, :64], a[:, :64], a[:, 64:], a[:, 64:]], axis=1)
            w_all = jnp.concatenate([w[:, :1024], dup(wk), dup(wv)], axis=1).astype(BF16)
            cos, sin, rope_map = _rope_tables(lp, ls_, bp, bs)
            x, q, k, v = _inproj_odd_call(xn, z, mods_g[l - 1], mods_g[l], norm_mix[l][None], cos, sin, rope_map, w_all)
            xres = [x]
            pb = WINDOW
            tq, g = 512, 2
            sink = sinks_c[i][None] * LOG2E
            sink_spec = [pl.BlockSpec(memory_space=pltpu.SMEM)]
            dupc = lambda c: jnp.concatenate([c[:, :, 0], c[:, :, 0], c[:, :, 1], c[:, :, 1]], axis=-1).reshape(bs * pb, 256).astype(BF16)
            ck, cv = dupc(cache_swa_k[i]), dupc(cache_swa_v[i])
            additive = lambda valid: jnp.asarray(np.where(valid, 0.0, -np.inf), F32)
            masks = (additive(_band_valid(g, pb)), additive(_band_valid(g, pb, tq // (CHUNK * g))),
                     additive(_band_valid(1, pb)))
            o = _attention(_swa_kernel, q, k, v, ck, cv, masks, [sink], sink_spec, width=1024, kv_width=256, pb=pb,
                           tq=tq, g=g, bp=bp, lp=lp, bs=bs, name="swa")
            undup = lambda a: jnp.concatenate([a[:, 0:64], a[:, 128:192]], axis=1).astype(F32)
            tail = lambda a: jnp.stack([undup(a[(b + 1) * lp - pb:(b + 1) * lp]) for b in range(bp)]).reshape(1, bp, pb, KVC, DHC)
            new = lambda a: undup(a[n_p:]).reshape(bs, ls_, KVC, DHC)
            sk_p, sv_p = tail(k), tail(v)
            sk_s = jnp.concatenate([cache_swa_k[i][:, ls_:], new(k)], axis=1)[None]
            sv_s = jnp.concatenate([cache_swa_v[i][:, ls_:], new(v)], axis=1)[None]
            os_, ws = [o], [w_out_odd[i].astype(BF16)]
        xn, disp, meta, counts = _outproj_call(xres, os_, ws, mods_g[l], norm_ffn[l][None], wr, br, n_pad)
        z = _moe_layer(disp, meta, counts, wgu, wdn, l, n, n_pad, sort_rows)

    y_prompt, y_sample = _final_call(xn, z, mods_g[DEPTH - 1], norm_final[None], n_p)
    return (y_prompt.reshape(bp, lp, d), y_sample.reshape(bs, ls_, d),
            gla_p, gla_s, bk_p, bv_p, bk_s, bv_s, sk_p, sv_p, sk_s, sv_s)
```

```python
import functools

import numpy as np
import jax
import jax.numpy as jnp
from jax import lax
from jax.experimental import pallas as pl
from jax.experimental.pallas import tpu as pltpu
from jax.experimental.pallas import tpu_sc as plsc

F32 = jnp.float32
BF16 = jnp.bfloat16

D_MODEL = 1024
DEPTH = 2
CHUNK = 64
PAST_LEN = 4096
HA, DKA, DVA = 4, 64, 128
GATE_RANK = 16
GATE_TAU = 16.0
HB, DHB = 8, 64
N_PREV_B = 8
MAX_REL = 128
HC, KVC, DHC = 16, 2, 64
WINDOW = 128
ROPE_THETA = 10000.0
N_EXPERTS = 16
N_GROUPS = 4
EXP_PER_GROUP = 4
D_FF = 512
EPS = 1e-6

N_PAIRS = 6
N_BUCKETS = N_GROUPS * N_PAIRS
BUCKET_ROWS = 32
Y_SLABS = 4
DISP_SLABS = Y_SLABS + 1
TMO = 256
MOE_TILES = 2
SC_WINDOW = 128
SC_WORKERS = 32
SC_GROUP = 3

TM = 512
SEQ_ROWS = 16
SUB = 16
LOG2E = 1.4426950408889634


def _cparams(sem, vmem_mb=48):
    return pltpu.CompilerParams(dimension_semantics=sem, vmem_limit_bytes=vmem_mb * 1024 * 1024)


def _dot(a, b):
    return jnp.dot(a, b, preferred_element_type=F32)


def _dot_nt(a, b):
    return lax.dot_general(a, b, (((1,), (1,)), ((), ())), preferred_element_type=F32)


def _split(a):
    hi = a.astype(BF16)
    lo = (a - hi.astype(F32)).astype(BF16)
    return hi, lo


def _dot3(a, b):
    ah, al = _split(a)
    bh, bl = _split(b)
    return _dot(ah, bh) + _dot(ah, bl) + _dot(al, bh)


def _dot3_narrow(a, b):
    ah, al = _split(a)
    bh, bl = _split(b)
    n = b.shape[1]
    p = _dot(ah, jnp.concatenate([bh, bl], axis=1))
    return p[:, :n] + p[:, n:] + _dot(al, bh)


def _sigmoid(x):
    return 1.0 / (1.0 + jnp.exp(-x))


def _group_affine(y, mul, add):
    parts = []
    for gi in range(y.shape[0] // CHUNK):
        p = y[gi * CHUNK:(gi + 1) * CHUNK]
        if mul is not None:
            p = p * mul[gi:gi + 1]
        if add is not None:
            p = p + add[gi:gi + 1]
        parts.append(p)
    return jnp.concatenate(parts, axis=0)


def _norm_mod(x, g, shift, scale):
    ms = jnp.mean(x * x, axis=-1, keepdims=True)
    return _group_affine(x * lax.rsqrt(ms + EPS) * g, 1.0 + scale, shift)


def _mod_spec(part):
    return pl.BlockSpec((TM // CHUNK, D_MODEL), lambda i: (i, part))


SHIFT_MIX, SCALE_MIX, GATE_MIX, SHIFT_FFN, SCALE_FFN, GATE_FFN = range(6)


def _on_token_tile(xp_ref, xs_ref, n_prompt_tiles, body):
    @pl.when(pl.program_id(0) < n_prompt_tiles)
    def _():
        body(xp_ref)

    @pl.when(pl.program_id(0) >= n_prompt_tiles)
    def _():
        body(xs_ref)


def _token_specs(n_prompt_tiles, d):
    return [pl.BlockSpec((TM, d), lambda i: (jnp.minimum(i, n_prompt_tiles - 1), 0)),
            pl.BlockSpec((TM, d), lambda i: (0, 0))]


def _ada_kernel(c_ref, w_ref, b_ref, o_ref):
    c = c_ref[...]
    o_ref[0] = _dot3(c * _sigmoid(c), w_ref[0]) + b_ref[0]


def _ada_call(c16, w_ada, b_ada):
    d = D_MODEL
    tn = 1024
    return pl.pallas_call(
        _ada_kernel,
        grid=(DEPTH, 6 * d // tn),
        in_specs=[pl.BlockSpec((SEQ_ROWS, d), lambda l, j: (0, 0)),
                  pl.BlockSpec((1, d, tn), lambda l, j: (l, 0, j)),
                  pl.BlockSpec((1, 1, tn), lambda l, j: (l, 0, j))],
        out_specs=pl.BlockSpec((1, SEQ_ROWS, tn), lambda l, j: (l, 0, j)),
        out_shape=jax.ShapeDtypeStruct((DEPTH, SEQ_ROWS, 6 * d), F32),
        compiler_params=_cparams(("arbitrary", "arbitrary")),
        name="ada",
    )(c16, w_ada, b_ada.reshape(DEPTH, 1, 6 * d))


def _inproj_even_kernel(xp_ref, xs_ref, sh_ref, sc_ref, g_ref, w_ref, wla_ref, wg_ref, bg_ref,
                        qa_ref, ka_ref, va_ref, ra_ref, qb_ref, kb_ref, vb_ref, ga_ref, *, n_prompt_tiles):
    def body(x_ref):
        t = x_ref.shape[0]
        outs = ((qa_ref, 0, 256, DKA ** -0.5), (ka_ref, 256, 512, None), (va_ref, 512, 1024, None),
                (ra_ref, 1024, 1536, None), (qb_ref, 1536, 2048, DHB ** -0.5 * LOG2E), (kb_ref, 2048, 2560, None),
                (vb_ref, 2560, 3072, None))
        shift, scale_ = sh_ref[...], sc_ref[...]
        halves = [slice(0, t // 2), slice(t // 2, t)]
        grp = [slice(0, t // (2 * CHUNK)), slice(t // (2 * CHUNK), t // CHUNK)]
        hbs = [_norm_mod(x_ref[rs, :], g_ref[...], shift[gs], scale_[gs]).astype(BF16) for rs, gs in zip(halves, grp)]
        for rs, hb in zip(halves, hbs):
            zs = [_dot(hb, w_ref[:, lo:hi]) for _, lo, hi, _ in outs]
            la = _dot(hb, wla_ref[...])
            for z, (o_ref, _, _, scale) in zip(zs, outs):
                o_ref[rs, :] = (z if scale is None else z * scale).astype(BF16)
            gl = _dot3(la, wg_ref[...]) + bg_ref[...]
            ga_ref[rs, :] = -(jnp.maximum(-gl, 0.0) + jnp.log(1.0 + jnp.exp(-jnp.abs(gl)))) * (1.0 / GATE_TAU)

    _on_token_tile(xp_ref, xs_ref, n_prompt_tiles, body)


def _inproj_even_call(xp, xs, mods, g, w_main, w_la, w_gate, b_gate):
    d = xp.shape[1]
    npt = xp.shape[0] // TM
    n = xp.shape[0] + xs.shape[0]
    row = lambda i: (i, 0)
    const = lambda i: (0, 0)
    widths = (256, 256, 512, 512, 512, 512, 512)
    out_shape = [jax.ShapeDtypeStruct((n, w), BF16) for w in widths] + [jax.ShapeDtypeStruct((n, 256), F32)]
    out_specs = [pl.BlockSpec((TM, w), row) for w in widths] + [pl.BlockSpec((TM, 256), row)]
    return pl.pallas_call(
        functools.partial(_inproj_even_kernel, n_prompt_tiles=npt),
        grid=(n // TM,),
        in_specs=_token_specs(npt, d) + [
            _mod_spec(SHIFT_MIX), _mod_spec(SCALE_MIX),
            pl.BlockSpec((1, d), const),
            pl.BlockSpec(w_main.shape, const), pl.BlockSpec(w_la.shape, const),
            pl.BlockSpec(w_gate.shape, const), pl.BlockSpec(b_gate.shape, const)],
        out_specs=out_specs, out_shape=out_shape,
        compiler_params=_cparams(("parallel",)),
        name="inproj_even",
    )(xp, xs, mods, mods, g, w_main, w_la, w_gate, b_gate)


def _rope(x, cos, sin_signed):
    t, w = x.shape
    lane = lax.broadcasted_iota(jnp.int32, (1, w), 1)
    first_half = (lane & 63) < 32
    rot = jnp.where(first_half, pltpu.roll(x, w - 32, 1), pltpu.roll(x, 32, 1))
    reps = w // 128
    return x * jnp.tile(cos, (1, reps)) + rot * jnp.tile(sin_signed, (1, reps))


def _unpack_pairs(slabs, dtype):
    lo = [pltpu.bitcast(s << 16, F32) for s in slabs]
    hi = [pltpu.bitcast(s & jnp.int32(-65536), F32) for s in slabs]
    return jnp.concatenate(lo + hi, axis=1).astype(dtype)


def _pack_pairs(x):
    bits = pltpu.bitcast(x.astype(BF16).astype(F32), jnp.int32)
    half = x.shape[1] // 2
    packed = ((bits[:, :half] >> 16) & jnp.int32(0xFFFF)) | (bits[:, half:] & jnp.int32(-65536))
    return [packed[:, 128 * s:128 * (s + 1)] for s in range(half // 128)]


def _add_moe(xn_ref, z_ref, gate_ref):
    y = _unpack_pairs([z_ref[s] for s in range(z_ref.shape[0])], F32)
    return xn_ref[...] + _group_affine(y, gate_ref[...], None)


def _rope_tables(lp, ls_, bp, bs):
    assert PAST_LEN + ls_ <= lp and lp % 128 == 0 and bs * ls_ == TM
    half = DHC // 2
    inv = ROPE_THETA ** (-jnp.arange(half, dtype=F32) / half)
    inv = jnp.tile(inv, 128 // half)
    sign = jnp.asarray(np.tile(np.repeat([-1.0, 1.0], half), 128 // DHC), F32)
    a = jnp.asarray(np.arange(lp // 128) * 128, F32)[:, None] * inv[None, :]
    b = jnp.asarray(np.arange(128), F32)[:, None] * inv[None, :]
    ca, sa, cb, sb = jnp.cos(a)[:, None], jnp.sin(a)[:, None], jnp.cos(b)[None], jnp.sin(b)[None]
    cos = (ca * cb - sa * sb).reshape(lp, 128)
    sin = ((sa * cb + ca * sb) * sign).reshape(lp, 128)
    with_sample = lambda t: jnp.concatenate([t, jnp.tile(t[PAST_LEN:PAST_LEN + ls_], (bs, 1))], axis=0)
    tiles = lp // TM
    return with_sample(cos), with_sample(sin), lambda i: (jnp.where(i < bp * tiles, i % tiles, tiles), 0)


def _inproj_odd_kernel(xn_ref, z_ref, gate_ref, sh_ref, sc_ref, g_ref, cos_ref, sin_ref, w_ref,
                       x_ref, q_ref, k_ref, v_ref):
    x = _add_moe(xn_ref, z_ref, gate_ref)
    x_ref[...] = x
    hb = _norm_mod(x, g_ref[...], sh_ref[...], sc_ref[...]).astype(BF16)
    cos, sin = cos_ref[...], sin_ref[...]
    q = _rope(_dot(hb, w_ref[:, 0:1024]), cos, sin)
    q_ref[...] = (q * (DHC ** -0.5 * LOG2E)).astype(BF16)
    k_ref[...] = _rope(_dot(hb, w_ref[:, 1024:1280]), cos, sin).astype(BF16)
    v_ref[...] = _dot(hb, w_ref[:, 1280:1536]).astype(BF16)


def _inproj_odd_call(xn, z, mods_prev, mods, g, cos, sin, rope_map, w):
    n, d = xn.shape
    row = lambda i: (i, 0)
    const = lambda i: (0, 0)
    widths = (1024, 256, 256)
    return pl.pallas_call(
        _inproj_odd_kernel,
        grid=(n // TM,),
        in_specs=[pl.BlockSpec((TM, d), row), pl.BlockSpec((z.shape[0], TM, 128), lambda i: (0, i, 0)),
                  _mod_spec(GATE_FFN), _mod_spec(SHIFT_MIX), _mod_spec(SCALE_MIX),
                  pl.BlockSpec((1, d), const),
                  pl.BlockSpec((TM, 128), rope_map), pl.BlockSpec((TM, 128), rope_map),
                  pl.BlockSpec(w.shape, const)],
        out_specs=[pl.BlockSpec((TM, d), row)] + [pl.BlockSpec((TM, wd), row) for wd in widths],
        out_shape=[jax.ShapeDtypeStruct((n, d), F32)] + [jax.ShapeDtypeStruct((n, wd), BF16) for wd in widths],
        compiler_params=_cparams(("parallel",)),
        name="inproj_odd",
    )(xn, z, mods_prev, mods, mods, g, cos, sin, w)


def _gla_tri():
    t = np.arange(CHUNK)[:, None]
    s = np.arange(CHUNK)[None, :]
    cum = s <= t
    start = s < (t // SUB) * SUB
    end = s < (t // SUB + 1) * SUB
    return jnp.asarray(np.concatenate([cum, start, end], axis=0).astype(np.float32), dtype=BF16)


def _gla_kernel(q_ref, k_ref, v_ref, g_ref, r_ref, s0_ref, gn_ref, tri_ref, o_ref, sout_ref, s_ref, *, nb):
    c_ = CHUNK
    nsub = c_ // SUB

    @pl.when(pl.program_id(1) == 0)
    def _():
        s_ref[...] = s0_ref[0]

    tri = tri_ref[...]
    lane = lax.broadcasted_iota(jnp.int32, (1, 128), 1)
    hmask = [jnp.where(lane < DKA, 1.0, 0.0), jnp.where(lane >= DKA, 1.0, 0.0)]
    ti = lax.broadcasted_iota(jnp.int32, (c_, c_), 0)
    si = lax.broadcasted_iota(jnp.int32, (c_, c_), 1)
    rb, cb = ti >> 4, si >> 4
    m_diag = (rb == cb) & (si <= ti)
    m_off = [(cb == j) & (rb > j) for j in range(nsub - 1)]
    hk = HA * DKA
    eye = lax.broadcasted_iota(jnp.int32, (hk, hk), 0) == lax.broadcasted_iota(jnp.int32, (hk, hk), 1)
    gn = gn_ref[...]

    chunks = range(nb)
    heads = [(p, hh) for p in range(HA // 2) for hh in range(2)]
    rows = [slice(c * c_, (c + 1) * c_) for c in chunks]
    pair = [slice(128 * p, 128 * (p + 1)) for p in range(HA // 2)]
    css = []
    for c in chunks:
        g_hi, g_lo = _split(g_ref[rows[c], :])
        css.append(_dot(tri, g_hi) + _dot(tri, g_lo))
    lhs1, lhs2, kds, kes, q_inter, klts, dcols = [], [], [], [], [], [], []
    for c in chunks:
        b, rs, re = css[c][0:c_], css[c][c_:2 * c_], css[c][2 * c_:3 * c_]
        q = q_ref[rows[c], :].astype(F32)
        k = k_ref[rows[c], :].astype(F32)
        bl = b[c_ - 1:c_, :]
        qd = q * jnp.exp(b - rs)
        kd = k * jnp.exp(rs - b)
        ke = k * jnp.exp(re - b)
        qi = q * jnp.exp(b)
        kl = k * jnp.exp(bl - b)
        ql = [q * jnp.exp(jnp.minimum(b - b[SUB * (j + 1) - 1:SUB * (j + 1), :], 0.0)) for j in range(nsub - 1)]
        dcols.append(jnp.sum(jnp.where(eye, jnp.broadcast_to(jnp.exp(bl), (hk, hk)), 0.0), axis=1, keepdims=True))
        kds.append([kd[:, ls].astype(BF16) for ls in pair])
        kes.append([ke[:, ls].astype(BF16) for ls in pair])
        klts.append([kl[:, ls].T.astype(BF16) for ls in pair])
        lhs1.append([(qd[:, pair[p]] * hmask[hh]).astype(BF16) for p, hh in heads])
        lhs2.append([jnp.concatenate([ql[j][:, pair[p]] * hmask[hh] for j in range(nsub - 1)], axis=0).astype(BF16)
                     for p, hh in heads])
        q_inter.append([(qi[:, pair[p]] * hmask[hh]).astype(BF16) for p, hh in heads])
    a1s = [[_dot_nt(lhs1[c][h], kds[c][p]) for h, (p, hh) in enumerate(heads)] for c in chunks]
    a2s = [[_dot_nt(lhs2[c][h], kes[c][p]) for h, (p, hh) in enumerate(heads)] for c in chunks]
    atts = []
    for c in chunks:
        per_head = []
        for h in range(HA):
            att = jnp.zeros((c_, c_), F32)
            for j in reversed(range(nsub - 1)):
                att = jnp.where(m_off[j], a2s[c][h][j * c_:(j + 1) * c_], att)
            per_head.append(jnp.where(m_diag, a1s[c][h], att).astype(BF16))
        atts.append(per_head)
    vs_ = [[v_ref[rows[c], DVA * h:DVA * (h + 1)] for h in range(HA)] for c in chunks]
    o_intra = [[_dot(atts[c][h], vs_[c][h]) for h in range(HA)] for c in chunks]
    upds = [jnp.concatenate([_dot(klts[c][p][DKA * hh:DKA * (hh + 1)], vs_[c][2 * p + hh]) for p, hh in heads], axis=0)
            for c in chunks]

    s_cur = s_ref[...]
    s_in = []
    for c in chunks:
        s_in.append(s_cur.astype(BF16))
        s_cur = dcols[c] * s_cur + upds[c]
    s_ref[...] = s_cur
    sout_ref[0] = s_cur

    for c in chunks:
        for h in range(HA):
            o = o_intra[c][h] + _dot(q_inter[c][h], s_in[c][pair[h // 2], :])
            ms = jnp.mean(o * o, axis=-1, keepdims=True)
            vs = slice(DVA * h, DVA * (h + 1))
            rr = r_ref[rows[c], vs].astype(F32)
            o_ref[rows[c], vs] = (o * lax.rsqrt(ms + EPS) * gn * (rr * _sigmoid(rr))).astype(BF16)


def _gla_call(q, k, v, g, r, s0, gn, o_prev, *, n_seq, seq_rows, row0, nb):
    tq = nb * CHUNK
    steps = seq_rows // tq
    blk0 = row0 // tq
    row = lambda b, j: (blk0 + b * steps + j, 0)
    const = lambda b, j: (0, 0)
    tri = _gla_tri()
    in_specs = [pl.BlockSpec((tq, 256), row), pl.BlockSpec((tq, 256), row), pl.BlockSpec((tq, 512), row),
                pl.BlockSpec((tq, 256), row), pl.BlockSpec((tq, 512), row),
                pl.BlockSpec((1, 256, 128), lambda b, j: (b, 0, 0)),
                pl.BlockSpec((1, 128), const), pl.BlockSpec(tri.shape, const)]
    args = [q, k, v, g, r, s0, gn, tri]
    aliases = {}
    if o_prev is not None:
        in_specs.append(pl.BlockSpec(memory_space=pl.ANY))
        args.append(o_prev)
        aliases = {len(args) - 1: 0}
    kern = functools.partial(_gla_kernel, nb=nb)
    if o_prev is not None:
        kern = _drop_arg(kern, 8)
    return pl.pallas_call(
        kern,
        grid=(n_seq, steps),
        in_specs=in_specs,
        out_specs=[pl.BlockSpec((tq, 512), row), pl.BlockSpec((1, 256, 128), lambda b, j: (b, 0, 0))],
        out_shape=[jax.ShapeDtypeStruct((q.shape[0], 512), BF16), jax.ShapeDtypeStruct((n_seq, 256, 128), F32)],
        scratch_shapes=[pltpu.VMEM((256, 128), F32)],
        input_output_aliases=aliases,
        compiler_params=_cparams(("arbitrary", "arbitrary")),
        name="gla",
    )(*args)


def _drop_arg(fn, idx):
    def wrapped(*refs):
        return fn(*refs[:idx], *refs[idx + 1:])
    return wrapped


def _window(prev_ref, cur_ref, lo, hi, pb, ls):
    if lo < pb:
        return jnp.concatenate([prev_ref[lo:pb, ls], cur_ref[0:hi - pb, ls]], axis=0)
    return cur_ref[lo - pb:hi - pb, ls]


def _band_kernel(q_ref, kp_ref, kc_ref, vp_ref, vc_ref, bias_ref, o_ref, *, g, n_sub, pb):
    qs = CHUNK * g
    kw_rows = pb + qs
    lane = lax.broadcasted_iota(jnp.int32, (1, 128), 1)
    low = lane < DHB
    hmask = [jnp.where(low, 1.0, 0.0), jnp.where(low, 0.0, 1.0)]
    for s in range(n_sub):
        sb = s if bias_ref.shape[0] > 1 else 0
        rows = slice(qs * s, qs * (s + 1))
        lanes = [slice(128 * p, 128 * (p + 1)) for p in range(HB // 2)]
        heads = [(p, hh) for p in range(HB // 2) for hh in range(2)]
        qps = [q_ref[rows, ls].astype(F32) for ls in lanes]
        kws = [_window(kp_ref, kc_ref, qs * s, qs * s + kw_rows, pb, ls) for ls in lanes]
        vws = [_window(vp_ref, vc_ref, qs * s, qs * s + kw_rows, pb, ls) for ls in lanes]
        scs = [_dot_nt((qps[p] * hmask[hh]).astype(BF16), kws[p]) + bias_ref[sb, 2 * p + hh] for p, hh in heads]
        pes = [jnp.exp2(sc - jnp.max(sc, axis=-1, keepdims=True)) for sc in scs]
        outs = [_dot(pe.astype(BF16), vws[p]) / jnp.sum(pe, axis=-1, keepdims=True) for pe, (p, hh) in zip(pes, heads)]
        for p, ls in enumerate(lanes):
            o_ref[rows, ls] = jnp.where(low, outs[2 * p], outs[2 * p + 1]).astype(BF16)


def _band_valid(g, pb, n_sub=None):
    rows, kw = CHUNK * g, pb + CHUNK * g
    r = np.arange(rows)[:, None]
    c = np.arange(kw)[None, :]
    dd = c // CHUNK - r // CHUNK
    band = (dd >= 0) & (dd <= pb // CHUNK)
    if n_sub is None:
        return band[None]
    return np.stack([band & (c >= pb - rows * s) for s in range(n_sub)])


def _band_bias(table, g, pb, valid):
    rows, kw = CHUNK * g, pb + CHUNK * g
    period = kw + rows
    m = np.arange(period)
    m = np.where(m < kw, m, m - period)
    ext = table[:, np.clip(m - pb, -MAX_REL, MAX_REL) + MAX_REL] * LOG2E
    flat = jnp.tile(ext, (1, rows))[:, :rows * (period - 1)]
    bias = flat.reshape(table.shape[0], rows, period - 1)[:, :, :kw]
    return jnp.where(valid[:, None], bias[None], -jnp.inf)


def _attn_call(kernel, q, kp, kc, vp, vc, extra, extra_specs, o_prev, *, width, kv_width, tq, pb,
               n_blocks, blk_map, prev_map, name):
    row = lambda i: (blk_map(i), 0)
    prev = lambda i: (prev_map(i), 0)
    in_specs = [pl.BlockSpec((tq, width), row),
                pl.BlockSpec((pb, kv_width), prev), pl.BlockSpec((tq, kv_width), row),
                pl.BlockSpec((pb, kv_width), prev), pl.BlockSpec((tq, kv_width), row)] + extra_specs
    args = [q, kp, kc, vp, vc] + extra
    aliases = {}
    if o_prev is not None:
        in_specs.append(pl.BlockSpec(memory_space=pl.ANY))
        args.append(o_prev)
        aliases = {len(args) - 1: 0}
        kernel = _drop_arg(kernel, len(args) - 1)
    return pl.pallas_call(
        kernel,
        grid=(n_blocks,),
        in_specs=in_specs,
        out_specs=pl.BlockSpec((tq, width), row),
        out_shape=jax.ShapeDtypeStruct((q.shape[0], width), BF16),
        input_output_aliases=aliases,
        compiler_params=_cparams(("parallel",)),
        name=name,
    )(*args)


def _attention(kernel_fn, q, k, v, cache_k, cache_v, masks, extra, extra_specs, *, width, kv_width, pb, tq, g,
               bp, lp, bs, name):
    bps = lp // tq
    n_sub = tq // (CHUNK * g)
    spec = lambda a: [pl.BlockSpec(a.shape, lambda i: (0,) * a.ndim)]
    kern = functools.partial(kernel_fn, g=g, n_sub=n_sub, pb=pb)
    common = dict(width=width, kv_width=kv_width, pb=pb)
    main = lambda i: (i // (bps - 1)) * bps + i % (bps - 1) + 1
    o = _attn_call(kern, q, k, k, v, v, [masks[0]] + extra, spec(masks[0]) + extra_specs, None, tq=tq,
                   n_blocks=bp * (bps - 1), blk_map=main, prev_map=lambda i: main(i) * (tq // pb) - 1,
                   name=name + "_main", **common)
    first = lambda i: i * bps
    o = _attn_call(kern, q, k, k, v, v, [masks[1]] + extra, spec(masks[1]) + extra_specs, o, tq=tq,
                   n_blocks=bp, blk_map=first, prev_map=lambda i: jnp.maximum(first(i) * (tq // pb) - 1, 0),
                   name=name + "_first", **common)
    samp = functools.partial(kernel_fn, g=1, n_sub=1, pb=pb)
    return _attn_call(samp, q, cache_k, k, cache_v, v, [masks[2]] + extra, spec(masks[2]) + extra_specs, o, tq=CHUNK,
                      n_blocks=bs, blk_map=lambda i: bp * lp // CHUNK + i, prev_map=lambda i: i,
                      name=name + "_sample", **common)


def _swa_kernel(q_ref, kp_ref, kc_ref, vp_ref, vc_ref, mask_ref, sink_ref, o_ref, *, g, n_sub, pb):
    qs = CHUNK * g
    kw_rows = pb + qs
    lane = lax.broadcasted_iota(jnp.int32, (1, 128), 1)
    low = lane < DHC
    hmask = [jnp.where(low, 1.0, 0.0), jnp.where(low, 0.0, 1.0)]
    pairs_per_kv = HC // KVC // 2
    for s in range(n_sub):
        msk = mask_ref[s if mask_ref.shape[0] > 1 else 0]
        rows = slice(qs * s, qs * (s + 1))
        kws = [_window(kp_ref, kc_ref, qs * s, qs * s + kw_rows, pb, slice(128 * kv, 128 * (kv + 1))) for kv in range(KVC)]
        vws = [_window(vp_ref, vc_ref, qs * s, qs * s + kw_rows, pb, slice(128 * kv, 128 * (kv + 1))) for kv in range(KVC)]
        heads = [(j, hh) for j in range(HC // 2) for hh in range(2)]
        qps = [q_ref[rows, 128 * j:128 * (j + 1)].astype(F32) for j in range(HC // 2)]
        scs = [_dot_nt((qps[j] * hmask[hh]).astype(BF16), kws[j // pairs_per_kv]) + msk for j, hh in heads]
        sks = [sink_ref[0, 2 * j + hh] for j, hh in heads]
        ms = [jnp.maximum(jnp.max(sc, axis=-1, keepdims=True), sk) for sc, sk in zip(scs, sks)]
        pes = [jnp.exp2(sc - m) for sc, m in zip(scs, ms)]
        outs = [_dot(pe.astype(BF16), vws[j // pairs_per_kv]) / (jnp.sum(pe, axis=-1, keepdims=True) + jnp.exp2(sk - m))
                for pe, sk, m, (j, hh) in zip(pes, sks, ms, heads)]
        for j in range(HC // 2):
            o_ref[rows, 128 * j:128 * (j + 1)] = jnp.where(low, outs[2 * j], outs[2 * j + 1]).astype(BF16)


def _route(logits_t):
    a = [logits_t[4 * j:4 * j + 4] for j in range(EXP_PER_GROUP)]

    def first_argmax(vals, m):
        idx = jnp.full(m.shape, float(len(vals) - 1), F32)
        for j in reversed(range(len(vals) - 1)):
            idx = jnp.where(vals[j] == m, float(j), idx)
        return idx

    m1 = functools.reduce(jnp.maximum, a)
    i1 = first_argmax(a, m1)
    bsec = [jnp.where(i1 == float(j), -jnp.inf, a[j]) for j in range(EXP_PER_GROUP)]
    m2 = functools.reduce(jnp.maximum, bsec)
    i2 = first_argmax(bsec, m2)
    rows = lambda x: [x[gi:gi + 1] for gi in range(N_GROUPS)]
    gm = functools.reduce(jnp.maximum, rows(m1))
    gscore = jnp.exp(m1 - gm) + jnp.exp(m2 - gm)
    gs = rows(gscore)
    gsel = first_argmax(gs, functools.reduce(jnp.maximum, gs))

    def pick(x):
        xr = rows(x)
        out = xr[N_GROUPS - 1]
        for gi in reversed(range(N_GROUPS - 1)):
            out = jnp.where(gsel == float(gi), xr[gi], out)
        return out

    p1 = jnp.exp(pick(m1) - gm)
    p2 = jnp.exp(pick(m2) - gm)
    w1 = p1 / (p1 + p2)
    w2 = p2 / (p1 + p2)
    s1, s2 = pick(i1), pick(i2)
    lo, hi = jnp.minimum(s1, s2), jnp.maximum(s1, s2)
    pair = jnp.where(lo == 0.0, hi - 1.0, jnp.where(lo == 1.0, hi + 1.0, 5.0))
    bucket = gsel * float(N_PAIRS) + pair
    first_is_lo = s1 < s2
    return bucket, jnp.where(first_is_lo, w1, w2), jnp.where(first_is_lo, w2, w1)


def _outproj_kernel(*refs, n_x, n_o, n_prompt_tiles):
    x_refs = refs[:n_x]
    o_refs = refs[n_x:n_x + n_o]
    w_refs = refs[n_x + n_o:n_x + 2 * n_o]
    (gate_ref, nf_ref, sh_ref, sc_ref, wr_ref, br_ref, tri_ref,
     xn_ref, disp_ref, meta_ref, cnt_ref, run_ref) = refs[n_x + 2 * n_o:]
    t = xn_ref.shape[0]

    @pl.when(pl.program_id(0) == 0)
    def _():
        run_ref[...] = jnp.zeros_like(run_ref)

    x_src = x_refs[0]
    if n_x == 2:
        def stage(x_ref):
            xn_ref[...] = x_ref[...]

        _on_token_tile(x_refs[0], x_refs[1], n_prompt_tiles, stage)
        x_src = xn_ref

    halves = [slice(0, t // 2), slice(t // 2, t)]
    grp = [slice(0, t // (2 * CHUNK)), slice(t // (2 * CHUNK), t // CHUNK)]
    ys = []
    for rs in halves:
        y = _dot(o_refs[0][rs, :], w_refs[0][...])
        for i in range(1, n_o):
            y = y + _dot(o_refs[i][rs, :], w_refs[i][...])
        ys.append(y)
    gate, shift, scale = gate_ref[...], sh_ref[...], sc_ref[...]
    gys = [_group_affine(y, gate[gs], None) for y, gs in zip(ys, grp)]

    for rs, gy in zip(halves, gys):
        xn_ref[rs, :] = x_src[rs, :] + gy
    hs = [_norm_mod(xn_ref[rs, :], nf_ref[...], shift[gs], scale[gs]) for rs, gs in zip(halves, grp)]
    for rs, h in zip(halves, hs):
        for s, slab in enumerate(_pack_pairs(h)):
            disp_ref[s, rs, :] = slab
    logits_t = [(_dot3_narrow(h, wr_ref[...]) + br_ref[...]).T[0:N_EXPERTS] for h in hs]
    bucket, w_lo, w_hi = _route(jnp.concatenate(logits_t, axis=1))
    r128 = lax.broadcasted_iota(jnp.int32, (128, t), 0)
    tok = (pl.program_id(0) * t + lax.broadcasted_iota(jnp.int32, (1, t), 1)).astype(F32)
    aux = jnp.where(r128 == 0, w_lo, jnp.where(r128 == 1, w_hi, jnp.where(r128 == 2, tok, 0.0))).T
    disp_ref[disp_ref.shape[0] - 1] = pltpu.bitcast(aux, jnp.int32)
    brow = lax.broadcasted_iota(jnp.int32, (BUCKET_ROWS, t), 0).astype(F32)
    onehot = jnp.where(brow == bucket, 1.0, 0.0)
    before = _dot(onehot.astype(BF16), tri_ref[...]) + run_ref[:, 0:1]
    rank = jnp.sum(onehot * before, axis=0, keepdims=True)
    run_ref[...] = run_ref[...] + jnp.sum(onehot, axis=1, keepdims=True)
    cnt_ref[...] = run_ref[...]
    r8 = lax.broadcasted_iota(jnp.int32, (8, t), 0)
    meta_ref[...] = jnp.where(r8 == 0, bucket, jnp.where(r8 == 1, rank, 0.0)).astype(jnp.int32)


def _outproj_call(xs_, os_, ws, mods, nf, wr, br, n_pad):
    d = xs_[0].shape[1]
    n = sum(a.shape[0] for a in xs_)
    npt = xs_[0].shape[0] // TM
    row = lambda i: (i, 0)
    const = lambda i: (0, 0)
    n_o = len(os_)
    in_specs = ((_token_specs(npt, d) if len(xs_) == 2 else [pl.BlockSpec((TM, d), row)])
                + [pl.BlockSpec((TM, o.shape[1]), row) for o in os_]
                + [pl.BlockSpec(w.shape, const) for w in ws]
                + [_mod_spec(GATE_MIX), pl.BlockSpec((1, d), const),
                   _mod_spec(SHIFT_FFN), _mod_spec(SCALE_FFN),
                   pl.BlockSpec(wr.shape, const), pl.BlockSpec(br.shape, const),
                   pl.BlockSpec((TM, TM), const)])
    tri = jnp.asarray(np.triu(np.ones((TM, TM), np.float32), k=1), dtype=BF16)
    return pl.pallas_call(
        functools.partial(_outproj_kernel, n_x=len(xs_), n_o=n_o, n_prompt_tiles=npt),
        grid=(n // TM,),
        in_specs=in_specs,
        out_specs=[pl.BlockSpec((TM, d), row), pl.BlockSpec((DISP_SLABS, TM, 128), lambda i: (0, i, 0)),
                   pl.BlockSpec((8, TM), lambda i: (0, i)), pl.BlockSpec((BUCKET_ROWS, 128), const)],
        out_shape=[jax.ShapeDtypeStruct((n, d), F32), jax.ShapeDtypeStruct((DISP_SLABS, n_pad, 128), jnp.int32),
                   jax.ShapeDtypeStruct((8, n), jnp.int32), jax.ShapeDtypeStruct((BUCKET_ROWS, 128), F32)],
        scratch_shapes=[pltpu.VMEM((BUCKET_ROWS, 128), F32)],
        compiler_params=_cparams(("arbitrary",)),
        name="outproj_router",
    )(*xs_, *os_, *ws, mods, nf, mods, mods, wr, br, tri)


def _sc_mesh():
    return plsc.VectorSubcoreMesh(core_axis_name="core", subcore_axis_name="subcore")


def _sc_row_copy(src, idx, n_out, scatter):
    r = idx.shape[0]
    k = SC_GROUP
    w_per = r // (SC_WINDOW * SC_WORKERS)
    assert r % (SC_WINDOW * SC_WORKERS) == 0 and w_per % k == 0
    n_groups = w_per // k

    @functools.partial(
        pl.kernel, out_type=jax.ShapeDtypeStruct((n_out, 128), src.dtype), mesh=_sc_mesh(),
        scratch_types=[pltpu.VMEM((w_per, SC_WINDOW), jnp.int32),
                       pltpu.VMEM((2 * k, SC_WINDOW, 128), src.dtype),
                       pltpu.SemaphoreType.DMA((2,)), pltpu.SemaphoreType.DMA((2,))])
    def copy(x_hbm, i_hbm, o_hbm, ibuf, xbuf, in_sem, out_sem):
        wid = lax.axis_index("core") * (SC_WORKERS // 2) + lax.axis_index("subcore")
        pltpu.sync_copy(i_hbm.at[wid], ibuf)
        first = wid * w_per

        def rows(j):
            return pl.ds((first + j) * SC_WINDOW, SC_WINDOW)

        def start_in(g, slot):
            cps = []
            for c in range(k):
                j = g * k + c
                s = x_hbm.at[rows(j)] if scatter else x_hbm.at[ibuf.at[j]]
                cps.append(pltpu.async_copy(s, xbuf.at[slot * k + c], in_sem.at[slot]))
            return cps

        def start_out(g, slot):
            cps = []
            for c in range(k):
                j = g * k + c
                dst = o_hbm.at[ibuf.at[j]] if scatter else o_hbm.at[rows(j)]
                cps.append(pltpu.async_copy(xbuf.at[slot * k + c], dst, out_sem.at[slot]))
            return cps

        pending_in = start_in(0, 0)
        for g in range(n_groups):
            slot = g % 2
            for cp in pending_in:
                cp.wait()
            pending_out = start_out(g, slot)
            if g + 1 < n_groups:
                pending_in = start_in(g + 1, 1 - slot)
            for cp in pending_out:
                cp.wait()

    return copy(src, idx.reshape(SC_WORKERS, w_per, SC_WINDOW))


def _sc_scatter_rows(src, idx, n_out):
    assert idx.shape == (src.shape[0],)
    return _sc_row_copy(src, idx, n_out, scatter=True)


def _moe_kernel(elo_ref, ehi_ref, nvalid_ref, xs_ref, *refs, n_tok, dump_tiles):
    w_refs, (y_ref, tok_ref) = refs[:4 * MOE_TILES], refs[4 * MOE_TILES:]
    step = pl.program_id(0)
    t = TMO
    tiles = range(MOE_TILES)
    rows = [slice(t * j, t * (j + 1)) for j in tiles]
    auxs = [pltpu.bitcast(xs_ref[Y_SLABS, rows[j], :], F32) for j in tiles]
    r = lax.broadcasted_iota(jnp.int32, (1, t), 1)
    for j in tiles:
        i = step * MOE_TILES + j
        spare = n_tok + (i % dump_tiles) * t + r
        tok = jnp.where(r < nvalid_ref[i], auxs[j].T[2:3, :].astype(jnp.int32), spare)
        for c in range(t // 128):
            tok_ref[j, c:c + 1, :] = tok[:, 128 * c:128 * (c + 1)]

    any_tokens = nvalid_ref[step * MOE_TILES] > 0
    for j in range(1, MOE_TILES):
        any_tokens = jnp.logical_or(any_tokens, nvalid_ref[step * MOE_TILES + j] > 0)

    @pl.when(any_tokens)
    def _():
        units = [(j, e) for j in tiles for e in range(2)]
        hs = [_unpack_pairs([xs_ref[s, rows[j], :] for s in range(Y_SLABS)], BF16) for j in tiles]
        abs_ = [_dot(hs[j], w_refs[4 * j + e][0, 0]) for j, e in units]
        acts = [(ab[:, :D_FF] * _sigmoid(ab[:, :D_FF]) * ab[:, D_FF:]).astype(BF16) for ab in abs_]
        ys = [_dot(act, w_refs[4 * j + 2 + e][0, 0]) for act, (j, e) in zip(acts, units)]
        for j in tiles:
            acc = auxs[j][:, 0:1] * ys[2 * j] + auxs[j][:, 1:2] * ys[2 * j + 1]
            for s, slab in enumerate(_pack_pairs(acc)):
                y_ref[s, rows[j], :] = slab

    @pl.when(jnp.logical_not(any_tokens))
    def _():
        y_ref[...] = jnp.zeros_like(y_ref)


def _moe_call(xs, elo, ehi, nvalid, wgu, wdn, layer, n_tiles, n_tok, dump_tiles):
    d = wgu.shape[2]
    m = MOE_TILES
    assert n_tiles % m == 0
    weight_specs = []
    for j in range(m):
        for shape in ((1, 1, d, 2 * D_FF), (1, 1, D_FF, d)):
            for sel in range(2):
                weight_specs.append(pl.BlockSpec(
                    shape, lambda i, lo, hi, v, j=j, sel=sel: (layer, (lo, hi)[sel][m * i + j], 0, 0)))
    weights = [w for _ in range(m) for w in (wgu, wgu, wdn, wdn)]
    return pl.pallas_call(
        functools.partial(_moe_kernel, n_tok=n_tok, dump_tiles=dump_tiles),
        grid_spec=pltpu.PrefetchScalarGridSpec(
            num_scalar_prefetch=3,
            grid=(n_tiles // m,),
            in_specs=[pl.BlockSpec((DISP_SLABS, m * TMO, 128), lambda i, lo, hi, v: (0, i, 0))] + weight_specs,
            out_specs=[pl.BlockSpec((Y_SLABS, m * TMO, 128), lambda i, lo, hi, v: (0, i, 0)),
                       pl.BlockSpec((m, TMO // 128, 128), lambda i, lo, hi, v: (i, 0, 0))]),
        out_shape=[jax.ShapeDtypeStruct((Y_SLABS, n_tiles * TMO, 128), jnp.int32),
                   jax.ShapeDtypeStruct((n_tiles, TMO // 128, 128), jnp.int32)],
        compiler_params=_cparams(("arbitrary",), vmem_mb=56),
        name="moe_grouped",
    )(elo, ehi, nvalid, xs, *weights)


def _moe_layer(disp, meta, counts, wgu, wdn, layer, n, n_pad, sort_rows):
    n_tiles = sort_rows // TMO
    cnt = counts[:N_BUCKETS, 0].astype(jnp.int32)
    padded = ((cnt + TMO - 1) // TMO) * TMO
    ends = jnp.cumsum(padded)
    offs = ends - padded
    bucket, rank = meta[0], meta[1]
    pos = rank + jnp.sum(jnp.where(bucket[None, :] == jnp.arange(N_BUCKETS, dtype=jnp.int32)[:, None],
                                   offs[:, None], 0), axis=0)
    tile_start = jnp.arange(n_tiles, dtype=jnp.int32) * TMO
    tile_bucket = jnp.minimum(jnp.sum((tile_start[:, None] >= ends[None, :]).astype(jnp.int32), axis=1), N_BUCKETS - 1)
    pair_lo = np.array([0, 0, 0, 1, 1, 2], np.int32)
    pair_hi = np.array([1, 2, 3, 2, 3, 3], np.int32)
    b_lo = jnp.asarray(np.repeat(np.arange(N_GROUPS), N_PAIRS) * EXP_PER_GROUP + np.tile(pair_lo, N_GROUPS), jnp.int32)
    b_hi = jnp.asarray(np.repeat(np.arange(N_GROUPS), N_PAIRS) * EXP_PER_GROUP + np.tile(pair_hi, N_GROUPS), jnp.int32)
    onehot_tb = (tile_bucket[:, None] == jnp.arange(N_BUCKETS, dtype=jnp.int32)[None, :]).astype(jnp.int32)
    elo = jnp.sum(onehot_tb * b_lo[None, :], axis=1)
    ehi = jnp.sum(onehot_tb * b_hi[None, :], axis=1)
    bucket_end = jnp.sum(onehot_tb * (offs + cnt)[None, :], axis=1)
    nvalid = jnp.where(tile_start < ends[-1], jnp.clip(bucket_end - tile_start, 0, TMO), 0)
    dump = sort_rows + jnp.arange(n_pad - n, dtype=jnp.int32)
    pos_sc = jnp.concatenate([pos, dump])
    total = sort_rows + n_pad - n
    sc_idx = (pos_sc[None, :] + (jnp.arange(DISP_SLABS, dtype=jnp.int32) * total)[:, None]).reshape(-1)
    xs = _sc_scatter_rows(disp.reshape(DISP_SLABS * n_pad, 128), sc_idx, DISP_SLABS * total)
    ys, tok = _moe_call(xs.reshape(DISP_SLABS, total, 128), elo, ehi, nvalid, wgu, wdn, layer, n_tiles,
                        n, (n_pad - n) // TMO)
    back_idx = (tok.reshape(1, sort_rows) + (jnp.arange(Y_SLABS, dtype=jnp.int32) * n_pad)[:, None]).reshape(-1)
    z = _sc_scatter_rows(ys.reshape(Y_SLABS * sort_rows, 128), back_idx, Y_SLABS * n_pad)
    return z.reshape(Y_SLABS, n_pad, 128)


def _final_kernel(xn_ref, z_ref, gate_ref, g_ref, yp_ref, ys_ref, *, n_prompt_tiles):
    x = _add_moe(xn_ref, z_ref, gate_ref)
    ms = jnp.mean(x * x, axis=-1, keepdims=True)
    y = x * lax.rsqrt(ms + EPS) * g_ref[...]
    i = pl.program_id(0)

    @pl.when(i < n_prompt_tiles)
    def _():
        yp_ref[...] = y

    @pl.when(i >= n_prompt_tiles)
    def _():
        ys_ref[...] = y


def _final_call(xn, z, mods, g, n_prompt):
    n, d = xn.shape
    npt = n_prompt // TM
    assert n - n_prompt == TM
    return pl.pallas_call(
        functools.partial(_final_kernel, n_prompt_tiles=npt),
        grid=(n // TM,),
        in_specs=[pl.BlockSpec((TM, d), lambda i: (i, 0)), pl.BlockSpec((z.shape[0], TM, 128), lambda i: (0, i, 0)),
                  _mod_spec(GATE_FFN), pl.BlockSpec((1, d), lambda i: (0, 0))],
        out_specs=_token_specs(npt, d),
        out_shape=[jax.ShapeDtypeStruct((n_prompt, d), F32), jax.ShapeDtypeStruct((TM, d), F32)],
        compiler_params=_cparams(("arbitrary",)),
        name="final_norm",
    )(xn, z, mods, g)


def kernel(x_prompt, x_sample, c_prompt, c_sample, state_gla, cache_band_k, cache_band_v, cache_swa_k, cache_swa_v,
           w_ada, b_ada, norm_mix, norm_ffn, norm_final, w_in_even, w_gate_a, b_gate_a, gla_norm, rel_bias_b,
           w_out_even, w_in_odd, sinks_c, w_out_odd, w_router, b_router, w_gate_up, w_down):
    bp, lp, d = x_prompt.shape
    bs, ls_, _ = x_sample.shape
    n_p, n_s = bp * lp, bs * ls_
    n = n_p + n_s
    assert ls_ == CHUNK and n_s == TM and lp % TM == 0 and PAST_LEN % CHUNK == 0

    xp2, xs2 = x_prompt.reshape(n_p, d), x_sample.reshape(n_s, d)

    c16 = jnp.zeros((SEQ_ROWS, d), F32).at[:bp].set(c_prompt).at[bp:bp + bs].set(c_sample)
    mods = _ada_call(c16, w_ada, b_ada)
    mods_g = [jnp.concatenate([jnp.broadcast_to(mods[l][b], (lp // CHUNK, 6 * d)) for b in range(bp)]
                              + [mods[l][bp:bp + bs]], axis=0) for l in range(DEPTH)]

    perm = np.array([4 * (c % 4) + c // 4 for c in range(N_EXPERTS)])
    wr = jnp.zeros((d, 128), F32).at[:, :N_EXPERTS].set(w_router[:, perm])
    br = jnp.zeros((1, 128), F32).at[0, :N_EXPERTS].set(b_router[perm])

    sc_unit = SC_WINDOW * SC_WORKERS * SC_GROUP
    n_pad = n + TMO
    while (DISP_SLABS * n_pad) % sc_unit or (Y_SLABS * n_pad) % TMO or (n_pad - n) % TMO:
        n_pad += TMO
    sort_rows = n + N_BUCKETS * TMO
    while (Y_SLABS * sort_rows) % sc_unit or sort_rows % (MOE_TILES * TMO):
        sort_rows += TMO

    gla_p = gla_s = bk_p = bv_p = bk_s = bv_s = sk_p = sv_p = sk_s = sv_s = None
    xn = z = None
    for l in range(DEPTH):
        i = l // 2
        if l % 2 == 0:
            w = w_in_even[i]
            w_main = jnp.concatenate([w[:, :1536], w[:, 1552:]], axis=1).astype(BF16)
            w_la = jnp.zeros((d, 128), F32).at[:, :GATE_RANK].set(w[:, 1536:1552]).astype(BF16)
            w_gate = jnp.zeros((128, HA * DKA), F32).at[:GATE_RANK].set(w_gate_a[i])
            qa, ka, va, ra, qb, kb, vb, ga = _inproj_even_call(
                xp2, xs2, mods_g[l], norm_mix[l][None], w_main, w_la, w_gate, b_gate_a[i][None])
            xres = [xp2, xs2]
            gn = gla_norm[i][None]
            oa, s_p = _gla_call(qa, ka, va, ga, ra, jnp.zeros((bp, 256, 128), F32), gn, None,
                                n_seq=bp, seq_rows=lp, row0=0, nb=8)
            oa, s_s = _gla_call(qa, ka, va, ga, ra, state_gla[i].reshape(bs, 256, 128), gn, oa,
                                n_seq=bs, seq_rows=ls_, row0=n_p, nb=1)
            gla_p, gla_s = s_p.reshape(1, bp, HA, DKA, DVA), s_s.reshape(1, bs, HA, DKA, DVA)
            pb = N_PREV_B * CHUNK
            tq, g = 512, 2
            ck = cache_band_k[i].reshape(bs * pb, HB * DHB).astype(BF16)
            cv = cache_band_v[i].reshape(bs * pb, HB * DHB).astype(BF16)
            biases = (_band_bias(rel_bias_b[i], g, pb, _band_valid(g, pb)),
                      _band_bias(rel_bias_b[i], g, pb, _band_valid(g, pb, tq // (CHUNK * g))),
                      _band_bias(rel_bias_b[i], 1, pb, _band_valid(1, pb)))
            ob = _attention(_band_kernel, qb, kb, vb, ck, cv, biases, [], [], width=512, kv_width=512, pb=pb,
                            tq=tq, g=g, bp=bp, lp=lp, bs=bs, name="band")
            tail = lambda a: jnp.stack([a[(b + 1) * lp - pb:(b + 1) * lp] for b in range(bp)]).astype(F32).reshape(1, bp, pb, HB, DHB)
            new = lambda a: a[n_p:].astype(F32).reshape(bs, ls_, HB, DHB)
            bk_p, bv_p = tail(kb), tail(vb)
            bk_s = jnp.concatenate([cache_band_k[i][:, ls_:], new(kb)], axis=1)[None]
            bv_s = jnp.concatenate([cache_band_v[i][:, ls_:], new(vb)], axis=1)[None]
            wo = w_out_even[i].astype(BF16)
            os_, ws = [oa, ob], [wo[:HA * DVA], wo[HA * DVA:]]
        else:
            w = w_in_odd[i]
            wk, wv = w[:, 1024:1152], w[:, 1152:1280]
            dup = lambda a: jnp.concatenate([a[:, :64], a[:, :64], a[:, 64:], a[:, 64:]], axis=1)
            w_all = jnp.concatenate([w[:, :1024], dup(wk), dup(wv)], axis=1).astype(BF16)
            cos, sin, rope_map = _rope_tables(lp, ls_, bp, bs)
            x, q, k, v = _inproj_odd_call(xn, z, mods_g[l - 1], mods_g[l], norm_mix[l][None], cos, sin, rope_map, w_all)
            xres = [x]
            pb = WINDOW
            tq, g = 512, 2
            sink = sinks_c[i][None] * LOG2E
            sink_spec = [pl.BlockSpec(memory_space=pltpu.SMEM)]
            dupc = lambda c: jnp.concatenate([c[:, :, 0], c[:, :, 0], c[:, :, 1], c[:, :, 1]], axis=-1).reshape(bs * pb, 256).astype(BF16)
            ck, cv = dupc(cache_swa_k[i]), dupc(cache_swa_v[i])
            additive = lambda valid: jnp.asarray(np.where(valid, 0.0, -np.inf), F32)
            masks = (additive(_band_valid(g, pb)), additive(_band_valid(g, pb, tq // (CHUNK * g))),
                     additive(_band_valid(1, pb)))
            o = _attention(_swa_kernel, q, k, v, ck, cv, masks, [sink], sink_spec, width=1024, kv_width=256, pb=pb,
                           tq=tq, g=g, bp=bp, lp=lp, bs=bs, name="swa")
            undup = lambda a: jnp.concatenate([a[:, 0:64], a[:, 128:192]], axis=1).astype(F32)
            tail = lambda a: jnp.stack([undup(a[(b + 1) * lp - pb:(b + 1) * lp]) for b in range(bp)]).reshape(1, bp, pb, KVC, DHC)
            new = lambda a: undup(a[n_p:]).reshape(bs, ls_, KVC, DHC)
            sk_p, sv_p = tail(k), tail(v)
            sk_s = jnp.concatenate([cache_swa_k[i][:, ls_:], new(k)], axis=1)[None]
            sv_s = jnp.concatenate([cache_swa_v[i][:, ls_:], new(v)], axis=1)[None]
            os_, ws = [o], [w_out_odd[i].astype(BF16)]
        xn, disp, meta, counts = _outproj_call(xres, os_, ws, mods_g[l], norm_ffn[l][None], wr, br, n_pad)
        z = _moe_layer(disp, meta, counts, w_gate_up[l:l + 1].astype(BF16), w_down[l:l + 1].astype(BF16), 0,
                       n, n_pad, sort_rows)

    y_prompt, y_sample = _final_call(xn, z, mods_g[DEPTH - 1], norm_final[None], n_p)
    return (y_prompt.reshape(bp, lp, d), y_sample.reshape(bs, ls_, d),
            gla_p, gla_s, bk_p, bv_p, bk_s, bv_s, sk_p, sv_p, sk_s, sv_s)
```

```python
import functools

import numpy as np
import jax
import jax.numpy as jnp
from jax import lax
from jax.experimental import pallas as pl
from jax.experimental.pallas import tpu as pltpu
from jax.experimental.pallas import tpu_sc as plsc

F32 = jnp.float32
BF16 = jnp.bfloat16

D_MODEL = 1024
DEPTH = 2
CHUNK = 64
PAST_LEN = 4096
HA, DKA, DVA = 4, 64, 128
GATE_RANK = 16
GATE_TAU = 16.0
HB, DHB = 8, 64
N_PREV_B = 8
MAX_REL = 128
HC, KVC, DHC = 16, 2, 64
WINDOW = 128
ROPE_THETA = 10000.0
N_EXPERTS = 16
N_GROUPS = 4
EXP_PER_GROUP = 4
D_FF = 512
EPS = 1e-6

N_PAIRS = 6
N_BUCKETS = N_GROUPS * N_PAIRS
BUCKET_ROWS = 32
Y_SLABS = 4
DISP_SLABS = Y_SLABS + 1
TMO = 256
MOE_TILES = 2
SC_WINDOW = 128
SC_WORKERS = 32
SC_GROUP = 3

TM = 512
SEQ_ROWS = 16
SUB = 16
LOG2E = 1.4426950408889634


def _cparams(sem, vmem_mb=48):
    return pltpu.CompilerParams(dimension_semantics=sem, vmem_limit_bytes=vmem_mb * 1024 * 1024)


def _dot(a, b):
    return jnp.dot(a, b, preferred_element_type=F32)


def _dot_nt(a, b):
    return lax.dot_general(a, b, (((1,), (1,)), ((), ())), preferred_element_type=F32)


def _split(a):
    hi = a.astype(BF16)
    lo = (a - hi.astype(F32)).astype(BF16)
    return hi, lo


def _dot3(a, b):
    ah, al = _split(a)
    bh, bl = _split(b)
    return _dot(ah, bh) + _dot(ah, bl) + _dot(al, bh)


def _dot3_narrow(a, b):
    ah, al = _split(a)
    bh, bl = _split(b)
    n = b.shape[1]
    p = _dot(ah, jnp.concatenate([bh, bl], axis=1))
    return p[:, :n] + p[:, n:] + _dot(al, bh)


def _sigmoid(x):
    return 1.0 / (1.0 + jnp.exp(-x))


def _group_affine(y, mul, add):
    parts = []
    for gi in range(y.shape[0] // CHUNK):
        p = y[gi * CHUNK:(gi + 1) * CHUNK]
        if mul is not None:
            p = p * mul[gi:gi + 1]
        if add is not None:
            p = p + add[gi:gi + 1]
        parts.append(p)
    return jnp.concatenate(parts, axis=0)


def _norm_mod(x, g, shift, scale):
    ms = jnp.mean(x * x, axis=-1, keepdims=True)
    return _group_affine(x * lax.rsqrt(ms + EPS) * g, 1.0 + scale, shift)


def _mod_spec(part):
    return pl.BlockSpec((TM // CHUNK, D_MODEL), lambda i: (i, part))


SHIFT_MIX, SCALE_MIX, GATE_MIX, SHIFT_FFN, SCALE_FFN, GATE_FFN = range(6)


def _on_token_tile(xp_ref, xs_ref, n_prompt_tiles, body):
    @pl.when(pl.program_id(0) < n_prompt_tiles)
    def _():
        body(xp_ref)

    @pl.when(pl.program_id(0) >= n_prompt_tiles)
    def _():
        body(xs_ref)


def _token_specs(n_prompt_tiles, d):
    return [pl.BlockSpec((TM, d), lambda i: (jnp.minimum(i, n_prompt_tiles - 1), 0)),
            pl.BlockSpec((TM, d), lambda i: (0, 0))]


def _ada_kernel(c_ref, w_ref, b_ref, o_ref):
    c = c_ref[...]
    o_ref[0] = _dot3(c * _sigmoid(c), w_ref[0]) + b_ref[0]


def _ada_call(c16, w_ada, b_ada):
    d = D_MODEL
    tn = 1024
    return pl.pallas_call(
        _ada_kernel,
        grid=(DEPTH, 6 * d // tn),
        in_specs=[pl.BlockSpec((SEQ_ROWS, d), lambda l, j: (0, 0)),
                  pl.BlockSpec((1, d, tn), lambda l, j: (l, 0, j)),
                  pl.BlockSpec((1, 1, tn), lambda l, j: (l, 0, j))],
        out_specs=pl.BlockSpec((1, SEQ_ROWS, tn), lambda l, j: (l, 0, j)),
        out_shape=jax.ShapeDtypeStruct((DEPTH, SEQ_ROWS, 6 * d), F32),
        compiler_params=_cparams(("arbitrary", "arbitrary")),
        name="ada",
    )(c16, w_ada, b_ada.reshape(DEPTH, 1, 6 * d))


def _inproj_even_kernel(xp_ref, xs_ref, sh_ref, sc_ref, g_ref, w_ref, wla_ref, wg_ref, bg_ref,
                        qa_ref, ka_ref, va_ref, ra_ref, qb_ref, kb_ref, vb_ref, ga_ref, *, n_prompt_tiles):
    def body(x_ref):
        t = x_ref.shape[0]
        outs = ((qa_ref, 0, 256, DKA ** -0.5), (ka_ref, 256, 512, None), (va_ref, 512, 1024, None),
                (ra_ref, 1024, 1536, None), (qb_ref, 1536, 2048, DHB ** -0.5 * LOG2E), (kb_ref, 2048, 2560, None),
                (vb_ref, 2560, 3072, None))
        shift, scale_ = sh_ref[...], sc_ref[...]
        halves = [slice(0, t // 2), slice(t // 2, t)]
        grp = [slice(0, t // (2 * CHUNK)), slice(t // (2 * CHUNK), t // CHUNK)]
        hbs = [_norm_mod(x_ref[rs, :], g_ref[...], shift[gs], scale_[gs]).astype(BF16) for rs, gs in zip(halves, grp)]
        for rs, hb in zip(halves, hbs):
            zs = [_dot(hb, w_ref[:, lo:hi]) for _, lo, hi, _ in outs]
            la = _dot(hb, wla_ref[...])
            for z, (o_ref, _, _, scale) in zip(zs, outs):
                o_ref[rs, :] = (z if scale is None else z * scale).astype(BF16)
            gl = _dot3(la, wg_ref[...]) + bg_ref[...]
            ga_ref[rs, :] = -(jnp.maximum(-gl, 0.0) + jnp.log(1.0 + jnp.exp(-jnp.abs(gl)))) * (1.0 / GATE_TAU)

    _on_token_tile(xp_ref, xs_ref, n_prompt_tiles, body)


def _inproj_even_call(xp, xs, mods, g, w_main, w_la, w_gate, b_gate):
    d = xp.shape[1]
    npt = xp.shape[0] // TM
    n = xp.shape[0] + xs.shape[0]
    row = lambda i: (i, 0)
    const = lambda i: (0, 0)
    widths = (256, 256, 512, 512, 512, 512, 512)
    out_shape = [jax.ShapeDtypeStruct((n, w), BF16) for w in widths] + [jax.ShapeDtypeStruct((n, 256), F32)]
    out_specs = [pl.BlockSpec((TM, w), row) for w in widths] + [pl.BlockSpec((TM, 256), row)]
    return pl.pallas_call(
        functools.partial(_inproj_even_kernel, n_prompt_tiles=npt),
        grid=(n // TM,),
        in_specs=_token_specs(npt, d) + [
            _mod_spec(SHIFT_MIX), _mod_spec(SCALE_MIX),
            pl.BlockSpec((1, d), const),
            pl.BlockSpec(w_main.shape, const), pl.BlockSpec(w_la.shape, const),
            pl.BlockSpec(w_gate.shape, const), pl.BlockSpec(b_gate.shape, const)],
        out_specs=out_specs, out_shape=out_shape,
        compiler_params=_cparams(("parallel",)),
        name="inproj_even",
    )(xp, xs, mods, mods, g, w_main, w_la, w_gate, b_gate)


def _rope(x, cos, sin_signed):
    t, w = x.shape
    lane = lax.broadcasted_iota(jnp.int32, (1, w), 1)
    first_half = (lane & 63) < 32
    rot = jnp.where(first_half, pltpu.roll(x, w - 32, 1), pltpu.roll(x, 32, 1))
    reps = w // 128
    return x * jnp.tile(cos, (1, reps)) + rot * jnp.tile(sin_signed, (1, reps))


def _unpack_pairs(slabs, dtype):
    lo = [pltpu.bitcast(s << 16, F32) for s in slabs]
    hi = [pltpu.bitcast(s & jnp.int32(-65536), F32) for s in slabs]
    return jnp.concatenate(lo + hi, axis=1).astype(dtype)


def _pack_pairs(x):
    bits = pltpu.bitcast(x.astype(BF16).astype(F32), jnp.int32)
    half = x.shape[1] // 2
    packed = ((bits[:, :half] >> 16) & jnp.int32(0xFFFF)) | (bits[:, half:] & jnp.int32(-65536))
    return [packed[:, 128 * s:128 * (s + 1)] for s in range(half // 128)]


def _add_moe(xn_ref, z_ref, gate_ref):
    y = _unpack_pairs([z_ref[s] for s in range(z_ref.shape[0])], F32)
    return xn_ref[...] + _group_affine(y, gate_ref[...], None)


def _rope_tables(lp, ls_, bp, bs):
    assert PAST_LEN + ls_ <= lp and lp % 128 == 0 and bs * ls_ == TM
    half = DHC // 2
    inv = ROPE_THETA ** (-jnp.arange(half, dtype=F32) / half)
    inv = jnp.tile(inv, 128 // half)
    sign = jnp.asarray(np.tile(np.repeat([-1.0, 1.0], half), 128 // DHC), F32)
    a = jnp.asarray(np.arange(lp // 128) * 128, F32)[:, None] * inv[None, :]
    b = jnp.asarray(np.arange(128), F32)[:, None] * inv[None, :]
    ca, sa, cb, sb = jnp.cos(a)[:, None], jnp.sin(a)[:, None], jnp.cos(b)[None], jnp.sin(b)[None]
    cos = (ca * cb - sa * sb).reshape(lp, 128)
    sin = ((sa * cb + ca * sb) * sign).reshape(lp, 128)
    with_sample = lambda t: jnp.concatenate([t, jnp.tile(t[PAST_LEN:PAST_LEN + ls_], (bs, 1))], axis=0)
    tiles = lp // TM
    return with_sample(cos), with_sample(sin), lambda i: (jnp.where(i < bp * tiles, i % tiles, tiles), 0)


def _inproj_odd_kernel(xn_ref, z_ref, gate_ref, sh_ref, sc_ref, g_ref, cos_ref, sin_ref, w_ref,
                       x_ref, q_ref, k_ref, v_ref):
    x = _add_moe(xn_ref, z_ref, gate_ref)
    x_ref[...] = x
    hb = _norm_mod(x, g_ref[...], sh_ref[...], sc_ref[...]).astype(BF16)
    cos, sin = cos_ref[...], sin_ref[...]
    q = _rope(_dot(hb, w_ref[:, 0:1024]), cos, sin)
    q_ref[...] = (q * (DHC ** -0.5 * LOG2E)).astype(BF16)
    k_ref[...] = _rope(_dot(hb, w_ref[:, 1024:1280]), cos, sin).astype(BF16)
    v_ref[...] = _dot(hb, w_ref[:, 1280:1536]).astype(BF16)


def _inproj_odd_call(xn, z, mods_prev, mods, g, cos, sin, rope_map, w):
    n, d = xn.shape
    row = lambda i: (i, 0)
    const = lambda i: (0, 0)
    widths = (1024, 256, 256)
    return pl.pallas_call(
        _inproj_odd_kernel,
        grid=(n // TM,),
        in_specs=[pl.BlockSpec((TM, d), row), pl.BlockSpec((z.shape[0], TM, 128), lambda i: (0, i, 0)),
                  _mod_spec(GATE_FFN), _mod_spec(SHIFT_MIX), _mod_spec(SCALE_MIX),
                  pl.BlockSpec((1, d), const),
                  pl.BlockSpec((TM, 128), rope_map), pl.BlockSpec((TM, 128), rope_map),
                  pl.BlockSpec(w.shape, const)],
        out_specs=[pl.BlockSpec((TM, d), row)] + [pl.BlockSpec((TM, wd), row) for wd in widths],
        out_shape=[jax.ShapeDtypeStruct((n, d), F32)] + [jax.ShapeDtypeStruct((n, wd), BF16) for wd in widths],
        compiler_params=_cparams(("parallel",)),
        name="inproj_odd",
    )(xn, z, mods_prev, mods, mods, g, cos, sin, w)


def _gla_tri():
    t = np.arange(CHUNK)[:, None]
    s = np.arange(CHUNK)[None, :]
    cum = s <= t
    start = s < (t // SUB) * SUB
    end = s < (t // SUB + 1) * SUB
    return jnp.asarray(np.concatenate([cum, start, end], axis=0).astype(np.float32), dtype=BF16)


def _gla_kernel(q_ref, k_ref, v_ref, g_ref, r_ref, s0_ref, gn_ref, tri_ref, o_ref, sout_ref, s_ref, *, nb):
    c_ = CHUNK
    nsub = c_ // SUB

    @pl.when(pl.program_id(1) == 0)
    def _():
        s_ref[...] = s0_ref[0]

    tri = tri_ref[...]
    lane = lax.broadcasted_iota(jnp.int32, (1, 128), 1)
    hmask = [jnp.where(lane < DKA, 1.0, 0.0), jnp.where(lane >= DKA, 1.0, 0.0)]
    ti = lax.broadcasted_iota(jnp.int32, (c_, c_), 0)
    si = lax.broadcasted_iota(jnp.int32, (c_, c_), 1)
    rb, cb = ti >> 4, si >> 4
    m_diag = (rb == cb) & (si <= ti)
    m_off = [(cb == j) & (rb > j) for j in range(nsub - 1)]
    hk = HA * DKA
    eye = lax.broadcasted_iota(jnp.int32, (hk, hk), 0) == lax.broadcasted_iota(jnp.int32, (hk, hk), 1)
    gn = gn_ref[...]

    chunks = range(nb)
    heads = [(p, hh) for p in range(HA // 2) for hh in range(2)]
    rows = [slice(c * c_, (c + 1) * c_) for c in chunks]
    pair = [slice(128 * p, 128 * (p + 1)) for p in range(HA // 2)]
    css = []
    for c in chunks:
        g_hi, g_lo = _split(g_ref[rows[c], :])
        css.append(_dot(tri, g_hi) + _dot(tri, g_lo))
    lhs1, lhs2, kds, kes, q_inter, klts, dcols = [], [], [], [], [], [], []
    for c in chunks:
        b, rs, re = css[c][0:c_], css[c][c_:2 * c_], css[c][2 * c_:3 * c_]
        q = q_ref[rows[c], :].astype(F32)
        k = k_ref[rows[c], :].astype(F32)
        bl = b[c_ - 1:c_, :]
        qd = q * jnp.exp(b - rs)
        kd = k * jnp.exp(rs - b)
        ke = k * jnp.exp(re - b)
        qi = q * jnp.exp(b)
        kl = k * jnp.exp(bl - b)
        ql = [q * jnp.exp(jnp.minimum(b - b[SUB * (j + 1) - 1:SUB * (j + 1), :], 0.0)) for j in range(nsub - 1)]
        dcols.append(jnp.sum(jnp.where(eye, jnp.broadcast_to(jnp.exp(bl), (hk, hk)), 0.0), axis=1, keepdims=True))
        kds.append([kd[:, ls].astype(BF16) for ls in pair])
        kes.append([ke[:, ls].astype(BF16) for ls in pair])
        klts.append([kl[:, ls].T.astype(BF16) for ls in pair])
        lhs1.append([(qd[:, pair[p]] * hmask[hh]).astype(BF16) for p, hh in heads])
        lhs2.append([jnp.concatenate([ql[j][:, pair[p]] * hmask[hh] for j in range(nsub - 1)], axis=0).astype(BF16)
                     for p, hh in heads])
        q_inter.append([(qi[:, pair[p]] * hmask[hh]).astype(BF16) for p, hh in heads])
    a1s = [[_dot_nt(lhs1[c][h], kds[c][p]) for h, (p, hh) in enumerate(heads)] for c in chunks]
    a2s = [[_dot_nt(lhs2[c][h], kes[c][p]) for h, (p, hh) in enumerate(heads)] for c in chunks]
    atts = []
    for c in chunks:
        per_head = []
        for h in range(HA):
            att = jnp.zeros((c_, c_), F32)
            for j in reversed(range(nsub - 1)):
                att = jnp.where(m_off[j], a2s[c][h][j * c_:(j + 1) * c_], att)
            per_head.append(jnp.where(m_diag, a1s[c][h], att).astype(BF16))
        atts.append(per_head)
    vs_ = [[v_ref[rows[c], DVA * h:DVA * (h + 1)] for h in range(HA)] for c in chunks]
    o_intra = [[_dot(atts[c][h], vs_[c][h]) for h in range(HA)] for c in chunks]
    upds = [jnp.concatenate([_dot(klts[c][p][DKA * hh:DKA * (hh + 1)], vs_[c][2 * p + hh]) for p, hh in heads], axis=0)
            for c in chunks]

    s_cur = s_ref[...]
    s_in = []
    for c in chunks:
        s_in.append(s_cur.astype(BF16))
        s_cur = dcols[c] * s_cur + upds[c]
    s_ref[...] = s_cur
    sout_ref[0] = s_cur

    for c in chunks:
        for h in range(HA):
            o = o_intra[c][h] + _dot(q_inter[c][h], s_in[c][pair[h // 2], :])
            ms = jnp.mean(o * o, axis=-1, keepdims=True)
            vs = slice(DVA * h, DVA * (h + 1))
            rr = r_ref[rows[c], vs].astype(F32)
            o_ref[rows[c], vs] = (o * lax.rsqrt(ms + EPS) * gn * (rr * _sigmoid(rr))).astype(BF16)


def _gla_call(q, k, v, g, r, s0, gn, o_prev, *, n_seq, seq_rows, row0, nb):
    tq = nb * CHUNK
    steps = seq_rows // tq
    blk0 = row0 // tq
    row = lambda b, j: (blk0 + b * steps + j, 0)
    const = lambda b, j: (0, 0)
    tri = _gla_tri()
    in_specs = [pl.BlockSpec((tq, 256), row), pl.BlockSpec((tq, 256), row), pl.BlockSpec((tq, 512), row),
                pl.BlockSpec((tq, 256), row), pl.BlockSpec((tq, 512), row),
                pl.BlockSpec((1, 256, 128), lambda b, j: (b, 0, 0)),
                pl.BlockSpec((1, 128), const), pl.BlockSpec(tri.shape, const)]
    args = [q, k, v, g, r, s0, gn, tri]
    aliases = {}
    if o_prev is not None:
        in_specs.append(pl.BlockSpec(memory_space=pl.ANY))
        args.append(o_prev)
        aliases = {len(args) - 1: 0}
    kern = functools.partial(_gla_kernel, nb=nb)
    if o_prev is not None:
        kern = _drop_arg(kern, 8)
    return pl.pallas_call(
        kern,
        grid=(n_seq, steps),
        in_specs=in_specs,
        out_specs=[pl.BlockSpec((tq, 512), row), pl.BlockSpec((1, 256, 128), lambda b, j: (b, 0, 0))],
        out_shape=[jax.ShapeDtypeStruct((q.shape[0], 512), BF16), jax.ShapeDtypeStruct((n_seq, 256, 128), F32)],
        scratch_shapes=[pltpu.VMEM((256, 128), F32)],
        input_output_aliases=aliases,
        compiler_params=_cparams(("arbitrary", "arbitrary")),
        name="gla",
    )(*args)


def _drop_arg(fn, idx):
    def wrapped(*refs):
        return fn(*refs[:idx], *refs[idx + 1:])
    return wrapped


def _window(prev_ref, cur_ref, lo, hi, pb, ls):
    if lo < pb:
        return jnp.concatenate([prev_ref[lo:pb, ls], cur_ref[0:hi - pb, ls]], axis=0)
    return cur_ref[lo - pb:hi - pb, ls]


def _band_kernel(q_ref, kp_ref, kc_ref, vp_ref, vc_ref, bias_ref, o_ref, *, g, n_sub, pb):
    qs = CHUNK * g
    kw_rows = pb + qs
    lane = lax.broadcasted_iota(jnp.int32, (1, 128), 1)
    low = lane < DHB
    hmask = [jnp.where(low, 1.0, 0.0), jnp.where(low, 0.0, 1.0)]
    for s in range(n_sub):
        sb = s if bias_ref.shape[0] > 1 else 0
        rows = slice(qs * s, qs * (s + 1))
        lanes = [slice(128 * p, 128 * (p + 1)) for p in range(HB // 2)]
        heads = [(p, hh) for p in range(HB // 2) for hh in range(2)]
        qps = [q_ref[rows, ls].astype(F32) for ls in lanes]
        kws = [_window(kp_ref, kc_ref, qs * s, qs * s + kw_rows, pb, ls) for ls in lanes]
        vws = [_window(vp_ref, vc_ref, qs * s, qs * s + kw_rows, pb, ls) for ls in lanes]
        scs = [_dot_nt((qps[p] * hmask[hh]).astype(BF16), kws[p]) + bias_ref[sb, 2 * p + hh] for p, hh in heads]
        pes = [jnp.exp2(sc - jnp.max(sc, axis=-1, keepdims=True)) for sc in scs]
        outs = [_dot(pe.astype(BF16), vws[p]) / jnp.sum(pe, axis=-1, keepdims=True) for pe, (p, hh) in zip(pes, heads)]
        for p, ls in enumerate(lanes):
            o_ref[rows, ls] = jnp.where(low, outs[2 * p], outs[2 * p + 1]).astype(BF16)


def _band_valid(g, pb, n_sub=None):
    rows, kw = CHUNK * g, pb + CHUNK * g
    r = np.arange(rows)[:, None]
    c = np.arange(kw)[None, :]
    dd = c // CHUNK - r // CHUNK
    band = (dd >= 0) & (dd <= pb // CHUNK)
    if n_sub is None:
        return band[None]
    return np.stack([band & (c >= pb - rows * s) for s in range(n_sub)])


def _band_bias(table, g, pb, valid):
    rows, kw = CHUNK * g, pb + CHUNK * g
    period = kw + rows
    m = np.arange(period)
    m = np.where(m < kw, m, m - period)
    ext = table[:, np.clip(m - pb, -MAX_REL, MAX_REL) + MAX_REL] * LOG2E
    flat = jnp.tile(ext, (1, rows))[:, :rows * (period - 1)]
    bias = flat.reshape(table.shape[0], rows, period - 1)[:, :, :kw]
    return jnp.where(valid[:, None], bias[None], -jnp.inf)


def _attn_call(kernel, q, kp, kc, vp, vc, extra, extra_specs, o_prev, *, width, kv_width, tq, pb,
               n_blocks, blk_map, prev_map, name):
    row = lambda i: (blk_map(i), 0)
    prev = lambda i: (prev_map(i), 0)
    in_specs = [pl.BlockSpec((tq, width), row),
                pl.BlockSpec((pb, kv_width), prev), pl.BlockSpec((tq, kv_width), row),
                pl.BlockSpec((pb, kv_width), prev), pl.BlockSpec((tq, kv_width), row)] + extra_specs
    args = [q, kp, kc, vp, vc] + extra
    aliases = {}
    if o_prev is not None:
        in_specs.append(pl.BlockSpec(memory_space=pl.ANY))
        args.append(o_prev)
        aliases = {len(args) - 1: 0}
        kernel = _drop_arg(kernel, len(args) - 1)
    return pl.pallas_call(
        kernel,
        grid=(n_blocks,),
        in_specs=in_specs,
        out_specs=pl.BlockSpec((tq, width), row),
        out_shape=jax.ShapeDtypeStruct((q.shape[0], width), BF16),
        input_output_aliases=aliases,
        compiler_params=_cparams(("parallel",)),
        name=name,
    )(*args)


def _attention(kernel_fn, q, k, v, cache_k, cache_v, masks, extra, extra_specs, *, width, kv_width, pb, tq, g,
               bp, lp, bs, name):
    bps = lp // tq
    n_sub = tq // (CHUNK * g)
    spec = lambda a: [pl.BlockSpec(a.shape, lambda i: (0,) * a.ndim)]
    kern = functools.partial(kernel_fn, g=g, n_sub=n_sub, pb=pb)
    common = dict(width=width, kv_width=kv_width, pb=pb)
    main = lambda i: (i // (bps - 1)) * bps + i % (bps - 1) + 1
    o = _attn_call(kern, q, k, k, v, v, [masks[0]] + extra, spec(masks[0]) + extra_specs, None, tq=tq,
                   n_blocks=bp * (bps - 1), blk_map=main, prev_map=lambda i: main(i) * (tq // pb) - 1,
                   name=name + "_main", **common)
    first = lambda i: i * bps
    o = _attn_call(kern, q, k, k, v, v, [masks[1]] + extra, spec(masks[1]) + extra_specs, o, tq=tq,
                   n_blocks=bp, blk_map=first, prev_map=lambda i: jnp.maximum(first(i) * (tq // pb) - 1, 0),
                   name=name + "_first", **common)
    samp = functools.partial(kernel_fn, g=1, n_sub=1, pb=pb)
    return _attn_call(samp, q, cache_k, k, cache_v, v, [masks[2]] + extra, spec(masks[2]) + extra_specs, o, tq=CHUNK,
                      n_blocks=bs, blk_map=lambda i: bp * lp // CHUNK + i, prev_map=lambda i: i,
                      name=name + "_sample", **common)


def _swa_kernel(q_ref, kp_ref, kc_ref, vp_ref, vc_ref, mask_ref, sink_ref, o_ref, *, g, n_sub, pb):
    qs = CHUNK * g
    kw_rows = pb + qs
    lane = lax.broadcasted_iota(jnp.int32, (1, 128), 1)
    low = lane < DHC
    hmask = [jnp.where(low, 1.0, 0.0), jnp.where(low, 0.0, 1.0)]
    pairs_per_kv = HC // KVC // 2
    for s in range(n_sub):
        msk = mask_ref[s if mask_ref.shape[0] > 1 else 0]
        rows = slice(qs * s, qs * (s + 1))
        kws = [_window(kp_ref, kc_ref, qs * s, qs * s + kw_rows, pb, slice(128 * kv, 128 * (kv + 1))) for kv in range(KVC)]
        vws = [_window(vp_ref, vc_ref, qs * s, qs * s + kw_rows, pb, slice(128 * kv, 128 * (kv + 1))) for kv in range(KVC)]
        heads = [(j, hh) for j in range(HC // 2) for hh in range(2)]
        qps = [q_ref[rows, 128 * j:128 * (j + 1)].astype(F32) for j in range(HC // 2)]
        scs = [_dot_nt((qps[j] * hmask[hh]).astype(BF16), kws[j // pairs_per_kv]) + msk for j, hh in heads]
        sks = [sink_ref[0, 2 * j + hh] for j, hh in heads]
        ms = [jnp.maximum(jnp.max(sc, axis=-1, keepdims=True), sk) for sc, sk in zip(scs, sks)]
        pes = [jnp.exp2(sc - m) for sc, m in zip(scs, ms)]
        outs = [_dot(pe.astype(BF16), vws[j // pairs_per_kv]) / (jnp.sum(pe, axis=-1, keepdims=True) + jnp.exp2(sk - m))
                for pe, sk, m, (j, hh) in zip(pes, sks, ms, heads)]
        for j in range(HC // 2):
            o_ref[rows, 128 * j:128 * (j + 1)] = jnp.where(low, outs[2 * j], outs[2 * j + 1]).astype(BF16)


def _route(logits_t):
    a = [logits_t[4 * j:4 * j + 4] for j in range(EXP_PER_GROUP)]

    def first_argmax(vals, m):
        idx = jnp.full(m.shape, float(len(vals) - 1), F32)
        for j in reversed(range(len(vals) - 1)):
            idx = jnp.where(vals[j] == m, float(j), idx)
        return idx

    m1 = functools.reduce(jnp.maximum, a)
    i1 = first_argmax(a, m1)
    bsec = [jnp.where(i1 == float(j), -jnp.inf, a[j]) for j in range(EXP_PER_GROUP)]
    m2 = functools.reduce(jnp.maximum, bsec)
    i2 = first_argmax(bsec, m2)
    rows = lambda x: [x[gi:gi + 1] for gi in range(N_GROUPS)]
    gm = functools.reduce(jnp.maximum, rows(m1))
    gscore = jnp.exp(m1 - gm) + jnp.exp(m2 - gm)
    gs = rows(gscore)
    gsel = first_argmax(gs, functools.reduce(jnp.maximum, gs))

    def pick(x):
        xr = rows(x)
        out = xr[N_GROUPS - 1]
        for gi in reversed(range(N_GROUPS - 1)):
            out = jnp.where(gsel == float(gi), xr[gi], out)
        return out

    p1 = jnp.exp(pick(m1) - gm)
    p2 = jnp.exp(pick(m2) - gm)
    w1 = p1 / (p1 + p2)
    w2 = p2 / (p1 + p2)
    s1, s2 = pick(i1), pick(i2)
    lo, hi = jnp.minimum(s1, s2), jnp.maximum(s1, s2)
    pair = jnp.where(lo == 0.0, hi - 1.0, jnp.where(lo == 1.0, hi + 1.0, 5.0))
    bucket = gsel * float(N_PAIRS) + pair
    first_is_lo = s1 < s2
    return bucket, jnp.where(first_is_lo, w1, w2), jnp.where(first_is_lo, w2, w1)


def _outproj_kernel(*refs, n_x, n_o, n_prompt_tiles):
    x_refs = refs[:n_x]
    o_refs = refs[n_x:n_x + n_o]
    w_refs = refs[n_x + n_o:n_x + 2 * n_o]
    (gate_ref, nf_ref, sh_ref, sc_ref, wr_ref, br_ref, tri_ref,
     xn_ref, disp_ref, meta_ref, cnt_ref, run_ref) = refs[n_x + 2 * n_o:]
    t = xn_ref.shape[0]

    @pl.when(pl.program_id(0) == 0)
    def _():
        run_ref[...] = jnp.zeros_like(run_ref)

    x_src = x_refs[0]
    if n_x == 2:
        def stage(x_ref):
            xn_ref[...] = x_ref[...]

        _on_token_tile(x_refs[0], x_refs[1], n_prompt_tiles, stage)
        x_src = xn_ref

    halves = [slice(0, t // 2), slice(t // 2, t)]
    grp = [slice(0, t // (2 * CHUNK)), slice(t // (2 * CHUNK), t // CHUNK)]
    ys = []
    for rs in halves:
        y = _dot(o_refs[0][rs, :], w_refs[0][...])
        for i in range(1, n_o):
            y = y + _dot(o_refs[i][rs, :], w_refs[i][...])
        ys.append(y)
    gate, shift, scale = gate_ref[...], sh_ref[...], sc_ref[...]
    gys = [_group_affine(y, gate[gs], None) for y, gs in zip(ys, grp)]

    for rs, gy in zip(halves, gys):
        xn_ref[rs, :] = x_src[rs, :] + gy
    hs = [_norm_mod(xn_ref[rs, :], nf_ref[...], shift[gs], scale[gs]) for rs, gs in zip(halves, grp)]
    for rs, h in zip(halves, hs):
        for s, slab in enumerate(_pack_pairs(h)):
            disp_ref[s, rs, :] = slab
    logits_t = [(_dot3_narrow(h, wr_ref[...]) + br_ref[...]).T[0:N_EXPERTS] for h in hs]
    bucket, w_lo, w_hi = _route(jnp.concatenate(logits_t, axis=1))
    r128 = lax.broadcasted_iota(jnp.int32, (128, t), 0)
    tok = (pl.program_id(0) * t + lax.broadcasted_iota(jnp.int32, (1, t), 1)).astype(F32)
    aux = jnp.where(r128 == 0, w_lo, jnp.where(r128 == 1, w_hi, jnp.where(r128 == 2, tok, 0.0))).T
    disp_ref[disp_ref.shape[0] - 1] = pltpu.bitcast(aux, jnp.int32)
    brow = lax.broadcasted_iota(jnp.int32, (BUCKET_ROWS, t), 0).astype(F32)
    onehot = jnp.where(brow == bucket, 1.0, 0.0)
    before = _dot(onehot.astype(BF16), tri_ref[...]) + run_ref[:, 0:1]
    rank = jnp.sum(onehot * before, axis=0, keepdims=True)
    run_ref[...] = run_ref[...] + jnp.sum(onehot, axis=1, keepdims=True)
    cnt_ref[...] = run_ref[...]
    r8 = lax.broadcasted_iota(jnp.int32, (8, t), 0)
    meta_ref[...] = jnp.where(r8 == 0, bucket, jnp.where(r8 == 1, rank, 0.0)).astype(jnp.int32)


def _outproj_call(xs_, os_, ws, mods, nf, wr, br, n_pad):
    d = xs_[0].shape[1]
    n = sum(a.shape[0] for a in xs_)
    npt = xs_[0].shape[0] // TM
    row = lambda i: (i, 0)
    const = lambda i: (0, 0)
    n_o = len(os_)
    in_specs = ((_token_specs(npt, d) if len(xs_) == 2 else [pl.BlockSpec((TM, d), row)])
                + [pl.BlockSpec((TM, o.shape[1]), row) for o in os_]
                + [pl.BlockSpec(w.shape, const) for w in ws]
                + [_mod_spec(GATE_MIX), pl.BlockSpec((1, d), const),
                   _mod_spec(SHIFT_FFN), _mod_spec(SCALE_FFN),
                   pl.BlockSpec(wr.shape, const), pl.BlockSpec(br.shape, const),
                   pl.BlockSpec((TM, TM), const)])
    tri = jnp.asarray(np.triu(np.ones((TM, TM), np.float32), k=1), dtype=BF16)
    return pl.pallas_call(
        functools.partial(_outproj_kernel, n_x=len(xs_), n_o=n_o, n_prompt_tiles=npt),
        grid=(n // TM,),
        in_specs=in_specs,
        out_specs=[pl.BlockSpec((TM, d), row), pl.BlockSpec((DISP_SLABS, TM, 128), lambda i: (0, i, 0)),
                   pl.BlockSpec((8, TM), lambda i: (0, i)), pl.BlockSpec((BUCKET_ROWS, 128), const)],
        out_shape=[jax.ShapeDtypeStruct((n, d), F32), jax.ShapeDtypeStruct((DISP_SLABS, n_pad, 128), jnp.int32),
                   jax.ShapeDtypeStruct((8, n), jnp.int32), jax.ShapeDtypeStruct((BUCKET_ROWS, 128), F32)],
        scratch_shapes=[pltpu.VMEM((BUCKET_ROWS, 128), F32)],
        compiler_params=_cparams(("arbitrary",)),
        name="outproj_router",
    )(*xs_, *os_, *ws, mods, nf, mods, mods, wr, br, tri)


def _sc_mesh():
    return plsc.VectorSubcoreMesh(core_axis_name="core", subcore_axis_name="subcore")


def _sc_row_copy(src, idx, n_out, scatter):
    r = idx.shape[0]
    k = SC_GROUP
    w_per = r // (SC_WINDOW * SC_WORKERS)
    assert r % (SC_WINDOW * SC_WORKERS) == 0 and w_per % k == 0
    n_groups = w_per // k

    @functools.partial(
        pl.kernel, out_type=jax.ShapeDtypeStruct((n_out, 128), src.dtype), mesh=_sc_mesh(),
        scratch_types=[pltpu.VMEM((w_per, SC_WINDOW), jnp.int32),
                       pltpu.VMEM((2 * k, SC_WINDOW, 128), src.dtype),
                       pltpu.SemaphoreType.DMA((2,)), pltpu.SemaphoreType.DMA((2,))])
    def copy(x_hbm, i_hbm, o_hbm, ibuf, xbuf, in_sem, out_sem):
        wid = lax.axis_index("core") * (SC_WORKERS // 2) + lax.axis_index("subcore")
        pltpu.sync_copy(i_hbm.at[wid], ibuf)
        first = wid * w_per

        def rows(j):
            return pl.ds((first + j) * SC_WINDOW, SC_WINDOW)

        def start_in(g, slot):
            cps = []
            for c in range(k):
                j = g * k + c
                s = x_hbm.at[rows(j)] if scatter else x_hbm.at[ibuf.at[j]]
                cps.append(pltpu.async_copy(s, xbuf.at[slot * k + c], in_sem.at[slot]))
            return cps

        def start_out(g, slot):
            cps = []
            for c in range(k):
                j = g * k + c
                dst = o_hbm.at[ibuf.at[j]] if scatter else o_hbm.at[rows(j)]
                cps.append(pltpu.async_copy(xbuf.at[slot * k + c], dst, out_sem.at[slot]))
            return cps

        pending_in = start_in(0, 0)
        for g in range(n_groups):
            slot = g % 2
            for cp in pending_in:
                cp.wait()
            pending_out = start_out(g, slot)
            if g + 1 < n_groups:
                pending_in = start_in(g + 1, 1 - slot)
            for cp in pending_out:
                cp.wait()

    return copy(src, idx.reshape(SC_WORKERS, w_per, SC_WINDOW))


def _sc_scatter_rows(src, idx, n_out):
    assert idx.shape == (src.shape[0],)
    return _sc_row_copy(src, idx, n_out, scatter=True)


def _moe_kernel(elo_ref, ehi_ref, nvalid_ref, xs_ref, *refs, n_tok, dump_tiles):
    w_refs, (y_ref, tok_ref) = refs[:4 * MOE_TILES], refs[4 * MOE_TILES:]
    step = pl.program_id(0)
    t = TMO
    tiles = range(MOE_TILES)
    rows = [slice(t * j, t * (j + 1)) for j in tiles]
    auxs = [pltpu.bitcast(xs_ref[Y_SLABS, rows[j], :], F32) for j in tiles]
    r = lax.broadcasted_iota(jnp.int32, (1, t), 1)
    for j in tiles:
        i = step * MOE_TILES + j
        spare = n_tok + (i % dump_tiles) * t + r
        tok = jnp.where(r < nvalid_ref[i], auxs[j].T[2:3, :].astype(jnp.int32), spare)
        for c in range(t // 128):
            tok_ref[j, c:c + 1, :] = tok[:, 128 * c:128 * (c + 1)]

    any_tokens = nvalid_ref[step * MOE_TILES] > 0
    for j in range(1, MOE_TILES):
        any_tokens = jnp.logical_or(any_tokens, nvalid_ref[step * MOE_TILES + j] > 0)

    @pl.when(any_tokens)
    def _():
        units = [(j, e) for j in tiles for e in range(2)]
        hs = [_unpack_pairs([xs_ref[s, rows[j], :] for s in range(Y_SLABS)], BF16) for j in tiles]
        abs_ = [_dot(hs[j], w_refs[4 * j + e][0, 0]) for j, e in units]
        acts = [(ab[:, :D_FF] * _sigmoid(ab[:, :D_FF]) * ab[:, D_FF:]).astype(BF16) for ab in abs_]
        ys = [_dot(act, w_refs[4 * j + 2 + e][0, 0]) for act, (j, e) in zip(acts, units)]
        for j in tiles:
            acc = auxs[j][:, 0:1] * ys[2 * j] + auxs[j][:, 1:2] * ys[2 * j + 1]
            for s, slab in enumerate(_pack_pairs(acc)):
                y_ref[s, rows[j], :] = slab

    @pl.when(jnp.logical_not(any_tokens))
    def _():
        y_ref[...] = jnp.zeros_like(y_ref)


def _moe_call(xs, elo, ehi, nvalid, wgu, wdn, layer, n_tiles, n_tok, dump_tiles):
    d = wgu.shape[2]
    m = MOE_TILES
    assert n_tiles % m == 0
    weight_specs = []
    for j in range(m):
        for shape in ((1, 1, d, 2 * D_FF), (1, 1, D_FF, d)):
            for sel in range(2):
                weight_specs.append(pl.BlockSpec(
                    shape, lambda i, lo, hi, v, j=j, sel=sel: (layer, (lo, hi)[sel][m * i + j], 0, 0)))
    weights = [w for _ in range(m) for w in (wgu, wgu, wdn, wdn)]
    return pl.pallas_call(
        functools.partial(_moe_kernel, n_tok=n_tok, dump_tiles=dump_tiles),
        grid_spec=pltpu.PrefetchScalarGridSpec(
            num_scalar_prefetch=3,
            grid=(n_tiles // m,),
            in_specs=[pl.BlockSpec((DISP_SLABS, m * TMO, 128), lambda i, lo, hi, v: (0, i, 0))] + weight_specs,
            out_specs=[pl.BlockSpec((Y_SLABS, m * TMO, 128), lambda i, lo, hi, v: (0, i, 0)),
                       pl.BlockSpec((m, TMO // 128, 128), lambda i, lo, hi, v: (i, 0, 0))]),
        out_shape=[jax.ShapeDtypeStruct((Y_SLABS, n_tiles * TMO, 128), jnp.int32),
                   jax.ShapeDtypeStruct((n_tiles, TMO // 128, 128), jnp.int32)],
        compiler_params=_cparams(("arbitrary",), vmem_mb=56),
        name="moe_grouped",
    )(elo, ehi, nvalid, xs, *weights)


def _after(x, token):
    return lax.optimization_barrier((x, token))[0]


def _cast_kernel(after_ref, w_ref, o_ref):
    o_ref[...] = w_ref[...].astype(o_ref.dtype)


def _cast_call(w, layer, after):
    _, e, k, n = w.shape
    return pl.pallas_call(
        _cast_kernel,
        grid=(e,),
        in_specs=[pl.BlockSpec(memory_space=pl.ANY), pl.BlockSpec((1, 1, k, n), lambda i: (layer, i, 0, 0))],
        out_specs=pl.BlockSpec((1, 1, k, n), lambda i: (0, i, 0, 0)),
        out_shape=jax.ShapeDtypeStruct((1, e, k, n), BF16),
        compiler_params=_cparams(("parallel",)),
        name="cast_weights",
    )(after, w)


def _moe_layer(disp, meta, counts, w_gate_up, w_down, layer, n, n_pad, sort_rows):
    n_tiles = sort_rows // TMO
    wgu = _cast_call(w_gate_up, layer, counts)
    wdn = _cast_call(w_down, layer, counts)
    layer = 0
    cnt = counts[:N_BUCKETS, 0].astype(jnp.int32)
    padded = ((cnt + TMO - 1) // TMO) * TMO
    ends = jnp.cumsum(padded)
    offs = ends - padded
    bucket, rank = meta[0], meta[1]
    pos = rank + jnp.sum(jnp.where(bucket[None, :] == jnp.arange(N_BUCKETS, dtype=jnp.int32)[:, None],
                                   offs[:, None], 0), axis=0)
    tile_start = jnp.arange(n_tiles, dtype=jnp.int32) * TMO
    tile_bucket = jnp.minimum(jnp.sum((tile_start[:, None] >= ends[None, :]).astype(jnp.int32), axis=1), N_BUCKETS - 1)
    pair_lo = np.array([0, 0, 0, 1, 1, 2], np.int32)
    pair_hi = np.array([1, 2, 3, 2, 3, 3], np.int32)
    b_lo = jnp.asarray(np.repeat(np.arange(N_GROUPS), N_PAIRS) * EXP_PER_GROUP + np.tile(pair_lo, N_GROUPS), jnp.int32)
    b_hi = jnp.asarray(np.repeat(np.arange(N_GROUPS), N_PAIRS) * EXP_PER_GROUP + np.tile(pair_hi, N_GROUPS), jnp.int32)
    onehot_tb = (tile_bucket[:, None] == jnp.arange(N_BUCKETS, dtype=jnp.int32)[None, :]).astype(jnp.int32)
    elo = jnp.sum(onehot_tb * b_lo[None, :], axis=1)
    ehi = jnp.sum(onehot_tb * b_hi[None, :], axis=1)
    bucket_end = jnp.sum(onehot_tb * (offs + cnt)[None, :], axis=1)
    nvalid = jnp.where(tile_start < ends[-1], jnp.clip(bucket_end - tile_start, 0, TMO), 0)
    dump = sort_rows + jnp.arange(n_pad - n, dtype=jnp.int32)
    pos_sc = jnp.concatenate([pos, dump])
    total = sort_rows + n_pad - n
    sc_idx = (pos_sc[None, :] + (jnp.arange(DISP_SLABS, dtype=jnp.int32) * total)[:, None]).reshape(-1)
    xs = _sc_scatter_rows(disp.reshape(DISP_SLABS * n_pad, 128), sc_idx, DISP_SLABS * total)
    ys, tok = _moe_call(xs.reshape(DISP_SLABS, total, 128), elo, ehi, nvalid, wgu, wdn, layer, n_tiles,
                        n, (n_pad - n) // TMO)
    back_idx = (tok.reshape(1, sort_rows) + (jnp.arange(Y_SLABS, dtype=jnp.int32) * n_pad)[:, None]).reshape(-1)
    z = _sc_scatter_rows(ys.reshape(Y_SLABS * sort_rows, 128), back_idx, Y_SLABS * n_pad)
    return z.reshape(Y_SLABS, n_pad, 128), tok


def _final_kernel(xn_ref, z_ref, gate_ref, g_ref, yp_ref, ys_ref, *, n_prompt_tiles):
    x = _add_moe(xn_ref, z_ref, gate_ref)
    ms = jnp.mean(x * x, axis=-1, keepdims=True)
    y = x * lax.rsqrt(ms + EPS) * g_ref[...]
    i = pl.program_id(0)

    @pl.when(i < n_prompt_tiles)
    def _():
        yp_ref[...] = y

    @pl.when(i >= n_prompt_tiles)
    def _():
        ys_ref[...] = y


def _final_call(xn, z, mods, g, n_prompt):
    n, d = xn.shape
    npt = n_prompt // TM
    assert n - n_prompt == TM
    return pl.pallas_call(
        functools.partial(_final_kernel, n_prompt_tiles=npt),
        grid=(n // TM,),
        in_specs=[pl.BlockSpec((TM, d), lambda i: (i, 0)), pl.BlockSpec((z.shape[0], TM, 128), lambda i: (0, i, 0)),
                  _mod_spec(GATE_FFN), pl.BlockSpec((1, d), lambda i: (0, 0))],
        out_specs=_token_specs(npt, d),
        out_shape=[jax.ShapeDtypeStruct((n_prompt, d), F32), jax.ShapeDtypeStruct((TM, d), F32)],
        compiler_params=_cparams(("arbitrary",)),
        name="final_norm",
    )(xn, z, mods, g)


def kernel(x_prompt, x_sample, c_prompt, c_sample, state_gla, cache_band_k, cache_band_v, cache_swa_k, cache_swa_v,
           w_ada, b_ada, norm_mix, norm_ffn, norm_final, w_in_even, w_gate_a, b_gate_a, gla_norm, rel_bias_b,
           w_out_even, w_in_odd, sinks_c, w_out_odd, w_router, b_router, w_gate_up, w_down):
    bp, lp, d = x_prompt.shape
    bs, ls_, _ = x_sample.shape
    n_p, n_s = bp * lp, bs * ls_
    n = n_p + n_s
    assert ls_ == CHUNK and n_s == TM and lp % TM == 0 and PAST_LEN % CHUNK == 0

    xp2, xs2 = x_prompt.reshape(n_p, d), x_sample.reshape(n_s, d)

    c16 = jnp.zeros((SEQ_ROWS, d), F32).at[:bp].set(c_prompt).at[bp:bp + bs].set(c_sample)
    mods = _ada_call(c16, w_ada, b_ada)
    seq_of_group = np.concatenate([np.repeat(np.arange(bp), lp // CHUNK), bp + np.arange(bs)])
    mods_g = [mods[l][seq_of_group] for l in range(DEPTH)]

    perm = np.array([4 * (c % 4) + c // 4 for c in range(N_EXPERTS)])
    wr = jnp.zeros((d, 128), F32).at[:, :N_EXPERTS].set(w_router[:, perm])
    br = jnp.zeros((1, 128), F32).at[0, :N_EXPERTS].set(b_router[perm])

    sc_unit = SC_WINDOW * SC_WORKERS * SC_GROUP
    n_pad = n + TMO
    while (DISP_SLABS * n_pad) % sc_unit or (Y_SLABS * n_pad) % TMO or (n_pad - n) % TMO:
        n_pad += TMO
    sort_rows = n + N_BUCKETS * TMO
    while (Y_SLABS * sort_rows) % sc_unit or sort_rows % (MOE_TILES * TMO):
        sort_rows += TMO

    gla_p = gla_s = bk_p = bv_p = bk_s = bv_s = sk_p = sv_p = sk_s = sv_s = None
    xn = z = tok = None
    for l in range(DEPTH):
        i = l // 2
        if l % 2 == 0:
            w = w_in_even[i]
            w_main = jnp.concatenate([w[:, :1536], w[:, 1552:]], axis=1).astype(BF16)
            w_la = jnp.zeros((d, 128), F32).at[:, :GATE_RANK].set(w[:, 1536:1552]).astype(BF16)
            w_gate = jnp.zeros((128, HA * DKA), F32).at[:GATE_RANK].set(w_gate_a[i])
            qa, ka, va, ra, qb, kb, vb, ga = _inproj_even_call(
                xp2, xs2, mods_g[l], norm_mix[l][None], w_main, w_la, w_gate, b_gate_a[i][None])
            xres = [xp2, xs2]
            gn = gla_norm[i][None]
            oa, s_p = _gla_call(qa, ka, va, ga, ra, jnp.zeros((bp, 256, 128), F32), gn, None,
                                n_seq=bp, seq_rows=lp, row0=0, nb=8)
            oa, s_s = _gla_call(qa, ka, va, ga, ra, state_gla[i].reshape(bs, 256, 128), gn, oa,
                                n_seq=bs, seq_rows=ls_, row0=n_p, nb=1)
            gla_p, gla_s = s_p.reshape(1, bp, HA, DKA, DVA), s_s.reshape(1, bs, HA, DKA, DVA)
            pb = N_PREV_B * CHUNK
            tq, g = 512, 2
            ck = cache_band_k[i].reshape(bs * pb, HB * DHB).astype(BF16)
            cv = cache_band_v[i].reshape(bs * pb, HB * DHB).astype(BF16)
            biases = (_band_bias(rel_bias_b[i], g, pb, _band_valid(g, pb)),
                      _band_bias(rel_bias_b[i], g, pb, _band_valid(g, pb, tq // (CHUNK * g))),
                      _band_bias(rel_bias_b[i], 1, pb, _band_valid(1, pb)))
            ob = _attention(_band_kernel, qb, kb, vb, ck, cv, biases, [], [], width=512, kv_width=512, pb=pb,
                            tq=tq, g=g, bp=bp, lp=lp, bs=bs, name="band")
            tail = lambda a: jnp.stack([a[(b + 1) * lp - pb:(b + 1) * lp] for b in range(bp)]).astype(F32).reshape(1, bp, pb, HB, DHB)
            new = lambda a: a[n_p:].astype(F32).reshape(bs, ls_, HB, DHB)
            bk_p, bv_p = tail(kb), tail(vb)
            bk_s = jnp.concatenate([cache_band_k[i][:, ls_:], new(kb)], axis=1)[None]
            bv_s = jnp.concatenate([cache_band_v[i][:, ls_:], new(vb)], axis=1)[None]
            wo = w_out_even[i].astype(BF16)
            os_, ws = [oa, ob], [wo[:HA * DVA], wo[HA * DVA:]]
        else:
            w = _after(w_in_odd[i], tok)
            w_out_l = _after(w_out_odd[i], tok)
            cache_k_l, cache_v_l = _after(cache_swa_k[i], tok), _after(cache_swa_v[i], tok)
            wk, wv = w[:, 1024:1152], w[:, 1152:1280]
            dup = lambda a: jnp.concatenate([a[:, :64], a[:, :64], a[:, 64:], a[:, 64:]], axis=1)
            w_all = jnp.concatenate([w[:, :1024], dup(wk), dup(wv)], axis=1).astype(BF16)
            cos, sin, rope_map = _rope_tables(lp, ls_, bp, bs)
            x, q, k, v = _inproj_odd_call(xn, z, mods_g[l - 1], mods_g[l], norm_mix[l][None], cos, sin, rope_map, w_all)
            xres = [x]
            pb = WINDOW
            tq, g = 512, 2
            sink = sinks_c[i][None] * LOG2E
            sink_spec = [pl.BlockSpec(memory_space=pltpu.SMEM)]
            dupc = lambda c: jnp.concatenate([c[:, :, 0], c[:, :, 0], c[:, :, 1], c[:, :, 1]], axis=-1).reshape(bs * pb, 256).astype(BF16)
            ck, cv = dupc(cache_k_l), dupc(cache_v_l)
            additive = lambda valid: jnp.asarray(np.where(valid, 0.0, -np.inf), F32)
            masks = (additive(_band_valid(g, pb)), additive(_band_valid(g, pb, tq // (CHUNK * g))),
                     additive(_band_valid(1, pb)))
            o = _attention(_swa_kernel, q, k, v, ck, cv, masks, [sink], sink_spec, width=1024, kv_width=256, pb=pb,
                           tq=tq, g=g, bp=bp, lp=lp, bs=bs, name="swa")
            undup = lambda a: jnp.concatenate([a[:, 0:64], a[:, 128:192]], axis=1).astype(F32)
            tail = lambda a: jnp.stack([undup(a[(b + 1) * lp - pb:(b + 1) * lp]) for b in range(bp)]).reshape(1, bp, pb, KVC, DHC)
            new = lambda a: undup(a[n_p:]).reshape(bs, ls_, KVC, DHC)
            sk_p, sv_p = tail(k), tail(v)
            sk_s = jnp.concatenate([cache_swa_k[i][:, ls_:], new(k)], axis=1)[None]
            sv_s = jnp.concatenate([cache_swa_v[i][:, ls_:], new(v)], axis=1)[None]
            os_, ws = [o], [w_out_l.astype(BF16)]
        xn, disp, meta, counts = _outproj_call(xres, os_, ws, mods_g[l], norm_ffn[l][None], wr, br, n_pad)
        z, tok = _moe_layer(disp, meta, counts, w_gate_up, w_down, l, n, n_pad, sort_rows)

    y_prompt, y_sample = _final_call(xn, z, mods_g[DEPTH - 1], norm_final[None], n_p)
    return (y_prompt.reshape(bp, lp, d), y_sample.reshape(bs, ls_, d),
            gla_p, gla_s, bk_p, bv_p, bk_s, bv_s, sk_p, sv_p, sk_s, sv_s)
```

```python
import functools

import numpy as np
import jax
import jax.numpy as jnp
from jax import lax
from jax.experimental import pallas as pl
from jax.experimental.pallas import tpu as pltpu
from jax.experimental.pallas import tpu_sc as plsc

F32 = jnp.float32
BF16 = jnp.bfloat16

D_MODEL = 1024
DEPTH = 2
CHUNK = 64
PAST_LEN = 4096
HA, DKA, DVA = 4, 64, 128
GATE_RANK = 16
GATE_TAU = 16.0
HB, DHB = 8, 64
N_PREV_B = 8
MAX_REL = 128
HC, KVC, DHC = 16, 2, 64
WINDOW = 128
ROPE_THETA = 10000.0
N_EXPERTS = 16
N_GROUPS = 4
EXP_PER_GROUP = 4
D_FF = 512
EPS = 1e-6

N_PAIRS = 6
N_BUCKETS = N_GROUPS * N_PAIRS
BUCKET_ROWS = 32
Y_SLABS = 4
DISP_SLABS = Y_SLABS + 1
TMO = 256
MOE_TILES = 2
SC_WINDOW = 128
SC_WORKERS = 32
SC_GROUP = 3

TM = 512
SEQ_ROWS = 16
SUB = 16
LOG2E = 1.4426950408889634


def _cparams(sem, vmem_mb=48):
    return pltpu.CompilerParams(dimension_semantics=sem, vmem_limit_bytes=vmem_mb * 1024 * 1024)


def _dot(a, b):
    return jnp.dot(a, b, preferred_element_type=F32)


def _dot_nt(a, b):
    return lax.dot_general(a, b, (((1,), (1,)), ((), ())), preferred_element_type=F32)


def _split(a):
    hi = a.astype(BF16)
    lo = (a - hi.astype(F32)).astype(BF16)
    return hi, lo


def _dot3(a, b):
    ah, al = _split(a)
    bh, bl = _split(b)
    return _dot(ah, bh) + _dot(ah, bl) + _dot(al, bh)


def _dot3_narrow(a, b):
    ah, al = _split(a)
    bh, bl = _split(b)
    n = b.shape[1]
    p = _dot(ah, jnp.concatenate([bh, bl], axis=1))
    return p[:, :n] + p[:, n:] + _dot(al, bh)


def _sigmoid(x):
    return 1.0 / (1.0 + jnp.exp(-x))


def _group_affine(y, mul, add):
    parts = []
    for gi in range(y.shape[0] // CHUNK):
        p = y[gi * CHUNK:(gi + 1) * CHUNK]
        if mul is not None:
            p = p * mul[gi:gi + 1]
        if add is not None:
            p = p + add[gi:gi + 1]
        parts.append(p)
    return jnp.concatenate(parts, axis=0)


def _norm_mod(x, g, shift, scale):
    ms = jnp.mean(x * x, axis=-1, keepdims=True)
    return _group_affine(x * lax.rsqrt(ms + EPS) * g, 1.0 + scale, shift)


def _mod_spec(part):
    return pl.BlockSpec((TM // CHUNK, D_MODEL), lambda i: (i, part))


SHIFT_MIX, SCALE_MIX, GATE_MIX, SHIFT_FFN, SCALE_FFN, GATE_FFN = range(6)


def _on_token_tile(xp_ref, xs_ref, n_prompt_tiles, body):
    @pl.when(pl.program_id(0) < n_prompt_tiles)
    def _():
        body(xp_ref)

    @pl.when(pl.program_id(0) >= n_prompt_tiles)
    def _():
        body(xs_ref)


def _token_specs(n_prompt_tiles, d):
    return [pl.BlockSpec((TM, d), lambda i: (jnp.minimum(i, n_prompt_tiles - 1), 0)),
            pl.BlockSpec((TM, d), lambda i: (0, 0))]


def _ada_kernel(c_ref, w_ref, b_ref, o_ref):
    c = c_ref[...]
    o_ref[0] = _dot3(c * _sigmoid(c), w_ref[0]) + b_ref[0]


def _ada_call(c16, w_ada, b_ada):
    d = D_MODEL
    tn = 1024
    return pl.pallas_call(
        _ada_kernel,
        grid=(DEPTH, 6 * d // tn),
        in_specs=[pl.BlockSpec((SEQ_ROWS, d), lambda l, j: (0, 0)),
                  pl.BlockSpec((1, d, tn), lambda l, j: (l, 0, j)),
                  pl.BlockSpec((1, 1, tn), lambda l, j: (l, 0, j))],
        out_specs=pl.BlockSpec((1, SEQ_ROWS, tn), lambda l, j: (l, 0, j)),
        out_shape=jax.ShapeDtypeStruct((DEPTH, SEQ_ROWS, 6 * d), F32),
        compiler_params=_cparams(("arbitrary", "arbitrary")),
        name="ada",
    )(c16, w_ada, b_ada.reshape(DEPTH, 1, 6 * d))


def _inproj_even_kernel(xp_ref, xs_ref, sh_ref, sc_ref, g_ref, w_ref, wla_ref, wg_ref, bg_ref,
                        qa_ref, ka_ref, va_ref, ra_ref, qb_ref, kb_ref, vb_ref, ga_ref, *, n_prompt_tiles):
    def body(x_ref):
        t = x_ref.shape[0]
        outs = ((qa_ref, 0, 256, DKA ** -0.5), (ka_ref, 256, 512, None), (va_ref, 512, 1024, None),
                (ra_ref, 1024, 1536, None), (qb_ref, 1536, 2048, DHB ** -0.5 * LOG2E), (kb_ref, 2048, 2560, None),
                (vb_ref, 2560, 3072, None))
        shift, scale_ = sh_ref[...], sc_ref[...]
        halves = [slice(0, t // 2), slice(t // 2, t)]
        grp = [slice(0, t // (2 * CHUNK)), slice(t // (2 * CHUNK), t // CHUNK)]
        hbs = [_norm_mod(x_ref[rs, :], g_ref[...], shift[gs], scale_[gs]).astype(BF16) for rs, gs in zip(halves, grp)]
        for rs, hb in zip(halves, hbs):
            zs = [_dot(hb, w_ref[:, lo:hi]) for _, lo, hi, _ in outs]
            la = _dot(hb, wla_ref[...])
            for z, (o_ref, _, _, scale) in zip(zs, outs):
                o_ref[rs, :] = (z if scale is None else z * scale).astype(BF16)
            gl = _dot3(la, wg_ref[...]) + bg_ref[...]
            ga_ref[rs, :] = -(jnp.maximum(-gl, 0.0) + jnp.log(1.0 + jnp.exp(-jnp.abs(gl)))) * (1.0 / GATE_TAU)

    _on_token_tile(xp_ref, xs_ref, n_prompt_tiles, body)


def _inproj_even_call(xp, xs, mods, g, w_main, w_la, w_gate, b_gate):
    d = xp.shape[1]
    npt = xp.shape[0] // TM
    n = xp.shape[0] + xs.shape[0]
    row = lambda i: (i, 0)
    const = lambda i: (0, 0)
    widths = (256, 256, 512, 512, 512, 512, 512)
    out_shape = [jax.ShapeDtypeStruct((n, w), BF16) for w in widths] + [jax.ShapeDtypeStruct((n, 256), F32)]
    out_specs = [pl.BlockSpec((TM, w), row) for w in widths] + [pl.BlockSpec((TM, 256), row)]
    return pl.pallas_call(
        functools.partial(_inproj_even_kernel, n_prompt_tiles=npt),
        grid=(n // TM,),
        in_specs=_token_specs(npt, d) + [
            _mod_spec(SHIFT_MIX), _mod_spec(SCALE_MIX),
            pl.BlockSpec((1, d), const),
            pl.BlockSpec(w_main.shape, const), pl.BlockSpec(w_la.shape, const),
            pl.BlockSpec(w_gate.shape, const), pl.BlockSpec(b_gate.shape, const)],
        out_specs=out_specs, out_shape=out_shape,
        compiler_params=_cparams(("parallel",)),
        name="inproj_even",
    )(xp, xs, mods, mods, g, w_main, w_la, w_gate, b_gate)


def _rope(x, cos, sin_signed):
    t, w = x.shape
    lane = lax.broadcasted_iota(jnp.int32, (1, w), 1)
    first_half = (lane & 63) < 32
    rot = jnp.where(first_half, pltpu.roll(x, w - 32, 1), pltpu.roll(x, 32, 1))
    reps = w // 128
    return x * jnp.tile(cos, (1, reps)) + rot * jnp.tile(sin_signed, (1, reps))


def _unpack_pairs(slabs, dtype):
    lo = [pltpu.bitcast(s << 16, F32) for s in slabs]
    hi = [pltpu.bitcast(s & jnp.int32(-65536), F32) for s in slabs]
    return jnp.concatenate(lo + hi, axis=1).astype(dtype)


def _pack_pairs(x):
    bits = pltpu.bitcast(x.astype(BF16).astype(F32), jnp.int32)
    half = x.shape[1] // 2
    packed = ((bits[:, :half] >> 16) & jnp.int32(0xFFFF)) | (bits[:, half:] & jnp.int32(-65536))
    return [packed[:, 128 * s:128 * (s + 1)] for s in range(half // 128)]


def _add_moe(xn_ref, z_ref, gate_ref):
    y = _unpack_pairs([z_ref[s] for s in range(z_ref.shape[0])], F32)
    return xn_ref[...] + _group_affine(y, gate_ref[...], None)


def _rope_tables(lp, ls_, bp, bs):
    assert PAST_LEN + ls_ <= lp and lp % 128 == 0 and bs * ls_ == TM
    half = DHC // 2
    inv = ROPE_THETA ** (-jnp.arange(half, dtype=F32) / half)
    inv = jnp.tile(inv, 128 // half)
    sign = jnp.asarray(np.tile(np.repeat([-1.0, 1.0], half), 128 // DHC), F32)
    a = jnp.asarray(np.arange(lp // 128) * 128, F32)[:, None] * inv[None, :]
    b = jnp.asarray(np.arange(128), F32)[:, None] * inv[None, :]
    ca, sa, cb, sb = jnp.cos(a)[:, None], jnp.sin(a)[:, None], jnp.cos(b)[None], jnp.sin(b)[None]
    cos = (ca * cb - sa * sb).reshape(lp, 128)
    sin = ((sa * cb + ca * sb) * sign).reshape(lp, 128)
    with_sample = lambda t: jnp.concatenate([t, jnp.tile(t[PAST_LEN:PAST_LEN + ls_], (bs, 1))], axis=0)
    tiles = lp // TM
    return with_sample(cos), with_sample(sin), lambda i: (jnp.where(i < bp * tiles, i % tiles, tiles), 0)


def _inproj_odd_kernel(xn_ref, z_ref, gate_ref, sh_ref, sc_ref, g_ref, cos_ref, sin_ref, w_ref,
                       x_ref, q_ref, k_ref, v_ref):
    x = _add_moe(xn_ref, z_ref, gate_ref)
    x_ref[...] = x
    hb = _norm_mod(x, g_ref[...], sh_ref[...], sc_ref[...]).astype(BF16)
    cos, sin = cos_ref[...], sin_ref[...]
    q = _rope(_dot(hb, w_ref[:, 0:1024]), cos, sin)
    q_ref[...] = (q * (DHC ** -0.5 * LOG2E)).astype(BF16)
    k_ref[...] = _rope(_dot(hb, w_ref[:, 1024:1280]), cos, sin).astype(BF16)
    v_ref[...] = _dot(hb, w_ref[:, 1280:1536]).astype(BF16)


def _inproj_odd_call(xn, z, mods_prev, mods, g, cos, sin, rope_map, w):
    n, d = xn.shape
    row = lambda i: (i, 0)
    const = lambda i: (0, 0)
    widths = (1024, 256, 256)
    return pl.pallas_call(
        _inproj_odd_kernel,
        grid=(n // TM,),
        in_specs=[pl.BlockSpec((TM, d), row), pl.BlockSpec((z.shape[0], TM, 128), lambda i: (0, i, 0)),
                  _mod_spec(GATE_FFN), _mod_spec(SHIFT_MIX), _mod_spec(SCALE_MIX),
                  pl.BlockSpec((1, d), const),
                  pl.BlockSpec((TM, 128), rope_map), pl.BlockSpec((TM, 128), rope_map),
                  pl.BlockSpec(w.shape, const)],
        out_specs=[pl.BlockSpec((TM, d), row)] + [pl.BlockSpec((TM, wd), row) for wd in widths],
        out_shape=[jax.ShapeDtypeStruct((n, d), F32)] + [jax.ShapeDtypeStruct((n, wd), BF16) for wd in widths],
        compiler_params=_cparams(("parallel",)),
        name="inproj_odd",
    )(xn, z, mods_prev, mods, mods, g, cos, sin, w)


def _gla_tri():
    t = np.arange(CHUNK)[:, None]
    s = np.arange(CHUNK)[None, :]
    cum = s <= t
    start = s < (t // SUB) * SUB
    end = s < (t // SUB + 1) * SUB
    return jnp.asarray(np.concatenate([cum, start, end], axis=0).astype(np.float32), dtype=BF16)


def _gla_kernel(q_ref, k_ref, v_ref, g_ref, r_ref, s0_ref, gn_ref, tri_ref, o_ref, sout_ref, s_ref, *, nb):
    c_ = CHUNK
    nsub = c_ // SUB

    @pl.when(pl.program_id(1) == 0)
    def _():
        s_ref[...] = s0_ref[0]

    tri = tri_ref[...]
    lane = lax.broadcasted_iota(jnp.int32, (1, 128), 1)
    hmask = [jnp.where(lane < DKA, 1.0, 0.0), jnp.where(lane >= DKA, 1.0, 0.0)]
    ti = lax.broadcasted_iota(jnp.int32, (c_, c_), 0)
    si = lax.broadcasted_iota(jnp.int32, (c_, c_), 1)
    rb, cb = ti >> 4, si >> 4
    m_diag = (rb == cb) & (si <= ti)
    m_off = [(cb == j) & (rb > j) for j in range(nsub - 1)]
    hk = HA * DKA
    eye = lax.broadcasted_iota(jnp.int32, (hk, hk), 0) == lax.broadcasted_iota(jnp.int32, (hk, hk), 1)
    gn = gn_ref[...]

    chunks = range(nb)
    heads = [(p, hh) for p in range(HA // 2) for hh in range(2)]
    rows = [slice(c * c_, (c + 1) * c_) for c in chunks]
    pair = [slice(128 * p, 128 * (p + 1)) for p in range(HA // 2)]
    css = []
    for c in chunks:
        g_hi, g_lo = _split(g_ref[rows[c], :])
        css.append(_dot(tri, g_hi) + _dot(tri, g_lo))
    lhs1, lhs2, kds, kes, q_inter, klts, dcols = [], [], [], [], [], [], []
    for c in chunks:
        b, rs, re = css[c][0:c_], css[c][c_:2 * c_], css[c][2 * c_:3 * c_]
        q = q_ref[rows[c], :].astype(F32)
        k = k_ref[rows[c], :].astype(F32)
        bl = b[c_ - 1:c_, :]
        qd = q * jnp.exp(b - rs)
        kd = k * jnp.exp(rs - b)
        ke = k * jnp.exp(re - b)
        qi = q * jnp.exp(b)
        kl = k * jnp.exp(bl - b)
        ql = [q * jnp.exp(jnp.minimum(b - b[SUB * (j + 1) - 1:SUB * (j + 1), :], 0.0)) for j in range(nsub - 1)]
        dcols.append(jnp.sum(jnp.where(eye, jnp.broadcast_to(jnp.exp(bl), (hk, hk)), 0.0), axis=1, keepdims=True))
        kds.append([kd[:, ls].astype(BF16) for ls in pair])
        kes.append([ke[:, ls].astype(BF16) for ls in pair])
        klts.append([kl[:, ls].T.astype(BF16) for ls in pair])
        lhs1.append([(qd[:, pair[p]] * hmask[hh]).astype(BF16) for p, hh in heads])
        lhs2.append([jnp.concatenate([ql[j][:, pair[p]] * hmask[hh] for j in range(nsub - 1)], axis=0).astype(BF16)
                     for p, hh in heads])
        q_inter.append([(qi[:, pair[p]] * hmask[hh]).astype(BF16) for p, hh in heads])
    a1s = [[_dot_nt(lhs1[c][h], kds[c][p]) for h, (p, hh) in enumerate(heads)] for c in chunks]
    a2s = [[_dot_nt(lhs2[c][h], kes[c][p]) for h, (p, hh) in enumerate(heads)] for c in chunks]
    atts = []
    for c in chunks:
        per_head = []
        for h in range(HA):
            att = jnp.zeros((c_, c_), F32)
            for j in reversed(range(nsub - 1)):
                att = jnp.where(m_off[j], a2s[c][h][j * c_:(j + 1) * c_], att)
            per_head.append(jnp.where(m_diag, a1s[c][h], att).astype(BF16))
        atts.append(per_head)
    vs_ = [[v_ref[rows[c], DVA * h:DVA * (h + 1)] for h in range(HA)] for c in chunks]
    o_intra = [[_dot(atts[c][h], vs_[c][h]) for h in range(HA)] for c in chunks]
    upds = [jnp.concatenate([_dot(klts[c][p][DKA * hh:DKA * (hh + 1)], vs_[c][2 * p + hh]) for p, hh in heads], axis=0)
            for c in chunks]

    s_cur = s_ref[...]
    s_in = []
    for c in chunks:
        s_in.append(s_cur.astype(BF16))
        s_cur = dcols[c] * s_cur + upds[c]
    s_ref[...] = s_cur
    sout_ref[0] = s_cur

    for c in chunks:
        for h in range(HA):
            o = o_intra[c][h] + _dot(q_inter[c][h], s_in[c][pair[h // 2], :])
            ms = jnp.mean(o * o, axis=-1, keepdims=True)
            vs = slice(DVA * h, DVA * (h + 1))
            rr = r_ref[rows[c], vs].astype(F32)
            o_ref[rows[c], vs] = (o * lax.rsqrt(ms + EPS) * gn * (rr * _sigmoid(rr))).astype(BF16)


def _gla_call(q, k, v, g, r, s0, gn, o_prev, *, n_seq, seq_rows, row0, nb):
    tq = nb * CHUNK
    steps = seq_rows // tq
    blk0 = row0 // tq
    row = lambda b, j: (blk0 + b * steps + j, 0)
    const = lambda b, j: (0, 0)
    tri = _gla_tri()
    in_specs = [pl.BlockSpec((tq, 256), row), pl.BlockSpec((tq, 256), row), pl.BlockSpec((tq, 512), row),
                pl.BlockSpec((tq, 256), row), pl.BlockSpec((tq, 512), row),
                pl.BlockSpec((1, 256, 128), lambda b, j: (b, 0, 0)),
                pl.BlockSpec((1, 128), const), pl.BlockSpec(tri.shape, const)]
    args = [q, k, v, g, r, s0, gn, tri]
    aliases = {}
    if o_prev is not None:
        in_specs.append(pl.BlockSpec(memory_space=pl.ANY))
        args.append(o_prev)
        aliases = {len(args) - 1: 0}
    kern = functools.partial(_gla_kernel, nb=nb)
    if o_prev is not None:
        kern = _drop_arg(kern, 8)
    return pl.pallas_call(
        kern,
        grid=(n_seq, steps),
        in_specs=in_specs,
        out_specs=[pl.BlockSpec((tq, 512), row), pl.BlockSpec((1, 256, 128), lambda b, j: (b, 0, 0))],
        out_shape=[jax.ShapeDtypeStruct((q.shape[0], 512), BF16), jax.ShapeDtypeStruct((n_seq, 256, 128), F32)],
        scratch_shapes=[pltpu.VMEM((256, 128), F32)],
        input_output_aliases=aliases,
        compiler_params=_cparams(("arbitrary", "arbitrary")),
        name="gla",
    )(*args)


def _drop_arg(fn, idx):
    def wrapped(*refs):
        return fn(*refs[:idx], *refs[idx + 1:])
    return wrapped


def _window(prev_ref, cur_ref, lo, hi, pb, ls):
    if lo < pb:
        return jnp.concatenate([prev_ref[lo:pb, ls], cur_ref[0:hi - pb, ls]], axis=0)
    return cur_ref[lo - pb:hi - pb, ls]


def _band_kernel(q_ref, kp_ref, kc_ref, vp_ref, vc_ref, bias_ref, o_ref, *, g, n_sub, pb):
    qs = CHUNK * g
    kw_rows = pb + qs
    lane = lax.broadcasted_iota(jnp.int32, (1, 128), 1)
    low = lane < DHB
    hmask = [jnp.where(low, 1.0, 0.0), jnp.where(low, 0.0, 1.0)]
    for s in range(n_sub):
        sb = s if bias_ref.shape[0] > 1 else 0
        rows = slice(qs * s, qs * (s + 1))
        lanes = [slice(128 * p, 128 * (p + 1)) for p in range(HB // 2)]
        heads = [(p, hh) for p in range(HB // 2) for hh in range(2)]
        qps = [q_ref[rows, ls].astype(F32) for ls in lanes]
        kws = [_window(kp_ref, kc_ref, qs * s, qs * s + kw_rows, pb, ls) for ls in lanes]
        vws = [_window(vp_ref, vc_ref, qs * s, qs * s + kw_rows, pb, ls) for ls in lanes]
        scs = [_dot_nt((qps[p] * hmask[hh]).astype(BF16), kws[p]) + bias_ref[sb, 2 * p + hh] for p, hh in heads]
        pes = [jnp.exp2(sc - jnp.max(sc, axis=-1, keepdims=True)) for sc in scs]
        outs = [_dot(pe.astype(BF16), vws[p]) / jnp.sum(pe, axis=-1, keepdims=True) for pe, (p, hh) in zip(pes, heads)]
        for p, ls in enumerate(lanes):
            o_ref[rows, ls] = jnp.where(low, outs[2 * p], outs[2 * p + 1]).astype(BF16)


def _band_valid(g, pb, n_sub=None):
    rows, kw = CHUNK * g, pb + CHUNK * g
    r = np.arange(rows)[:, None]
    c = np.arange(kw)[None, :]
    dd = c // CHUNK - r // CHUNK
    band = (dd >= 0) & (dd <= pb // CHUNK)
    if n_sub is None:
        return band[None]
    return np.stack([band & (c >= pb - rows * s) for s in range(n_sub)])


def _band_bias(table, g, pb, valid):
    rows, kw = CHUNK * g, pb + CHUNK * g
    period = kw + rows
    m = np.arange(period)
    m = np.where(m < kw, m, m - period)
    ext = table[:, np.clip(m - pb, -MAX_REL, MAX_REL) + MAX_REL] * LOG2E
    flat = jnp.tile(ext, (1, rows))[:, :rows * (period - 1)]
    bias = flat.reshape(table.shape[0], rows, period - 1)[:, :, :kw]
    return jnp.where(valid[:, None], bias[None], -jnp.inf)


def _attn_call(kernel, q, kp, kc, vp, vc, extra, extra_specs, o_prev, *, width, kv_width, tq, pb,
               n_blocks, blk_map, prev_map, name):
    row = lambda i: (blk_map(i), 0)
    prev = lambda i: (prev_map(i), 0)
    in_specs = [pl.BlockSpec((tq, width), row),
                pl.BlockSpec((pb, kv_width), prev), pl.BlockSpec((tq, kv_width), row),
                pl.BlockSpec((pb, kv_width), prev), pl.BlockSpec((tq, kv_width), row)] + extra_specs
    args = [q, kp, kc, vp, vc] + extra
    aliases = {}
    if o_prev is not None:
        in_specs.append(pl.BlockSpec(memory_space=pl.ANY))
        args.append(o_prev)
        aliases = {len(args) - 1: 0}
        kernel = _drop_arg(kernel, len(args) - 1)
    return pl.pallas_call(
        kernel,
        grid=(n_blocks,),
        in_specs=in_specs,
        out_specs=pl.BlockSpec((tq, width), row),
        out_shape=jax.ShapeDtypeStruct((q.shape[0], width), BF16),
        input_output_aliases=aliases,
        compiler_params=_cparams(("parallel",)),
        name=name,
    )(*args)


def _attention(kernel_fn, q, k, v, cache_k, cache_v, masks, extra, extra_specs, *, width, kv_width, pb, tq, g,
               bp, lp, bs, name):
    bps = lp // tq
    n_sub = tq // (CHUNK * g)
    spec = lambda a: [pl.BlockSpec(a.shape, lambda i: (0,) * a.ndim)]
    kern = functools.partial(kernel_fn, g=g, n_sub=n_sub, pb=pb)
    common = dict(width=width, kv_width=kv_width, pb=pb)
    main = lambda i: (i // (bps - 1)) * bps + i % (bps - 1) + 1
    o = _attn_call(kern, q, k, k, v, v, [masks[0]] + extra, spec(masks[0]) + extra_specs, None, tq=tq,
                   n_blocks=bp * (bps - 1), blk_map=main, prev_map=lambda i: main(i) * (tq // pb) - 1,
                   name=name + "_main", **common)
    first = lambda i: i * bps
    o = _attn_call(kern, q, k, k, v, v, [masks[1]] + extra, spec(masks[1]) + extra_specs, o, tq=tq,
                   n_blocks=bp, blk_map=first, prev_map=lambda i: jnp.maximum(first(i) * (tq // pb) - 1, 0),
                   name=name + "_first", **common)
    samp = functools.partial(kernel_fn, g=1, n_sub=1, pb=pb)
    return _attn_call(samp, q, cache_k, k, cache_v, v, [masks[2]] + extra, spec(masks[2]) + extra_specs, o, tq=CHUNK,
                      n_blocks=bs, blk_map=lambda i: bp * lp // CHUNK + i, prev_map=lambda i: i,
                      name=name + "_sample", **common)


def _swa_kernel(q_ref, kp_ref, kc_ref, vp_ref, vc_ref, mask_ref, sink_ref, o_ref, *, g, n_sub, pb):
    qs = CHUNK * g
    kw_rows = pb + qs
    lane = lax.broadcasted_iota(jnp.int32, (1, 128), 1)
    low = lane < DHC
    hmask = [jnp.where(low, 1.0, 0.0), jnp.where(low, 0.0, 1.0)]
    pairs_per_kv = HC // KVC // 2
    for s in range(n_sub):
        msk = mask_ref[s if mask_ref.shape[0] > 1 else 0]
        rows = slice(qs * s, qs * (s + 1))
        kws = [_window(kp_ref, kc_ref, qs * s, qs * s + kw_rows, pb, slice(128 * kv, 128 * (kv + 1))) for kv in range(KVC)]
        vws = [_window(vp_ref, vc_ref, qs * s, qs * s + kw_rows, pb, slice(128 * kv, 128 * (kv + 1))) for kv in range(KVC)]
        heads = [(j, hh) for j in range(HC // 2) for hh in range(2)]
        qps = [q_ref[rows, 128 * j:128 * (j + 1)].astype(F32) for j in range(HC // 2)]
        scs = [_dot_nt((qps[j] * hmask[hh]).astype(BF16), kws[j // pairs_per_kv]) + msk for j, hh in heads]
        sks = [sink_ref[0, 2 * j + hh] for j, hh in heads]
        ms = [jnp.maximum(jnp.max(sc, axis=-1, keepdims=True), sk) for sc, sk in zip(scs, sks)]
        pes = [jnp.exp2(sc - m) for sc, m in zip(scs, ms)]
        outs = [_dot(pe.astype(BF16), vws[j // pairs_per_kv]) / (jnp.sum(pe, axis=-1, keepdims=True) + jnp.exp2(sk - m))
                for pe, sk, m, (j, hh) in zip(pes, sks, ms, heads)]
        for j in range(HC // 2):
            o_ref[rows, 128 * j:128 * (j + 1)] = jnp.where(low, outs[2 * j], outs[2 * j + 1]).astype(BF16)


def _route(logits_t):
    a = [logits_t[4 * j:4 * j + 4] for j in range(EXP_PER_GROUP)]

    def first_argmax(vals, m):
        idx = jnp.full(m.shape, float(len(vals) - 1), F32)
        for j in reversed(range(len(vals) - 1)):
            idx = jnp.where(vals[j] == m, float(j), idx)
        return idx

    m1 = functools.reduce(jnp.maximum, a)
    i1 = first_argmax(a, m1)
    bsec = [jnp.where(i1 == float(j), -jnp.inf, a[j]) for j in range(EXP_PER_GROUP)]
    m2 = functools.reduce(jnp.maximum, bsec)
    i2 = first_argmax(bsec, m2)
    rows = lambda x: [x[gi:gi + 1] for gi in range(N_GROUPS)]
    gm = functools.reduce(jnp.maximum, rows(m1))
    gscore = jnp.exp(m1 - gm) + jnp.exp(m2 - gm)
    gs = rows(gscore)
    gsel = first_argmax(gs, functools.reduce(jnp.maximum, gs))

    def pick(x):
        xr = rows(x)
        out = xr[N_GROUPS - 1]
        for gi in reversed(range(N_GROUPS - 1)):
            out = jnp.where(gsel == float(gi), xr[gi], out)
        return out

    p1 = jnp.exp(pick(m1) - gm)
    p2 = jnp.exp(pick(m2) - gm)
    w1 = p1 / (p1 + p2)
    w2 = p2 / (p1 + p2)
    s1, s2 = pick(i1), pick(i2)
    lo, hi = jnp.minimum(s1, s2), jnp.maximum(s1, s2)
    pair = jnp.where(lo == 0.0, hi - 1.0, jnp.where(lo == 1.0, hi + 1.0, 5.0))
    bucket = gsel * float(N_PAIRS) + pair
    first_is_lo = s1 < s2
    return bucket, jnp.where(first_is_lo, w1, w2), jnp.where(first_is_lo, w2, w1)


def _outproj_kernel(*refs, n_x, n_o, n_prompt_tiles):
    x_refs = refs[:n_x]
    o_refs = refs[n_x:n_x + n_o]
    w_refs = refs[n_x + n_o:n_x + 2 * n_o]
    (gate_ref, nf_ref, sh_ref, sc_ref, wr_ref, br_ref, tri_ref,
     xn_ref, disp_ref, meta_ref, cnt_ref, run_ref) = refs[n_x + 2 * n_o:]
    t = xn_ref.shape[0]

    @pl.when(pl.program_id(0) == 0)
    def _():
        run_ref[...] = jnp.zeros_like(run_ref)

    x_src = x_refs[0]
    if n_x == 2:
        def stage(x_ref):
            xn_ref[...] = x_ref[...]

        _on_token_tile(x_refs[0], x_refs[1], n_prompt_tiles, stage)
        x_src = xn_ref

    halves = [slice(0, t // 2), slice(t // 2, t)]
    grp = [slice(0, t // (2 * CHUNK)), slice(t // (2 * CHUNK), t // CHUNK)]
    ys = []
    for rs in halves:
        y = _dot(o_refs[0][rs, :], w_refs[0][...])
        for i in range(1, n_o):
            y = y + _dot(o_refs[i][rs, :], w_refs[i][...])
        ys.append(y)
    gate, shift, scale = gate_ref[...], sh_ref[...], sc_ref[...]
    gys = [_group_affine(y, gate[gs], None) for y, gs in zip(ys, grp)]

    for rs, gy in zip(halves, gys):
        xn_ref[rs, :] = x_src[rs, :] + gy
    hs = [_norm_mod(xn_ref[rs, :], nf_ref[...], shift[gs], scale[gs]) for rs, gs in zip(halves, grp)]
    for rs, h in zip(halves, hs):
        for s, slab in enumerate(_pack_pairs(h)):
            disp_ref[s, rs, :] = slab
    logits_t = [(_dot3_narrow(h, wr_ref[...]) + br_ref[...]).T[0:N_EXPERTS] for h in hs]
    bucket, w_lo, w_hi = _route(jnp.concatenate(logits_t, axis=1))
    r128 = lax.broadcasted_iota(jnp.int32, (128, t), 0)
    tok = (pl.program_id(0) * t + lax.broadcasted_iota(jnp.int32, (1, t), 1)).astype(F32)
    aux = jnp.where(r128 == 0, w_lo, jnp.where(r128 == 1, w_hi, jnp.where(r128 == 2, tok, 0.0))).T
    disp_ref[disp_ref.shape[0] - 1] = pltpu.bitcast(aux, jnp.int32)
    brow = lax.broadcasted_iota(jnp.int32, (BUCKET_ROWS, t), 0).astype(F32)
    onehot = jnp.where(brow == bucket, 1.0, 0.0)
    before = _dot(onehot.astype(BF16), tri_ref[...]) + run_ref[:, 0:1]
    rank = jnp.sum(onehot * before, axis=0, keepdims=True)
    run_ref[...] = run_ref[...] + jnp.sum(onehot, axis=1, keepdims=True)
    cnt_ref[...] = run_ref[...]
    r8 = lax.broadcasted_iota(jnp.int32, (8, t), 0)
    meta_ref[...] = jnp.where(r8 == 0, bucket, jnp.where(r8 == 1, rank, 0.0)).astype(jnp.int32)


def _outproj_call(xs_, os_, ws, mods, nf, wr, br, n_pad):
    d = xs_[0].shape[1]
    n = sum(a.shape[0] for a in xs_)
    npt = xs_[0].shape[0] // TM
    row = lambda i: (i, 0)
    const = lambda i: (0, 0)
    n_o = len(os_)
    in_specs = ((_token_specs(npt, d) if len(xs_) == 2 else [pl.BlockSpec((TM, d), row)])
                + [pl.BlockSpec((TM, o.shape[1]), row) for o in os_]
                + [pl.BlockSpec(w.shape, const) for w in ws]
                + [_mod_spec(GATE_MIX), pl.BlockSpec((1, d), const),
                   _mod_spec(SHIFT_FFN), _mod_spec(SCALE_FFN),
                   pl.BlockSpec(wr.shape, const), pl.BlockSpec(br.shape, const),
                   pl.BlockSpec((TM, TM), const)])
    tri = jnp.asarray(np.triu(np.ones((TM, TM), np.float32), k=1), dtype=BF16)
    return pl.pallas_call(
        functools.partial(_outproj_kernel, n_x=len(xs_), n_o=n_o, n_prompt_tiles=npt),
        grid=(n // TM,),
        in_specs=in_specs,
        out_specs=[pl.BlockSpec((TM, d), row), pl.BlockSpec((DISP_SLABS, TM, 128), lambda i: (0, i, 0)),
                   pl.BlockSpec((8, TM), lambda i: (0, i)), pl.BlockSpec((BUCKET_ROWS, 128), const)],
        out_shape=[jax.ShapeDtypeStruct((n, d), F32), jax.ShapeDtypeStruct((DISP_SLABS, n_pad, 128), jnp.int32),
                   jax.ShapeDtypeStruct((8, n), jnp.int32), jax.ShapeDtypeStruct((BUCKET_ROWS, 128), F32)],
        scratch_shapes=[pltpu.VMEM((BUCKET_ROWS, 128), F32)],
        compiler_params=_cparams(("arbitrary",)),
        name="outproj_router",
    )(*xs_, *os_, *ws, mods, nf, mods, mods, wr, br, tri)


def _sc_mesh():
    return plsc.VectorSubcoreMesh(core_axis_name="core", subcore_axis_name="subcore")


def _sc_row_copy(src, idx, n_out, scatter):
    r = idx.shape[0]
    k = SC_GROUP
    w_per = r // (SC_WINDOW * SC_WORKERS)
    assert r % (SC_WINDOW * SC_WORKERS) == 0 and w_per % k == 0
    n_groups = w_per // k

    @functools.partial(
        pl.kernel, out_type=jax.ShapeDtypeStruct((n_out, 128), src.dtype), mesh=_sc_mesh(),
        scratch_types=[pltpu.VMEM((w_per, SC_WINDOW), jnp.int32),
                       pltpu.VMEM((2 * k, SC_WINDOW, 128), src.dtype),
                       pltpu.SemaphoreType.DMA((2,)), pltpu.SemaphoreType.DMA((2,))])
    def copy(x_hbm, i_hbm, o_hbm, ibuf, xbuf, in_sem, out_sem):
        wid = lax.axis_index("core") * (SC_WORKERS // 2) + lax.axis_index("subcore")
        pltpu.sync_copy(i_hbm.at[wid], ibuf)
        first = wid * w_per

        def rows(j):
            return pl.ds((first + j) * SC_WINDOW, SC_WINDOW)

        def start_in(g, slot):
            cps = []
            for c in range(k):
                j = g * k + c
                s = x_hbm.at[rows(j)] if scatter else x_hbm.at[ibuf.at[j]]
                cps.append(pltpu.async_copy(s, xbuf.at[slot * k + c], in_sem.at[slot]))
            return cps

        def start_out(g, slot):
            cps = []
            for c in range(k):
                j = g * k + c
                dst = o_hbm.at[ibuf.at[j]] if scatter else o_hbm.at[rows(j)]
                cps.append(pltpu.async_copy(xbuf.at[slot * k + c], dst, out_sem.at[slot]))
            return cps

        pending_in = start_in(0, 0)
        for g in range(n_groups):
            slot = g % 2
            for cp in pending_in:
                cp.wait()
            pending_out = start_out(g, slot)
            if g + 1 < n_groups:
                pending_in = start_in(g + 1, 1 - slot)
            for cp in pending_out:
                cp.wait()

    return copy(src, idx.reshape(SC_WORKERS, w_per, SC_WINDOW))


def _sc_scatter_rows(src, idx, n_out):
    assert idx.shape == (src.shape[0],)
    return _sc_row_copy(src, idx, n_out, scatter=True)


def _moe_kernel(elo_ref, ehi_ref, nvalid_ref, xs_ref, *refs, n_tok, dump_tiles):
    w_refs, (y_ref, tok_ref) = refs[:4 * MOE_TILES], refs[4 * MOE_TILES:]
    step = pl.program_id(0)
    t = TMO
    tiles = range(MOE_TILES)
    rows = [slice(t * j, t * (j + 1)) for j in tiles]
    auxs = [pltpu.bitcast(xs_ref[Y_SLABS, rows[j], :], F32) for j in tiles]
    r = lax.broadcasted_iota(jnp.int32, (1, t), 1)
    for j in tiles:
        i = step * MOE_TILES + j
        spare = n_tok + (i % dump_tiles) * t + r
        tok = jnp.where(r < nvalid_ref[i], auxs[j].T[2:3, :].astype(jnp.int32), spare)
        for c in range(t // 128):
            tok_ref[j, c:c + 1, :] = tok[:, 128 * c:128 * (c + 1)]

    any_tokens = nvalid_ref[step * MOE_TILES] > 0
    for j in range(1, MOE_TILES):
        any_tokens = jnp.logical_or(any_tokens, nvalid_ref[step * MOE_TILES + j] > 0)

    @pl.when(any_tokens)
    def _():
        units = [(j, e) for j in tiles for e in range(2)]
        hs = [_unpack_pairs([xs_ref[s, rows[j], :] for s in range(Y_SLABS)], BF16) for j in tiles]
        abs_ = [_dot(hs[j], w_refs[4 * j + e][0, 0]) for j, e in units]
        acts = [(ab[:, :D_FF] * _sigmoid(ab[:, :D_FF]) * ab[:, D_FF:]).astype(BF16) for ab in abs_]
        ys = [_dot(act, w_refs[4 * j + 2 + e][0, 0]) for act, (j, e) in zip(acts, units)]
        for j in tiles:
            acc = auxs[j][:, 0:1] * ys[2 * j] + auxs[j][:, 1:2] * ys[2 * j + 1]
            for s, slab in enumerate(_pack_pairs(acc)):
                y_ref[s, rows[j], :] = slab

    @pl.when(jnp.logical_not(any_tokens))
    def _():
        y_ref[...] = jnp.zeros_like(y_ref)


def _moe_call(xs, elo, ehi, nvalid, wgu, wdn, layer, n_tiles, n_tok, dump_tiles):
    d = wgu.shape[2]
    m = MOE_TILES
    assert n_tiles % m == 0
    weight_specs = []
    for j in range(m):
        for shape in ((1, 1, d, 2 * D_FF), (1, 1, D_FF, d)):
            for sel in range(2):
                weight_specs.append(pl.BlockSpec(
                    shape, lambda i, lo, hi, v, j=j, sel=sel: (layer, (lo, hi)[sel][m * i + j], 0, 0)))
    weights = [w for _ in range(m) for w in (wgu, wgu, wdn, wdn)]
    return pl.pallas_call(
        functools.partial(_moe_kernel, n_tok=n_tok, dump_tiles=dump_tiles),
        grid_spec=pltpu.PrefetchScalarGridSpec(
            num_scalar_prefetch=3,
            grid=(n_tiles // m,),
            in_specs=[pl.BlockSpec((DISP_SLABS, m * TMO, 128), lambda i, lo, hi, v: (0, i, 0))] + weight_specs,
            out_specs=[pl.BlockSpec((Y_SLABS, m * TMO, 128), lambda i, lo, hi, v: (0, i, 0)),
                       pl.BlockSpec((m, TMO // 128, 128), lambda i, lo, hi, v: (i, 0, 0))]),
        out_shape=[jax.ShapeDtypeStruct((Y_SLABS, n_tiles * TMO, 128), jnp.int32),
                   jax.ShapeDtypeStruct((n_tiles, TMO // 128, 128), jnp.int32)],
        compiler_params=_cparams(("arbitrary",), vmem_mb=56),
        name="moe_grouped",
    )(elo, ehi, nvalid, xs, *weights)


def _after(x, token):
    return lax.optimization_barrier((x, token))[0]


def _cast_kernel(after_ref, w_ref, o_ref):
    o_ref[...] = w_ref[...].astype(o_ref.dtype)


def _cast_call(w, layer, after):
    _, e, k, n = w.shape
    return pl.pallas_call(
        _cast_kernel,
        grid=(e,),
        in_specs=[pl.BlockSpec(memory_space=pl.ANY), pl.BlockSpec((1, 1, k, n), lambda i: (layer, i, 0, 0))],
        out_specs=pl.BlockSpec((1, 1, k, n), lambda i: (0, i, 0, 0)),
        out_shape=jax.ShapeDtypeStruct((1, e, k, n), BF16),
        compiler_params=_cparams(("parallel",)),
        name="cast_weights",
    )(after, w)


def _moe_layer(disp, meta, counts, wgu, wdn, n, n_pad, sort_rows):
    n_tiles = sort_rows // TMO
    layer = 0
    cnt = counts[:N_BUCKETS, 0].astype(jnp.int32)
    padded = ((cnt + TMO - 1) // TMO) * TMO
    ends = jnp.cumsum(padded)
    offs = ends - padded
    bucket, rank = meta[0], meta[1]
    pos = rank + jnp.sum(jnp.where(bucket[None, :] == jnp.arange(N_BUCKETS, dtype=jnp.int32)[:, None],
                                   offs[:, None], 0), axis=0)
    tile_start = jnp.arange(n_tiles, dtype=jnp.int32) * TMO
    tile_bucket = jnp.minimum(jnp.sum((tile_start[:, None] >= ends[None, :]).astype(jnp.int32), axis=1), N_BUCKETS - 1)
    pair_lo = np.array([0, 0, 0, 1, 1, 2], np.int32)
    pair_hi = np.array([1, 2, 3, 2, 3, 3], np.int32)
    b_lo = jnp.asarray(np.repeat(np.arange(N_GROUPS), N_PAIRS) * EXP_PER_GROUP + np.tile(pair_lo, N_GROUPS), jnp.int32)
    b_hi = jnp.asarray(np.repeat(np.arange(N_GROUPS), N_PAIRS) * EXP_PER_GROUP + np.tile(pair_hi, N_GROUPS), jnp.int32)
    onehot_tb = (tile_bucket[:, None] == jnp.arange(N_BUCKETS, dtype=jnp.int32)[None, :]).astype(jnp.int32)
    elo = jnp.sum(onehot_tb * b_lo[None, :], axis=1)
    ehi = jnp.sum(onehot_tb * b_hi[None, :], axis=1)
    bucket_end = jnp.sum(onehot_tb * (offs + cnt)[None, :], axis=1)
    nvalid = jnp.where(tile_start < ends[-1], jnp.clip(bucket_end - tile_start, 0, TMO), 0)
    dump = sort_rows + jnp.arange(n_pad - n, dtype=jnp.int32)
    pos_sc = jnp.concatenate([pos, dump])
    total = sort_rows + n_pad - n
    sc_idx = (pos_sc[None, :] + (jnp.arange(DISP_SLABS, dtype=jnp.int32) * total)[:, None]).reshape(-1)
    xs = _sc_scatter_rows(disp.reshape(DISP_SLABS * n_pad, 128), sc_idx, DISP_SLABS * total)
    ys, tok = _moe_call(xs.reshape(DISP_SLABS, total, 128), elo, ehi, nvalid, wgu, wdn, layer, n_tiles,
                        n, (n_pad - n) // TMO)
    back_idx = (tok.reshape(1, sort_rows) + (jnp.arange(Y_SLABS, dtype=jnp.int32) * n_pad)[:, None]).reshape(-1)
    z = _sc_scatter_rows(ys.reshape(Y_SLABS * sort_rows, 128), back_idx, Y_SLABS * n_pad)
    return z.reshape(Y_SLABS, n_pad, 128), tok


def _final_kernel(xn_ref, z_ref, gate_ref, g_ref, yp_ref, ys_ref, *, n_prompt_tiles):
    x = _add_moe(xn_ref, z_ref, gate_ref)
    ms = jnp.mean(x * x, axis=-1, keepdims=True)
    y = x * lax.rsqrt(ms + EPS) * g_ref[...]
    i = pl.program_id(0)

    @pl.when(i < n_prompt_tiles)
    def _():
        yp_ref[...] = y

    @pl.when(i >= n_prompt_tiles)
    def _():
        ys_ref[...] = y


def _final_call(xn, z, mods, g, n_prompt):
    n, d = xn.shape
    npt = n_prompt // TM
    assert n - n_prompt == TM
    return pl.pallas_call(
        functools.partial(_final_kernel, n_prompt_tiles=npt),
        grid=(n // TM,),
        in_specs=[pl.BlockSpec((TM, d), lambda i: (i, 0)), pl.BlockSpec((z.shape[0], TM, 128), lambda i: (0, i, 0)),
                  _mod_spec(GATE_FFN), pl.BlockSpec((1, d), lambda i: (0, 0))],
        out_specs=_token_specs(npt, d),
        out_shape=[jax.ShapeDtypeStruct((n_prompt, d), F32), jax.ShapeDtypeStruct((TM, d), F32)],
        compiler_params=_cparams(("arbitrary",)),
        name="final_norm",
    )(xn, z, mods, g)


def kernel(x_prompt, x_sample, c_prompt, c_sample, state_gla, cache_band_k, cache_band_v, cache_swa_k, cache_swa_v,
           w_ada, b_ada, norm_mix, norm_ffn, norm_final, w_in_even, w_gate_a, b_gate_a, gla_norm, rel_bias_b,
           w_out_even, w_in_odd, sinks_c, w_out_odd, w_router, b_router, w_gate_up, w_down):
    bp, lp, d = x_prompt.shape
    bs, ls_, _ = x_sample.shape
    n_p, n_s = bp * lp, bs * ls_
    n = n_p + n_s
    assert ls_ == CHUNK and n_s == TM and lp % TM == 0 and PAST_LEN % CHUNK == 0

    xp2, xs2 = x_prompt.reshape(n_p, d), x_sample.reshape(n_s, d)

    c16 = jnp.zeros((SEQ_ROWS, d), F32).at[:bp].set(c_prompt).at[bp:bp + bs].set(c_sample)
    mods = _ada_call(c16, w_ada, b_ada)
    seq_of_group = np.concatenate([np.repeat(np.arange(bp), lp // CHUNK), bp + np.arange(bs)])
    mods_g = [mods[l][seq_of_group] for l in range(DEPTH)]

    perm = np.array([4 * (c % 4) + c // 4 for c in range(N_EXPERTS)])
    wr = jnp.zeros((d, 128), F32).at[:, :N_EXPERTS].set(w_router[:, perm])
    br = jnp.zeros((1, 128), F32).at[0, :N_EXPERTS].set(b_router[perm])

    sc_unit = SC_WINDOW * SC_WORKERS * SC_GROUP
    n_pad = n + TMO
    while (DISP_SLABS * n_pad) % sc_unit or (Y_SLABS * n_pad) % TMO or (n_pad - n) % TMO:
        n_pad += TMO
    sort_rows = n + N_BUCKETS * TMO
    while (Y_SLABS * sort_rows) % sc_unit or sort_rows % (MOE_TILES * TMO):
        sort_rows += TMO

    gla_p = gla_s = bk_p = bv_p = bk_s = bv_s = sk_p = sv_p = sk_s = sv_s = None
    xn = z = tok = None
    for l in range(DEPTH):
        i = l // 2
        if l % 2 == 0:
            w = w_in_even[i]
            w_main = jnp.concatenate([w[:, :1536], w[:, 1552:]], axis=1).astype(BF16)
            w_la = jnp.zeros((d, 128), F32).at[:, :GATE_RANK].set(w[:, 1536:1552]).astype(BF16)
            w_gate = jnp.zeros((128, HA * DKA), F32).at[:GATE_RANK].set(w_gate_a[i])
            qa, ka, va, ra, qb, kb, vb, ga = _inproj_even_call(
                xp2, xs2, mods_g[l], norm_mix[l][None], w_main, w_la, w_gate, b_gate_a[i][None])
            xres = [xp2, xs2]
            gn = gla_norm[i][None]
            oa, s_p = _gla_call(qa, ka, va, ga, ra, jnp.zeros((bp, 256, 128), F32), gn, None,
                                n_seq=bp, seq_rows=lp, row0=0, nb=8)
            oa, s_s = _gla_call(qa, ka, va, ga, ra, state_gla[i].reshape(bs, 256, 128), gn, oa,
                                n_seq=bs, seq_rows=ls_, row0=n_p, nb=1)
            gla_p, gla_s = s_p.reshape(1, bp, HA, DKA, DVA), s_s.reshape(1, bs, HA, DKA, DVA)
            pb = N_PREV_B * CHUNK
            tq, g = 512, 2
            ck = cache_band_k[i].reshape(bs * pb, HB * DHB).astype(BF16)
            cv = cache_band_v[i].reshape(bs * pb, HB * DHB).astype(BF16)
            biases = (_band_bias(rel_bias_b[i], g, pb, _band_valid(g, pb)),
                      _band_bias(rel_bias_b[i], g, pb, _band_valid(g, pb, tq // (CHUNK * g))),
                      _band_bias(rel_bias_b[i], 1, pb, _band_valid(1, pb)))
            ob = _attention(_band_kernel, qb, kb, vb, ck, cv, biases, [], [], width=512, kv_width=512, pb=pb,
                            tq=tq, g=g, bp=bp, lp=lp, bs=bs, name="band")
            tail = lambda a: jnp.stack([a[(b + 1) * lp - pb:(b + 1) * lp] for b in range(bp)]).astype(F32).reshape(1, bp, pb, HB, DHB)
            new = lambda a: a[n_p:].astype(F32).reshape(bs, ls_, HB, DHB)
            bk_p, bv_p = tail(kb), tail(vb)
            bk_s = jnp.concatenate([cache_band_k[i][:, ls_:], new(kb)], axis=1)[None]
            bv_s = jnp.concatenate([cache_band_v[i][:, ls_:], new(vb)], axis=1)[None]
            wo = w_out_even[i].astype(BF16)
            os_, ws = [oa, ob], [wo[:HA * DVA], wo[HA * DVA:]]
        else:
            w = _after(w_in_odd[i], tok)
            w_out_l = _after(w_out_odd[i], tok)
            cache_k_l, cache_v_l = _after(cache_swa_k[i], tok), _after(cache_swa_v[i], tok)
            wk, wv = w[:, 1024:1152], w[:, 1152:1280]
            dup = lambda a: jnp.concatenate([a[:, :64], a[:, :64], a[:, 64:], a[:, 64:]], axis=1)
            w_all = jnp.concatenate([w[:, :1024], dup(wk), dup(wv)], axis=1).astype(BF16)
            cos, sin, rope_map = _rope_tables(lp, ls_, bp, bs)
            x, q, k, v = _inproj_odd_call(xn, z, mods_g[l - 1], mods_g[l], norm_mix[l][None], cos, sin, rope_map, w_all)
            xres = [x]
            pb = WINDOW
            tq, g = 512, 2
            sink = sinks_c[i][None] * LOG2E
            sink_spec = [pl.BlockSpec(memory_space=pltpu.SMEM)]
            dupc = lambda c: jnp.concatenate([c[:, :, 0], c[:, :, 0], c[:, :, 1], c[:, :, 1]], axis=-1).reshape(bs * pb, 256).astype(BF16)
            ck, cv = dupc(cache_k_l), dupc(cache_v_l)
            additive = lambda valid: jnp.asarray(np.where(valid, 0.0, -np.inf), F32)
            masks = (additive(_band_valid(g, pb)), additive(_band_valid(g, pb, tq // (CHUNK * g))),
                     additive(_band_valid(1, pb)))
            o = _attention(_swa_kernel, q, k, v, ck, cv, masks, [sink], sink_spec, width=1024, kv_width=256, pb=pb,
                           tq=tq, g=g, bp=bp, lp=lp, bs=bs, name="swa")
            undup = lambda a: jnp.concatenate([a[:, 0:64], a[:, 128:192]], axis=1).astype(F32)
            tail = lambda a: jnp.stack([undup(a[(b + 1) * lp - pb:(b + 1) * lp]) for b in range(bp)]).reshape(1, bp, pb, KVC, DHC)
            new = lambda a: undup(a[n_p:]).reshape(bs, ls_, KVC, DHC)
            sk_p, sv_p = tail(k), tail(v)
            sk_s = jnp.concatenate([cache_swa_k[i][:, ls_:], new(k)], axis=1)[None]
            sv_s = jnp.concatenate([cache_swa_v[i][:, ls_:], new(v)], axis=1)[None]
            os_, ws = [o], [w_out_l.astype(BF16)]
        xn, disp, meta, counts = _outproj_call(xres, os_, ws, mods_g[l], norm_ffn[l][None], wr, br, n_pad)
        wgu = _cast_call(w_gate_up, l, counts)
        wdn = _cast_call(w_down, l, counts if tok is None else tok)
        z, tok = _moe_layer(disp, meta, counts, wgu, wdn, n, n_pad, sort_rows)

    y_prompt, y_sample = _final_call(xn, z, mods_g[DEPTH - 1], norm_final[None], n_p)
    return (y_prompt.reshape(bp, lp, d), y_sample.reshape(bs, ls_, d),
            gla_p, gla_s, bk_p, bv_p, bk_s, bv_s, sk_p, sv_p, sk_s, sv_s)
```

```python
import functools

import numpy as np
import jax
import jax.numpy as jnp
from jax import lax
from jax.experimental import pallas as pl
from jax.experimental.pallas import tpu as pltpu
from jax.experimental.pallas import tpu_sc as plsc

F32 = jnp.float32
BF16 = jnp.bfloat16

D_MODEL = 1024
DEPTH = 2
CHUNK = 64
PAST_LEN = 4096
HA, DKA, DVA = 4, 64, 128
GATE_RANK = 16
GATE_TAU = 16.0
HB, DHB = 8, 64
N_PREV_B = 8
MAX_REL = 128
HC, KVC, DHC = 16, 2, 64
WINDOW = 128
ROPE_THETA = 10000.0
N_EXPERTS = 16
N_GROUPS = 4
EXP_PER_GROUP = 4
D_FF = 512
EPS = 1e-6

N_PAIRS = 6
N_BUCKETS = N_GROUPS * N_PAIRS
BUCKET_ROWS = 32
Y_SLABS = 4
DISP_SLABS = Y_SLABS + 1
TMO = 256
MOE_TILES = 2
SC_WINDOW = 128
SC_WORKERS = 32
SC_GROUP = 3

TM = 512
SEQ_ROWS = 16
SUB = 16
LOG2E = 1.4426950408889634


def _cparams(sem, vmem_mb=48):
    return pltpu.CompilerParams(dimension_semantics=sem, vmem_limit_bytes=vmem_mb * 1024 * 1024)


def _dot(a, b):
    return jnp.dot(a, b, preferred_element_type=F32)


def _dot_nt(a, b):
    return lax.dot_general(a, b, (((1,), (1,)), ((), ())), preferred_element_type=F32)


def _split(a):
    hi = a.astype(BF16)
    lo = (a - hi.astype(F32)).astype(BF16)
    return hi, lo


def _dot3(a, b):
    ah, al = _split(a)
    bh, bl = _split(b)
    return _dot(ah, bh) + _dot(ah, bl) + _dot(al, bh)


def _dot3_narrow(a, b):
    ah, al = _split(a)
    bh, bl = _split(b)
    n = b.shape[1]
    p = _dot(ah, jnp.concatenate([bh, bl], axis=1))
    return p[:, :n] + p[:, n:] + _dot(al, bh)


def _sigmoid(x):
    return 1.0 / (1.0 + jnp.exp(-x))


def _group_affine(y, mul, add):
    parts = []
    for gi in range(y.shape[0] // CHUNK):
        p = y[gi * CHUNK:(gi + 1) * CHUNK]
        if mul is not None:
            p = p * mul[gi:gi + 1]
        if add is not None:
            p = p + add[gi:gi + 1]
        parts.append(p)
    return jnp.concatenate(parts, axis=0)


def _norm_mod(x, g, shift, scale):
    ms = jnp.mean(x * x, axis=-1, keepdims=True)
    return _group_affine(x * lax.rsqrt(ms + EPS) * g, 1.0 + scale, shift)


def _mod_spec(part):
    return pl.BlockSpec((TM // CHUNK, D_MODEL), lambda i: (i, part))


SHIFT_MIX, SCALE_MIX, GATE_MIX, SHIFT_FFN, SCALE_FFN, GATE_FFN = range(6)


def _on_token_tile(xp_ref, xs_ref, n_prompt_tiles, body):
    @pl.when(pl.program_id(0) < n_prompt_tiles)
    def _():
        body(xp_ref)

    @pl.when(pl.program_id(0) >= n_prompt_tiles)
    def _():
        body(xs_ref)


def _token_specs(n_prompt_tiles, d):
    return [pl.BlockSpec((TM, d), lambda i: (jnp.minimum(i, n_prompt_tiles - 1), 0)),
            pl.BlockSpec((TM, d), lambda i: (0, 0))]


def _ada_kernel(c_ref, w_ref, b_ref, o_ref):
    c = c_ref[...]
    o_ref[0] = _dot3(c * _sigmoid(c), w_ref[0]) + b_ref[0]


def _ada_call(c16, w_ada, b_ada):
    d = D_MODEL
    tn = 1024
    return pl.pallas_call(
        _ada_kernel,
        grid=(DEPTH, 6 * d // tn),
        in_specs=[pl.BlockSpec((SEQ_ROWS, d), lambda l, j: (0, 0)),
                  pl.BlockSpec((1, d, tn), lambda l, j: (l, 0, j)),
                  pl.BlockSpec((1, 1, tn), lambda l, j: (l, 0, j))],
        out_specs=pl.BlockSpec((1, SEQ_ROWS, tn), lambda l, j: (l, 0, j)),
        out_shape=jax.ShapeDtypeStruct((DEPTH, SEQ_ROWS, 6 * d), F32),
        compiler_params=_cparams(("arbitrary", "arbitrary")),
        name="ada",
    )(c16, w_ada, b_ada.reshape(DEPTH, 1, 6 * d))


def _inproj_even_kernel(xp_ref, xs_ref, sh_ref, sc_ref, g_ref, w_ref, wla_ref, wg_ref, bg_ref,
                        qa_ref, ka_ref, va_ref, ra_ref, qb_ref, kb_ref, vb_ref, ga_ref, *, n_prompt_tiles):
    def body(x_ref):
        t = x_ref.shape[0]
        outs = ((qa_ref, 0, 256, DKA ** -0.5), (ka_ref, 256, 512, None), (va_ref, 512, 1024, None),
                (ra_ref, 1024, 1536, None), (qb_ref, 1536, 2048, DHB ** -0.5 * LOG2E), (kb_ref, 2048, 2560, None),
                (vb_ref, 2560, 3072, None))
        shift, scale_ = sh_ref[...], sc_ref[...]
        halves = [slice(0, t // 2), slice(t // 2, t)]
        grp = [slice(0, t // (2 * CHUNK)), slice(t // (2 * CHUNK), t // CHUNK)]
        hbs = [_norm_mod(x_ref[rs, :], g_ref[...], shift[gs], scale_[gs]).astype(BF16) for rs, gs in zip(halves, grp)]
        for rs, hb in zip(halves, hbs):
            zs = [_dot(hb, w_ref[:, lo:hi]) for _, lo, hi, _ in outs]
            la = _dot(hb, wla_ref[...])
            for z, (o_ref, _, _, scale) in zip(zs, outs):
                o_ref[rs, :] = (z if scale is None else z * scale).astype(BF16)
            gl = _dot3(la, wg_ref[...]) + bg_ref[...]
            ga_ref[rs, :] = -(jnp.maximum(-gl, 0.0) + jnp.log(1.0 + jnp.exp(-jnp.abs(gl)))) * (1.0 / GATE_TAU)

    _on_token_tile(xp_ref, xs_ref, n_prompt_tiles, body)


def _inproj_even_call(xp, xs, mods, g, w_main, w_la, w_gate, b_gate):
    d = xp.shape[1]
    npt = xp.shape[0] // TM
    n = xp.shape[0] + xs.shape[0]
    row = lambda i: (i, 0)
    const = lambda i: (0, 0)
    widths = (256, 256, 512, 512, 512, 512, 512)
    out_shape = [jax.ShapeDtypeStruct((n, w), BF16) for w in widths] + [jax.ShapeDtypeStruct((n, 256), F32)]
    out_specs = [pl.BlockSpec((TM, w), row) for w in widths] + [pl.BlockSpec((TM, 256), row)]
    return pl.pallas_call(
        functools.partial(_inproj_even_kernel, n_prompt_tiles=npt),
        grid=(n // TM,),
        in_specs=_token_specs(npt, d) + [
            _mod_spec(SHIFT_MIX), _mod_spec(SCALE_MIX),
            pl.BlockSpec((1, d), const),
            pl.BlockSpec(w_main.shape, const), pl.BlockSpec(w_la.shape, const),
            pl.BlockSpec(w_gate.shape, const), pl.BlockSpec(b_gate.shape, const)],
        out_specs=out_specs, out_shape=out_shape,
        compiler_params=_cparams(("parallel",)),
        name="inproj_even",
    )(xp, xs, mods, mods, g, w_main, w_la, w_gate, b_gate)


def _rope(x, cos, sin_signed):
    t, w = x.shape
    lane = lax.broadcasted_iota(jnp.int32, (1, w), 1)
    first_half = (lane & 63) < 32
    rot = jnp.where(first_half, pltpu.roll(x, w - 32, 1), pltpu.roll(x, 32, 1))
    reps = w // 128
    return x * jnp.tile(cos, (1, reps)) + rot * jnp.tile(sin_signed, (1, reps))


def _unpack_pairs(slabs, dtype):
    lo = [pltpu.bitcast(s << 16, F32) for s in slabs]
    hi = [pltpu.bitcast(s & jnp.int32(-65536), F32) for s in slabs]
    return jnp.concatenate(lo + hi, axis=1).astype(dtype)


def _pack_pairs(x):
    bits = pltpu.bitcast(x.astype(BF16).astype(F32), jnp.int32)
    half = x.shape[1] // 2
    packed = ((bits[:, :half] >> 16) & jnp.int32(0xFFFF)) | (bits[:, half:] & jnp.int32(-65536))
    return [packed[:, 128 * s:128 * (s + 1)] for s in range(half // 128)]


def _add_moe(xn_ref, z_ref, gate_ref):
    y = _unpack_pairs([z_ref[s] for s in range(z_ref.shape[0])], F32)
    return xn_ref[...] + _group_affine(y, gate_ref[...], None)


def _rope_tables(lp, ls_, bp, bs):
    assert PAST_LEN + ls_ <= lp and lp % 128 == 0 and bs * ls_ == TM
    half = DHC // 2
    inv = ROPE_THETA ** (-jnp.arange(half, dtype=F32) / half)
    inv = jnp.tile(inv, 128 // half)
    sign = jnp.asarray(np.tile(np.repeat([-1.0, 1.0], half), 128 // DHC), F32)
    a = jnp.asarray(np.arange(lp // 128) * 128, F32)[:, None] * inv[None, :]
    b = jnp.asarray(np.arange(128), F32)[:, None] * inv[None, :]
    ca, sa, cb, sb = jnp.cos(a)[:, None], jnp.sin(a)[:, None], jnp.cos(b)[None], jnp.sin(b)[None]
    cos = (ca * cb - sa * sb).reshape(lp, 128)
    sin = ((sa * cb + ca * sb) * sign).reshape(lp, 128)
    with_sample = lambda t: jnp.concatenate([t, jnp.tile(t[PAST_LEN:PAST_LEN + ls_], (bs, 1))], axis=0)
    tiles = lp // TM
    return with_sample(cos), with_sample(sin), lambda i: (jnp.where(i < bp * tiles, i % tiles, tiles), 0)


def _inproj_odd_kernel(xn_ref, z_ref, gate_ref, sh_ref, sc_ref, g_ref, cos_ref, sin_ref, w_ref,
                       x_ref, q_ref, k_ref, v_ref):
    x = _add_moe(xn_ref, z_ref, gate_ref)
    x_ref[...] = x
    hb = _norm_mod(x, g_ref[...], sh_ref[...], sc_ref[...]).astype(BF16)
    cos, sin = cos_ref[...], sin_ref[...]
    q = _rope(_dot(hb, w_ref[:, 0:1024]), cos, sin)
    q_ref[...] = (q * (DHC ** -0.5 * LOG2E)).astype(BF16)
    k_ref[...] = _rope(_dot(hb, w_ref[:, 1024:1280]), cos, sin).astype(BF16)
    v_ref[...] = _dot(hb, w_ref[:, 1280:1536]).astype(BF16)


def _inproj_odd_call(xn, z, mods_prev, mods, g, cos, sin, rope_map, w):
    n, d = xn.shape
    row = lambda i: (i, 0)
    const = lambda i: (0, 0)
    widths = (1024, 256, 256)
    return pl.pallas_call(
        _inproj_odd_kernel,
        grid=(n // TM,),
        in_specs=[pl.BlockSpec((TM, d), row), pl.BlockSpec((z.shape[0], TM, 128), lambda i: (0, i, 0)),
                  _mod_spec(GATE_FFN), _mod_spec(SHIFT_MIX), _mod_spec(SCALE_MIX),
                  pl.BlockSpec((1, d), const),
                  pl.BlockSpec((TM, 128), rope_map), pl.BlockSpec((TM, 128), rope_map),
                  pl.BlockSpec(w.shape, const)],
        out_specs=[pl.BlockSpec((TM, d), row)] + [pl.BlockSpec((TM, wd), row) for wd in widths],
        out_shape=[jax.ShapeDtypeStruct((n, d), F32)] + [jax.ShapeDtypeStruct((n, wd), BF16) for wd in widths],
        compiler_params=_cparams(("parallel",)),
        name="inproj_odd",
    )(xn, z, mods_prev, mods, mods, g, cos, sin, w)


def _gla_tri():
    t = np.arange(CHUNK)[:, None]
    s = np.arange(CHUNK)[None, :]
    cum = s <= t
    start = s < (t // SUB) * SUB
    end = s < (t // SUB + 1) * SUB
    return jnp.asarray(np.concatenate([cum, start, end], axis=0).astype(np.float32), dtype=BF16)


def _gla_kernel(q_ref, k_ref, v_ref, g_ref, r_ref, s0_ref, gn_ref, tri_ref, o_ref, sout_ref, s_ref, *, nb):
    c_ = CHUNK
    nsub = c_ // SUB

    @pl.when(pl.program_id(1) == 0)
    def _():
        s_ref[...] = s0_ref[0]

    tri = tri_ref[...]
    lane = lax.broadcasted_iota(jnp.int32, (1, 128), 1)
    hmask = [jnp.where(lane < DKA, 1.0, 0.0), jnp.where(lane >= DKA, 1.0, 0.0)]
    ti = lax.broadcasted_iota(jnp.int32, (c_, c_), 0)
    si = lax.broadcasted_iota(jnp.int32, (c_, c_), 1)
    rb, cb = ti >> 4, si >> 4
    m_diag = (rb == cb) & (si <= ti)
    m_off = [(cb == j) & (rb > j) for j in range(nsub - 1)]
    hk = HA * DKA
    eye = lax.broadcasted_iota(jnp.int32, (hk, hk), 0) == lax.broadcasted_iota(jnp.int32, (hk, hk), 1)
    gn = gn_ref[...]

    chunks = range(nb)
    heads = [(p, hh) for p in range(HA // 2) for hh in range(2)]
    rows = [slice(c * c_, (c + 1) * c_) for c in chunks]
    pair = [slice(128 * p, 128 * (p + 1)) for p in range(HA // 2)]
    css = []
    for c in chunks:
        g_hi, g_lo = _split(g_ref[rows[c], :])
        css.append(_dot(tri, g_hi) + _dot(tri, g_lo))
    lhs1, lhs2, kds, kes, q_inter, klts, dcols = [], [], [], [], [], [], []
    for c in chunks:
        b, rs, re = css[c][0:c_], css[c][c_:2 * c_], css[c][2 * c_:3 * c_]
        q = q_ref[rows[c], :].astype(F32)
        k = k_ref[rows[c], :].astype(F32)
        bl = b[c_ - 1:c_, :]
        qd = q * jnp.exp(b - rs)
        kd = k * jnp.exp(rs - b)
        ke = k * jnp.exp(re - b)
        qi = q * jnp.exp(b)
        kl = k * jnp.exp(bl - b)
        ql = [q * jnp.exp(jnp.minimum(b - b[SUB * (j + 1) - 1:SUB * (j + 1), :], 0.0)) for j in range(nsub - 1)]
        dcols.append(jnp.sum(jnp.where(eye, jnp.broadcast_to(jnp.exp(bl), (hk, hk)), 0.0), axis=1, keepdims=True))
        kds.append([kd[:, ls].astype(BF16) for ls in pair])
        kes.append([ke[:, ls].astype(BF16) for ls in pair])
        klts.append([kl[:, ls].T.astype(BF16) for ls in pair])
        lhs1.append([(qd[:, pair[p]] * hmask[hh]).astype(BF16) for p, hh in heads])
        lhs2.append([jnp.concatenate([ql[j][:, pair[p]] * hmask[hh] for j in range(nsub - 1)], axis=0).astype(BF16)
                     for p, hh in heads])
        q_inter.append([(qi[:, pair[p]] * hmask[hh]).astype(BF16) for p, hh in heads])
    a1s = [[_dot_nt(lhs1[c][h], kds[c][p]) for h, (p, hh) in enumerate(heads)] for c in chunks]
    a2s = [[_dot_nt(lhs2[c][h], kes[c][p]) for h, (p, hh) in enumerate(heads)] for c in chunks]
    atts = []
    for c in chunks:
        per_head = []
        for h in range(HA):
            att = jnp.zeros((c_, c_), F32)
            for j in reversed(range(nsub - 1)):
                att = jnp.where(m_off[j], a2s[c][h][j * c_:(j + 1) * c_], att)
            per_head.append(jnp.where(m_diag, a1s[c][h], att).astype(BF16))
        atts.append(per_head)
    vs_ = [[v_ref[rows[c], DVA * h:DVA * (h + 1)] for h in range(HA)] for c in chunks]
    o_intra = [[_dot(atts[c][h], vs_[c][h]) for h in range(HA)] for c in chunks]
    upds = [jnp.concatenate([_dot(klts[c][p][DKA * hh:DKA * (hh + 1)], vs_[c][2 * p + hh]) for p, hh in heads], axis=0)
            for c in chunks]

    s_cur = s_ref[...]
    s_in = []
    for c in chunks:
        s_in.append(s_cur.astype(BF16))
        s_cur = dcols[c] * s_cur + upds[c]
    s_ref[...] = s_cur
    sout_ref[0] = s_cur

    for c in chunks:
        for h in range(HA):
            o = o_intra[c][h] + _dot(q_inter[c][h], s_in[c][pair[h // 2], :])
            ms = jnp.mean(o * o, axis=-1, keepdims=True)
            vs = slice(DVA * h, DVA * (h + 1))
            rr = r_ref[rows[c], vs].astype(F32)
            o_ref[rows[c], vs] = (o * lax.rsqrt(ms + EPS) * gn * (rr * _sigmoid(rr))).astype(BF16)


def _gla_call(q, k, v, g, r, s0, gn, o_prev, *, n_seq, seq_rows, row0, nb):
    tq = nb * CHUNK
    steps = seq_rows // tq
    blk0 = row0 // tq
    row = lambda b, j: (blk0 + b * steps + j, 0)
    const = lambda b, j: (0, 0)
    tri = _gla_tri()
    in_specs = [pl.BlockSpec((tq, 256), row), pl.BlockSpec((tq, 256), row), pl.BlockSpec((tq, 512), row),
                pl.BlockSpec((tq, 256), row), pl.BlockSpec((tq, 512), row),
                pl.BlockSpec((1, 256, 128), lambda b, j: (b, 0, 0)),
                pl.BlockSpec((1, 128), const), pl.BlockSpec(tri.shape, const)]
    args = [q, k, v, g, r, s0, gn, tri]
    aliases = {}
    if o_prev is not None:
        in_specs.append(pl.BlockSpec(memory_space=pl.ANY))
        args.append(o_prev)
        aliases = {len(args) - 1: 0}
    kern = functools.partial(_gla_kernel, nb=nb)
    if o_prev is not None:
        kern = _drop_arg(kern, 8)
    return pl.pallas_call(
        kern,
        grid=(n_seq, steps),
        in_specs=in_specs,
        out_specs=[pl.BlockSpec((tq, 512), row), pl.BlockSpec((1, 256, 128), lambda b, j: (b, 0, 0))],
        out_shape=[jax.ShapeDtypeStruct((q.shape[0], 512), BF16), jax.ShapeDtypeStruct((n_seq, 256, 128), F32)],
        scratch_shapes=[pltpu.VMEM((256, 128), F32)],
        input_output_aliases=aliases,
        compiler_params=_cparams(("arbitrary", "arbitrary")),
        name="gla",
    )(*args)


def _drop_arg(fn, idx):
    def wrapped(*refs):
        return fn(*refs[:idx], *refs[idx + 1:])
    return wrapped


def _window(prev_ref, cur_ref, lo, hi, pb, ls):
    if lo < pb:
        return jnp.concatenate([prev_ref[lo:pb, ls], cur_ref[0:hi - pb, ls]], axis=0)
    return cur_ref[lo - pb:hi - pb, ls]


def _band_kernel(q_ref, kp_ref, kc_ref, vp_ref, vc_ref, bias_ref, o_ref, *, g, n_sub, pb):
    qs = CHUNK * g
    kw_rows = pb + qs
    lane = lax.broadcasted_iota(jnp.int32, (1, 128), 1)
    low = lane < DHB
    hmask = [jnp.where(low, 1.0, 0.0), jnp.where(low, 0.0, 1.0)]
    for s in range(n_sub):
        sb = s if bias_ref.shape[0] > 1 else 0
        rows = slice(qs * s, qs * (s + 1))
        lanes = [slice(128 * p, 128 * (p + 1)) for p in range(HB // 2)]
        heads = [(p, hh) for p in range(HB // 2) for hh in range(2)]
        qps = [q_ref[rows, ls].astype(F32) for ls in lanes]
        kws = [_window(kp_ref, kc_ref, qs * s, qs * s + kw_rows, pb, ls) for ls in lanes]
        vws = [_window(vp_ref, vc_ref, qs * s, qs * s + kw_rows, pb, ls) for ls in lanes]
        kts = [kw.astype(F32).T.astype(BF16) for kw in kws]
        scs = [_dot((qps[p] * hmask[hh]).astype(BF16), kts[p]) + bias_ref[sb, 2 * p + hh] for p, hh in heads]
        pes = [jnp.exp2(sc - jnp.max(sc, axis=-1, keepdims=True)) for sc in scs]
        outs = [_dot(pe.astype(BF16), vws[p]) / jnp.sum(pe, axis=-1, keepdims=True) for pe, (p, hh) in zip(pes, heads)]
        for p, ls in enumerate(lanes):
            o_ref[rows, ls] = jnp.where(low, outs[2 * p], outs[2 * p + 1]).astype(BF16)


def _band_valid(g, pb, n_sub=None):
    rows, kw = CHUNK * g, pb + CHUNK * g
    r = np.arange(rows)[:, None]
    c = np.arange(kw)[None, :]
    dd = c // CHUNK - r // CHUNK
    band = (dd >= 0) & (dd <= pb // CHUNK)
    if n_sub is None:
        return band[None]
    return np.stack([band & (c >= pb - rows * s) for s in range(n_sub)])


def _band_bias(table, g, pb, valid):
    rows, kw = CHUNK * g, pb + CHUNK * g
    period = kw + rows
    m = np.arange(period)
    m = np.where(m < kw, m, m - period)
    ext = table[:, np.clip(m - pb, -MAX_REL, MAX_REL) + MAX_REL] * LOG2E
    flat = jnp.tile(ext, (1, rows))[:, :rows * (period - 1)]
    bias = flat.reshape(table.shape[0], rows, period - 1)[:, :, :kw]
    return jnp.where(valid[:, None], bias[None], -jnp.inf)


def _attn_call(kernel, q, kp, kc, vp, vc, extra, extra_specs, o_prev, *, width, kv_width, tq, pb,
               n_blocks, blk_map, prev_map, name):
    row = lambda i: (blk_map(i), 0)
    prev = lambda i: (prev_map(i), 0)
    in_specs = [pl.BlockSpec((tq, width), row),
                pl.BlockSpec((pb, kv_width), prev), pl.BlockSpec((tq, kv_width), row),
                pl.BlockSpec((pb, kv_width), prev), pl.BlockSpec((tq, kv_width), row)] + extra_specs
    args = [q, kp, kc, vp, vc] + extra
    aliases = {}
    if o_prev is not None:
        in_specs.append(pl.BlockSpec(memory_space=pl.ANY))
        args.append(o_prev)
        aliases = {len(args) - 1: 0}
        kernel = _drop_arg(kernel, len(args) - 1)
    return pl.pallas_call(
        kernel,
        grid=(n_blocks,),
        in_specs=in_specs,
        out_specs=pl.BlockSpec((tq, width), row),
        out_shape=jax.ShapeDtypeStruct((q.shape[0], width), BF16),
        input_output_aliases=aliases,
        compiler_params=_cparams(("parallel",)),
        name=name,
    )(*args)


def _attention(kernel_fn, q, k, v, cache_k, cache_v, masks, extra, extra_specs, *, width, kv_width, pb, tq, g,
               bp, lp, bs, name):
    bps = lp // tq
    n_sub = tq // (CHUNK * g)
    spec = lambda a: [pl.BlockSpec(a.shape, lambda i: (0,) * a.ndim)]
    kern = functools.partial(kernel_fn, g=g, n_sub=n_sub, pb=pb)
    common = dict(width=width, kv_width=kv_width, pb=pb)
    main = lambda i: (i // (bps - 1)) * bps + i % (bps - 1) + 1
    o = _attn_call(kern, q, k, k, v, v, [masks[0]] + extra, spec(masks[0]) + extra_specs, None, tq=tq,
                   n_blocks=bp * (bps - 1), blk_map=main, prev_map=lambda i: main(i) * (tq // pb) - 1,
                   name=name + "_main", **common)
    first = lambda i: i * bps
    o = _attn_call(kern, q, k, k, v, v, [masks[1]] + extra, spec(masks[1]) + extra_specs, o, tq=tq,
                   n_blocks=bp, blk_map=first, prev_map=lambda i: jnp.maximum(first(i) * (tq // pb) - 1, 0),
                   name=name + "_first", **common)
    samp = functools.partial(kernel_fn, g=1, n_sub=1, pb=pb)
    return _attn_call(samp, q, cache_k, k, cache_v, v, [masks[2]] + extra, spec(masks[2]) + extra_specs, o, tq=CHUNK,
                      n_blocks=bs, blk_map=lambda i: bp * lp // CHUNK + i, prev_map=lambda i: i,
                      name=name + "_sample", **common)


def _swa_kernel(q_ref, kp_ref, kc_ref, vp_ref, vc_ref, mask_ref, sink_ref, o_ref, *, g, n_sub, pb):
    qs = CHUNK * g
    kw_rows = pb + qs
    lane = lax.broadcasted_iota(jnp.int32, (1, 128), 1)
    low = lane < DHC
    hmask = [jnp.where(low, 1.0, 0.0), jnp.where(low, 0.0, 1.0)]
    pairs_per_kv = HC // KVC // 2
    for s in range(n_sub):
        msk = mask_ref[s if mask_ref.shape[0] > 1 else 0]
        rows = slice(qs * s, qs * (s + 1))
        kws = [_window(kp_ref, kc_ref, qs * s, qs * s + kw_rows, pb, slice(128 * kv, 128 * (kv + 1))) for kv in range(KVC)]
        vws = [_window(vp_ref, vc_ref, qs * s, qs * s + kw_rows, pb, slice(128 * kv, 128 * (kv + 1))) for kv in range(KVC)]
        heads = [(j, hh) for j in range(HC // 2) for hh in range(2)]
        qps = [q_ref[rows, 128 * j:128 * (j + 1)].astype(F32) for j in range(HC // 2)]
        scs = [_dot_nt((qps[j] * hmask[hh]).astype(BF16), kws[j // pairs_per_kv]) + msk for j, hh in heads]
        sks = [sink_ref[0, 2 * j + hh] for j, hh in heads]
        ms = [jnp.maximum(jnp.max(sc, axis=-1, keepdims=True), sk) for sc, sk in zip(scs, sks)]
        pes = [jnp.exp2(sc - m) for sc, m in zip(scs, ms)]
        outs = [_dot(pe.astype(BF16), vws[j // pairs_per_kv]) / (jnp.sum(pe, axis=-1, keepdims=True) + jnp.exp2(sk - m))
                for pe, sk, m, (j, hh) in zip(pes, sks, ms, heads)]
        for j in range(HC // 2):
            o_ref[rows, 128 * j:128 * (j + 1)] = jnp.where(low, outs[2 * j], outs[2 * j + 1]).astype(BF16)


def _route(logits_t):
    a = [logits_t[4 * j:4 * j + 4] for j in range(EXP_PER_GROUP)]

    def first_argmax(vals, m):
        idx = jnp.full(m.shape, float(len(vals) - 1), F32)
        for j in reversed(range(len(vals) - 1)):
            idx = jnp.where(vals[j] == m, float(j), idx)
        return idx

    m1 = functools.reduce(jnp.maximum, a)
    i1 = first_argmax(a, m1)
    bsec = [jnp.where(i1 == float(j), -jnp.inf, a[j]) for j in range(EXP_PER_GROUP)]
    m2 = functools.reduce(jnp.maximum, bsec)
    i2 = first_argmax(bsec, m2)
    rows = lambda x: [x[gi:gi + 1] for gi in range(N_GROUPS)]
    gm = functools.reduce(jnp.maximum, rows(m1))
    gscore = jnp.exp(m1 - gm) + jnp.exp(m2 - gm)
    gs = rows(gscore)
    gsel = first_argmax(gs, functools.reduce(jnp.maximum, gs))

    def pick(x):
        xr = rows(x)
        out = xr[N_GROUPS - 1]
        for gi in reversed(range(N_GROUPS - 1)):
            out = jnp.where(gsel == float(gi), xr[gi], out)
        return out

    p1 = jnp.exp(pick(m1) - gm)
    p2 = jnp.exp(pick(m2) - gm)
    w1 = p1 / (p1 + p2)
    w2 = p2 / (p1 + p2)
    s1, s2 = pick(i1), pick(i2)
    lo, hi = jnp.minimum(s1, s2), jnp.maximum(s1, s2)
    pair = jnp.where(lo == 0.0, hi - 1.0, jnp.where(lo == 1.0, hi + 1.0, 5.0))
    bucket = gsel * float(N_PAIRS) + pair
    first_is_lo = s1 < s2
    return bucket, jnp.where(first_is_lo, w1, w2), jnp.where(first_is_lo, w2, w1)


def _outproj_kernel(*refs, n_x, n_o, n_prompt_tiles):
    x_refs = refs[:n_x]
    o_refs = refs[n_x:n_x + n_o]
    w_refs = refs[n_x + n_o:n_x + 2 * n_o]
    (gate_ref, nf_ref, sh_ref, sc_ref, wr_ref, br_ref, tri_ref,
     xn_ref, disp_ref, meta_ref, cnt_ref, run_ref) = refs[n_x + 2 * n_o:]
    t = xn_ref.shape[0]

    @pl.when(pl.program_id(0) == 0)
    def _():
        run_ref[...] = jnp.zeros_like(run_ref)

    x_src = x_refs[0]
    if n_x == 2:
        def stage(x_ref):
            xn_ref[...] = x_ref[...]

        _on_token_tile(x_refs[0], x_refs[1], n_prompt_tiles, stage)
        x_src = xn_ref

    halves = [slice(0, t // 2), slice(t // 2, t)]
    grp = [slice(0, t // (2 * CHUNK)), slice(t // (2 * CHUNK), t // CHUNK)]
    ys = []
    for rs in halves:
        y = _dot(o_refs[0][rs, :], w_refs[0][...])
        for i in range(1, n_o):
            y = y + _dot(o_refs[i][rs, :], w_refs[i][...])
        ys.append(y)
    gate, shift, scale = gate_ref[...], sh_ref[...], sc_ref[...]
    gys = [_group_affine(y, gate[gs], None) for y, gs in zip(ys, grp)]

    for rs, gy in zip(halves, gys):
        xn_ref[rs, :] = x_src[rs, :] + gy
    hs = [_norm_mod(xn_ref[rs, :], nf_ref[...], shift[gs], scale[gs]) for rs, gs in zip(halves, grp)]
    for rs, h in zip(halves, hs):
        for s, slab in enumerate(_pack_pairs(h)):
            disp_ref[s, rs, :] = slab
    logits_t = [(_dot3_narrow(h, wr_ref[...]) + br_ref[...]).T[0:N_EXPERTS] for h in hs]
    bucket, w_lo, w_hi = _route(jnp.concatenate(logits_t, axis=1))
    r128 = lax.broadcasted_iota(jnp.int32, (128, t), 0)
    tok = (pl.program_id(0) * t + lax.broadcasted_iota(jnp.int32, (1, t), 1)).astype(F32)
    aux = jnp.where(r128 == 0, w_lo, jnp.where(r128 == 1, w_hi, jnp.where(r128 == 2, tok, 0.0))).T
    disp_ref[disp_ref.shape[0] - 1] = pltpu.bitcast(aux, jnp.int32)
    brow = lax.broadcasted_iota(jnp.int32, (BUCKET_ROWS, t), 0).astype(F32)
    onehot = jnp.where(brow == bucket, 1.0, 0.0)
    before = _dot(onehot.astype(BF16), tri_ref[...]) + run_ref[:, 0:1]
    rank = jnp.sum(onehot * before, axis=0, keepdims=True)
    run_ref[...] = run_ref[...] + jnp.sum(onehot, axis=1, keepdims=True)
    cnt_ref[...] = run_ref[...]
    r8 = lax.broadcasted_iota(jnp.int32, (8, t), 0)
    meta_ref[...] = jnp.where(r8 == 0, bucket, jnp.where(r8 == 1, rank, 0.0)).astype(jnp.int32)


def _outproj_call(xs_, os_, ws, mods, nf, wr, br, n_pad):
    d = xs_[0].shape[1]
    n = sum(a.shape[0] for a in xs_)
    npt = xs_[0].shape[0] // TM
    row = lambda i: (i, 0)
    const = lambda i: (0, 0)
    n_o = len(os_)
    in_specs = ((_token_specs(npt, d) if len(xs_) == 2 else [pl.BlockSpec((TM, d), row)])
                + [pl.BlockSpec((TM, o.shape[1]), row) for o in os_]
                + [pl.BlockSpec(w.shape, const) for w in ws]
                + [_mod_spec(GATE_MIX), pl.BlockSpec((1, d), const),
                   _mod_spec(SHIFT_FFN), _mod_spec(SCALE_FFN),
                   pl.BlockSpec(wr.shape, const), pl.BlockSpec(br.shape, const),
                   pl.BlockSpec((TM, TM), const)])
    tri = jnp.asarray(np.triu(np.ones((TM, TM), np.float32), k=1), dtype=BF16)
    return pl.pallas_call(
        functools.partial(_outproj_kernel, n_x=len(xs_), n_o=n_o, n_prompt_tiles=npt),
        grid=(n // TM,),
        in_specs=in_specs,
        out_specs=[pl.BlockSpec((TM, d), row), pl.BlockSpec((DISP_SLABS, TM, 128), lambda i: (0, i, 0)),
                   pl.BlockSpec((8, TM), lambda i: (0, i)), pl.BlockSpec((BUCKET_ROWS, 128), const)],
        out_shape=[jax.ShapeDtypeStruct((n, d), F32), jax.ShapeDtypeStruct((DISP_SLABS, n_pad, 128), jnp.int32),
                   jax.ShapeDtypeStruct((8, n), jnp.int32), jax.ShapeDtypeStruct((BUCKET_ROWS, 128), F32)],
        scratch_shapes=[pltpu.VMEM((BUCKET_ROWS, 128), F32)],
        compiler_params=_cparams(("arbitrary",)),
        name="outproj_router",
    )(*xs_, *os_, *ws, mods, nf, mods, mods, wr, br, tri)


def _sc_mesh():
    return plsc.VectorSubcoreMesh(core_axis_name="core", subcore_axis_name="subcore")


def _sc_row_copy(src, idx, n_out, scatter):
    r = idx.shape[0]
    k = SC_GROUP
    w_per = r // (SC_WINDOW * SC_WORKERS)
    assert r % (SC_WINDOW * SC_WORKERS) == 0 and w_per % k == 0
    n_groups = w_per // k

    @functools.partial(
        pl.kernel, out_type=jax.ShapeDtypeStruct((n_out, 128), src.dtype), mesh=_sc_mesh(),
        scratch_types=[pltpu.VMEM((w_per, SC_WINDOW), jnp.int32),
                       pltpu.VMEM((2 * k, SC_WINDOW, 128), src.dtype),
                       pltpu.SemaphoreType.DMA((2,)), pltpu.SemaphoreType.DMA((2,))])
    def copy(x_hbm, i_hbm, o_hbm, ibuf, xbuf, in_sem, out_sem):
        wid = lax.axis_index("core") * (SC_WORKERS // 2) + lax.axis_index("subcore")
        pltpu.sync_copy(i_hbm.at[wid], ibuf)
        first = wid * w_per

        def rows(j):
            return pl.ds((first + j) * SC_WINDOW, SC_WINDOW)

        def start_in(g, slot):
            cps = []
            for c in range(k):
                j = g * k + c
                s = x_hbm.at[rows(j)] if scatter else x_hbm.at[ibuf.at[j]]
                cps.append(pltpu.async_copy(s, xbuf.at[slot * k + c], in_sem.at[slot]))
            return cps

        def start_out(g, slot):
            cps = []
            for c in range(k):
                j = g * k + c
                dst = o_hbm.at[ibuf.at[j]] if scatter else o_hbm.at[rows(j)]
                cps.append(pltpu.async_copy(xbuf.at[slot * k + c], dst, out_sem.at[slot]))
            return cps

        pending_in = start_in(0, 0)
        for g in range(n_groups):
            slot = g % 2
            for cp in pending_in:
                cp.wait()
            pending_out = start_out(g, slot)
            if g + 1 < n_groups:
                pending_in = start_in(g + 1, 1 - slot)
            for cp in pending_out:
                cp.wait()

    return copy(src, idx.reshape(SC_WORKERS, w_per, SC_WINDOW))


def _sc_scatter_rows(src, idx, n_out):
    assert idx.shape == (src.shape[0],)
    return _sc_row_copy(src, idx, n_out, scatter=True)


def _moe_kernel(elo_ref, ehi_ref, nvalid_ref, xs_ref, *refs, n_tok, dump_tiles):
    w_refs, (y_ref, tok_ref) = refs[:4 * MOE_TILES], refs[4 * MOE_TILES:]
    step = pl.program_id(0)
    t = TMO
    tiles = range(MOE_TILES)
    rows = [slice(t * j, t * (j + 1)) for j in tiles]
    auxs = [pltpu.bitcast(xs_ref[Y_SLABS, rows[j], :], F32) for j in tiles]
    r = lax.broadcasted_iota(jnp.int32, (1, t), 1)
    for j in tiles:
        i = step * MOE_TILES + j
        spare = n_tok + (i % dump_tiles) * t + r
        tok = jnp.where(r < nvalid_ref[i], auxs[j].T[2:3, :].astype(jnp.int32), spare)
        for c in range(t // 128):
            tok_ref[j, c:c + 1, :] = tok[:, 128 * c:128 * (c + 1)]

    any_tokens = nvalid_ref[step * MOE_TILES] > 0
    for j in range(1, MOE_TILES):
        any_tokens = jnp.logical_or(any_tokens, nvalid_ref[step * MOE_TILES + j] > 0)

    @pl.when(any_tokens)
    def _():
        units = [(j, e) for j in tiles for e in range(2)]
        hs = [_unpack_pairs([xs_ref[s, rows[j], :] for s in range(Y_SLABS)], BF16) for j in tiles]
        abs_ = [_dot(hs[j], w_refs[4 * j + e][0, 0]) for j, e in units]
        acts = [(ab[:, :D_FF] * _sigmoid(ab[:, :D_FF]) * ab[:, D_FF:]).astype(BF16) for ab in abs_]
        ys = [_dot(act, w_refs[4 * j + 2 + e][0, 0]) for act, (j, e) in zip(acts, units)]
        for j in tiles:
            acc = auxs[j][:, 0:1] * ys[2 * j] + auxs[j][:, 1:2] * ys[2 * j + 1]
            for s, slab in enumerate(_pack_pairs(acc)):
                y_ref[s, rows[j], :] = slab

    @pl.when(jnp.logical_not(any_tokens))
    def _():
        y_ref[...] = jnp.zeros_like(y_ref)


def _moe_call(xs, elo, ehi, nvalid, wgu, wdn, layer, n_tiles, n_tok, dump_tiles):
    d = wgu.shape[2]
    m = MOE_TILES
    assert n_tiles % m == 0
    weight_specs = []
    for j in range(m):
        for shape in ((1, 1, d, 2 * D_FF), (1, 1, D_FF, d)):
            for sel in range(2):
                weight_specs.append(pl.BlockSpec(
                    shape, lambda i, lo, hi, v, j=j, sel=sel: (layer, (lo, hi)[sel][m * i + j], 0, 0)))
    weights = [w for _ in range(m) for w in (wgu, wgu, wdn, wdn)]
    return pl.pallas_call(
        functools.partial(_moe_kernel, n_tok=n_tok, dump_tiles=dump_tiles),
        grid_spec=pltpu.PrefetchScalarGridSpec(
            num_scalar_prefetch=3,
            grid=(n_tiles // m,),
            in_specs=[pl.BlockSpec((DISP_SLABS, m * TMO, 128), lambda i, lo, hi, v: (0, i, 0))] + weight_specs,
            out_specs=[pl.BlockSpec((Y_SLABS, m * TMO, 128), lambda i, lo, hi, v: (0, i, 0)),
                       pl.BlockSpec((m, TMO // 128, 128), lambda i, lo, hi, v: (i, 0, 0))]),
        out_shape=[jax.ShapeDtypeStruct((Y_SLABS, n_tiles * TMO, 128), jnp.int32),
                   jax.ShapeDtypeStruct((n_tiles, TMO // 128, 128), jnp.int32)],
        compiler_params=_cparams(("arbitrary",), vmem_mb=56),
        name="moe_grouped",
    )(elo, ehi, nvalid, xs, *weights)


def _after(x, token):
    return lax.optimization_barrier((x, token))[0]


def _cast_kernel(after_ref, w_ref, o_ref):
    o_ref[...] = w_ref[...].astype(o_ref.dtype)


def _cast_call(w, layer, after):
    _, e, k, n = w.shape
    return pl.pallas_call(
        _cast_kernel,
        grid=(e,),
        in_specs=[pl.BlockSpec(memory_space=pl.ANY), pl.BlockSpec((1, 1, k, n), lambda i: (layer, i, 0, 0))],
        out_specs=pl.BlockSpec((1, 1, k, n), lambda i: (0, i, 0, 0)),
        out_shape=jax.ShapeDtypeStruct((1, e, k, n), BF16),
        compiler_params=_cparams(("parallel",)),
        name="cast_weights",
    )(after, w)


def _moe_layer(disp, meta, counts, w_gate_up, w_down, layer, n, n_pad, sort_rows):
    n_tiles = sort_rows // TMO
    wgu = _cast_call(w_gate_up, layer, counts)
    wdn = _cast_call(w_down, layer, counts)
    layer = 0
    cnt = counts[:N_BUCKETS, 0].astype(jnp.int32)
    padded = ((cnt + TMO - 1) // TMO) * TMO
    ends = jnp.cumsum(padded)
    offs = ends - padded
    bucket, rank = meta[0], meta[1]
    pos = rank + jnp.sum(jnp.where(bucket[None, :] == jnp.arange(N_BUCKETS, dtype=jnp.int32)[:, None],
                                   offs[:, None], 0), axis=0)
    tile_start = jnp.arange(n_tiles, dtype=jnp.int32) * TMO
    tile_bucket = jnp.minimum(jnp.sum((tile_start[:, None] >= ends[None, :]).astype(jnp.int32), axis=1), N_BUCKETS - 1)
    pair_lo = np.array([0, 0, 0, 1, 1, 2], np.int32)
    pair_hi = np.array([1, 2, 3, 2, 3, 3], np.int32)
    b_lo = jnp.asarray(np.repeat(np.arange(N_GROUPS), N_PAIRS) * EXP_PER_GROUP + np.tile(pair_lo, N_GROUPS), jnp.int32)
    b_hi = jnp.asarray(np.repeat(np.arange(N_GROUPS), N_PAIRS) * EXP_PER_GROUP + np.tile(pair_hi, N_GROUPS), jnp.int32)
    onehot_tb = (tile_bucket[:, None] == jnp.arange(N_BUCKETS, dtype=jnp.int32)[None, :]).astype(jnp.int32)
    elo = jnp.sum(onehot_tb * b_lo[None, :], axis=1)
    ehi = jnp.sum(onehot_tb * b_hi[None, :], axis=1)
    bucket_end = jnp.sum(onehot_tb * (offs + cnt)[None, :], axis=1)
    nvalid = jnp.where(tile_start < ends[-1], jnp.clip(bucket_end - tile_start, 0, TMO), 0)
    dump = sort_rows + jnp.arange(n_pad - n, dtype=jnp.int32)
    pos_sc = jnp.concatenate([pos, dump])
    total = sort_rows + n_pad - n
    sc_idx = (pos_sc[None, :] + (jnp.arange(DISP_SLABS, dtype=jnp.int32) * total)[:, None]).reshape(-1)
    xs = _sc_scatter_rows(disp.reshape(DISP_SLABS * n_pad, 128), sc_idx, DISP_SLABS * total)
    ys, tok = _moe_call(xs.reshape(DISP_SLABS, total, 128), elo, ehi, nvalid, wgu, wdn, layer, n_tiles,
                        n, (n_pad - n) // TMO)
    back_idx = (tok.reshape(1, sort_rows) + (jnp.arange(Y_SLABS, dtype=jnp.int32) * n_pad)[:, None]).reshape(-1)
    z = _sc_scatter_rows(ys.reshape(Y_SLABS * sort_rows, 128), back_idx, Y_SLABS * n_pad)
    return z.reshape(Y_SLABS, n_pad, 128), tok


def _final_kernel(xn_ref, z_ref, gate_ref, g_ref, yp_ref, ys_ref, *, n_prompt_tiles):
    x = _add_moe(xn_ref, z_ref, gate_ref)
    ms = jnp.mean(x * x, axis=-1, keepdims=True)
    y = x * lax.rsqrt(ms + EPS) * g_ref[...]
    i = pl.program_id(0)

    @pl.when(i < n_prompt_tiles)
    def _():
        yp_ref[...] = y

    @pl.when(i >= n_prompt_tiles)
    def _():
        ys_ref[...] = y


def _final_call(xn, z, mods, g, n_prompt):
    n, d = xn.shape
    npt = n_prompt // TM
    assert n - n_prompt == TM
    return pl.pallas_call(
        functools.partial(_final_kernel, n_prompt_tiles=npt),
        grid=(n // TM,),
        in_specs=[pl.BlockSpec((TM, d), lambda i: (i, 0)), pl.BlockSpec((z.shape[0], TM, 128), lambda i: (0, i, 0)),
                  _mod_spec(GATE_FFN), pl.BlockSpec((1, d), lambda i: (0, 0))],
        out_specs=_token_specs(npt, d),
        out_shape=[jax.ShapeDtypeStruct((n_prompt, d), F32), jax.ShapeDtypeStruct((TM, d), F32)],
        compiler_params=_cparams(("arbitrary",)),
        name="final_norm",
    )(xn, z, mods, g)


def kernel(x_prompt, x_sample, c_prompt, c_sample, state_gla, cache_band_k, cache_band_v, cache_swa_k, cache_swa_v,
           w_ada, b_ada, norm_mix, norm_ffn, norm_final, w_in_even, w_gate_a, b_gate_a, gla_norm, rel_bias_b,
           w_out_even, w_in_odd, sinks_c, w_out_odd, w_router, b_router, w_gate_up, w_down):
    bp, lp, d = x_prompt.shape
    bs, ls_, _ = x_sample.shape
    n_p, n_s = bp * lp, bs * ls_
    n = n_p + n_s
    assert ls_ == CHUNK and n_s == TM and lp % TM == 0 and PAST_LEN % CHUNK == 0

    xp2, xs2 = x_prompt.reshape(n_p, d), x_sample.reshape(n_s, d)

    c16 = jnp.zeros((SEQ_ROWS, d), F32).at[:bp].set(c_prompt).at[bp:bp + bs].set(c_sample)
    mods = _ada_call(c16, w_ada, b_ada)
    seq_of_group = np.concatenate([np.repeat(np.arange(bp), lp // CHUNK), bp + np.arange(bs)])
    mods_g = [mods[l][seq_of_group] for l in range(DEPTH)]

    perm = np.array([4 * (c % 4) + c // 4 for c in range(N_EXPERTS)])
    wr = jnp.zeros((d, 128), F32).at[:, :N_EXPERTS].set(w_router[:, perm])
    br = jnp.zeros((1, 128), F32).at[0, :N_EXPERTS].set(b_router[perm])

    sc_unit = SC_WINDOW * SC_WORKERS * SC_GROUP
    n_pad = n + TMO
    while (DISP_SLABS * n_pad) % sc_unit or (Y_SLABS * n_pad) % TMO or (n_pad - n) % TMO:
        n_pad += TMO
    sort_rows = n + N_BUCKETS * TMO
    while (Y_SLABS * sort_rows) % sc_unit or sort_rows % (MOE_TILES * TMO):
        sort_rows += TMO

    gla_p = gla_s = bk_p = bv_p = bk_s = bv_s = sk_p = sv_p = sk_s = sv_s = None
    xn = z = tok = None
    for l in range(DEPTH):
        i = l // 2
        if l % 2 == 0:
            w = w_in_even[i]
            w_main = jnp.concatenate([w[:, :1536], w[:, 1552:]], axis=1).astype(BF16)
            w_la = jnp.zeros((d, 128), F32).at[:, :GATE_RANK].set(w[:, 1536:1552]).astype(BF16)
            w_gate = jnp.zeros((128, HA * DKA), F32).at[:GATE_RANK].set(w_gate_a[i])
            qa, ka, va, ra, qb, kb, vb, ga = _inproj_even_call(
                xp2, xs2, mods_g[l], norm_mix[l][None], w_main, w_la, w_gate, b_gate_a[i][None])
            xres = [xp2, xs2]
            gn = gla_norm[i][None]
            oa, s_p = _gla_call(qa, ka, va, ga, ra, jnp.zeros((bp, 256, 128), F32), gn, None,
                                n_seq=bp, seq_rows=lp, row0=0, nb=8)
            oa, s_s = _gla_call(qa, ka, va, ga, ra, state_gla[i].reshape(bs, 256, 128), gn, oa,
                                n_seq=bs, seq_rows=ls_, row0=n_p, nb=1)
            gla_p, gla_s = s_p.reshape(1, bp, HA, DKA, DVA), s_s.reshape(1, bs, HA, DKA, DVA)
            pb = N_PREV_B * CHUNK
            tq, g = 512, 2
            ck = cache_band_k[i].reshape(bs * pb, HB * DHB).astype(BF16)
            cv = cache_band_v[i].reshape(bs * pb, HB * DHB).astype(BF16)
            biases = (_band_bias(rel_bias_b[i], g, pb, _band_valid(g, pb)),
                      _band_bias(rel_bias_b[i], g, pb, _band_valid(g, pb, tq // (CHUNK * g))),
                      _band_bias(rel_bias_b[i], 1, pb, _band_valid(1, pb)))
            ob = _attention(_band_kernel, qb, kb, vb, ck, cv, biases, [], [], width=512, kv_width=512, pb=pb,
                            tq=tq, g=g, bp=bp, lp=lp, bs=bs, name="band")
            tail = lambda a: jnp.stack([a[(b + 1) * lp - pb:(b + 1) * lp] for b in range(bp)]).astype(F32).reshape(1, bp, pb, HB, DHB)
            new = lambda a: a[n_p:].astype(F32).reshape(bs, ls_, HB, DHB)
            bk_p, bv_p = tail(kb), tail(vb)
            bk_s = jnp.concatenate([cache_band_k[i][:, ls_:], new(kb)], axis=1)[None]
            bv_s = jnp.concatenate([cache_band_v[i][:, ls_:], new(vb)], axis=1)[None]
            wo = w_out_even[i].astype(BF16)
            os_, ws = [oa, ob], [wo[:HA * DVA], wo[HA * DVA:]]
        else:
            w = _after(w_in_odd[i], tok)
            w_out_l = _after(w_out_odd[i], tok)
            cache_k_l, cache_v_l = _after(cache_swa_k[i], tok), _after(cache_swa_v[i], tok)
            wk, wv = w[:, 1024:1152], w[:, 1152:1280]
            dup = lambda a: jnp.concatenate([a[:, :64], a[:, :64], a[:, 64:], a[:, 64:]], axis=1)
            w_all = jnp.concatenate([w[:, :1024], dup(wk), dup(wv)], axis=1).astype(BF16)
            cos, sin, rope_map = _rope_tables(lp, ls_, bp, bs)
            x, q, k, v = _inproj_odd_call(xn, z, mods_g[l - 1], mods_g[l], norm_mix[l][None], cos, sin, rope_map, w_all)
            xres = [x]
            pb = WINDOW
            tq, g = 512, 2
            sink = sinks_c[i][None] * LOG2E
            sink_spec = [pl.BlockSpec(memory_space=pltpu.SMEM)]
            dupc = lambda c: jnp.concatenate([c[:, :, 0], c[:, :, 0], c[:, :, 1], c[:, :, 1]], axis=-1).reshape(bs * pb, 256).astype(BF16)
            ck, cv = dupc(cache_k_l), dupc(cache_v_l)
            additive = lambda valid: jnp.asarray(np.where(valid, 0.0, -np.inf), F32)
            masks = (additive(_band_valid(g, pb)), additive(_band_valid(g, pb, tq // (CHUNK * g))),
                     additive(_band_valid(1, pb)))
            o = _attention(_swa_kernel, q, k, v, ck, cv, masks, [sink], sink_spec, width=1024, kv_width=256, pb=pb,
                           tq=tq, g=g, bp=bp, lp=lp, bs=bs, name="swa")
            undup = lambda a: jnp.concatenate([a[:, 0:64], a[:, 128:192]], axis=1).astype(F32)
            tail = lambda a: jnp.stack([undup(a[(b + 1) * lp - pb:(b + 1) * lp]) for b in range(bp)]).reshape(1, bp, pb, KVC, DHC)
            new = lambda a: undup(a[n_p:]).reshape(bs, ls_, KVC, DHC)
            sk_p, sv_p = tail(k), tail(v)
            sk_s = jnp.concatenate([cache_swa_k[i][:, ls_:], new(k)], axis=1)[None]
            sv_s = jnp.concatenate([cache_swa_v[i][:, ls_:], new(v)], axis=1)[None]
            os_, ws = [o], [w_out_l.astype(BF16)]
        xn, disp, meta, counts = _outproj_call(xres, os_, ws, mods_g[l], norm_ffn[l][None], wr, br, n_pad)
        z, tok = _moe_layer(disp, meta, counts, w_gate_up, w_down, l, n, n_pad, sort_rows)

    y_prompt, y_sample = _final_call(xn, z, mods_g[DEPTH - 1], norm_final[None], n_p)
    return (y_prompt.reshape(bp, lp, d), y_sample.reshape(bs, ls_, d),
            gla_p, gla_s, bk_p, bv_p, bk_s, bv_s, sk_p, sv_p, sk_s, sv_s)
```

```python
import functools

import numpy as np
import jax
import jax.numpy as jnp
from jax import lax
from jax.experimental import pallas as pl
from jax.experimental.pallas import tpu as pltpu
from jax.experimental.pallas import tpu_sc as plsc

F32 = jnp.float32
BF16 = jnp.bfloat16

D_MODEL = 1024
DEPTH = 2
CHUNK = 64
PAST_LEN = 4096
HA, DKA, DVA = 4, 64, 128
GATE_RANK = 16
GATE_TAU = 16.0
HB, DHB = 8, 64
N_PREV_B = 8
MAX_REL = 128
HC, KVC, DHC = 16, 2, 64
WINDOW = 128
ROPE_THETA = 10000.0
N_EXPERTS = 16
N_GROUPS = 4
EXP_PER_GROUP = 4
D_FF = 512
EPS = 1e-6

N_PAIRS = 6
N_BUCKETS = N_GROUPS * N_PAIRS
BUCKET_ROWS = 32
Y_SLABS = 4
DISP_SLABS = Y_SLABS + 1
TMO = 256
SC_WINDOW = 128
SC_WORKERS = 32
SC_GROUP = 3

TM = 512
SEQ_ROWS = 16
SUB = 16
LOG2E = 1.4426950408889634


def _cparams(sem, vmem_mb=48):
    return pltpu.CompilerParams(dimension_semantics=sem, vmem_limit_bytes=vmem_mb * 1024 * 1024)


def _dot(a, b):
    return jnp.dot(a, b, preferred_element_type=F32)


def _dot_nt(a, b):
    return lax.dot_general(a, b, (((1,), (1,)), ((), ())), preferred_element_type=F32)


def _split(a):
    hi = a.astype(BF16)
    lo = (a - hi.astype(F32)).astype(BF16)
    return hi, lo


def _dot3(a, b):
    ah, al = _split(a)
    bh, bl = _split(b)
    return _dot(ah, bh) + _dot(ah, bl) + _dot(al, bh)


def _dot3_narrow(a, b):
    ah, al = _split(a)
    bh, bl = _split(b)
    n = b.shape[1]
    p = _dot(ah, jnp.concatenate([bh, bl], axis=1))
    return p[:, :n] + p[:, n:] + _dot(al, bh)


def _sigmoid(x):
    return 1.0 / (1.0 + jnp.exp(-x))


def _group_affine(y, mul, add):
    parts = []
    for gi in range(y.shape[0] // CHUNK):
        p = y[gi * CHUNK:(gi + 1) * CHUNK]
        if mul is not None:
            p = p * mul[gi:gi + 1]
        if add is not None:
            p = p + add[gi:gi + 1]
        parts.append(p)
    return jnp.concatenate(parts, axis=0)


def _norm_mod(x, g, shift, scale):
    ms = jnp.mean(x * x, axis=-1, keepdims=True)
    return _group_affine(x * lax.rsqrt(ms + EPS) * g, 1.0 + scale, shift)


def _mod_spec(part):
    return pl.BlockSpec((TM // CHUNK, D_MODEL), lambda i: (i, part))


SHIFT_MIX, SCALE_MIX, GATE_MIX, SHIFT_FFN, SCALE_FFN, GATE_FFN = range(6)


def _on_token_tile(xp_ref, xs_ref, n_prompt_tiles, body):
    @pl.when(pl.program_id(0) < n_prompt_tiles)
    def _():
        body(xp_ref)

    @pl.when(pl.program_id(0) >= n_prompt_tiles)
    def _():
        body(xs_ref)


def _token_specs(n_prompt_tiles, d):
    return [pl.BlockSpec((TM, d), lambda i: (jnp.minimum(i, n_prompt_tiles - 1), 0)),
            pl.BlockSpec((TM, d), lambda i: (0, 0))]


def _ada_kernel(c_ref, w_ref, b_ref, o_ref):
    c = c_ref[...]
    o_ref[0] = _dot3(c * _sigmoid(c), w_ref[0]) + b_ref[0]


def _ada_call(c16, w_ada, b_ada):
    d = D_MODEL
    tn = 1024
    return pl.pallas_call(
        _ada_kernel,
        grid=(DEPTH, 6 * d // tn),
        in_specs=[pl.BlockSpec((SEQ_ROWS, d), lambda l, j: (0, 0)),
                  pl.BlockSpec((1, d, tn), lambda l, j: (l, 0, j)),
                  pl.BlockSpec((1, 1, tn), lambda l, j: (l, 0, j))],
        out_specs=pl.BlockSpec((1, SEQ_ROWS, tn), lambda l, j: (l, 0, j)),
        out_shape=jax.ShapeDtypeStruct((DEPTH, SEQ_ROWS, 6 * d), F32),
        compiler_params=_cparams(("arbitrary", "arbitrary")),
        name="ada",
    )(c16, w_ada, b_ada.reshape(DEPTH, 1, 6 * d))


def _inproj_even_kernel(xp_ref, xs_ref, sh_ref, sc_ref, g_ref, w_ref, wla_ref, wg_ref, bg_ref,
                        qa_ref, ka_ref, va_ref, ra_ref, qb_ref, kb_ref, vb_ref, ga_ref, *, n_prompt_tiles):
    def body(x_ref):
        t = x_ref.shape[0]
        outs = ((qa_ref, 0, 256, DKA ** -0.5), (ka_ref, 256, 512, None), (va_ref, 512, 1024, None),
                (ra_ref, 1024, 1536, None), (qb_ref, 1536, 2048, DHB ** -0.5 * LOG2E), (kb_ref, 2048, 2560, None),
                (vb_ref, 2560, 3072, None))
        shift, scale_ = sh_ref[...], sc_ref[...]
        halves = [slice(0, t // 2), slice(t // 2, t)]
        grp = [slice(0, t // (2 * CHUNK)), slice(t // (2 * CHUNK), t // CHUNK)]
        hbs = [_norm_mod(x_ref[rs, :], g_ref[...], shift[gs], scale_[gs]).astype(BF16) for rs, gs in zip(halves, grp)]
        for rs, hb in zip(halves, hbs):
            zs = [_dot(hb, w_ref[:, lo:hi]) for _, lo, hi, _ in outs]
            la = _dot(hb, wla_ref[...])
            for z, (o_ref, _, _, scale) in zip(zs, outs):
                o_ref[rs, :] = (z if scale is None else z * scale).astype(BF16)
            gl = _dot3(la, wg_ref[...]) + bg_ref[...]
            ga_ref[rs, :] = -(jnp.maximum(-gl, 0.0) + jnp.log(1.0 + jnp.exp(-jnp.abs(gl)))) * (1.0 / GATE_TAU)

    _on_token_tile(xp_ref, xs_ref, n_prompt_tiles, body)


def _inproj_even_call(xp, xs, mods, g, w_main, w_la, w_gate, b_gate):
    d = xp.shape[1]
    npt = xp.shape[0] // TM
    n = xp.shape[0] + xs.shape[0]
    row = lambda i: (i, 0)
    const = lambda i: (0, 0)
    widths = (256, 256, 512, 512, 512, 512, 512)
    out_shape = [jax.ShapeDtypeStruct((n, w), BF16) for w in widths] + [jax.ShapeDtypeStruct((n, 256), F32)]
    out_specs = [pl.BlockSpec((TM, w), row) for w in widths] + [pl.BlockSpec((TM, 256), row)]
    return pl.pallas_call(
        functools.partial(_inproj_even_kernel, n_prompt_tiles=npt),
        grid=(n // TM,),
        in_specs=_token_specs(npt, d) + [
            _mod_spec(SHIFT_MIX), _mod_spec(SCALE_MIX),
            pl.BlockSpec((1, d), const),
            pl.BlockSpec(w_main.shape, const), pl.BlockSpec(w_la.shape, const),
            pl.BlockSpec(w_gate.shape, const), pl.BlockSpec(b_gate.shape, const)],
        out_specs=out_specs, out_shape=out_shape,
        compiler_params=_cparams(("parallel",)),
        name="inproj_even",
    )(xp, xs, mods, mods, g, w_main, w_la, w_gate, b_gate)


def _rope(x, cos, sin_signed):
    t, w = x.shape
    lane = lax.broadcasted_iota(jnp.int32, (1, w), 1)
    first_half = (lane & 63) < 32
    rot = jnp.where(first_half, pltpu.roll(x, w - 32, 1), pltpu.roll(x, 32, 1))
    reps = w // 128
    return x * jnp.tile(cos, (1, reps)) + rot * jnp.tile(sin_signed, (1, reps))


def _unpack_pairs(slabs, dtype):
    lo = [pltpu.bitcast(s << 16, F32) for s in slabs]
    hi = [pltpu.bitcast(s & jnp.int32(-65536), F32) for s in slabs]
    return jnp.concatenate(lo + hi, axis=1).astype(dtype)


def _pack_pairs(x):
    bits = pltpu.bitcast(x.astype(BF16).astype(F32), jnp.int32)
    half = x.shape[1] // 2
    packed = ((bits[:, :half] >> 16) & jnp.int32(0xFFFF)) | (bits[:, half:] & jnp.int32(-65536))
    return [packed[:, 128 * s:128 * (s + 1)] for s in range(half // 128)]


def _add_moe(xn_ref, z_ref, gate_ref):
    y = _unpack_pairs([z_ref[s] for s in range(z_ref.shape[0])], F32)
    return xn_ref[...] + _group_affine(y, gate_ref[...], None)


def _rope_tables(lp, ls_, bp, bs):
    assert PAST_LEN + ls_ <= lp and lp % 128 == 0 and bs * ls_ == TM
    half = DHC // 2
    inv = ROPE_THETA ** (-jnp.arange(half, dtype=F32) / half)
    inv = jnp.tile(inv, 128 // half)
    sign = jnp.asarray(np.tile(np.repeat([-1.0, 1.0], half), 128 // DHC), F32)
    a = jnp.asarray(np.arange(lp // 128) * 128, F32)[:, None] * inv[None, :]
    b = jnp.asarray(np.arange(128), F32)[:, None] * inv[None, :]
    ca, sa, cb, sb = jnp.cos(a)[:, None], jnp.sin(a)[:, None], jnp.cos(b)[None], jnp.sin(b)[None]
    cos = (ca * cb - sa * sb).reshape(lp, 128)
    sin = ((sa * cb + ca * sb) * sign).reshape(lp, 128)
    with_sample = lambda t: jnp.concatenate([t, jnp.tile(t[PAST_LEN:PAST_LEN + ls_], (bs, 1))], axis=0)
    tiles = lp // TM
    return with_sample(cos), with_sample(sin), lambda i: (jnp.where(i < bp * tiles, i % tiles, tiles), 0)


def _inproj_odd_kernel(xn_ref, z_ref, gate_ref, sh_ref, sc_ref, g_ref, cos_ref, sin_ref, w_ref,
                       x_ref, q_ref, k_ref, v_ref):
    x = _add_moe(xn_ref, z_ref, gate_ref)
    x_ref[...] = x
    hb = _norm_mod(x, g_ref[...], sh_ref[...], sc_ref[...]).astype(BF16)
    cos, sin = cos_ref[...], sin_ref[...]
    q = _rope(_dot(hb, w_ref[:, 0:1024]), cos, sin)
    q_ref[...] = (q * (DHC ** -0.5 * LOG2E)).astype(BF16)
    k_ref[...] = _rope(_dot(hb, w_ref[:, 1024:1280]), cos, sin).astype(BF16)
    v_ref[...] = _dot(hb, w_ref[:, 1280:1536]).astype(BF16)


def _inproj_odd_call(xn, z, mods_prev, mods, g, cos, sin, rope_map, w):
    n, d = xn.shape
    row = lambda i: (i, 0)
    const = lambda i: (0, 0)
    widths = (1024, 256, 256)
    return pl.pallas_call(
        _inproj_odd_kernel,
        grid=(n // TM,),
        in_specs=[pl.BlockSpec((TM, d), row), pl.BlockSpec((z.shape[0], TM, 128), lambda i: (0, i, 0)),
                  _mod_spec(GATE_FFN), _mod_spec(SHIFT_MIX), _mod_spec(SCALE_MIX),
                  pl.BlockSpec((1, d), const),
                  pl.BlockSpec((TM, 128), rope_map), pl.BlockSpec((TM, 128), rope_map),
                  pl.BlockSpec(w.shape, const)],
        out_specs=[pl.BlockSpec((TM, d), row)] + [pl.BlockSpec((TM, wd), row) for wd in widths],
        out_shape=[jax.ShapeDtypeStruct((n, d), F32)] + [jax.ShapeDtypeStruct((n, wd), BF16) for wd in widths],
        compiler_params=_cparams(("parallel",)),
        name="inproj_odd",
    )(xn, z, mods_prev, mods, mods, g, cos, sin, w)


def _gla_tri():
    t = np.arange(CHUNK)[:, None]
    s = np.arange(CHUNK)[None, :]
    cum = s <= t
    start = s < (t // SUB) * SUB
    end = s < (t // SUB + 1) * SUB
    return jnp.asarray(np.concatenate([cum, start, end], axis=0).astype(np.float32), dtype=BF16)


def _gla_kernel(q_ref, k_ref, v_ref, g_ref, r_ref, s0_ref, gn_ref, tri_ref, o_ref, sout_ref, s_ref, *, nb):
    c_ = CHUNK
    nsub = c_ // SUB

    @pl.when(pl.program_id(1) == 0)
    def _():
        s_ref[...] = s0_ref[0]

    tri = tri_ref[...]
    lane = lax.broadcasted_iota(jnp.int32, (1, 128), 1)
    hmask = [jnp.where(lane < DKA, 1.0, 0.0), jnp.where(lane >= DKA, 1.0, 0.0)]
    ti = lax.broadcasted_iota(jnp.int32, (c_, c_), 0)
    si = lax.broadcasted_iota(jnp.int32, (c_, c_), 1)
    rb, cb = ti >> 4, si >> 4
    m_diag = (rb == cb) & (si <= ti)
    m_off = [(cb == j) & (rb > j) for j in range(nsub - 1)]
    hk = HA * DKA
    eye = lax.broadcasted_iota(jnp.int32, (hk, hk), 0) == lax.broadcasted_iota(jnp.int32, (hk, hk), 1)
    gn = gn_ref[...]

    chunks = range(nb)
    heads = [(p, hh) for p in range(HA // 2) for hh in range(2)]
    rows = [slice(c * c_, (c + 1) * c_) for c in chunks]
    pair = [slice(128 * p, 128 * (p + 1)) for p in range(HA // 2)]
    css = []
    for c in chunks:
        g_hi, g_lo = _split(g_ref[rows[c], :])
        css.append(_dot(tri, g_hi) + _dot(tri, g_lo))
    lhs1, lhs2, kds, kes, q_inter, klts, dcols = [], [], [], [], [], [], []
    for c in chunks:
        b, rs, re = css[c][0:c_], css[c][c_:2 * c_], css[c][2 * c_:3 * c_]
        q = q_ref[rows[c], :].astype(F32)
        k = k_ref[rows[c], :].astype(F32)
        bl = b[c_ - 1:c_, :]
        qd = q * jnp.exp(b - rs)
        kd = k * jnp.exp(rs - b)
        ke = k * jnp.exp(re - b)
        qi = q * jnp.exp(b)
        kl = k * jnp.exp(bl - b)
        ql = [q * jnp.exp(jnp.minimum(b - b[SUB * (j + 1) - 1:SUB * (j + 1), :], 0.0)) for j in range(nsub - 1)]
        dcols.append(jnp.sum(jnp.where(eye, jnp.broadcast_to(jnp.exp(bl), (hk, hk)), 0.0), axis=1, keepdims=True))
        kds.append([kd[:, ls].astype(BF16) for ls in pair])
        kes.append([ke[:, ls].astype(BF16) for ls in pair])
        klts.append([kl[:, ls].T.astype(BF16) for ls in pair])
        lhs1.append([(qd[:, pair[p]] * hmask[hh]).astype(BF16) for p, hh in heads])
        lhs2.append([jnp.concatenate([ql[j][:, pair[p]] * hmask[hh] for j in range(nsub - 1)], axis=0).astype(BF16)
                     for p, hh in heads])
        q_inter.append([(qi[:, pair[p]] * hmask[hh]).astype(BF16) for p, hh in heads])
    a1s = [[_dot_nt(lhs1[c][h], kds[c][p]) for h, (p, hh) in enumerate(heads)] for c in chunks]
    a2s = [[_dot_nt(lhs2[c][h], kes[c][p]) for h, (p, hh) in enumerate(heads)] for c in chunks]
    atts = []
    for c in chunks:
        per_head = []
        for h in range(HA):
            att = jnp.zeros((c_, c_), F32)
            for j in reversed(range(nsub - 1)):
                att = jnp.where(m_off[j], a2s[c][h][j * c_:(j + 1) * c_], att)
            per_head.append(jnp.where(m_diag, a1s[c][h], att).astype(BF16))
        atts.append(per_head)
    vs_ = [[v_ref[rows[c], DVA * h:DVA * (h + 1)] for h in range(HA)] for c in chunks]
    o_intra = [[_dot(atts[c][h], vs_[c][h]) for h in range(HA)] for c in chunks]
    upds = [jnp.concatenate([_dot(klts[c][p][DKA * hh:DKA * (hh + 1)], vs_[c][2 * p + hh]) for p, hh in heads], axis=0)
            for c in chunks]

    s_cur = s_ref[...]
    s_in = []
    for c in chunks:
        s_in.append(s_cur.astype(BF16))
        s_cur = dcols[c] * s_cur + upds[c]
    s_ref[...] = s_cur
    sout_ref[0] = s_cur

    for c in chunks:
        for h in range(HA):
            o = o_intra[c][h] + _dot(q_inter[c][h], s_in[c][pair[h // 2], :])
            ms = jnp.mean(o * o, axis=-1, keepdims=True)
            vs = slice(DVA * h, DVA * (h + 1))
            rr = r_ref[rows[c], vs].astype(F32)
            o_ref[rows[c], vs] = (o * lax.rsqrt(ms + EPS) * gn * (rr * _sigmoid(rr))).astype(BF16)


def _gla_call(q, k, v, g, r, s0, gn, o_prev, *, n_seq, seq_rows, row0, nb):
    tq = nb * CHUNK
    steps = seq_rows // tq
    blk0 = row0 // tq
    row = lambda b, j: (blk0 + b * steps + j, 0)
    const = lambda b, j: (0, 0)
    tri = _gla_tri()
    in_specs = [pl.BlockSpec((tq, 256), row), pl.BlockSpec((tq, 256), row), pl.BlockSpec((tq, 512), row),
                pl.BlockSpec((tq, 256), row), pl.BlockSpec((tq, 512), row),
                pl.BlockSpec((1, 256, 128), lambda b, j: (b, 0, 0)),
                pl.BlockSpec((1, 128), const), pl.BlockSpec(tri.shape, const)]
    args = [q, k, v, g, r, s0, gn, tri]
    aliases = {}
    if o_prev is not None:
        in_specs.append(pl.BlockSpec(memory_space=pl.ANY))
        args.append(o_prev)
        aliases = {len(args) - 1: 0}
    kern = functools.partial(_gla_kernel, nb=nb)
    if o_prev is not None:
        kern = _drop_arg(kern, 8)
    return pl.pallas_call(
        kern,
        grid=(n_seq, steps),
        in_specs=in_specs,
        out_specs=[pl.BlockSpec((tq, 512), row), pl.BlockSpec((1, 256, 128), lambda b, j: (b, 0, 0))],
        out_shape=[jax.ShapeDtypeStruct((q.shape[0], 512), BF16), jax.ShapeDtypeStruct((n_seq, 256, 128), F32)],
        scratch_shapes=[pltpu.VMEM((256, 128), F32)],
        input_output_aliases=aliases,
        compiler_params=_cparams(("arbitrary", "arbitrary")),
        name="gla",
    )(*args)


def _drop_arg(fn, idx):
    def wrapped(*refs):
        return fn(*refs[:idx], *refs[idx + 1:])
    return wrapped


def _window(prev_ref, cur_ref, lo, hi, pb, ls):
    if lo < pb:
        return jnp.concatenate([prev_ref[lo:pb, ls], cur_ref[0:hi - pb, ls]], axis=0)
    return cur_ref[lo - pb:hi - pb, ls]


def _band_kernel(q_ref, kp_ref, kc_ref, vp_ref, vc_ref, bias_ref, o_ref, *, g, n_sub, pb):
    qs = CHUNK * g
    kw_rows = pb + qs
    lane = lax.broadcasted_iota(jnp.int32, (1, 128), 1)
    low = lane < DHB
    hmask = [jnp.where(low, 1.0, 0.0), jnp.where(low, 0.0, 1.0)]
    for s in range(n_sub):
        sb = s if bias_ref.shape[0] > 1 else 0
        rows = slice(qs * s, qs * (s + 1))
        lanes = [slice(128 * p, 128 * (p + 1)) for p in range(HB // 2)]
        heads = [(p, hh) for p in range(HB // 2) for hh in range(2)]
        qps = [q_ref[rows, ls].astype(F32) for ls in lanes]
        kws = [_window(kp_ref, kc_ref, qs * s, qs * s + kw_rows, pb, ls) for ls in lanes]
        vws = [_window(vp_ref, vc_ref, qs * s, qs * s + kw_rows, pb, ls) for ls in lanes]
        scs = [_dot_nt((qps[p] * hmask[hh]).astype(BF16), kws[p]) + bias_ref[sb, 2 * p + hh] for p, hh in heads]
        pes = [jnp.exp2(sc - jnp.max(sc, axis=-1, keepdims=True)) for sc in scs]
        outs = [_dot(pe.astype(BF16), vws[p]) / jnp.sum(pe, axis=-1, keepdims=True) for pe, (p, hh) in zip(pes, heads)]
        for p, ls in enumerate(lanes):
            o_ref[rows, ls] = jnp.where(low, outs[2 * p], outs[2 * p + 1]).astype(BF16)


def _band_valid(g, pb, n_sub=None):
    rows, kw = CHUNK * g, pb + CHUNK * g
    r = np.arange(rows)[:, None]
    c = np.arange(kw)[None, :]
    dd = c // CHUNK - r // CHUNK
    band = (dd >= 0) & (dd <= pb // CHUNK)
    if n_sub is None:
        return band[None]
    return np.stack([band & (c >= pb - rows * s) for s in range(n_sub)])


def _band_bias(table, g, pb, valid):
    rows, kw = CHUNK * g, pb + CHUNK * g
    period = kw + rows
    m = np.arange(period)
    m = np.where(m < kw, m, m - period)
    ext = table[:, np.clip(m - pb, -MAX_REL, MAX_REL) + MAX_REL] * LOG2E
    flat = jnp.tile(ext, (1, rows))[:, :rows * (period - 1)]
    bias = flat.reshape(table.shape[0], rows, period - 1)[:, :, :kw]
    return jnp.where(valid[:, None], bias[None], -jnp.inf)


def _attn_call(kernel, q, kp, kc, vp, vc, extra, extra_specs, o_prev, *, width, kv_width, tq, pb,
               n_blocks, blk_map, prev_map, name):
    row = lambda i: (blk_map(i), 0)
    prev = lambda i: (prev_map(i), 0)
    in_specs = [pl.BlockSpec((tq, width), row),
                pl.BlockSpec((pb, kv_width), prev), pl.BlockSpec((tq, kv_width), row),
                pl.BlockSpec((pb, kv_width), prev), pl.BlockSpec((tq, kv_width), row)] + extra_specs
    args = [q, kp, kc, vp, vc] + extra
    aliases = {}
    if o_prev is not None:
        in_specs.append(pl.BlockSpec(memory_space=pl.ANY))
        args.append(o_prev)
        aliases = {len(args) - 1: 0}
        kernel = _drop_arg(kernel, len(args) - 1)
    return pl.pallas_call(
        kernel,
        grid=(n_blocks,),
        in_specs=in_specs,
        out_specs=pl.BlockSpec((tq, width), row),
        out_shape=jax.ShapeDtypeStruct((q.shape[0], width), BF16),
        input_output_aliases=aliases,
        compiler_params=_cparams(("parallel",)),
        name=name,
    )(*args)


def _attention(kernel_fn, q, k, v, cache_k, cache_v, masks, extra, extra_specs, *, width, kv_width, pb, tq, g,
               bp, lp, bs, name):
    bps = lp // tq
    n_sub = tq // (CHUNK * g)
    spec = lambda a: [pl.BlockSpec(a.shape, lambda i: (0,) * a.ndim)]
    kern = functools.partial(kernel_fn, g=g, n_sub=n_sub, pb=pb)
    common = dict(width=width, kv_width=kv_width, pb=pb)
    main = lambda i: (i // (bps - 1)) * bps + i % (bps - 1) + 1
    o = _attn_call(kern, q, k, k, v, v, [masks[0]] + extra, spec(masks[0]) + extra_specs, None, tq=tq,
                   n_blocks=bp * (bps - 1), blk_map=main, prev_map=lambda i: main(i) * (tq // pb) - 1,
                   name=name + "_main", **common)
    first = lambda i: i * bps
    o = _attn_call(kern, q, k, k, v, v, [masks[1]] + extra, spec(masks[1]) + extra_specs, o, tq=tq,
                   n_blocks=bp, blk_map=first, prev_map=lambda i: jnp.maximum(first(i) * (tq // pb) - 1, 0),
                   name=name + "_first", **common)
    samp = functools.partial(kernel_fn, g=1, n_sub=1, pb=pb)
    return _attn_call(samp, q, cache_k, k, cache_v, v, [masks[2]] + extra, spec(masks[2]) + extra_specs, o, tq=CHUNK,
                      n_blocks=bs, blk_map=lambda i: bp * lp // CHUNK + i, prev_map=lambda i: i,
                      name=name + "_sample", **common)


def _swa_kernel(q_ref, kp_ref, kc_ref, vp_ref, vc_ref, mask_ref, sink_ref, o_ref, *, g, n_sub, pb):
    qs = CHUNK * g
    kw_rows = pb + qs
    lane = lax.broadcasted_iota(jnp.int32, (1, 128), 1)
    low = lane < DHC
    hmask = [jnp.where(low, 1.0, 0.0), jnp.where(low, 0.0, 1.0)]
    pairs_per_kv = HC // KVC // 2
    for s in range(n_sub):
        msk = mask_ref[s if mask_ref.shape[0] > 1 else 0]
        rows = slice(qs * s, qs * (s + 1))
        kws = [_window(kp_ref, kc_ref, qs * s, qs * s + kw_rows, pb, slice(128 * kv, 128 * (kv + 1))) for kv in range(KVC)]
        vws = [_window(vp_ref, vc_ref, qs * s, qs * s + kw_rows, pb, slice(128 * kv, 128 * (kv + 1))) for kv in range(KVC)]
        heads = [(j, hh) for j in range(HC // 2) for hh in range(2)]
        qps = [q_ref[rows, 128 * j:128 * (j + 1)].astype(F32) for j in range(HC // 2)]
        scs = [_dot_nt((qps[j] * hmask[hh]).astype(BF16), kws[j // pairs_per_kv]) + msk for j, hh in heads]
        sks = [sink_ref[0, 2 * j + hh] for j, hh in heads]
        ms = [jnp.maximum(jnp.max(sc, axis=-1, keepdims=True), sk) for sc, sk in zip(scs, sks)]
        pes = [jnp.exp2(sc - m) for sc, m in zip(scs, ms)]
        outs = [_dot(pe.astype(BF16), vws[j // pairs_per_kv]) / (jnp.sum(pe, axis=-1, keepdims=True) + jnp.exp2(sk - m))
                for pe, sk, m, (j, hh) in zip(pes, sks, ms, heads)]
        for j in range(HC // 2):
            o_ref[rows, 128 * j:128 * (j + 1)] = jnp.where(low, outs[2 * j], outs[2 * j + 1]).astype(BF16)


def _route(logits_t):
    a = [logits_t[4 * j:4 * j + 4] for j in range(EXP_PER_GROUP)]

    def first_argmax(vals, m):
        idx = jnp.full(m.shape, float(len(vals) - 1), F32)
        for j in reversed(range(len(vals) - 1)):
            idx = jnp.where(vals[j] == m, float(j), idx)
        return idx

    m1 = functools.reduce(jnp.maximum, a)
    i1 = first_argmax(a, m1)
    bsec = [jnp.where(i1 == float(j), -jnp.inf, a[j]) for j in range(EXP_PER_GROUP)]
    m2 = functools.reduce(jnp.maximum, bsec)
    i2 = first_argmax(bsec, m2)
    rows = lambda x: [x[gi:gi + 1] for gi in range(N_GROUPS)]
    gm = functools.reduce(jnp.maximum, rows(m1))
    gscore = jnp.exp(m1 - gm) + jnp.exp(m2 - gm)
    gs = rows(gscore)
    gsel = first_argmax(gs, functools.reduce(jnp.maximum, gs))

    def pick(x):
        xr = rows(x)
        out = xr[N_GROUPS - 1]
        for gi in reversed(range(N_GROUPS - 1)):
            out = jnp.where(gsel == float(gi), xr[gi], out)
        return out

    p1 = jnp.exp(pick(m1) - gm)
    p2 = jnp.exp(pick(m2) - gm)
    w1 = p1 / (p1 + p2)
    w2 = p2 / (p1 + p2)
    s1, s2 = pick(i1), pick(i2)
    lo, hi = jnp.minimum(s1, s2), jnp.maximum(s1, s2)
    pair = jnp.where(lo == 0.0, hi - 1.0, jnp.where(lo == 1.0, hi + 1.0, 5.0))
    bucket = gsel * float(N_PAIRS) + pair
    first_is_lo = s1 < s2
    return bucket, jnp.where(first_is_lo, w1, w2), jnp.where(first_is_lo, w2, w1)


def _outproj_kernel(*refs, n_x, n_o, n_prompt_tiles):
    x_refs = refs[:n_x]
    o_refs = refs[n_x:n_x + n_o]
    w_refs = refs[n_x + n_o:n_x + 2 * n_o]
    (gate_ref, nf_ref, sh_ref, sc_ref, wr_ref, br_ref, tri_ref,
     xn_ref, disp_ref, meta_ref, cnt_ref, run_ref) = refs[n_x + 2 * n_o:]
    t = xn_ref.shape[0]

    @pl.when(pl.program_id(0) == 0)
    def _():
        run_ref[...] = jnp.zeros_like(run_ref)

    x_src = x_refs[0]
    if n_x == 2:
        def stage(x_ref):
            xn_ref[...] = x_ref[...]

        _on_token_tile(x_refs[0], x_refs[1], n_prompt_tiles, stage)
        x_src = xn_ref

    halves = [slice(0, t // 2), slice(t // 2, t)]
    grp = [slice(0, t // (2 * CHUNK)), slice(t // (2 * CHUNK), t // CHUNK)]
    ys = []
    for rs in halves:
        y = _dot(o_refs[0][rs, :], w_refs[0][...])
        for i in range(1, n_o):
            y = y + _dot(o_refs[i][rs, :], w_refs[i][...])
        ys.append(y)
    gate, shift, scale = gate_ref[...], sh_ref[...], sc_ref[...]
    gys = [_group_affine(y, gate[gs], None) for y, gs in zip(ys, grp)]

    for rs, gy in zip(halves, gys):
        xn_ref[rs, :] = x_src[rs, :] + gy
    hs = [_norm_mod(xn_ref[rs, :], nf_ref[...], shift[gs], scale[gs]) for rs, gs in zip(halves, grp)]
    for rs, h in zip(halves, hs):
        for s, slab in enumerate(_pack_pairs(h)):
            disp_ref[s, rs, :] = slab
    logits_t = [(_dot3_narrow(h, wr_ref[...]) + br_ref[...]).T[0:N_EXPERTS] for h in hs]
    bucket, w_lo, w_hi = _route(jnp.concatenate(logits_t, axis=1))
    r128 = lax.broadcasted_iota(jnp.int32, (128, t), 0)
    tok = (pl.program_id(0) * t + lax.broadcasted_iota(jnp.int32, (1, t), 1)).astype(F32)
    aux = jnp.where(r128 == 0, w_lo, jnp.where(r128 == 1, w_hi, jnp.where(r128 == 2, tok, 0.0))).T
    disp_ref[disp_ref.shape[0] - 1] = pltpu.bitcast(aux, jnp.int32)
    brow = lax.broadcasted_iota(jnp.int32, (BUCKET_ROWS, t), 0).astype(F32)
    onehot = jnp.where(brow == bucket, 1.0, 0.0)
    before = _dot(onehot.astype(BF16), tri_ref[...]) + run_ref[:, 0:1]
    rank = jnp.sum(onehot * before, axis=0, keepdims=True)
    run_ref[...] = run_ref[...] + jnp.sum(onehot, axis=1, keepdims=True)
    cnt_ref[...] = run_ref[...]
    r8 = lax.broadcasted_iota(jnp.int32, (8, t), 0)
    meta_ref[...] = jnp.where(r8 == 0, bucket, jnp.where(r8 == 1, rank, 0.0)).astype(jnp.int32)


def _outproj_call(xs_, os_, ws, mods, nf, wr, br, n_pad):
    d = xs_[0].shape[1]
    n = sum(a.shape[0] for a in xs_)
    npt = xs_[0].shape[0] // TM
    row = lambda i: (i, 0)
    const = lambda i: (0, 0)
    n_o = len(os_)
    in_specs = ((_token_specs(npt, d) if len(xs_) == 2 else [pl.BlockSpec((TM, d), row)])
                + [pl.BlockSpec((TM, o.shape[1]), row) for o in os_]
                + [pl.BlockSpec(w.shape, const) for w in ws]
                + [_mod_spec(GATE_MIX), pl.BlockSpec((1, d), const),
                   _mod_spec(SHIFT_FFN), _mod_spec(SCALE_FFN),
                   pl.BlockSpec(wr.shape, const), pl.BlockSpec(br.shape, const),
                   pl.BlockSpec((TM, TM), const)])
    tri = jnp.asarray(np.triu(np.ones((TM, TM), np.float32), k=1), dtype=BF16)
    return pl.pallas_call(
        functools.partial(_outproj_kernel, n_x=len(xs_), n_o=n_o, n_prompt_tiles=npt),
        grid=(n // TM,),
        in_specs=in_specs,
        out_specs=[pl.BlockSpec((TM, d), row), pl.BlockSpec((DISP_SLABS, TM, 128), lambda i: (0, i, 0)),
                   pl.BlockSpec((8, TM), lambda i: (0, i)), pl.BlockSpec((BUCKET_ROWS, 128), const)],
        out_shape=[jax.ShapeDtypeStruct((n, d), F32), jax.ShapeDtypeStruct((DISP_SLABS, n_pad, 128), jnp.int32),
                   jax.ShapeDtypeStruct((8, n), jnp.int32), jax.ShapeDtypeStruct((BUCKET_ROWS, 128), F32)],
        scratch_shapes=[pltpu.VMEM((BUCKET_ROWS, 128), F32)],
        compiler_params=_cparams(("arbitrary",)),
        name="outproj_router",
    )(*xs_, *os_, *ws, mods, nf, mods, mods, wr, br, tri)


def _sc_mesh():
    return plsc.VectorSubcoreMesh(core_axis_name="core", subcore_axis_name="subcore")


def _sc_row_copy(src, idx, n_out, scatter):
    r = idx.shape[0]
    k = SC_GROUP
    w_per = r // (SC_WINDOW * SC_WORKERS)
    assert r % (SC_WINDOW * SC_WORKERS) == 0 and w_per % k == 0
    n_groups = w_per // k

    @functools.partial(
        pl.kernel, out_type=jax.ShapeDtypeStruct((n_out, 128), src.dtype), mesh=_sc_mesh(),
        scratch_types=[pltpu.VMEM((w_per, SC_WINDOW), jnp.int32),
                       pltpu.VMEM((2 * k, SC_WINDOW, 128), src.dtype),
                       pltpu.SemaphoreType.DMA((2,)), pltpu.SemaphoreType.DMA((2,))])
    def copy(x_hbm, i_hbm, o_hbm, ibuf, xbuf, in_sem, out_sem):
        wid = lax.axis_index("core") * (SC_WORKERS // 2) + lax.axis_index("subcore")
        pltpu.sync_copy(i_hbm.at[wid], ibuf)
        first = wid * w_per

        def rows(j):
            return pl.ds((first + j) * SC_WINDOW, SC_WINDOW)

        def start_in(g, slot):
            cps = []
            for c in range(k):
                j = g * k + c
                s = x_hbm.at[rows(j)] if scatter else x_hbm.at[ibuf.at[j]]
                cps.append(pltpu.async_copy(s, xbuf.at[slot * k + c], in_sem.at[slot]))
            return cps

        def start_out(g, slot):
            cps = []
            for c in range(k):
                j = g * k + c
                dst = o_hbm.at[ibuf.at[j]] if scatter else o_hbm.at[rows(j)]
                cps.append(pltpu.async_copy(xbuf.at[slot * k + c], dst, out_sem.at[slot]))
            return cps

        pending_in = start_in(0, 0)
        for g in range(n_groups):
            slot = g % 2
            for cp in pending_in:
                cp.wait()
            pending_out = start_out(g, slot)
            if g + 1 < n_groups:
                pending_in = start_in(g + 1, 1 - slot)
            for cp in pending_out:
                cp.wait()

    return copy(src, idx.reshape(SC_WORKERS, w_per, SC_WINDOW))


def _sc_scatter_rows(src, idx, n_out):
    assert idx.shape == (src.shape[0],)
    return _sc_row_copy(src, idx, n_out, scatter=True)


def _moe_kernel(elo_ref, ehi_ref, nvalid_ref, fresh_ref, xs_ref, gu_lo_ref, gu_hi_ref, dn_lo_ref, dn_hi_ref,
                y_ref, tok_ref, wgu_ref, wdn_ref, *, n_tok, dump_tiles):
    i = pl.program_id(0)
    t = xs_ref.shape[1]

    @pl.when(fresh_ref[i] == 1)
    def _():
        for e, (gu_ref, dn_ref) in enumerate(((gu_lo_ref, dn_lo_ref), (gu_hi_ref, dn_hi_ref))):
            wgu_ref[e] = gu_ref[0, 0].astype(BF16)
            wdn_ref[e] = dn_ref[0, 0].astype(BF16)

    aux = pltpu.bitcast(xs_ref[Y_SLABS], F32)
    r = lax.broadcasted_iota(jnp.int32, (1, t), 1)
    spare = n_tok + (i % dump_tiles) * t + r
    tok = jnp.where(r < nvalid_ref[i], aux.T[2:3, :].astype(jnp.int32), spare)
    for c in range(t // 128):
        tok_ref[0, c:c + 1, :] = tok[:, 128 * c:128 * (c + 1)]

    @pl.when(nvalid_ref[i] > 0)
    def _():
        h = _unpack_pairs([xs_ref[s] for s in range(Y_SLABS)], BF16)
        abs_ = [_dot(h, wgu_ref[e]) for e in range(2)]
        acts = [(ab[:, :D_FF] * _sigmoid(ab[:, :D_FF]) * ab[:, D_FF:]).astype(BF16) for ab in abs_]
        ys = [_dot(act, wdn_ref[e]) for e, act in enumerate(acts)]
        acc = aux[:, 0:1] * ys[0] + aux[:, 1:2] * ys[1]
        for s, slab in enumerate(_pack_pairs(acc)):
            y_ref[s] = slab

    @pl.when(nvalid_ref[i] == 0)
    def _():
        y_ref[...] = jnp.zeros_like(y_ref)


def _moe_call(xs, elo, ehi, nvalid, fresh, w_gate_up, w_down, layer, n_tiles, n_tok, dump_tiles):
    d = w_gate_up.shape[2]
    gu = lambda sel: pl.BlockSpec((1, 1, d, 2 * D_FF), lambda i, lo, hi, v, f: (layer, (lo, hi)[sel][i], 0, 0))
    dn = lambda sel: pl.BlockSpec((1, 1, D_FF, d), lambda i, lo, hi, v, f: (layer, (lo, hi)[sel][i], 0, 0))
    return pl.pallas_call(
        functools.partial(_moe_kernel, n_tok=n_tok, dump_tiles=dump_tiles),
        grid_spec=pltpu.PrefetchScalarGridSpec(
            num_scalar_prefetch=4,
            grid=(n_tiles,),
            in_specs=[pl.BlockSpec((DISP_SLABS, TMO, 128), lambda i, lo, hi, v, f: (0, i, 0)),
                      gu(0), gu(1), dn(0), dn(1)],
            out_specs=[pl.BlockSpec((Y_SLABS, TMO, 128), lambda i, lo, hi, v, f: (0, i, 0)),
                       pl.BlockSpec((1, TMO // 128, 128), lambda i, lo, hi, v, f: (i, 0, 0))],
            scratch_shapes=[pltpu.VMEM((2, d, 2 * D_FF), BF16), pltpu.VMEM((2, D_FF, d), BF16)]),
        out_shape=[jax.ShapeDtypeStruct((Y_SLABS, n_tiles * TMO, 128), jnp.int32),
                   jax.ShapeDtypeStruct((n_tiles, TMO // 128, 128), jnp.int32)],
        compiler_params=_cparams(("arbitrary",)),
        name="moe_grouped",
    )(elo, ehi, nvalid, fresh, xs, w_gate_up, w_gate_up, w_down, w_down)


def _after(x, token):
    return lax.optimization_barrier((x, token))[0]


def _moe_layer(disp, meta, counts, w_gate_up, w_down, layer, n, n_pad, sort_rows):
    n_tiles = sort_rows // TMO
    cnt = counts[:N_BUCKETS, 0].astype(jnp.int32)
    padded = ((cnt + TMO - 1) // TMO) * TMO
    ends = jnp.cumsum(padded)
    offs = ends - padded
    bucket, rank = meta[0], meta[1]
    pos = rank + jnp.sum(jnp.where(bucket[None, :] == jnp.arange(N_BUCKETS, dtype=jnp.int32)[:, None],
                                   offs[:, None], 0), axis=0)
    tile_start = jnp.arange(n_tiles, dtype=jnp.int32) * TMO
    tile_bucket = jnp.minimum(jnp.sum((tile_start[:, None] >= ends[None, :]).astype(jnp.int32), axis=1), N_BUCKETS - 1)
    pair_lo = np.array([0, 0, 0, 1, 1, 2], np.int32)
    pair_hi = np.array([1, 2, 3, 2, 3, 3], np.int32)
    b_lo = jnp.asarray(np.repeat(np.arange(N_GROUPS), N_PAIRS) * EXP_PER_GROUP + np.tile(pair_lo, N_GROUPS), jnp.int32)
    b_hi = jnp.asarray(np.repeat(np.arange(N_GROUPS), N_PAIRS) * EXP_PER_GROUP + np.tile(pair_hi, N_GROUPS), jnp.int32)
    onehot_tb = (tile_bucket[:, None] == jnp.arange(N_BUCKETS, dtype=jnp.int32)[None, :]).astype(jnp.int32)
    elo = jnp.sum(onehot_tb * b_lo[None, :], axis=1)
    ehi = jnp.sum(onehot_tb * b_hi[None, :], axis=1)
    bucket_end = jnp.sum(onehot_tb * (offs + cnt)[None, :], axis=1)
    nvalid = jnp.where(tile_start < ends[-1], jnp.clip(bucket_end - tile_start, 0, TMO), 0)
    fresh = jnp.concatenate([jnp.ones((1,), jnp.int32), (tile_bucket[1:] != tile_bucket[:-1]).astype(jnp.int32)])
    dump = sort_rows + jnp.arange(n_pad - n, dtype=jnp.int32)
    pos_sc = jnp.concatenate([pos, dump])
    total = sort_rows + n_pad - n
    sc_idx = (pos_sc[None, :] + (jnp.arange(DISP_SLABS, dtype=jnp.int32) * total)[:, None]).reshape(-1)
    xs = _sc_scatter_rows(disp.reshape(DISP_SLABS * n_pad, 128), sc_idx, DISP_SLABS * total)
    ys, tok = _moe_call(xs.reshape(DISP_SLABS, total, 128), elo, ehi, nvalid, fresh, w_gate_up, w_down, layer,
                        n_tiles, n, (n_pad - n) // TMO)
    back_idx = (tok.reshape(1, sort_rows) + (jnp.arange(Y_SLABS, dtype=jnp.int32) * n_pad)[:, None]).reshape(-1)
    z = _sc_scatter_rows(ys.reshape(Y_SLABS * sort_rows, 128), back_idx, Y_SLABS * n_pad)
    return z.reshape(Y_SLABS, n_pad, 128), tok


def _final_kernel(xn_ref, z_ref, gate_ref, g_ref, yp_ref, ys_ref, *, n_prompt_tiles):
    x = _add_moe(xn_ref, z_ref, gate_ref)
    ms = jnp.mean(x * x, axis=-1, keepdims=True)
    y = x * lax.rsqrt(ms + EPS) * g_ref[...]
    i = pl.program_id(0)

    @pl.when(i < n_prompt_tiles)
    def _():
        yp_ref[...] = y

    @pl.when(i >= n_prompt_tiles)
    def _():
        ys_ref[...] = y


def _final_call(xn, z, mods, g, n_prompt):
    n, d = xn.shape
    npt = n_prompt // TM
    assert n - n_prompt == TM
    return pl.pallas_call(
        functools.partial(_final_kernel, n_prompt_tiles=npt),
        grid=(n // TM,),
        in_specs=[pl.BlockSpec((TM, d), lambda i: (i, 0)), pl.BlockSpec((z.shape[0], TM, 128), lambda i: (0, i, 0)),
                  _mod_spec(GATE_FFN), pl.BlockSpec((1, d), lambda i: (0, 0))],
        out_specs=_token_specs(npt, d),
        out_shape=[jax.ShapeDtypeStruct((n_prompt, d), F32), jax.ShapeDtypeStruct((TM, d), F32)],
        compiler_params=_cparams(("arbitrary",)),
        name="final_norm",
    )(xn, z, mods, g)


def kernel(x_prompt, x_sample, c_prompt, c_sample, state_gla, cache_band_k, cache_band_v, cache_swa_k, cache_swa_v,
           w_ada, b_ada, norm_mix, norm_ffn, norm_final, w_in_even, w_gate_a, b_gate_a, gla_norm, rel_bias_b,
           w_out_even, w_in_odd, sinks_c, w_out_odd, w_router, b_router, w_gate_up, w_down):
    bp, lp, d = x_prompt.shape
    bs, ls_, _ = x_sample.shape
    n_p, n_s = bp * lp, bs * ls_
    n = n_p + n_s
    assert ls_ == CHUNK and n_s == TM and lp % TM == 0 and PAST_LEN % CHUNK == 0

    xp2, xs2 = x_prompt.reshape(n_p, d), x_sample.reshape(n_s, d)

    c16 = jnp.zeros((SEQ_ROWS, d), F32).at[:bp].set(c_prompt).at[bp:bp + bs].set(c_sample)
    mods = _ada_call(c16, w_ada, b_ada)
    seq_of_group = np.concatenate([np.repeat(np.arange(bp), lp // CHUNK), bp + np.arange(bs)])
    mods_g = [mods[l][seq_of_group] for l in range(DEPTH)]

    perm = np.array([4 * (c % 4) + c // 4 for c in range(N_EXPERTS)])
    wr = jnp.zeros((d, 128), F32).at[:, :N_EXPERTS].set(w_router[:, perm])
    br = jnp.zeros((1, 128), F32).at[0, :N_EXPERTS].set(b_router[perm])

    sc_unit = SC_WINDOW * SC_WORKERS * SC_GROUP
    n_pad = n + TMO
    while (DISP_SLABS * n_pad) % sc_unit or (Y_SLABS * n_pad) % TMO or (n_pad - n) % TMO:
        n_pad += TMO
    sort_rows = n + N_BUCKETS * TMO
    while (Y_SLABS * sort_rows) % sc_unit:
        sort_rows += TMO

    gla_p = gla_s = bk_p = bv_p = bk_s = bv_s = sk_p = sv_p = sk_s = sv_s = None
    xn = z = tok = None
    for l in range(DEPTH):
        i = l // 2
        if l % 2 == 0:
            w = w_in_even[i]
            w_main = jnp.concatenate([w[:, :1536], w[:, 1552:]], axis=1).astype(BF16)
            w_la = jnp.zeros((d, 128), F32).at[:, :GATE_RANK].set(w[:, 1536:1552]).astype(BF16)
            w_gate = jnp.zeros((128, HA * DKA), F32).at[:GATE_RANK].set(w_gate_a[i])
            qa, ka, va, ra, qb, kb, vb, ga = _inproj_even_call(
                xp2, xs2, mods_g[l], norm_mix[l][None], w_main, w_la, w_gate, b_gate_a[i][None])
            xres = [xp2, xs2]
            gn = gla_norm[i][None]
            oa, s_p = _gla_call(qa, ka, va, ga, ra, jnp.zeros((bp, 256, 128), F32), gn, None,
                                n_seq=bp, seq_rows=lp, row0=0, nb=8)
            oa, s_s = _gla_call(qa, ka, va, ga, ra, state_gla[i].reshape(bs, 256, 128), gn, oa,
                                n_seq=bs, seq_rows=ls_, row0=n_p, nb=1)
            gla_p, gla_s = s_p.reshape(1, bp, HA, DKA, DVA), s_s.reshape(1, bs, HA, DKA, DVA)
            pb = N_PREV_B * CHUNK
            tq, g = 512, 2
            ck = cache_band_k[i].reshape(bs * pb, HB * DHB).astype(BF16)
            cv = cache_band_v[i].reshape(bs * pb, HB * DHB).astype(BF16)
            biases = (_band_bias(rel_bias_b[i], g, pb, _band_valid(g, pb)),
                      _band_bias(rel_bias_b[i], g, pb, _band_valid(g, pb, tq // (CHUNK * g))),
                      _band_bias(rel_bias_b[i], 1, pb, _band_valid(1, pb)))
            ob = _attention(_band_kernel, qb, kb, vb, ck, cv, biases, [], [], width=512, kv_width=512, pb=pb,
                            tq=tq, g=g, bp=bp, lp=lp, bs=bs, name="band")
            tail = lambda a: jnp.stack([a[(b + 1) * lp - pb:(b + 1) * lp] for b in range(bp)]).astype(F32).reshape(1, bp, pb, HB, DHB)
            new = lambda a: a[n_p:].astype(F32).reshape(bs, ls_, HB, DHB)
            bk_p, bv_p = tail(kb), tail(vb)
            bk_s = jnp.concatenate([cache_band_k[i][:, ls_:], new(kb)], axis=1)[None]
            bv_s = jnp.concatenate([cache_band_v[i][:, ls_:], new(vb)], axis=1)[None]
            wo = w_out_even[i].astype(BF16)
            os_, ws = [oa, ob], [wo[:HA * DVA], wo[HA * DVA:]]
        else:
            w = _after(w_in_odd[i], tok)
            w_out_l = _after(w_out_odd[i], tok)
            cache_k_l, cache_v_l = _after(cache_swa_k[i], tok), _after(cache_swa_v[i], tok)
            wk, wv = w[:, 1024:1152], w[:, 1152:1280]
            dup = lambda a: jnp.concatenate([a[:, :64], a[:, :64], a[:, 64:], a[:, 64:]], axis=1)
            w_all = jnp.concatenate([w[:, :1024], dup(wk), dup(wv)], axis=1).astype(BF16)
            cos, sin, rope_map = _rope_tables(lp, ls_, bp, bs)
            x, q, k, v = _inproj_odd_call(xn, z, mods_g[l - 1], mods_g[l], norm_mix[l][None], cos, sin, rope_map, w_all)
            xres = [x]
            pb = WINDOW
            tq, g = 512, 2
            sink = sinks_c[i][None] * LOG2E
            sink_spec = [pl.BlockSpec(memory_space=pltpu.SMEM)]
            dupc = lambda c: jnp.concatenate([c[:, :, 0], c[:, :, 0], c[:, :, 1], c[:, :, 1]], axis=-1).reshape(bs * pb, 256).astype(BF16)
            ck, cv = dupc(cache_k_l), dupc(cache_v_l)
            additive = lambda valid: jnp.asarray(np.where(valid, 0.0, -np.inf), F32)
            masks = (additive(_band_valid(g, pb)), additive(_band_valid(g, pb, tq // (CHUNK * g))),
                     additive(_band_valid(1, pb)))
            o = _attention(_swa_kernel, q, k, v, ck, cv, masks, [sink], sink_spec, width=1024, kv_width=256, pb=pb,
                           tq=tq, g=g, bp=bp, lp=lp, bs=bs, name="swa")
            undup = lambda a: jnp.concatenate([a[:, 0:64], a[:, 128:192]], axis=1).astype(F32)
            tail = lambda a: jnp.stack([undup(a[(b + 1) * lp - pb:(b + 1) * lp]) for b in range(bp)]).reshape(1, bp, pb, KVC, DHC)
            new = lambda a: undup(a[n_p:]).reshape(bs, ls_, KVC, DHC)
            sk_p, sv_p = tail(k), tail(v)
            sk_s = jnp.concatenate([cache_swa_k[i][:, ls_:], new(k)], axis=1)[None]
            sv_s = jnp.concatenate([cache_swa_v[i][:, ls_:], new(v)], axis=1)[None]
            os_, ws = [o], [w_out_l.astype(BF16)]
        xn, disp, meta, counts = _outproj_call(xres, os_, ws, mods_g[l], norm_ffn[l][None], wr, br, n_pad)
        z, tok = _moe_layer(disp, meta, counts, w_gate_up, w_down, l, n, n_pad, sort_rows)

    y_prompt, y_sample = _final_call(xn, z, mods_g[DEPTH - 1], norm_final[None], n_p)
    return (y_prompt.reshape(bp, lp, d), y_sample.reshape(bs, ls_, d),
            gla_p, gla_s, bk_p, bv_p, bk_s, bv_s, sk_p, sv_p, sk_s, sv_s)
```

```python
import functools

import numpy as np
import jax
import jax.numpy as jnp
from jax import lax
from jax.experimental import pallas as pl
from jax.experimental.pallas import tpu as pltpu
from jax.experimental.pallas import tpu_sc as plsc

F32 = jnp.float32
BF16 = jnp.bfloat16

D_MODEL = 1024
DEPTH = 2
CHUNK = 64
PAST_LEN = 4096
HA, DKA, DVA = 4, 64, 128
GATE_RANK = 16
GATE_TAU = 16.0
HB, DHB = 8, 64
N_PREV_B = 8
MAX_REL = 128
HC, KVC, DHC = 16, 2, 64
WINDOW = 128
ROPE_THETA = 10000.0
N_EXPERTS = 16
N_GROUPS = 4
EXP_PER_GROUP = 4
D_FF = 512
EPS = 1e-6

N_PAIRS = 6
N_BUCKETS = N_GROUPS * N_PAIRS
BUCKET_ROWS = 32
Y_SLABS = 4
DISP_SLABS = Y_SLABS + 1
TMO = 256
MOE_TILES = 2
SC_WINDOW = 128
SC_WORKERS = 32
SC_GROUP = 3

TM = 512
SEQ_ROWS = 16
SUB = 16
LOG2E = 1.4426950408889634
VMEM_LIMIT_MB = 48
VMEM_LIMIT_MOE_MB = 56


def _cparams(sem, vmem_mb=VMEM_LIMIT_MB):
    return pltpu.CompilerParams(dimension_semantics=sem, vmem_limit_bytes=vmem_mb * 1024 * 1024)


def _dot(a, b):
    return jnp.dot(a, b, preferred_element_type=F32)


def _dot_nt(a, b):
    return lax.dot_general(a, b, (((1,), (1,)), ((), ())), preferred_element_type=F32)


def _split(a):
    hi = a.astype(BF16)
    lo = (a - hi.astype(F32)).astype(BF16)
    return hi, lo


def _dot3(a, b):
    ah, al = _split(a)
    bh, bl = _split(b)
    return _dot(ah, bh) + _dot(ah, bl) + _dot(al, bh)


def _dot3_narrow(a, b):
    ah, al = _split(a)
    bh, bl = _split(b)
    n = b.shape[1]
    p = _dot(ah, jnp.concatenate([bh, bl], axis=1))
    return p[:, :n] + p[:, n:] + _dot(al, bh)


def _sigmoid(x):
    return 1.0 / (1.0 + jnp.exp(-x))


def _group_affine(y, mul, add):
    parts = []
    for gi in range(y.shape[0] // CHUNK):
        p = y[gi * CHUNK:(gi + 1) * CHUNK]
        if mul is not None:
            p = p * mul[gi:gi + 1]
        if add is not None:
            p = p + add[gi:gi + 1]
        parts.append(p)
    return jnp.concatenate(parts, axis=0)


def _norm_mod(x, g, shift, scale):
    ms = jnp.mean(x * x, axis=-1, keepdims=True)
    return _group_affine(x * lax.rsqrt(ms + EPS) * g, 1.0 + scale, shift)


def _mod_spec(part):
    return pl.BlockSpec((TM // CHUNK, D_MODEL), lambda i: (i, part))


SHIFT_MIX, SCALE_MIX, GATE_MIX, SHIFT_FFN, SCALE_FFN, GATE_FFN = range(6)


def _on_token_tile(xp_ref, xs_ref, n_prompt_tiles, body):
    @pl.when(pl.program_id(0) < n_prompt_tiles)
    def _():
        body(xp_ref)

    @pl.when(pl.program_id(0) >= n_prompt_tiles)
    def _():
        body(xs_ref)


def _token_specs(n_prompt_tiles, d):
    return [pl.BlockSpec((TM, d), lambda i: (jnp.minimum(i, n_prompt_tiles - 1), 0)),
            pl.BlockSpec((TM, d), lambda i: (0, 0))]


def _ada_kernel(c_ref, w_ref, b_ref, o_ref):
    c = c_ref[...]
    o_ref[0] = _dot3(c * _sigmoid(c), w_ref[0]) + b_ref[0]


def _ada_call(c16, w_ada, b_ada):
    d = D_MODEL
    tn = 1024
    return pl.pallas_call(
        _ada_kernel,
        grid=(DEPTH, 6 * d // tn),
        in_specs=[pl.BlockSpec((SEQ_ROWS, d), lambda l, j: (0, 0)),
                  pl.BlockSpec((1, d, tn), lambda l, j: (l, 0, j)),
                  pl.BlockSpec((1, 1, tn), lambda l, j: (l, 0, j))],
        out_specs=pl.BlockSpec((1, SEQ_ROWS, tn), lambda l, j: (l, 0, j)),
        out_shape=jax.ShapeDtypeStruct((DEPTH, SEQ_ROWS, 6 * d), F32),
        compiler_params=_cparams(("arbitrary", "arbitrary")),
        name="ada",
    )(c16, w_ada, b_ada.reshape(DEPTH, 1, 6 * d))


def _inproj_even_kernel(xp_ref, xs_ref, sh_ref, sc_ref, g_ref, w_ref, wla_ref, wg_ref, bg_ref,
                        qa_ref, ka_ref, va_ref, ra_ref, qb_ref, kb_ref, vb_ref, ga_ref, *, n_prompt_tiles):
    def body(x_ref):
        t = x_ref.shape[0]
        outs = ((qa_ref, 0, 256, DKA ** -0.5), (ka_ref, 256, 512, None), (va_ref, 512, 1024, None),
                (ra_ref, 1024, 1536, None), (qb_ref, 1536, 2048, DHB ** -0.5 * LOG2E), (kb_ref, 2048, 2560, None),
                (vb_ref, 2560, 3072, None))
        shift, scale_ = sh_ref[...], sc_ref[...]
        halves = [slice(0, t // 2), slice(t // 2, t)]
        grp = [slice(0, t // (2 * CHUNK)), slice(t // (2 * CHUNK), t // CHUNK)]
        hbs = [_norm_mod(x_ref[rs, :], g_ref[...], shift[gs], scale_[gs]).astype(BF16) for rs, gs in zip(halves, grp)]
        for rs, hb in zip(halves, hbs):
            zs = [_dot(hb, w_ref[:, lo:hi]) for _, lo, hi, _ in outs]
            la = _dot(hb, wla_ref[...])
            for z, (o_ref, _, _, scale) in zip(zs, outs):
                o_ref[rs, :] = (z if scale is None else z * scale).astype(BF16)
            gl = _dot3(la, wg_ref[...]) + bg_ref[...]
            ga_ref[rs, :] = -(jnp.maximum(-gl, 0.0) + jnp.log(1.0 + jnp.exp(-jnp.abs(gl)))) * (1.0 / GATE_TAU)

    _on_token_tile(xp_ref, xs_ref, n_prompt_tiles, body)


def _inproj_even_call(xp, xs, mods, g, w_main, w_la, w_gate, b_gate):
    d = xp.shape[1]
    npt = xp.shape[0] // TM
    n = xp.shape[0] + xs.shape[0]
    row = lambda i: (i, 0)
    const = lambda i: (0, 0)
    widths = (256, 256, 512, 512, 512, 512, 512)
    out_shape = [jax.ShapeDtypeStruct((n, w), BF16) for w in widths] + [jax.ShapeDtypeStruct((n, 256), F32)]
    out_specs = [pl.BlockSpec((TM, w), row) for w in widths] + [pl.BlockSpec((TM, 256), row)]
    return pl.pallas_call(
        functools.partial(_inproj_even_kernel, n_prompt_tiles=npt),
        grid=(n // TM,),
        in_specs=_token_specs(npt, d) + [
            _mod_spec(SHIFT_MIX), _mod_spec(SCALE_MIX),
            pl.BlockSpec((1, d), const),
            pl.BlockSpec(w_main.shape, const), pl.BlockSpec(w_la.shape, const),
            pl.BlockSpec(w_gate.shape, const), pl.BlockSpec(b_gate.shape, const)],
        out_specs=out_specs, out_shape=out_shape,
        compiler_params=_cparams(("parallel",)),
        name="inproj_even",
    )(xp, xs, mods, mods, g, w_main, w_la, w_gate, b_gate)


def _rope(x, cos, sin_signed):
    t, w = x.shape
    lane = lax.broadcasted_iota(jnp.int32, (1, w), 1)
    first_half = (lane & 63) < 32
    rot = jnp.where(first_half, pltpu.roll(x, w - 32, 1), pltpu.roll(x, 32, 1))
    reps = w // 128
    return x * jnp.tile(cos, (1, reps)) + rot * jnp.tile(sin_signed, (1, reps))


def _unpack_pairs(slabs, dtype):
    lo = [pltpu.bitcast(s << 16, F32) for s in slabs]
    hi = [pltpu.bitcast(s & jnp.int32(-65536), F32) for s in slabs]
    return jnp.concatenate(lo + hi, axis=1).astype(dtype)


def _pack_pairs(x):
    bits = pltpu.bitcast(x.astype(BF16).astype(F32), jnp.int32)
    half = x.shape[1] // 2
    packed = ((bits[:, :half] >> 16) & jnp.int32(0xFFFF)) | (bits[:, half:] & jnp.int32(-65536))
    return [packed[:, 128 * s:128 * (s + 1)] for s in range(half // 128)]


def _add_moe(xn_ref, z_ref, gate_ref):
    y = _unpack_pairs([z_ref[s] for s in range(z_ref.shape[0])], F32)
    return xn_ref[...] + _group_affine(y, gate_ref[...], None)


def _rope_tables(lp, ls_, bp, bs):
    assert PAST_LEN + ls_ <= lp and lp % 128 == 0 and bs * ls_ == TM
    half = DHC // 2
    inv = ROPE_THETA ** (-jnp.arange(half, dtype=F32) / half)
    inv = jnp.tile(inv, 128 // half)
    sign = jnp.asarray(np.tile(np.repeat([-1.0, 1.0], half), 128 // DHC), F32)
    a = jnp.asarray(np.arange(lp // 128) * 128, F32)[:, None] * inv[None, :]
    b = jnp.asarray(np.arange(128), F32)[:, None] * inv[None, :]
    ca, sa, cb, sb = jnp.cos(a)[:, None], jnp.sin(a)[:, None], jnp.cos(b)[None], jnp.sin(b)[None]
    cos = (ca * cb - sa * sb).reshape(lp, 128)
    sin = ((sa * cb + ca * sb) * sign).reshape(lp, 128)
    with_sample = lambda t: jnp.concatenate([t, jnp.tile(t[PAST_LEN:PAST_LEN + ls_], (bs, 1))], axis=0)
    tiles = lp // TM
    return with_sample(cos), with_sample(sin), lambda i: (jnp.where(i < bp * tiles, i % tiles, tiles), 0)


def _inproj_odd_kernel(xn_ref, z_ref, gate_ref, sh_ref, sc_ref, g_ref, cos_ref, sin_ref, w_ref,
                       x_ref, q_ref, k_ref, v_ref):
    t = xn_ref.shape[0]
    halves = [slice(0, t // 2), slice(t // 2, t)]
    grp = [slice(0, t // (2 * CHUNK)), slice(t // (2 * CHUNK), t // CHUNK)]
    gate, shift, scale = gate_ref[...], sh_ref[...], sc_ref[...]
    xs = []
    for rs, gs in zip(halves, grp):
        y = _unpack_pairs([z_ref[s, rs, :] for s in range(z_ref.shape[0])], F32)
        xs.append(xn_ref[rs, :] + _group_affine(y, gate[gs], None))
    for rs, x in zip(halves, xs):
        x_ref[rs, :] = x
    hbs = [_norm_mod(x, g_ref[...], shift[gs], scale[gs]).astype(BF16) for x, gs in zip(xs, grp)]
    qs = [_dot(hb, w_ref[:, 0:1024]) for hb in hbs]
    ks = [_dot(hb, w_ref[:, 1024:1280]) for hb in hbs]
    vs = [_dot(hb, w_ref[:, 1280:1536]) for hb in hbs]
    for rs, q, k, v in zip(halves, qs, ks, vs):
        cos, sin = cos_ref[rs, :], sin_ref[rs, :]
        q_ref[rs, :] = (_rope(q, cos, sin) * (DHC ** -0.5 * LOG2E)).astype(BF16)
        k_ref[rs, :] = _rope(k, cos, sin).astype(BF16)
        v_ref[rs, :] = v.astype(BF16)


def _inproj_odd_call(xn, z, mods_prev, mods, g, cos, sin, rope_map, w):
    n, d = xn.shape
    row = lambda i: (i, 0)
    const = lambda i: (0, 0)
    widths = (1024, 256, 256)
    return pl.pallas_call(
        _inproj_odd_kernel,
        grid=(n // TM,),
        in_specs=[pl.BlockSpec((TM, d), row), pl.BlockSpec((z.shape[0], TM, 128), lambda i: (0, i, 0)),
                  _mod_spec(GATE_FFN), _mod_spec(SHIFT_MIX), _mod_spec(SCALE_MIX),
                  pl.BlockSpec((1, d), const),
                  pl.BlockSpec((TM, 128), rope_map), pl.BlockSpec((TM, 128), rope_map),
                  pl.BlockSpec(w.shape, const)],
        out_specs=[pl.BlockSpec((TM, d), row)] + [pl.BlockSpec((TM, wd), row) for wd in widths],
        out_shape=[jax.ShapeDtypeStruct((n, d), F32)] + [jax.ShapeDtypeStruct((n, wd), BF16) for wd in widths],
        compiler_params=_cparams(("parallel",)),
        name="inproj_odd",
    )(xn, z, mods_prev, mods, mods, g, cos, sin, w)


def _gla_tri():
    t = np.arange(CHUNK)[:, None]
    s = np.arange(CHUNK)[None, :]
    cum = s <= t
    start = s < (t // SUB) * SUB
    end = s < (t // SUB + 1) * SUB
    return jnp.asarray(np.concatenate([cum, start, end], axis=0).astype(np.float32), dtype=BF16)


def _gla_kernel(q_ref, k_ref, v_ref, g_ref, r_ref, s0_ref, gn_ref, tri_ref, o_ref, sout_ref, s_ref, *, nb):
    c_ = CHUNK
    nsub = c_ // SUB

    @pl.when(pl.program_id(1) == 0)
    def _():
        s_ref[...] = s0_ref[0]

    tri = tri_ref[...]
    lane = lax.broadcasted_iota(jnp.int32, (1, 128), 1)
    hmask = [jnp.where(lane < DKA, 1.0, 0.0), jnp.where(lane >= DKA, 1.0, 0.0)]
    ti = lax.broadcasted_iota(jnp.int32, (c_, c_), 0)
    si = lax.broadcasted_iota(jnp.int32, (c_, c_), 1)
    rb, cb = ti >> 4, si >> 4
    m_diag = (rb == cb) & (si <= ti)
    m_off = [(cb == j) & (rb > j) for j in range(nsub - 1)]
    hk = HA * DKA
    gn = gn_ref[...]

    chunks = range(nb)
    heads = [(p, hh) for p in range(HA // 2) for hh in range(2)]
    rows = [slice(c * c_, (c + 1) * c_) for c in chunks]
    pair = [slice(128 * p, 128 * (p + 1)) for p in range(HA // 2)]
    css = []
    for c in chunks:
        g_hi, g_lo = _split(g_ref[rows[c], :])
        css.append(_dot(tri, g_hi) + _dot(tri, g_lo))
    lhs1, lhs2, kds, kes, q_inter, klts, dcols = [], [], [], [], [], [], []
    for c in chunks:
        b, rs, re = css[c][0:c_], css[c][c_:2 * c_], css[c][2 * c_:3 * c_]
        q = q_ref[rows[c], :].astype(F32)
        k = k_ref[rows[c], :].astype(F32)
        bl = b[c_ - 1:c_, :]
        qd = q * jnp.exp(b - rs)
        kd = k * jnp.exp(rs - b)
        ke = k * jnp.exp(re - b)
        qi = q * jnp.exp(b)
        kl = k * jnp.exp(bl - b)
        ql = [q * jnp.exp(jnp.minimum(b - b[SUB * (j + 1) - 1:SUB * (j + 1), :], 0.0)) for j in range(nsub - 1)]
        dcols.append(jnp.broadcast_to(jnp.exp(bl), (8, hk)).T[:, 0:1])
        kds.append([(kd[:, pair[p]] * hmask[hh]).astype(BF16) for p, hh in heads])
        kes.append([(ke[:, pair[p]] * hmask[hh]).astype(BF16) for p, hh in heads])
        klts.append([kl[:, ls].T.astype(BF16) for ls in pair])
        lhs1.append([qd[:, ls].astype(BF16) for ls in pair])
        lhs2.append([jnp.concatenate([ql[j][:, ls] for j in range(nsub - 1)], axis=0).astype(BF16) for ls in pair])
        q_inter.append([(qi[:, pair[p]] * hmask[hh]).astype(BF16) for p, hh in heads])
    a1s = [[_dot_nt(lhs1[c][p], kds[c][h]) for h, (p, hh) in enumerate(heads)] for c in chunks]
    a2s = [[_dot_nt(lhs2[c][p], kes[c][h]) for h, (p, hh) in enumerate(heads)] for c in chunks]
    atts = []
    for c in chunks:
        per_head = []
        for h in range(HA):
            att = jnp.zeros((c_, c_), F32)
            for j in reversed(range(nsub - 1)):
                att = jnp.where(m_off[j], a2s[c][h][j * c_:(j + 1) * c_], att)
            per_head.append(jnp.where(m_diag, a1s[c][h], att).astype(BF16))
        atts.append(per_head)
    vs_ = [[v_ref[rows[c], DVA * h:DVA * (h + 1)] for h in range(HA)] for c in chunks]
    o_intra = [[_dot(atts[c][h], vs_[c][h]) for h in range(HA)] for c in chunks]
    upds = [jnp.concatenate([_dot(klts[c][p][DKA * hh:DKA * (hh + 1)], vs_[c][2 * p + hh]) for p, hh in heads], axis=0)
            for c in chunks]

    s_cur = s_ref[...]
    s_in = []
    for c in chunks:
        s_in.append(s_cur.astype(BF16))
        s_cur = dcols[c] * s_cur + upds[c]
    s_ref[...] = s_cur
    sout_ref[0] = s_cur

    for c in chunks:
        for h in range(HA):
            o = o_intra[c][h] + _dot(q_inter[c][h], s_in[c][pair[h // 2], :])
            ms = jnp.mean(o * o, axis=-1, keepdims=True)
            vs = slice(DVA * h, DVA * (h + 1))
            rr = r_ref[rows[c], vs].astype(F32)
            o_ref[rows[c], vs] = (o * lax.rsqrt(ms + EPS) * gn * (rr * _sigmoid(rr))).astype(BF16)


def _gla_call(q, k, v, g, r, s0, gn, o_prev, *, n_seq, seq_rows, row0, nb):
    tq = nb * CHUNK
    steps = seq_rows // tq
    blk0 = row0 // tq
    row = lambda b, j: (blk0 + b * steps + j, 0)
    const = lambda b, j: (0, 0)
    tri = _gla_tri()
    in_specs = [pl.BlockSpec((tq, 256), row), pl.BlockSpec((tq, 256), row), pl.BlockSpec((tq, 512), row),
                pl.BlockSpec((tq, 256), row), pl.BlockSpec((tq, 512), row),
                pl.BlockSpec((1, 256, 128), lambda b, j: (b, 0, 0)),
                pl.BlockSpec((1, 128), const), pl.BlockSpec(tri.shape, const)]
    args = [q, k, v, g, r, s0, gn, tri]
    aliases = {}
    if o_prev is not None:
        in_specs.append(pl.BlockSpec(memory_space=pl.ANY))
        args.append(o_prev)
        aliases = {len(args) - 1: 0}
    kern = functools.partial(_gla_kernel, nb=nb)
    if o_prev is not None:
        kern = _drop_arg(kern, 8)
    return pl.pallas_call(
        kern,
        grid=(n_seq, steps),
        in_specs=in_specs,
        out_specs=[pl.BlockSpec((tq, 512), row), pl.BlockSpec((1, 256, 128), lambda b, j: (b, 0, 0))],
        out_shape=[jax.ShapeDtypeStruct((q.shape[0], 512), BF16), jax.ShapeDtypeStruct((n_seq, 256, 128), F32)],
        scratch_shapes=[pltpu.VMEM((256, 128), F32)],
        input_output_aliases=aliases,
        compiler_params=_cparams(("arbitrary", "arbitrary")),
        name="gla",
    )(*args)


def _drop_arg(fn, idx):
    def wrapped(*refs):
        return fn(*refs[:idx], *refs[idx + 1:])
    return wrapped


def _window(prev_ref, cur_ref, lo, hi, pb, ls):
    if lo < pb:
        return jnp.concatenate([prev_ref[lo:pb, ls], cur_ref[0:hi - pb, ls]], axis=0)
    return cur_ref[lo - pb:hi - pb, ls]


def _band_kernel(q_ref, kp_ref, kc_ref, vp_ref, vc_ref, bias_ref, o_ref, *, g, n_sub, pb):
    qs = CHUNK * g
    kw_rows = pb + qs
    lane = lax.broadcasted_iota(jnp.int32, (1, 128), 1)
    low = lane < DHB
    hmask = [jnp.where(low, 1.0, 0.0), jnp.where(low, 0.0, 1.0)]
    for s in range(n_sub):
        sb = s if bias_ref.shape[0] > 1 else 0
        rows = slice(qs * s, qs * (s + 1))
        lanes = [slice(128 * p, 128 * (p + 1)) for p in range(HB // 2)]
        heads = [(p, hh) for p in range(HB // 2) for hh in range(2)]
        qps = [q_ref[rows, ls].astype(F32) for ls in lanes]
        kws = [_window(kp_ref, kc_ref, qs * s, qs * s + kw_rows, pb, ls) for ls in lanes]
        vws = [_window(vp_ref, vc_ref, qs * s, qs * s + kw_rows, pb, ls) for ls in lanes]
        scs = [_dot_nt((qps[p] * hmask[hh]).astype(BF16), kws[p]) + bias_ref[sb, 2 * p + hh] for p, hh in heads]
        pes = [jnp.exp2(sc - jnp.max(sc, axis=-1, keepdims=True)) for sc in scs]
        outs = [_dot(pe.astype(BF16), vws[p]) / jnp.sum(pe, axis=-1, keepdims=True) for pe, (p, hh) in zip(pes, heads)]
        for p, ls in enumerate(lanes):
            o_ref[rows, ls] = jnp.where(low, outs[2 * p], outs[2 * p + 1]).astype(BF16)


def _band_valid(g, pb, n_sub=None):
    rows, kw = CHUNK * g, pb + CHUNK * g
    r = np.arange(rows)[:, None]
    c = np.arange(kw)[None, :]
    dd = c // CHUNK - r // CHUNK
    band = (dd >= 0) & (dd <= pb // CHUNK)
    if n_sub is None:
        return band[None]
    return np.stack([band & (c >= pb - rows * s) for s in range(n_sub)])


def _band_bias(table, g, pb, valid):
    rows, kw = CHUNK * g, pb + CHUNK * g
    period = kw + rows
    m = np.arange(period)
    m = np.where(m < kw, m, m - period)
    ext = table[:, np.clip(m - pb, -MAX_REL, MAX_REL) + MAX_REL] * LOG2E
    flat = jnp.tile(ext, (1, rows))[:, :rows * (period - 1)]
    bias = flat.reshape(table.shape[0], rows, period - 1)[:, :, :kw]
    return jnp.where(valid[:, None], bias[None], -jnp.inf)


def _attn_call(kernel, q, kp, kc, vp, vc, extra, extra_specs, o_prev, *, width, kv_width, tq, pb,
               n_blocks, blk_map, prev_map, name):
    row = lambda i: (blk_map(i), 0)
    prev = lambda i: (prev_map(i), 0)
    in_specs = [pl.BlockSpec((tq, width), row),
                pl.BlockSpec((pb, kv_width), prev), pl.BlockSpec((tq, kv_width), row),
                pl.BlockSpec((pb, kv_width), prev), pl.BlockSpec((tq, kv_width), row)] + extra_specs
    args = [q, kp, kc, vp, vc] + extra
    aliases = {}
    if o_prev is not None:
        in_specs.append(pl.BlockSpec(memory_space=pl.ANY))
        args.append(o_prev)
        aliases = {len(args) - 1: 0}
        kernel = _drop_arg(kernel, len(args) - 1)
    return pl.pallas_call(
        kernel,
        grid=(n_blocks,),
        in_specs=in_specs,
        out_specs=pl.BlockSpec((tq, width), row),
        out_shape=jax.ShapeDtypeStruct((q.shape[0], width), BF16),
        input_output_aliases=aliases,
        compiler_params=_cparams(("parallel",)),
        name=name,
    )(*args)


def _attention(kernel_fn, q, k, v, cache_k, cache_v, masks, extra, extra_specs, *, width, kv_width, pb, tq, g,
               bp, lp, bs, name):
    bps = lp // tq
    n_sub = tq // (CHUNK * g)
    spec = lambda a: [pl.BlockSpec(a.shape, lambda i: (0,) * a.ndim)]
    kern = functools.partial(kernel_fn, g=g, n_sub=n_sub, pb=pb)
    common = dict(width=width, kv_width=kv_width, pb=pb)
    main = lambda i: (i // (bps - 1)) * bps + i % (bps - 1) + 1
    o = _attn_call(kern, q, k, k, v, v, [masks[0]] + extra, spec(masks[0]) + extra_specs, None, tq=tq,
                   n_blocks=bp * (bps - 1), blk_map=main, prev_map=lambda i: main(i) * (tq // pb) - 1,
                   name=name + "_main", **common)
    first = lambda i: i * bps
    o = _attn_call(kern, q, k, k, v, v, [masks[1]] + extra, spec(masks[1]) + extra_specs, o, tq=tq,
                   n_blocks=bp, blk_map=first, prev_map=lambda i: jnp.maximum(first(i) * (tq // pb) - 1, 0),
                   name=name + "_first", **common)
    samp = functools.partial(kernel_fn, g=1, n_sub=1, pb=pb)
    return _attn_call(samp, q, cache_k, k, cache_v, v, [masks[2]] + extra, spec(masks[2]) + extra_specs, o, tq=CHUNK,
                      n_blocks=bs, blk_map=lambda i: bp * lp // CHUNK + i, prev_map=lambda i: i,
                      name=name + "_sample", **common)


def _swa_kernel(q_ref, kp_ref, kc_ref, vp_ref, vc_ref, mask_ref, sink_ref, o_ref, *, g, n_sub, pb):
    qs = CHUNK * g
    kw_rows = pb + qs
    lane = lax.broadcasted_iota(jnp.int32, (1, 128), 1)
    low = lane < DHC
    hmask = [jnp.where(low, 1.0, 0.0), jnp.where(low, 0.0, 1.0)]
    pairs_per_kv = HC // KVC // 2
    for s in range(n_sub):
        msk = mask_ref[s if mask_ref.shape[0] > 1 else 0]
        rows = slice(qs * s, qs * (s + 1))
        kws = [_window(kp_ref, kc_ref, qs * s, qs * s + kw_rows, pb, slice(128 * kv, 128 * (kv + 1))) for kv in range(KVC)]
        vws = [_window(vp_ref, vc_ref, qs * s, qs * s + kw_rows, pb, slice(128 * kv, 128 * (kv + 1))) for kv in range(KVC)]
        heads = [(j, hh) for j in range(HC // 2) for hh in range(2)]
        qps = [q_ref[rows, 128 * j:128 * (j + 1)].astype(F32) for j in range(HC // 2)]
        scs = [_dot_nt((qps[j] * hmask[hh]).astype(BF16), kws[j // pairs_per_kv]) + msk for j, hh in heads]
        sks = [sink_ref[0, 2 * j + hh] for j, hh in heads]
        ms = [jnp.maximum(jnp.max(sc, axis=-1, keepdims=True), sk) for sc, sk in zip(scs, sks)]
        pes = [jnp.exp2(sc - m) for sc, m in zip(scs, ms)]
        outs = [_dot(pe.astype(BF16), vws[j // pairs_per_kv]) / (jnp.sum(pe, axis=-1, keepdims=True) + jnp.exp2(sk - m))
                for pe, sk, m, (j, hh) in zip(pes, sks, ms, heads)]
        for j in range(HC // 2):
            o_ref[rows, 128 * j:128 * (j + 1)] = jnp.where(low, outs[2 * j], outs[2 * j + 1]).astype(BF16)


def _route(logits_t):
    a = [logits_t[4 * j:4 * j + 4] for j in range(EXP_PER_GROUP)]

    def first_argmax(vals, m):
        idx = jnp.full(m.shape, float(len(vals) - 1), F32)
        for j in reversed(range(len(vals) - 1)):
            idx = jnp.where(vals[j] == m, float(j), idx)
        return idx

    m1 = functools.reduce(jnp.maximum, a)
    i1 = first_argmax(a, m1)
    bsec = [jnp.where(i1 == float(j), -jnp.inf, a[j]) for j in range(EXP_PER_GROUP)]
    m2 = functools.reduce(jnp.maximum, bsec)
    i2 = first_argmax(bsec, m2)
    rows = lambda x: [x[gi:gi + 1] for gi in range(N_GROUPS)]
    gm = functools.reduce(jnp.maximum, rows(m1))
    gscore = jnp.exp(m1 - gm) + jnp.exp(m2 - gm)
    gs = rows(gscore)
    gsel = first_argmax(gs, functools.reduce(jnp.maximum, gs))

    def pick(x):
        xr = rows(x)
        out = xr[N_GROUPS - 1]
        for gi in reversed(range(N_GROUPS - 1)):
            out = jnp.where(gsel == float(gi), xr[gi], out)
        return out

    p1 = jnp.exp(pick(m1) - gm)
    p2 = jnp.exp(pick(m2) - gm)
    w1 = p1 / (p1 + p2)
    w2 = p2 / (p1 + p2)
    s1, s2 = pick(i1), pick(i2)
    lo, hi = jnp.minimum(s1, s2), jnp.maximum(s1, s2)
    pair = jnp.where(lo == 0.0, hi - 1.0, jnp.where(lo == 1.0, hi + 1.0, 5.0))
    bucket = gsel * float(N_PAIRS) + pair
    first_is_lo = s1 < s2
    return bucket, jnp.where(first_is_lo, w1, w2), jnp.where(first_is_lo, w2, w1)


def _outproj_kernel(*refs, n_x, n_o, n_prompt_tiles):
    x_refs = refs[:n_x]
    o_refs = refs[n_x:n_x + n_o]
    w_refs = refs[n_x + n_o:n_x + 2 * n_o]
    (gate_ref, nf_ref, sh_ref, sc_ref, wr_ref, br_ref, tri_ref,
     xn_ref, disp_ref, meta_ref, cnt_ref, run_ref) = refs[n_x + 2 * n_o:]
    t = xn_ref.shape[0]

    @pl.when(pl.program_id(0) == 0)
    def _():
        run_ref[...] = jnp.zeros_like(run_ref)

    x_src = x_refs[0]
    if n_x == 2:
        def stage(x_ref):
            xn_ref[...] = x_ref[...]

        _on_token_tile(x_refs[0], x_refs[1], n_prompt_tiles, stage)
        x_src = xn_ref

    halves = [slice(0, t // 2), slice(t // 2, t)]
    grp = [slice(0, t // (2 * CHUNK)), slice(t // (2 * CHUNK), t // CHUNK)]
    ys = []
    for rs in halves:
        y = _dot(o_refs[0][rs, :], w_refs[0][...])
        for i in range(1, n_o):
            y = y + _dot(o_refs[i][rs, :], w_refs[i][...])
        ys.append(y)
    gate, shift, scale = gate_ref[...], sh_ref[...], sc_ref[...]
    gys = [_group_affine(y, gate[gs], None) for y, gs in zip(ys, grp)]

    for rs, gy in zip(halves, gys):
        xn_ref[rs, :] = x_src[rs, :] + gy
    hs = [_norm_mod(xn_ref[rs, :], nf_ref[...], shift[gs], scale[gs]) for rs, gs in zip(halves, grp)]
    for rs, h in zip(halves, hs):
        for s, slab in enumerate(_pack_pairs(h)):
            disp_ref[s, rs, :] = slab
    logits_t = [(_dot3_narrow(h, wr_ref[...]) + br_ref[...]).T[0:N_EXPERTS] for h in hs]
    bucket, w_lo, w_hi = _route(jnp.concatenate(logits_t, axis=1))
    r128 = lax.broadcasted_iota(jnp.int32, (128, t), 0)
    tok = (pl.program_id(0) * t + lax.broadcasted_iota(jnp.int32, (1, t), 1)).astype(F32)
    aux = jnp.where(r128 == 0, w_lo, jnp.where(r128 == 1, w_hi, jnp.where(r128 == 2, tok, 0.0))).T
    disp_ref[disp_ref.shape[0] - 1] = pltpu.bitcast(aux, jnp.int32)
    brow = lax.broadcasted_iota(jnp.int32, (BUCKET_ROWS, t), 0).astype(F32)
    onehot = jnp.where(brow == bucket, 1.0, 0.0)
    before = _dot(onehot.astype(BF16), tri_ref[...]) + run_ref[:, 0:1]
    rank = jnp.sum(onehot * before, axis=0, keepdims=True)
    run_ref[...] = run_ref[...] + jnp.sum(onehot, axis=1, keepdims=True)
    cnt_ref[...] = run_ref[...]
    r8 = lax.broadcasted_iota(jnp.int32, (8, t), 0)
    meta_ref[...] = jnp.where(r8 == 0, bucket, jnp.where(r8 == 1, rank, 0.0)).astype(jnp.int32)


def _outproj_call(xs_, os_, ws, mods, nf, wr, br, n_pad):
    d = xs_[0].shape[1]
    n = sum(a.shape[0] for a in xs_)
    npt = xs_[0].shape[0] // TM
    row = lambda i: (i, 0)
    const = lambda i: (0, 0)
    n_o = len(os_)
    in_specs = ((_token_specs(npt, d) if len(xs_) == 2 else [pl.BlockSpec((TM, d), row)])
                + [pl.BlockSpec((TM, o.shape[1]), row) for o in os_]
                + [pl.BlockSpec(w.shape, const) for w in ws]
                + [_mod_spec(GATE_MIX), pl.BlockSpec((1, d), const),
                   _mod_spec(SHIFT_FFN), _mod_spec(SCALE_FFN),
                   pl.BlockSpec(wr.shape, const), pl.BlockSpec(br.shape, const),
                   pl.BlockSpec((TM, TM), const)])
    tri = jnp.asarray(np.triu(np.ones((TM, TM), np.float32), k=1), dtype=BF16)
    return pl.pallas_call(
        functools.partial(_outproj_kernel, n_x=len(xs_), n_o=n_o, n_prompt_tiles=npt),
        grid=(n // TM,),
        in_specs=in_specs,
        out_specs=[pl.BlockSpec((TM, d), row), pl.BlockSpec((DISP_SLABS, TM, 128), lambda i: (0, i, 0)),
                   pl.BlockSpec((8, TM), lambda i: (0, i)), pl.BlockSpec((BUCKET_ROWS, 128), const)],
        out_shape=[jax.ShapeDtypeStruct((n, d), F32), jax.ShapeDtypeStruct((DISP_SLABS, n_pad, 128), jnp.int32),
                   jax.ShapeDtypeStruct((8, n), jnp.int32), jax.ShapeDtypeStruct((BUCKET_ROWS, 128), F32)],
        scratch_shapes=[pltpu.VMEM((BUCKET_ROWS, 128), F32)],
        compiler_params=_cparams(("arbitrary",)),
        name="outproj_router",
    )(*xs_, *os_, *ws, mods, nf, mods, mods, wr, br, tri)


def _sc_mesh():
    return plsc.VectorSubcoreMesh(core_axis_name="core", subcore_axis_name="subcore")


def _sc_scatter_rows(src, idx, n_out):
    r = idx.shape[0]
    k = SC_GROUP
    w_per = r // (SC_WINDOW * SC_WORKERS)
    assert idx.shape == (src.shape[0],) and r % (SC_WINDOW * SC_WORKERS) == 0 and w_per % k == 0
    n_groups = w_per // k

    @functools.partial(
        pl.kernel, out_type=jax.ShapeDtypeStruct((n_out, 128), src.dtype), mesh=_sc_mesh(),
        scratch_types=[pltpu.VMEM((w_per, SC_WINDOW), jnp.int32),
                       pltpu.VMEM((2 * k, SC_WINDOW, 128), src.dtype),
                       pltpu.SemaphoreType.DMA((2,)), pltpu.SemaphoreType.DMA((2,))])
    def copy(x_hbm, i_hbm, o_hbm, ibuf, xbuf, in_sem, out_sem):
        wid = lax.axis_index("core") * (SC_WORKERS // 2) + lax.axis_index("subcore")
        pltpu.sync_copy(i_hbm.at[wid], ibuf)
        first = wid * w_per

        def start_in(g, slot):
            return [pltpu.async_copy(x_hbm.at[pl.ds((first + g * k + c) * SC_WINDOW, SC_WINDOW)],
                                     xbuf.at[slot * k + c], in_sem.at[slot]) for c in range(k)]

        def start_out(g, slot):
            return [pltpu.async_copy(xbuf.at[slot * k + c], o_hbm.at[ibuf.at[g * k + c]], out_sem.at[slot])
                    for c in range(k)]

        pending_in = start_in(0, 0)
        for g in range(n_groups):
            slot = g % 2
            for cp in pending_in:
                cp.wait()
            pending_out = start_out(g, slot)
            if g + 1 < n_groups:
                pending_in = start_in(g + 1, 1 - slot)
            for cp in pending_out:
                cp.wait()

    return copy(src, idx.reshape(SC_WORKERS, w_per, SC_WINDOW))


def _moe_kernel(elo_ref, ehi_ref, nvalid_ref, xs_ref, *refs, n_tok, dump_tiles):
    w_refs, (y_ref, tok_ref) = refs[:4 * MOE_TILES], refs[4 * MOE_TILES:]
    step = pl.program_id(0)
    t = TMO
    tiles = range(MOE_TILES)
    rows = [slice(t * j, t * (j + 1)) for j in tiles]
    auxs = [pltpu.bitcast(xs_ref[Y_SLABS, rows[j], :], F32) for j in tiles]
    r = lax.broadcasted_iota(jnp.int32, (1, t), 1)
    for j in tiles:
        i = step * MOE_TILES + j
        spare = n_tok + (i % dump_tiles) * t + r
        tok = jnp.where(r < nvalid_ref[i], auxs[j].T[2:3, :].astype(jnp.int32), spare)
        for c in range(t // 128):
            tok_ref[j, c:c + 1, :] = tok[:, 128 * c:128 * (c + 1)]

    any_tokens = nvalid_ref[step * MOE_TILES] > 0
    for j in range(1, MOE_TILES):
        any_tokens = jnp.logical_or(any_tokens, nvalid_ref[step * MOE_TILES + j] > 0)

    @pl.when(any_tokens)
    def _():
        units = [(j, e) for j in tiles for e in range(2)]
        hs = [_unpack_pairs([xs_ref[s, rows[j], :] for s in range(Y_SLABS)], BF16) for j in tiles]
        abs_ = [_dot(hs[j], w_refs[4 * j + e][0, 0]) for j, e in units]
        acts = [(ab[:, :D_FF] * _sigmoid(ab[:, :D_FF]) * ab[:, D_FF:]).astype(BF16) for ab in abs_]
        ys = [_dot(act, w_refs[4 * j + 2 + e][0, 0]) for act, (j, e) in zip(acts, units)]
        for j in tiles:
            acc = auxs[j][:, 0:1] * ys[2 * j] + auxs[j][:, 1:2] * ys[2 * j + 1]
            for s, slab in enumerate(_pack_pairs(acc)):
                y_ref[s, rows[j], :] = slab

    @pl.when(jnp.logical_not(any_tokens))
    def _():
        y_ref[...] = jnp.zeros_like(y_ref)


def _moe_call(xs, elo, ehi, nvalid, wgu, wdn, n_tiles, n_tok, dump_tiles):
    d = wgu.shape[2]
    m = MOE_TILES
    assert n_tiles % m == 0
    weight_specs = []
    for j in range(m):
        for shape in ((1, 1, d, 2 * D_FF), (1, 1, D_FF, d)):
            for sel in range(2):
                weight_specs.append(pl.BlockSpec(
                    shape, lambda i, lo, hi, v, j=j, sel=sel: (0, (lo, hi)[sel][m * i + j], 0, 0)))
    weights = [w for _ in range(m) for w in (wgu, wgu, wdn, wdn)]
    return pl.pallas_call(
        functools.partial(_moe_kernel, n_tok=n_tok, dump_tiles=dump_tiles),
        grid_spec=pltpu.PrefetchScalarGridSpec(
            num_scalar_prefetch=3,
            grid=(n_tiles // m,),
            in_specs=[pl.BlockSpec((DISP_SLABS, m * TMO, 128), lambda i, lo, hi, v: (0, i, 0))] + weight_specs,
            out_specs=[pl.BlockSpec((Y_SLABS, m * TMO, 128), lambda i, lo, hi, v: (0, i, 0)),
                       pl.BlockSpec((m, TMO // 128, 128), lambda i, lo, hi, v: (i, 0, 0))]),
        out_shape=[jax.ShapeDtypeStruct((Y_SLABS, n_tiles * TMO, 128), jnp.int32),
                   jax.ShapeDtypeStruct((n_tiles, TMO // 128, 128), jnp.int32)],
        compiler_params=_cparams(("arbitrary",), vmem_mb=VMEM_LIMIT_MOE_MB),
        name="moe_grouped",
    )(elo, ehi, nvalid, xs, *weights)


def _after(x, token):
    return lax.optimization_barrier((x, token))[0]


def _cast_kernel(after_ref, w_ref, o_ref):
    o_ref[...] = w_ref[...].astype(o_ref.dtype)


def _cast_call(w, layer, after):
    _, e, k, n = w.shape
    return pl.pallas_call(
        _cast_kernel,
        grid=(e,),
        in_specs=[pl.BlockSpec(memory_space=pl.ANY), pl.BlockSpec((1, 1, k, n), lambda i: (layer, i, 0, 0))],
        out_specs=pl.BlockSpec((1, 1, k, n), lambda i: (0, i, 0, 0)),
        out_shape=jax.ShapeDtypeStruct((1, e, k, n), BF16),
        compiler_params=_cparams(("parallel",)),
        name="cast_weights",
    )(after, w)


def _moe_layer(disp, meta, counts, w_gate_up, w_down, layer, n, n_pad, sort_rows):
    n_tiles = sort_rows // TMO
    wgu = _cast_call(w_gate_up, layer, counts)
    wdn = _cast_call(w_down, layer, counts)
    cnt = counts[:N_BUCKETS, 0].astype(jnp.int32)
    padded = ((cnt + TMO - 1) // TMO) * TMO
    ends = jnp.cumsum(padded)
    offs = ends - padded
    bucket, rank = meta[0], meta[1]
    pos = rank + jnp.sum(jnp.where(bucket[None, :] == jnp.arange(N_BUCKETS, dtype=jnp.int32)[:, None],
                                   offs[:, None], 0), axis=0)
    tile_start = jnp.arange(n_tiles, dtype=jnp.int32) * TMO
    tile_bucket = jnp.minimum(jnp.sum((tile_start[:, None] >= ends[None, :]).astype(jnp.int32), axis=1), N_BUCKETS - 1)
    pair_lo = np.array([0, 0, 0, 1, 1, 2], np.int32)
    pair_hi = np.array([1, 2, 3, 2, 3, 3], np.int32)
    b_lo = jnp.asarray(np.repeat(np.arange(N_GROUPS), N_PAIRS) * EXP_PER_GROUP + np.tile(pair_lo, N_GROUPS), jnp.int32)
    b_hi = jnp.asarray(np.repeat(np.arange(N_GROUPS), N_PAIRS) * EXP_PER_GROUP + np.tile(pair_hi, N_GROUPS), jnp.int32)
    onehot_tb = (tile_bucket[:, None] == jnp.arange(N_BUCKETS, dtype=jnp.int32)[None, :]).astype(jnp.int32)
    elo = jnp.sum(onehot_tb * b_lo[None, :], axis=1)
    ehi = jnp.sum(onehot_tb * b_hi[None, :], axis=1)
    bucket_end = jnp.sum(onehot_tb * (offs + cnt)[None, :], axis=1)
    nvalid = jnp.where(tile_start < ends[-1], jnp.clip(bucket_end - tile_start, 0, TMO), 0)
    dump = sort_rows + jnp.arange(n_pad - n, dtype=jnp.int32)
    pos_sc = jnp.concatenate([pos, dump])
    total = sort_rows + n_pad - n
    sc_idx = (pos_sc[None, :] + (jnp.arange(DISP_SLABS, dtype=jnp.int32) * total)[:, None]).reshape(-1)
    xs = _sc_scatter_rows(disp.reshape(DISP_SLABS * n_pad, 128), sc_idx, DISP_SLABS * total)
    ys, tok = _moe_call(xs.reshape(DISP_SLABS, total, 128), elo, ehi, nvalid, wgu, wdn, n_tiles,
                        n, (n_pad - n) // TMO)
    back_idx = (tok.reshape(1, sort_rows) + (jnp.arange(Y_SLABS, dtype=jnp.int32) * n_pad)[:, None]).reshape(-1)
    z = _sc_scatter_rows(ys.reshape(Y_SLABS * sort_rows, 128), back_idx, Y_SLABS * n_pad)
    return z.reshape(Y_SLABS, n_pad, 128), tok


def _final_kernel(xn_ref, z_ref, gate_ref, g_ref, yp_ref, ys_ref, *, n_prompt_tiles):
    x = _add_moe(xn_ref, z_ref, gate_ref)
    ms = jnp.mean(x * x, axis=-1, keepdims=True)
    y = x * lax.rsqrt(ms + EPS) * g_ref[...]
    i = pl.program_id(0)

    @pl.when(i < n_prompt_tiles)
    def _():
        yp_ref[...] = y

    @pl.when(i >= n_prompt_tiles)
    def _():
        ys_ref[...] = y


def _final_call(xn, z, mods, g, n_prompt):
    n, d = xn.shape
    npt = n_prompt // TM
    assert n - n_prompt == TM
    return pl.pallas_call(
        functools.partial(_final_kernel, n_prompt_tiles=npt),
        grid=(n // TM,),
        in_specs=[pl.BlockSpec((TM, d), lambda i: (i, 0)), pl.BlockSpec((z.shape[0], TM, 128), lambda i: (0, i, 0)),
                  _mod_spec(GATE_FFN), pl.BlockSpec((1, d), lambda i: (0, 0))],
        out_specs=_token_specs(npt, d),
        out_shape=[jax.ShapeDtypeStruct((n_prompt, d), F32), jax.ShapeDtypeStruct((TM, d), F32)],
        compiler_params=_cparams(("arbitrary",)),
        name="final_norm",
    )(xn, z, mods, g)


def kernel(x_prompt, x_sample, c_prompt, c_sample, state_gla, cache_band_k, cache_band_v, cache_swa_k, cache_swa_v,
           w_ada, b_ada, norm_mix, norm_ffn, norm_final, w_in_even, w_gate_a, b_gate_a, gla_norm, rel_bias_b,
           w_out_even, w_in_odd, sinks_c, w_out_odd, w_router, b_router, w_gate_up, w_down):
    bp, lp, d = x_prompt.shape
    bs, ls_, _ = x_sample.shape
    n_p, n_s = bp * lp, bs * ls_
    n = n_p + n_s
    assert ls_ == CHUNK and n_s == TM and lp % TM == 0 and PAST_LEN % CHUNK == 0

    xp2, xs2 = x_prompt.reshape(n_p, d), x_sample.reshape(n_s, d)

    c16 = jnp.zeros((SEQ_ROWS, d), F32).at[:bp].set(c_prompt).at[bp:bp + bs].set(c_sample)
    mods = _ada_call(c16, w_ada, b_ada)
    seq_of_group = np.concatenate([np.repeat(np.arange(bp), lp // CHUNK), bp + np.arange(bs)])
    mods_g = [mods[l][seq_of_group] for l in range(DEPTH)]

    perm = np.array([4 * (c % 4) + c // 4 for c in range(N_EXPERTS)])
    wr = jnp.zeros((d, 128), F32).at[:, :N_EXPERTS].set(w_router[:, perm])
    br = jnp.zeros((1, 128), F32).at[0, :N_EXPERTS].set(b_router[perm])

    sc_unit = SC_WINDOW * SC_WORKERS * SC_GROUP
    n_pad = n + TMO
    while (DISP_SLABS * n_pad) % sc_unit or (Y_SLABS * n_pad) % TMO or (n_pad - n) % TMO:
        n_pad += TMO
    sort_rows = n + N_BUCKETS * TMO
    while (Y_SLABS * sort_rows) % sc_unit or sort_rows % (MOE_TILES * TMO):
        sort_rows += TMO

    gla_p = gla_s = bk_p = bv_p = bk_s = bv_s = sk_p = sv_p = sk_s = sv_s = None
    xn = z = tok = None
    for l in range(DEPTH):
        i = l // 2
        if l % 2 == 0:
            w = w_in_even[i]
            w_main = jnp.concatenate([w[:, :1536], w[:, 1552:]], axis=1).astype(BF16)
            w_la = jnp.zeros((d, 128), F32).at[:, :GATE_RANK].set(w[:, 1536:1552]).astype(BF16)
            w_gate = jnp.zeros((128, HA * DKA), F32).at[:GATE_RANK].set(w_gate_a[i])
            qa, ka, va, ra, qb, kb, vb, ga = _inproj_even_call(
                xp2, xs2, mods_g[l], norm_mix[l][None], w_main, w_la, w_gate, b_gate_a[i][None])
            xres = [xp2, xs2]
            gn = gla_norm[i][None]
            oa, s_p = _gla_call(qa, ka, va, ga, ra, jnp.zeros((bp, 256, 128), F32), gn, None,
                                n_seq=bp, seq_rows=lp, row0=0, nb=8)
            oa, s_s = _gla_call(qa, ka, va, ga, ra, state_gla[i].reshape(bs, 256, 128), gn, oa,
                                n_seq=bs, seq_rows=ls_, row0=n_p, nb=1)
            gla_p, gla_s = s_p.reshape(1, bp, HA, DKA, DVA), s_s.reshape(1, bs, HA, DKA, DVA)
            pb = N_PREV_B * CHUNK
            tq, g = 512, 2
            ck = cache_band_k[i].reshape(bs * pb, HB * DHB).astype(BF16)
            cv = cache_band_v[i].reshape(bs * pb, HB * DHB).astype(BF16)
            biases = (_band_bias(rel_bias_b[i], g, pb, _band_valid(g, pb)),
                      _band_bias(rel_bias_b[i], g, pb, _band_valid(g, pb, tq // (CHUNK * g))),
                      _band_bias(rel_bias_b[i], 1, pb, _band_valid(1, pb)))
            ob = _attention(_band_kernel, qb, kb, vb, ck, cv, biases, [], [], width=512, kv_width=512, pb=pb,
                            tq=tq, g=g, bp=bp, lp=lp, bs=bs, name="band")
            tail = lambda a: jnp.stack([a[(b + 1) * lp - pb:(b + 1) * lp] for b in range(bp)]).astype(F32).reshape(1, bp, pb, HB, DHB)
            new = lambda a: a[n_p:].astype(F32).reshape(bs, ls_, HB, DHB)
            bk_p, bv_p = tail(kb), tail(vb)
            bk_s = jnp.concatenate([cache_band_k[i][:, ls_:], new(kb)], axis=1)[None]
            bv_s = jnp.concatenate([cache_band_v[i][:, ls_:], new(vb)], axis=1)[None]
            wo = w_out_even[i].astype(BF16)
            os_, ws = [oa, ob], [wo[:HA * DVA], wo[HA * DVA:]]
        else:
            w = _after(w_in_odd[i], tok)
            w_out_l = _after(w_out_odd[i], tok)
            cache_k_l, cache_v_l = _after(cache_swa_k[i], tok), _after(cache_swa_v[i], tok)
            wk, wv = w[:, 1024:1152], w[:, 1152:1280]
            dup = lambda a: jnp.concatenate([a[:, :64], a[:, :64], a[:, 64:], a[:, 64:]], axis=1)
            w_all = jnp.concatenate([w[:, :1024], dup(wk), dup(wv)], axis=1).astype(BF16)
            cos, sin, rope_map = _rope_tables(lp, ls_, bp, bs)
            x, q, k, v = _inproj_odd_call(xn, z, mods_g[l - 1], mods_g[l], norm_mix[l][None], cos, sin, rope_map, w_all)
            xres = [x]
            pb = WINDOW
            tq, g = 512, 2
            sink = sinks_c[i][None] * LOG2E
            sink_spec = [pl.BlockSpec(memory_space=pltpu.SMEM)]
            dupc = lambda c: jnp.concatenate([c[:, :, 0], c[:, :, 0], c[:, :, 1], c[:, :, 1]], axis=-1).reshape(bs * pb, 256).astype(BF16)
            ck, cv = dupc(cache_k_l), dupc(cache_v_l)
            additive = lambda valid: jnp.asarray(np.where(valid, 0.0, -np.inf), F32)
            masks = (additive(_band_valid(g, pb)), additive(_band_valid(g, pb, tq // (CHUNK * g))),
                     additive(_band_valid(1, pb)))
            o = _attention(_swa_kernel, q, k, v, ck, cv, masks, [sink], sink_spec, width=1024, kv_width=256, pb=pb,
                           tq=tq, g=g, bp=bp, lp=lp, bs=bs, name="swa")
            undup = lambda a: jnp.concatenate([a[:, 0:64], a[:, 128:192]], axis=1).astype(F32)
            tail = lambda a: jnp.stack([undup(a[(b + 1) * lp - pb:(b + 1) * lp]) for b in range(bp)]).reshape(1, bp, pb, KVC, DHC)
            new = lambda a: undup(a[n_p:]).reshape(bs, ls_, KVC, DHC)
            sk_p, sv_p = tail(k), tail(v)
            sk_s = jnp.concatenate([cache_swa_k[i][:, ls_:], new(k)], axis=1)[None]
            sv_s = jnp.concatenate([cache_swa_v[i][:, ls_:], new(v)], axis=1)[None]
            os_, ws = [o], [w_out_l.astype(BF16)]
        xn, disp, meta, counts = _outproj_call(xres, os_, ws, mods_g[l], norm_ffn[l][None], wr, br, n_pad)
        z, tok = _moe_layer(disp, meta, counts, w_gate_up, w_down, l, n, n_pad, sort_rows)

    y_prompt, y_sample = _final_call(xn, z, mods_g[DEPTH - 1], norm_final[None], n_p)
    return (y_prompt.reshape(bp, lp, d), y_sample.reshape(bs, ls_, d),
            gla_p, gla_s, bk_p, bv_p, bk_s, bv_s, sk_p, sv_p, sk_s, sv_s)
```

```python
import functools

import numpy as np
import jax
import jax.numpy as jnp
from jax import lax
from jax.experimental import pallas as pl
from jax.experimental.pallas import tpu as pltpu
from jax.experimental.pallas import tpu_sc as plsc

F32 = jnp.float32
BF16 = jnp.bfloat16

D_MODEL = 1024
DEPTH = 2
CHUNK = 64
PAST_LEN = 4096
HA, DKA, DVA = 4, 64, 128
GATE_RANK = 16
GATE_TAU = 16.0
HB, DHB = 8, 64
N_PREV_B = 8
MAX_REL = 128
HC, KVC, DHC = 16, 2, 64
WINDOW = 128
ROPE_THETA = 10000.0
N_EXPERTS = 16
N_GROUPS = 4
EXP_PER_GROUP = 4
D_FF = 512
EPS = 1e-6

N_PAIRS = 6
N_BUCKETS = N_GROUPS * N_PAIRS
BUCKET_ROWS = 32
Y_SLABS = 4
DISP_SLABS = Y_SLABS + 1
TMO = 256
MOE_TILES = 2
SC_WINDOW = 128
SC_WORKERS = 32
SC_GROUP = 3

TM = 512
SEQ_ROWS = 16
SUB = 16
LOG2E = 1.4426950408889634
VMEM_LIMIT_MB = 48
VMEM_LIMIT_MOE_MB = 56


def _cparams(sem, vmem_mb=VMEM_LIMIT_MB):
    return pltpu.CompilerParams(dimension_semantics=sem, vmem_limit_bytes=vmem_mb * 1024 * 1024)


def _dot(a, b):
    return jnp.dot(a, b, preferred_element_type=F32)


def _dot_nt(a, b):
    return lax.dot_general(a, b, (((1,), (1,)), ((), ())), preferred_element_type=F32)


def _split(a):
    hi = a.astype(BF16)
    lo = (a - hi.astype(F32)).astype(BF16)
    return hi, lo


def _dot3(a, b):
    ah, al = _split(a)
    bh, bl = _split(b)
    return _dot(ah, bh) + _dot(ah, bl) + _dot(al, bh)


def _dot3_narrow(a, b):
    ah, al = _split(a)
    bh, bl = _split(b)
    n = b.shape[1]
    p = _dot(ah, jnp.concatenate([bh, bl], axis=1))
    return p[:, :n] + p[:, n:] + _dot(al, bh)


def _sigmoid(x):
    return 1.0 / (1.0 + jnp.exp(-x))


def _group_affine(y, mul, add):
    parts = []
    for gi in range(y.shape[0] // CHUNK):
        p = y[gi * CHUNK:(gi + 1) * CHUNK]
        if mul is not None:
            p = p * mul[gi:gi + 1]
        if add is not None:
            p = p + add[gi:gi + 1]
        parts.append(p)
    return jnp.concatenate(parts, axis=0)


def _norm_mod(x, g, shift, scale):
    ms = jnp.mean(x * x, axis=-1, keepdims=True)
    return _group_affine(x * lax.rsqrt(ms + EPS) * g, 1.0 + scale, shift)


def _mod_spec(part):
    return pl.BlockSpec((TM // CHUNK, D_MODEL), lambda i: (i, part))


SHIFT_MIX, SCALE_MIX, GATE_MIX, SHIFT_FFN, SCALE_FFN, GATE_FFN = range(6)


def _on_token_tile(xp_ref, xs_ref, n_prompt_tiles, body):
    @pl.when(pl.program_id(0) < n_prompt_tiles)
    def _():
        body(xp_ref)

    @pl.when(pl.program_id(0) >= n_prompt_tiles)
    def _():
        body(xs_ref)


def _token_specs(n_prompt_tiles, d):
    return [pl.BlockSpec((TM, d), lambda i: (jnp.minimum(i, n_prompt_tiles - 1), 0)),
            pl.BlockSpec((TM, d), lambda i: (0, 0))]


def _ada_kernel(c_ref, w_ref, b_ref, o_ref):
    c = c_ref[...]
    o_ref[0] = _dot3(c * _sigmoid(c), w_ref[0]) + b_ref[0]


def _ada_call(c16, w_ada, b_ada):
    d = D_MODEL
    tn = 1024
    return pl.pallas_call(
        _ada_kernel,
        grid=(DEPTH, 6 * d // tn),
        in_specs=[pl.BlockSpec((SEQ_ROWS, d), lambda l, j: (0, 0)),
                  pl.BlockSpec((1, d, tn), lambda l, j: (l, 0, j)),
                  pl.BlockSpec((1, 1, tn), lambda l, j: (l, 0, j))],
        out_specs=pl.BlockSpec((1, SEQ_ROWS, tn), lambda l, j: (l, 0, j)),
        out_shape=jax.ShapeDtypeStruct((DEPTH, SEQ_ROWS, 6 * d), F32),
        compiler_params=_cparams(("arbitrary", "arbitrary")),
        name="ada",
    )(c16, w_ada, b_ada.reshape(DEPTH, 1, 6 * d))


def _inproj_even_kernel(xp_ref, xs_ref, sh_ref, sc_ref, g_ref, w_ref, wla_ref, wg_ref, bg_ref,
                        qa_ref, ka_ref, va_ref, ra_ref, qb_ref, kb_ref, vb_ref, ga_ref, *, n_prompt_tiles):
    def body(x_ref):
        t = x_ref.shape[0]
        outs = ((qa_ref, 0, 256, DKA ** -0.5), (ka_ref, 256, 512, None), (va_ref, 512, 1024, None),
                (ra_ref, 1024, 1536, None), (qb_ref, 1536, 2048, DHB ** -0.5 * LOG2E), (kb_ref, 2048, 2560, None),
                (vb_ref, 2560, 3072, None))
        shift, scale_ = sh_ref[...], sc_ref[...]
        halves = [slice(0, t // 2), slice(t // 2, t)]
        grp = [slice(0, t // (2 * CHUNK)), slice(t // (2 * CHUNK), t // CHUNK)]
        hbs = [_norm_mod(x_ref[rs, :], g_ref[...], shift[gs], scale_[gs]).astype(BF16) for rs, gs in zip(halves, grp)]
        for rs, hb in zip(halves, hbs):
            zs = [_dot(hb, w_ref[:, lo:hi]) for _, lo, hi, _ in outs]
            la = _dot(hb, wla_ref[...])
            for z, (o_ref, _, _, scale) in zip(zs, outs):
                o_ref[rs, :] = (z if scale is None else z * scale).astype(BF16)
            gl = _dot3(la, wg_ref[...]) + bg_ref[...]
            ga_ref[rs, :] = -(jnp.maximum(-gl, 0.0) + jnp.log(1.0 + jnp.exp(-jnp.abs(gl)))) * (1.0 / GATE_TAU)

    _on_token_tile(xp_ref, xs_ref, n_prompt_tiles, body)


def _inproj_even_call(xp, xs, mods, g, w_main, w_la, w_gate, b_gate):
    d = xp.shape[1]
    npt = xp.shape[0] // TM
    n = xp.shape[0] + xs.shape[0]
    row = lambda i: (i, 0)
    const = lambda i: (0, 0)
    widths = (256, 256, 512, 512, 512, 512, 512)
    out_shape = [jax.ShapeDtypeStruct((n, w), BF16) for w in widths] + [jax.ShapeDtypeStruct((n, 256), F32)]
    out_specs = [pl.BlockSpec((TM, w), row) for w in widths] + [pl.BlockSpec((TM, 256), row)]
    return pl.pallas_call(
        functools.partial(_inproj_even_kernel, n_prompt_tiles=npt),
        grid=(n // TM,),
        in_specs=_token_specs(npt, d) + [
            _mod_spec(SHIFT_MIX), _mod_spec(SCALE_MIX),
            pl.BlockSpec((1, d), const),
            pl.BlockSpec(w_main.shape, const), pl.BlockSpec(w_la.shape, const),
            pl.BlockSpec(w_gate.shape, const), pl.BlockSpec(b_gate.shape, const)],
        out_specs=out_specs, out_shape=out_shape,
        compiler_params=_cparams(("parallel",)),
        name="inproj_even",
    )(xp, xs, mods, mods, g, w_main, w_la, w_gate, b_gate)


def _rope(x, cos, sin_signed):
    t, w = x.shape
    lane = lax.broadcasted_iota(jnp.int32, (1, w), 1)
    first_half = (lane & 63) < 32
    rot = jnp.where(first_half, pltpu.roll(x, w - 32, 1), pltpu.roll(x, 32, 1))
    reps = w // 128
    return x * jnp.tile(cos, (1, reps)) + rot * jnp.tile(sin_signed, (1, reps))


def _unpack_pairs(slabs, dtype):
    lo = [pltpu.bitcast(s << 16, F32) for s in slabs]
    hi = [pltpu.bitcast(s & jnp.int32(-65536), F32) for s in slabs]
    return jnp.concatenate(lo + hi, axis=1).astype(dtype)


def _pack_pairs(x):
    bits = pltpu.bitcast(x.astype(BF16).astype(F32), jnp.int32)
    half = x.shape[1] // 2
    packed = ((bits[:, :half] >> 16) & jnp.int32(0xFFFF)) | (bits[:, half:] & jnp.int32(-65536))
    return [packed[:, 128 * s:128 * (s + 1)] for s in range(half // 128)]


def _add_moe(xn_ref, z_ref, gate_ref):
    y = _unpack_pairs([z_ref[s] for s in range(z_ref.shape[0])], F32)
    return xn_ref[...] + _group_affine(y, gate_ref[...], None)


def _rope_tables(lp, ls_, bp, bs):
    assert PAST_LEN + ls_ <= lp and lp % 128 == 0 and bs * ls_ == TM
    half = DHC // 2
    inv = ROPE_THETA ** (-jnp.arange(half, dtype=F32) / half)
    inv = jnp.tile(inv, 128 // half)
    sign = jnp.asarray(np.tile(np.repeat([-1.0, 1.0], half), 128 // DHC), F32)
    a = jnp.asarray(np.arange(lp // 128) * 128, F32)[:, None] * inv[None, :]
    b = jnp.asarray(np.arange(128), F32)[:, None] * inv[None, :]
    ca, sa, cb, sb = jnp.cos(a)[:, None], jnp.sin(a)[:, None], jnp.cos(b)[None], jnp.sin(b)[None]
    cos = (ca * cb - sa * sb).reshape(lp, 128)
    sin = ((sa * cb + ca * sb) * sign).reshape(lp, 128)
    with_sample = lambda t: jnp.concatenate([t, jnp.tile(t[PAST_LEN:PAST_LEN + ls_], (bs, 1))], axis=0)
    tiles = lp // TM
    return with_sample(cos), with_sample(sin), lambda i: (jnp.where(i < bp * tiles, i % tiles, tiles), 0)


def _inproj_odd_kernel(xn_ref, z_ref, gate_ref, sh_ref, sc_ref, g_ref, cos_ref, sin_ref, w_ref,
                       x_ref, q_ref, k_ref, v_ref):
    t = xn_ref.shape[0]
    halves = [slice(0, t // 2), slice(t // 2, t)]
    grp = [slice(0, t // (2 * CHUNK)), slice(t // (2 * CHUNK), t // CHUNK)]
    gate, shift, scale = gate_ref[...], sh_ref[...], sc_ref[...]
    xs = []
    for rs, gs in zip(halves, grp):
        y = _unpack_pairs([z_ref[s, rs, :] for s in range(z_ref.shape[0])], F32)
        xs.append(xn_ref[rs, :] + _group_affine(y, gate[gs], None))
    for rs, x in zip(halves, xs):
        x_ref[rs, :] = x
    hbs = [_norm_mod(x, g_ref[...], shift[gs], scale[gs]).astype(BF16) for x, gs in zip(xs, grp)]
    qs = [_dot(hb, w_ref[:, 0:1024]) for hb in hbs]
    ks = [_dot(hb, w_ref[:, 1024:1280]) for hb in hbs]
    vs = [_dot(hb, w_ref[:, 1280:1536]) for hb in hbs]
    for rs, q, k, v in zip(halves, qs, ks, vs):
        cos, sin = cos_ref[rs, :], sin_ref[rs, :]
        q_ref[rs, :] = (_rope(q, cos, sin) * (DHC ** -0.5 * LOG2E)).astype(BF16)
        k_ref[rs, :] = _rope(k, cos, sin).astype(BF16)
        v_ref[rs, :] = v.astype(BF16)


def _inproj_odd_call(xn, z, mods_prev, mods, g, cos, sin, rope_map, w):
    n, d = xn.shape
    row = lambda i: (i, 0)
    const = lambda i: (0, 0)
    widths = (1024, 256, 256)
    return pl.pallas_call(
        _inproj_odd_kernel,
        grid=(n // TM,),
        in_specs=[pl.BlockSpec((TM, d), row), pl.BlockSpec((z.shape[0], TM, 128), lambda i: (0, i, 0)),
                  _mod_spec(GATE_FFN), _mod_spec(SHIFT_MIX), _mod_spec(SCALE_MIX),
                  pl.BlockSpec((1, d), const),
                  pl.BlockSpec((TM, 128), rope_map), pl.BlockSpec((TM, 128), rope_map),
                  pl.BlockSpec(w.shape, const)],
        out_specs=[pl.BlockSpec((TM, d), row)] + [pl.BlockSpec((TM, wd), row) for wd in widths],
        out_shape=[jax.ShapeDtypeStruct((n, d), F32)] + [jax.ShapeDtypeStruct((n, wd), BF16) for wd in widths],
        compiler_params=_cparams(("parallel",)),
        name="inproj_odd",
    )(xn, z, mods_prev, mods, mods, g, cos, sin, w)


def _gla_tri():
    t = np.arange(CHUNK)[:, None]
    s = np.arange(CHUNK)[None, :]
    cum = s <= t
    start = s < (t // SUB) * SUB
    end = s < (t // SUB + 1) * SUB
    return jnp.asarray(np.concatenate([cum, start, end], axis=0).astype(np.float32), dtype=BF16)


def _gla_kernel(q_ref, k_ref, v_ref, g_ref, r_ref, s0_ref, gn_ref, tri_ref, o_ref, sout_ref, s_ref, *, nb):
    c_ = CHUNK
    nsub = c_ // SUB

    @pl.when(pl.program_id(1) == 0)
    def _():
        s_ref[...] = s0_ref[0]

    tri = tri_ref[...]
    lane = lax.broadcasted_iota(jnp.int32, (1, 128), 1)
    hmask = [jnp.where(lane < DKA, 1.0, 0.0), jnp.where(lane >= DKA, 1.0, 0.0)]
    ti = lax.broadcasted_iota(jnp.int32, (c_, c_), 0)
    si = lax.broadcasted_iota(jnp.int32, (c_, c_), 1)
    rb, cb = ti >> 4, si >> 4
    m_diag = (rb == cb) & (si <= ti)
    m_off = [(cb == j) & (rb > j) for j in range(nsub - 1)]
    hk = HA * DKA
    gn = gn_ref[...]

    chunks = range(nb)
    heads = [(p, hh) for p in range(HA // 2) for hh in range(2)]
    rows = [slice(c * c_, (c + 1) * c_) for c in chunks]
    pair = [slice(128 * p, 128 * (p + 1)) for p in range(HA // 2)]
    css = []
    for c in chunks:
        g_hi, g_lo = _split(g_ref[rows[c], :])
        css.append(_dot(tri, g_hi) + _dot(tri, g_lo))
    lhs1, lhs2, kds, kes, q_inter, klts, dcols = [], [], [], [], [], [], []
    for c in chunks:
        b, rs, re = css[c][0:c_], css[c][c_:2 * c_], css[c][2 * c_:3 * c_]
        q = q_ref[rows[c], :].astype(F32)
        k = k_ref[rows[c], :].astype(F32)
        bl = b[c_ - 1:c_, :]
        qd = q * jnp.exp(b - rs)
        kd = k * jnp.exp(rs - b)
        ke = k * jnp.exp(re - b)
        qi = q * jnp.exp(b)
        kl = k * jnp.exp(bl - b)
        ql = [q * jnp.exp(jnp.minimum(b - b[SUB * (j + 1) - 1:SUB * (j + 1), :], 0.0)) for j in range(nsub - 1)]
        dcols.append(jnp.broadcast_to(jnp.exp(bl), (8, hk)).T[:, 0:1])
        kds.append([(kd[:, pair[p]] * hmask[hh]).astype(BF16) for p, hh in heads])
        kes.append([(ke[:, pair[p]] * hmask[hh]).astype(BF16) for p, hh in heads])
        klts.append([kl[:, ls].T.astype(BF16) for ls in pair])
        lhs1.append([qd[:, ls].astype(BF16) for ls in pair])
        lhs2.append([jnp.concatenate([ql[j][:, ls] for j in range(nsub - 1)], axis=0).astype(BF16) for ls in pair])
        q_inter.append([(qi[:, pair[p]] * hmask[hh]).astype(BF16) for p, hh in heads])
    a1s = [[_dot_nt(lhs1[c][p], kds[c][h]) for h, (p, hh) in enumerate(heads)] for c in chunks]
    a2s = [[_dot_nt(lhs2[c][p], kes[c][h]) for h, (p, hh) in enumerate(heads)] for c in chunks]
    atts = []
    for c in chunks:
        per_head = []
        for h in range(HA):
            att = jnp.zeros((c_, c_), F32)
            for j in reversed(range(nsub - 1)):
                att = jnp.where(m_off[j], a2s[c][h][j * c_:(j + 1) * c_], att)
            per_head.append(jnp.where(m_diag, a1s[c][h], att).astype(BF16))
        atts.append(per_head)
    vs_ = [[v_ref[rows[c], DVA * h:DVA * (h + 1)] for h in range(HA)] for c in chunks]
    o_intra = [[_dot(atts[c][h], vs_[c][h]) for h in range(HA)] for c in chunks]
    upds = [jnp.concatenate([_dot(klts[c][p][DKA * hh:DKA * (hh + 1)], vs_[c][2 * p + hh]) for p, hh in heads], axis=0)
            for c in chunks]

    s_cur = s_ref[...]
    s_in = []
    for c in chunks:
        s_in.append(s_cur.astype(BF16))
        s_cur = dcols[c] * s_cur + upds[c]
    s_ref[...] = s_cur
    sout_ref[0] = s_cur

    for c in chunks:
        for h in range(HA):
            o = o_intra[c][h] + _dot(q_inter[c][h], s_in[c][pair[h // 2], :])
            ms = jnp.mean(o * o, axis=-1, keepdims=True)
            vs = slice(DVA * h, DVA * (h + 1))
            rr = r_ref[rows[c], vs].astype(F32)
            o_ref[rows[c], vs] = (o * lax.rsqrt(ms + EPS) * gn * (rr * _sigmoid(rr))).astype(BF16)


def _gla_call(q, k, v, g, r, s0, gn, o_prev, *, n_seq, seq_rows, row0, nb):
    tq = nb * CHUNK
    steps = seq_rows // tq
    blk0 = row0 // tq
    row = lambda b, j: (blk0 + b * steps + j, 0)
    const = lambda b, j: (0, 0)
    tri = _gla_tri()
    in_specs = [pl.BlockSpec((tq, 256), row), pl.BlockSpec((tq, 256), row), pl.BlockSpec((tq, 512), row),
                pl.BlockSpec((tq, 256), row), pl.BlockSpec((tq, 512), row),
                pl.BlockSpec((1, 256, 128), lambda b, j: (b, 0, 0)),
                pl.BlockSpec((1, 128), const), pl.BlockSpec(tri.shape, const)]
    args = [q, k, v, g, r, s0, gn, tri]
    aliases = {}
    if o_prev is not None:
        in_specs.append(pl.BlockSpec(memory_space=pl.ANY))
        args.append(o_prev)
        aliases = {len(args) - 1: 0}
    kern = functools.partial(_gla_kernel, nb=nb)
    if o_prev is not None:
        kern = _drop_arg(kern, 8)
    return pl.pallas_call(
        kern,
        grid=(n_seq, steps),
        in_specs=in_specs,
        out_specs=[pl.BlockSpec((tq, 512), row), pl.BlockSpec((1, 256, 128), lambda b, j: (b, 0, 0))],
        out_shape=[jax.ShapeDtypeStruct((q.shape[0], 512), BF16), jax.ShapeDtypeStruct((n_seq, 256, 128), F32)],
        scratch_shapes=[pltpu.VMEM((256, 128), F32)],
        input_output_aliases=aliases,
        compiler_params=_cparams(("arbitrary", "arbitrary")),
        name="gla",
    )(*args)


def _drop_arg(fn, idx):
    def wrapped(*refs):
        return fn(*refs[:idx], *refs[idx + 1:])
    return wrapped


def _window(prev_ref, cur_ref, lo, hi, pb, ls):
    if lo < pb:
        return jnp.concatenate([prev_ref[lo:pb, ls], cur_ref[0:hi - pb, ls]], axis=0)
    return cur_ref[lo - pb:hi - pb, ls]


def _band_kernel(q_ref, kp_ref, kc_ref, vp_ref, vc_ref, bias_ref, o_ref, *, g, n_sub, pb):
    qs = CHUNK * g
    kw_rows = pb + qs
    lane = lax.broadcasted_iota(jnp.int32, (1, 128), 1)
    low = lane < DHB
    hmask = [jnp.where(low, 1.0, 0.0), jnp.where(low, 0.0, 1.0)]
    for s in range(n_sub):
        sb = s if bias_ref.shape[0] > 1 else 0
        rows = slice(qs * s, qs * (s + 1))
        lanes = [slice(128 * p, 128 * (p + 1)) for p in range(HB // 2)]
        heads = [(p, hh) for p in range(HB // 2) for hh in range(2)]
        qps = [q_ref[rows, ls].astype(F32) for ls in lanes]
        kws = [_window(kp_ref, kc_ref, qs * s, qs * s + kw_rows, pb, ls) for ls in lanes]
        vws = [_window(vp_ref, vc_ref, qs * s, qs * s + kw_rows, pb, ls) for ls in lanes]
        qq = [jnp.concatenate([(qps[p] * hmask[hh]).astype(BF16) for hh in range(2)], axis=0) for p in range(HB // 2)]
        sc2 = [_dot_nt(qq[p], kws[p]) for p in range(HB // 2)]
        scs = [sc2[p][qs * hh:qs * (hh + 1)] + bias_ref[sb, 2 * p + hh] for p, hh in heads]
        pes = [jnp.exp2(sc - jnp.max(sc, axis=-1, keepdims=True)) for sc in scs]
        pp = [jnp.concatenate([pes[2 * p + hh].astype(BF16) for hh in range(2)], axis=0) for p in range(HB // 2)]
        o2 = [_dot(pp[p], vws[p]) for p in range(HB // 2)]
        outs = [o2[p][qs * hh:qs * (hh + 1)] / jnp.sum(pes[2 * p + hh], axis=-1, keepdims=True) for p, hh in heads]
        for p, ls in enumerate(lanes):
            o_ref[rows, ls] = jnp.where(low, outs[2 * p], outs[2 * p + 1]).astype(BF16)


def _band_valid(g, pb, n_sub=None):
    rows, kw = CHUNK * g, pb + CHUNK * g
    r = np.arange(rows)[:, None]
    c = np.arange(kw)[None, :]
    dd = c // CHUNK - r // CHUNK
    band = (dd >= 0) & (dd <= pb // CHUNK)
    if n_sub is None:
        return band[None]
    return np.stack([band & (c >= pb - rows * s) for s in range(n_sub)])


def _band_bias(table, g, pb, valid):
    rows, kw = CHUNK * g, pb + CHUNK * g
    period = kw + rows
    m = np.arange(period)
    m = np.where(m < kw, m, m - period)
    ext = table[:, np.clip(m - pb, -MAX_REL, MAX_REL) + MAX_REL] * LOG2E
    flat = jnp.tile(ext, (1, rows))[:, :rows * (period - 1)]
    bias = flat.reshape(table.shape[0], rows, period - 1)[:, :, :kw]
    return jnp.where(valid[:, None], bias[None], -jnp.inf)


def _attn_call(kernel, q, kp, kc, vp, vc, extra, extra_specs, o_prev, *, width, kv_width, tq, pb,
               n_blocks, blk_map, prev_map, name):
    row = lambda i: (blk_map(i), 0)
    prev = lambda i: (prev_map(i), 0)
    in_specs = [pl.BlockSpec((tq, width), row),
                pl.BlockSpec((pb, kv_width), prev), pl.BlockSpec((tq, kv_width), row),
                pl.BlockSpec((pb, kv_width), prev), pl.BlockSpec((tq, kv_width), row)] + extra_specs
    args = [q, kp, kc, vp, vc] + extra
    aliases = {}
    if o_prev is not None:
        in_specs.append(pl.BlockSpec(memory_space=pl.ANY))
        args.append(o_prev)
        aliases = {len(args) - 1: 0}
        kernel = _drop_arg(kernel, len(args) - 1)
    return pl.pallas_call(
        kernel,
        grid=(n_blocks,),
        in_specs=in_specs,
        out_specs=pl.BlockSpec((tq, width), row),
        out_shape=jax.ShapeDtypeStruct((q.shape[0], width), BF16),
        input_output_aliases=aliases,
        compiler_params=_cparams(("parallel",)),
        name=name,
    )(*args)


def _attention(kernel_fn, q, k, v, cache_k, cache_v, masks, extra, extra_specs, *, width, kv_width, pb, tq, g,
               bp, lp, bs, name):
    bps = lp // tq
    n_sub = tq // (CHUNK * g)
    spec = lambda a: [pl.BlockSpec(a.shape, lambda i: (0,) * a.ndim)]
    kern = functools.partial(kernel_fn, g=g, n_sub=n_sub, pb=pb)
    common = dict(width=width, kv_width=kv_width, pb=pb)
    main = lambda i: (i // (bps - 1)) * bps + i % (bps - 1) + 1
    o = _attn_call(kern, q, k, k, v, v, [masks[0]] + extra, spec(masks[0]) + extra_specs, None, tq=tq,
                   n_blocks=bp * (bps - 1), blk_map=main, prev_map=lambda i: main(i) * (tq // pb) - 1,
                   name=name + "_main", **common)
    first = lambda i: i * bps
    o = _attn_call(kern, q, k, k, v, v, [masks[1]] + extra, spec(masks[1]) + extra_specs, o, tq=tq,
                   n_blocks=bp, blk_map=first, prev_map=lambda i: jnp.maximum(first(i) * (tq // pb) - 1, 0),
                   name=name + "_first", **common)
    samp = functools.partial(kernel_fn, g=1, n_sub=1, pb=pb)
    return _attn_call(samp, q, cache_k, k, cache_v, v, [masks[2]] + extra, spec(masks[2]) + extra_specs, o, tq=CHUNK,
                      n_blocks=bs, blk_map=lambda i: bp * lp // CHUNK + i, prev_map=lambda i: i,
                      name=name + "_sample", **common)


def _swa_kernel(q_ref, kp_ref, kc_ref, vp_ref, vc_ref, mask_ref, sink_ref, o_ref, *, g, n_sub, pb):
    qs = CHUNK * g
    kw_rows = pb + qs
    lane = lax.broadcasted_iota(jnp.int32, (1, 128), 1)
    low = lane < DHC
    hmask = [jnp.where(low, 1.0, 0.0), jnp.where(low, 0.0, 1.0)]
    pairs_per_kv = HC // KVC // 2
    for s in range(n_sub):
        msk = mask_ref[s if mask_ref.shape[0] > 1 else 0]
        rows = slice(qs * s, qs * (s + 1))
        kws = [_window(kp_ref, kc_ref, qs * s, qs * s + kw_rows, pb, slice(128 * kv, 128 * (kv + 1))) for kv in range(KVC)]
        vws = [_window(vp_ref, vc_ref, qs * s, qs * s + kw_rows, pb, slice(128 * kv, 128 * (kv + 1))) for kv in range(KVC)]
        heads = [(j, hh) for j in range(HC // 2) for hh in range(2)]
        qps = [q_ref[rows, 128 * j:128 * (j + 1)].astype(F32) for j in range(HC // 2)]
        per_kv = 2 * pairs_per_kv
        qq = [jnp.concatenate([(qps[j] * hmask[hh]).astype(BF16) for j, hh in heads[per_kv * kv:per_kv * (kv + 1)]],
                              axis=0) for kv in range(KVC)]
        sc2 = [_dot_nt(qq[kv], kws[kv]) for kv in range(KVC)]
        scs = [sc2[u // per_kv][qs * (u % per_kv):qs * (u % per_kv + 1)] + msk for u in range(len(heads))]
        sks = [sink_ref[0, 2 * j + hh] for j, hh in heads]
        ms = [jnp.maximum(jnp.max(sc, axis=-1, keepdims=True), sk) for sc, sk in zip(scs, sks)]
        pes = [jnp.exp2(sc - m) for sc, m in zip(scs, ms)]
        pp = [jnp.concatenate([pe.astype(BF16) for pe in pes[per_kv * kv:per_kv * (kv + 1)]], axis=0) for kv in range(KVC)]
        o2 = [_dot(pp[kv], vws[kv]) for kv in range(KVC)]
        outs = [o2[u // per_kv][qs * (u % per_kv):qs * (u % per_kv + 1)]
                / (jnp.sum(pes[u], axis=-1, keepdims=True) + jnp.exp2(sks[u] - ms[u])) for u in range(len(heads))]
        for j in range(HC // 2):
            o_ref[rows, 128 * j:128 * (j + 1)] = jnp.where(low, outs[2 * j], outs[2 * j + 1]).astype(BF16)


def _route(logits_t):
    a = [logits_t[4 * j:4 * j + 4] for j in range(EXP_PER_GROUP)]

    def first_argmax(vals, m):
        idx = jnp.full(m.shape, float(len(vals) - 1), F32)
        for j in reversed(range(len(vals) - 1)):
            idx = jnp.where(vals[j] == m, float(j), idx)
        return idx

    m1 = functools.reduce(jnp.maximum, a)
    i1 = first_argmax(a, m1)
    bsec = [jnp.where(i1 == float(j), -jnp.inf, a[j]) for j in range(EXP_PER_GROUP)]
    m2 = functools.reduce(jnp.maximum, bsec)
    i2 = first_argmax(bsec, m2)
    rows = lambda x: [x[gi:gi + 1] for gi in range(N_GROUPS)]
    gm = functools.reduce(jnp.maximum, rows(m1))
    gscore = jnp.exp(m1 - gm) + jnp.exp(m2 - gm)
    gs = rows(gscore)
    gsel = first_argmax(gs, functools.reduce(jnp.maximum, gs))

    def pick(x):
        xr = rows(x)
        out = xr[N_GROUPS - 1]
        for gi in reversed(range(N_GROUPS - 1)):
            out = jnp.where(gsel == float(gi), xr[gi], out)
        return out

    p1 = jnp.exp(pick(m1) - gm)
    p2 = jnp.exp(pick(m2) - gm)
    w1 = p1 / (p1 + p2)
    w2 = p2 / (p1 + p2)
    s1, s2 = pick(i1), pick(i2)
    lo, hi = jnp.minimum(s1, s2), jnp.maximum(s1, s2)
    pair = jnp.where(lo == 0.0, hi - 1.0, jnp.where(lo == 1.0, hi + 1.0, 5.0))
    bucket = gsel * float(N_PAIRS) + pair
    first_is_lo = s1 < s2
    return bucket, jnp.where(first_is_lo, w1, w2), jnp.where(first_is_lo, w2, w1)


def _outproj_kernel(*refs, n_x, n_o, n_prompt_tiles):
    x_refs = refs[:n_x]
    o_refs = refs[n_x:n_x + n_o]
    w_refs = refs[n_x + n_o:n_x + 2 * n_o]
    (gate_ref, nf_ref, sh_ref, sc_ref, wr_ref, br_ref, tri_ref,
     xn_ref, disp_ref, meta_ref, cnt_ref, run_ref) = refs[n_x + 2 * n_o:]
    t = xn_ref.shape[0]

    @pl.when(pl.program_id(0) == 0)
    def _():
        run_ref[...] = jnp.zeros_like(run_ref)

    x_src = x_refs[0]
    if n_x == 2:
        def stage(x_ref):
            xn_ref[...] = x_ref[...]

        _on_token_tile(x_refs[0], x_refs[1], n_prompt_tiles, stage)
        x_src = xn_ref

    halves = [slice(0, t // 2), slice(t // 2, t)]
    grp = [slice(0, t // (2 * CHUNK)), slice(t // (2 * CHUNK), t // CHUNK)]
    ys = []
    for rs in halves:
        y = _dot(o_refs[0][rs, :], w_refs[0][...])
        for i in range(1, n_o):
            y = y + _dot(o_refs[i][rs, :], w_refs[i][...])
        ys.append(y)
    gate, shift, scale = gate_ref[...], sh_ref[...], sc_ref[...]
    gys = [_group_affine(y, gate[gs], None) for y, gs in zip(ys, grp)]

    for rs, gy in zip(halves, gys):
        xn_ref[rs, :] = x_src[rs, :] + gy
    hs = [_norm_mod(xn_ref[rs, :], nf_ref[...], shift[gs], scale[gs]) for rs, gs in zip(halves, grp)]
    for rs, h in zip(halves, hs):
        for s, slab in enumerate(_pack_pairs(h)):
            disp_ref[s, rs, :] = slab
    logits_t = [(_dot3_narrow(h, wr_ref[...]) + br_ref[...]).T[0:N_EXPERTS] for h in hs]
    bucket, w_lo, w_hi = _route(jnp.concatenate(logits_t, axis=1))
    r128 = lax.broadcasted_iota(jnp.int32, (128, t), 0)
    tok = (pl.program_id(0) * t + lax.broadcasted_iota(jnp.int32, (1, t), 1)).astype(F32)
    aux = jnp.where(r128 == 0, w_lo, jnp.where(r128 == 1, w_hi, jnp.where(r128 == 2, tok, 0.0))).T
    disp_ref[disp_ref.shape[0] - 1] = pltpu.bitcast(aux, jnp.int32)
    brow = lax.broadcasted_iota(jnp.int32, (BUCKET_ROWS, t), 0).astype(F32)
    onehot = jnp.where(brow == bucket, 1.0, 0.0)
    before = _dot(onehot.astype(BF16), tri_ref[...]) + run_ref[:, 0:1]
    rank = jnp.sum(onehot * before, axis=0, keepdims=True)
    run_ref[...] = run_ref[...] + jnp.sum(onehot, axis=1, keepdims=True)
    cnt_ref[...] = run_ref[...]
    r8 = lax.broadcasted_iota(jnp.int32, (8, t), 0)
    meta_ref[...] = jnp.where(r8 == 0, bucket, jnp.where(r8 == 1, rank, 0.0)).astype(jnp.int32)


def _outproj_call(xs_, os_, ws, mods, nf, wr, br, n_pad):
    d = xs_[0].shape[1]
    n = sum(a.shape[0] for a in xs_)
    npt = xs_[0].shape[0] // TM
    row = lambda i: (i, 0)
    const = lambda i: (0, 0)
    n_o = len(os_)
    in_specs = ((_token_specs(npt, d) if len(xs_) == 2 else [pl.BlockSpec((TM, d), row)])
                + [pl.BlockSpec((TM, o.shape[1]), row) for o in os_]
                + [pl.BlockSpec(w.shape, const) for w in ws]
                + [_mod_spec(GATE_MIX), pl.BlockSpec((1, d), const),
                   _mod_spec(SHIFT_FFN), _mod_spec(SCALE_FFN),
                   pl.BlockSpec(wr.shape, const), pl.BlockSpec(br.shape, const),
                   pl.BlockSpec((TM, TM), const)])
    tri = jnp.asarray(np.triu(np.ones((TM, TM), np.float32), k=1), dtype=BF16)
    return pl.pallas_call(
        functools.partial(_outproj_kernel, n_x=len(xs_), n_o=n_o, n_prompt_tiles=npt),
        grid=(n // TM,),
        in_specs=in_specs,
        out_specs=[pl.BlockSpec((TM, d), row), pl.BlockSpec((DISP_SLABS, TM, 128), lambda i: (0, i, 0)),
                   pl.BlockSpec((8, TM), lambda i: (0, i)), pl.BlockSpec((BUCKET_ROWS, 128), const)],
        out_shape=[jax.ShapeDtypeStruct((n, d), F32), jax.ShapeDtypeStruct((DISP_SLABS, n_pad, 128), jnp.int32),
                   jax.ShapeDtypeStruct((8, n), jnp.int32), jax.ShapeDtypeStruct((BUCKET_ROWS, 128), F32)],
        scratch_shapes=[pltpu.VMEM((BUCKET_ROWS, 128), F32)],
        compiler_params=_cparams(("arbitrary",)),
        name="outproj_router",
    )(*xs_, *os_, *ws, mods, nf, mods, mods, wr, br, tri)


def _sc_mesh():
    return plsc.VectorSubcoreMesh(core_axis_name="core", subcore_axis_name="subcore")


def _sc_scatter_rows(src, idx, n_out):
    r = idx.shape[0]
    k = SC_GROUP
    w_per = r // (SC_WINDOW * SC_WORKERS)
    assert idx.shape == (src.shape[0],) and r % (SC_WINDOW * SC_WORKERS) == 0 and w_per % k == 0
    n_groups = w_per // k

    @functools.partial(
        pl.kernel, out_type=jax.ShapeDtypeStruct((n_out, 128), src.dtype), mesh=_sc_mesh(),
        scratch_types=[pltpu.VMEM((w_per, SC_WINDOW), jnp.int32),
                       pltpu.VMEM((2 * k, SC_WINDOW, 128), src.dtype),
                       pltpu.SemaphoreType.DMA((2,)), pltpu.SemaphoreType.DMA((2,))])
    def copy(x_hbm, i_hbm, o_hbm, ibuf, xbuf, in_sem, out_sem):
        wid = lax.axis_index("core") * (SC_WORKERS // 2) + lax.axis_index("subcore")
        pltpu.sync_copy(i_hbm.at[wid], ibuf)
        first = wid * w_per

        def start_in(g, slot):
            return [pltpu.async_copy(x_hbm.at[pl.ds((first + g * k + c) * SC_WINDOW, SC_WINDOW)],
                                     xbuf.at[slot * k + c], in_sem.at[slot]) for c in range(k)]

        def start_out(g, slot):
            return [pltpu.async_copy(xbuf.at[slot * k + c], o_hbm.at[ibuf.at[g * k + c]], out_sem.at[slot])
                    for c in range(k)]

        pending_in = start_in(0, 0)
        for g in range(n_groups):
            slot = g % 2
            for cp in pending_in:
                cp.wait()
            pending_out = start_out(g, slot)
            if g + 1 < n_groups:
                pending_in = start_in(g + 1, 1 - slot)
            for cp in pending_out:
                cp.wait()

    return copy(src, idx.reshape(SC_WORKERS, w_per, SC_WINDOW))


def _moe_kernel(elo_ref, ehi_ref, nvalid_ref, xs_ref, *refs, n_tok, dump_tiles):
    w_refs, (y_ref, tok_ref) = refs[:4 * MOE_TILES], refs[4 * MOE_TILES:]
    step = pl.program_id(0)
    t = TMO
    tiles = range(MOE_TILES)
    rows = [slice(t * j, t * (j + 1)) for j in tiles]
    auxs = [pltpu.bitcast(xs_ref[Y_SLABS, rows[j], :], F32) for j in tiles]
    r = lax.broadcasted_iota(jnp.int32, (1, t), 1)
    for j in tiles:
        i = step * MOE_TILES + j
        spare = n_tok + (i % dump_tiles) * t + r
        tok = jnp.where(r < nvalid_ref[i], auxs[j].T[2:3, :].astype(jnp.int32), spare)
        for c in range(t // 128):
            tok_ref[j, c:c + 1, :] = tok[:, 128 * c:128 * (c + 1)]

    any_tokens = nvalid_ref[step * MOE_TILES] > 0
    for j in range(1, MOE_TILES):
        any_tokens = jnp.logical_or(any_tokens, nvalid_ref[step * MOE_TILES + j] > 0)

    @pl.when(any_tokens)
    def _():
        units = [(j, e) for j in tiles for e in range(2)]
        hs = [_unpack_pairs([xs_ref[s, rows[j], :] for s in range(Y_SLABS)], BF16) for j in tiles]
        abs_ = [_dot(hs[j], w_refs[4 * j + e][0, 0]) for j, e in units]
        acts = [(ab[:, :D_FF] * _sigmoid(ab[:, :D_FF]) * ab[:, D_FF:]).astype(BF16) for ab in abs_]
        ys = [_dot(act, w_refs[4 * j + 2 + e][0, 0]) for act, (j, e) in zip(acts, units)]
        for j in tiles:
            acc = auxs[j][:, 0:1] * ys[2 * j] + auxs[j][:, 1:2] * ys[2 * j + 1]
            for s, slab in enumerate(_pack_pairs(acc)):
                y_ref[s, rows[j], :] = slab

    @pl.when(jnp.logical_not(any_tokens))
    def _():
        y_ref[...] = jnp.zeros_like(y_ref)


def _moe_call(xs, elo, ehi, nvalid, wgu, wdn, n_tiles, n_tok, dump_tiles):
    d = wgu.shape[2]
    m = MOE_TILES
    assert n_tiles % m == 0
    weight_specs = []
    for j in range(m):
        for shape in ((1, 1, d, 2 * D_FF), (1, 1, D_FF, d)):
            for sel in range(2):
                weight_specs.append(pl.BlockSpec(
                    shape, lambda i, lo, hi, v, j=j, sel=sel: (0, (lo, hi)[sel][m * i + j], 0, 0)))
    weights = [w for _ in range(m) for w in (wgu, wgu, wdn, wdn)]
    return pl.pallas_call(
        functools.partial(_moe_kernel, n_tok=n_tok, dump_tiles=dump_tiles),
        grid_spec=pltpu.PrefetchScalarGridSpec(
            num_scalar_prefetch=3,
            grid=(n_tiles // m,),
            in_specs=[pl.BlockSpec((DISP_SLABS, m * TMO, 128), lambda i, lo, hi, v: (0, i, 0))] + weight_specs,
            out_specs=[pl.BlockSpec((Y_SLABS, m * TMO, 128), lambda i, lo, hi, v: (0, i, 0)),
                       pl.BlockSpec((m, TMO // 128, 128), lambda i, lo, hi, v: (i, 0, 0))]),
        out_shape=[jax.ShapeDtypeStruct((Y_SLABS, n_tiles * TMO, 128), jnp.int32),
                   jax.ShapeDtypeStruct((n_tiles, TMO // 128, 128), jnp.int32)],
        compiler_params=_cparams(("arbitrary",), vmem_mb=VMEM_LIMIT_MOE_MB),
        name="moe_grouped",
    )(elo, ehi, nvalid, xs, *weights)


def _after(x, token):
    return lax.optimization_barrier((x, token))[0]


def _cast_kernel(after_ref, w_ref, o_ref):
    o_ref[...] = w_ref[...].astype(o_ref.dtype)


def _cast_call(w, layer, after):
    _, e, k, n = w.shape
    return pl.pallas_call(
        _cast_kernel,
        grid=(e,),
        in_specs=[pl.BlockSpec(memory_space=pl.ANY), pl.BlockSpec((1, 1, k, n), lambda i: (layer, i, 0, 0))],
        out_specs=pl.BlockSpec((1, 1, k, n), lambda i: (0, i, 0, 0)),
        out_shape=jax.ShapeDtypeStruct((1, e, k, n), BF16),
        compiler_params=_cparams(("parallel",)),
        name="cast_weights",
    )(after, w)


def _moe_layer(disp, meta, counts, w_gate_up, w_down, layer, n, n_pad, sort_rows):
    n_tiles = sort_rows // TMO
    wgu = _cast_call(w_gate_up, layer, counts)
    wdn = _cast_call(w_down, layer, counts)
    cnt = counts[:N_BUCKETS, 0].astype(jnp.int32)
    padded = ((cnt + TMO - 1) // TMO) * TMO
    ends = jnp.cumsum(padded)
    offs = ends - padded
    bucket, rank = meta[0], meta[1]
    pos = rank + jnp.sum(jnp.where(bucket[None, :] == jnp.arange(N_BUCKETS, dtype=jnp.int32)[:, None],
                                   offs[:, None], 0), axis=0)
    tile_start = jnp.arange(n_tiles, dtype=jnp.int32) * TMO
    tile_bucket = jnp.minimum(jnp.sum((tile_start[:, None] >= ends[None, :]).astype(jnp.int32), axis=1), N_BUCKETS - 1)
    pair_lo = np.array([0, 0, 0, 1, 1, 2], np.int32)
    pair_hi = np.array([1, 2, 3, 2, 3, 3], np.int32)
    b_lo = jnp.asarray(np.repeat(np.arange(N_GROUPS), N_PAIRS) * EXP_PER_GROUP + np.tile(pair_lo, N_GROUPS), jnp.int32)
    b_hi = jnp.asarray(np.repeat(np.arange(N_GROUPS), N_PAIRS) * EXP_PER_GROUP + np.tile(pair_hi, N_GROUPS), jnp.int32)
    onehot_tb = (tile_bucket[:, None] == jnp.arange(N_BUCKETS, dtype=jnp.int32)[None, :]).astype(jnp.int32)
    elo = jnp.sum(onehot_tb * b_lo[None, :], axis=1)
    ehi = jnp.sum(onehot_tb * b_hi[None, :], axis=1)
    bucket_end = jnp.sum(onehot_tb * (offs + cnt)[None, :], axis=1)
    nvalid = jnp.where(tile_start < ends[-1], jnp.clip(bucket_end - tile_start, 0, TMO), 0)
    dump = sort_rows + jnp.arange(n_pad - n, dtype=jnp.int32)
    pos_sc = jnp.concatenate([pos, dump])
    total = sort_rows + n_pad - n
    sc_idx = (pos_sc[None, :] + (jnp.arange(DISP_SLABS, dtype=jnp.int32) * total)[:, None]).reshape(-1)
    xs = _sc_scatter_rows(disp.reshape(DISP_SLABS * n_pad, 128), sc_idx, DISP_SLABS * total)
    ys, tok = _moe_call(xs.reshape(DISP_SLABS, total, 128), elo, ehi, nvalid, wgu, wdn, n_tiles,
                        n, (n_pad - n) // TMO)
    back_idx = (tok.reshape(1, sort_rows) + (jnp.arange(Y_SLABS, dtype=jnp.int32) * n_pad)[:, None]).reshape(-1)
    z = _sc_scatter_rows(ys.reshape(Y_SLABS * sort_rows, 128), back_idx, Y_SLABS * n_pad)
    return z.reshape(Y_SLABS, n_pad, 128), tok


def _final_kernel(xn_ref, z_ref, gate_ref, g_ref, yp_ref, ys_ref, *, n_prompt_tiles):
    x = _add_moe(xn_ref, z_ref, gate_ref)
    ms = jnp.mean(x * x, axis=-1, keepdims=True)
    y = x * lax.rsqrt(ms + EPS) * g_ref[...]
    i = pl.program_id(0)

    @pl.when(i < n_prompt_tiles)
    def _():
        yp_ref[...] = y

    @pl.when(i >= n_prompt_tiles)
    def _():
        ys_ref[...] = y


def _final_call(xn, z, mods, g, n_prompt):
    n, d = xn.shape
    npt = n_prompt // TM
    assert n - n_prompt == TM
    return pl.pallas_call(
        functools.partial(_final_kernel, n_prompt_tiles=npt),
        grid=(n // TM,),
        in_specs=[pl.BlockSpec((TM, d), lambda i: (i, 0)), pl.BlockSpec((z.shape[0], TM, 128), lambda i: (0, i, 0)),
                  _mod_spec(GATE_FFN), pl.BlockSpec((1, d), lambda i: (0, 0))],
        out_specs=_token_specs(npt, d),
        out_shape=[jax.ShapeDtypeStruct((n_prompt, d), F32), jax.ShapeDtypeStruct((TM, d), F32)],
        compiler_params=_cparams(("arbitrary",)),
        name="final_norm",
    )(xn, z, mods, g)


def kernel(x_prompt, x_sample, c_prompt, c_sample, state_gla, cache_band_k, cache_band_v, cache_swa_k, cache_swa_v,
           w_ada, b_ada, norm_mix, norm_ffn, norm_final, w_in_even, w_gate_a, b_gate_a, gla_norm, rel_bias_b,
           w_out_even, w_in_odd, sinks_c, w_out_odd, w_router, b_router, w_gate_up, w_down):
    bp, lp, d = x_prompt.shape
    bs, ls_, _ = x_sample.shape
    n_p, n_s = bp * lp, bs * ls_
    n = n_p + n_s
    assert ls_ == CHUNK and n_s == TM and lp % TM == 0 and PAST_LEN % CHUNK == 0

    xp2, xs2 = x_prompt.reshape(n_p, d), x_sample.reshape(n_s, d)

    c16 = jnp.zeros((SEQ_ROWS, d), F32).at[:bp].set(c_prompt).at[bp:bp + bs].set(c_sample)
    mods = _ada_call(c16, w_ada, b_ada)
    seq_of_group = np.concatenate([np.repeat(np.arange(bp), lp // CHUNK), bp + np.arange(bs)])
    mods_g = [mods[l][seq_of_group] for l in range(DEPTH)]

    perm = np.array([4 * (c % 4) + c // 4 for c in range(N_EXPERTS)])
    wr = jnp.zeros((d, 128), F32).at[:, :N_EXPERTS].set(w_router[:, perm])
    br = jnp.zeros((1, 128), F32).at[0, :N_EXPERTS].set(b_router[perm])

    sc_unit = SC_WINDOW * SC_WORKERS * SC_GROUP
    n_pad = n + TMO
    while (DISP_SLABS * n_pad) % sc_unit or (Y_SLABS * n_pad) % TMO or (n_pad - n) % TMO:
        n_pad += TMO
    sort_rows = n + N_BUCKETS * TMO
    while (Y_SLABS * sort_rows) % sc_unit or sort_rows % (MOE_TILES * TMO):
        sort_rows += TMO

    gla_p = gla_s = bk_p = bv_p = bk_s = bv_s = sk_p = sv_p = sk_s = sv_s = None
    xn = z = tok = None
    for l in range(DEPTH):
        i = l // 2
        if l % 2 == 0:
            w = w_in_even[i]
            w_main = jnp.concatenate([w[:, :1536], w[:, 1552:]], axis=1).astype(BF16)
            w_la = jnp.zeros((d, 128), F32).at[:, :GATE_RANK].set(w[:, 1536:1552]).astype(BF16)
            w_gate = jnp.zeros((128, HA * DKA), F32).at[:GATE_RANK].set(w_gate_a[i])
            qa, ka, va, ra, qb, kb, vb, ga = _inproj_even_call(
                xp2, xs2, mods_g[l], norm_mix[l][None], w_main, w_la, w_gate, b_gate_a[i][None])
            xres = [xp2, xs2]
            gn = gla_norm[i][None]
            oa, s_p = _gla_call(qa, ka, va, ga, ra, jnp.zeros((bp, 256, 128), F32), gn, None,
                                n_seq=bp, seq_rows=lp, row0=0, nb=8)
            oa, s_s = _gla_call(qa, ka, va, ga, ra, state_gla[i].reshape(bs, 256, 128), gn, oa,
                                n_seq=bs, seq_rows=ls_, row0=n_p, nb=1)
            gla_p, gla_s = s_p.reshape(1, bp, HA, DKA, DVA), s_s.reshape(1, bs, HA, DKA, DVA)
            pb = N_PREV_B * CHUNK
            tq, g = 512, 2
            ck = cache_band_k[i].reshape(bs * pb, HB * DHB).astype(BF16)
            cv = cache_band_v[i].reshape(bs * pb, HB * DHB).astype(BF16)
            biases = (_band_bias(rel_bias_b[i], g, pb, _band_valid(g, pb)),
                      _band_bias(rel_bias_b[i], g, pb, _band_valid(g, pb, tq // (CHUNK * g))),
                      _band_bias(rel_bias_b[i], 1, pb, _band_valid(1, pb)))
            ob = _attention(_band_kernel, qb, kb, vb, ck, cv, biases, [], [], width=512, kv_width=512, pb=pb,
                            tq=tq, g=g, bp=bp, lp=lp, bs=bs, name="band")
            tail = lambda a: jnp.stack([a[(b + 1) * lp - pb:(b + 1) * lp] for b in range(bp)]).astype(F32).reshape(1, bp, pb, HB, DHB)
            new = lambda a: a[n_p:].astype(F32).reshape(bs, ls_, HB, DHB)
            bk_p, bv_p = tail(kb), tail(vb)
            bk_s = jnp.concatenate([cache_band_k[i][:, ls_:], new(kb)], axis=1)[None]
            bv_s = jnp.concatenate([cache_band_v[i][:, ls_:], new(vb)], axis=1)[None]
            wo = w_out_even[i].astype(BF16)
            os_, ws = [oa, ob], [wo[:HA * DVA], wo[HA * DVA:]]
        else:
            w = _after(w_in_odd[i], tok)
            w_out_l = _after(w_out_odd[i], tok)
            cache_k_l, cache_v_l = _after(cache_swa_k[i], tok), _after(cache_swa_v[i], tok)
            wk, wv = w[:, 1024:1152], w[:, 1152:1280]
            dup = lambda a: jnp.concatenate([a[:, :64], a[:, :64], a[:, 64:], a[:, 64:]], axis=1)
            w_all = jnp.concatenate([w[:, :1024], dup(wk), dup(wv)], axis=1).astype(BF16)
            cos, sin, rope_map = _rope_tables(lp, ls_, bp, bs)
            x, q, k, v = _inproj_odd_call(xn, z, mods_g[l - 1], mods_g[l], norm_mix[l][None], cos, sin, rope_map, w_all)
            xres = [x]
            pb = WINDOW
            tq, g = 512, 2
            sink = sinks_c[i][None] * LOG2E
            sink_spec = [pl.BlockSpec(memory_space=pltpu.SMEM)]
            dupc = lambda c: jnp.concatenate([c[:, :, 0], c[:, :, 0], c[:, :, 1], c[:, :, 1]], axis=-1).reshape(bs * pb, 256).astype(BF16)
            ck, cv = dupc(cache_k_l), dupc(cache_v_l)
            additive = lambda valid: jnp.asarray(np.where(valid, 0.0, -np.inf), F32)
            masks = (additive(_band_valid(g, pb)), additive(_band_valid(g, pb, tq // (CHUNK * g))),
                     additive(_band_valid(1, pb)))
            o = _attention(_swa_kernel, q, k, v, ck, cv, masks, [sink], sink_spec, width=1024, kv_width=256, pb=pb,
                           tq=tq, g=g, bp=bp, lp=lp, bs=bs, name="swa")
            undup = lambda a: jnp.concatenate([a[:, 0:64], a[:, 128:192]], axis=1).astype(F32)
            tail = lambda a: jnp.stack([undup(a[(b + 1) * lp - pb:(b + 1) * lp]) for b in range(bp)]).reshape(1, bp, pb, KVC, DHC)
            new = lambda a: undup(a[n_p:]).reshape(bs, ls_, KVC, DHC)
            sk_p, sv_p = tail(k), tail(v)
            sk_s = jnp.concatenate([cache_swa_k[i][:, ls_:], new(k)], axis=1)[None]
            sv_s = jnp.concatenate([cache_swa_v[i][:, ls_:], new(v)], axis=1)[None]
            os_, ws = [o], [w_out_l.astype(BF16)]
        xn, disp, meta, counts = _outproj_call(xres, os_, ws, mods_g[l], norm_ffn[l][None], wr, br, n_pad)
        z, tok = _moe_layer(disp, meta, counts, w_gate_up, w_down, l, n, n_pad, sort_rows)

    y_prompt, y_sample = _final_call(xn, z, mods_g[DEPTH - 1], norm_final[None], n_p)
    return (y_prompt.reshape(bp, lp, d), y_sample.reshape(bs, ls_, d),
            gla_p, gla_s, bk_p, bv_p, bk_s, bv_s, sk_p, sv_p, sk_s, sv_s)
```

```python
import functools

import numpy as np
import jax
import jax.numpy as jnp
from jax import lax
from jax.experimental import pallas as pl
from jax.experimental.pallas import tpu as pltpu
from jax.experimental.pallas import tpu_sc as plsc

F32 = jnp.float32
BF16 = jnp.bfloat16

D_MODEL = 1024
DEPTH = 2
CHUNK = 64
PAST_LEN = 4096
HA, DKA, DVA = 4, 64, 128
GATE_RANK = 16
GATE_TAU = 16.0
HB, DHB = 8, 64
N_PREV_B = 8
MAX_REL = 128
HC, KVC, DHC = 16, 2, 64
WINDOW = 128
ROPE_THETA = 10000.0
N_EXPERTS = 16
N_GROUPS = 4
EXP_PER_GROUP = 4
D_FF = 512
EPS = 1e-6

N_PAIRS = 6
N_BUCKETS = N_GROUPS * N_PAIRS
BUCKET_ROWS = 32
Y_SLABS = 4
DISP_SLABS = Y_SLABS + 1
TMO = 256
MOE_TILES = 2
SC_WINDOW = 128
SC_WORKERS = 32
SC_GROUP = 3

TM = 512
SEQ_ROWS = 16
SUB = 16
LOG2E = 1.4426950408889634
VMEM_LIMIT_MB = 48
VMEM_LIMIT_MOE_MB = 56


def _cparams(sem, vmem_mb=VMEM_LIMIT_MB):
    return pltpu.CompilerParams(dimension_semantics=sem, vmem_limit_bytes=vmem_mb * 1024 * 1024)


def _dot(a, b):
    return jnp.dot(a, b, preferred_element_type=F32)


def _dot_nt(a, b):
    return lax.dot_general(a, b, (((1,), (1,)), ((), ())), preferred_element_type=F32)


def _split(a):
    hi = a.astype(BF16)
    lo = (a - hi.astype(F32)).astype(BF16)
    return hi, lo


def _dot3(a, b):
    ah, al = _split(a)
    bh, bl = _split(b)
    return _dot(ah, bh) + _dot(ah, bl) + _dot(al, bh)


def _dot3_narrow(a, b):
    ah, al = _split(a)
    bh, bl = _split(b)
    n = b.shape[1]
    p = _dot(ah, jnp.concatenate([bh, bl], axis=1))
    return p[:, :n] + p[:, n:] + _dot(al, bh)


def _sigmoid(x):
    return 1.0 / (1.0 + jnp.exp(-x))


def _group_affine(y, mul, add):
    parts = []
    for gi in range(y.shape[0] // CHUNK):
        p = y[gi * CHUNK:(gi + 1) * CHUNK]
        if mul is not None:
            p = p * mul[gi:gi + 1]
        if add is not None:
            p = p + add[gi:gi + 1]
        parts.append(p)
    return jnp.concatenate(parts, axis=0)


def _norm_mod(x, g, shift, scale):
    ms = jnp.mean(x * x, axis=-1, keepdims=True)
    return _group_affine(x * lax.rsqrt(ms + EPS) * g, 1.0 + scale, shift)


def _mod_spec(part, rows):
    return pl.BlockSpec((TM // CHUNK, D_MODEL), lambda i: (rows(i), part))


SHIFT_MIX, SCALE_MIX, GATE_MIX, SHIFT_FFN, SCALE_FFN, GATE_FFN = range(6)


def _on_token_tile(xp_ref, xs_ref, n_prompt_tiles, body):
    @pl.when(pl.program_id(0) < n_prompt_tiles)
    def _():
        body(xp_ref)

    @pl.when(pl.program_id(0) >= n_prompt_tiles)
    def _():
        body(xs_ref)


def _token_specs(n_prompt_tiles, d):
    return [pl.BlockSpec((TM, d), lambda i: (jnp.minimum(i, n_prompt_tiles - 1), 0)),
            pl.BlockSpec((TM, d), lambda i: (0, 0))]


def _ada_kernel(c_ref, w_ref, b_ref, o_ref):
    c = c_ref[...]
    o_ref[0] = _dot3(c * _sigmoid(c), w_ref[0]) + b_ref[0]


def _ada_call(c16, w_ada, b_ada):
    d = D_MODEL
    tn = 1024
    return pl.pallas_call(
        _ada_kernel,
        grid=(DEPTH, 6 * d // tn),
        in_specs=[pl.BlockSpec((SEQ_ROWS, d), lambda l, j: (0, 0)),
                  pl.BlockSpec((1, d, tn), lambda l, j: (l, 0, j)),
                  pl.BlockSpec((1, 1, tn), lambda l, j: (l, 0, j))],
        out_specs=pl.BlockSpec((1, SEQ_ROWS, tn), lambda l, j: (l, 0, j)),
        out_shape=jax.ShapeDtypeStruct((DEPTH, SEQ_ROWS, 6 * d), F32),
        compiler_params=_cparams(("arbitrary", "arbitrary")),
        name="ada",
    )(c16, w_ada, b_ada.reshape(DEPTH, 1, 6 * d))


def _inproj_even_kernel(xp_ref, xs_ref, sh_ref, sc_ref, g_ref, w_ref, wla_ref, wg_ref, bg_ref,
                        qa_ref, ka_ref, va_ref, ra_ref, qb_ref, kb_ref, vb_ref, ga_ref, *, n_prompt_tiles):
    def body(x_ref):
        t = x_ref.shape[0]
        outs = ((qa_ref, 0, 256, DKA ** -0.5), (ka_ref, 256, 512, None), (va_ref, 512, 1024, None),
                (ra_ref, 1024, 1536, None), (qb_ref, 1536, 2048, DHB ** -0.5 * LOG2E), (kb_ref, 2048, 2560, None),
                (vb_ref, 2560, 3072, None))
        shift, scale_ = sh_ref[...], sc_ref[...]
        halves = [slice(0, t // 2), slice(t // 2, t)]
        grp = [slice(0, t // (2 * CHUNK)), slice(t // (2 * CHUNK), t // CHUNK)]
        hbs = [_norm_mod(x_ref[rs, :], g_ref[...], shift[gs], scale_[gs]).astype(BF16) for rs, gs in zip(halves, grp)]
        for rs, hb in zip(halves, hbs):
            zs = [_dot(hb, w_ref[:, lo:hi]) for _, lo, hi, _ in outs]
            la = _dot(hb, wla_ref[...])
            for z, (o_ref, _, _, scale) in zip(zs, outs):
                o_ref[rs, :] = (z if scale is None else z * scale).astype(BF16)
            gl = _dot3(la, wg_ref[...]) + bg_ref[...]
            ga_ref[rs, :] = -(jnp.maximum(-gl, 0.0) + jnp.log(1.0 + jnp.exp(-jnp.abs(gl)))) * (1.0 / GATE_TAU)

    _on_token_tile(xp_ref, xs_ref, n_prompt_tiles, body)


def _inproj_even_call(xp, xs, mods, mod_rows, g, w_main, w_la, w_gate, b_gate):
    d = xp.shape[1]
    npt = xp.shape[0] // TM
    n = xp.shape[0] + xs.shape[0]
    row = lambda i: (i, 0)
    const = lambda i: (0, 0)
    widths = (256, 256, 512, 512, 512, 512, 512)
    out_shape = [jax.ShapeDtypeStruct((n, w), BF16) for w in widths] + [jax.ShapeDtypeStruct((n, 256), F32)]
    out_specs = [pl.BlockSpec((TM, w), row) for w in widths] + [pl.BlockSpec((TM, 256), row)]
    return pl.pallas_call(
        functools.partial(_inproj_even_kernel, n_prompt_tiles=npt),
        grid=(n // TM,),
        in_specs=_token_specs(npt, d) + [
            _mod_spec(SHIFT_MIX, mod_rows), _mod_spec(SCALE_MIX, mod_rows),
            pl.BlockSpec((1, d), const),
            pl.BlockSpec(w_main.shape, const), pl.BlockSpec(w_la.shape, const),
            pl.BlockSpec(w_gate.shape, const), pl.BlockSpec(b_gate.shape, const)],
        out_specs=out_specs, out_shape=out_shape,
        compiler_params=_cparams(("parallel",)),
        name="inproj_even",
    )(xp, xs, mods, mods, g, w_main, w_la, w_gate, b_gate)


def _rope(x, cos, sin_signed):
    t, w = x.shape
    lane = lax.broadcasted_iota(jnp.int32, (1, w), 1)
    first_half = (lane & 63) < 32
    rot = jnp.where(first_half, pltpu.roll(x, w - 32, 1), pltpu.roll(x, 32, 1))
    reps = w // 128
    return x * jnp.tile(cos, (1, reps)) + rot * jnp.tile(sin_signed, (1, reps))


def _unpack_pairs(slabs, dtype):
    lo = [pltpu.bitcast(s << 16, F32) for s in slabs]
    hi = [pltpu.bitcast(s & jnp.int32(-65536), F32) for s in slabs]
    return jnp.concatenate(lo + hi, axis=1).astype(dtype)


def _pack_pairs(x):
    bits = pltpu.bitcast(x.astype(BF16).astype(F32), jnp.int32)
    half = x.shape[1] // 2
    packed = ((bits[:, :half] >> 16) & jnp.int32(0xFFFF)) | (bits[:, half:] & jnp.int32(-65536))
    return [packed[:, 128 * s:128 * (s + 1)] for s in range(half // 128)]


def _add_moe(xn_ref, z_ref, gate_ref):
    y = _unpack_pairs([z_ref[s] for s in range(z_ref.shape[0])], F32)
    return xn_ref[...] + _group_affine(y, gate_ref[...], None)


def _rope_tables(lp, ls_, bp, bs):
    assert PAST_LEN + ls_ <= lp and lp % 128 == 0 and bs * ls_ == TM
    half = DHC // 2
    inv = ROPE_THETA ** (-jnp.arange(half, dtype=F32) / half)
    inv = jnp.tile(inv, 128 // half)
    sign = jnp.asarray(np.tile(np.repeat([-1.0, 1.0], half), 128 // DHC), F32)
    a = jnp.asarray(np.arange(lp // 128) * 128, F32)[:, None] * inv[None, :]
    b = jnp.asarray(np.arange(128), F32)[:, None] * inv[None, :]
    ca, sa, cb, sb = jnp.cos(a)[:, None], jnp.sin(a)[:, None], jnp.cos(b)[None], jnp.sin(b)[None]
    cos = (ca * cb - sa * sb).reshape(lp, 128)
    sin = ((sa * cb + ca * sb) * sign).reshape(lp, 128)
    with_sample = lambda t: jnp.concatenate([t, jnp.tile(t[PAST_LEN:PAST_LEN + ls_], (bs, 1))], axis=0)
    tiles = lp // TM
    return with_sample(cos), with_sample(sin), lambda i: (jnp.where(i < bp * tiles, i % tiles, tiles), 0)


def _inproj_odd_kernel(xn_ref, z_ref, gate_ref, sh_ref, sc_ref, g_ref, cos_ref, sin_ref, w_ref,
                       x_ref, q_ref, k_ref, v_ref):
    t = xn_ref.shape[0]
    halves = [slice(0, t // 2), slice(t // 2, t)]
    grp = [slice(0, t // (2 * CHUNK)), slice(t // (2 * CHUNK), t // CHUNK)]
    gate, shift, scale = gate_ref[...], sh_ref[...], sc_ref[...]
    xs = []
    for rs, gs in zip(halves, grp):
        y = _unpack_pairs([z_ref[s, rs, :] for s in range(z_ref.shape[0])], F32)
        xs.append(xn_ref[rs, :] + _group_affine(y, gate[gs], None))
    for rs, x in zip(halves, xs):
        x_ref[rs, :] = x
    hbs = [_norm_mod(x, g_ref[...], shift[gs], scale[gs]).astype(BF16) for x, gs in zip(xs, grp)]
    qs = [_dot(hb, w_ref[:, 0:1024]) for hb in hbs]
    ks = [_dot(hb, w_ref[:, 1024:1280]) for hb in hbs]
    vs = [_dot(hb, w_ref[:, 1280:1536]) for hb in hbs]
    for rs, q, k, v in zip(halves, qs, ks, vs):
        cos, sin = cos_ref[rs, :], sin_ref[rs, :]
        q_ref[rs, :] = (_rope(q, cos, sin) * (DHC ** -0.5 * LOG2E)).astype(BF16)
        k_ref[rs, :] = _rope(k, cos, sin).astype(BF16)
        v_ref[rs, :] = v.astype(BF16)


def _inproj_odd_call(xn, z, mods_prev, mods, mod_rows, g, cos, sin, rope_map, w):
    n, d = xn.shape
    row = lambda i: (i, 0)
    const = lambda i: (0, 0)
    widths = (1024, 256, 256)
    return pl.pallas_call(
        _inproj_odd_kernel,
        grid=(n // TM,),
        in_specs=[pl.BlockSpec((TM, d), row), pl.BlockSpec((z.shape[0], TM, 128), lambda i: (0, i, 0)),
                  _mod_spec(GATE_FFN, mod_rows), _mod_spec(SHIFT_MIX, mod_rows), _mod_spec(SCALE_MIX, mod_rows),
                  pl.BlockSpec((1, d), const),
                  pl.BlockSpec((TM, 128), rope_map), pl.BlockSpec((TM, 128), rope_map),
                  pl.BlockSpec(w.shape, const)],
        out_specs=[pl.BlockSpec((TM, d), row)] + [pl.BlockSpec((TM, wd), row) for wd in widths],
        out_shape=[jax.ShapeDtypeStruct((n, d), F32)] + [jax.ShapeDtypeStruct((n, wd), BF16) for wd in widths],
        compiler_params=_cparams(("parallel",)),
        name="inproj_odd",
    )(xn, z, mods_prev, mods, mods, g, cos, sin, w)


def _gla_tri():
    t = np.arange(CHUNK)[:, None]
    s = np.arange(CHUNK)[None, :]
    cum = s <= t
    start = s < (t // SUB) * SUB
    end = s < (t // SUB + 1) * SUB
    return jnp.asarray(np.concatenate([cum, start, end], axis=0).astype(np.float32), dtype=BF16)


def _gla_kernel(q_ref, k_ref, v_ref, g_ref, r_ref, s0_ref, gn_ref, tri_ref, o_ref, sout_ref, s_ref, *, nb):
    c_ = CHUNK
    nsub = c_ // SUB

    @pl.when(pl.program_id(1) == 0)
    def _():
        s_ref[...] = s0_ref[0]

    tri = tri_ref[...]
    lane = lax.broadcasted_iota(jnp.int32, (1, 128), 1)
    hmask = [jnp.where(lane < DKA, 1.0, 0.0), jnp.where(lane >= DKA, 1.0, 0.0)]
    ti = lax.broadcasted_iota(jnp.int32, (c_, c_), 0)
    si = lax.broadcasted_iota(jnp.int32, (c_, c_), 1)
    rb, cb = ti >> 4, si >> 4
    m_diag = (rb == cb) & (si <= ti)
    m_off = [(cb == j) & (rb > j) for j in range(nsub - 1)]
    hk = HA * DKA
    gn = gn_ref[...]

    chunks = range(nb)
    heads = [(p, hh) for p in range(HA // 2) for hh in range(2)]
    rows = [slice(c * c_, (c + 1) * c_) for c in chunks]
    pair = [slice(128 * p, 128 * (p + 1)) for p in range(HA // 2)]
    css = []
    for c in chunks:
        g_hi, g_lo = _split(g_ref[rows[c], :])
        css.append(_dot(tri, g_hi) + _dot(tri, g_lo))
    lhs1, lhs2, kds, kes, q_inter, klts, dcols = [], [], [], [], [], [], []
    for c in chunks:
        b, rs, re = css[c][0:c_], css[c][c_:2 * c_], css[c][2 * c_:3 * c_]
        q = q_ref[rows[c], :].astype(F32)
        k = k_ref[rows[c], :].astype(F32)
        bl = b[c_ - 1:c_, :]
        qd = q * jnp.exp(b - rs)
        kd = k * jnp.exp(rs - b)
        ke = k * jnp.exp(re - b)
        qi = q * jnp.exp(b)
        kl = k * jnp.exp(bl - b)
        ql = [q * jnp.exp(jnp.minimum(b - b[SUB * (j + 1) - 1:SUB * (j + 1), :], 0.0)) for j in range(nsub - 1)]
        dcols.append(jnp.broadcast_to(jnp.exp(bl), (8, hk)).T[:, 0:1])
        kds.append([(kd[:, pair[p]] * hmask[hh]).astype(BF16) for p, hh in heads])
        kes.append([(ke[:, pair[p]] * hmask[hh]).astype(BF16) for p, hh in heads])
        klts.append([kl[:, ls].T.astype(BF16) for ls in pair])
        lhs1.append([qd[:, ls].astype(BF16) for ls in pair])
        lhs2.append([jnp.concatenate([ql[j][:, ls] for j in range(nsub - 1)], axis=0).astype(BF16) for ls in pair])
        q_inter.append([(qi[:, pair[p]] * hmask[hh]).astype(BF16) for p, hh in heads])
    a1s = [[_dot_nt(lhs1[c][p], kds[c][h]) for h, (p, hh) in enumerate(heads)] for c in chunks]
    a2s = [[_dot_nt(lhs2[c][p], kes[c][h]) for h, (p, hh) in enumerate(heads)] for c in chunks]
    atts = []
    for c in chunks:
        per_head = []
        for h in range(HA):
            att = jnp.zeros((c_, c_), F32)
            for j in reversed(range(nsub - 1)):
                att = jnp.where(m_off[j], a2s[c][h][j * c_:(j + 1) * c_], att)
            per_head.append(jnp.where(m_diag, a1s[c][h], att).astype(BF16))
        atts.append(per_head)
    vs_ = [[v_ref[rows[c], DVA * h:DVA * (h + 1)] for h in range(HA)] for c in chunks]
    o_intra = [[_dot(atts[c][h], vs_[c][h]) for h in range(HA)] for c in chunks]
    upds = [jnp.concatenate([_dot(klts[c][p][DKA * hh:DKA * (hh + 1)], vs_[c][2 * p + hh]) for p, hh in heads], axis=0)
            for c in chunks]

    s_cur = s_ref[...]
    s_in = []
    for c in chunks:
        s_in.append(s_cur.astype(BF16))
        s_cur = dcols[c] * s_cur + upds[c]
    s_ref[...] = s_cur
    sout_ref[0] = s_cur

    for c in chunks:
        for h in range(HA):
            o = o_intra[c][h] + _dot(q_inter[c][h], s_in[c][pair[h // 2], :])
            ms = jnp.mean(o * o, axis=-1, keepdims=True)
            vs = slice(DVA * h, DVA * (h + 1))
            rr = r_ref[rows[c], vs].astype(F32)
            o_ref[rows[c], vs] = (o * lax.rsqrt(ms + EPS) * gn * (rr * _sigmoid(rr))).astype(BF16)


def _gla_call(q, k, v, g, r, s0, gn, o_prev, *, n_seq, seq_rows, row0, nb):
    tq = nb * CHUNK
    steps = seq_rows // tq
    blk0 = row0 // tq
    row = lambda b, j: (blk0 + b * steps + j, 0)
    const = lambda b, j: (0, 0)
    tri = _gla_tri()
    in_specs = [pl.BlockSpec((tq, 256), row), pl.BlockSpec((tq, 256), row), pl.BlockSpec((tq, 512), row),
                pl.BlockSpec((tq, 256), row), pl.BlockSpec((tq, 512), row),
                pl.BlockSpec((1, 256, 128), lambda b, j: (b, 0, 0)),
                pl.BlockSpec((1, 128), const), pl.BlockSpec(tri.shape, const)]
    args = [q, k, v, g, r, s0, gn, tri]
    aliases = {}
    if o_prev is not None:
        in_specs.append(pl.BlockSpec(memory_space=pl.ANY))
        args.append(o_prev)
        aliases = {len(args) - 1: 0}
    kern = functools.partial(_gla_kernel, nb=nb)
    if o_prev is not None:
        kern = _drop_arg(kern, 8)
    return pl.pallas_call(
        kern,
        grid=(n_seq, steps),
        in_specs=in_specs,
        out_specs=[pl.BlockSpec((tq, 512), row), pl.BlockSpec((1, 256, 128), lambda b, j: (b, 0, 0))],
        out_shape=[jax.ShapeDtypeStruct((q.shape[0], 512), BF16), jax.ShapeDtypeStruct((n_seq, 256, 128), F32)],
        scratch_shapes=[pltpu.VMEM((256, 128), F32)],
        input_output_aliases=aliases,
        compiler_params=_cparams(("arbitrary", "arbitrary")),
        name="gla",
    )(*args)


def _drop_arg(fn, idx):
    def wrapped(*refs):
        return fn(*refs[:idx], *refs[idx + 1:])
    return wrapped


def _window(prev_ref, cur_ref, lo, hi, pb, ls):
    if lo < pb:
        return jnp.concatenate([prev_ref[lo:pb, ls], cur_ref[0:hi - pb, ls]], axis=0)
    return cur_ref[lo - pb:hi - pb, ls]


def _band_kernel(q_ref, kp_ref, kc_ref, vp_ref, vc_ref, bias_ref, o_ref, *, g, n_sub, pb):
    qs = CHUNK * g
    kw_rows = pb + qs
    lane = lax.broadcasted_iota(jnp.int32, (1, 128), 1)
    low = lane < DHB
    hmask = [jnp.where(low, 1.0, 0.0), jnp.where(low, 0.0, 1.0)]
    for s in range(n_sub):
        sb = s if bias_ref.shape[0] > 1 else 0
        rows = slice(qs * s, qs * (s + 1))
        lanes = [slice(128 * p, 128 * (p + 1)) for p in range(HB // 2)]
        heads = [(p, hh) for p in range(HB // 2) for hh in range(2)]
        qps = [q_ref[rows, ls].astype(F32) for ls in lanes]
        kws = [_window(kp_ref, kc_ref, qs * s, qs * s + kw_rows, pb, ls) for ls in lanes]
        vws = [_window(vp_ref, vc_ref, qs * s, qs * s + kw_rows, pb, ls) for ls in lanes]
        qq = [jnp.concatenate([(qps[p] * hmask[hh]).astype(BF16) for hh in range(2)], axis=0) for p in range(HB // 2)]
        sc2 = [_dot_nt(qq[p], kws[p]) for p in range(HB // 2)]
        scs = [sc2[p][qs * hh:qs * (hh + 1)] + bias_ref[sb, 2 * p + hh] for p, hh in heads]
        pes = [jnp.exp2(sc - jnp.max(sc, axis=-1, keepdims=True)) for sc in scs]
        pp = [jnp.concatenate([pes[2 * p + hh].astype(BF16) for hh in range(2)], axis=0) for p in range(HB // 2)]
        o2 = [_dot(pp[p], vws[p]) for p in range(HB // 2)]
        outs = [o2[p][qs * hh:qs * (hh + 1)] / jnp.sum(pes[2 * p + hh], axis=-1, keepdims=True) for p, hh in heads]
        for p, ls in enumerate(lanes):
            o_ref[rows, ls] = jnp.where(low, outs[2 * p], outs[2 * p + 1]).astype(BF16)


def _band_valid(g, pb, n_sub=None):
    rows, kw = CHUNK * g, pb + CHUNK * g
    r = np.arange(rows)[:, None]
    c = np.arange(kw)[None, :]
    dd = c // CHUNK - r // CHUNK
    band = (dd >= 0) & (dd <= pb // CHUNK)
    if n_sub is None:
        return band[None]
    return np.stack([band & (c >= pb - rows * s) for s in range(n_sub)])


def _band_bias(table, g, pb, valid):
    rows, kw = CHUNK * g, pb + CHUNK * g
    period = kw + rows
    m = np.arange(period)
    m = np.where(m < kw, m, m - period)
    ext = table[:, np.clip(m - pb, -MAX_REL, MAX_REL) + MAX_REL] * LOG2E
    flat = jnp.tile(ext, (1, rows))[:, :rows * (period - 1)]
    bias = flat.reshape(table.shape[0], rows, period - 1)[:, :, :kw]
    return jnp.where(valid[:, None], bias[None], -jnp.inf)


def _attn_call(kernel, q, kp, kc, vp, vc, extra, extra_specs, o_prev, *, width, kv_width, tq, pb,
               n_blocks, blk_map, prev_map, name):
    row = lambda i: (blk_map(i), 0)
    prev = lambda i: (prev_map(i), 0)
    in_specs = [pl.BlockSpec((tq, width), row),
                pl.BlockSpec((pb, kv_width), prev), pl.BlockSpec((tq, kv_width), row),
                pl.BlockSpec((pb, kv_width), prev), pl.BlockSpec((tq, kv_width), row)] + extra_specs
    args = [q, kp, kc, vp, vc] + extra
    aliases = {}
    if o_prev is not None:
        in_specs.append(pl.BlockSpec(memory_space=pl.ANY))
        args.append(o_prev)
        aliases = {len(args) - 1: 0}
        kernel = _drop_arg(kernel, len(args) - 1)
    return pl.pallas_call(
        kernel,
        grid=(n_blocks,),
        in_specs=in_specs,
        out_specs=pl.BlockSpec((tq, width), row),
        out_shape=jax.ShapeDtypeStruct((q.shape[0], width), BF16),
        input_output_aliases=aliases,
        compiler_params=_cparams(("parallel",)),
        name=name,
    )(*args)


def _attention(kernel_fn, q, k, v, cache_k, cache_v, masks, extra, extra_specs, *, width, kv_width, pb, tq, g,
               bp, lp, bs, name):
    bps = lp // tq
    n_sub = tq // (CHUNK * g)
    spec = lambda a: [pl.BlockSpec(a.shape, lambda i: (0,) * a.ndim)]
    kern = functools.partial(kernel_fn, g=g, n_sub=n_sub, pb=pb)
    common = dict(width=width, kv_width=kv_width, pb=pb)
    main = lambda i: (i // (bps - 1)) * bps + i % (bps - 1) + 1
    o = _attn_call(kern, q, k, k, v, v, [masks[0]] + extra, spec(masks[0]) + extra_specs, None, tq=tq,
                   n_blocks=bp * (bps - 1), blk_map=main, prev_map=lambda i: main(i) * (tq // pb) - 1,
                   name=name + "_main", **common)
    first = lambda i: i * bps
    o = _attn_call(kern, q, k, k, v, v, [masks[1]] + extra, spec(masks[1]) + extra_specs, o, tq=tq,
                   n_blocks=bp, blk_map=first, prev_map=lambda i: jnp.maximum(first(i) * (tq // pb) - 1, 0),
                   name=name + "_first", **common)
    samp = functools.partial(kernel_fn, g=1, n_sub=1, pb=pb)
    return _attn_call(samp, q, cache_k, k, cache_v, v, [masks[2]] + extra, spec(masks[2]) + extra_specs, o, tq=CHUNK,
                      n_blocks=bs, blk_map=lambda i: bp * lp // CHUNK + i, prev_map=lambda i: i,
                      name=name + "_sample", **common)


def _swa_kernel(q_ref, kp_ref, kc_ref, vp_ref, vc_ref, mask_ref, sink_ref, o_ref, *, g, n_sub, pb):
    qs = CHUNK * g
    kw_rows = pb + qs
    lane = lax.broadcasted_iota(jnp.int32, (1, 128), 1)
    low = lane < DHC
    hmask = [jnp.where(low, 1.0, 0.0), jnp.where(low, 0.0, 1.0)]
    pairs_per_kv = HC // KVC // 2
    for s in range(n_sub):
        msk = mask_ref[s if mask_ref.shape[0] > 1 else 0]
        rows = slice(qs * s, qs * (s + 1))
        kws = [_window(kp_ref, kc_ref, qs * s, qs * s + kw_rows, pb, slice(128 * kv, 128 * (kv + 1))) for kv in range(KVC)]
        vws = [_window(vp_ref, vc_ref, qs * s, qs * s + kw_rows, pb, slice(128 * kv, 128 * (kv + 1))) for kv in range(KVC)]
        heads = [(j, hh) for j in range(HC // 2) for hh in range(2)]
        qps = [q_ref[rows, 128 * j:128 * (j + 1)].astype(F32) for j in range(HC // 2)]
        per_kv = 2 * pairs_per_kv
        qq = [jnp.concatenate([(qps[j] * hmask[hh]).astype(BF16) for j, hh in heads[per_kv * kv:per_kv * (kv + 1)]],
                              axis=0) for kv in range(KVC)]
        sc2 = [_dot_nt(qq[kv], kws[kv]) for kv in range(KVC)]
        scs = [sc2[u // per_kv][qs * (u % per_kv):qs * (u % per_kv + 1)] + msk for u in range(len(heads))]
        sks = [sink_ref[0, 2 * j + hh] for j, hh in heads]
        ms = [jnp.maximum(jnp.max(sc, axis=-1, keepdims=True), sk) for sc, sk in zip(scs, sks)]
        pes = [jnp.exp2(sc - m) for sc, m in zip(scs, ms)]
        pp = [jnp.concatenate([pe.astype(BF16) for pe in pes[per_kv * kv:per_kv * (kv + 1)]], axis=0) for kv in range(KVC)]
        o2 = [_dot(pp[kv], vws[kv]) for kv in range(KVC)]
        outs = [o2[u // per_kv][qs * (u % per_kv):qs * (u % per_kv + 1)]
                / (jnp.sum(pes[u], axis=-1, keepdims=True) + jnp.exp2(sks[u] - ms[u])) for u in range(len(heads))]
        for j in range(HC // 2):
            o_ref[rows, 128 * j:128 * (j + 1)] = jnp.where(low, outs[2 * j], outs[2 * j + 1]).astype(BF16)


def _route(logits_t):
    a = [logits_t[4 * j:4 * j + 4] for j in range(EXP_PER_GROUP)]

    def first_argmax(vals, m):
        idx = jnp.full(m.shape, float(len(vals) - 1), F32)
        for j in reversed(range(len(vals) - 1)):
            idx = jnp.where(vals[j] == m, float(j), idx)
        return idx

    m1 = functools.reduce(jnp.maximum, a)
    i1 = first_argmax(a, m1)
    bsec = [jnp.where(i1 == float(j), -jnp.inf, a[j]) for j in range(EXP_PER_GROUP)]
    m2 = functools.reduce(jnp.maximum, bsec)
    i2 = first_argmax(bsec, m2)
    rows = lambda x: [x[gi:gi + 1] for gi in range(N_GROUPS)]
    gm = functools.reduce(jnp.maximum, rows(m1))
    gscore = jnp.exp(m1 - gm) + jnp.exp(m2 - gm)
    gs = rows(gscore)
    gsel = first_argmax(gs, functools.reduce(jnp.maximum, gs))

    def pick(x):
        xr = rows(x)
        out = xr[N_GROUPS - 1]
        for gi in reversed(range(N_GROUPS - 1)):
            out = jnp.where(gsel == float(gi), xr[gi], out)
        return out

    p1 = jnp.exp(pick(m1) - gm)
    p2 = jnp.exp(pick(m2) - gm)
    w1 = p1 / (p1 + p2)
    w2 = p2 / (p1 + p2)
    s1, s2 = pick(i1), pick(i2)
    lo, hi = jnp.minimum(s1, s2), jnp.maximum(s1, s2)
    pair = jnp.where(lo == 0.0, hi - 1.0, jnp.where(lo == 1.0, hi + 1.0, 5.0))
    bucket = gsel * float(N_PAIRS) + pair
    first_is_lo = s1 < s2
    return bucket, jnp.where(first_is_lo, w1, w2), jnp.where(first_is_lo, w2, w1)


def _outproj_kernel(*refs, n_x, n_o, n_prompt_tiles):
    x_refs = refs[:n_x]
    o_refs = refs[n_x:n_x + n_o]
    w_refs = refs[n_x + n_o:n_x + 2 * n_o]
    (gate_ref, nf_ref, sh_ref, sc_ref, wr_ref, br_ref, tri_ref,
     xn_ref, disp_ref, meta_ref, cnt_ref, run_ref) = refs[n_x + 2 * n_o:]
    t = xn_ref.shape[0]

    @pl.when(pl.program_id(0) == 0)
    def _():
        run_ref[...] = jnp.zeros_like(run_ref)

    x_src = x_refs[0]
    if n_x == 2:
        def stage(x_ref):
            xn_ref[...] = x_ref[...]

        _on_token_tile(x_refs[0], x_refs[1], n_prompt_tiles, stage)
        x_src = xn_ref

    halves = [slice(0, t // 2), slice(t // 2, t)]
    grp = [slice(0, t // (2 * CHUNK)), slice(t // (2 * CHUNK), t // CHUNK)]
    ys = []
    for rs in halves:
        y = _dot(o_refs[0][rs, :], w_refs[0][...])
        for i in range(1, n_o):
            y = y + _dot(o_refs[i][rs, :], w_refs[i][...])
        ys.append(y)
    gate, shift, scale = gate_ref[...], sh_ref[...], sc_ref[...]
    gys = [_group_affine(y, gate[gs], None) for y, gs in zip(ys, grp)]

    for rs, gy in zip(halves, gys):
        xn_ref[rs, :] = x_src[rs, :] + gy
    hs = [_norm_mod(xn_ref[rs, :], nf_ref[...], shift[gs], scale[gs]) for rs, gs in zip(halves, grp)]
    for rs, h in zip(halves, hs):
        for s, slab in enumerate(_pack_pairs(h)):
            disp_ref[s, rs, :] = slab
    logits_t = [(_dot3_narrow(h, wr_ref[...]) + br_ref[...]).T[0:N_EXPERTS] for h in hs]
    bucket, w_lo, w_hi = _route(jnp.concatenate(logits_t, axis=1))
    r128 = lax.broadcasted_iota(jnp.int32, (128, t), 0)
    tok = (pl.program_id(0) * t + lax.broadcasted_iota(jnp.int32, (1, t), 1)).astype(F32)
    aux = jnp.where(r128 == 0, w_lo, jnp.where(r128 == 1, w_hi, jnp.where(r128 == 2, tok, 0.0))).T
    disp_ref[disp_ref.shape[0] - 1] = pltpu.bitcast(aux, jnp.int32)
    brow = lax.broadcasted_iota(jnp.int32, (BUCKET_ROWS, t), 0).astype(F32)
    onehot = jnp.where(brow == bucket, 1.0, 0.0)
    before = _dot(onehot.astype(BF16), tri_ref[...]) + run_ref[:, 0:1]
    rank = jnp.sum(onehot * before, axis=0, keepdims=True)
    run_ref[...] = run_ref[...] + jnp.sum(onehot, axis=1, keepdims=True)
    cnt_ref[...] = run_ref[...]
    r8 = lax.broadcasted_iota(jnp.int32, (8, t), 0)
    meta_ref[...] = jnp.where(r8 == 0, bucket, jnp.where(r8 == 1, rank, 0.0)).astype(jnp.int32)


def _outproj_call(xs_, os_, ws, mods, mod_rows, nf, wr, br, n_pad):
    d = xs_[0].shape[1]
    n = sum(a.shape[0] for a in xs_)
    npt = xs_[0].shape[0] // TM
    row = lambda i: (i, 0)
    const = lambda i: (0, 0)
    n_o = len(os_)
    in_specs = ((_token_specs(npt, d) if len(xs_) == 2 else [pl.BlockSpec((TM, d), row)])
                + [pl.BlockSpec((TM, o.shape[1]), row) for o in os_]
                + [pl.BlockSpec(w.shape, const) for w in ws]
                + [_mod_spec(GATE_MIX, mod_rows), pl.BlockSpec((1, d), const),
                   _mod_spec(SHIFT_FFN, mod_rows), _mod_spec(SCALE_FFN, mod_rows),
                   pl.BlockSpec(wr.shape, const), pl.BlockSpec(br.shape, const),
                   pl.BlockSpec((TM, TM), const)])
    tri = jnp.asarray(np.triu(np.ones((TM, TM), np.float32), k=1), dtype=BF16)
    return pl.pallas_call(
        functools.partial(_outproj_kernel, n_x=len(xs_), n_o=n_o, n_prompt_tiles=npt),
        grid=(n // TM,),
        in_specs=in_specs,
        out_specs=[pl.BlockSpec((TM, d), row), pl.BlockSpec((DISP_SLABS, TM, 128), lambda i: (0, i, 0)),
                   pl.BlockSpec((8, TM), lambda i: (0, i)), pl.BlockSpec((BUCKET_ROWS, 128), const)],
        out_shape=[jax.ShapeDtypeStruct((n, d), F32), jax.ShapeDtypeStruct((DISP_SLABS, n_pad, 128), jnp.int32),
                   jax.ShapeDtypeStruct((8, n), jnp.int32), jax.ShapeDtypeStruct((BUCKET_ROWS, 128), F32)],
        scratch_shapes=[pltpu.VMEM((BUCKET_ROWS, 128), F32)],
        compiler_params=_cparams(("arbitrary",)),
        name="outproj_router",
    )(*xs_, *os_, *ws, mods, nf, mods, mods, wr, br, tri)


def _sc_mesh():
    return plsc.VectorSubcoreMesh(core_axis_name="core", subcore_axis_name="subcore")


def _sc_scatter_rows(src, idx, n_out):
    r = idx.shape[0]
    k = SC_GROUP
    w_per = r // (SC_WINDOW * SC_WORKERS)
    assert idx.shape == (src.shape[0],) and r % (SC_WINDOW * SC_WORKERS) == 0 and w_per % k == 0
    n_groups = w_per // k

    @functools.partial(
        pl.kernel, out_type=jax.ShapeDtypeStruct((n_out, 128), src.dtype), mesh=_sc_mesh(),
        scratch_types=[pltpu.VMEM((w_per, SC_WINDOW), jnp.int32),
                       pltpu.VMEM((2 * k, SC_WINDOW, 128), src.dtype),
                       pltpu.SemaphoreType.DMA((2,)), pltpu.SemaphoreType.DMA((2,))])
    def copy(x_hbm, i_hbm, o_hbm, ibuf, xbuf, in_sem, out_sem):
        wid = lax.axis_index("core") * (SC_WORKERS // 2) + lax.axis_index("subcore")
        pltpu.sync_copy(i_hbm.at[wid], ibuf)
        first = wid * w_per

        def start_in(g, slot):
            return [pltpu.async_copy(x_hbm.at[pl.ds((first + g * k + c) * SC_WINDOW, SC_WINDOW)],
                                     xbuf.at[slot * k + c], in_sem.at[slot]) for c in range(k)]

        def start_out(g, slot):
            return [pltpu.async_copy(xbuf.at[slot * k + c], o_hbm.at[ibuf.at[g * k + c]], out_sem.at[slot])
                    for c in range(k)]

        pending_in = start_in(0, 0)
        for g in range(n_groups):
            slot = g % 2
            for cp in pending_in:
                cp.wait()
            pending_out = start_out(g, slot)
            if g + 1 < n_groups:
                pending_in = start_in(g + 1, 1 - slot)
            for cp in pending_out:
                cp.wait()

    return copy(src, idx.reshape(SC_WORKERS, w_per, SC_WINDOW))


def _moe_kernel(elo_ref, ehi_ref, nvalid_ref, xs_ref, *refs, n_tok, dump_tiles):
    w_refs, (y_ref, tok_ref) = refs[:4 * MOE_TILES], refs[4 * MOE_TILES:]
    step = pl.program_id(0)
    t = TMO
    tiles = range(MOE_TILES)
    rows = [slice(t * j, t * (j + 1)) for j in tiles]
    auxs = [pltpu.bitcast(xs_ref[Y_SLABS, rows[j], :], F32) for j in tiles]
    r = lax.broadcasted_iota(jnp.int32, (1, t), 1)
    for j in tiles:
        i = step * MOE_TILES + j
        spare = n_tok + (i % dump_tiles) * t + r
        tok = jnp.where(r < nvalid_ref[i], auxs[j].T[2:3, :].astype(jnp.int32), spare)
        for c in range(t // 128):
            tok_ref[j, c:c + 1, :] = tok[:, 128 * c:128 * (c + 1)]

    any_tokens = nvalid_ref[step * MOE_TILES] > 0
    for j in range(1, MOE_TILES):
        any_tokens = jnp.logical_or(any_tokens, nvalid_ref[step * MOE_TILES + j] > 0)

    @pl.when(any_tokens)
    def _():
        units = [(j, e) for j in tiles for e in range(2)]
        hs = [_unpack_pairs([xs_ref[s, rows[j], :] for s in range(Y_SLABS)], BF16) for j in tiles]
        abs_ = [_dot(hs[j], w_refs[4 * j + e][0, 0]) for j, e in units]
        acts = [(ab[:, :D_FF] * _sigmoid(ab[:, :D_FF]) * ab[:, D_FF:]).astype(BF16) for ab in abs_]
        ys = [_dot(act, w_refs[4 * j + 2 + e][0, 0]) for act, (j, e) in zip(acts, units)]
        for j in tiles:
            acc = auxs[j][:, 0:1] * ys[2 * j] + auxs[j][:, 1:2] * ys[2 * j + 1]
            for s, slab in enumerate(_pack_pairs(acc)):
                y_ref[s, rows[j], :] = slab

    @pl.when(jnp.logical_not(any_tokens))
    def _():
        y_ref[...] = jnp.zeros_like(y_ref)


def _moe_call(xs, elo, ehi, nvalid, wgu, wdn, n_tiles, n_tok, dump_tiles):
    d = wgu.shape[2]
    m = MOE_TILES
    assert n_tiles % m == 0
    weight_specs = []
    for j in range(m):
        for shape in ((1, 1, d, 2 * D_FF), (1, 1, D_FF, d)):
            for sel in range(2):
                weight_specs.append(pl.BlockSpec(
                    shape, lambda i, lo, hi, v, j=j, sel=sel: (0, (lo, hi)[sel][m * i + j], 0, 0)))
    weights = [w for _ in range(m) for w in (wgu, wgu, wdn, wdn)]
    return pl.pallas_call(
        functools.partial(_moe_kernel, n_tok=n_tok, dump_tiles=dump_tiles),
        grid_spec=pltpu.PrefetchScalarGridSpec(
            num_scalar_prefetch=3,
            grid=(n_tiles // m,),
            in_specs=[pl.BlockSpec((DISP_SLABS, m * TMO, 128), lambda i, lo, hi, v: (0, i, 0))] + weight_specs,
            out_specs=[pl.BlockSpec((Y_SLABS, m * TMO, 128), lambda i, lo, hi, v: (0, i, 0)),
                       pl.BlockSpec((m, TMO // 128, 128), lambda i, lo, hi, v: (i, 0, 0))]),
        out_shape=[jax.ShapeDtypeStruct((Y_SLABS, n_tiles * TMO, 128), jnp.int32),
                   jax.ShapeDtypeStruct((n_tiles, TMO // 128, 128), jnp.int32)],
        compiler_params=_cparams(("arbitrary",), vmem_mb=VMEM_LIMIT_MOE_MB),
        name="moe_grouped",
    )(elo, ehi, nvalid, xs, *weights)


def _after(x, token):
    return lax.optimization_barrier((x, token))[0]


def _cast_kernel(after_ref, w_ref, o_ref):
    o_ref[...] = w_ref[...].astype(o_ref.dtype)


def _cast_call(w, layer, after):
    _, e, k, n = w.shape
    return pl.pallas_call(
        _cast_kernel,
        grid=(e,),
        in_specs=[pl.BlockSpec(memory_space=pl.ANY), pl.BlockSpec((1, 1, k, n), lambda i: (layer, i, 0, 0))],
        out_specs=pl.BlockSpec((1, 1, k, n), lambda i: (0, i, 0, 0)),
        out_shape=jax.ShapeDtypeStruct((1, e, k, n), BF16),
        compiler_params=_cparams(("parallel",)),
        name="cast_weights",
    )(after, w)


def _moe_layer(disp, meta, counts, w_gate_up, w_down, layer, n, n_pad, sort_rows):
    n_tiles = sort_rows // TMO
    wgu = _cast_call(w_gate_up, layer, counts)
    wdn = _cast_call(w_down, layer, counts)
    cnt = counts[:N_BUCKETS, 0].astype(jnp.int32)
    padded = ((cnt + TMO - 1) // TMO) * TMO
    ends = jnp.cumsum(padded)
    offs = ends - padded
    bucket, rank = meta[0], meta[1]
    pos = rank + jnp.sum(jnp.where(bucket[None, :] == jnp.arange(N_BUCKETS, dtype=jnp.int32)[:, None],
                                   offs[:, None], 0), axis=0)
    tile_start = jnp.arange(n_tiles, dtype=jnp.int32) * TMO
    tile_bucket = jnp.minimum(jnp.sum((tile_start[:, None] >= ends[None, :]).astype(jnp.int32), axis=1), N_BUCKETS - 1)
    pair_lo = np.array([0, 0, 0, 1, 1, 2], np.int32)
    pair_hi = np.array([1, 2, 3, 2, 3, 3], np.int32)
    b_lo = jnp.asarray(np.repeat(np.arange(N_GROUPS), N_PAIRS) * EXP_PER_GROUP + np.tile(pair_lo, N_GROUPS), jnp.int32)
    b_hi = jnp.asarray(np.repeat(np.arange(N_GROUPS), N_PAIRS) * EXP_PER_GROUP + np.tile(pair_hi, N_GROUPS), jnp.int32)
    onehot_tb = (tile_bucket[:, None] == jnp.arange(N_BUCKETS, dtype=jnp.int32)[None, :]).astype(jnp.int32)
    elo = jnp.sum(onehot_tb * b_lo[None, :], axis=1)
    ehi = jnp.sum(onehot_tb * b_hi[None, :], axis=1)
    bucket_end = jnp.sum(onehot_tb * (offs + cnt)[None, :], axis=1)
    nvalid = jnp.where(tile_start < ends[-1], jnp.clip(bucket_end - tile_start, 0, TMO), 0)
    dump = sort_rows + jnp.arange(n_pad - n, dtype=jnp.int32)
    pos_sc = jnp.concatenate([pos, dump])
    total = sort_rows + n_pad - n
    sc_idx = (pos_sc[None, :] + (jnp.arange(DISP_SLABS, dtype=jnp.int32) * total)[:, None]).reshape(-1)
    xs = _sc_scatter_rows(disp.reshape(DISP_SLABS * n_pad, 128), sc_idx, DISP_SLABS * total)
    ys, tok = _moe_call(xs.reshape(DISP_SLABS, total, 128), elo, ehi, nvalid, wgu, wdn, n_tiles,
                        n, (n_pad - n) // TMO)
    back_idx = (tok.reshape(1, sort_rows) + (jnp.arange(Y_SLABS, dtype=jnp.int32) * n_pad)[:, None]).reshape(-1)
    z = _sc_scatter_rows(ys.reshape(Y_SLABS * sort_rows, 128), back_idx, Y_SLABS * n_pad)
    return z.reshape(Y_SLABS, n_pad, 128), tok


def _final_kernel(xn_ref, z_ref, gate_ref, g_ref, yp_ref, ys_ref, *, n_prompt_tiles):
    x = _add_moe(xn_ref, z_ref, gate_ref)
    ms = jnp.mean(x * x, axis=-1, keepdims=True)
    y = x * lax.rsqrt(ms + EPS) * g_ref[...]
    i = pl.program_id(0)

    @pl.when(i < n_prompt_tiles)
    def _():
        yp_ref[...] = y

    @pl.when(i >= n_prompt_tiles)
    def _():
        ys_ref[...] = y


def _final_call(xn, z, mods, mod_rows, g, n_prompt):
    n, d = xn.shape
    npt = n_prompt // TM
    assert n - n_prompt == TM
    return pl.pallas_call(
        functools.partial(_final_kernel, n_prompt_tiles=npt),
        grid=(n // TM,),
        in_specs=[pl.BlockSpec((TM, d), lambda i: (i, 0)), pl.BlockSpec((z.shape[0], TM, 128), lambda i: (0, i, 0)),
                  _mod_spec(GATE_FFN, mod_rows), pl.BlockSpec((1, d), lambda i: (0, 0))],
        out_specs=_token_specs(npt, d),
        out_shape=[jax.ShapeDtypeStruct((n_prompt, d), F32), jax.ShapeDtypeStruct((TM, d), F32)],
        compiler_params=_cparams(("arbitrary",)),
        name="final_norm",
    )(xn, z, mods, g)


def kernel(x_prompt, x_sample, c_prompt, c_sample, state_gla, cache_band_k, cache_band_v, cache_swa_k, cache_swa_v,
           w_ada, b_ada, norm_mix, norm_ffn, norm_final, w_in_even, w_gate_a, b_gate_a, gla_norm, rel_bias_b,
           w_out_even, w_in_odd, sinks_c, w_out_odd, w_router, b_router, w_gate_up, w_down):
    bp, lp, d = x_prompt.shape
    bs, ls_, _ = x_sample.shape
    n_p, n_s = bp * lp, bs * ls_
    n = n_p + n_s
    assert ls_ == CHUNK and n_s == TM and lp % TM == 0 and PAST_LEN % CHUNK == 0

    xp2, xs2 = x_prompt.reshape(n_p, d), x_sample.reshape(n_s, d)

    c16 = jnp.zeros((SEQ_ROWS, d), F32).at[:bp].set(c_prompt).at[bp:bp + bs].set(c_sample)
    mods = _ada_call(c16, w_ada, b_ada)
    grp = TM // CHUNK
    assert bs == grp
    kind_rows = np.concatenate([np.repeat(np.arange(bp), grp), bp + np.arange(bs)])
    mods_g = [mods[l][kind_rows] for l in range(DEPTH)]
    mod_rows = lambda i: jnp.minimum(i // (lp // TM), bp)

    perm = np.array([4 * (c % 4) + c // 4 for c in range(N_EXPERTS)])
    wr = jnp.zeros((d, 128), F32).at[:, :N_EXPERTS].set(w_router[:, perm])
    br = jnp.zeros((1, 128), F32).at[0, :N_EXPERTS].set(b_router[perm])

    sc_unit = SC_WINDOW * SC_WORKERS * SC_GROUP
    n_pad = n + TMO
    while (DISP_SLABS * n_pad) % sc_unit or (Y_SLABS * n_pad) % TMO or (n_pad - n) % TMO:
        n_pad += TMO
    sort_rows = n + N_BUCKETS * TMO
    while (Y_SLABS * sort_rows) % sc_unit or sort_rows % (MOE_TILES * TMO):
        sort_rows += TMO

    gla_p = gla_s = bk_p = bv_p = bk_s = bv_s = sk_p = sv_p = sk_s = sv_s = None
    xn = z = tok = None
    for l in range(DEPTH):
        i = l // 2
        if l % 2 == 0:
            w = w_in_even[i]
            w_main = jnp.concatenate([w[:, :1536], w[:, 1552:]], axis=1).astype(BF16)
            w_la = jnp.zeros((d, 128), F32).at[:, :GATE_RANK].set(w[:, 1536:1552]).astype(BF16)
            w_gate = jnp.zeros((128, HA * DKA), F32).at[:GATE_RANK].set(w_gate_a[i])
            qa, ka, va, ra, qb, kb, vb, ga = _inproj_even_call(
                xp2, xs2, mods_g[l], mod_rows, norm_mix[l][None], w_main, w_la, w_gate, b_gate_a[i][None])
            xres = [xp2, xs2]
            gn = gla_norm[i][None]
            oa, s_p = _gla_call(qa, ka, va, ga, ra, jnp.zeros((bp, 256, 128), F32), gn, None,
                                n_seq=bp, seq_rows=lp, row0=0, nb=8)
            oa, s_s = _gla_call(qa, ka, va, ga, ra, state_gla[i].reshape(bs, 256, 128), gn, oa,
                                n_seq=bs, seq_rows=ls_, row0=n_p, nb=1)
            gla_p, gla_s = s_p.reshape(1, bp, HA, DKA, DVA), s_s.reshape(1, bs, HA, DKA, DVA)
            pb = N_PREV_B * CHUNK
            tq, g = 512, 2
            ck = cache_band_k[i].reshape(bs * pb, HB * DHB).astype(BF16)
            cv = cache_band_v[i].reshape(bs * pb, HB * DHB).astype(BF16)
            biases = (_band_bias(rel_bias_b[i], g, pb, _band_valid(g, pb)),
                      _band_bias(rel_bias_b[i], g, pb, _band_valid(g, pb, tq // (CHUNK * g))),
                      _band_bias(rel_bias_b[i], 1, pb, _band_valid(1, pb)))
            ob = _attention(_band_kernel, qb, kb, vb, ck, cv, biases, [], [], width=512, kv_width=512, pb=pb,
                            tq=tq, g=g, bp=bp, lp=lp, bs=bs, name="band")
            tail = lambda a: jnp.stack([a[(b + 1) * lp - pb:(b + 1) * lp] for b in range(bp)]).astype(F32).reshape(1, bp, pb, HB, DHB)
            new = lambda a: a[n_p:].astype(F32).reshape(bs, ls_, HB, DHB)
            bk_p, bv_p = tail(kb), tail(vb)
            bk_s = jnp.concatenate([cache_band_k[i][:, ls_:], new(kb)], axis=1)[None]
            bv_s = jnp.concatenate([cache_band_v[i][:, ls_:], new(vb)], axis=1)[None]
            wo = w_out_even[i].astype(BF16)
            os_, ws = [oa, ob], [wo[:HA * DVA], wo[HA * DVA:]]
        else:
            w = _after(w_in_odd[i], tok)
            w_out_l = _after(w_out_odd[i], tok)
            cache_k_l, cache_v_l = _after(cache_swa_k[i], tok), _after(cache_swa_v[i], tok)
            wk, wv = w[:, 1024:1152], w[:, 1152:1280]
            dup = lambda a: jnp.concatenate([a[:, :64], a[:, :64], a[:, 64:], a[:, 64:]], axis=1)
            w_all = jnp.concatenate([w[:, :1024], dup(wk), dup(wv)], axis=1).astype(BF16)
            cos, sin, rope_map = _rope_tables(lp, ls_, bp, bs)
            x, q, k, v = _inproj_odd_call(xn, z, mods_g[l - 1], mods_g[l], mod_rows, norm_mix[l][None], cos, sin,
                                          rope_map, w_all)
            xres = [x]
            pb = WINDOW
            tq, g = 512, 2
            sink = sinks_c[i][None] * LOG2E
            sink_spec = [pl.BlockSpec(memory_space=pltpu.SMEM)]
            dupc = lambda c: jnp.concatenate([c[:, :, 0], c[:, :, 0], c[:, :, 1], c[:, :, 1]], axis=-1).reshape(bs * pb, 256).astype(BF16)
            ck, cv = dupc(cache_k_l), dupc(cache_v_l)
            additive = lambda valid: jnp.asarray(np.where(valid, 0.0, -np.inf), F32)
            masks = (additive(_band_valid(g, pb)), additive(_band_valid(g, pb, tq // (CHUNK * g))),
                     additive(_band_valid(1, pb)))
            o = _attention(_swa_kernel, q, k, v, ck, cv, masks, [sink], sink_spec, width=1024, kv_width=256, pb=pb,
                           tq=tq, g=g, bp=bp, lp=lp, bs=bs, name="swa")
            undup = lambda a: jnp.concatenate([a[:, 0:64], a[:, 128:192]], axis=1).astype(F32)
            tail = lambda a: jnp.stack([undup(a[(b + 1) * lp - pb:(b + 1) * lp]) for b in range(bp)]).reshape(1, bp, pb, KVC, DHC)
            new = lambda a: undup(a[n_p:]).reshape(bs, ls_, KVC, DHC)
            sk_p, sv_p = tail(k), tail(v)
            sk_s = jnp.concatenate([cache_swa_k[i][:, ls_:], new(k)], axis=1)[None]
            sv_s = jnp.concatenate([cache_swa_v[i][:, ls_:], new(v)], axis=1)[None]
            os_, ws = [o], [w_out_l.astype(BF16)]
        xn, disp, meta, counts = _outproj_call(xres, os_, ws, mods_g[l], mod_rows, norm_ffn[l][None], wr, br, n_pad)
        z, tok = _moe_layer(disp, meta, counts, w_gate_up, w_down, l, n, n_pad, sort_rows)

    y_prompt, y_sample = _final_call(xn, z, mods_g[DEPTH - 1], mod_rows, norm_final[None], n_p)
    return (y_prompt.reshape(bp, lp, d), y_sample.reshape(bs, ls_, d),
            gla_p, gla_s, bk_p, bv_p, bk_s, bv_s, sk_p, sv_p, sk_s, sv_s)
```

```python
import functools

import numpy as np
import jax
import jax.numpy as jnp
from jax import lax
from jax.experimental import pallas as pl
from jax.experimental.pallas import tpu as pltpu
from jax.experimental.pallas import tpu_sc as plsc

F32 = jnp.float32
BF16 = jnp.bfloat16

D_MODEL = 1024
DEPTH = 2
CHUNK = 64
PAST_LEN = 4096
HA, DKA, DVA = 4, 64, 128
GATE_RANK = 16
GATE_TAU = 16.0
HB, DHB = 8, 64
N_PREV_B = 8
MAX_REL = 128
HC, KVC, DHC = 16, 2, 64
WINDOW = 128
ROPE_THETA = 10000.0
N_EXPERTS = 16
N_GROUPS = 4
EXP_PER_GROUP = 4
D_FF = 512
EPS = 1e-6

N_PAIRS = 6
N_BUCKETS = N_GROUPS * N_PAIRS
BUCKET_ROWS = 32
Y_SLABS = 4
DISP_SLABS = Y_SLABS + 1
TMO = 256
MOE_TILES = 2
SC_WINDOW = 128
SC_WORKERS = 32
SC_GROUP = 3

TM = 512
SEQ_ROWS = 16
SUB = 16
LOG2E = 1.4426950408889634
VMEM_LIMIT_MB = 48
VMEM_LIMIT_MOE_MB = 56


def _cparams(sem, vmem_mb=VMEM_LIMIT_MB):
    return pltpu.CompilerParams(dimension_semantics=sem, vmem_limit_bytes=vmem_mb * 1024 * 1024)


def _dot(a, b):
    return jnp.dot(a, b, preferred_element_type=F32)


def _dot_nt(a, b):
    return lax.dot_general(a, b, (((1,), (1,)), ((), ())), preferred_element_type=F32)


def _split(a):
    hi = a.astype(BF16)
    lo = (a - hi.astype(F32)).astype(BF16)
    return hi, lo


def _dot3(a, b):
    ah, al = _split(a)
    bh, bl = _split(b)
    return _dot(ah, bh) + _dot(ah, bl) + _dot(al, bh)


def _dot3_narrow(a, b):
    ah, al = _split(a)
    bh, bl = _split(b)
    n = b.shape[1]
    p = _dot(ah, jnp.concatenate([bh, bl], axis=1))
    return p[:, :n] + p[:, n:] + _dot(al, bh)


def _sigmoid(x):
    return 1.0 / (1.0 + jnp.exp(-x))


def _group_affine(y, mul, add):
    parts = []
    for gi in range(y.shape[0] // CHUNK):
        p = y[gi * CHUNK:(gi + 1) * CHUNK]
        if mul is not None:
            p = p * mul[gi:gi + 1]
        if add is not None:
            p = p + add[gi:gi + 1]
        parts.append(p)
    return jnp.concatenate(parts, axis=0)


def _norm_mod(x, g, shift, scale):
    ms = jnp.mean(x * x, axis=-1, keepdims=True)
    return _group_affine(x * lax.rsqrt(ms + EPS) * g, 1.0 + scale, shift)


def _mod_spec(part):
    return pl.BlockSpec((TM // CHUNK, D_MODEL), lambda i: (i, part))


SHIFT_MIX, SCALE_MIX, GATE_MIX, SHIFT_FFN, SCALE_FFN, GATE_FFN = range(6)


def _on_token_tile(xp_ref, xs_ref, n_prompt_tiles, body):
    @pl.when(pl.program_id(0) < n_prompt_tiles)
    def _():
        body(xp_ref)

    @pl.when(pl.program_id(0) >= n_prompt_tiles)
    def _():
        body(xs_ref)


RING = 3


def _ring_block(srcs, buf, sem, n_prompt_tiles=None):
    s = pl.program_id(0)
    n_steps = pl.num_programs(0)
    t = buf.shape[1]

    def copy(src, blk, slot):
        return pltpu.make_async_copy(src.at[pl.ds(pl.multiple_of(blk * t, t), t)], buf.at[slot], sem.at[slot])

    def start(step, slot):
        if len(srcs) == 1:
            copy(srcs[0], step, slot).start()
        elif isinstance(step, int):
            assert step < n_prompt_tiles
            copy(srcs[0], step, slot).start()
        else:
            @pl.when(step < n_prompt_tiles)
            def _():
                copy(srcs[0], step, slot).start()

            @pl.when(step >= n_prompt_tiles)
            def _():
                copy(srcs[1], step - n_prompt_tiles, slot).start()

    @pl.when(s == 0)
    def _():
        for k in range(RING - 1):
            start(k, k)

    ahead = s + (RING - 1)

    @pl.when(ahead < n_steps)
    def _():
        start(ahead, ahead % RING)

    slot = s % RING
    copy(srcs[0], 0, slot).wait()
    return buf.at[slot]


def _ring_scratch(d):
    return [pltpu.VMEM((RING, TM, d), F32), pltpu.SemaphoreType.DMA((RING,))]


def _token_specs(n_prompt_tiles, d):
    return [pl.BlockSpec((TM, d), lambda i: (jnp.minimum(i, n_prompt_tiles - 1), 0)),
            pl.BlockSpec((TM, d), lambda i: (0, 0))]


def _ada_kernel(c_ref, w_ref, b_ref, o_ref):
    c = c_ref[...]
    o_ref[0] = _dot3(c * _sigmoid(c), w_ref[0]) + b_ref[0]


def _ada_call(c16, w_ada, b_ada):
    d = D_MODEL
    tn = 1024
    return pl.pallas_call(
        _ada_kernel,
        grid=(DEPTH, 6 * d // tn),
        in_specs=[pl.BlockSpec((SEQ_ROWS, d), lambda l, j: (0, 0)),
                  pl.BlockSpec((1, d, tn), lambda l, j: (l, 0, j)),
                  pl.BlockSpec((1, 1, tn), lambda l, j: (l, 0, j))],
        out_specs=pl.BlockSpec((1, SEQ_ROWS, tn), lambda l, j: (l, 0, j)),
        out_shape=jax.ShapeDtypeStruct((DEPTH, SEQ_ROWS, 6 * d), F32),
        compiler_params=_cparams(("arbitrary", "arbitrary")),
        name="ada",
    )(c16, w_ada, b_ada.reshape(DEPTH, 1, 6 * d))


def _inproj_even_kernel(xp_ref, xs_ref, sh_ref, sc_ref, g_ref, w_ref, wla_ref, wg_ref, bg_ref,
                        qa_ref, ka_ref, va_ref, ra_ref, qb_ref, kb_ref, vb_ref, ga_ref, *, n_prompt_tiles):
    def body(x_ref):
        t = x_ref.shape[0]
        outs = ((qa_ref, 0, 256, DKA ** -0.5), (ka_ref, 256, 512, None), (va_ref, 512, 1024, None),
                (ra_ref, 1024, 1536, None), (qb_ref, 1536, 2048, DHB ** -0.5 * LOG2E), (kb_ref, 2048, 2560, None),
                (vb_ref, 2560, 3072, None))
        shift, scale_ = sh_ref[...], sc_ref[...]
        halves = [slice(0, t // 2), slice(t // 2, t)]
        grp = [slice(0, t // (2 * CHUNK)), slice(t // (2 * CHUNK), t // CHUNK)]
        hbs = [_norm_mod(x_ref[rs, :], g_ref[...], shift[gs], scale_[gs]).astype(BF16) for rs, gs in zip(halves, grp)]
        for rs, hb in zip(halves, hbs):
            zs = [_dot(hb, w_ref[:, lo:hi]) for _, lo, hi, _ in outs]
            la = _dot(hb, wla_ref[...])
            for z, (o_ref, _, _, scale) in zip(zs, outs):
                o_ref[rs, :] = (z if scale is None else z * scale).astype(BF16)
            gl = _dot3(la, wg_ref[...]) + bg_ref[...]
            ga_ref[rs, :] = -(jnp.maximum(-gl, 0.0) + jnp.log(1.0 + jnp.exp(-jnp.abs(gl)))) * (1.0 / GATE_TAU)

    _on_token_tile(xp_ref, xs_ref, n_prompt_tiles, body)


def _inproj_even_call(xp, xs, mods, g, w_main, w_la, w_gate, b_gate):
    d = xp.shape[1]
    npt = xp.shape[0] // TM
    n = xp.shape[0] + xs.shape[0]
    row = lambda i: (i, 0)
    const = lambda i: (0, 0)
    widths = (256, 256, 512, 512, 512, 512, 512)
    out_shape = [jax.ShapeDtypeStruct((n, w), BF16) for w in widths] + [jax.ShapeDtypeStruct((n, 256), F32)]
    out_specs = [pl.BlockSpec((TM, w), row) for w in widths] + [pl.BlockSpec((TM, 256), row)]
    return pl.pallas_call(
        functools.partial(_inproj_even_kernel, n_prompt_tiles=npt),
        grid=(n // TM,),
        in_specs=_token_specs(npt, d) + [
            _mod_spec(SHIFT_MIX), _mod_spec(SCALE_MIX),
            pl.BlockSpec((1, d), const),
            pl.BlockSpec(w_main.shape, const), pl.BlockSpec(w_la.shape, const),
            pl.BlockSpec(w_gate.shape, const), pl.BlockSpec(b_gate.shape, const)],
        out_specs=out_specs, out_shape=out_shape,
        compiler_params=_cparams(("parallel",)),
        name="inproj_even",
    )(xp, xs, mods, mods, g, w_main, w_la, w_gate, b_gate)


def _rope(x, cos, sin_signed):
    t, w = x.shape
    lane = lax.broadcasted_iota(jnp.int32, (1, w), 1)
    first_half = (lane & 63) < 32
    rot = jnp.where(first_half, pltpu.roll(x, w - 32, 1), pltpu.roll(x, 32, 1))
    reps = w // 128
    return x * jnp.tile(cos, (1, reps)) + rot * jnp.tile(sin_signed, (1, reps))


def _unpack_pairs(slabs, dtype):
    lo = [pltpu.bitcast(s << 16, F32) for s in slabs]
    hi = [pltpu.bitcast(s & jnp.int32(-65536), F32) for s in slabs]
    return jnp.concatenate(lo + hi, axis=1).astype(dtype)


def _pack_pairs(x):
    bits = pltpu.bitcast(x.astype(BF16).astype(F32), jnp.int32)
    half = x.shape[1] // 2
    packed = ((bits[:, :half] >> 16) & jnp.int32(0xFFFF)) | (bits[:, half:] & jnp.int32(-65536))
    return [packed[:, 128 * s:128 * (s + 1)] for s in range(half // 128)]


def _add_moe(xn_ref, z_ref, gate_ref):
    y = _unpack_pairs([z_ref[s] for s in range(z_ref.shape[0])], F32)
    return xn_ref[...] + _group_affine(y, gate_ref[...], None)


def _rope_tables(lp, ls_, bp, bs):
    assert PAST_LEN + ls_ <= lp and lp % 128 == 0 and bs * ls_ == TM
    half = DHC // 2
    inv = ROPE_THETA ** (-jnp.arange(half, dtype=F32) / half)
    inv = jnp.tile(inv, 128 // half)
    sign = jnp.asarray(np.tile(np.repeat([-1.0, 1.0], half), 128 // DHC), F32)
    a = jnp.asarray(np.arange(lp // 128) * 128, F32)[:, None] * inv[None, :]
    b = jnp.asarray(np.arange(128), F32)[:, None] * inv[None, :]
    ca, sa, cb, sb = jnp.cos(a)[:, None], jnp.sin(a)[:, None], jnp.cos(b)[None], jnp.sin(b)[None]
    cos = (ca * cb - sa * sb).reshape(lp, 128)
    sin = ((sa * cb + ca * sb) * sign).reshape(lp, 128)
    with_sample = lambda t: jnp.concatenate([t, jnp.tile(t[PAST_LEN:PAST_LEN + ls_], (bs, 1))], axis=0)
    tiles = lp // TM
    return with_sample(cos), with_sample(sin), lambda i: (jnp.where(i < bp * tiles, i % tiles, tiles), 0)


def _inproj_odd_kernel(xn_hbm, z_ref, gate_ref, sh_ref, sc_ref, g_ref, cos_ref, sin_ref, w_ref,
                       x_ref, q_ref, k_ref, v_ref, xbuf, xsem):
    xn_ref = _ring_block([xn_hbm], xbuf, xsem)
    t = xn_ref.shape[0]
    halves = [slice(0, t // 2), slice(t // 2, t)]
    grp = [slice(0, t // (2 * CHUNK)), slice(t // (2 * CHUNK), t // CHUNK)]
    gate, shift, scale = gate_ref[...], sh_ref[...], sc_ref[...]
    xs = []
    for rs, gs in zip(halves, grp):
        y = _unpack_pairs([z_ref[s, rs, :] for s in range(z_ref.shape[0])], F32)
        xs.append(xn_ref[rs, :] + _group_affine(y, gate[gs], None))
    for rs, x in zip(halves, xs):
        x_ref[rs, :] = x
    hbs = [_norm_mod(x, g_ref[...], shift[gs], scale[gs]).astype(BF16) for x, gs in zip(xs, grp)]
    qs = [_dot(hb, w_ref[:, 0:1024]) for hb in hbs]
    ks = [_dot(hb, w_ref[:, 1024:1280]) for hb in hbs]
    vs = [_dot(hb, w_ref[:, 1280:1536]) for hb in hbs]
    for rs, q, k, v in zip(halves, qs, ks, vs):
        cos, sin = cos_ref[rs, :], sin_ref[rs, :]
        q_ref[rs, :] = (_rope(q, cos, sin) * (DHC ** -0.5 * LOG2E)).astype(BF16)
        k_ref[rs, :] = _rope(k, cos, sin).astype(BF16)
        v_ref[rs, :] = v.astype(BF16)


def _inproj_odd_call(xn, z, mods_prev, mods, g, cos, sin, rope_map, w):
    n, d = xn.shape
    row = lambda i: (i, 0)
    const = lambda i: (0, 0)
    widths = (1024, 256, 256)
    return pl.pallas_call(
        _inproj_odd_kernel,
        grid=(n // TM,),
        in_specs=[pl.BlockSpec(memory_space=pl.ANY), pl.BlockSpec((z.shape[0], TM, 128), lambda i: (0, i, 0)),
                  _mod_spec(GATE_FFN), _mod_spec(SHIFT_MIX), _mod_spec(SCALE_MIX),
                  pl.BlockSpec((1, d), const),
                  pl.BlockSpec((TM, 128), rope_map), pl.BlockSpec((TM, 128), rope_map),
                  pl.BlockSpec(w.shape, const)],
        out_specs=[pl.BlockSpec((TM, d), row)] + [pl.BlockSpec((TM, wd), row) for wd in widths],
        out_shape=[jax.ShapeDtypeStruct((n, d), F32)] + [jax.ShapeDtypeStruct((n, wd), BF16) for wd in widths],
        scratch_shapes=_ring_scratch(d),
        compiler_params=_cparams(("arbitrary",)),
        name="inproj_odd",
    )(xn, z, mods_prev, mods, mods, g, cos, sin, w)


def _gla_tri():
    t = np.arange(CHUNK)[:, None]
    s = np.arange(CHUNK)[None, :]
    cum = s <= t
    start = s < (t // SUB) * SUB
    end = s < (t // SUB + 1) * SUB
    return jnp.asarray(np.concatenate([cum, start, end], axis=0).astype(np.float32), dtype=BF16)


def _gla_kernel(q_ref, k_ref, v_ref, g_ref, r_ref, s0_ref, gn_ref, tri_ref, o_ref, sout_ref, s_ref, *, nb):
    c_ = CHUNK
    nsub = c_ // SUB

    @pl.when(pl.program_id(1) == 0)
    def _():
        s_ref[...] = s0_ref[0]

    tri = tri_ref[...]
    lane = lax.broadcasted_iota(jnp.int32, (1, 128), 1)
    hmask = [jnp.where(lane < DKA, 1.0, 0.0), jnp.where(lane >= DKA, 1.0, 0.0)]
    ti = lax.broadcasted_iota(jnp.int32, (c_, c_), 0)
    si = lax.broadcasted_iota(jnp.int32, (c_, c_), 1)
    rb, cb = ti >> 4, si >> 4
    m_diag = (rb == cb) & (si <= ti)
    m_off = [(cb == j) & (rb > j) for j in range(nsub - 1)]
    hk = HA * DKA
    gn = gn_ref[...]

    chunks = range(nb)
    heads = [(p, hh) for p in range(HA // 2) for hh in range(2)]
    rows = [slice(c * c_, (c + 1) * c_) for c in chunks]
    pair = [slice(128 * p, 128 * (p + 1)) for p in range(HA // 2)]
    css = []
    for c in chunks:
        g_hi, g_lo = _split(g_ref[rows[c], :])
        css.append(_dot(tri, g_hi) + _dot(tri, g_lo))
    lhs1, lhs2, kds, kes, q_inter, klts, dcols = [], [], [], [], [], [], []
    for c in chunks:
        b, rs, re = css[c][0:c_], css[c][c_:2 * c_], css[c][2 * c_:3 * c_]
        q = q_ref[rows[c], :].astype(F32)
        k = k_ref[rows[c], :].astype(F32)
        bl = b[c_ - 1:c_, :]
        qd = q * jnp.exp(b - rs)
        kd = k * jnp.exp(rs - b)
        ke = k * jnp.exp(re - b)
        qi = q * jnp.exp(b)
        kl = k * jnp.exp(bl - b)
        ql = [q * jnp.exp(jnp.minimum(b - b[SUB * (j + 1) - 1:SUB * (j + 1), :], 0.0)) for j in range(nsub - 1)]
        dcols.append(jnp.broadcast_to(jnp.exp(bl), (8, hk)).T[:, 0:1])
        kds.append([(kd[:, pair[p]] * hmask[hh]).astype(BF16) for p, hh in heads])
        kes.append([(ke[:, pair[p]] * hmask[hh]).astype(BF16) for p, hh in heads])
        klts.append([kl[:, ls].T.astype(BF16) for ls in pair])
        lhs1.append([qd[:, ls].astype(BF16) for ls in pair])
        lhs2.append([jnp.concatenate([ql[j][:, ls] for j in range(nsub - 1)], axis=0).astype(BF16) for ls in pair])
        q_inter.append([(qi[:, pair[p]] * hmask[hh]).astype(BF16) for p, hh in heads])
    a1s = [[_dot_nt(lhs1[c][p], kds[c][h]) for h, (p, hh) in enumerate(heads)] for c in chunks]
    a2s = [[_dot_nt(lhs2[c][p], kes[c][h]) for h, (p, hh) in enumerate(heads)] for c in chunks]
    atts = []
    for c in chunks:
        per_head = []
        for h in range(HA):
            att = jnp.zeros((c_, c_), F32)
            for j in reversed(range(nsub - 1)):
                att = jnp.where(m_off[j], a2s[c][h][j * c_:(j + 1) * c_], att)
            per_head.append(jnp.where(m_diag, a1s[c][h], att).astype(BF16))
        atts.append(per_head)
    vs_ = [[v_ref[rows[c], DVA * h:DVA * (h + 1)] for h in range(HA)] for c in chunks]
    o_intra = [[_dot(atts[c][h], vs_[c][h]) for h in range(HA)] for c in chunks]
    upds = [jnp.concatenate([_dot(klts[c][p][DKA * hh:DKA * (hh + 1)], vs_[c][2 * p + hh]) for p, hh in heads], axis=0)
            for c in chunks]

    s_cur = s_ref[...]
    s_in = []
    for c in chunks:
        s_in.append(s_cur.astype(BF16))
        s_cur = dcols[c] * s_cur + upds[c]
    s_ref[...] = s_cur
    sout_ref[0] = s_cur

    for c in chunks:
        for h in range(HA):
            o = o_intra[c][h] + _dot(q_inter[c][h], s_in[c][pair[h // 2], :])
            ms = jnp.mean(o * o, axis=-1, keepdims=True)
            vs = slice(DVA * h, DVA * (h + 1))
            rr = r_ref[rows[c], vs].astype(F32)
            o_ref[rows[c], vs] = (o * lax.rsqrt(ms + EPS) * gn * (rr * _sigmoid(rr))).astype(BF16)


def _gla_call(q, k, v, g, r, s0, gn, o_prev, *, n_seq, seq_rows, row0, nb):
    tq = nb * CHUNK
    steps = seq_rows // tq
    blk0 = row0 // tq
    row = lambda b, j: (blk0 + b * steps + j, 0)
    const = lambda b, j: (0, 0)
    tri = _gla_tri()
    in_specs = [pl.BlockSpec((tq, 256), row), pl.BlockSpec((tq, 256), row), pl.BlockSpec((tq, 512), row),
                pl.BlockSpec((tq, 256), row), pl.BlockSpec((tq, 512), row),
                pl.BlockSpec((1, 256, 128), lambda b, j: (b, 0, 0)),
                pl.BlockSpec((1, 128), const), pl.BlockSpec(tri.shape, const)]
    args = [q, k, v, g, r, s0, gn, tri]
    aliases = {}
    if o_prev is not None:
        in_specs.append(pl.BlockSpec(memory_space=pl.ANY))
        args.append(o_prev)
        aliases = {len(args) - 1: 0}
    kern = functools.partial(_gla_kernel, nb=nb)
    if o_prev is not None:
        kern = _drop_arg(kern, 8)
    return pl.pallas_call(
        kern,
        grid=(n_seq, steps),
        in_specs=in_specs,
        out_specs=[pl.BlockSpec((tq, 512), row), pl.BlockSpec((1, 256, 128), lambda b, j: (b, 0, 0))],
        out_shape=[jax.ShapeDtypeStruct((q.shape[0], 512), BF16), jax.ShapeDtypeStruct((n_seq, 256, 128), F32)],
        scratch_shapes=[pltpu.VMEM((256, 128), F32)],
        input_output_aliases=aliases,
        compiler_params=_cparams(("arbitrary", "arbitrary")),
        name="gla",
    )(*args)


def _drop_arg(fn, idx):
    def wrapped(*refs):
        return fn(*refs[:idx], *refs[idx + 1:])
    return wrapped


def _window(prev_ref, cur_ref, lo, hi, pb, ls):
    if lo < pb:
        return jnp.concatenate([prev_ref[lo:pb, ls], cur_ref[0:hi - pb, ls]], axis=0)
    return cur_ref[lo - pb:hi - pb, ls]


def _band_kernel(q_ref, kp_ref, kc_ref, vp_ref, vc_ref, bias_ref, o_ref, *, g, n_sub, pb):
    qs = CHUNK * g
    kw_rows = pb + qs
    lane = lax.broadcasted_iota(jnp.int32, (1, 128), 1)
    low = lane < DHB
    hmask = [jnp.where(low, 1.0, 0.0), jnp.where(low, 0.0, 1.0)]
    for s in range(n_sub):
        sb = s if bias_ref.shape[0] > 1 else 0
        rows = slice(qs * s, qs * (s + 1))
        lanes = [slice(128 * p, 128 * (p + 1)) for p in range(HB // 2)]
        heads = [(p, hh) for p in range(HB // 2) for hh in range(2)]
        qps = [q_ref[rows, ls].astype(F32) for ls in lanes]
        kws = [_window(kp_ref, kc_ref, qs * s, qs * s + kw_rows, pb, ls) for ls in lanes]
        vws = [_window(vp_ref, vc_ref, qs * s, qs * s + kw_rows, pb, ls) for ls in lanes]
        qq = [jnp.concatenate([(qps[p] * hmask[hh]).astype(BF16) for hh in range(2)], axis=0) for p in range(HB // 2)]
        sc2 = [_dot_nt(qq[p], kws[p]) for p in range(HB // 2)]
        scs = [sc2[p][qs * hh:qs * (hh + 1)] + bias_ref[sb, 2 * p + hh] for p, hh in heads]
        pes = [jnp.exp2(sc - jnp.max(sc, axis=-1, keepdims=True)) for sc in scs]
        pp = [jnp.concatenate([pes[2 * p + hh].astype(BF16) for hh in range(2)], axis=0) for p in range(HB // 2)]
        o2 = [_dot(pp[p], vws[p]) for p in range(HB // 2)]
        outs = [o2[p][qs * hh:qs * (hh + 1)] / jnp.sum(pes[2 * p + hh], axis=-1, keepdims=True) for p, hh in heads]
        for p, ls in enumerate(lanes):
            o_ref[rows, ls] = jnp.where(low, outs[2 * p], outs[2 * p + 1]).astype(BF16)


def _band_valid(g, pb, n_sub=None):
    rows, kw = CHUNK * g, pb + CHUNK * g
    r = np.arange(rows)[:, None]
    c = np.arange(kw)[None, :]
    dd = c // CHUNK - r // CHUNK
    band = (dd >= 0) & (dd <= pb // CHUNK)
    if n_sub is None:
        return band[None]
    return np.stack([band & (c >= pb - rows * s) for s in range(n_sub)])


def _band_bias(table, g, pb, valid):
    rows, kw = CHUNK * g, pb + CHUNK * g
    period = kw + rows
    m = np.arange(period)
    m = np.where(m < kw, m, m - period)
    ext = table[:, np.clip(m - pb, -MAX_REL, MAX_REL) + MAX_REL] * LOG2E
    flat = jnp.tile(ext, (1, rows))[:, :rows * (period - 1)]
    bias = flat.reshape(table.shape[0], rows, period - 1)[:, :, :kw]
    return jnp.where(valid[:, None], bias[None], -jnp.inf)


def _attn_call(kernel, q, kp, kc, vp, vc, extra, extra_specs, o_prev, *, width, kv_width, tq, pb,
               n_blocks, blk_map, prev_map, name):
    row = lambda i: (blk_map(i), 0)
    prev = lambda i: (prev_map(i), 0)
    in_specs = [pl.BlockSpec((tq, width), row),
                pl.BlockSpec((pb, kv_width), prev), pl.BlockSpec((tq, kv_width), row),
                pl.BlockSpec((pb, kv_width), prev), pl.BlockSpec((tq, kv_width), row)] + extra_specs
    args = [q, kp, kc, vp, vc] + extra
    aliases = {}
    if o_prev is not None:
        in_specs.append(pl.BlockSpec(memory_space=pl.ANY))
        args.append(o_prev)
        aliases = {len(args) - 1: 0}
        kernel = _drop_arg(kernel, len(args) - 1)
    return pl.pallas_call(
        kernel,
        grid=(n_blocks,),
        in_specs=in_specs,
        out_specs=pl.BlockSpec((tq, width), row),
        out_shape=jax.ShapeDtypeStruct((q.shape[0], width), BF16),
        input_output_aliases=aliases,
        compiler_params=_cparams(("parallel",)),
        name=name,
    )(*args)


def _attention(kernel_fn, q, k, v, cache_k, cache_v, masks, extra, extra_specs, *, width, kv_width, pb, tq, g,
               bp, lp, bs, name):
    bps = lp // tq
    n_sub = tq // (CHUNK * g)
    spec = lambda a: [pl.BlockSpec(a.shape, lambda i: (0,) * a.ndim)]
    kern = functools.partial(kernel_fn, g=g, n_sub=n_sub, pb=pb)
    common = dict(width=width, kv_width=kv_width, pb=pb)
    main = lambda i: (i // (bps - 1)) * bps + i % (bps - 1) + 1
    o = _attn_call(kern, q, k, k, v, v, [masks[0]] + extra, spec(masks[0]) + extra_specs, None, tq=tq,
                   n_blocks=bp * (bps - 1), blk_map=main, prev_map=lambda i: main(i) * (tq // pb) - 1,
                   name=name + "_main", **common)
    first = lambda i: i * bps
    o = _attn_call(kern, q, k, k, v, v, [masks[1]] + extra, spec(masks[1]) + extra_specs, o, tq=tq,
                   n_blocks=bp, blk_map=first, prev_map=lambda i: jnp.maximum(first(i) * (tq // pb) - 1, 0),
                   name=name + "_first", **common)
    samp = functools.partial(kernel_fn, g=1, n_sub=1, pb=pb)
    return _attn_call(samp, q, cache_k, k, cache_v, v, [masks[2]] + extra, spec(masks[2]) + extra_specs, o, tq=CHUNK,
                      n_blocks=bs, blk_map=lambda i: bp * lp // CHUNK + i, prev_map=lambda i: i,
                      name=name + "_sample", **common)


def _swa_kernel(q_ref, kp_ref, kc_ref, vp_ref, vc_ref, mask_ref, sink_ref, o_ref, *, g, n_sub, pb):
    qs = CHUNK * g
    kw_rows = pb + qs
    lane = lax.broadcasted_iota(jnp.int32, (1, 128), 1)
    low = lane < DHC
    hmask = [jnp.where(low, 1.0, 0.0), jnp.where(low, 0.0, 1.0)]
    pairs_per_kv = HC // KVC // 2
    for s in range(n_sub):
        msk = mask_ref[s if mask_ref.shape[0] > 1 else 0]
        rows = slice(qs * s, qs * (s + 1))
        kws = [_window(kp_ref, kc_ref, qs * s, qs * s + kw_rows, pb, slice(128 * kv, 128 * (kv + 1))) for kv in range(KVC)]
        vws = [_window(vp_ref, vc_ref, qs * s, qs * s + kw_rows, pb, slice(128 * kv, 128 * (kv + 1))) for kv in range(KVC)]
        heads = [(j, hh) for j in range(HC // 2) for hh in range(2)]
        qps = [q_ref[rows, 128 * j:128 * (j + 1)].astype(F32) for j in range(HC // 2)]
        per_kv = 2 * pairs_per_kv
        qq = [jnp.concatenate([(qps[j] * hmask[hh]).astype(BF16) for j, hh in heads[per_kv * kv:per_kv * (kv + 1)]],
                              axis=0) for kv in range(KVC)]
        sc2 = [_dot_nt(qq[kv], kws[kv]) for kv in range(KVC)]
        scs = [sc2[u // per_kv][qs * (u % per_kv):qs * (u % per_kv + 1)] + msk for u in range(len(heads))]
        sks = [sink_ref[0, 2 * j + hh] for j, hh in heads]
        ms = [jnp.maximum(jnp.max(sc, axis=-1, keepdims=True), sk) for sc, sk in zip(scs, sks)]
        pes = [jnp.exp2(sc - m) for sc, m in zip(scs, ms)]
        pp = [jnp.concatenate([pe.astype(BF16) for pe in pes[per_kv * kv:per_kv * (kv + 1)]], axis=0) for kv in range(KVC)]
        o2 = [_dot(pp[kv], vws[kv]) for kv in range(KVC)]
        outs = [o2[u // per_kv][qs * (u % per_kv):qs * (u % per_kv + 1)]
                / (jnp.sum(pes[u], axis=-1, keepdims=True) + jnp.exp2(sks[u] - ms[u])) for u in range(len(heads))]
        for j in range(HC // 2):
            o_ref[rows, 128 * j:128 * (j + 1)] = jnp.where(low, outs[2 * j], outs[2 * j + 1]).astype(BF16)


def _route(logits_t):
    a = [logits_t[4 * j:4 * j + 4] for j in range(EXP_PER_GROUP)]

    def first_argmax(vals, m):
        idx = jnp.full(m.shape, float(len(vals) - 1), F32)
        for j in reversed(range(len(vals) - 1)):
            idx = jnp.where(vals[j] == m, float(j), idx)
        return idx

    m1 = functools.reduce(jnp.maximum, a)
    i1 = first_argmax(a, m1)
    bsec = [jnp.where(i1 == float(j), -jnp.inf, a[j]) for j in range(EXP_PER_GROUP)]
    m2 = functools.reduce(jnp.maximum, bsec)
    i2 = first_argmax(bsec, m2)
    rows = lambda x: [x[gi:gi + 1] for gi in range(N_GROUPS)]
    gm = functools.reduce(jnp.maximum, rows(m1))
    gscore = jnp.exp(m1 - gm) + jnp.exp(m2 - gm)
    gs = rows(gscore)
    gsel = first_argmax(gs, functools.reduce(jnp.maximum, gs))

    def pick(x):
        xr = rows(x)
        out = xr[N_GROUPS - 1]
        for gi in reversed(range(N_GROUPS - 1)):
            out = jnp.where(gsel == float(gi), xr[gi], out)
        return out

    p1 = jnp.exp(pick(m1) - gm)
    p2 = jnp.exp(pick(m2) - gm)
    w1 = p1 / (p1 + p2)
    w2 = p2 / (p1 + p2)
    s1, s2 = pick(i1), pick(i2)
    lo, hi = jnp.minimum(s1, s2), jnp.maximum(s1, s2)
    pair = jnp.where(lo == 0.0, hi - 1.0, jnp.where(lo == 1.0, hi + 1.0, 5.0))
    bucket = gsel * float(N_PAIRS) + pair
    first_is_lo = s1 < s2
    return bucket, jnp.where(first_is_lo, w1, w2), jnp.where(first_is_lo, w2, w1)


def _outproj_kernel(*refs, n_x, n_o, n_prompt_tiles):
    x_refs = refs[:n_x]
    o_refs = refs[n_x:n_x + n_o]
    w_refs = refs[n_x + n_o:n_x + 2 * n_o]
    (gate_ref, nf_ref, sh_ref, sc_ref, wr_ref, br_ref, tri_ref,
     xn_ref, disp_ref, meta_ref, cnt_ref, run_ref, xbuf, xsem) = refs[n_x + 2 * n_o:]
    t = xn_ref.shape[0]

    @pl.when(pl.program_id(0) == 0)
    def _():
        run_ref[...] = jnp.zeros_like(run_ref)

    x_src = _ring_block(list(x_refs), xbuf, xsem, n_prompt_tiles)

    halves = [slice(0, t // 2), slice(t // 2, t)]
    grp = [slice(0, t // (2 * CHUNK)), slice(t // (2 * CHUNK), t // CHUNK)]
    ys = []
    for rs in halves:
        y = _dot(o_refs[0][rs, :], w_refs[0][...])
        for i in range(1, n_o):
            y = y + _dot(o_refs[i][rs, :], w_refs[i][...])
        ys.append(y)
    gate, shift, scale = gate_ref[...], sh_ref[...], sc_ref[...]
    gys = [_group_affine(y, gate[gs], None) for y, gs in zip(ys, grp)]

    for rs, gy in zip(halves, gys):
        xn_ref[rs, :] = x_src[rs, :] + gy
    hs = [_norm_mod(xn_ref[rs, :], nf_ref[...], shift[gs], scale[gs]) for rs, gs in zip(halves, grp)]
    for rs, h in zip(halves, hs):
        for s, slab in enumerate(_pack_pairs(h)):
            disp_ref[s, rs, :] = slab
    logits_t = [(_dot3_narrow(h, wr_ref[...]) + br_ref[...]).T[0:N_EXPERTS] for h in hs]
    bucket, w_lo, w_hi = _route(jnp.concatenate(logits_t, axis=1))
    r128 = lax.broadcasted_iota(jnp.int32, (128, t), 0)
    tok = (pl.program_id(0) * t + lax.broadcasted_iota(jnp.int32, (1, t), 1)).astype(F32)
    aux = jnp.where(r128 == 0, w_lo, jnp.where(r128 == 1, w_hi, jnp.where(r128 == 2, tok, 0.0))).T
    disp_ref[disp_ref.shape[0] - 1] = pltpu.bitcast(aux, jnp.int32)
    brow = lax.broadcasted_iota(jnp.int32, (BUCKET_ROWS, t), 0).astype(F32)
    onehot = jnp.where(brow == bucket, 1.0, 0.0)
    before = _dot(onehot.astype(BF16), tri_ref[...]) + run_ref[:, 0:1]
    rank = jnp.sum(onehot * before, axis=0, keepdims=True)
    run_ref[...] = run_ref[...] + jnp.sum(onehot, axis=1, keepdims=True)
    cnt_ref[...] = run_ref[...]
    r8 = lax.broadcasted_iota(jnp.int32, (8, t), 0)
    meta_ref[...] = jnp.where(r8 == 0, bucket, jnp.where(r8 == 1, rank, 0.0)).astype(jnp.int32)


def _outproj_call(xs_, os_, ws, mods, nf, wr, br, n_pad):
    d = xs_[0].shape[1]
    n = sum(a.shape[0] for a in xs_)
    npt = xs_[0].shape[0] // TM
    row = lambda i: (i, 0)
    const = lambda i: (0, 0)
    n_o = len(os_)
    in_specs = ([pl.BlockSpec(memory_space=pl.ANY) for _ in xs_]
                + [pl.BlockSpec((TM, o.shape[1]), row) for o in os_]
                + [pl.BlockSpec(w.shape, const) for w in ws]
                + [_mod_spec(GATE_MIX), pl.BlockSpec((1, d), const),
                   _mod_spec(SHIFT_FFN), _mod_spec(SCALE_FFN),
                   pl.BlockSpec(wr.shape, const), pl.BlockSpec(br.shape, const),
                   pl.BlockSpec((TM, TM), const)])
    tri = jnp.asarray(np.triu(np.ones((TM, TM), np.float32), k=1), dtype=BF16)
    return pl.pallas_call(
        functools.partial(_outproj_kernel, n_x=len(xs_), n_o=n_o, n_prompt_tiles=npt),
        grid=(n // TM,),
        in_specs=in_specs,
        out_specs=[pl.BlockSpec((TM, d), row), pl.BlockSpec((DISP_SLABS, TM, 128), lambda i: (0, i, 0)),
                   pl.BlockSpec((8, TM), lambda i: (0, i)), pl.BlockSpec((BUCKET_ROWS, 128), const)],
        out_shape=[jax.ShapeDtypeStruct((n, d), F32), jax.ShapeDtypeStruct((DISP_SLABS, n_pad, 128), jnp.int32),
                   jax.ShapeDtypeStruct((8, n), jnp.int32), jax.ShapeDtypeStruct((BUCKET_ROWS, 128), F32)],
        scratch_shapes=[pltpu.VMEM((BUCKET_ROWS, 128), F32)] + _ring_scratch(d),
        compiler_params=_cparams(("arbitrary",)),
        name="outproj_router",
    )(*xs_, *os_, *ws, mods, nf, mods, mods, wr, br, tri)


def _sc_mesh():
    return plsc.VectorSubcoreMesh(core_axis_name="core", subcore_axis_name="subcore")


def _sc_scatter_rows(src, idx, n_out):
    r = idx.shape[0]
    k = SC_GROUP
    w_per = r // (SC_WINDOW * SC_WORKERS)
    assert idx.shape == (src.shape[0],) and r % (SC_WINDOW * SC_WORKERS) == 0 and w_per % k == 0
    n_groups = w_per // k

    @functools.partial(
        pl.kernel, out_type=jax.ShapeDtypeStruct((n_out, 128), src.dtype), mesh=_sc_mesh(),
        scratch_types=[pltpu.VMEM((w_per, SC_WINDOW), jnp.int32),
                       pltpu.VMEM((2 * k, SC_WINDOW, 128), src.dtype),
                       pltpu.SemaphoreType.DMA((2,)), pltpu.SemaphoreType.DMA((2,))])
    def copy(x_hbm, i_hbm, o_hbm, ibuf, xbuf, in_sem, out_sem):
        wid = lax.axis_index("core") * (SC_WORKERS // 2) + lax.axis_index("subcore")
        pltpu.sync_copy(i_hbm.at[wid], ibuf)
        first = wid * w_per

        def start_in(g, slot):
            return [pltpu.async_copy(x_hbm.at[pl.ds((first + g * k + c) * SC_WINDOW, SC_WINDOW)],
                                     xbuf.at[slot * k + c], in_sem.at[slot]) for c in range(k)]

        def start_out(g, slot):
            return [pltpu.async_copy(xbuf.at[slot * k + c], o_hbm.at[ibuf.at[g * k + c]], out_sem.at[slot])
                    for c in range(k)]

        pending_in = start_in(0, 0)
        for g in range(n_groups):
            slot = g % 2
            for cp in pending_in:
                cp.wait()
            pending_out = start_out(g, slot)
            if g + 1 < n_groups:
                pending_in = start_in(g + 1, 1 - slot)
            for cp in pending_out:
                cp.wait()

    return copy(src, idx.reshape(SC_WORKERS, w_per, SC_WINDOW))


def _moe_kernel(elo_ref, ehi_ref, nvalid_ref, xs_ref, *refs, n_tok, dump_tiles):
    w_refs, (y_ref, tok_ref) = refs[:4 * MOE_TILES], refs[4 * MOE_TILES:]
    step = pl.program_id(0)
    t = TMO
    tiles = range(MOE_TILES)
    rows = [slice(t * j, t * (j + 1)) for j in tiles]
    auxs = [pltpu.bitcast(xs_ref[Y_SLABS, rows[j], :], F32) for j in tiles]
    r = lax.broadcasted_iota(jnp.int32, (1, t), 1)
    for j in tiles:
        i = step * MOE_TILES + j
        spare = n_tok + (i % dump_tiles) * t + r
        tok = jnp.where(r < nvalid_ref[i], auxs[j].T[2:3, :].astype(jnp.int32), spare)
        for c in range(t // 128):
            tok_ref[j, c:c + 1, :] = tok[:, 128 * c:128 * (c + 1)]

    any_tokens = nvalid_ref[step * MOE_TILES] > 0
    for j in range(1, MOE_TILES):
        any_tokens = jnp.logical_or(any_tokens, nvalid_ref[step * MOE_TILES + j] > 0)

    @pl.when(any_tokens)
    def _():
        units = [(j, e) for j in tiles for e in range(2)]
        hs = [_unpack_pairs([xs_ref[s, rows[j], :] for s in range(Y_SLABS)], BF16) for j in tiles]
        abs_ = [_dot(hs[j], w_refs[4 * j + e][0, 0]) for j, e in units]
        acts = [(ab[:, :D_FF] * _sigmoid(ab[:, :D_FF]) * ab[:, D_FF:]).astype(BF16) for ab in abs_]
        ys = [_dot(act, w_refs[4 * j + 2 + e][0, 0]) for act, (j, e) in zip(acts, units)]
        for j in tiles:
            acc = auxs[j][:, 0:1] * ys[2 * j] + auxs[j][:, 1:2] * ys[2 * j + 1]
            for s, slab in enumerate(_pack_pairs(acc)):
                y_ref[s, rows[j], :] = slab

    @pl.when(jnp.logical_not(any_tokens))
    def _():
        y_ref[...] = jnp.zeros_like(y_ref)


def _moe_call(xs, elo, ehi, nvalid, wgu, wdn, n_tiles, n_tok, dump_tiles):
    d = wgu.shape[2]
    m = MOE_TILES
    assert n_tiles % m == 0
    weight_specs = []
    for j in range(m):
        for shape in ((1, 1, d, 2 * D_FF), (1, 1, D_FF, d)):
            for sel in range(2):
                weight_specs.append(pl.BlockSpec(
                    shape, lambda i, lo, hi, v, j=j, sel=sel: (0, (lo, hi)[sel][m * i + j], 0, 0)))
    weights = [w for _ in range(m) for w in (wgu, wgu, wdn, wdn)]
    return pl.pallas_call(
        functools.partial(_moe_kernel, n_tok=n_tok, dump_tiles=dump_tiles),
        grid_spec=pltpu.PrefetchScalarGridSpec(
            num_scalar_prefetch=3,
            grid=(n_tiles // m,),
            in_specs=[pl.BlockSpec((DISP_SLABS, m * TMO, 128), lambda i, lo, hi, v: (0, i, 0))] + weight_specs,
            out_specs=[pl.BlockSpec((Y_SLABS, m * TMO, 128), lambda i, lo, hi, v: (0, i, 0)),
                       pl.BlockSpec((m, TMO // 128, 128), lambda i, lo, hi, v: (i, 0, 0))]),
        out_shape=[jax.ShapeDtypeStruct((Y_SLABS, n_tiles * TMO, 128), jnp.int32),
                   jax.ShapeDtypeStruct((n_tiles, TMO // 128, 128), jnp.int32)],
        compiler_params=_cparams(("arbitrary",), vmem_mb=VMEM_LIMIT_MOE_MB),
        name="moe_grouped",
    )(elo, ehi, nvalid, xs, *weights)


def _after(x, token):
    return lax.optimization_barrier((x, token))[0]


def _cast_kernel(after_ref, w_ref, o_ref):
    o_ref[...] = w_ref[...].astype(o_ref.dtype)


def _cast_call(w, layer, after):
    _, e, k, n = w.shape
    return pl.pallas_call(
        _cast_kernel,
        grid=(e,),
        in_specs=[pl.BlockSpec(memory_space=pl.ANY), pl.BlockSpec((1, 1, k, n), lambda i: (layer, i, 0, 0))],
        out_specs=pl.BlockSpec((1, 1, k, n), lambda i: (0, i, 0, 0)),
        out_shape=jax.ShapeDtypeStruct((1, e, k, n), BF16),
        compiler_params=_cparams(("parallel",)),
        name="cast_weights",
    )(after, w)


def _moe_layer(disp, meta, counts, w_gate_up, w_down, layer, n, n_pad, sort_rows):
    n_tiles = sort_rows // TMO
    wgu = _cast_call(w_gate_up, layer, counts)
    wdn = _cast_call(w_down, layer, counts)
    cnt = counts[:N_BUCKETS, 0].astype(jnp.int32)
    padded = ((cnt + TMO - 1) // TMO) * TMO
    ends = jnp.cumsum(padded)
    offs = ends - padded
    bucket, rank = meta[0], meta[1]
    pos = rank + jnp.sum(jnp.where(bucket[None, :] == jnp.arange(N_BUCKETS, dtype=jnp.int32)[:, None],
                                   offs[:, None], 0), axis=0)
    tile_start = jnp.arange(n_tiles, dtype=jnp.int32) * TMO
    tile_bucket = jnp.minimum(jnp.sum((tile_start[:, None] >= ends[None, :]).astype(jnp.int32), axis=1), N_BUCKETS - 1)
    pair_lo = np.array([0, 0, 0, 1, 1, 2], np.int32)
    pair_hi = np.array([1, 2, 3, 2, 3, 3], np.int32)
    b_lo = jnp.asarray(np.repeat(np.arange(N_GROUPS), N_PAIRS) * EXP_PER_GROUP + np.tile(pair_lo, N_GROUPS), jnp.int32)
    b_hi = jnp.asarray(np.repeat(np.arange(N_GROUPS), N_PAIRS) * EXP_PER_GROUP + np.tile(pair_hi, N_GROUPS), jnp.int32)
    onehot_tb = (tile_bucket[:, None] == jnp.arange(N_BUCKETS, dtype=jnp.int32)[None, :]).astype(jnp.int32)
    elo = jnp.sum(onehot_tb * b_lo[None, :], axis=1)
    ehi = jnp.sum(onehot_tb * b_hi[None, :], axis=1)
    bucket_end = jnp.sum(onehot_tb * (offs + cnt)[None, :], axis=1)
    nvalid = jnp.where(tile_start < ends[-1], jnp.clip(bucket_end - tile_start, 0, TMO), 0)
    dump = sort_rows + jnp.arange(n_pad - n, dtype=jnp.int32)
    pos_sc = jnp.concatenate([pos, dump])
    total = sort_rows + n_pad - n
    sc_idx = (pos_sc[None, :] + (jnp.arange(DISP_SLABS, dtype=jnp.int32) * total)[:, None]).reshape(-1)
    xs = _sc_scatter_rows(disp.reshape(DISP_SLABS * n_pad, 128), sc_idx, DISP_SLABS * total)
    ys, tok = _moe_call(xs.reshape(DISP_SLABS, total, 128), elo, ehi, nvalid, wgu, wdn, n_tiles,
                        n, (n_pad - n) // TMO)
    back_idx = (tok.reshape(1, sort_rows) + (jnp.arange(Y_SLABS, dtype=jnp.int32) * n_pad)[:, None]).reshape(-1)
    z = _sc_scatter_rows(ys.reshape(Y_SLABS * sort_rows, 128), back_idx, Y_SLABS * n_pad)
    return z.reshape(Y_SLABS, n_pad, 128), tok


def _final_kernel(xn_hbm, z_ref, gate_ref, g_ref, yp_ref, ys_ref, xbuf, xsem, *, n_prompt_tiles):
    x = _add_moe(_ring_block([xn_hbm], xbuf, xsem), z_ref, gate_ref)
    ms = jnp.mean(x * x, axis=-1, keepdims=True)
    y = x * lax.rsqrt(ms + EPS) * g_ref[...]
    i = pl.program_id(0)

    @pl.when(i < n_prompt_tiles)
    def _():
        yp_ref[...] = y

    @pl.when(i >= n_prompt_tiles)
    def _():
        ys_ref[...] = y


def _final_call(xn, z, mods, g, n_prompt):
    n, d = xn.shape
    npt = n_prompt // TM
    assert n - n_prompt == TM
    return pl.pallas_call(
        functools.partial(_final_kernel, n_prompt_tiles=npt),
        grid=(n // TM,),
        in_specs=[pl.BlockSpec(memory_space=pl.ANY), pl.BlockSpec((z.shape[0], TM, 128), lambda i: (0, i, 0)),
                  _mod_spec(GATE_FFN), pl.BlockSpec((1, d), lambda i: (0, 0))],
        out_specs=_token_specs(npt, d),
        out_shape=[jax.ShapeDtypeStruct((n_prompt, d), F32), jax.ShapeDtypeStruct((TM, d), F32)],
        scratch_shapes=_ring_scratch(d),
        compiler_params=_cparams(("arbitrary",)),
        name="final_norm",
    )(xn, z, mods, g)


def kernel(x_prompt, x_sample, c_prompt, c_sample, state_gla, cache_band_k, cache_band_v, cache_swa_k, cache_swa_v,
           w_ada, b_ada, norm_mix, norm_ffn, norm_final, w_in_even, w_gate_a, b_gate_a, gla_norm, rel_bias_b,
           w_out_even, w_in_odd, sinks_c, w_out_odd, w_router, b_router, w_gate_up, w_down):
    bp, lp, d = x_prompt.shape
    bs, ls_, _ = x_sample.shape
    n_p, n_s = bp * lp, bs * ls_
    n = n_p + n_s
    assert ls_ == CHUNK and n_s == TM and lp % TM == 0 and PAST_LEN % CHUNK == 0

    xp2, xs2 = x_prompt.reshape(n_p, d), x_sample.reshape(n_s, d)

    c16 = jnp.zeros((SEQ_ROWS, d), F32).at[:bp].set(c_prompt).at[bp:bp + bs].set(c_sample)
    mods = _ada_call(c16, w_ada, b_ada)
    seq_of_group = np.concatenate([np.repeat(np.arange(bp), lp // CHUNK), bp + np.arange(bs)])
    mods_g = [mods[l][seq_of_group] for l in range(DEPTH)]

    perm = np.array([4 * (c % 4) + c // 4 for c in range(N_EXPERTS)])
    wr = jnp.zeros((d, 128), F32).at[:, :N_EXPERTS].set(w_router[:, perm])
    br = jnp.zeros((1, 128), F32).at[0, :N_EXPERTS].set(b_router[perm])

    sc_unit = SC_WINDOW * SC_WORKERS * SC_GROUP
    n_pad = n + TMO
    while (DISP_SLABS * n_pad) % sc_unit or (Y_SLABS * n_pad) % TMO or (n_pad - n) % TMO:
        n_pad += TMO
    sort_rows = n + N_BUCKETS * TMO
    while (Y_SLABS * sort_rows) % sc_unit or sort_rows % (MOE_TILES * TMO):
        sort_rows += TMO

    gla_p = gla_s = bk_p = bv_p = bk_s = bv_s = sk_p = sv_p = sk_s = sv_s = None
    xn = z = tok = None
    for l in range(DEPTH):
        i = l // 2
        if l % 2 == 0:
            w = w_in_even[i]
            w_main = jnp.concatenate([w[:, :1536], w[:, 1552:]], axis=1).astype(BF16)
            w_la = jnp.zeros((d, 128), F32).at[:, :GATE_RANK].set(w[:, 1536:1552]).astype(BF16)
            w_gate = jnp.zeros((128, HA * DKA), F32).at[:GATE_RANK].set(w_gate_a[i])
            qa, ka, va, ra, qb, kb, vb, ga = _inproj_even_call(
                xp2, xs2, mods_g[l], norm_mix[l][None], w_main, w_la, w_gate, b_gate_a[i][None])
            xres = [xp2, xs2]
            gn = gla_norm[i][None]
            oa, s_p = _gla_call(qa, ka, va, ga, ra, jnp.zeros((bp, 256, 128), F32), gn, None,
                                n_seq=bp, seq_rows=lp, row0=0, nb=8)
            oa, s_s = _gla_call(qa, ka, va, ga, ra, state_gla[i].reshape(bs, 256, 128), gn, oa,
                                n_seq=bs, seq_rows=ls_, row0=n_p, nb=1)
            gla_p, gla_s = s_p.reshape(1, bp, HA, DKA, DVA), s_s.reshape(1, bs, HA, DKA, DVA)
            pb = N_PREV_B * CHUNK
            tq, g = 512, 2
            ck = cache_band_k[i].reshape(bs * pb, HB * DHB).astype(BF16)
            cv = cache_band_v[i].reshape(bs * pb, HB * DHB).astype(BF16)
            biases = (_band_bias(rel_bias_b[i], g, pb, _band_valid(g, pb)),
                      _band_bias(rel_bias_b[i], g, pb, _band_valid(g, pb, tq // (CHUNK * g))),
                      _band_bias(rel_bias_b[i], 1, pb, _band_valid(1, pb)))
            ob = _attention(_band_kernel, qb, kb, vb, ck, cv, biases, [], [], width=512, kv_width=512, pb=pb,
                            tq=tq, g=g, bp=bp, lp=lp, bs=bs, name="band")
            tail = lambda a: jnp.stack([a[(b + 1) * lp - pb:(b + 1) * lp] for b in range(bp)]).astype(F32).reshape(1, bp, pb, HB, DHB)
            new = lambda a: a[n_p:].astype(F32).reshape(bs, ls_, HB, DHB)
            bk_p, bv_p = tail(kb), tail(vb)
            bk_s = jnp.concatenate([cache_band_k[i][:, ls_:], new(kb)], axis=1)[None]
            bv_s = jnp.concatenate([cache_band_v[i][:, ls_:], new(vb)], axis=1)[None]
            wo = w_out_even[i].astype(BF16)
            os_, ws = [oa, ob], [wo[:HA * DVA], wo[HA * DVA:]]
        else:
            w = _after(w_in_odd[i], tok)
            w_out_l = _after(w_out_odd[i], tok)
            cache_k_l, cache_v_l = _after(cache_swa_k[i], tok), _after(cache_swa_v[i], tok)
            wk, wv = w[:, 1024:1152], w[:, 1152:1280]
            dup = lambda a: jnp.concatenate([a[:, :64], a[:, :64], a[:, 64:], a[:, 64:]], axis=1)
            w_all = jnp.concatenate([w[:, :1024], dup(wk), dup(wv)], axis=1).astype(BF16)
            cos, sin, rope_map = _rope_tables(lp, ls_, bp, bs)
            x, q, k, v = _inproj_odd_call(xn, z, mods_g[l - 1], mods_g[l], norm_mix[l][None], cos, sin, rope_map, w_all)
            xres = [x]
            pb = WINDOW
            tq, g = 512, 2
            sink = sinks_c[i][None] * LOG2E
            sink_spec = [pl.BlockSpec(memory_space=pltpu.SMEM)]
            dupc = lambda c: jnp.concatenate([c[:, :, 0], c[:, :, 0], c[:, :, 1], c[:, :, 1]], axis=-1).reshape(bs * pb, 256).astype(BF16)
            ck, cv = dupc(cache_k_l), dupc(cache_v_l)
            additive = lambda valid: jnp.asarray(np.where(valid, 0.0, -np.inf), F32)
            masks = (additive(_band_valid(g, pb)), additive(_band_valid(g, pb, tq // (CHUNK * g))),
                     additive(_band_valid(1, pb)))
            o = _attention(_swa_kernel, q, k, v, ck, cv, masks, [sink], sink_spec, width=1024, kv_width=256, pb=pb,
                           tq=tq, g=g, bp=bp, lp=lp, bs=bs, name="swa")
            undup = lambda a: jnp.concatenate([a[:, 0:64], a[:, 128:192]], axis=1).astype(F32)
            tail = lambda a: jnp.stack([undup(a[(b + 1) * lp - pb:(b + 1) * lp]) for b in range(bp)]).reshape(1, bp, pb, KVC, DHC)
            new = lambda a: undup(a[n_p:]).reshape(bs, ls_, KVC, DHC)
            sk_p, sv_p = tail(k), tail(v)
            sk_s = jnp.concatenate([cache_swa_k[i][:, ls_:], new(k)], axis=1)[None]
            sv_s = jnp.concatenate([cache_swa_v[i][:, ls_:], new(v)], axis=1)[None]
            os_, ws = [o], [w_out_l.astype(BF16)]
        xn, disp, meta, counts = _outproj_call(xres, os_, ws, mods_g[l], norm_ffn[l][None], wr, br, n_pad)
        z, tok = _moe_layer(disp, meta, counts, w_gate_up, w_down, l, n, n_pad, sort_rows)

    y_prompt, y_sample = _final_call(xn, z, mods_g[DEPTH - 1], norm_final[None], n_p)
    return (y_prompt.reshape(bp, lp, d), y_sample.reshape(bs, ls_, d),
            gla_p, gla_s, bk_p, bv_p, bk_s, bv_s, sk_p, sv_p, sk_s, sv_s)
```

```python
import functools

import numpy as np
import jax
import jax.numpy as jnp
from jax import lax
from jax.experimental import pallas as pl
from jax.experimental.pallas import tpu as pltpu
from jax.experimental.pallas import tpu_sc as plsc

F32 = jnp.float32
BF16 = jnp.bfloat16

D_MODEL = 1024
DEPTH = 2
CHUNK = 64
PAST_LEN = 4096
HA, DKA, DVA = 4, 64, 128
GATE_RANK = 16
GATE_TAU = 16.0
HB, DHB = 8, 64
N_PREV_B = 8
MAX_REL = 128
HC, KVC, DHC = 16, 2, 64
WINDOW = 128
ROPE_THETA = 10000.0
N_EXPERTS = 16
N_GROUPS = 4
EXP_PER_GROUP = 4
D_FF = 512
EPS = 1e-6

N_PAIRS = 6
N_BUCKETS = N_GROUPS * N_PAIRS
BUCKET_ROWS = 32
Y_SLABS = 4
DISP_SLABS = Y_SLABS + 1
TMO = 256
MOE_TILES = 2
SC_WINDOW = 128
SC_WORKERS = 32
SC_GROUP = 3

TM = 512
SEQ_ROWS = 16
SUB = 16
LOG2E = 1.4426950408889634
VMEM_LIMIT_MB = 48
VMEM_LIMIT_MOE_MB = 56


def _cparams(sem, vmem_mb=VMEM_LIMIT_MB):
    return pltpu.CompilerParams(dimension_semantics=sem, vmem_limit_bytes=vmem_mb * 1024 * 1024)


def _dot(a, b):
    return jnp.dot(a, b, preferred_element_type=F32)


def _dot_nt(a, b):
    return lax.dot_general(a, b, (((1,), (1,)), ((), ())), preferred_element_type=F32)


def _split(a):
    hi = a.astype(BF16)
    lo = (a - hi.astype(F32)).astype(BF16)
    return hi, lo


def _dot3(a, b):
    ah, al = _split(a)
    bh, bl = _split(b)
    return _dot(ah, bh) + _dot(ah, bl) + _dot(al, bh)


def _dot3_narrow(a, b):
    ah, al = _split(a)
    bh, bl = _split(b)
    n = b.shape[1]
    p = _dot(ah, jnp.concatenate([bh, bl], axis=1))
    return p[:, :n] + p[:, n:] + _dot(al, bh)


def _sigmoid(x):
    return 1.0 / (1.0 + jnp.exp(-x))


def _group_affine(y, mul, add):
    parts = []
    for gi in range(y.shape[0] // CHUNK):
        p = y[gi * CHUNK:(gi + 1) * CHUNK]
        if mul is not None:
            p = p * mul[gi:gi + 1]
        if add is not None:
            p = p + add[gi:gi + 1]
        parts.append(p)
    return jnp.concatenate(parts, axis=0)


def _norm_mod(x, g, shift, scale):
    ms = jnp.mean(x * x, axis=-1, keepdims=True)
    return _group_affine(x * lax.rsqrt(ms + EPS) * g, 1.0 + scale, shift)


def _mod_spec(part):
    return pl.BlockSpec((TM // CHUNK, D_MODEL), lambda i: (i, part))


SHIFT_MIX, SCALE_MIX, GATE_MIX, SHIFT_FFN, SCALE_FFN, GATE_FFN = range(6)


def _on_token_tile(xp_ref, xs_ref, n_prompt_tiles, body):
    @pl.when(pl.program_id(0) < n_prompt_tiles)
    def _():
        body(xp_ref)

    @pl.when(pl.program_id(0) >= n_prompt_tiles)
    def _():
        body(xs_ref)


RING = 3


def _ring_block(srcs, buf, sem, n_prompt_tiles=None):
    t = buf.shape[1]

    def copy(src, blk, slot):
        return pltpu.make_async_copy(src.at[pl.ds(pl.multiple_of(blk * t, t), t)], buf.at[slot], sem.at[slot])

    def start(step, slot):
        if len(srcs) == 1:
            copy(srcs[0], step, slot).start()
        elif isinstance(step, int):
            assert step < n_prompt_tiles
            copy(srcs[0], step, slot).start()
        else:
            @pl.when(step < n_prompt_tiles)
            def _():
                copy(srcs[0], step, slot).start()

            @pl.when(step >= n_prompt_tiles)
            def _():
                copy(srcs[1], step - n_prompt_tiles, slot).start()

    return _ring(buf, start, lambda slot: copy(srcs[0], 0, slot).wait())


def _ring(buf, start, wait):
    s = pl.program_id(0)

    @pl.when(s == 0)
    def _():
        for k in range(RING - 1):
            start(k, k)

    ahead = s + (RING - 1)

    @pl.when(ahead < pl.num_programs(0))
    def _():
        start(ahead, ahead % RING)

    slot = s % RING
    wait(slot)
    return buf.at[slot]


def _ring_window(src, buf, sem, axis):
    def copy(step, slot):
        idx = (slice(None),) * axis + (pl.ds(pl.multiple_of(step * TM, TM), TM),)
        return pltpu.make_async_copy(src.at[idx], buf.at[slot], sem.at[slot])

    return _ring(buf, lambda step, slot: copy(step, slot).start(), lambda slot: copy(0, slot).wait())


def _ring_scratch(d):
    return [pltpu.VMEM((RING, TM, d), F32), pltpu.SemaphoreType.DMA((RING,))]


def _token_specs(n_prompt_tiles, d):
    return [pl.BlockSpec((TM, d), lambda i: (jnp.minimum(i, n_prompt_tiles - 1), 0)),
            pl.BlockSpec((TM, d), lambda i: (0, 0))]


def _ada_kernel(c_ref, w_ref, b_ref, o_ref):
    c = c_ref[...]
    o_ref[0] = _dot3(c * _sigmoid(c), w_ref[0]) + b_ref[0]


def _ada_call(c16, w_ada, b_ada):
    d = D_MODEL
    tn = 1024
    return pl.pallas_call(
        _ada_kernel,
        grid=(DEPTH, 6 * d // tn),
        in_specs=[pl.BlockSpec((SEQ_ROWS, d), lambda l, j: (0, 0)),
                  pl.BlockSpec((1, d, tn), lambda l, j: (l, 0, j)),
                  pl.BlockSpec((1, 1, tn), lambda l, j: (l, 0, j))],
        out_specs=pl.BlockSpec((1, SEQ_ROWS, tn), lambda l, j: (l, 0, j)),
        out_shape=jax.ShapeDtypeStruct((DEPTH, SEQ_ROWS, 6 * d), F32),
        compiler_params=_cparams(("arbitrary", "arbitrary")),
        name="ada",
    )(c16, w_ada, b_ada.reshape(DEPTH, 1, 6 * d))


def _inproj_even_kernel(xp_ref, xs_ref, sh_ref, sc_ref, g_ref, w_ref, wla_ref, wg_ref, bg_ref,
                        qa_ref, ka_ref, va_ref, ra_ref, qb_ref, kb_ref, vb_ref, ga_ref, *, n_prompt_tiles):
    def body(x_ref):
        t = x_ref.shape[0]
        outs = ((qa_ref, 0, 256, DKA ** -0.5), (ka_ref, 256, 512, None), (va_ref, 512, 1024, None),
                (ra_ref, 1024, 1536, None), (qb_ref, 1536, 2048, DHB ** -0.5 * LOG2E), (kb_ref, 2048, 2560, None),
                (vb_ref, 2560, 3072, None))
        shift, scale_ = sh_ref[...], sc_ref[...]
        halves = [slice(0, t // 2), slice(t // 2, t)]
        grp = [slice(0, t // (2 * CHUNK)), slice(t // (2 * CHUNK), t // CHUNK)]
        hbs = [_norm_mod(x_ref[rs, :], g_ref[...], shift[gs], scale_[gs]).astype(BF16) for rs, gs in zip(halves, grp)]
        for rs, hb in zip(halves, hbs):
            zs = [_dot(hb, w_ref[:, lo:hi]) for _, lo, hi, _ in outs]
            la = _dot(hb, wla_ref[...])
            for z, (o_ref, _, _, scale) in zip(zs, outs):
                o_ref[rs, :] = (z if scale is None else z * scale).astype(BF16)
            gl = _dot3(la, wg_ref[...]) + bg_ref[...]
            ga_ref[rs, :] = -(jnp.maximum(-gl, 0.0) + jnp.log(1.0 + jnp.exp(-jnp.abs(gl)))) * (1.0 / GATE_TAU)

    _on_token_tile(xp_ref, xs_ref, n_prompt_tiles, body)


def _inproj_even_call(xp, xs, mods, g, w_main, w_la, w_gate, b_gate):
    d = xp.shape[1]
    npt = xp.shape[0] // TM
    n = xp.shape[0] + xs.shape[0]
    row = lambda i: (i, 0)
    const = lambda i: (0, 0)
    widths = (256, 256, 512, 512, 512, 512, 512)
    out_shape = [jax.ShapeDtypeStruct((n, w), BF16) for w in widths] + [jax.ShapeDtypeStruct((n, 256), F32)]
    out_specs = [pl.BlockSpec((TM, w), row) for w in widths] + [pl.BlockSpec((TM, 256), row)]
    return pl.pallas_call(
        functools.partial(_inproj_even_kernel, n_prompt_tiles=npt),
        grid=(n // TM,),
        in_specs=_token_specs(npt, d) + [
            _mod_spec(SHIFT_MIX), _mod_spec(SCALE_MIX),
            pl.BlockSpec((1, d), const),
            pl.BlockSpec(w_main.shape, const), pl.BlockSpec(w_la.shape, const),
            pl.BlockSpec(w_gate.shape, const), pl.BlockSpec(b_gate.shape, const)],
        out_specs=out_specs, out_shape=out_shape,
        compiler_params=_cparams(("parallel",)),
        name="inproj_even",
    )(xp, xs, mods, mods, g, w_main, w_la, w_gate, b_gate)


def _rope(x, cos, sin_signed):
    t, w = x.shape
    lane = lax.broadcasted_iota(jnp.int32, (1, w), 1)
    first_half = (lane & 63) < 32
    rot = jnp.where(first_half, pltpu.roll(x, w - 32, 1), pltpu.roll(x, 32, 1))
    reps = w // 128
    return x * jnp.tile(cos, (1, reps)) + rot * jnp.tile(sin_signed, (1, reps))


def _unpack_pairs(slabs, dtype):
    lo = [pltpu.bitcast(s << 16, F32) for s in slabs]
    hi = [pltpu.bitcast(s & jnp.int32(-65536), F32) for s in slabs]
    return jnp.concatenate(lo + hi, axis=1).astype(dtype)


def _pack_pairs(x):
    bits = pltpu.bitcast(x.astype(BF16).astype(F32), jnp.int32)
    half = x.shape[1] // 2
    packed = ((bits[:, :half] >> 16) & jnp.int32(0xFFFF)) | (bits[:, half:] & jnp.int32(-65536))
    return [packed[:, 128 * s:128 * (s + 1)] for s in range(half // 128)]


def _add_moe(xn_ref, z_ref, gate_ref):
    y = _unpack_pairs([z_ref[s] for s in range(z_ref.shape[0])], F32)
    return xn_ref[...] + _group_affine(y, gate_ref[...], None)


def _rope_tables(lp, ls_, bp, bs):
    assert PAST_LEN + ls_ <= lp and lp % 128 == 0 and bs * ls_ == TM
    half = DHC // 2
    inv = ROPE_THETA ** (-jnp.arange(half, dtype=F32) / half)
    inv = jnp.tile(inv, 128 // half)
    sign = jnp.asarray(np.tile(np.repeat([-1.0, 1.0], half), 128 // DHC), F32)
    a = jnp.asarray(np.arange(lp // 128) * 128, F32)[:, None] * inv[None, :]
    b = jnp.asarray(np.arange(128), F32)[:, None] * inv[None, :]
    ca, sa, cb, sb = jnp.cos(a)[:, None], jnp.sin(a)[:, None], jnp.cos(b)[None], jnp.sin(b)[None]
    cos = (ca * cb - sa * sb).reshape(lp, 128)
    sin = ((sa * cb + ca * sb) * sign).reshape(lp, 128)
    with_sample = lambda t: jnp.concatenate([t, jnp.tile(t[PAST_LEN:PAST_LEN + ls_], (bs, 1))], axis=0)
    tiles = lp // TM
    return with_sample(cos), with_sample(sin), lambda i: (jnp.where(i < bp * tiles, i % tiles, tiles), 0)


def _inproj_odd_kernel(xn_hbm, z_hbm, gate_ref, sh_ref, sc_ref, g_ref, cos_ref, sin_ref, w_ref,
                       x_ref, q_ref, k_ref, v_ref, xbuf, xsem, zbuf, zsem):
    xn_ref = _ring_block([xn_hbm], xbuf, xsem)
    z_ref = _ring_window(z_hbm, zbuf, zsem, axis=1)
    t = xn_ref.shape[0]
    halves = [slice(0, t // 2), slice(t // 2, t)]
    grp = [slice(0, t // (2 * CHUNK)), slice(t // (2 * CHUNK), t // CHUNK)]
    gate, shift, scale = gate_ref[...], sh_ref[...], sc_ref[...]
    xs = []
    for rs, gs in zip(halves, grp):
        y = _unpack_pairs([z_ref[s, rs, :] for s in range(z_ref.shape[0])], F32)
        xs.append(xn_ref[rs, :] + _group_affine(y, gate[gs], None))
    for rs, x in zip(halves, xs):
        x_ref[rs, :] = x
    hbs = [_norm_mod(x, g_ref[...], shift[gs], scale[gs]).astype(BF16) for x, gs in zip(xs, grp)]
    qs = [_dot(hb, w_ref[:, 0:1024]) for hb in hbs]
    ks = [_dot(hb, w_ref[:, 1024:1280]) for hb in hbs]
    vs = [_dot(hb, w_ref[:, 1280:1536]) for hb in hbs]
    for rs, q, k, v in zip(halves, qs, ks, vs):
        cos, sin = cos_ref[rs, :], sin_ref[rs, :]
        q_ref[rs, :] = (_rope(q, cos, sin) * (DHC ** -0.5 * LOG2E)).astype(BF16)
        k_ref[rs, :] = _rope(k, cos, sin).astype(BF16)
        v_ref[rs, :] = v.astype(BF16)


def _inproj_odd_call(xn, z, mods_prev, mods, g, cos, sin, rope_map, w):
    n, d = xn.shape
    row = lambda i: (i, 0)
    const = lambda i: (0, 0)
    widths = (1024, 256, 256)
    return pl.pallas_call(
        _inproj_odd_kernel,
        grid=(n // TM,),
        in_specs=[pl.BlockSpec(memory_space=pl.ANY), pl.BlockSpec(memory_space=pl.ANY),
                  _mod_spec(GATE_FFN), _mod_spec(SHIFT_MIX), _mod_spec(SCALE_MIX),
                  pl.BlockSpec((1, d), const),
                  pl.BlockSpec((TM, 128), rope_map), pl.BlockSpec((TM, 128), rope_map),
                  pl.BlockSpec(w.shape, const)],
        out_specs=[pl.BlockSpec((TM, d), row)] + [pl.BlockSpec((TM, wd), row) for wd in widths],
        out_shape=[jax.ShapeDtypeStruct((n, d), F32)] + [jax.ShapeDtypeStruct((n, wd), BF16) for wd in widths],
        scratch_shapes=_ring_scratch(d) + [pltpu.VMEM((RING, z.shape[0], TM, 128), z.dtype),
                                           pltpu.SemaphoreType.DMA((RING,))],
        compiler_params=_cparams(("arbitrary",)),
        name="inproj_odd",
    )(xn, z, mods_prev, mods, mods, g, cos, sin, w)


def _gla_tri():
    t = np.arange(CHUNK)[:, None]
    s = np.arange(CHUNK)[None, :]
    cum = s <= t
    start = s < (t // SUB) * SUB
    end = s < (t // SUB + 1) * SUB
    return jnp.asarray(np.concatenate([cum, start, end], axis=0).astype(np.float32), dtype=BF16)


def _gla_kernel(q_ref, k_ref, v_ref, g_ref, r_ref, s0_ref, gn_ref, tri_ref, o_ref, sout_ref, s_ref, *, nb):
    c_ = CHUNK
    nsub = c_ // SUB

    @pl.when(pl.program_id(1) == 0)
    def _():
        s_ref[...] = s0_ref[0]

    tri = tri_ref[...]
    lane = lax.broadcasted_iota(jnp.int32, (1, 128), 1)
    hmask = [jnp.where(lane < DKA, 1.0, 0.0), jnp.where(lane >= DKA, 1.0, 0.0)]
    ti = lax.broadcasted_iota(jnp.int32, (c_, c_), 0)
    si = lax.broadcasted_iota(jnp.int32, (c_, c_), 1)
    rb, cb = ti >> 4, si >> 4
    m_diag = (rb == cb) & (si <= ti)
    m_off = [(cb == j) & (rb > j) for j in range(nsub - 1)]
    hk = HA * DKA
    gn = gn_ref[...]

    chunks = range(nb)
    heads = [(p, hh) for p in range(HA // 2) for hh in range(2)]
    rows = [slice(c * c_, (c + 1) * c_) for c in chunks]
    pair = [slice(128 * p, 128 * (p + 1)) for p in range(HA // 2)]
    css = []
    for c in chunks:
        g_hi, g_lo = _split(g_ref[rows[c], :])
        css.append(_dot(tri, g_hi) + _dot(tri, g_lo))
    lhs1, lhs2, kds, kes, q_inter, klts, dcols = [], [], [], [], [], [], []
    for c in chunks:
        b, rs, re = css[c][0:c_], css[c][c_:2 * c_], css[c][2 * c_:3 * c_]
        q = q_ref[rows[c], :].astype(F32)
        k = k_ref[rows[c], :].astype(F32)
        bl = b[c_ - 1:c_, :]
        qd = q * jnp.exp(b - rs)
        kd = k * jnp.exp(rs - b)
        ke = k * jnp.exp(re - b)
        qi = q * jnp.exp(b)
        kl = k * jnp.exp(bl - b)
        ql = [q * jnp.exp(jnp.minimum(b - b[SUB * (j + 1) - 1:SUB * (j + 1), :], 0.0)) for j in range(nsub - 1)]
        dcols.append(jnp.broadcast_to(jnp.exp(bl), (8, hk)).T[:, 0:1])
        kds.append([(kd[:, pair[p]] * hmask[hh]).astype(BF16) for p, hh in heads])
        kes.append([(ke[:, pair[p]] * hmask[hh]).astype(BF16) for p, hh in heads])
        klts.append([kl[:, ls].T.astype(BF16) for ls in pair])
        lhs1.append([qd[:, ls].astype(BF16) for ls in pair])
        lhs2.append([jnp.concatenate([ql[j][:, ls] for j in range(nsub - 1)], axis=0).astype(BF16) for ls in pair])
        q_inter.append([(qi[:, pair[p]] * hmask[hh]).astype(BF16) for p, hh in heads])
    a1s = [[_dot_nt(lhs1[c][p], kds[c][h]) for h, (p, hh) in enumerate(heads)] for c in chunks]
    a2s = [[_dot_nt(lhs2[c][p], kes[c][h]) for h, (p, hh) in enumerate(heads)] for c in chunks]
    atts = []
    for c in chunks:
        per_head = []
        for h in range(HA):
            att = jnp.zeros((c_, c_), F32)
            for j in reversed(range(nsub - 1)):
                att = jnp.where(m_off[j], a2s[c][h][j * c_:(j + 1) * c_], att)
            per_head.append(jnp.where(m_diag, a1s[c][h], att).astype(BF16))
        atts.append(per_head)
    vs_ = [[v_ref[rows[c], DVA * h:DVA * (h + 1)] for h in range(HA)] for c in chunks]
    o_intra = [[_dot(atts[c][h], vs_[c][h]) for h in range(HA)] for c in chunks]
    upds = [jnp.concatenate([_dot(klts[c][p][DKA * hh:DKA * (hh + 1)], vs_[c][2 * p + hh]) for p, hh in heads], axis=0)
            for c in chunks]

    s_cur = s_ref[...]
    s_in = []
    for c in chunks:
        s_in.append(s_cur.astype(BF16))
        s_cur = dcols[c] * s_cur + upds[c]
    s_ref[...] = s_cur
    sout_ref[0] = s_cur

    for c in chunks:
        for h in range(HA):
            o = o_intra[c][h] + _dot(q_inter[c][h], s_in[c][pair[h // 2], :])
            ms = jnp.mean(o * o, axis=-1, keepdims=True)
            vs = slice(DVA * h, DVA * (h + 1))
            rr = r_ref[rows[c], vs].astype(F32)
            o_ref[rows[c], vs] = (o * lax.rsqrt(ms + EPS) * gn * (rr * _sigmoid(rr))).astype(BF16)


def _gla_call(q, k, v, g, r, s0, gn, o_prev, *, n_seq, seq_rows, row0, nb):
    tq = nb * CHUNK
    steps = seq_rows // tq
    blk0 = row0 // tq
    row = lambda b, j: (blk0 + b * steps + j, 0)
    const = lambda b, j: (0, 0)
    tri = _gla_tri()
    in_specs = [pl.BlockSpec((tq, 256), row), pl.BlockSpec((tq, 256), row), pl.BlockSpec((tq, 512), row),
                pl.BlockSpec((tq, 256), row), pl.BlockSpec((tq, 512), row),
                pl.BlockSpec((1, 256, 128), lambda b, j: (b, 0, 0)),
                pl.BlockSpec((1, 128), const), pl.BlockSpec(tri.shape, const)]
    args = [q, k, v, g, r, s0, gn, tri]
    aliases = {}
    if o_prev is not None:
        in_specs.append(pl.BlockSpec(memory_space=pl.ANY))
        args.append(o_prev)
        aliases = {len(args) - 1: 0}
    kern = functools.partial(_gla_kernel, nb=nb)
    if o_prev is not None:
        kern = _drop_arg(kern, 8)
    return pl.pallas_call(
        kern,
        grid=(n_seq, steps),
        in_specs=in_specs,
        out_specs=[pl.BlockSpec((tq, 512), row), pl.BlockSpec((1, 256, 128), lambda b, j: (b, 0, 0))],
        out_shape=[jax.ShapeDtypeStruct((q.shape[0], 512), BF16), jax.ShapeDtypeStruct((n_seq, 256, 128), F32)],
        scratch_shapes=[pltpu.VMEM((256, 128), F32)],
        input_output_aliases=aliases,
        compiler_params=_cparams(("arbitrary", "arbitrary")),
        name="gla",
    )(*args)


def _drop_arg(fn, idx):
    def wrapped(*refs):
        return fn(*refs[:idx], *refs[idx + 1:])
    return wrapped


def _window(prev_ref, cur_ref, lo, hi, pb, ls):
    if lo < pb:
        return jnp.concatenate([prev_ref[lo:pb, ls], cur_ref[0:hi - pb, ls]], axis=0)
    return cur_ref[lo - pb:hi - pb, ls]


def _band_kernel(q_ref, kp_ref, kc_ref, vp_ref, vc_ref, bias_ref, o_ref, *, g, n_sub, pb):
    qs = CHUNK * g
    kw_rows = pb + qs
    lane = lax.broadcasted_iota(jnp.int32, (1, 128), 1)
    low = lane < DHB
    hmask = [jnp.where(low, 1.0, 0.0), jnp.where(low, 0.0, 1.0)]
    for s in range(n_sub):
        sb = s if bias_ref.shape[0] > 1 else 0
        rows = slice(qs * s, qs * (s + 1))
        lanes = [slice(128 * p, 128 * (p + 1)) for p in range(HB // 2)]
        heads = [(p, hh) for p in range(HB // 2) for hh in range(2)]
        qps = [q_ref[rows, ls].astype(F32) for ls in lanes]
        kws = [_window(kp_ref, kc_ref, qs * s, qs * s + kw_rows, pb, ls) for ls in lanes]
        vws = [_window(vp_ref, vc_ref, qs * s, qs * s + kw_rows, pb, ls) for ls in lanes]
        qq = [jnp.concatenate([(qps[p] * hmask[hh]).astype(BF16) for hh in range(2)], axis=0) for p in range(HB // 2)]
        sc2 = [_dot_nt(qq[p], kws[p]) for p in range(HB // 2)]
        scs = [sc2[p][qs * hh:qs * (hh + 1)] + bias_ref[sb, 2 * p + hh] for p, hh in heads]
        pes = [jnp.exp2(sc - jnp.max(sc, axis=-1, keepdims=True)) for sc in scs]
        pp = [jnp.concatenate([pes[2 * p + hh].astype(BF16) for hh in range(2)], axis=0) for p in range(HB // 2)]
        o2 = [_dot(pp[p], vws[p]) for p in range(HB // 2)]
        outs = [o2[p][qs * hh:qs * (hh + 1)] / jnp.sum(pes[2 * p + hh], axis=-1, keepdims=True) for p, hh in heads]
        for p, ls in enumerate(lanes):
            o_ref[rows, ls] = jnp.where(low, outs[2 * p], outs[2 * p + 1]).astype(BF16)


def _band_valid(g, pb, n_sub=None):
    rows, kw = CHUNK * g, pb + CHUNK * g
    r = np.arange(rows)[:, None]
    c = np.arange(kw)[None, :]
    dd = c // CHUNK - r // CHUNK
    band = (dd >= 0) & (dd <= pb // CHUNK)
    if n_sub is None:
        return band[None]
    return np.stack([band & (c >= pb - rows * s) for s in range(n_sub)])


def _band_bias(table, g, pb, valid):
    rows, kw = CHUNK * g, pb + CHUNK * g
    period = kw + rows
    m = np.arange(period)
    m = np.where(m < kw, m, m - period)
    ext = table[:, np.clip(m - pb, -MAX_REL, MAX_REL) + MAX_REL] * LOG2E
    flat = jnp.tile(ext, (1, rows))[:, :rows * (period - 1)]
    bias = flat.reshape(table.shape[0], rows, period - 1)[:, :, :kw]
    return jnp.where(valid[:, None], bias[None], -jnp.inf)


def _attn_call(kernel, q, kp, kc, vp, vc, extra, extra_specs, o_prev, *, width, kv_width, tq, pb,
               n_blocks, blk_map, prev_map, name):
    row = lambda i: (blk_map(i), 0)
    prev = lambda i: (prev_map(i), 0)
    in_specs = [pl.BlockSpec((tq, width), row),
                pl.BlockSpec((pb, kv_width), prev), pl.BlockSpec((tq, kv_width), row),
                pl.BlockSpec((pb, kv_width), prev), pl.BlockSpec((tq, kv_width), row)] + extra_specs
    args = [q, kp, kc, vp, vc] + extra
    aliases = {}
    if o_prev is not None:
        in_specs.append(pl.BlockSpec(memory_space=pl.ANY))
        args.append(o_prev)
        aliases = {len(args) - 1: 0}
        kernel = _drop_arg(kernel, len(args) - 1)
    return pl.pallas_call(
        kernel,
        grid=(n_blocks,),
        in_specs=in_specs,
        out_specs=pl.BlockSpec((tq, width), row),
        out_shape=jax.ShapeDtypeStruct((q.shape[0], width), BF16),
        input_output_aliases=aliases,
        compiler_params=_cparams(("parallel",)),
        name=name,
    )(*args)


def _attention(kernel_fn, q, k, v, cache_k, cache_v, masks, extra, extra_specs, *, width, kv_width, pb, tq, g,
               bp, lp, bs, name):
    bps = lp // tq
    n_sub = tq // (CHUNK * g)
    spec = lambda a: [pl.BlockSpec(a.shape, lambda i: (0,) * a.ndim)]
    kern = functools.partial(kernel_fn, g=g, n_sub=n_sub, pb=pb)
    common = dict(width=width, kv_width=kv_width, pb=pb)
    main = lambda i: (i // (bps - 1)) * bps + i % (bps - 1) + 1
    o = _attn_call(kern, q, k, k, v, v, [masks[0]] + extra, spec(masks[0]) + extra_specs, None, tq=tq,
                   n_blocks=bp * (bps - 1), blk_map=main, prev_map=lambda i: main(i) * (tq // pb) - 1,
                   name=name + "_main", **common)
    first = lambda i: i * bps
    o = _attn_call(kern, q, k, k, v, v, [masks[1]] + extra, spec(masks[1]) + extra_specs, o, tq=tq,
                   n_blocks=bp, blk_map=first, prev_map=lambda i: jnp.maximum(first(i) * (tq // pb) - 1, 0),
                   name=name + "_first", **common)
    samp = functools.partial(kernel_fn, g=1, n_sub=1, pb=pb)
    return _attn_call(samp, q, cache_k, k, cache_v, v, [masks[2]] + extra, spec(masks[2]) + extra_specs, o, tq=CHUNK,
                      n_blocks=bs, blk_map=lambda i: bp * lp // CHUNK + i, prev_map=lambda i: i,
                      name=name + "_sample", **common)


def _swa_kernel(q_ref, kp_ref, kc_ref, vp_ref, vc_ref, mask_ref, sink_ref, o_ref, *, g, n_sub, pb):
    qs = CHUNK * g
    kw_rows = pb + qs
    lane = lax.broadcasted_iota(jnp.int32, (1, 128), 1)
    low = lane < DHC
    hmask = [jnp.where(low, 1.0, 0.0), jnp.where(low, 0.0, 1.0)]
    pairs_per_kv = HC // KVC // 2
    for s in range(n_sub):
        msk = mask_ref[s if mask_ref.shape[0] > 1 else 0]
        rows = slice(qs * s, qs * (s + 1))
        kws = [_window(kp_ref, kc_ref, qs * s, qs * s + kw_rows, pb, slice(128 * kv, 128 * (kv + 1))) for kv in range(KVC)]
        vws = [_window(vp_ref, vc_ref, qs * s, qs * s + kw_rows, pb, slice(128 * kv, 128 * (kv + 1))) for kv in range(KVC)]
        heads = [(j, hh) for j in range(HC // 2) for hh in range(2)]
        qps = [q_ref[rows, 128 * j:128 * (j + 1)].astype(F32) for j in range(HC // 2)]
        per_kv = 2 * pairs_per_kv
        qq = [jnp.concatenate([(qps[j] * hmask[hh]).astype(BF16) for j, hh in heads[per_kv * kv:per_kv * (kv + 1)]],
                              axis=0) for kv in range(KVC)]
        sc2 = [_dot_nt(qq[kv], kws[kv]) for kv in range(KVC)]
        scs = [sc2[u // per_kv][qs * (u % per_kv):qs * (u % per_kv + 1)] + msk for u in range(len(heads))]
        sks = [sink_ref[0, 2 * j + hh] for j, hh in heads]
        ms = [jnp.maximum(jnp.max(sc, axis=-1, keepdims=True), sk) for sc, sk in zip(scs, sks)]
        pes = [jnp.exp2(sc - m) for sc, m in zip(scs, ms)]
        pp = [jnp.concatenate([pe.astype(BF16) for pe in pes[per_kv * kv:per_kv * (kv + 1)]], axis=0) for kv in range(KVC)]
        o2 = [_dot(pp[kv], vws[kv]) for kv in range(KVC)]
        outs = [o2[u // per_kv][qs * (u % per_kv):qs * (u % per_kv + 1)]
                / (jnp.sum(pes[u], axis=-1, keepdims=True) + jnp.exp2(sks[u] - ms[u])) for u in range(len(heads))]
        for j in range(HC // 2):
            o_ref[rows, 128 * j:128 * (j + 1)] = jnp.where(low, outs[2 * j], outs[2 * j + 1]).astype(BF16)


def _route(logits_t):
    a = [logits_t[4 * j:4 * j + 4] for j in range(EXP_PER_GROUP)]

    def first_argmax(vals, m):
        idx = jnp.full(m.shape, float(len(vals) - 1), F32)
        for j in reversed(range(len(vals) - 1)):
            idx = jnp.where(vals[j] == m, float(j), idx)
        return idx

    m1 = functools.reduce(jnp.maximum, a)
    i1 = first_argmax(a, m1)
    bsec = [jnp.where(i1 == float(j), -jnp.inf, a[j]) for j in range(EXP_PER_GROUP)]
    m2 = functools.reduce(jnp.maximum, bsec)
    i2 = first_argmax(bsec, m2)
    rows = lambda x: [x[gi:gi + 1] for gi in range(N_GROUPS)]
    gm = functools.reduce(jnp.maximum, rows(m1))
    gscore = jnp.exp(m1 - gm) + jnp.exp(m2 - gm)
    gs = rows(gscore)
    gsel = first_argmax(gs, functools.reduce(jnp.maximum, gs))

    def pick(x):
        xr = rows(x)
        out = xr[N_GROUPS - 1]
        for gi in reversed(range(N_GROUPS - 1)):
            out = jnp.where(gsel == float(gi), xr[gi], out)
        return out

    p1 = jnp.exp(pick(m1) - gm)
    p2 = jnp.exp(pick(m2) - gm)
    w1 = p1 / (p1 + p2)
    w2 = p2 / (p1 + p2)
    s1, s2 = pick(i1), pick(i2)
    lo, hi = jnp.minimum(s1, s2), jnp.maximum(s1, s2)
    pair = jnp.where(lo == 0.0, hi - 1.0, jnp.where(lo == 1.0, hi + 1.0, 5.0))
    bucket = gsel * float(N_PAIRS) + pair
    first_is_lo = s1 < s2
    return bucket, jnp.where(first_is_lo, w1, w2), jnp.where(first_is_lo, w2, w1)


def _outproj_kernel(*refs, n_x, n_o, n_prompt_tiles):
    x_refs = refs[:n_x]
    o_refs = refs[n_x:n_x + n_o]
    w_refs = refs[n_x + n_o:n_x + 2 * n_o]
    (gate_ref, nf_ref, sh_ref, sc_ref, wr_ref, br_ref, tri_ref,
     xn_ref, disp_ref, meta_ref, cnt_ref, run_ref, xbuf, xsem) = refs[n_x + 2 * n_o:len(refs) - 2 * n_o]
    o_scratch = refs[len(refs) - 2 * n_o:]
    o_refs = [_ring_window(o_hbm, o_scratch[2 * i], o_scratch[2 * i + 1], axis=0) for i, o_hbm in enumerate(o_refs)]
    t = xn_ref.shape[0]

    @pl.when(pl.program_id(0) == 0)
    def _():
        run_ref[...] = jnp.zeros_like(run_ref)

    x_src = _ring_block(list(x_refs), xbuf, xsem, n_prompt_tiles)

    halves = [slice(0, t // 2), slice(t // 2, t)]
    grp = [slice(0, t // (2 * CHUNK)), slice(t // (2 * CHUNK), t // CHUNK)]
    ys = []
    for rs in halves:
        y = _dot(o_refs[0][rs, :], w_refs[0][...])
        for i in range(1, n_o):
            y = y + _dot(o_refs[i][rs, :], w_refs[i][...])
        ys.append(y)
    gate, shift, scale = gate_ref[...], sh_ref[...], sc_ref[...]
    gys = [_group_affine(y, gate[gs], None) for y, gs in zip(ys, grp)]

    for rs, gy in zip(halves, gys):
        xn_ref[rs, :] = x_src[rs, :] + gy
    hs = [_norm_mod(xn_ref[rs, :], nf_ref[...], shift[gs], scale[gs]) for rs, gs in zip(halves, grp)]
    for rs, h in zip(halves, hs):
        for s, slab in enumerate(_pack_pairs(h)):
            disp_ref[s, rs, :] = slab
    logits_t = [(_dot3_narrow(h, wr_ref[...]) + br_ref[...]).T[0:N_EXPERTS] for h in hs]
    bucket, w_lo, w_hi = _route(jnp.concatenate(logits_t, axis=1))
    r128 = lax.broadcasted_iota(jnp.int32, (128, t), 0)
    tok = (pl.program_id(0) * t + lax.broadcasted_iota(jnp.int32, (1, t), 1)).astype(F32)
    aux = jnp.where(r128 == 0, w_lo, jnp.where(r128 == 1, w_hi, jnp.where(r128 == 2, tok, 0.0))).T
    disp_ref[disp_ref.shape[0] - 1] = pltpu.bitcast(aux, jnp.int32)
    brow = lax.broadcasted_iota(jnp.int32, (BUCKET_ROWS, t), 0).astype(F32)
    onehot = jnp.where(brow == bucket, 1.0, 0.0)
    before = _dot(onehot.astype(BF16), tri_ref[...]) + run_ref[:, 0:1]
    rank = jnp.sum(onehot * before, axis=0, keepdims=True)
    run_ref[...] = run_ref[...] + jnp.sum(onehot, axis=1, keepdims=True)
    cnt_ref[...] = run_ref[...]
    r8 = lax.broadcasted_iota(jnp.int32, (8, t), 0)
    meta_ref[...] = jnp.where(r8 == 0, bucket, jnp.where(r8 == 1, rank, 0.0)).astype(jnp.int32)


def _outproj_call(xs_, os_, ws, mods, nf, wr, br, n_pad):
    d = xs_[0].shape[1]
    n = sum(a.shape[0] for a in xs_)
    npt = xs_[0].shape[0] // TM
    row = lambda i: (i, 0)
    const = lambda i: (0, 0)
    n_o = len(os_)
    in_specs = ([pl.BlockSpec(memory_space=pl.ANY) for _ in xs_]
                + [pl.BlockSpec(memory_space=pl.ANY) for _ in os_]
                + [pl.BlockSpec(w.shape, const) for w in ws]
                + [_mod_spec(GATE_MIX), pl.BlockSpec((1, d), const),
                   _mod_spec(SHIFT_FFN), _mod_spec(SCALE_FFN),
                   pl.BlockSpec(wr.shape, const), pl.BlockSpec(br.shape, const),
                   pl.BlockSpec((TM, TM), const)])
    tri = jnp.asarray(np.triu(np.ones((TM, TM), np.float32), k=1), dtype=BF16)
    return pl.pallas_call(
        functools.partial(_outproj_kernel, n_x=len(xs_), n_o=n_o, n_prompt_tiles=npt),
        grid=(n // TM,),
        in_specs=in_specs,
        out_specs=[pl.BlockSpec((TM, d), row), pl.BlockSpec((DISP_SLABS, TM, 128), lambda i: (0, i, 0)),
                   pl.BlockSpec((8, TM), lambda i: (0, i)), pl.BlockSpec((BUCKET_ROWS, 128), const)],
        out_shape=[jax.ShapeDtypeStruct((n, d), F32), jax.ShapeDtypeStruct((DISP_SLABS, n_pad, 128), jnp.int32),
                   jax.ShapeDtypeStruct((8, n), jnp.int32), jax.ShapeDtypeStruct((BUCKET_ROWS, 128), F32)],
        scratch_shapes=([pltpu.VMEM((BUCKET_ROWS, 128), F32)] + _ring_scratch(d)
                        + [sc for o in os_ for sc in (pltpu.VMEM((RING, TM, o.shape[1]), o.dtype),
                                                      pltpu.SemaphoreType.DMA((RING,)))]),
        compiler_params=_cparams(("arbitrary",)),
        name="outproj_router",
    )(*xs_, *os_, *ws, mods, nf, mods, mods, wr, br, tri)


def _sc_mesh():
    return plsc.VectorSubcoreMesh(core_axis_name="core", subcore_axis_name="subcore")


def _sc_scatter_rows(src, idx, n_out):
    r = idx.shape[0]
    k = SC_GROUP
    w_per = r // (SC_WINDOW * SC_WORKERS)
    assert idx.shape == (src.shape[0],) and r % (SC_WINDOW * SC_WORKERS) == 0 and w_per % k == 0
    n_groups = w_per // k

    @functools.partial(
        pl.kernel, out_type=jax.ShapeDtypeStruct((n_out, 128), src.dtype), mesh=_sc_mesh(),
        scratch_types=[pltpu.VMEM((w_per, SC_WINDOW), jnp.int32),
                       pltpu.VMEM((2 * k, SC_WINDOW, 128), src.dtype),
                       pltpu.SemaphoreType.DMA((2,)), pltpu.SemaphoreType.DMA((2,))])
    def copy(x_hbm, i_hbm, o_hbm, ibuf, xbuf, in_sem, out_sem):
        wid = lax.axis_index("core") * (SC_WORKERS // 2) + lax.axis_index("subcore")
        pltpu.sync_copy(i_hbm.at[wid], ibuf)
        first = wid * w_per

        def start_in(g, slot):
            return [pltpu.async_copy(x_hbm.at[pl.ds((first + g * k + c) * SC_WINDOW, SC_WINDOW)],
                                     xbuf.at[slot * k + c], in_sem.at[slot]) for c in range(k)]

        def start_out(g, slot):
            return [pltpu.async_copy(xbuf.at[slot * k + c], o_hbm.at[ibuf.at[g * k + c]], out_sem.at[slot])
                    for c in range(k)]

        pending_in = start_in(0, 0)
        for g in range(n_groups):
            slot = g % 2
            for cp in pending_in:
                cp.wait()
            pending_out = start_out(g, slot)
            if g + 1 < n_groups:
                pending_in = start_in(g + 1, 1 - slot)
            for cp in pending_out:
                cp.wait()

    return copy(src, idx.reshape(SC_WORKERS, w_per, SC_WINDOW))


def _moe_kernel(elo_ref, ehi_ref, nvalid_ref, xs_ref, *refs, n_tok, dump_tiles):
    w_refs, (y_ref, tok_ref) = refs[:4 * MOE_TILES], refs[4 * MOE_TILES:]
    step = pl.program_id(0)
    t = TMO
    tiles = range(MOE_TILES)
    rows = [slice(t * j, t * (j + 1)) for j in tiles]
    auxs = [pltpu.bitcast(xs_ref[Y_SLABS, rows[j], :], F32) for j in tiles]
    r = lax.broadcasted_iota(jnp.int32, (1, t), 1)
    for j in tiles:
        i = step * MOE_TILES + j
        spare = n_tok + (i % dump_tiles) * t + r
        tok = jnp.where(r < nvalid_ref[i], auxs[j].T[2:3, :].astype(jnp.int32), spare)
        for c in range(t // 128):
            tok_ref[j, c:c + 1, :] = tok[:, 128 * c:128 * (c + 1)]

    any_tokens = nvalid_ref[step * MOE_TILES] > 0
    for j in range(1, MOE_TILES):
        any_tokens = jnp.logical_or(any_tokens, nvalid_ref[step * MOE_TILES + j] > 0)

    @pl.when(any_tokens)
    def _():
        units = [(j, e) for j in tiles for e in range(2)]
        hs = [_unpack_pairs([xs_ref[s, rows[j], :] for s in range(Y_SLABS)], BF16) for j in tiles]
        abs_ = [_dot(hs[j], w_refs[4 * j + e][0, 0]) for j, e in units]
        acts = [(ab[:, :D_FF] * _sigmoid(ab[:, :D_FF]) * ab[:, D_FF:]).astype(BF16) for ab in abs_]
        ys = [_dot(act, w_refs[4 * j + 2 + e][0, 0]) for act, (j, e) in zip(acts, units)]
        for j in tiles:
            acc = auxs[j][:, 0:1] * ys[2 * j] + auxs[j][:, 1:2] * ys[2 * j + 1]
            for s, slab in enumerate(_pack_pairs(acc)):
                y_ref[s, rows[j], :] = slab

    @pl.when(jnp.logical_not(any_tokens))
    def _():
        y_ref[...] = jnp.zeros_like(y_ref)


def _moe_call(xs, elo, ehi, nvalid, wgu, wdn, n_tiles, n_tok, dump_tiles):
    d = wgu.shape[2]
    m = MOE_TILES
    assert n_tiles % m == 0
    weight_specs = []
    for j in range(m):
        for shape in ((1, 1, d, 2 * D_FF), (1, 1, D_FF, d)):
            for sel in range(2):
                weight_specs.append(pl.BlockSpec(
                    shape, lambda i, lo, hi, v, j=j, sel=sel: (0, (lo, hi)[sel][m * i + j], 0, 0)))
    weights = [w for _ in range(m) for w in (wgu, wgu, wdn, wdn)]
    return pl.pallas_call(
        functools.partial(_moe_kernel, n_tok=n_tok, dump_tiles=dump_tiles),
        grid_spec=pltpu.PrefetchScalarGridSpec(
            num_scalar_prefetch=3,
            grid=(n_tiles // m,),
            in_specs=[pl.BlockSpec((DISP_SLABS, m * TMO, 128), lambda i, lo, hi, v: (0, i, 0))] + weight_specs,
            out_specs=[pl.BlockSpec((Y_SLABS, m * TMO, 128), lambda i, lo, hi, v: (0, i, 0)),
                       pl.BlockSpec((m, TMO // 128, 128), lambda i, lo, hi, v: (i, 0, 0))]),
        out_shape=[jax.ShapeDtypeStruct((Y_SLABS, n_tiles * TMO, 128), jnp.int32),
                   jax.ShapeDtypeStruct((n_tiles, TMO // 128, 128), jnp.int32)],
        compiler_params=_cparams(("arbitrary",), vmem_mb=VMEM_LIMIT_MOE_MB),
        name="moe_grouped",
    )(elo, ehi, nvalid, xs, *weights)


def _after(x, token):
    return lax.optimization_barrier((x, token))[0]


def _cast_kernel(after_ref, w_ref, o_ref):
    o_ref[...] = w_ref[...].astype(o_ref.dtype)


def _cast_call(w, layer, after):
    _, e, k, n = w.shape
    return pl.pallas_call(
        _cast_kernel,
        grid=(e,),
        in_specs=[pl.BlockSpec(memory_space=pl.ANY), pl.BlockSpec((1, 1, k, n), lambda i: (layer, i, 0, 0))],
        out_specs=pl.BlockSpec((1, 1, k, n), lambda i: (0, i, 0, 0)),
        out_shape=jax.ShapeDtypeStruct((1, e, k, n), BF16),
        compiler_params=_cparams(("parallel",)),
        name="cast_weights",
    )(after, w)


def _moe_layer(disp, meta, counts, w_gate_up, w_down, layer, n, n_pad, sort_rows):
    n_tiles = sort_rows // TMO
    wgu = _cast_call(w_gate_up, layer, counts)
    wdn = _cast_call(w_down, layer, counts)
    cnt = counts[:N_BUCKETS, 0].astype(jnp.int32)
    padded = ((cnt + TMO - 1) // TMO) * TMO
    ends = jnp.cumsum(padded)
    offs = ends - padded
    bucket, rank = meta[0], meta[1]
    pos = rank + jnp.sum(jnp.where(bucket[None, :] == jnp.arange(N_BUCKETS, dtype=jnp.int32)[:, None],
                                   offs[:, None], 0), axis=0)
    tile_start = jnp.arange(n_tiles, dtype=jnp.int32) * TMO
    tile_bucket = jnp.minimum(jnp.sum((tile_start[:, None] >= ends[None, :]).astype(jnp.int32), axis=1), N_BUCKETS - 1)
    pair_lo = np.array([0, 0, 0, 1, 1, 2], np.int32)
    pair_hi = np.array([1, 2, 3, 2, 3, 3], np.int32)
    b_lo = jnp.asarray(np.repeat(np.arange(N_GROUPS), N_PAIRS) * EXP_PER_GROUP + np.tile(pair_lo, N_GROUPS), jnp.int32)
    b_hi = jnp.asarray(np.repeat(np.arange(N_GROUPS), N_PAIRS) * EXP_PER_GROUP + np.tile(pair_hi, N_GROUPS), jnp.int32)
    onehot_tb = (tile_bucket[:, None] == jnp.arange(N_BUCKETS, dtype=jnp.int32)[None, :]).astype(jnp.int32)
    elo = jnp.sum(onehot_tb * b_lo[None, :], axis=1)
    ehi = jnp.sum(onehot_tb * b_hi[None, :], axis=1)
    bucket_end = jnp.sum(onehot_tb * (offs + cnt)[None, :], axis=1)
    nvalid = jnp.where(tile_start < ends[-1], jnp.clip(bucket_end - tile_start, 0, TMO), 0)
    dump = sort_rows + jnp.arange(n_pad - n, dtype=jnp.int32)
    pos_sc = jnp.concatenate([pos, dump])
    total = sort_rows + n_pad - n
    sc_idx = (pos_sc[None, :] + (jnp.arange(DISP_SLABS, dtype=jnp.int32) * total)[:, None]).reshape(-1)
    xs = _sc_scatter_rows(disp.reshape(DISP_SLABS * n_pad, 128), sc_idx, DISP_SLABS * total)
    ys, tok = _moe_call(xs.reshape(DISP_SLABS, total, 128), elo, ehi, nvalid, wgu, wdn, n_tiles,
                        n, (n_pad - n) // TMO)
    back_idx = (tok.reshape(1, sort_rows) + (jnp.arange(Y_SLABS, dtype=jnp.int32) * n_pad)[:, None]).reshape(-1)
    z = _sc_scatter_rows(ys.reshape(Y_SLABS * sort_rows, 128), back_idx, Y_SLABS * n_pad)
    return z.reshape(Y_SLABS, n_pad, 128), tok


def _final_kernel(xn_hbm, z_hbm, gate_ref, g_ref, yp_ref, ys_ref, xbuf, xsem, zbuf, zsem, *, n_prompt_tiles):
    x = _add_moe(_ring_block([xn_hbm], xbuf, xsem), _ring_window(z_hbm, zbuf, zsem, axis=1), gate_ref)
    ms = jnp.mean(x * x, axis=-1, keepdims=True)
    y = x * lax.rsqrt(ms + EPS) * g_ref[...]
    i = pl.program_id(0)

    @pl.when(i < n_prompt_tiles)
    def _():
        yp_ref[...] = y

    @pl.when(i >= n_prompt_tiles)
    def _():
        ys_ref[...] = y


def _final_call(xn, z, mods, g, n_prompt):
    n, d = xn.shape
    npt = n_prompt // TM
    assert n - n_prompt == TM
    return pl.pallas_call(
        functools.partial(_final_kernel, n_prompt_tiles=npt),
        grid=(n // TM,),
        in_specs=[pl.BlockSpec(memory_space=pl.ANY), pl.BlockSpec(memory_space=pl.ANY),
                  _mod_spec(GATE_FFN), pl.BlockSpec((1, d), lambda i: (0, 0))],
        out_specs=_token_specs(npt, d),
        out_shape=[jax.ShapeDtypeStruct((n_prompt, d), F32), jax.ShapeDtypeStruct((TM, d), F32)],
        scratch_shapes=_ring_scratch(d) + [pltpu.VMEM((RING, z.shape[0], TM, 128), z.dtype),
                                           pltpu.SemaphoreType.DMA((RING,))],
        compiler_params=_cparams(("arbitrary",)),
        name="final_norm",
    )(xn, z, mods, g)


def kernel(x_prompt, x_sample, c_prompt, c_sample, state_gla, cache_band_k, cache_band_v, cache_swa_k, cache_swa_v,
           w_ada, b_ada, norm_mix, norm_ffn, norm_final, w_in_even, w_gate_a, b_gate_a, gla_norm, rel_bias_b,
           w_out_even, w_in_odd, sinks_c, w_out_odd, w_router, b_router, w_gate_up, w_down):
    bp, lp, d = x_prompt.shape
    bs, ls_, _ = x_sample.shape
    n_p, n_s = bp * lp, bs * ls_
    n = n_p + n_s
    assert ls_ == CHUNK and n_s == TM and lp % TM == 0 and PAST_LEN % CHUNK == 0

    xp2, xs2 = x_prompt.reshape(n_p, d), x_sample.reshape(n_s, d)

    c16 = jnp.zeros((SEQ_ROWS, d), F32).at[:bp].set(c_prompt).at[bp:bp + bs].set(c_sample)
    mods = _ada_call(c16, w_ada, b_ada)
    seq_of_group = np.concatenate([np.repeat(np.arange(bp), lp // CHUNK), bp + np.arange(bs)])
    mods_g = [mods[l][seq_of_group] for l in range(DEPTH)]

    perm = np.array([4 * (c % 4) + c // 4 for c in range(N_EXPERTS)])
    wr = jnp.zeros((d, 128), F32).at[:, :N_EXPERTS].set(w_router[:, perm])
    br = jnp.zeros((1, 128), F32).at[0, :N_EXPERTS].set(b_router[perm])

    sc_unit = SC_WINDOW * SC_WORKERS * SC_GROUP
    n_pad = n + TMO
    while (DISP_SLABS * n_pad) % sc_unit or (Y_SLABS * n_pad) % TMO or (n_pad - n) % TMO:
        n_pad += TMO
    sort_rows = n + N_BUCKETS * TMO
    while (Y_SLABS * sort_rows) % sc_unit or sort_rows % (MOE_TILES * TMO):
        sort_rows += TMO

    gla_p = gla_s = bk_p = bv_p = bk_s = bv_s = sk_p = sv_p = sk_s = sv_s = None
    xn = z = tok = None
    for l in range(DEPTH):
        i = l // 2
        if l % 2 == 0:
            w = w_in_even[i]
            w_main = jnp.concatenate([w[:, :1536], w[:, 1552:]], axis=1).astype(BF16)
            w_la = jnp.zeros((d, 128), F32).at[:, :GATE_RANK].set(w[:, 1536:1552]).astype(BF16)
            w_gate = jnp.zeros((128, HA * DKA), F32).at[:GATE_RANK].set(w_gate_a[i])
            qa, ka, va, ra, qb, kb, vb, ga = _inproj_even_call(
                xp2, xs2, mods_g[l], norm_mix[l][None], w_main, w_la, w_gate, b_gate_a[i][None])
            xres = [xp2, xs2]
            gn = gla_norm[i][None]
            oa, s_p = _gla_call(qa, ka, va, ga, ra, jnp.zeros((bp, 256, 128), F32), gn, None,
                                n_seq=bp, seq_rows=lp, row0=0, nb=8)
            oa, s_s = _gla_call(qa, ka, va, ga, ra, state_gla[i].reshape(bs, 256, 128), gn, oa,
                                n_seq=bs, seq_rows=ls_, row0=n_p, nb=1)
            gla_p, gla_s = s_p.reshape(1, bp, HA, DKA, DVA), s_s.reshape(1, bs, HA, DKA, DVA)
            pb = N_PREV_B * CHUNK
            tq, g = 512, 2
            ck = cache_band_k[i].reshape(bs * pb, HB * DHB).astype(BF16)
            cv = cache_band_v[i].reshape(bs * pb, HB * DHB).astype(BF16)
            biases = (_band_bias(rel_bias_b[i], g, pb, _band_valid(g, pb)),
                      _band_bias(rel_bias_b[i], g, pb, _band_valid(g, pb, tq // (CHUNK * g))),
                      _band_bias(rel_bias_b[i], 1, pb, _band_valid(1, pb)))
            ob = _attention(_band_kernel, qb, kb, vb, ck, cv, biases, [], [], width=512, kv_width=512, pb=pb,
                            tq=tq, g=g, bp=bp, lp=lp, bs=bs, name="band")
            tail = lambda a: jnp.stack([a[(b + 1) * lp - pb:(b + 1) * lp] for b in range(bp)]).astype(F32).reshape(1, bp, pb, HB, DHB)
            new = lambda a: a[n_p:].astype(F32).reshape(bs, ls_, HB, DHB)
            bk_p, bv_p = tail(kb), tail(vb)
            bk_s = jnp.concatenate([cache_band_k[i][:, ls_:], new(kb)], axis=1)[None]
            bv_s = jnp.concatenate([cache_band_v[i][:, ls_:], new(vb)], axis=1)[None]
            wo = w_out_even[i].astype(BF16)
            os_, ws = [oa, ob], [wo[:HA * DVA], wo[HA * DVA:]]
        else:
            w = _after(w_in_odd[i], tok)
            w_out_l = _after(w_out_odd[i], tok)
            cache_k_l, cache_v_l = _after(cache_swa_k[i], tok), _after(cache_swa_v[i], tok)
            wk, wv = w[:, 1024:1152], w[:, 1152:1280]
            dup = lambda a: jnp.concatenate([a[:, :64], a[:, :64], a[:, 64:], a[:, 64:]], axis=1)
            w_all = jnp.concatenate([w[:, :1024], dup(wk), dup(wv)], axis=1).astype(BF16)
            cos, sin, rope_map = _rope_tables(lp, ls_, bp, bs)
            x, q, k, v = _inproj_odd_call(xn, z, mods_g[l - 1], mods_g[l], norm_mix[l][None], cos, sin, rope_map, w_all)
            xres = [x]
            pb = WINDOW
            tq, g = 512, 2
            sink = sinks_c[i][None] * LOG2E
            sink_spec = [pl.BlockSpec(memory_space=pltpu.SMEM)]
            dupc = lambda c: jnp.concatenate([c[:, :, 0], c[:, :, 0], c[:, :, 1], c[:, :, 1]], axis=-1).reshape(bs * pb, 256).astype(BF16)
            ck, cv = dupc(cache_k_l), dupc(cache_v_l)
            additive = lambda valid: jnp.asarray(np.where(valid, 0.0, -np.inf), F32)
            masks = (additive(_band_valid(g, pb)), additive(_band_valid(g, pb, tq // (CHUNK * g))),
                     additive(_band_valid(1, pb)))
            o = _attention(_swa_kernel, q, k, v, ck, cv, masks, [sink], sink_spec, width=1024, kv_width=256, pb=pb,
                           tq=tq, g=g, bp=bp, lp=lp, bs=bs, name="swa")
            undup = lambda a: jnp.concatenate([a[:, 0:64], a[:, 128:192]], axis=1).astype(F32)
            tail = lambda a: jnp.stack([undup(a[(b + 1) * lp - pb:(b + 1) * lp]) for b in range(bp)]).reshape(1, bp, pb, KVC, DHC)
            new = lambda a: undup(a[n_p:]).reshape(bs, ls_, KVC, DHC)
            sk_p, sv_p = tail(k), tail(v)
            sk_s = jnp.concatenate([cache_swa_k[i][:, ls_:], new(k)], axis=1)[None]
            sv_s = jnp.concatenate([cache_swa_v[i][:, ls_:], new(v)], axis=1)[None]
            os_, ws = [o], [w_out_l.astype(BF16)]
        xn, disp, meta, counts = _outproj_call(xres, os_, ws, mods_g[l], norm_ffn[l][None], wr, br, n_pad)
        z, tok = _moe_layer(disp, meta, counts, w_gate_up, w_down, l, n, n_pad, sort_rows)

    y_prompt, y_sample = _final_call(xn, z, mods_g[DEPTH - 1], norm_final[None], n_p)
    return (y_prompt.reshape(bp, lp, d), y_sample.reshape(bs, ls_, d),
            gla_p, gla_s, bk_p, bv_p, bk_s, bv_s, sk_p, sv_p, sk_s, sv_s)
```

```python
import functools

import numpy as np
import jax
import jax.numpy as jnp
from jax import lax
from jax.experimental import pallas as pl
from jax.experimental.pallas import tpu as pltpu
from jax.experimental.pallas import tpu_sc as plsc

F32 = jnp.float32
BF16 = jnp.bfloat16

D_MODEL = 1024
DEPTH = 2
CHUNK = 64
PAST_LEN = 4096
HA, DKA, DVA = 4, 64, 128
GATE_RANK = 16
GATE_TAU = 16.0
HB, DHB = 8, 64
N_PREV_B = 8
MAX_REL = 128
HC, KVC, DHC = 16, 2, 64
WINDOW = 128
ROPE_THETA = 10000.0
N_EXPERTS = 16
N_GROUPS = 4
EXP_PER_GROUP = 4
D_FF = 512
EPS = 1e-6

N_PAIRS = 6
N_BUCKETS = N_GROUPS * N_PAIRS
BUCKET_ROWS = 32
Y_SLABS = 4
DISP_SLABS = Y_SLABS + 1
TMO = 256
MOE_TILES = 2
SC_WINDOW = 128
SC_WORKERS = 32
SC_GROUP = 3

TM = 512
SEQ_ROWS = 16
SUB = 16
LOG2E = 1.4426950408889634
VMEM_LIMIT_MB = 48
VMEM_LIMIT_MOE_MB = 56


def _cparams(sem, vmem_mb=VMEM_LIMIT_MB):
    return pltpu.CompilerParams(dimension_semantics=sem, vmem_limit_bytes=vmem_mb * 1024 * 1024)


def _dot(a, b):
    return jnp.dot(a, b, preferred_element_type=F32)


def _dot_nt(a, b):
    return lax.dot_general(a, b, (((1,), (1,)), ((), ())), preferred_element_type=F32)


def _split(a):
    hi = a.astype(BF16)
    lo = (a - hi.astype(F32)).astype(BF16)
    return hi, lo


def _dot3(a, b):
    ah, al = _split(a)
    bh, bl = _split(b)
    return _dot(ah, bh) + _dot(ah, bl) + _dot(al, bh)


def _dot3_narrow(a, b):
    ah, al = _split(a)
    bh, bl = _split(b)
    n = b.shape[1]
    p = _dot(ah, jnp.concatenate([bh, bl], axis=1))
    return p[:, :n] + p[:, n:] + _dot(al, bh)


def _sigmoid(x):
    return 1.0 / (1.0 + jnp.exp(-x))


def _group_affine(y, mul, add):
    parts = []
    for gi in range(y.shape[0] // CHUNK):
        p = y[gi * CHUNK:(gi + 1) * CHUNK]
        if mul is not None:
            p = p * mul[gi:gi + 1]
        if add is not None:
            p = p + add[gi:gi + 1]
        parts.append(p)
    return jnp.concatenate(parts, axis=0)


def _norm_mod(x, g, shift, scale):
    ms = jnp.mean(x * x, axis=-1, keepdims=True)
    return _group_affine(x * lax.rsqrt(ms + EPS) * g, 1.0 + scale, shift)


def _mod_spec(part):
    return pl.BlockSpec((TM // CHUNK, D_MODEL), lambda i: (i, part))


SHIFT_MIX, SCALE_MIX, GATE_MIX, SHIFT_FFN, SCALE_FFN, GATE_FFN = range(6)


def _on_token_tile(xp_ref, xs_ref, n_prompt_tiles, body):
    @pl.when(pl.program_id(0) < n_prompt_tiles)
    def _():
        body(xp_ref)

    @pl.when(pl.program_id(0) >= n_prompt_tiles)
    def _():
        body(xs_ref)


RING = 3


def _ring_block(srcs, buf, sem, n_prompt_tiles=None):
    t = buf.shape[1]

    def copy(src, blk, slot):
        return pltpu.make_async_copy(src.at[pl.ds(pl.multiple_of(blk * t, t), t)], buf.at[slot], sem.at[slot])

    def start(step, slot):
        if len(srcs) == 1:
            copy(srcs[0], step, slot).start()
        elif isinstance(step, int):
            assert step < n_prompt_tiles
            copy(srcs[0], step, slot).start()
        else:
            @pl.when(step < n_prompt_tiles)
            def _():
                copy(srcs[0], step, slot).start()

            @pl.when(step >= n_prompt_tiles)
            def _():
                copy(srcs[1], step - n_prompt_tiles, slot).start()

    return _ring(buf, start, lambda slot: copy(srcs[0], 0, slot).wait())


def _ring(buf, start, wait):
    s = pl.program_id(0)

    @pl.when(s == 0)
    def _():
        for k in range(RING - 1):
            start(k, k)

    ahead = s + (RING - 1)

    @pl.when(ahead < pl.num_programs(0))
    def _():
        start(ahead, ahead % RING)

    slot = s % RING
    wait(slot)
    return buf.at[slot]


def _ring_window(src, buf, sem, axis):
    def copy(step, slot):
        idx = (slice(None),) * axis + (pl.ds(pl.multiple_of(step * TM, TM), TM),)
        return pltpu.make_async_copy(src.at[idx], buf.at[slot], sem.at[slot])

    return _ring(buf, lambda step, slot: copy(step, slot).start(), lambda slot: copy(0, slot).wait())


def _ring_scratch(d):
    return [pltpu.VMEM((RING, TM, d), F32), pltpu.SemaphoreType.DMA((RING,))]


def _token_specs(n_prompt_tiles, d):
    return [pl.BlockSpec((TM, d), lambda i: (jnp.minimum(i, n_prompt_tiles - 1), 0)),
            pl.BlockSpec((TM, d), lambda i: (0, 0))]


def _ada_kernel(c_ref, w_ref, b_ref, o_ref):
    c = c_ref[...]
    o_ref[0] = _dot3(c * _sigmoid(c), w_ref[0]) + b_ref[0]


def _ada_call(c16, w_ada, b_ada):
    d = D_MODEL
    tn = 1024
    return pl.pallas_call(
        _ada_kernel,
        grid=(DEPTH, 6 * d // tn),
        in_specs=[pl.BlockSpec((SEQ_ROWS, d), lambda l, j: (0, 0)),
                  pl.BlockSpec((1, d, tn), lambda l, j: (l, 0, j)),
                  pl.BlockSpec((1, 1, tn), lambda l, j: (l, 0, j))],
        out_specs=pl.BlockSpec((1, SEQ_ROWS, tn), lambda l, j: (l, 0, j)),
        out_shape=jax.ShapeDtypeStruct((DEPTH, SEQ_ROWS, 6 * d), F32),
        compiler_params=_cparams(("arbitrary", "arbitrary")),
        name="ada",
    )(c16, w_ada, b_ada.reshape(DEPTH, 1, 6 * d))


def _inproj_even_kernel(xp_ref, xs_ref, sh_ref, sc_ref, g_ref, w_ref, wla_ref, wg_ref, bg_ref,
                        qa_ref, ka_ref, va_ref, ra_ref, qb_ref, kb_ref, vb_ref, ga_ref, *, n_prompt_tiles):
    def body(x_ref):
        t = x_ref.shape[0]
        outs = ((qa_ref, 0, 256, DKA ** -0.5), (ka_ref, 256, 512, None), (va_ref, 512, 1024, None),
                (ra_ref, 1024, 1536, None), (qb_ref, 1536, 2048, DHB ** -0.5 * LOG2E), (kb_ref, 2048, 2560, None),
                (vb_ref, 2560, 3072, None))
        shift, scale_ = sh_ref[...], sc_ref[...]
        halves = [slice(0, t // 2), slice(t // 2, t)]
        grp = [slice(0, t // (2 * CHUNK)), slice(t // (2 * CHUNK), t // CHUNK)]
        hbs = [_norm_mod(x_ref[rs, :], g_ref[...], shift[gs], scale_[gs]).astype(BF16) for rs, gs in zip(halves, grp)]
        for rs, hb in zip(halves, hbs):
            zs = [_dot(hb, w_ref[:, lo:hi]) for _, lo, hi, _ in outs]
            la = _dot(hb, wla_ref[...])
            for z, (o_ref, _, _, scale) in zip(zs, outs):
                o_ref[rs, :] = (z if scale is None else z * scale).astype(BF16)
            gl = _dot3(la, wg_ref[...]) + bg_ref[...]
            ga_ref[rs, :] = -(jnp.maximum(-gl, 0.0) + jnp.log(1.0 + jnp.exp(-jnp.abs(gl)))) * (1.0 / GATE_TAU)

    _on_token_tile(xp_ref, xs_ref, n_prompt_tiles, body)


def _inproj_even_call(xp, xs, mods, g, w_main, w_la, w_gate, b_gate):
    d = xp.shape[1]
    npt = xp.shape[0] // TM
    n = xp.shape[0] + xs.shape[0]
    row = lambda i: (i, 0)
    const = lambda i: (0, 0)
    widths = (256, 256, 512, 512, 512, 512, 512)
    out_shape = [jax.ShapeDtypeStruct((n, w), BF16) for w in widths] + [jax.ShapeDtypeStruct((n, 256), F32)]
    out_specs = [pl.BlockSpec((TM, w), row) for w in widths] + [pl.BlockSpec((TM, 256), row)]
    return pl.pallas_call(
        functools.partial(_inproj_even_kernel, n_prompt_tiles=npt),
        grid=(n // TM,),
        in_specs=_token_specs(npt, d) + [
            _mod_spec(SHIFT_MIX), _mod_spec(SCALE_MIX),
            pl.BlockSpec((1, d), const),
            pl.BlockSpec(w_main.shape, const), pl.BlockSpec(w_la.shape, const),
            pl.BlockSpec(w_gate.shape, const), pl.BlockSpec(b_gate.shape, const)],
        out_specs=out_specs, out_shape=out_shape,
        compiler_params=_cparams(("parallel",)),
        name="inproj_even",
    )(xp, xs, mods, mods, g, w_main, w_la, w_gate, b_gate)


def _rope(x, cos, sin_signed):
    t, w = x.shape
    lane = lax.broadcasted_iota(jnp.int32, (1, w), 1)
    first_half = (lane & 63) < 32
    rot = jnp.where(first_half, pltpu.roll(x, w - 32, 1), pltpu.roll(x, 32, 1))
    reps = w // 128
    return x * jnp.tile(cos, (1, reps)) + rot * jnp.tile(sin_signed, (1, reps))


def _unpack_pairs(slabs, dtype):
    lo = [pltpu.bitcast(s << 16, F32) for s in slabs]
    hi = [pltpu.bitcast(s & jnp.int32(-65536), F32) for s in slabs]
    return jnp.concatenate(lo + hi, axis=1).astype(dtype)


def _pack_pairs(x):
    bits = pltpu.bitcast(x.astype(BF16).astype(F32), jnp.int32)
    half = x.shape[1] // 2
    packed = ((bits[:, :half] >> 16) & jnp.int32(0xFFFF)) | (bits[:, half:] & jnp.int32(-65536))
    return [packed[:, 128 * s:128 * (s + 1)] for s in range(half // 128)]


def _add_moe(xn_ref, z_ref, gate_ref):
    y = _unpack_pairs([z_ref[s] for s in range(z_ref.shape[0])], F32)
    return xn_ref[...] + _group_affine(y, gate_ref[...], None)


def _rope_tables(lp, ls_, bp, bs):
    assert PAST_LEN + ls_ <= lp and lp % 128 == 0 and bs * ls_ == TM
    half = DHC // 2
    inv = ROPE_THETA ** (-jnp.arange(half, dtype=F32) / half)
    inv = jnp.tile(inv, 128 // half)
    sign = jnp.asarray(np.tile(np.repeat([-1.0, 1.0], half), 128 // DHC), F32)
    a = jnp.asarray(np.arange(lp // 128) * 128, F32)[:, None] * inv[None, :]
    b = jnp.asarray(np.arange(128), F32)[:, None] * inv[None, :]
    ca, sa, cb, sb = jnp.cos(a)[:, None], jnp.sin(a)[:, None], jnp.cos(b)[None], jnp.sin(b)[None]
    cos = (ca * cb - sa * sb).reshape(lp, 128)
    sin = ((sa * cb + ca * sb) * sign).reshape(lp, 128)
    with_sample = lambda t: jnp.concatenate([t, jnp.tile(t[PAST_LEN:PAST_LEN + ls_], (bs, 1))], axis=0)
    tiles = lp // TM
    return with_sample(cos), with_sample(sin), lambda i: (jnp.where(i < bp * tiles, i % tiles, tiles), 0)


def _inproj_odd_kernel(xn_hbm, z_hbm, gate_ref, sh_ref, sc_ref, g_ref, cos_ref, sin_ref, w_ref,
                       q_ref, k_ref, v_ref, xbuf, xsem, zbuf, zsem):
    xn_ref = _ring_block([xn_hbm], xbuf, xsem)
    z_ref = _ring_window(z_hbm, zbuf, zsem, axis=1)
    t = xn_ref.shape[0]
    halves = [slice(0, t // 2), slice(t // 2, t)]
    grp = [slice(0, t // (2 * CHUNK)), slice(t // (2 * CHUNK), t // CHUNK)]
    gate, shift, scale = gate_ref[...], sh_ref[...], sc_ref[...]
    xs = []
    for rs, gs in zip(halves, grp):
        y = _unpack_pairs([z_ref[s, rs, :] for s in range(z_ref.shape[0])], F32)
        xs.append(xn_ref[rs, :] + _group_affine(y, gate[gs], None))
    hbs =[_norm_mod(x, g_ref[...], shift[gs], scale[gs]).astype(BF16) for x, gs in zip(xs, grp)]
    qs = [_dot(hb, w_ref[:, 0:1024]) for hb in hbs]
    ks = [_dot(hb, w_ref[:, 1024:1280]) for hb in hbs]
    vs = [_dot(hb, w_ref[:, 1280:1536]) for hb in hbs]
    for rs, q, k, v in zip(halves, qs, ks, vs):
        cos, sin = cos_ref[rs, :], sin_ref[rs, :]
        q_ref[rs, :] = (_rope(q, cos, sin) * (DHC ** -0.5 * LOG2E)).astype(BF16)
        k_ref[rs, :] = _rope(k, cos, sin).astype(BF16)
        v_ref[rs, :] = v.astype(BF16)


def _inproj_odd_call(xn, z, mods_prev, mods, g, cos, sin, rope_map, w):
    n, d = xn.shape
    row = lambda i: (i, 0)
    const = lambda i: (0, 0)
    widths = (1024, 256, 256)
    return pl.pallas_call(
        _inproj_odd_kernel,
        grid=(n // TM,),
        in_specs=[pl.BlockSpec(memory_space=pl.ANY), pl.BlockSpec(memory_space=pl.ANY),
                  _mod_spec(GATE_FFN), _mod_spec(SHIFT_MIX), _mod_spec(SCALE_MIX),
                  pl.BlockSpec((1, d), const),
                  pl.BlockSpec((TM, 128), rope_map), pl.BlockSpec((TM, 128), rope_map),
                  pl.BlockSpec(w.shape, const)],
        out_specs=[pl.BlockSpec((TM, wd), row) for wd in widths],
        out_shape=[jax.ShapeDtypeStruct((n, wd), BF16) for wd in widths],
        scratch_shapes=_ring_scratch(d) + [pltpu.VMEM((RING, z.shape[0], TM, 128), z.dtype),
                                           pltpu.SemaphoreType.DMA((RING,))],
        compiler_params=_cparams(("arbitrary",)),
        name="inproj_odd",
    )(xn, z, mods_prev, mods, mods, g, cos, sin, w)


def _gla_tri():
    t = np.arange(CHUNK)[:, None]
    s = np.arange(CHUNK)[None, :]
    cum = s <= t
    start = s < (t // SUB) * SUB
    end = s < (t // SUB + 1) * SUB
    return jnp.asarray(np.concatenate([cum, start, end], axis=0).astype(np.float32), dtype=BF16)


def _gla_kernel(q_ref, k_ref, v_ref, g_ref, r_ref, s0_ref, gn_ref, tri_ref, o_ref, sout_ref, s_ref, *, nb):
    c_ = CHUNK
    nsub = c_ // SUB

    @pl.when(pl.program_id(1) == 0)
    def _():
        s_ref[...] = s0_ref[0]

    tri = tri_ref[...]
    lane = lax.broadcasted_iota(jnp.int32, (1, 128), 1)
    hmask = [jnp.where(lane < DKA, 1.0, 0.0), jnp.where(lane >= DKA, 1.0, 0.0)]
    ti = lax.broadcasted_iota(jnp.int32, (c_, c_), 0)
    si = lax.broadcasted_iota(jnp.int32, (c_, c_), 1)
    rb, cb = ti >> 4, si >> 4
    m_diag = (rb == cb) & (si <= ti)
    m_off = [(cb == j) & (rb > j) for j in range(nsub - 1)]
    hk = HA * DKA
    gn = gn_ref[...]

    chunks = range(nb)
    heads = [(p, hh) for p in range(HA // 2) for hh in range(2)]
    rows = [slice(c * c_, (c + 1) * c_) for c in chunks]
    pair = [slice(128 * p, 128 * (p + 1)) for p in range(HA // 2)]
    css = []
    for c in chunks:
        g_hi, g_lo = _split(g_ref[rows[c], :])
        css.append(_dot(tri, g_hi) + _dot(tri, g_lo))
    lhs1, lhs2, kds, kes, q_inter, klts, dcols = [], [], [], [], [], [], []
    for c in chunks:
        b, rs, re = css[c][0:c_], css[c][c_:2 * c_], css[c][2 * c_:3 * c_]
        q = q_ref[rows[c], :].astype(F32)
        k = k_ref[rows[c], :].astype(F32)
        bl = b[c_ - 1:c_, :]
        qd = q * jnp.exp(b - rs)
        kd = k * jnp.exp(rs - b)
        ke = k * jnp.exp(re - b)
        qi = q * jnp.exp(b)
        kl = k * jnp.exp(bl - b)
        ql = [q * jnp.exp(jnp.minimum(b - b[SUB * (j + 1) - 1:SUB * (j + 1), :], 0.0)) for j in range(nsub - 1)]
        dcols.append(jnp.broadcast_to(jnp.exp(bl), (8, hk)).T[:, 0:1])
        kds.append([(kd[:, pair[p]] * hmask[hh]).astype(BF16) for p, hh in heads])
        kes.append([(ke[:, pair[p]] * hmask[hh]).astype(BF16) for p, hh in heads])
        klts.append([kl[:, ls].T.astype(BF16) for ls in pair])
        lhs1.append([qd[:, ls].astype(BF16) for ls in pair])
        lhs2.append([jnp.concatenate([ql[j][:, ls] for j in range(nsub - 1)], axis=0).astype(BF16) for ls in pair])
        q_inter.append([(qi[:, pair[p]] * hmask[hh]).astype(BF16) for p, hh in heads])
    a1s = [[_dot_nt(lhs1[c][p], kds[c][h]) for h, (p, hh) in enumerate(heads)] for c in chunks]
    a2s = [[_dot_nt(lhs2[c][p], kes[c][h]) for h, (p, hh) in enumerate(heads)] for c in chunks]
    atts = []
    for c in chunks:
        per_head = []
        for h in range(HA):
            att = jnp.zeros((c_, c_), F32)
            for j in reversed(range(nsub - 1)):
                att = jnp.where(m_off[j], a2s[c][h][j * c_:(j + 1) * c_], att)
            per_head.append(jnp.where(m_diag, a1s[c][h], att).astype(BF16))
        atts.append(per_head)
    vs_ = [[v_ref[rows[c], DVA * h:DVA * (h + 1)] for h in range(HA)] for c in chunks]
    o_intra = [[_dot(atts[c][h], vs_[c][h]) for h in range(HA)] for c in chunks]
    upds = [jnp.concatenate([_dot(klts[c][p][DKA * hh:DKA * (hh + 1)], vs_[c][2 * p + hh]) for p, hh in heads], axis=0)
            for c in chunks]

    s_cur = s_ref[...]
    s_in = []
    for c in chunks:
        s_in.append(s_cur.astype(BF16))
        s_cur = dcols[c] * s_cur + upds[c]
    s_ref[...] = s_cur
    sout_ref[0] = s_cur

    for c in chunks:
        for h in range(HA):
            o = o_intra[c][h] + _dot(q_inter[c][h], s_in[c][pair[h // 2], :])
            ms = jnp.mean(o * o, axis=-1, keepdims=True)
            vs = slice(DVA * h, DVA * (h + 1))
            rr = r_ref[rows[c], vs].astype(F32)
            o_ref[rows[c], vs] = (o * lax.rsqrt(ms + EPS) * gn * (rr * _sigmoid(rr))).astype(BF16)


def _gla_call(q, k, v, g, r, s0, gn, o_prev, *, n_seq, seq_rows, row0, nb):
    tq = nb * CHUNK
    steps = seq_rows // tq
    blk0 = row0 // tq
    row = lambda b, j: (blk0 + b * steps + j, 0)
    const = lambda b, j: (0, 0)
    tri = _gla_tri()
    in_specs = [pl.BlockSpec((tq, 256), row), pl.BlockSpec((tq, 256), row), pl.BlockSpec((tq, 512), row),
                pl.BlockSpec((tq, 256), row), pl.BlockSpec((tq, 512), row),
                pl.BlockSpec((1, 256, 128), lambda b, j: (b, 0, 0)),
                pl.BlockSpec((1, 128), const), pl.BlockSpec(tri.shape, const)]
    args = [q, k, v, g, r, s0, gn, tri]
    aliases = {}
    if o_prev is not None:
        in_specs.append(pl.BlockSpec(memory_space=pl.ANY))
        args.append(o_prev)
        aliases = {len(args) - 1: 0}
    kern = functools.partial(_gla_kernel, nb=nb)
    if o_prev is not None:
        kern = _drop_arg(kern, 8)
    return pl.pallas_call(
        kern,
        grid=(n_seq, steps),
        in_specs=in_specs,
        out_specs=[pl.BlockSpec((tq, 512), row), pl.BlockSpec((1, 256, 128), lambda b, j: (b, 0, 0))],
        out_shape=[jax.ShapeDtypeStruct((q.shape[0], 512), BF16), jax.ShapeDtypeStruct((n_seq, 256, 128), F32)],
        scratch_shapes=[pltpu.VMEM((256, 128), F32)],
        input_output_aliases=aliases,
        compiler_params=_cparams(("arbitrary", "arbitrary")),
        name="gla",
    )(*args)


def _drop_arg(fn, idx):
    def wrapped(*refs):
        return fn(*refs[:idx], *refs[idx + 1:])
    return wrapped


def _window(prev_ref, cur_ref, lo, hi, pb, ls):
    if lo < pb:
        return jnp.concatenate([prev_ref[lo:pb, ls], cur_ref[0:hi - pb, ls]], axis=0)
    return cur_ref[lo - pb:hi - pb, ls]


def _band_kernel(q_ref, kp_ref, kc_ref, vp_ref, vc_ref, bias_ref, o_ref, *, g, n_sub, pb):
    qs = CHUNK * g
    kw_rows = pb + qs
    lane = lax.broadcasted_iota(jnp.int32, (1, 128), 1)
    low = lane < DHB
    hmask = [jnp.where(low, 1.0, 0.0), jnp.where(low, 0.0, 1.0)]
    for s in range(n_sub):
        sb = s if bias_ref.shape[0] > 1 else 0
        rows = slice(qs * s, qs * (s + 1))
        lanes = [slice(128 * p, 128 * (p + 1)) for p in range(HB // 2)]
        heads = [(p, hh) for p in range(HB // 2) for hh in range(2)]
        qps = [q_ref[rows, ls].astype(F32) for ls in lanes]
        kws = [_window(kp_ref, kc_ref, qs * s, qs * s + kw_rows, pb, ls) for ls in lanes]
        vws = [_window(vp_ref, vc_ref, qs * s, qs * s + kw_rows, pb, ls) for ls in lanes]
        qq = [jnp.concatenate([(qps[p] * hmask[hh]).astype(BF16) for hh in range(2)], axis=0) for p in range(HB // 2)]
        sc2 = [_dot_nt(qq[p], kws[p]) for p in range(HB // 2)]
        scs = [sc2[p][qs * hh:qs * (hh + 1)] + bias_ref[sb, 2 * p + hh] for p, hh in heads]
        pes = [jnp.exp2(sc - jnp.max(sc, axis=-1, keepdims=True)) for sc in scs]
        pp = [jnp.concatenate([pes[2 * p + hh].astype(BF16) for hh in range(2)], axis=0) for p in range(HB // 2)]
        o2 = [_dot(pp[p], vws[p]) for p in range(HB // 2)]
        outs = [o2[p][qs * hh:qs * (hh + 1)] / jnp.sum(pes[2 * p + hh], axis=-1, keepdims=True) for p, hh in heads]
        for p, ls in enumerate(lanes):
            o_ref[rows, ls] = jnp.where(low, outs[2 * p], outs[2 * p + 1]).astype(BF16)


def _band_valid(g, pb, n_sub=None):
    rows, kw = CHUNK * g, pb + CHUNK * g
    r = np.arange(rows)[:, None]
    c = np.arange(kw)[None, :]
    dd = c // CHUNK - r // CHUNK
    band = (dd >= 0) & (dd <= pb // CHUNK)
    if n_sub is None:
        return band[None]
    return np.stack([band & (c >= pb - rows * s) for s in range(n_sub)])


def _band_bias(table, g, pb, valid):
    rows, kw = CHUNK * g, pb + CHUNK * g
    period = kw + rows
    m = np.arange(period)
    m = np.where(m < kw, m, m - period)
    ext = table[:, np.clip(m - pb, -MAX_REL, MAX_REL) + MAX_REL] * LOG2E
    flat = jnp.tile(ext, (1, rows))[:, :rows * (period - 1)]
    bias = flat.reshape(table.shape[0], rows, period - 1)[:, :, :kw]
    return jnp.where(valid[:, None], bias[None], -jnp.inf)


def _attn_call(kernel, q, kp, kc, vp, vc, extra, extra_specs, o_prev, *, width, kv_width, tq, pb,
               n_blocks, blk_map, prev_map, name):
    row = lambda i: (blk_map(i), 0)
    prev = lambda i: (prev_map(i), 0)
    in_specs = [pl.BlockSpec((tq, width), row),
                pl.BlockSpec((pb, kv_width), prev), pl.BlockSpec((tq, kv_width), row),
                pl.BlockSpec((pb, kv_width), prev), pl.BlockSpec((tq, kv_width), row)] + extra_specs
    args = [q, kp, kc, vp, vc] + extra
    aliases = {}
    if o_prev is not None:
        in_specs.append(pl.BlockSpec(memory_space=pl.ANY))
        args.append(o_prev)
        aliases = {len(args) - 1: 0}
        kernel = _drop_arg(kernel, len(args) - 1)
    return pl.pallas_call(
        kernel,
        grid=(n_blocks,),
        in_specs=in_specs,
        out_specs=pl.BlockSpec((tq, width), row),
        out_shape=jax.ShapeDtypeStruct((q.shape[0], width), BF16),
        input_output_aliases=aliases,
        compiler_params=_cparams(("parallel",)),
        name=name,
    )(*args)


def _attention(kernel_fn, q, k, v, cache_k, cache_v, masks, extra, extra_specs, *, width, kv_width, pb, tq, g,
               bp, lp, bs, name):
    bps = lp // tq
    n_sub = tq // (CHUNK * g)
    spec = lambda a: [pl.BlockSpec(a.shape, lambda i: (0,) * a.ndim)]
    kern = functools.partial(kernel_fn, g=g, n_sub=n_sub, pb=pb)
    common = dict(width=width, kv_width=kv_width, pb=pb)
    main = lambda i: (i // (bps - 1)) * bps + i % (bps - 1) + 1
    o = _attn_call(kern, q, k, k, v, v, [masks[0]] + extra, spec(masks[0]) + extra_specs, None, tq=tq,
                   n_blocks=bp * (bps - 1), blk_map=main, prev_map=lambda i: main(i) * (tq // pb) - 1,
                   name=name + "_main", **common)
    first = lambda i: i * bps
    o = _attn_call(kern, q, k, k, v, v, [masks[1]] + extra, spec(masks[1]) + extra_specs, o, tq=tq,
                   n_blocks=bp, blk_map=first, prev_map=lambda i: jnp.maximum(first(i) * (tq // pb) - 1, 0),
                   name=name + "_first", **common)
    samp = functools.partial(kernel_fn, g=1, n_sub=1, pb=pb)
    return _attn_call(samp, q, cache_k, k, cache_v, v, [masks[2]] + extra, spec(masks[2]) + extra_specs, o, tq=CHUNK,
                      n_blocks=bs, blk_map=lambda i: bp * lp // CHUNK + i, prev_map=lambda i: i,
                      name=name + "_sample", **common)


def _swa_kernel(q_ref, kp_ref, kc_ref, vp_ref, vc_ref, mask_ref, sink_ref, o_ref, *, g, n_sub, pb):
    qs = CHUNK * g
    kw_rows = pb + qs
    lane = lax.broadcasted_iota(jnp.int32, (1, 128), 1)
    low = lane < DHC
    hmask = [jnp.where(low, 1.0, 0.0), jnp.where(low, 0.0, 1.0)]
    pairs_per_kv = HC // KVC // 2
    for s in range(n_sub):
        msk = mask_ref[s if mask_ref.shape[0] > 1 else 0]
        rows = slice(qs * s, qs * (s + 1))
        kws = [_window(kp_ref, kc_ref, qs * s, qs * s + kw_rows, pb, slice(128 * kv, 128 * (kv + 1))) for kv in range(KVC)]
        vws = [_window(vp_ref, vc_ref, qs * s, qs * s + kw_rows, pb, slice(128 * kv, 128 * (kv + 1))) for kv in range(KVC)]
        heads = [(j, hh) for j in range(HC // 2) for hh in range(2)]
        qps = [q_ref[rows, 128 * j:128 * (j + 1)].astype(F32) for j in range(HC // 2)]
        per_kv = 2 * pairs_per_kv
        qq = [jnp.concatenate([(qps[j] * hmask[hh]).astype(BF16) for j, hh in heads[per_kv * kv:per_kv * (kv + 1)]],
                              axis=0) for kv in range(KVC)]
        sc2 = [_dot_nt(qq[kv], kws[kv]) for kv in range(KVC)]
        scs = [sc2[u // per_kv][qs * (u % per_kv):qs * (u % per_kv + 1)] + msk for u in range(len(heads))]
        sks = [sink_ref[0, 2 * j + hh] for j, hh in heads]
        ms = [jnp.maximum(jnp.max(sc, axis=-1, keepdims=True), sk) for sc, sk in zip(scs, sks)]
        pes = [jnp.exp2(sc - m) for sc, m in zip(scs, ms)]
        pp = [jnp.concatenate([pe.astype(BF16) for pe in pes[per_kv * kv:per_kv * (kv + 1)]], axis=0) for kv in range(KVC)]
        o2 = [_dot(pp[kv], vws[kv]) for kv in range(KVC)]
        outs = [o2[u // per_kv][qs * (u % per_kv):qs * (u % per_kv + 1)]
                / (jnp.sum(pes[u], axis=-1, keepdims=True) + jnp.exp2(sks[u] - ms[u])) for u in range(len(heads))]
        for j in range(HC // 2):
            o_ref[rows, 128 * j:128 * (j + 1)] = jnp.where(low, outs[2 * j], outs[2 * j + 1]).astype(BF16)


def _route(logits_t):
    a = [logits_t[4 * j:4 * j + 4] for j in range(EXP_PER_GROUP)]

    def first_argmax(vals, m):
        idx = jnp.full(m.shape, float(len(vals) - 1), F32)
        for j in reversed(range(len(vals) - 1)):
            idx = jnp.where(vals[j] == m, float(j), idx)
        return idx

    m1 = functools.reduce(jnp.maximum, a)
    i1 = first_argmax(a, m1)
    bsec = [jnp.where(i1 == float(j), -jnp.inf, a[j]) for j in range(EXP_PER_GROUP)]
    m2 = functools.reduce(jnp.maximum, bsec)
    i2 = first_argmax(bsec, m2)
    rows = lambda x: [x[gi:gi + 1] for gi in range(N_GROUPS)]
    gm = functools.reduce(jnp.maximum, rows(m1))
    gscore = jnp.exp(m1 - gm) + jnp.exp(m2 - gm)
    gs = rows(gscore)
    gsel = first_argmax(gs, functools.reduce(jnp.maximum, gs))

    def pick(x):
        xr = rows(x)
        out = xr[N_GROUPS - 1]
        for gi in reversed(range(N_GROUPS - 1)):
            out = jnp.where(gsel == float(gi), xr[gi], out)
        return out

    p1 = jnp.exp(pick(m1) - gm)
    p2 = jnp.exp(pick(m2) - gm)
    w1 = p1 / (p1 + p2)
    w2 = p2 / (p1 + p2)
    s1, s2 = pick(i1), pick(i2)
    lo, hi = jnp.minimum(s1, s2), jnp.maximum(s1, s2)
    pair = jnp.where(lo == 0.0, hi - 1.0, jnp.where(lo == 1.0, hi + 1.0, 5.0))
    bucket = gsel * float(N_PAIRS) + pair
    first_is_lo = s1 < s2
    return bucket, jnp.where(first_is_lo, w1, w2), jnp.where(first_is_lo, w2, w1)


def _outproj_kernel(*refs, n_x, n_o, n_prompt_tiles, moe_pending):
    n_in = n_x + (2 if moe_pending else 0)
    x_refs = refs[:n_x]
    o_refs = refs[n_in:n_in + n_o]
    w_refs = refs[n_in + n_o:n_in + 2 * n_o]
    n_tail = 2 * n_o + (2 if moe_pending else 0)
    (gate_ref, nf_ref, sh_ref, sc_ref, wr_ref, br_ref, tri_ref,
     xn_ref, disp_ref, meta_ref, cnt_ref, run_ref, xbuf, xsem) = refs[n_in + 2 * n_o:len(refs) - n_tail]
    o_scratch = refs[len(refs) - n_tail:len(refs) - n_tail + 2 * n_o]
    o_refs = [_ring_window(o_hbm, o_scratch[2 * i], o_scratch[2 * i + 1], axis=0) for i, o_hbm in enumerate(o_refs)]
    t = xn_ref.shape[0]

    @pl.when(pl.program_id(0) == 0)
    def _():
        run_ref[...] = jnp.zeros_like(run_ref)

    x_view = _ring_block(list(x_refs), xbuf, xsem, n_prompt_tiles)
    if moe_pending:
        z_view = _ring_window(refs[n_x], refs[-2], refs[-1], axis=1)
        gate_prev = refs[n_x + 1][...]

    def x_rows(rs, gs):
        if not moe_pending:
            return x_view[rs, :]
        y = _unpack_pairs([z_view[s, rs, :] for s in range(z_view.shape[0])], F32)
        return x_view[rs, :] + _group_affine(y, gate_prev[gs], None)

    halves = [slice(0, t // 2), slice(t // 2, t)]
    grp = [slice(0, t // (2 * CHUNK)), slice(t // (2 * CHUNK), t // CHUNK)]
    ys = []
    for rs in halves:
        y = _dot(o_refs[0][rs, :], w_refs[0][...])
        for i in range(1, n_o):
            y = y + _dot(o_refs[i][rs, :], w_refs[i][...])
        ys.append(y)
    gate, shift, scale = gate_ref[...], sh_ref[...], sc_ref[...]
    gys = [_group_affine(y, gate[gs], None) for y, gs in zip(ys, grp)]

    for rs, gs, gy in zip(halves, grp, gys):
        xn_ref[rs, :] = x_rows(rs, gs) + gy
    hs = [_norm_mod(xn_ref[rs, :], nf_ref[...], shift[gs], scale[gs]) for rs, gs in zip(halves, grp)]
    for rs, h in zip(halves, hs):
        for s, slab in enumerate(_pack_pairs(h)):
            disp_ref[s, rs, :] = slab
    logits_t = [(_dot3_narrow(h, wr_ref[...]) + br_ref[...]).T[0:N_EXPERTS] for h in hs]
    bucket, w_lo, w_hi = _route(jnp.concatenate(logits_t, axis=1))
    r128 = lax.broadcasted_iota(jnp.int32, (128, t), 0)
    tok = (pl.program_id(0) * t + lax.broadcasted_iota(jnp.int32, (1, t), 1)).astype(F32)
    aux = jnp.where(r128 == 0, w_lo, jnp.where(r128 == 1, w_hi, jnp.where(r128 == 2, tok, 0.0))).T
    disp_ref[disp_ref.shape[0] - 1] = pltpu.bitcast(aux, jnp.int32)
    brow = lax.broadcasted_iota(jnp.int32, (BUCKET_ROWS, t), 0).astype(F32)
    onehot = jnp.where(brow == bucket, 1.0, 0.0)
    before = _dot(onehot.astype(BF16), tri_ref[...]) + run_ref[:, 0:1]
    rank = jnp.sum(onehot * before, axis=0, keepdims=True)
    run_ref[...] = run_ref[...] + jnp.sum(onehot, axis=1, keepdims=True)
    cnt_ref[...] = run_ref[...]
    r8 = lax.broadcasted_iota(jnp.int32, (8, t), 0)
    meta_ref[...] = jnp.where(r8 == 0, bucket, jnp.where(r8 == 1, rank, 0.0)).astype(jnp.int32)


def _outproj_call(xs_, os_, ws, mods, nf, wr, br, n_pad, moe_prev=None):
    d = xs_[0].shape[1]
    n = sum(a.shape[0] for a in xs_)
    npt = xs_[0].shape[0] // TM
    row = lambda i: (i, 0)
    const = lambda i: (0, 0)
    n_o = len(os_)
    prev_specs, prev_args, prev_scratch = [], [], []
    if moe_prev is not None:
        z, mods_prev = moe_prev
        prev_specs = [pl.BlockSpec(memory_space=pl.ANY), _mod_spec(GATE_FFN)]
        prev_args = [z, mods_prev]
        prev_scratch = [pltpu.VMEM((RING, z.shape[0], TM, 128), z.dtype), pltpu.SemaphoreType.DMA((RING,))]
    in_specs = ([pl.BlockSpec(memory_space=pl.ANY) for _ in xs_]
                + prev_specs
                + [pl.BlockSpec(memory_space=pl.ANY) for _ in os_]
                + [pl.BlockSpec(w.shape, const) for w in ws]
                + [_mod_spec(GATE_MIX), pl.BlockSpec((1, d), const),
                   _mod_spec(SHIFT_FFN), _mod_spec(SCALE_FFN),
                   pl.BlockSpec(wr.shape, const), pl.BlockSpec(br.shape, const),
                   pl.BlockSpec((TM, TM), const)])
    tri = jnp.asarray(np.triu(np.ones((TM, TM), np.float32), k=1), dtype=BF16)
    return pl.pallas_call(
        functools.partial(_outproj_kernel, n_x=len(xs_), n_o=n_o, n_prompt_tiles=npt,
                          moe_pending=moe_prev is not None),
        grid=(n // TM,),
        in_specs=in_specs,
        out_specs=[pl.BlockSpec((TM, d), row), pl.BlockSpec((DISP_SLABS, TM, 128), lambda i: (0, i, 0)),
                   pl.BlockSpec((8, TM), lambda i: (0, i)), pl.BlockSpec((BUCKET_ROWS, 128), const)],
        out_shape=[jax.ShapeDtypeStruct((n, d), F32), jax.ShapeDtypeStruct((DISP_SLABS, n_pad, 128), jnp.int32),
                   jax.ShapeDtypeStruct((8, n), jnp.int32), jax.ShapeDtypeStruct((BUCKET_ROWS, 128), F32)],
        scratch_shapes=([pltpu.VMEM((BUCKET_ROWS, 128), F32)] + _ring_scratch(d)
                        + [sc for o in os_ for sc in (pltpu.VMEM((RING, TM, o.shape[1]), o.dtype),
                                                      pltpu.SemaphoreType.DMA((RING,)))]
                        + prev_scratch),
        compiler_params=_cparams(("arbitrary",)),
        name="outproj_router",
    )(*xs_, *prev_args, *os_, *ws, mods, nf, mods, mods, wr, br, tri)


def _sc_mesh():
    return plsc.VectorSubcoreMesh(core_axis_name="core", subcore_axis_name="subcore")


def _sc_scatter_rows(src, idx, n_out):
    r = idx.shape[0]
    k = SC_GROUP
    w_per = r // (SC_WINDOW * SC_WORKERS)
    assert idx.shape == (src.shape[0],) and r % (SC_WINDOW * SC_WORKERS) == 0 and w_per % k == 0
    n_groups = w_per // k

    @functools.partial(
        pl.kernel, out_type=jax.ShapeDtypeStruct((n_out, 128), src.dtype), mesh=_sc_mesh(),
        scratch_types=[pltpu.VMEM((w_per, SC_WINDOW), jnp.int32),
                       pltpu.VMEM((2 * k, SC_WINDOW, 128), src.dtype),
                       pltpu.SemaphoreType.DMA((2,)), pltpu.SemaphoreType.DMA((2,))])
    def copy(x_hbm, i_hbm, o_hbm, ibuf, xbuf, in_sem, out_sem):
        wid = lax.axis_index("core") * (SC_WORKERS // 2) + lax.axis_index("subcore")
        pltpu.sync_copy(i_hbm.at[wid], ibuf)
        first = wid * w_per

        def start_in(g, slot):
            return [pltpu.async_copy(x_hbm.at[pl.ds((first + g * k + c) * SC_WINDOW, SC_WINDOW)],
                                     xbuf.at[slot * k + c], in_sem.at[slot]) for c in range(k)]

        def start_out(g, slot):
            return [pltpu.async_copy(xbuf.at[slot * k + c], o_hbm.at[ibuf.at[g * k + c]], out_sem.at[slot])
                    for c in range(k)]

        pending_in = start_in(0, 0)
        for g in range(n_groups):
            slot = g % 2
            for cp in pending_in:
                cp.wait()
            pending_out = start_out(g, slot)
            if g + 1 < n_groups:
                pending_in = start_in(g + 1, 1 - slot)
            for cp in pending_out:
                cp.wait()

    return copy(src, idx.reshape(SC_WORKERS, w_per, SC_WINDOW))


def _moe_kernel(elo_ref, ehi_ref, nvalid_ref, xs_ref, *refs, n_tok, dump_tiles):
    w_refs, (y_ref, tok_ref) = refs[:4 * MOE_TILES], refs[4 * MOE_TILES:]
    step = pl.program_id(0)
    t = TMO
    tiles = range(MOE_TILES)
    rows = [slice(t * j, t * (j + 1)) for j in tiles]
    auxs = [pltpu.bitcast(xs_ref[Y_SLABS, rows[j], :], F32) for j in tiles]
    r = lax.broadcasted_iota(jnp.int32, (1, t), 1)
    for j in tiles:
        i = step * MOE_TILES + j
        spare = n_tok + (i % dump_tiles) * t + r
        tok = jnp.where(r < nvalid_ref[i], auxs[j].T[2:3, :].astype(jnp.int32), spare)
        for c in range(t // 128):
            tok_ref[j, c:c + 1, :] = tok[:, 128 * c:128 * (c + 1)]

    any_tokens = nvalid_ref[step * MOE_TILES] > 0
    for j in range(1, MOE_TILES):
        any_tokens = jnp.logical_or(any_tokens, nvalid_ref[step * MOE_TILES + j] > 0)

    @pl.when(any_tokens)
    def _():
        units = [(j, e) for j in tiles for e in range(2)]
        hs = [_unpack_pairs([xs_ref[s, rows[j], :] for s in range(Y_SLABS)], BF16) for j in tiles]
        abs_ = [_dot(hs[j], w_refs[4 * j + e][0, 0]) for j, e in units]
        acts = [(ab[:, :D_FF] * _sigmoid(ab[:, :D_FF]) * ab[:, D_FF:]).astype(BF16) for ab in abs_]
        ys = [_dot(act, w_refs[4 * j + 2 + e][0, 0]) for act, (j, e) in zip(acts, units)]
        for j in tiles:
            acc = auxs[j][:, 0:1] * ys[2 * j] + auxs[j][:, 1:2] * ys[2 * j + 1]
            for s, slab in enumerate(_pack_pairs(acc)):
                y_ref[s, rows[j], :] = slab

    @pl.when(jnp.logical_not(any_tokens))
    def _():
        y_ref[...] = jnp.zeros_like(y_ref)


def _moe_call(xs, elo, ehi, nvalid, wgu, wdn, n_tiles, n_tok, dump_tiles):
    d = wgu.shape[2]
    m = MOE_TILES
    assert n_tiles % m == 0
    weight_specs = []
    for j in range(m):
        for shape in ((1, 1, d, 2 * D_FF), (1, 1, D_FF, d)):
            for sel in range(2):
                weight_specs.append(pl.BlockSpec(
                    shape, lambda i, lo, hi, v, j=j, sel=sel: (0, (lo, hi)[sel][m * i + j], 0, 0)))
    weights = [w for _ in range(m) for w in (wgu, wgu, wdn, wdn)]
    return pl.pallas_call(
        functools.partial(_moe_kernel, n_tok=n_tok, dump_tiles=dump_tiles),
        grid_spec=pltpu.PrefetchScalarGridSpec(
            num_scalar_prefetch=3,
            grid=(n_tiles // m,),
            in_specs=[pl.BlockSpec((DISP_SLABS, m * TMO, 128), lambda i, lo, hi, v: (0, i, 0))] + weight_specs,
            out_specs=[pl.BlockSpec((Y_SLABS, m * TMO, 128), lambda i, lo, hi, v: (0, i, 0)),
                       pl.BlockSpec((m, TMO // 128, 128), lambda i, lo, hi, v: (i, 0, 0))]),
        out_shape=[jax.ShapeDtypeStruct((Y_SLABS, n_tiles * TMO, 128), jnp.int32),
                   jax.ShapeDtypeStruct((n_tiles, TMO // 128, 128), jnp.int32)],
        compiler_params=_cparams(("arbitrary",), vmem_mb=VMEM_LIMIT_MOE_MB),
        name="moe_grouped",
    )(elo, ehi, nvalid, xs, *weights)


def _after(x, token):
    return lax.optimization_barrier((x, token))[0]


def _cast_kernel(after_ref, w_ref, o_ref):
    o_ref[...] = w_ref[...].astype(o_ref.dtype)


def _cast_call(w, layer, after):
    _, e, k, n = w.shape
    return pl.pallas_call(
        _cast_kernel,
        grid=(e,),
        in_specs=[pl.BlockSpec(memory_space=pl.ANY), pl.BlockSpec((1, 1, k, n), lambda i: (layer, i, 0, 0))],
        out_specs=pl.BlockSpec((1, 1, k, n), lambda i: (0, i, 0, 0)),
        out_shape=jax.ShapeDtypeStruct((1, e, k, n), BF16),
        compiler_params=_cparams(("parallel",)),
        name="cast_weights",
    )(after, w)


def _moe_layer(disp, meta, counts, w_gate_up, w_down, layer, n, n_pad, sort_rows):
    n_tiles = sort_rows // TMO
    wgu = _cast_call(w_gate_up, layer, counts)
    wdn = _cast_call(w_down, layer, counts)
    cnt = counts[:N_BUCKETS, 0].astype(jnp.int32)
    padded = ((cnt + TMO - 1) // TMO) * TMO
    ends = jnp.cumsum(padded)
    offs = ends - padded
    bucket, rank = meta[0], meta[1]
    pos = rank + jnp.sum(jnp.where(bucket[None, :] == jnp.arange(N_BUCKETS, dtype=jnp.int32)[:, None],
                                   offs[:, None], 0), axis=0)
    tile_start = jnp.arange(n_tiles, dtype=jnp.int32) * TMO
    tile_bucket = jnp.minimum(jnp.sum((tile_start[:, None] >= ends[None, :]).astype(jnp.int32), axis=1), N_BUCKETS - 1)
    pair_lo = np.array([0, 0, 0, 1, 1, 2], np.int32)
    pair_hi = np.array([1, 2, 3, 2, 3, 3], np.int32)
    b_lo = jnp.asarray(np.repeat(np.arange(N_GROUPS), N_PAIRS) * EXP_PER_GROUP + np.tile(pair_lo, N_GROUPS), jnp.int32)
    b_hi = jnp.asarray(np.repeat(np.arange(N_GROUPS), N_PAIRS) * EXP_PER_GROUP + np.tile(pair_hi, N_GROUPS), jnp.int32)
    onehot_tb = (tile_bucket[:, None] == jnp.arange(N_BUCKETS, dtype=jnp.int32)[None, :]).astype(jnp.int32)
    elo = jnp.sum(onehot_tb * b_lo[None, :], axis=1)
    ehi = jnp.sum(onehot_tb * b_hi[None, :], axis=1)
    bucket_end = jnp.sum(onehot_tb * (offs + cnt)[None, :], axis=1)
    nvalid = jnp.where(tile_start < ends[-1], jnp.clip(bucket_end - tile_start, 0, TMO), 0)
    dump = sort_rows + jnp.arange(n_pad - n, dtype=jnp.int32)
    pos_sc = jnp.concatenate([pos, dump])
    total = sort_rows + n_pad - n
    sc_idx = (pos_sc[None, :] + (jnp.arange(DISP_SLABS, dtype=jnp.int32) * total)[:, None]).reshape(-1)
    xs = _sc_scatter_rows(disp.reshape(DISP_SLABS * n_pad, 128), sc_idx, DISP_SLABS * total)
    ys, tok = _moe_call(xs.reshape(DISP_SLABS, total, 128), elo, ehi, nvalid, wgu, wdn, n_tiles,
                        n, (n_pad - n) // TMO)
    back_idx = (tok.reshape(1, sort_rows) + (jnp.arange(Y_SLABS, dtype=jnp.int32) * n_pad)[:, None]).reshape(-1)
    z = _sc_scatter_rows(ys.reshape(Y_SLABS * sort_rows, 128), back_idx, Y_SLABS * n_pad)
    return z.reshape(Y_SLABS, n_pad, 128), tok


def _final_kernel(xn_hbm, z_hbm, gate_ref, g_ref, yp_ref, ys_ref, xbuf, xsem, zbuf, zsem, *, n_prompt_tiles):
    x = _add_moe(_ring_block([xn_hbm], xbuf, xsem), _ring_window(z_hbm, zbuf, zsem, axis=1), gate_ref)
    ms = jnp.mean(x * x, axis=-1, keepdims=True)
    y = x * lax.rsqrt(ms + EPS) * g_ref[...]
    i = pl.program_id(0)

    @pl.when(i < n_prompt_tiles)
    def _():
        yp_ref[...] = y

    @pl.when(i >= n_prompt_tiles)
    def _():
        ys_ref[...] = y


def _final_call(xn, z, mods, g, n_prompt):
    n, d = xn.shape
    npt = n_prompt // TM
    assert n - n_prompt == TM
    return pl.pallas_call(
        functools.partial(_final_kernel, n_prompt_tiles=npt),
        grid=(n // TM,),
        in_specs=[pl.BlockSpec(memory_space=pl.ANY), pl.BlockSpec(memory_space=pl.ANY),
                  _mod_spec(GATE_FFN), pl.BlockSpec((1, d), lambda i: (0, 0))],
        out_specs=_token_specs(npt, d),
        out_shape=[jax.ShapeDtypeStruct((n_prompt, d), F32), jax.ShapeDtypeStruct((TM, d), F32)],
        scratch_shapes=_ring_scratch(d) + [pltpu.VMEM((RING, z.shape[0], TM, 128), z.dtype),
                                           pltpu.SemaphoreType.DMA((RING,))],
        compiler_params=_cparams(("arbitrary",)),
        name="final_norm",
    )(xn, z, mods, g)


def kernel(x_prompt, x_sample, c_prompt, c_sample, state_gla, cache_band_k, cache_band_v, cache_swa_k, cache_swa_v,
           w_ada, b_ada, norm_mix, norm_ffn, norm_final, w_in_even, w_gate_a, b_gate_a, gla_norm, rel_bias_b,
           w_out_even, w_in_odd, sinks_c, w_out_odd, w_router, b_router, w_gate_up, w_down):
    bp, lp, d = x_prompt.shape
    bs, ls_, _ = x_sample.shape
    n_p, n_s = bp * lp, bs * ls_
    n = n_p + n_s
    assert ls_ == CHUNK and n_s == TM and lp % TM == 0 and PAST_LEN % CHUNK == 0

    xp2, xs2 = x_prompt.reshape(n_p, d), x_sample.reshape(n_s, d)

    c16 = jnp.zeros((SEQ_ROWS, d), F32).at[:bp].set(c_prompt).at[bp:bp + bs].set(c_sample)
    mods = _ada_call(c16, w_ada, b_ada)
    seq_of_group = np.concatenate([np.repeat(np.arange(bp), lp // CHUNK), bp + np.arange(bs)])
    mods_g = [mods[l][seq_of_group] for l in range(DEPTH)]

    perm = np.array([4 * (c % 4) + c // 4 for c in range(N_EXPERTS)])
    wr = jnp.zeros((d, 128), F32).at[:, :N_EXPERTS].set(w_router[:, perm])
    br = jnp.zeros((1, 128), F32).at[0, :N_EXPERTS].set(b_router[perm])

    sc_unit = SC_WINDOW * SC_WORKERS * SC_GROUP
    n_pad = n + TMO
    while (DISP_SLABS * n_pad) % sc_unit or (Y_SLABS * n_pad) % TMO or (n_pad - n) % TMO:
        n_pad += TMO
    sort_rows = n + N_BUCKETS * TMO
    while (Y_SLABS * sort_rows) % sc_unit or sort_rows % (MOE_TILES * TMO):
        sort_rows += TMO

    gla_p = gla_s = bk_p = bv_p = bk_s = bv_s = sk_p = sv_p = sk_s = sv_s = None
    xn = z = tok = None
    for l in range(DEPTH):
        i = l // 2
        if l % 2 == 0:
            w = w_in_even[i]
            w_main = jnp.concatenate([w[:, :1536], w[:, 1552:]], axis=1).astype(BF16)
            w_la = jnp.zeros((d, 128), F32).at[:, :GATE_RANK].set(w[:, 1536:1552]).astype(BF16)
            w_gate = jnp.zeros((128, HA * DKA), F32).at[:GATE_RANK].set(w_gate_a[i])
            qa, ka, va, ra, qb, kb, vb, ga = _inproj_even_call(
                xp2, xs2, mods_g[l], norm_mix[l][None], w_main, w_la, w_gate, b_gate_a[i][None])
            xres, moe_prev = [xp2, xs2], None
            gn = gla_norm[i][None]
            oa, s_p = _gla_call(qa, ka, va, ga, ra, jnp.zeros((bp, 256, 128), F32), gn, None,
                                n_seq=bp, seq_rows=lp, row0=0, nb=8)
            oa, s_s = _gla_call(qa, ka, va, ga, ra, state_gla[i].reshape(bs, 256, 128), gn, oa,
                                n_seq=bs, seq_rows=ls_, row0=n_p, nb=1)
            gla_p, gla_s = s_p.reshape(1, bp, HA, DKA, DVA), s_s.reshape(1, bs, HA, DKA, DVA)
            pb = N_PREV_B * CHUNK
            tq, g = 512, 2
            ck = cache_band_k[i].reshape(bs * pb, HB * DHB).astype(BF16)
            cv = cache_band_v[i].reshape(bs * pb, HB * DHB).astype(BF16)
            biases = (_band_bias(rel_bias_b[i], g, pb, _band_valid(g, pb)),
                      _band_bias(rel_bias_b[i], g, pb, _band_valid(g, pb, tq // (CHUNK * g))),
                      _band_bias(rel_bias_b[i], 1, pb, _band_valid(1, pb)))
            ob = _attention(_band_kernel, qb, kb, vb, ck, cv, biases, [], [], width=512, kv_width=512, pb=pb,
                            tq=tq, g=g, bp=bp, lp=lp, bs=bs, name="band")
            tail = lambda a: jnp.stack([a[(b + 1) * lp - pb:(b + 1) * lp] for b in range(bp)]).astype(F32).reshape(1, bp, pb, HB, DHB)
            new = lambda a: a[n_p:].astype(F32).reshape(bs, ls_, HB, DHB)
            bk_p, bv_p = tail(kb), tail(vb)
            bk_s = jnp.concatenate([cache_band_k[i][:, ls_:], new(kb)], axis=1)[None]
            bv_s = jnp.concatenate([cache_band_v[i][:, ls_:], new(vb)], axis=1)[None]
            wo = w_out_even[i].astype(BF16)
            os_, ws = [oa, ob], [wo[:HA * DVA], wo[HA * DVA:]]
        else:
            w = _after(w_in_odd[i], tok)
            w_out_l = _after(w_out_odd[i], tok)
            cache_k_l, cache_v_l = _after(cache_swa_k[i], tok), _after(cache_swa_v[i], tok)
            wk, wv = w[:, 1024:1152], w[:, 1152:1280]
            dup = lambda a: jnp.concatenate([a[:, :64], a[:, :64], a[:, 64:], a[:, 64:]], axis=1)
            w_all = jnp.concatenate([w[:, :1024], dup(wk), dup(wv)], axis=1).astype(BF16)
            cos, sin, rope_map = _rope_tables(lp, ls_, bp, bs)
            q, k, v = _inproj_odd_call(xn, z, mods_g[l - 1], mods_g[l], norm_mix[l][None], cos, sin, rope_map, w_all)
            xres, moe_prev = [xn], (z, mods_g[l - 1])
            pb = WINDOW
            tq, g = 512, 2
            sink = sinks_c[i][None] * LOG2E
            sink_spec = [pl.BlockSpec(memory_space=pltpu.SMEM)]
            dupc = lambda c: jnp.concatenate([c[:, :, 0], c[:, :, 0], c[:, :, 1], c[:, :, 1]], axis=-1).reshape(bs * pb, 256).astype(BF16)
            ck, cv = dupc(cache_k_l), dupc(cache_v_l)
            additive = lambda valid: jnp.asarray(np.where(valid, 0.0, -np.inf), F32)
            masks = (additive(_band_valid(g, pb)), additive(_band_valid(g, pb, tq // (CHUNK * g))),
                     additive(_band_valid(1, pb)))
            o = _attention(_swa_kernel, q, k, v, ck, cv, masks, [sink], sink_spec, width=1024, kv_width=256, pb=pb,
                           tq=tq, g=g, bp=bp, lp=lp, bs=bs, name="swa")
            undup = lambda a: jnp.concatenate([a[:, 0:64], a[:, 128:192]], axis=1).astype(F32)
            tail = lambda a: jnp.stack([undup(a[(b + 1) * lp - pb:(b + 1) * lp]) for b in range(bp)]).reshape(1, bp, pb, KVC, DHC)
            new = lambda a: undup(a[n_p:]).reshape(bs, ls_, KVC, DHC)
            sk_p, sv_p = tail(k), tail(v)
            sk_s = jnp.concatenate([cache_swa_k[i][:, ls_:], new(k)], axis=1)[None]
            sv_s = jnp.concatenate([cache_swa_v[i][:, ls_:], new(v)], axis=1)[None]
            os_, ws = [o], [w_out_l.astype(BF16)]
        xn, disp, meta, counts = _outproj_call(xres, os_, ws, mods_g[l], norm_ffn[l][None], wr, br, n_pad, moe_prev)
        z, tok = _moe_layer(disp, meta, counts, w_gate_up, w_down, l, n, n_pad, sort_rows)

    y_prompt, y_sample = _final_call(xn, z, mods_g[DEPTH - 1], norm_final[None], n_p)
    return (y_prompt.reshape(bp, lp, d), y_sample.reshape(bs, ls_, d),
            gla_p, gla_s, bk_p, bv_p, bk_s, bv_s, sk_p, sv_p, sk_s, sv_s)
```

```python
import functools

import numpy as np
import jax
import jax.numpy as jnp
from jax import lax
from jax.experimental import pallas as pl
from jax.experimental.pallas import tpu as pltpu
from jax.experimental.pallas import tpu_sc as plsc

F32 = jnp.float32
BF16 = jnp.bfloat16

D_MODEL = 1024
DEPTH = 2
CHUNK = 64
PAST_LEN = 4096
HA, DKA, DVA = 4, 64, 128
GATE_RANK = 16
GATE_TAU = 16.0
HB, DHB = 8, 64
N_PREV_B = 8
MAX_REL = 128
HC, KVC, DHC = 16, 2, 64
WINDOW = 128
ROPE_THETA = 10000.0
N_EXPERTS = 16
N_GROUPS = 4
EXP_PER_GROUP = 4
D_FF = 512
EPS = 1e-6

N_PAIRS = 6
N_BUCKETS = N_GROUPS * N_PAIRS
BUCKET_ROWS = 32
Y_SLABS = 4
DISP_SLABS = Y_SLABS + 1
TMO = 256
MOE_TILES = 2
SC_WINDOW = 128
SC_WORKERS = 32
SC_GROUP = 3

TM = 512
SEQ_ROWS = 16
SUB = 16
LOG2E = 1.4426950408889634
VMEM_LIMIT_MB = 48
VMEM_LIMIT_MOE_MB = 56


def _cparams(sem, vmem_mb=VMEM_LIMIT_MB):
    return pltpu.CompilerParams(dimension_semantics=sem, vmem_limit_bytes=vmem_mb * 1024 * 1024)


def _dot(a, b):
    return jnp.dot(a, b, preferred_element_type=F32)


def _dot_nt(a, b):
    return lax.dot_general(a, b, (((1,), (1,)), ((), ())), preferred_element_type=F32)


def _split(a):
    hi = a.astype(BF16)
    lo = (a - hi.astype(F32)).astype(BF16)
    return hi, lo


def _dot3(a, b):
    ah, al = _split(a)
    bh, bl = _split(b)
    return _dot(ah, bh) + _dot(ah, bl) + _dot(al, bh)


def _dot3_narrow(a, b):
    ah, al = _split(a)
    bh, bl = _split(b)
    n = b.shape[1]
    p = _dot(ah, jnp.concatenate([bh, bl], axis=1))
    return p[:, :n] + p[:, n:] + _dot(al, bh)


def _sigmoid(x):
    return 1.0 / (1.0 + jnp.exp(-x))


def _group_affine(y, mul, add):
    parts = []
    for gi in range(y.shape[0] // CHUNK):
        p = y[gi * CHUNK:(gi + 1) * CHUNK]
        if mul is not None:
            p = p * mul[gi:gi + 1]
        if add is not None:
            p = p + add[gi:gi + 1]
        parts.append(p)
    return jnp.concatenate(parts, axis=0)


def _norm_mod(x, g, shift, scale):
    ms = jnp.mean(x * x, axis=-1, keepdims=True)
    return _group_affine(x * lax.rsqrt(ms + EPS) * g, 1.0 + scale, shift)


def _mod_spec(part):
    return pl.BlockSpec((TM // CHUNK, D_MODEL), lambda i: (i, part))


SHIFT_MIX, SCALE_MIX, GATE_MIX, SHIFT_FFN, SCALE_FFN, GATE_FFN = range(6)


RING = 3


def _ring_block(srcs, buf, sem, n_prompt_tiles=None):
    t = buf.shape[1]

    def copy(src, blk, slot):
        return pltpu.make_async_copy(src.at[pl.ds(pl.multiple_of(blk * t, t), t)], buf.at[slot], sem.at[slot])

    def start(step, slot):
        if len(srcs) == 1:
            copy(srcs[0], step, slot).start()
        elif isinstance(step, int):
            assert step < n_prompt_tiles
            copy(srcs[0], step, slot).start()
        else:
            @pl.when(step < n_prompt_tiles)
            def _():
                copy(srcs[0], step, slot).start()

            @pl.when(step >= n_prompt_tiles)
            def _():
                copy(srcs[1], step - n_prompt_tiles, slot).start()

    return _ring(buf, start, lambda slot: copy(srcs[0], 0, slot).wait())


def _ring(buf, start, wait):
    s = pl.program_id(0)

    @pl.when(s == 0)
    def _():
        for k in range(RING - 1):
            start(k, k)

    ahead = s + (RING - 1)

    @pl.when(ahead < pl.num_programs(0))
    def _():
        start(ahead, ahead % RING)

    slot = s % RING
    wait(slot)
    return buf.at[slot]


def _ring_window(src, buf, sem, axis):
    def copy(step, slot):
        idx = (slice(None),) * axis + (pl.ds(pl.multiple_of(step * TM, TM), TM),)
        return pltpu.make_async_copy(src.at[idx], buf.at[slot], sem.at[slot])

    return _ring(buf, lambda step, slot: copy(step, slot).start(), lambda slot: copy(0, slot).wait())


def _ring_scratch(d):
    return [pltpu.VMEM((RING, TM, d), F32), pltpu.SemaphoreType.DMA((RING,))]


def _token_specs(n_prompt_tiles, d):
    return [pl.BlockSpec((TM, d), lambda i: (jnp.minimum(i, n_prompt_tiles - 1), 0)),
            pl.BlockSpec((TM, d), lambda i: (0, 0))]


def _ada_kernel(c_ref, w_ref, b_ref, o_ref):
    c = c_ref[...]
    o_ref[0] = _dot3(c * _sigmoid(c), w_ref[0]) + b_ref[0]


def _ada_call(c16, w_ada, b_ada):
    d = D_MODEL
    tn = 1024
    return pl.pallas_call(
        _ada_kernel,
        grid=(DEPTH, 6 * d // tn),
        in_specs=[pl.BlockSpec((SEQ_ROWS, d), lambda l, j: (0, 0)),
                  pl.BlockSpec((1, d, tn), lambda l, j: (l, 0, j)),
                  pl.BlockSpec((1, 1, tn), lambda l, j: (l, 0, j))],
        out_specs=pl.BlockSpec((1, SEQ_ROWS, tn), lambda l, j: (l, 0, j)),
        out_shape=jax.ShapeDtypeStruct((DEPTH, SEQ_ROWS, 6 * d), F32),
        compiler_params=_cparams(("arbitrary", "arbitrary")),
        name="ada",
    )(c16, w_ada, b_ada.reshape(DEPTH, 1, 6 * d))


def _inproj_even_kernel(xp_hbm, xs_hbm, sh_ref, sc_ref, g_ref, w_ref, wla_ref, wg_ref, bg_ref,
                        qa_ref, ka_ref, va_ref, ra_ref, qb_ref, kb_ref, vb_ref, ga_ref, xbuf, xsem, *, n_prompt_tiles):
    def body(x_ref):
        t = x_ref.shape[0]
        outs = ((qa_ref, 0, 256, DKA ** -0.5), (ka_ref, 256, 512, None), (va_ref, 512, 1024, None),
                (ra_ref, 1024, 1536, None), (qb_ref, 1536, 2048, DHB ** -0.5 * LOG2E), (kb_ref, 2048, 2560, None),
                (vb_ref, 2560, 3072, None))
        shift, scale_ = sh_ref[...], sc_ref[...]
        halves = [slice(0, t // 2), slice(t // 2, t)]
        grp = [slice(0, t // (2 * CHUNK)), slice(t // (2 * CHUNK), t // CHUNK)]
        hbs = [_norm_mod(x_ref[rs, :], g_ref[...], shift[gs], scale_[gs]).astype(BF16) for rs, gs in zip(halves, grp)]
        for rs, hb in zip(halves, hbs):
            zs = [_dot(hb, w_ref[:, lo:hi]) for _, lo, hi, _ in outs]
            la = _dot(hb, wla_ref[...])
            for z, (o_ref, _, _, scale) in zip(zs, outs):
                o_ref[rs, :] = (z if scale is None else z * scale).astype(BF16)
            gl = _dot3(la, wg_ref[...]) + bg_ref[...]
            ga_ref[rs, :] = -(jnp.maximum(-gl, 0.0) + jnp.log(1.0 + jnp.exp(-jnp.abs(gl)))) * (1.0 / GATE_TAU)

    body(_ring_block([xp_hbm, xs_hbm], xbuf, xsem, n_prompt_tiles))


def _inproj_even_call(xp, xs, mods, g, w_main, w_la, w_gate, b_gate):
    d = xp.shape[1]
    npt = xp.shape[0] // TM
    n = xp.shape[0] + xs.shape[0]
    row = lambda i: (i, 0)
    const = lambda i: (0, 0)
    widths = (256, 256, 512, 512, 512, 512, 512)
    out_shape = [jax.ShapeDtypeStruct((n, w), BF16) for w in widths] + [jax.ShapeDtypeStruct((n, 256), F32)]
    out_specs = [pl.BlockSpec((TM, w), row) for w in widths] + [pl.BlockSpec((TM, 256), row)]
    return pl.pallas_call(
        functools.partial(_inproj_even_kernel, n_prompt_tiles=npt),
        grid=(n // TM,),
        in_specs=[pl.BlockSpec(memory_space=pl.ANY), pl.BlockSpec(memory_space=pl.ANY),
                  _mod_spec(SHIFT_MIX), _mod_spec(SCALE_MIX),
                  pl.BlockSpec((1, d), const),
                  pl.BlockSpec(w_main.shape, const), pl.BlockSpec(w_la.shape, const),
                  pl.BlockSpec(w_gate.shape, const), pl.BlockSpec(b_gate.shape, const)],
        out_specs=out_specs, out_shape=out_shape,
        scratch_shapes=_ring_scratch(d),
        compiler_params=_cparams(("arbitrary",)),
        name="inproj_even",
    )(xp, xs, mods, mods, g, w_main, w_la, w_gate, b_gate)


def _rope(x, cos, sin_signed):
    t, w = x.shape
    lane = lax.broadcasted_iota(jnp.int32, (1, w), 1)
    first_half = (lane & 63) < 32
    rot = jnp.where(first_half, pltpu.roll(x, w - 32, 1), pltpu.roll(x, 32, 1))
    reps = w // 128
    return x * jnp.tile(cos, (1, reps)) + rot * jnp.tile(sin_signed, (1, reps))


def _unpack_pairs(slabs, dtype):
    lo = [pltpu.bitcast(s << 16, F32) for s in slabs]
    hi = [pltpu.bitcast(s & jnp.int32(-65536), F32) for s in slabs]
    return jnp.concatenate(lo + hi, axis=1).astype(dtype)


def _pack_pairs(x):
    bits = pltpu.bitcast(x.astype(BF16).astype(F32), jnp.int32)
    half = x.shape[1] // 2
    packed = ((bits[:, :half] >> 16) & jnp.int32(0xFFFF)) | (bits[:, half:] & jnp.int32(-65536))
    return [packed[:, 128 * s:128 * (s + 1)] for s in range(half // 128)]


def _add_moe(xn_ref, z_ref, gate_ref):
    y = _unpack_pairs([z_ref[s] for s in range(z_ref.shape[0])], F32)
    return xn_ref[...] + _group_affine(y, gate_ref[...], None)


def _rope_tables(lp, ls_, bp, bs):
    assert PAST_LEN + ls_ <= lp and lp % 128 == 0 and bs * ls_ == TM
    half = DHC // 2
    inv = ROPE_THETA ** (-jnp.arange(half, dtype=F32) / half)
    inv = jnp.tile(inv, 128 // half)
    sign = jnp.asarray(np.tile(np.repeat([-1.0, 1.0], half), 128 // DHC), F32)
    a = jnp.asarray(np.arange(lp // 128) * 128, F32)[:, None] * inv[None, :]
    b = jnp.asarray(np.arange(128), F32)[:, None] * inv[None, :]
    ca, sa, cb, sb = jnp.cos(a)[:, None], jnp.sin(a)[:, None], jnp.cos(b)[None], jnp.sin(b)[None]
    cos = (ca * cb - sa * sb).reshape(lp, 128)
    sin = ((sa * cb + ca * sb) * sign).reshape(lp, 128)
    with_sample = lambda t: jnp.concatenate([t, jnp.tile(t[PAST_LEN:PAST_LEN + ls_], (bs, 1))], axis=0)
    tiles = lp // TM
    return with_sample(cos), with_sample(sin), lambda i: (jnp.where(i < bp * tiles, i % tiles, tiles), 0)


def _inproj_odd_kernel(xn_hbm, z_hbm, gate_ref, sh_ref, sc_ref, g_ref, cos_ref, sin_ref, w_ref,
                       q_ref, k_ref, v_ref, xbuf, xsem, zbuf, zsem):
    xn_ref = _ring_block([xn_hbm], xbuf, xsem)
    z_ref = _ring_window(z_hbm, zbuf, zsem, axis=1)
    t = xn_ref.shape[0]
    halves = [slice(0, t // 2), slice(t // 2, t)]
    grp = [slice(0, t // (2 * CHUNK)), slice(t // (2 * CHUNK), t // CHUNK)]
    gate, shift, scale = gate_ref[...], sh_ref[...], sc_ref[...]
    xs = []
    for rs, gs in zip(halves, grp):
        y = _unpack_pairs([z_ref[s, rs, :] for s in range(z_ref.shape[0])], F32)
        xs.append(xn_ref[rs, :] + _group_affine(y, gate[gs], None))
    hbs =[_norm_mod(x, g_ref[...], shift[gs], scale[gs]).astype(BF16) for x, gs in zip(xs, grp)]
    qs = [_dot(hb, w_ref[:, 0:1024]) for hb in hbs]
    ks = [_dot(hb, w_ref[:, 1024:1280]) for hb in hbs]
    vs = [_dot(hb, w_ref[:, 1280:1536]) for hb in hbs]
    for rs, q, k, v in zip(halves, qs, ks, vs):
        cos, sin = cos_ref[rs, :], sin_ref[rs, :]
        q_ref[rs, :] = (_rope(q, cos, sin) * (DHC ** -0.5 * LOG2E)).astype(BF16)
        k_ref[rs, :] = _rope(k, cos, sin).astype(BF16)
        v_ref[rs, :] = v.astype(BF16)


def _inproj_odd_call(xn, z, mods_prev, mods, g, cos, sin, rope_map, w):
    n, d = xn.shape
    row = lambda i: (i, 0)
    const = lambda i: (0, 0)
    widths = (1024, 256, 256)
    return pl.pallas_call(
        _inproj_odd_kernel,
        grid=(n // TM,),
        in_specs=[pl.BlockSpec(memory_space=pl.ANY), pl.BlockSpec(memory_space=pl.ANY),
                  _mod_spec(GATE_FFN), _mod_spec(SHIFT_MIX), _mod_spec(SCALE_MIX),
                  pl.BlockSpec((1, d), const),
                  pl.BlockSpec((TM, 128), rope_map), pl.BlockSpec((TM, 128), rope_map),
                  pl.BlockSpec(w.shape, const)],
        out_specs=[pl.BlockSpec((TM, wd), row) for wd in widths],
        out_shape=[jax.ShapeDtypeStruct((n, wd), BF16) for wd in widths],
        scratch_shapes=_ring_scratch(d) + [pltpu.VMEM((RING, z.shape[0], TM, 128), z.dtype),
                                           pltpu.SemaphoreType.DMA((RING,))],
        compiler_params=_cparams(("arbitrary",)),
        name="inproj_odd",
    )(xn, z, mods_prev, mods, mods, g, cos, sin, w)


def _gla_tri():
    t = np.arange(CHUNK)[:, None]
    s = np.arange(CHUNK)[None, :]
    cum = s <= t
    start = s < (t // SUB) * SUB
    end = s < (t // SUB + 1) * SUB
    return jnp.asarray(np.concatenate([cum, start, end], axis=0).astype(np.float32), dtype=BF16)


def _gla_kernel(q_ref, k_ref, v_ref, g_ref, r_ref, s0_ref, gn_ref, tri_ref, o_ref, sout_ref, s_ref, *, nb):
    c_ = CHUNK
    nsub = c_ // SUB

    @pl.when(pl.program_id(1) == 0)
    def _():
        s_ref[...] = s0_ref[0]

    tri = tri_ref[...]
    lane = lax.broadcasted_iota(jnp.int32, (1, 128), 1)
    hmask = [jnp.where(lane < DKA, 1.0, 0.0), jnp.where(lane >= DKA, 1.0, 0.0)]
    ti = lax.broadcasted_iota(jnp.int32, (c_, c_), 0)
    si = lax.broadcasted_iota(jnp.int32, (c_, c_), 1)
    rb, cb = ti >> 4, si >> 4
    m_diag = (rb == cb) & (si <= ti)
    m_off = [(cb == j) & (rb > j) for j in range(nsub - 1)]
    hk = HA * DKA
    gn = gn_ref[...]

    chunks = range(nb)
    heads = [(p, hh) for p in range(HA // 2) for hh in range(2)]
    rows = [slice(c * c_, (c + 1) * c_) for c in chunks]
    pair = [slice(128 * p, 128 * (p + 1)) for p in range(HA // 2)]
    css = []
    for c in chunks:
        g_hi, g_lo = _split(g_ref[rows[c], :])
        css.append(_dot(tri, g_hi) + _dot(tri, g_lo))
    lhs1, lhs2, kds, kes, q_inter, klts, dcols = [], [], [], [], [], [], []
    for c in chunks:
        b, rs, re = css[c][0:c_], css[c][c_:2 * c_], css[c][2 * c_:3 * c_]
        q = q_ref[rows[c], :].astype(F32)
        k = k_ref[rows[c], :].astype(F32)
        bl = b[c_ - 1:c_, :]
        qd = q * jnp.exp(b - rs)
        kd = k * jnp.exp(rs - b)
        ke = k * jnp.exp(re - b)
        qi = q * jnp.exp(b)
        kl = k * jnp.exp(bl - b)
        ql = [q * jnp.exp(jnp.minimum(b - b[SUB * (j + 1) - 1:SUB * (j + 1), :], 0.0)) for j in range(nsub - 1)]
        dcols.append(jnp.broadcast_to(jnp.exp(bl), (8, hk)).T[:, 0:1])
        kds.append([(kd[:, pair[p]] * hmask[hh]).astype(BF16) for p, hh in heads])
        kes.append([(ke[:, pair[p]] * hmask[hh]).astype(BF16) for p, hh in heads])
        klts.append([kl[:, ls].T.astype(BF16) for ls in pair])
        lhs1.append([qd[:, ls].astype(BF16) for ls in pair])
        lhs2.append([jnp.concatenate([ql[j][:, ls] for j in range(nsub - 1)], axis=0).astype(BF16) for ls in pair])
        q_inter.append([(qi[:, pair[p]] * hmask[hh]).astype(BF16) for p, hh in heads])
    a1s = [[_dot_nt(lhs1[c][p], kds[c][h]) for h, (p, hh) in enumerate(heads)] for c in chunks]
    a2s = [[_dot_nt(lhs2[c][p], kes[c][h]) for h, (p, hh) in enumerate(heads)] for c in chunks]
    atts = []
    for c in chunks:
        per_head = []
        for h in range(HA):
            att = jnp.zeros((c_, c_), F32)
            for j in reversed(range(nsub - 1)):
                att = jnp.where(m_off[j], a2s[c][h][j * c_:(j + 1) * c_], att)
            per_head.append(jnp.where(m_diag, a1s[c][h], att).astype(BF16))
        atts.append(per_head)
    vs_ = [[v_ref[rows[c], DVA * h:DVA * (h + 1)] for h in range(HA)] for c in chunks]
    o_intra = [[_dot(atts[c][h], vs_[c][h]) for h in range(HA)] for c in chunks]
    upds = [jnp.concatenate([_dot(klts[c][p][DKA * hh:DKA * (hh + 1)], vs_[c][2 * p + hh]) for p, hh in heads], axis=0)
            for c in chunks]

    s_cur = s_ref[...]
    s_in = []
    for c in chunks:
        s_in.append(s_cur.astype(BF16))
        s_cur = dcols[c] * s_cur + upds[c]
    s_ref[...] = s_cur
    sout_ref[0] = s_cur

    for c in chunks:
        for h in range(HA):
            o = o_intra[c][h] + _dot(q_inter[c][h], s_in[c][pair[h // 2], :])
            ms = jnp.mean(o * o, axis=-1, keepdims=True)
            vs = slice(DVA * h, DVA * (h + 1))
            rr = r_ref[rows[c], vs].astype(F32)
            o_ref[rows[c], vs] = (o * lax.rsqrt(ms + EPS) * gn * (rr * _sigmoid(rr))).astype(BF16)


def _gla_call(q, k, v, g, r, s0, gn, o_prev, *, n_seq, seq_rows, row0, nb):
    tq = nb * CHUNK
    steps = seq_rows // tq
    blk0 = row0 // tq
    row = lambda b, j: (blk0 + b * steps + j, 0)
    const = lambda b, j: (0, 0)
    tri = _gla_tri()
    in_specs = [pl.BlockSpec((tq, 256), row), pl.BlockSpec((tq, 256), row), pl.BlockSpec((tq, 512), row),
                pl.BlockSpec((tq, 256), row), pl.BlockSpec((tq, 512), row),
                pl.BlockSpec((1, 256, 128), lambda b, j: (b, 0, 0)),
                pl.BlockSpec((1, 128), const), pl.BlockSpec(tri.shape, const)]
    args = [q, k, v, g, r, s0, gn, tri]
    aliases = {}
    if o_prev is not None:
        in_specs.append(pl.BlockSpec(memory_space=pl.ANY))
        args.append(o_prev)
        aliases = {len(args) - 1: 0}
    kern = functools.partial(_gla_kernel, nb=nb)
    if o_prev is not None:
        kern = _drop_arg(kern, 8)
    return pl.pallas_call(
        kern,
        grid=(n_seq, steps),
        in_specs=in_specs,
        out_specs=[pl.BlockSpec((tq, 512), row), pl.BlockSpec((1, 256, 128), lambda b, j: (b, 0, 0))],
        out_shape=[jax.ShapeDtypeStruct((q.shape[0], 512), BF16), jax.ShapeDtypeStruct((n_seq, 256, 128), F32)],
        scratch_shapes=[pltpu.VMEM((256, 128), F32)],
        input_output_aliases=aliases,
        compiler_params=_cparams(("arbitrary", "arbitrary")),
        name="gla",
    )(*args)


def _drop_arg(fn, idx):
    def wrapped(*refs):
        return fn(*refs[:idx], *refs[idx + 1:])
    return wrapped


def _window(prev_ref, cur_ref, lo, hi, pb, ls):
    if lo < pb:
        return jnp.concatenate([prev_ref[lo:pb, ls], cur_ref[0:hi - pb, ls]], axis=0)
    return cur_ref[lo - pb:hi - pb, ls]


def _band_kernel(q_ref, kp_ref, kc_ref, vp_ref, vc_ref, bias_ref, o_ref, *, g, n_sub, pb):
    qs = CHUNK * g
    kw_rows = pb + qs
    lane = lax.broadcasted_iota(jnp.int32, (1, 128), 1)
    low = lane < DHB
    hmask = [jnp.where(low, 1.0, 0.0), jnp.where(low, 0.0, 1.0)]
    for s in range(n_sub):
        sb = s if bias_ref.shape[0] > 1 else 0
        rows = slice(qs * s, qs * (s + 1))
        lanes = [slice(128 * p, 128 * (p + 1)) for p in range(HB // 2)]
        heads = [(p, hh) for p in range(HB // 2) for hh in range(2)]
        qps = [q_ref[rows, ls].astype(F32) for ls in lanes]
        kws = [_window(kp_ref, kc_ref, qs * s, qs * s + kw_rows, pb, ls) for ls in lanes]
        vws = [_window(vp_ref, vc_ref, qs * s, qs * s + kw_rows, pb, ls) for ls in lanes]
        qq = [jnp.concatenate([(qps[p] * hmask[hh]).astype(BF16) for hh in range(2)], axis=0) for p in range(HB // 2)]
        sc2 = [_dot_nt(qq[p], kws[p]) for p in range(HB // 2)]
        scs = [sc2[p][qs * hh:qs * (hh + 1)] + bias_ref[sb, 2 * p + hh] for p, hh in heads]
        pes = [jnp.exp2(sc - jnp.max(sc, axis=-1, keepdims=True)) for sc in scs]
        pp = [jnp.concatenate([pes[2 * p + hh].astype(BF16) for hh in range(2)], axis=0) for p in range(HB // 2)]
        o2 = [_dot(pp[p], vws[p]) for p in range(HB // 2)]
        outs = [o2[p][qs * hh:qs * (hh + 1)] / jnp.sum(pes[2 * p + hh], axis=-1, keepdims=True) for p, hh in heads]
        for p, ls in enumerate(lanes):
            o_ref[rows, ls] = jnp.where(low, outs[2 * p], outs[2 * p + 1]).astype(BF16)


def _band_valid(g, pb, n_sub=None):
    rows, kw = CHUNK * g, pb + CHUNK * g
    r = np.arange(rows)[:, None]
    c = np.arange(kw)[None, :]
    dd = c // CHUNK - r // CHUNK
    band = (dd >= 0) & (dd <= pb // CHUNK)
    if n_sub is None:
        return band[None]
    return np.stack([band & (c >= pb - rows * s) for s in range(n_sub)])


def _band_bias(table, g, pb, valid):
    rows, kw = CHUNK * g, pb + CHUNK * g
    period = kw + rows
    m = np.arange(period)
    m = np.where(m < kw, m, m - period)
    ext = table[:, np.clip(m - pb, -MAX_REL, MAX_REL) + MAX_REL] * LOG2E
    flat = jnp.tile(ext, (1, rows))[:, :rows * (period - 1)]
    bias = flat.reshape(table.shape[0], rows, period - 1)[:, :, :kw]
    return jnp.where(valid[:, None], bias[None], -jnp.inf)


def _attn_call(kernel, q, kp, kc, vp, vc, extra, extra_specs, o_prev, *, width, kv_width, tq, pb,
               n_blocks, blk_map, prev_map, name):
    row = lambda i: (blk_map(i), 0)
    prev = lambda i: (prev_map(i), 0)
    in_specs = [pl.BlockSpec((tq, width), row),
                pl.BlockSpec((pb, kv_width), prev), pl.BlockSpec((tq, kv_width), row),
                pl.BlockSpec((pb, kv_width), prev), pl.BlockSpec((tq, kv_width), row)] + extra_specs
    args = [q, kp, kc, vp, vc] + extra
    aliases = {}
    if o_prev is not None:
        in_specs.append(pl.BlockSpec(memory_space=pl.ANY))
        args.append(o_prev)
        aliases = {len(args) - 1: 0}
        kernel = _drop_arg(kernel, len(args) - 1)
    return pl.pallas_call(
        kernel,
        grid=(n_blocks,),
        in_specs=in_specs,
        out_specs=pl.BlockSpec((tq, width), row),
        out_shape=jax.ShapeDtypeStruct((q.shape[0], width), BF16),
        input_output_aliases=aliases,
        compiler_params=_cparams(("parallel",)),
        name=name,
    )(*args)


def _attention(kernel_fn, q, k, v, cache_k, cache_v, masks, extra, extra_specs, *, width, kv_width, pb, tq, g,
               bp, lp, bs, name):
    bps = lp // tq
    n_sub = tq // (CHUNK * g)
    spec = lambda a: [pl.BlockSpec(a.shape, lambda i: (0,) * a.ndim)]
    kern = functools.partial(kernel_fn, g=g, n_sub=n_sub, pb=pb)
    common = dict(width=width, kv_width=kv_width, pb=pb)
    main = lambda i: (i // (bps - 1)) * bps + i % (bps - 1) + 1
    o = _attn_call(kern, q, k, k, v, v, [masks[0]] + extra, spec(masks[0]) + extra_specs, None, tq=tq,
                   n_blocks=bp * (bps - 1), blk_map=main, prev_map=lambda i: main(i) * (tq // pb) - 1,
                   name=name + "_main", **common)
    first = lambda i: i * bps
    o = _attn_call(kern, q, k, k, v, v, [masks[1]] + extra, spec(masks[1]) + extra_specs, o, tq=tq,
                   n_blocks=bp, blk_map=first, prev_map=lambda i: jnp.maximum(first(i) * (tq // pb) - 1, 0),
                   name=name + "_first", **common)
    samp = functools.partial(kernel_fn, g=1, n_sub=1, pb=pb)
    return _attn_call(samp, q, cache_k, k, cache_v, v, [masks[2]] + extra, spec(masks[2]) + extra_specs, o, tq=CHUNK,
                      n_blocks=bs, blk_map=lambda i: bp * lp // CHUNK + i, prev_map=lambda i: i,
                      name=name + "_sample", **common)


def _swa_kernel(q_ref, kp_ref, kc_ref, vp_ref, vc_ref, mask_ref, sink_ref, o_ref, *, g, n_sub, pb):
    qs = CHUNK * g
    kw_rows = pb + qs
    lane = lax.broadcasted_iota(jnp.int32, (1, 128), 1)
    low = lane < DHC
    hmask = [jnp.where(low, 1.0, 0.0), jnp.where(low, 0.0, 1.0)]
    pairs_per_kv = HC // KVC // 2
    for s in range(n_sub):
        msk = mask_ref[s if mask_ref.shape[0] > 1 else 0]
        rows = slice(qs * s, qs * (s + 1))
        kws = [_window(kp_ref, kc_ref, qs * s, qs * s + kw_rows, pb, slice(128 * kv, 128 * (kv + 1))) for kv in range(KVC)]
        vws = [_window(vp_ref, vc_ref, qs * s, qs * s + kw_rows, pb, slice(128 * kv, 128 * (kv + 1))) for kv in range(KVC)]
        heads = [(j, hh) for j in range(HC // 2) for hh in range(2)]
        qps = [q_ref[rows, 128 * j:128 * (j + 1)].astype(F32) for j in range(HC // 2)]
        per_kv = 2 * pairs_per_kv
        qq = [jnp.concatenate([(qps[j] * hmask[hh]).astype(BF16) for j, hh in heads[per_kv * kv:per_kv * (kv + 1)]],
                              axis=0) for kv in range(KVC)]
        sc2 = [_dot_nt(qq[kv], kws[kv]) for kv in range(KVC)]
        scs = [sc2[u // per_kv][qs * (u % per_kv):qs * (u % per_kv + 1)] + msk for u in range(len(heads))]
        sks = [sink_ref[0, 2 * j + hh] for j, hh in heads]
        ms = [jnp.maximum(jnp.max(sc, axis=-1, keepdims=True), sk) for sc, sk in zip(scs, sks)]
        pes = [jnp.exp2(sc - m) for sc, m in zip(scs, ms)]
        pp = [jnp.concatenate([pe.astype(BF16) for pe in pes[per_kv * kv:per_kv * (kv + 1)]], axis=0) for kv in range(KVC)]
        o2 = [_dot(pp[kv], vws[kv]) for kv in range(KVC)]
        outs = [o2[u // per_kv][qs * (u % per_kv):qs * (u % per_kv + 1)]
                / (jnp.sum(pes[u], axis=-1, keepdims=True) + jnp.exp2(sks[u] - ms[u])) for u in range(len(heads))]
        for j in range(HC // 2):
            o_ref[rows, 128 * j:128 * (j + 1)] = jnp.where(low, outs[2 * j], outs[2 * j + 1]).astype(BF16)


def _route(logits_t):
    a = [logits_t[4 * j:4 * j + 4] for j in range(EXP_PER_GROUP)]

    def first_argmax(vals, m):
        idx = jnp.full(m.shape, float(len(vals) - 1), F32)
        for j in reversed(range(len(vals) - 1)):
            idx = jnp.where(vals[j] == m, float(j), idx)
        return idx

    m1 = functools.reduce(jnp.maximum, a)
    i1 = first_argmax(a, m1)
    bsec = [jnp.where(i1 == float(j), -jnp.inf, a[j]) for j in range(EXP_PER_GROUP)]
    m2 = functools.reduce(jnp.maximum, bsec)
    i2 = first_argmax(bsec, m2)
    rows = lambda x: [x[gi:gi + 1] for gi in range(N_GROUPS)]
    gm = functools.reduce(jnp.maximum, rows(m1))
    gscore = jnp.exp(m1 - gm) + jnp.exp(m2 - gm)
    gs = rows(gscore)
    gsel = first_argmax(gs, functools.reduce(jnp.maximum, gs))

    def pick(x):
        xr = rows(x)
        out = xr[N_GROUPS - 1]
        for gi in reversed(range(N_GROUPS - 1)):
            out = jnp.where(gsel == float(gi), xr[gi], out)
        return out

    p1 = jnp.exp(pick(m1) - gm)
    p2 = jnp.exp(pick(m2) - gm)
    w1 = p1 / (p1 + p2)
    w2 = p2 / (p1 + p2)
    s1, s2 = pick(i1), pick(i2)
    lo, hi = jnp.minimum(s1, s2), jnp.maximum(s1, s2)
    pair = jnp.where(lo == 0.0, hi - 1.0, jnp.where(lo == 1.0, hi + 1.0, 5.0))
    bucket = gsel * float(N_PAIRS) + pair
    first_is_lo = s1 < s2
    return bucket, jnp.where(first_is_lo, w1, w2), jnp.where(first_is_lo, w2, w1)


def _outproj_kernel(*refs, n_x, n_o, n_prompt_tiles, moe_pending):
    n_in = n_x + (2 if moe_pending else 0)
    x_refs = refs[:n_x]
    o_refs = refs[n_in:n_in + n_o]
    w_refs = refs[n_in + n_o:n_in + 2 * n_o]
    n_tail = 2 * n_o + (2 if moe_pending else 0)
    (gate_ref, nf_ref, sh_ref, sc_ref, wr_ref, br_ref, tri_ref,
     xn_ref, disp_ref, meta_ref, cnt_ref, run_ref, xbuf, xsem) = refs[n_in + 2 * n_o:len(refs) - n_tail]
    o_scratch = refs[len(refs) - n_tail:len(refs) - n_tail + 2 * n_o]
    o_refs = [_ring_window(o_hbm, o_scratch[2 * i], o_scratch[2 * i + 1], axis=0) for i, o_hbm in enumerate(o_refs)]
    t = xn_ref.shape[0]

    @pl.when(pl.program_id(0) == 0)
    def _():
        run_ref[...] = jnp.zeros_like(run_ref)

    x_view = _ring_block(list(x_refs), xbuf, xsem, n_prompt_tiles)
    if moe_pending:
        z_view = _ring_window(refs[n_x], refs[-2], refs[-1], axis=1)
        gate_prev = refs[n_x + 1][...]

    def x_rows(rs, gs):
        if not moe_pending:
            return x_view[rs, :]
        y = _unpack_pairs([z_view[s, rs, :] for s in range(z_view.shape[0])], F32)
        return x_view[rs, :] + _group_affine(y, gate_prev[gs], None)

    halves = [slice(0, t // 2), slice(t // 2, t)]
    grp = [slice(0, t // (2 * CHUNK)), slice(t // (2 * CHUNK), t // CHUNK)]
    ys = []
    for rs in halves:
        y = _dot(o_refs[0][rs, :], w_refs[0][...])
        for i in range(1, n_o):
            y = y + _dot(o_refs[i][rs, :], w_refs[i][...])
        ys.append(y)
    gate, shift, scale = gate_ref[...], sh_ref[...], sc_ref[...]
    gys = [_group_affine(y, gate[gs], None) for y, gs in zip(ys, grp)]

    for rs, gs, gy in zip(halves, grp, gys):
        xn_ref[rs, :] = x_rows(rs, gs) + gy
    hs = [_norm_mod(xn_ref[rs, :], nf_ref[...], shift[gs], scale[gs]) for rs, gs in zip(halves, grp)]
    for rs, h in zip(halves, hs):
        for s, slab in enumerate(_pack_pairs(h)):
            disp_ref[s, rs, :] = slab
    logits_t = [(_dot3_narrow(h, wr_ref[...]) + br_ref[...]).T[0:N_EXPERTS] for h in hs]
    bucket, w_lo, w_hi = _route(jnp.concatenate(logits_t, axis=1))
    r128 = lax.broadcasted_iota(jnp.int32, (128, t), 0)
    tok = (pl.program_id(0) * t + lax.broadcasted_iota(jnp.int32, (1, t), 1)).astype(F32)
    aux = jnp.where(r128 == 0, w_lo, jnp.where(r128 == 1, w_hi, jnp.where(r128 == 2, tok, 0.0))).T
    disp_ref[disp_ref.shape[0] - 1] = pltpu.bitcast(aux, jnp.int32)
    brow = lax.broadcasted_iota(jnp.int32, (BUCKET_ROWS, t), 0).astype(F32)
    onehot = jnp.where(brow == bucket, 1.0, 0.0)
    before = _dot(onehot.astype(BF16), tri_ref[...]) + run_ref[:, 0:1]
    rank = jnp.sum(onehot * before, axis=0, keepdims=True)
    run_ref[...] = run_ref[...] + jnp.sum(onehot, axis=1, keepdims=True)
    cnt_ref[...] = run_ref[...]
    r8 = lax.broadcasted_iota(jnp.int32, (8, t), 0)
    meta_ref[...] = jnp.where(r8 == 0, bucket, jnp.where(r8 == 1, rank, 0.0)).astype(jnp.int32)


def _outproj_call(xs_, os_, ws, mods, nf, wr, br, n_pad, moe_prev=None):
    d = xs_[0].shape[1]
    n = sum(a.shape[0] for a in xs_)
    npt = xs_[0].shape[0] // TM
    row = lambda i: (i, 0)
    const = lambda i: (0, 0)
    n_o = len(os_)
    prev_specs, prev_args, prev_scratch = [], [], []
    if moe_prev is not None:
        z, mods_prev = moe_prev
        prev_specs = [pl.BlockSpec(memory_space=pl.ANY), _mod_spec(GATE_FFN)]
        prev_args = [z, mods_prev]
        prev_scratch = [pltpu.VMEM((RING, z.shape[0], TM, 128), z.dtype), pltpu.SemaphoreType.DMA((RING,))]
    in_specs = ([pl.BlockSpec(memory_space=pl.ANY) for _ in xs_]
                + prev_specs
                + [pl.BlockSpec(memory_space=pl.ANY) for _ in os_]
                + [pl.BlockSpec(w.shape, const) for w in ws]
                + [_mod_spec(GATE_MIX), pl.BlockSpec((1, d), const),
                   _mod_spec(SHIFT_FFN), _mod_spec(SCALE_FFN),
                   pl.BlockSpec(wr.shape, const), pl.BlockSpec(br.shape, const),
                   pl.BlockSpec((TM, TM), const)])
    tri = jnp.asarray(np.triu(np.ones((TM, TM), np.float32), k=1), dtype=BF16)
    return pl.pallas_call(
        functools.partial(_outproj_kernel, n_x=len(xs_), n_o=n_o, n_prompt_tiles=npt,
                          moe_pending=moe_prev is not None),
        grid=(n // TM,),
        in_specs=in_specs,
        out_specs=[pl.BlockSpec((TM, d), row), pl.BlockSpec((DISP_SLABS, TM, 128), lambda i: (0, i, 0)),
                   pl.BlockSpec((8, TM), lambda i: (0, i)), pl.BlockSpec((BUCKET_ROWS, 128), const)],
        out_shape=[jax.ShapeDtypeStruct((n, d), F32), jax.ShapeDtypeStruct((DISP_SLABS, n_pad, 128), jnp.int32),
                   jax.ShapeDtypeStruct((8, n), jnp.int32), jax.ShapeDtypeStruct((BUCKET_ROWS, 128), F32)],
        scratch_shapes=([pltpu.VMEM((BUCKET_ROWS, 128), F32)] + _ring_scratch(d)
                        + [sc for o in os_ for sc in (pltpu.VMEM((RING, TM, o.shape[1]), o.dtype),
                                                      pltpu.SemaphoreType.DMA((RING,)))]
                        + prev_scratch),
        compiler_params=_cparams(("arbitrary",)),
        name="outproj_router",
    )(*xs_, *prev_args, *os_, *ws, mods, nf, mods, mods, wr, br, tri)


def _sc_mesh():
    return plsc.VectorSubcoreMesh(core_axis_name="core", subcore_axis_name="subcore")


def _sc_scatter_rows(src, idx, n_out):
    r = idx.shape[0]
    k = SC_GROUP
    w_per = r // (SC_WINDOW * SC_WORKERS)
    assert idx.shape == (src.shape[0],) and r % (SC_WINDOW * SC_WORKERS) == 0 and w_per % k == 0
    n_groups = w_per // k

    @functools.partial(
        pl.kernel, out_type=jax.ShapeDtypeStruct((n_out, 128), src.dtype), mesh=_sc_mesh(),
        scratch_types=[pltpu.VMEM((w_per, SC_WINDOW), jnp.int32),
                       pltpu.VMEM((2 * k, SC_WINDOW, 128), src.dtype),
                       pltpu.SemaphoreType.DMA((2,)), pltpu.SemaphoreType.DMA((2,))])
    def copy(x_hbm, i_hbm, o_hbm, ibuf, xbuf, in_sem, out_sem):
        wid = lax.axis_index("core") * (SC_WORKERS // 2) + lax.axis_index("subcore")
        pltpu.sync_copy(i_hbm.at[wid], ibuf)
        first = wid * w_per

        def start_in(g, slot):
            return [pltpu.async_copy(x_hbm.at[pl.ds((first + g * k + c) * SC_WINDOW, SC_WINDOW)],
                                     xbuf.at[slot * k + c], in_sem.at[slot]) for c in range(k)]

        def start_out(g, slot):
            return [pltpu.async_copy(xbuf.at[slot * k + c], o_hbm.at[ibuf.at[g * k + c]], out_sem.at[slot])
                    for c in range(k)]

        pending_in = start_in(0, 0)
        for g in range(n_groups):
            slot = g % 2
            for cp in pending_in:
                cp.wait()
            pending_out = start_out(g, slot)
            if g + 1 < n_groups:
                pending_in = start_in(g + 1, 1 - slot)
            for cp in pending_out:
                cp.wait()

    return copy(src, idx.reshape(SC_WORKERS, w_per, SC_WINDOW))


def _moe_kernel(elo_ref, ehi_ref, nvalid_ref, xs_ref, *refs, n_tok, dump_tiles):
    w_refs, (y_ref, tok_ref) = refs[:4 * MOE_TILES], refs[4 * MOE_TILES:]
    step = pl.program_id(0)
    t = TMO
    tiles = range(MOE_TILES)
    rows = [slice(t * j, t * (j + 1)) for j in tiles]
    auxs = [pltpu.bitcast(xs_ref[Y_SLABS, rows[j], :], F32) for j in tiles]
    r = lax.broadcasted_iota(jnp.int32, (1, t), 1)
    for j in tiles:
        i = step * MOE_TILES + j
        spare = n_tok + (i % dump_tiles) * t + r
        tok = jnp.where(r < nvalid_ref[i], auxs[j].T[2:3, :].astype(jnp.int32), spare)
        for c in range(t // 128):
            tok_ref[j, c:c + 1, :] = tok[:, 128 * c:128 * (c + 1)]

    any_tokens = nvalid_ref[step * MOE_TILES] > 0
    for j in range(1, MOE_TILES):
        any_tokens = jnp.logical_or(any_tokens, nvalid_ref[step * MOE_TILES + j] > 0)

    @pl.when(any_tokens)
    def _():
        units = [(j, e) for j in tiles for e in range(2)]
        hs = [_unpack_pairs([xs_ref[s, rows[j], :] for s in range(Y_SLABS)], BF16) for j in tiles]
        abs_ = [_dot(hs[j], w_refs[4 * j + e][0, 0]) for j, e in units]
        acts = [(ab[:, :D_FF] * _sigmoid(ab[:, :D_FF]) * ab[:, D_FF:]).astype(BF16) for ab in abs_]
        ys = [_dot(act, w_refs[4 * j + 2 + e][0, 0]) for act, (j, e) in zip(acts, units)]
        for j in tiles:
            acc = auxs[j][:, 0:1] * ys[2 * j] + auxs[j][:, 1:2] * ys[2 * j + 1]
            for s, slab in enumerate(_pack_pairs(acc)):
                y_ref[s, rows[j], :] = slab

    @pl.when(jnp.logical_not(any_tokens))
    def _():
        y_ref[...] = jnp.zeros_like(y_ref)


def _moe_call(xs, elo, ehi, nvalid, wgu, wdn, n_tiles, n_tok, dump_tiles):
    d = wgu.shape[2]
    m = MOE_TILES
    assert n_tiles % m == 0
    weight_specs = []
    for j in range(m):
        for shape in ((1, 1, d, 2 * D_FF), (1, 1, D_FF, d)):
            for sel in range(2):
                weight_specs.append(pl.BlockSpec(
                    shape, lambda i, lo, hi, v, j=j, sel=sel: (0, (lo, hi)[sel][m * i + j], 0, 0)))
    weights = [w for _ in range(m) for w in (wgu, wgu, wdn, wdn)]
    return pl.pallas_call(
        functools.partial(_moe_kernel, n_tok=n_tok, dump_tiles=dump_tiles),
        grid_spec=pltpu.PrefetchScalarGridSpec(
            num_scalar_prefetch=3,
            grid=(n_tiles // m,),
            in_specs=[pl.BlockSpec((DISP_SLABS, m * TMO, 128), lambda i, lo, hi, v: (0, i, 0))] + weight_specs,
            out_specs=[pl.BlockSpec((Y_SLABS, m * TMO, 128), lambda i, lo, hi, v: (0, i, 0)),
                       pl.BlockSpec((m, TMO // 128, 128), lambda i, lo, hi, v: (i, 0, 0))]),
        out_shape=[jax.ShapeDtypeStruct((Y_SLABS, n_tiles * TMO, 128), jnp.int32),
                   jax.ShapeDtypeStruct((n_tiles, TMO // 128, 128), jnp.int32)],
        compiler_params=_cparams(("arbitrary",), vmem_mb=VMEM_LIMIT_MOE_MB),
        name="moe_grouped",
    )(elo, ehi, nvalid, xs, *weights)


def _after(x, token):
    return lax.optimization_barrier((x, token))[0]


def _cast_kernel(after_ref, w_ref, o_ref):
    o_ref[...] = w_ref[...].astype(o_ref.dtype)


def _cast_call(w, layer, after):
    _, e, k, n = w.shape
    return pl.pallas_call(
        _cast_kernel,
        grid=(e,),
        in_specs=[pl.BlockSpec(memory_space=pl.ANY), pl.BlockSpec((1, 1, k, n), lambda i: (layer, i, 0, 0))],
        out_specs=pl.BlockSpec((1, 1, k, n), lambda i: (0, i, 0, 0)),
        out_shape=jax.ShapeDtypeStruct((1, e, k, n), BF16),
        compiler_params=_cparams(("parallel",)),
        name="cast_weights",
    )(after, w)


def _moe_layer(disp, meta, counts, w_gate_up, w_down, layer, n, n_pad, sort_rows):
    n_tiles = sort_rows // TMO
    wgu = _cast_call(w_gate_up, layer, counts)
    wdn = _cast_call(w_down, layer, counts)
    cnt = counts[:N_BUCKETS, 0].astype(jnp.int32)
    padded = ((cnt + TMO - 1) // TMO) * TMO
    ends = jnp.cumsum(padded)
    offs = ends - padded
    bucket, rank = meta[0], meta[1]
    pos = rank + jnp.sum(jnp.where(bucket[None, :] == jnp.arange(N_BUCKETS, dtype=jnp.int32)[:, None],
                                   offs[:, None], 0), axis=0)
    tile_start = jnp.arange(n_tiles, dtype=jnp.int32) * TMO
    tile_bucket = jnp.minimum(jnp.sum((tile_start[:, None] >= ends[None, :]).astype(jnp.int32), axis=1), N_BUCKETS - 1)
    pair_lo = np.array([0, 0, 0, 1, 1, 2], np.int32)
    pair_hi = np.array([1, 2, 3, 2, 3, 3], np.int32)
    b_lo = jnp.asarray(np.repeat(np.arange(N_GROUPS), N_PAIRS) * EXP_PER_GROUP + np.tile(pair_lo, N_GROUPS), jnp.int32)
    b_hi = jnp.asarray(np.repeat(np.arange(N_GROUPS), N_PAIRS) * EXP_PER_GROUP + np.tile(pair_hi, N_GROUPS), jnp.int32)
    onehot_tb = (tile_bucket[:, None] == jnp.arange(N_BUCKETS, dtype=jnp.int32)[None, :]).astype(jnp.int32)
    elo = jnp.sum(onehot_tb * b_lo[None, :], axis=1)
    ehi = jnp.sum(onehot_tb * b_hi[None, :], axis=1)
    bucket_end = jnp.sum(onehot_tb * (offs + cnt)[None, :], axis=1)
    nvalid = jnp.where(tile_start < ends[-1], jnp.clip(bucket_end - tile_start, 0, TMO), 0)
    dump = sort_rows + jnp.arange(n_pad - n, dtype=jnp.int32)
    pos_sc = jnp.concatenate([pos, dump])
    total = sort_rows + n_pad - n
    sc_idx = (pos_sc[None, :] + (jnp.arange(DISP_SLABS, dtype=jnp.int32) * total)[:, None]).reshape(-1)
    xs = _sc_scatter_rows(disp.reshape(DISP_SLABS * n_pad, 128), sc_idx, DISP_SLABS * total)
    ys, tok = _moe_call(xs.reshape(DISP_SLABS, total, 128), elo, ehi, nvalid, wgu, wdn, n_tiles,
                        n, (n_pad - n) // TMO)
    back_idx = (tok.reshape(1, sort_rows) + (jnp.arange(Y_SLABS, dtype=jnp.int32) * n_pad)[:, None]).reshape(-1)
    z = _sc_scatter_rows(ys.reshape(Y_SLABS * sort_rows, 128), back_idx, Y_SLABS * n_pad)
    return z.reshape(Y_SLABS, n_pad, 128), tok


def _final_kernel(xn_hbm, z_hbm, gate_ref, g_ref, yp_ref, ys_ref, xbuf, xsem, zbuf, zsem, *, n_prompt_tiles):
    x = _add_moe(_ring_block([xn_hbm], xbuf, xsem), _ring_window(z_hbm, zbuf, zsem, axis=1), gate_ref)
    ms = jnp.mean(x * x, axis=-1, keepdims=True)
    y = x * lax.rsqrt(ms + EPS) * g_ref[...]
    i = pl.program_id(0)

    @pl.when(i < n_prompt_tiles)
    def _():
        yp_ref[...] = y

    @pl.when(i >= n_prompt_tiles)
    def _():
        ys_ref[...] = y


def _final_call(xn, z, mods, g, n_prompt):
    n, d = xn.shape
    npt = n_prompt // TM
    assert n - n_prompt == TM
    return pl.pallas_call(
        functools.partial(_final_kernel, n_prompt_tiles=npt),
        grid=(n // TM,),
        in_specs=[pl.BlockSpec(memory_space=pl.ANY), pl.BlockSpec(memory_space=pl.ANY),
                  _mod_spec(GATE_FFN), pl.BlockSpec((1, d), lambda i: (0, 0))],
        out_specs=_token_specs(npt, d),
        out_shape=[jax.ShapeDtypeStruct((n_prompt, d), F32), jax.ShapeDtypeStruct((TM, d), F32)],
        scratch_shapes=_ring_scratch(d) + [pltpu.VMEM((RING, z.shape[0], TM, 128), z.dtype),
                                           pltpu.SemaphoreType.DMA((RING,))],
        compiler_params=_cparams(("arbitrary",)),
        name="final_norm",
    )(xn, z, mods, g)


def kernel(x_prompt, x_sample, c_prompt, c_sample, state_gla, cache_band_k, cache_band_v, cache_swa_k, cache_swa_v,
           w_ada, b_ada, norm_mix, norm_ffn, norm_final, w_in_even, w_gate_a, b_gate_a, gla_norm, rel_bias_b,
           w_out_even, w_in_odd, sinks_c, w_out_odd, w_router, b_router, w_gate_up, w_down):
    bp, lp, d = x_prompt.shape
    bs, ls_, _ = x_sample.shape
    n_p, n_s = bp * lp, bs * ls_
    n = n_p + n_s
    assert ls_ == CHUNK and n_s == TM and lp % TM == 0 and PAST_LEN % CHUNK == 0

    xp2, xs2 = x_prompt.reshape(n_p, d), x_sample.reshape(n_s, d)

    c16 = jnp.zeros((SEQ_ROWS, d), F32).at[:bp].set(c_prompt).at[bp:bp + bs].set(c_sample)
    mods = _ada_call(c16, w_ada, b_ada)
    seq_of_group = np.concatenate([np.repeat(np.arange(bp), lp // CHUNK), bp + np.arange(bs)])
    mods_g = [mods[l][seq_of_group] for l in range(DEPTH)]

    perm = np.array([4 * (c % 4) + c // 4 for c in range(N_EXPERTS)])
    wr = jnp.zeros((d, 128), F32).at[:, :N_EXPERTS].set(w_router[:, perm])
    br = jnp.zeros((1, 128), F32).at[0, :N_EXPERTS].set(b_router[perm])

    sc_unit = SC_WINDOW * SC_WORKERS * SC_GROUP
    n_pad = n + TMO
    while (DISP_SLABS * n_pad) % sc_unit or (Y_SLABS * n_pad) % TMO or (n_pad - n) % TMO:
        n_pad += TMO
    sort_rows = n + N_BUCKETS * TMO
    while (Y_SLABS * sort_rows) % sc_unit or sort_rows % (MOE_TILES * TMO):
        sort_rows += TMO

    gla_p = gla_s = bk_p = bv_p = bk_s = bv_s = sk_p = sv_p = sk_s = sv_s = None
    xn = z = tok = None
    for l in range(DEPTH):
        i = l // 2
        if l % 2 == 0:
            w = w_in_even[i]
            w_main = jnp.concatenate([w[:, :1536], w[:, 1552:]], axis=1).astype(BF16)
            w_la = jnp.zeros((d, 128), F32).at[:, :GATE_RANK].set(w[:, 1536:1552]).astype(BF16)
            w_gate = jnp.zeros((128, HA * DKA), F32).at[:GATE_RANK].set(w_gate_a[i])
            qa, ka, va, ra, qb, kb, vb, ga = _inproj_even_call(
                xp2, xs2, mods_g[l], norm_mix[l][None], w_main, w_la, w_gate, b_gate_a[i][None])
            xres, moe_prev = [xp2, xs2], None
            gn = gla_norm[i][None]
            oa, s_p = _gla_call(qa, ka, va, ga, ra, jnp.zeros((bp, 256, 128), F32), gn, None,
                                n_seq=bp, seq_rows=lp, row0=0, nb=8)
            oa, s_s = _gla_call(qa, ka, va, ga, ra, state_gla[i].reshape(bs, 256, 128), gn, oa,
                                n_seq=bs, seq_rows=ls_, row0=n_p, nb=1)
            gla_p, gla_s = s_p.reshape(1, bp, HA, DKA, DVA), s_s.reshape(1, bs, HA, DKA, DVA)
            pb = N_PREV_B * CHUNK
            tq, g = 512, 2
            ck = cache_band_k[i].reshape(bs * pb, HB * DHB).astype(BF16)
            cv = cache_band_v[i].reshape(bs * pb, HB * DHB).astype(BF16)
            biases = (_band_bias(rel_bias_b[i], g, pb, _band_valid(g, pb)),
                      _band_bias(rel_bias_b[i], g, pb, _band_valid(g, pb, tq // (CHUNK * g))),
                      _band_bias(rel_bias_b[i], 1, pb, _band_valid(1, pb)))
            ob = _attention(_band_kernel, qb, kb, vb, ck, cv, biases, [], [], width=512, kv_width=512, pb=pb,
                            tq=tq, g=g, bp=bp, lp=lp, bs=bs, name="band")
            tail = lambda a: jnp.stack([a[(b + 1) * lp - pb:(b + 1) * lp] for b in range(bp)]).astype(F32).reshape(1, bp, pb, HB, DHB)
            new = lambda a: a[n_p:].astype(F32).reshape(bs, ls_, HB, DHB)
            bk_p, bv_p = tail(kb), tail(vb)
            bk_s = jnp.concatenate([cache_band_k[i][:, ls_:], new(kb)], axis=1)[None]
            bv_s = jnp.concatenate([cache_band_v[i][:, ls_:], new(vb)], axis=1)[None]
            wo = w_out_even[i].astype(BF16)
            os_, ws = [oa, ob], [wo[:HA * DVA], wo[HA * DVA:]]
        else:
            w = _after(w_in_odd[i], tok)
            w_out_l = _after(w_out_odd[i], tok)
            cache_k_l, cache_v_l = _after(cache_swa_k[i], tok), _after(cache_swa_v[i], tok)
            wk, wv = w[:, 1024:1152], w[:, 1152:1280]
            dup = lambda a: jnp.concatenate([a[:, :64], a[:, :64], a[:, 64:], a[:, 64:]], axis=1)
            w_all = jnp.concatenate([w[:, :1024], dup(wk), dup(wv)], axis=1).astype(BF16)
            cos, sin, rope_map = _rope_tables(lp, ls_, bp, bs)
            q, k, v = _inproj_odd_call(xn, z, mods_g[l - 1], mods_g[l], norm_mix[l][None], cos, sin, rope_map, w_all)
            xres, moe_prev = [xn], (z, mods_g[l - 1])
            pb = WINDOW
            tq, g = 512, 2
            sink = sinks_c[i][None] * LOG2E
            sink_spec = [pl.BlockSpec(memory_space=pltpu.SMEM)]
            dupc = lambda c: jnp.concatenate([c[:, :, 0], c[:, :, 0], c[:, :, 1], c[:, :, 1]], axis=-1).reshape(bs * pb, 256).astype(BF16)
            ck, cv = dupc(cache_k_l), dupc(cache_v_l)
            additive = lambda valid: jnp.asarray(np.where(valid, 0.0, -np.inf), F32)
            masks = (additive(_band_valid(g, pb)), additive(_band_valid(g, pb, tq // (CHUNK * g))),
                     additive(_band_valid(1, pb)))
            o = _attention(_swa_kernel, q, k, v, ck, cv, masks, [sink], sink_spec, width=1024, kv_width=256, pb=pb,
                           tq=tq, g=g, bp=bp, lp=lp, bs=bs, name="swa")
            undup = lambda a: jnp.concatenate([a[:, 0:64], a[:, 128:192]], axis=1).astype(F32)
            tail = lambda a: jnp.stack([undup(a[(b + 1) * lp - pb:(b + 1) * lp]) for b in range(bp)]).reshape(1, bp, pb, KVC, DHC)
            new = lambda a: undup(a[n_p:]).reshape(bs, ls_, KVC, DHC)
            sk_p, sv_p = tail(k), tail(v)
            sk_s = jnp.concatenate([cache_swa_k[i][:, ls_:], new(k)], axis=1)[None]
            sv_s = jnp.concatenate([cache_swa_v[i][:, ls_:], new(v)], axis=1)[None]
            os_, ws = [o], [w_out_l.astype(BF16)]
        xn, disp, meta, counts = _outproj_call(xres, os_, ws, mods_g[l], norm_ffn[l][None], wr, br, n_pad, moe_prev)
        z, tok = _moe_layer(disp, meta, counts, w_gate_up, w_down, l, n, n_pad, sort_rows)

    y_prompt, y_sample = _final_call(xn, z, mods_g[DEPTH - 1], norm_final[None], n_p)
    return (y_prompt.reshape(bp, lp, d), y_sample.reshape(bs, ls_, d),
            gla_p, gla_s, bk_p, bv_p, bk_s, bv_s, sk_p, sv_p, sk_s, sv_s)
```

```python
import functools

import numpy as np
import jax
import jax.numpy as jnp
from jax import lax
from jax.experimental import pallas as pl
from jax.experimental.pallas import tpu as pltpu
from jax.experimental.pallas import tpu_sc as plsc

F32 = jnp.float32
BF16 = jnp.bfloat16

D_MODEL = 1024
DEPTH = 2
CHUNK = 64
PAST_LEN = 4096
HA, DKA, DVA = 4, 64, 128
GATE_RANK = 16
GATE_TAU = 16.0
HB, DHB = 8, 64
N_PREV_B = 8
MAX_REL = 128
HC, KVC, DHC = 16, 2, 64
WINDOW = 128
ROPE_THETA = 10000.0
N_EXPERTS = 16
N_GROUPS = 4
EXP_PER_GROUP = 4
D_FF = 512
EPS = 1e-6

N_PAIRS = 6
N_BUCKETS = N_GROUPS * N_PAIRS
BUCKET_ROWS = 32
Y_SLABS = 4
DISP_SLABS = Y_SLABS + 1
TMO = 256
MOE_TILES = 2
SC_WINDOW = 128
SC_WORKERS = 32
SC_GROUP = 3

TM = 512
SEQ_ROWS = 16
SUB = 16
LOG2E = 1.4426950408889634
VMEM_LIMIT_MB = 48
VMEM_LIMIT_MOE_MB = 56


def _cparams(sem, vmem_mb=VMEM_LIMIT_MB):
    return pltpu.CompilerParams(dimension_semantics=sem, vmem_limit_bytes=vmem_mb * 1024 * 1024)


def _dot(a, b):
    return jnp.dot(a, b, preferred_element_type=F32)


def _dot_nt(a, b):
    return lax.dot_general(a, b, (((1,), (1,)), ((), ())), preferred_element_type=F32)


def _split(a):
    hi = a.astype(BF16)
    lo = (a - hi.astype(F32)).astype(BF16)
    return hi, lo


def _dot3(a, b):
    ah, al = _split(a)
    bh, bl = _split(b)
    return _dot(ah, bh) + _dot(ah, bl) + _dot(al, bh)


def _dot3_narrow(a, b):
    ah, al = _split(a)
    bh, bl = _split(b)
    n = b.shape[1]
    p = _dot(ah, jnp.concatenate([bh, bl], axis=1))
    return p[:, :n] + p[:, n:] + _dot(al, bh)


def _sigmoid(x):
    return 1.0 / (1.0 + jnp.exp(-x))


def _group_affine(y, mul, add):
    parts = []
    for gi in range(y.shape[0] // CHUNK):
        p = y[gi * CHUNK:(gi + 1) * CHUNK]
        if mul is not None:
            p = p * mul[gi:gi + 1]
        if add is not None:
            p = p + add[gi:gi + 1]
        parts.append(p)
    return jnp.concatenate(parts, axis=0)


def _norm_mod(x, g, shift, scale):
    ms = jnp.mean(x * x, axis=-1, keepdims=True)
    return _group_affine(x * lax.rsqrt(ms + EPS) * g, 1.0 + scale, shift)


def _mod_spec(part):
    return pl.BlockSpec((TM // CHUNK, D_MODEL), lambda i: (i, part))


SHIFT_MIX, SCALE_MIX, GATE_MIX, SHIFT_FFN, SCALE_FFN, GATE_FFN = range(6)


def _on_token_tile(xp_ref, xs_ref, n_prompt_tiles, body):
    @pl.when(pl.program_id(0) < n_prompt_tiles)
    def _():
        body(xp_ref)

    @pl.when(pl.program_id(0) >= n_prompt_tiles)
    def _():
        body(xs_ref)


RING = 3


def _ring_block(srcs, buf, sem, n_prompt_tiles=None):
    s = pl.program_id(0)
    n_steps = pl.num_programs(0)
    t = buf.shape[1]

    def copy(src, blk, slot):
        return pltpu.make_async_copy(src.at[pl.ds(pl.multiple_of(blk * t, t), t)], buf.at[slot], sem.at[slot])

    def start(step, slot):
        if len(srcs) == 1:
            copy(srcs[0], step, slot).start()
        elif isinstance(step, int):
            assert step < n_prompt_tiles
            copy(srcs[0], step, slot).start()
        else:
            @pl.when(step < n_prompt_tiles)
            def _():
                copy(srcs[0], step, slot).start()

            @pl.when(step >= n_prompt_tiles)
            def _():
                copy(srcs[1], step - n_prompt_tiles, slot).start()

    @pl.when(s == 0)
    def _():
        for k in range(RING - 1):
            start(k, k)

    ahead = s + (RING - 1)

    @pl.when(ahead < n_steps)
    def _():
        start(ahead, ahead % RING)

    slot = s % RING
    copy(srcs[0], 0, slot).wait()
    return buf.at[slot]


def _ring_scratch(d):
    return [pltpu.VMEM((RING, TM, d), F32), pltpu.SemaphoreType.DMA((RING,))]


def _token_specs(n_prompt_tiles, d):
    return [pl.BlockSpec((TM, d), lambda i: (jnp.minimum(i, n_prompt_tiles - 1), 0)),
            pl.BlockSpec((TM, d), lambda i: (0, 0))]


def _ada_kernel(c_ref, w_ref, b_ref, o_ref):
    c = c_ref[...]
    o_ref[0] = _dot3(c * _sigmoid(c), w_ref[0]) + b_ref[0]


def _ada_call(c16, w_ada, b_ada):
    d = D_MODEL
    tn = 1024
    return pl.pallas_call(
        _ada_kernel,
        grid=(DEPTH, 6 * d // tn),
        in_specs=[pl.BlockSpec((SEQ_ROWS, d), lambda l, j: (0, 0)),
                  pl.BlockSpec((1, d, tn), lambda l, j: (l, 0, j)),
                  pl.BlockSpec((1, 1, tn), lambda l, j: (l, 0, j))],
        out_specs=pl.BlockSpec((1, SEQ_ROWS, tn), lambda l, j: (l, 0, j)),
        out_shape=jax.ShapeDtypeStruct((DEPTH, SEQ_ROWS, 6 * d), F32),
        compiler_params=_cparams(("arbitrary", "arbitrary")),
        name="ada",
    )(c16, w_ada, b_ada.reshape(DEPTH, 1, 6 * d))


def _inproj_even_kernel(xp_ref, xs_ref, sh_ref, sc_ref, g_ref, w_ref, wla_ref, wg_ref, bg_ref,
                        qa_ref, ka_ref, va_ref, ra_ref, qb_ref, kb_ref, vb_ref, ga_ref, *, n_prompt_tiles):
    def body(x_ref):
        t = x_ref.shape[0]
        outs = ((qa_ref, 0, 256, DKA ** -0.5), (ka_ref, 256, 512, None), (va_ref, 512, 1024, None),
                (ra_ref, 1024, 1536, None), (qb_ref, 1536, 2048, DHB ** -0.5 * LOG2E), (kb_ref, 2048, 2560, None),
                (vb_ref, 2560, 3072, None))
        shift, scale_ = sh_ref[...], sc_ref[...]
        halves = [slice(0, t // 2), slice(t // 2, t)]
        grp = [slice(0, t // (2 * CHUNK)), slice(t // (2 * CHUNK), t // CHUNK)]
        hbs = [_norm_mod(x_ref[rs, :], g_ref[...], shift[gs], scale_[gs]).astype(BF16) for rs, gs in zip(halves, grp)]
        for rs, hb in zip(halves, hbs):
            zs = [_dot(hb, w_ref[:, lo:hi]) for _, lo, hi, _ in outs]
            la = _dot(hb, wla_ref[...])
            for z, (o_ref, _, _, scale) in zip(zs, outs):
                o_ref[rs, :] = (z if scale is None else z * scale).astype(BF16)
            gl = _dot3(la, wg_ref[...]) + bg_ref[...]
            ga_ref[rs, :] = -(jnp.maximum(-gl, 0.0) + jnp.log(1.0 + jnp.exp(-jnp.abs(gl)))) * (1.0 / GATE_TAU)

    _on_token_tile(xp_ref, xs_ref, n_prompt_tiles, body)


def _inproj_even_call(xp, xs, mods, g, w_main, w_la, w_gate, b_gate):
    d = xp.shape[1]
    npt = xp.shape[0] // TM
    n = xp.shape[0] + xs.shape[0]
    row = lambda i: (i, 0)
    const = lambda i: (0, 0)
    widths = (256, 256, 512, 512, 512, 512, 512)
    out_shape = [jax.ShapeDtypeStruct((n, w), BF16) for w in widths] + [jax.ShapeDtypeStruct((n, 256), F32)]
    out_specs = [pl.BlockSpec((TM, w), row) for w in widths] + [pl.BlockSpec((TM, 256), row)]
    return pl.pallas_call(
        functools.partial(_inproj_even_kernel, n_prompt_tiles=npt),
        grid=(n // TM,),
        in_specs=_token_specs(npt, d) + [
            _mod_spec(SHIFT_MIX), _mod_spec(SCALE_MIX),
            pl.BlockSpec((1, d), const),
            pl.BlockSpec(w_main.shape, const), pl.BlockSpec(w_la.shape, const),
            pl.BlockSpec(w_gate.shape, const), pl.BlockSpec(b_gate.shape, const)],
        out_specs=out_specs, out_shape=out_shape,
        compiler_params=_cparams(("parallel",)),
        name="inproj_even",
    )(xp, xs, mods, mods, g, w_main, w_la, w_gate, b_gate)


def _rope(x, cos, sin_signed):
    t, w = x.shape
    lane = lax.broadcasted_iota(jnp.int32, (1, w), 1)
    first_half = (lane & 63) < 32
    rot = jnp.where(first_half, pltpu.roll(x, w - 32, 1), pltpu.roll(x, 32, 1))
    reps = w // 128
    return x * jnp.tile(cos, (1, reps)) + rot * jnp.tile(sin_signed, (1, reps))


def _unpack_pairs(slabs, dtype):
    lo = [pltpu.bitcast(s << 16, F32) for s in slabs]
    hi = [pltpu.bitcast(s & jnp.int32(-65536), F32) for s in slabs]
    return jnp.concatenate(lo + hi, axis=1).astype(dtype)


def _pack_pairs(x):
    bits = pltpu.bitcast(x.astype(BF16).astype(F32), jnp.int32)
    half = x.shape[1] // 2
    packed = ((bits[:, :half] >> 16) & jnp.int32(0xFFFF)) | (bits[:, half:] & jnp.int32(-65536))
    return [packed[:, 128 * s:128 * (s + 1)] for s in range(half // 128)]


def _add_moe(xn_ref, z_ref, gate_ref):
    y = _unpack_pairs([z_ref[s] for s in range(z_ref.shape[0])], F32)
    return xn_ref[...] + _group_affine(y, gate_ref[...], None)


def _rope_tables(lp, ls_, bp, bs):
    assert PAST_LEN + ls_ <= lp and lp % 128 == 0 and bs * ls_ == TM
    half = DHC // 2
    inv = ROPE_THETA ** (-jnp.arange(half, dtype=F32) / half)
    inv = jnp.tile(inv, 128 // half)
    sign = jnp.asarray(np.tile(np.repeat([-1.0, 1.0], half), 128 // DHC), F32)
    a = jnp.asarray(np.arange(lp // 128) * 128, F32)[:, None] * inv[None, :]
    b = jnp.asarray(np.arange(128), F32)[:, None] * inv[None, :]
    ca, sa, cb, sb = jnp.cos(a)[:, None], jnp.sin(a)[:, None], jnp.cos(b)[None], jnp.sin(b)[None]
    cos = (ca * cb - sa * sb).reshape(lp, 128)
    sin = ((sa * cb + ca * sb) * sign).reshape(lp, 128)
    with_sample = lambda t: jnp.concatenate([t, jnp.tile(t[PAST_LEN:PAST_LEN + ls_], (bs, 1))], axis=0)
    tiles = lp // TM
    return with_sample(cos), with_sample(sin), lambda i: (jnp.where(i < bp * tiles, i % tiles, tiles), 0)


def _inproj_odd_kernel(xn_hbm, z_ref, gate_ref, sh_ref, sc_ref, g_ref, cos_ref, sin_ref, w_ref,
                       x_ref, q_ref, k_ref, v_ref, xbuf, xsem):
    xn_ref = _ring_block([xn_hbm], xbuf, xsem)
    t = xn_ref.shape[0]
    halves = [slice(0, t // 2), slice(t // 2, t)]
    grp = [slice(0, t // (2 * CHUNK)), slice(t // (2 * CHUNK), t // CHUNK)]
    gate, shift, scale = gate_ref[...], sh_ref[...], sc_ref[...]
    xs = []
    for rs, gs in zip(halves, grp):
        y = _unpack_pairs([z_ref[s, rs, :] for s in range(z_ref.shape[0])], F32)
        xs.append(xn_ref[rs, :] + _group_affine(y, gate[gs], None))
    for rs, x in zip(halves, xs):
        x_ref[rs, :] = x
    hbs = [_norm_mod(x, g_ref[...], shift[gs], scale[gs]).astype(BF16) for x, gs in zip(xs, grp)]
    qs = [_dot(hb, w_ref[:, 0:1024]) for hb in hbs]
    ks = [_dot(hb, w_ref[:, 1024:1280]) for hb in hbs]
    vs = [_dot(hb, w_ref[:, 1280:1536]) for hb in hbs]
    for rs, q, k, v in zip(halves, qs, ks, vs):
        cos, sin = cos_ref[rs, :], sin_ref[rs, :]
        q_ref[rs, :] = (_rope(q, cos, sin) * (DHC ** -0.5 * LOG2E)).astype(BF16)
        k_ref[rs, :] = _rope(k, cos, sin).astype(BF16)
        v_ref[rs, :] = v.astype(BF16)


def _inproj_odd_call(xn, z, mods_prev, mods, g, cos, sin, rope_map, w):
    n, d = xn.shape
    row = lambda i: (i, 0)
    const = lambda i: (0, 0)
    widths = (1024, 256, 256)
    return pl.pallas_call(
        _inproj_odd_kernel,
        grid=(n // TM,),
        in_specs=[pl.BlockSpec(memory_space=pl.ANY), pl.BlockSpec((z.shape[0], TM, 128), lambda i: (0, i, 0)),
                  _mod_spec(GATE_FFN), _mod_spec(SHIFT_MIX), _mod_spec(SCALE_MIX),
                  pl.BlockSpec((1, d), const),
                  pl.BlockSpec((TM, 128), rope_map), pl.BlockSpec((TM, 128), rope_map),
                  pl.BlockSpec(w.shape, const)],
        out_specs=[pl.BlockSpec((TM, d), row)] + [pl.BlockSpec((TM, wd), row) for wd in widths],
        out_shape=[jax.ShapeDtypeStruct((n, d), F32)] + [jax.ShapeDtypeStruct((n, wd), BF16) for wd in widths],
        scratch_shapes=_ring_scratch(d),
        compiler_params=_cparams(("arbitrary",)),
        name="inproj_odd",
    )(xn, z, mods_prev, mods, mods, g, cos, sin, w)


def _gla_tri():
    t = np.arange(CHUNK)[:, None]
    s = np.arange(CHUNK)[None, :]
    cum = s <= t
    start = s < (t // SUB) * SUB
    end = s < (t // SUB + 1) * SUB
    return jnp.asarray(np.concatenate([cum, start, end], axis=0).astype(np.float32), dtype=BF16)


def _gla_kernel(q_ref, k_ref, v_ref, g_ref, r_ref, s0_ref, gn_ref, tri_ref, o_ref, sout_ref, s_ref, *, nb):
    c_ = CHUNK
    nsub = c_ // SUB

    @pl.when(pl.program_id(1) == 0)
    def _():
        s_ref[...] = s0_ref[0]

    tri = tri_ref[...]
    lane = lax.broadcasted_iota(jnp.int32, (1, 128), 1)
    hmask = [jnp.where(lane < DKA, 1.0, 0.0), jnp.where(lane >= DKA, 1.0, 0.0)]
    ti = lax.broadcasted_iota(jnp.int32, (c_, c_), 0)
    si = lax.broadcasted_iota(jnp.int32, (c_, c_), 1)
    rb, cb = ti >> 4, si >> 4
    m_diag = (rb == cb) & (si <= ti)
    m_off = [(cb == j) & (rb > j) for j in range(nsub - 1)]
    hk = HA * DKA
    gn = gn_ref[...]

    chunks = range(nb)
    heads = [(p, hh) for p in range(HA // 2) for hh in range(2)]
    rows = [slice(c * c_, (c + 1) * c_) for c in chunks]
    pair = [slice(128 * p, 128 * (p + 1)) for p in range(HA // 2)]
    css = []
    for c in chunks:
        g_hi, g_lo = _split(g_ref[rows[c], :])
        css.append(_dot(tri, g_hi) + _dot(tri, g_lo))
    lhs1, lhs2, kds, kes, q_inter, klts, dcols = [], [], [], [], [], [], []
    for c in chunks:
        b, rs, re = css[c][0:c_], css[c][c_:2 * c_], css[c][2 * c_:3 * c_]
        q = q_ref[rows[c], :].astype(F32)
        k = k_ref[rows[c], :].astype(F32)
        bl = b[c_ - 1:c_, :]
        qd = q * jnp.exp(b - rs)
        kd = k * jnp.exp(rs - b)
        ke = k * jnp.exp(re - b)
        qi = q * jnp.exp(b)
        kl = k * jnp.exp(bl - b)
        ql = [q * jnp.exp(jnp.minimum(b - b[SUB * (j + 1) - 1:SUB * (j + 1), :], 0.0)) for j in range(nsub - 1)]
        dcols.append(jnp.broadcast_to(jnp.exp(bl), (8, hk)).T[:, 0:1])
        kds.append([(kd[:, pair[p]] * hmask[hh]).astype(BF16) for p, hh in heads])
        kes.append([(ke[:, pair[p]] * hmask[hh]).astype(BF16) for p, hh in heads])
        klts.append([kl[:, ls].T.astype(BF16) for ls in pair])
        lhs1.append([qd[:, ls].astype(BF16) for ls in pair])
        lhs2.append([jnp.concatenate([ql[j][:, ls] for j in range(nsub - 1)], axis=0).astype(BF16) for ls in pair])
        q_inter.append([(qi[:, pair[p]] * hmask[hh]).astype(BF16) for p, hh in heads])
    a1s = [[_dot_nt(lhs1[c][p], kds[c][h]) for h, (p, hh) in enumerate(heads)] for c in chunks]
    a2s = [[_dot_nt(lhs2[c][p], kes[c][h]) for h, (p, hh) in enumerate(heads)] for c in chunks]
    atts = []
    for c in chunks:
        per_head = []
        for h in range(HA):
            att = jnp.zeros((c_, c_), F32)
            for j in reversed(range(nsub - 1)):
                att = jnp.where(m_off[j], a2s[c][h][j * c_:(j + 1) * c_], att)
            per_head.append(jnp.where(m_diag, a1s[c][h], att).astype(BF16))
        atts.append(per_head)
    vs_ = [[v_ref[rows[c], DVA * h:DVA * (h + 1)] for h in range(HA)] for c in chunks]
    o_intra = [[_dot(atts[c][h], vs_[c][h]) for h in range(HA)] for c in chunks]
    upds = [jnp.concatenate([_dot(klts[c][p][DKA * hh:DKA * (hh + 1)], vs_[c][2 * p + hh]) for p, hh in heads], axis=0)
            for c in chunks]

    s_cur = s_ref[...]
    s_in = []
    for c in chunks:
        s_in.append(s_cur.astype(BF16))
        s_cur = dcols[c] * s_cur + upds[c]
    s_ref[...] = s_cur
    sout_ref[0] = s_cur

    for c in chunks:
        for h in range(HA):
            o = o_intra[c][h] + _dot(q_inter[c][h], s_in[c][pair[h // 2], :])
            ms = jnp.mean(o * o, axis=-1, keepdims=True)
            vs = slice(DVA * h, DVA * (h + 1))
            rr = r_ref[rows[c], vs].astype(F32)
            o_ref[rows[c], vs] = (o * lax.rsqrt(ms + EPS) * gn * (rr * _sigmoid(rr))).astype(BF16)


def _gla_call(q, k, v, g, r, s0, gn, o_prev, *, n_seq, seq_rows, row0, nb):
    tq = nb * CHUNK
    steps = seq_rows // tq
    blk0 = row0 // tq
    row = lambda b, j: (blk0 + b * steps + j, 0)
    const = lambda b, j: (0, 0)
    tri = _gla_tri()
    in_specs = [pl.BlockSpec((tq, 256), row), pl.BlockSpec((tq, 256), row), pl.BlockSpec((tq, 512), row),
                pl.BlockSpec((tq, 256), row), pl.BlockSpec((tq, 512), row),
                pl.BlockSpec((1, 256, 128), lambda b, j: (b, 0, 0)),
                pl.BlockSpec((1, 128), const), pl.BlockSpec(tri.shape, const)]
    args = [q, k, v, g, r, s0, gn, tri]
    aliases = {}
    if o_prev is not None:
        in_specs.append(pl.BlockSpec(memory_space=pl.ANY))
        args.append(o_prev)
        aliases = {len(args) - 1: 0}
    kern = functools.partial(_gla_kernel, nb=nb)
    if o_prev is not None:
        kern = _drop_arg(kern, 8)
    return pl.pallas_call(
        kern,
        grid=(n_seq, steps),
        in_specs=in_specs,
        out_specs=[pl.BlockSpec((tq, 512), row), pl.BlockSpec((1, 256, 128), lambda b, j: (b, 0, 0))],
        out_shape=[jax.ShapeDtypeStruct((q.shape[0], 512), BF16), jax.ShapeDtypeStruct((n_seq, 256, 128), F32)],
        scratch_shapes=[pltpu.VMEM((256, 128), F32)],
        input_output_aliases=aliases,
        compiler_params=_cparams(("arbitrary", "arbitrary")),
        name="gla",
    )(*args)


def _drop_arg(fn, idx):
    def wrapped(*refs):
        return fn(*refs[:idx], *refs[idx + 1:])
    return wrapped


def _window(prev_ref, cur_ref, lo, hi, pb, ls):
    if lo < pb:
        return jnp.concatenate([prev_ref[lo:pb, ls], cur_ref[0:hi - pb, ls]], axis=0)
    return cur_ref[lo - pb:hi - pb, ls]


def _band_kernel(q_ref, kp_ref, kc_ref, vp_ref, vc_ref, bias_ref, o_ref, *, g, n_sub, pb):
    qs = CHUNK * g
    kw_rows = pb + qs
    lane = lax.broadcasted_iota(jnp.int32, (1, 128), 1)
    low = lane < DHB
    hmask = [jnp.where(low, 1.0, 0.0), jnp.where(low, 0.0, 1.0)]
    for s in range(n_sub):
        sb = s if bias_ref.shape[0] > 1 else 0
        rows = slice(qs * s, qs * (s + 1))
        lanes = [slice(128 * p, 128 * (p + 1)) for p in range(HB // 2)]
        heads = [(p, hh) for p in range(HB // 2) for hh in range(2)]
        qps = [q_ref[rows, ls].astype(F32) for ls in lanes]
        kws = [_window(kp_ref, kc_ref, qs * s, qs * s + kw_rows, pb, ls) for ls in lanes]
        vws = [_window(vp_ref, vc_ref, qs * s, qs * s + kw_rows, pb, ls) for ls in lanes]
        qq = [jnp.concatenate([(qps[p] * hmask[hh]).astype(BF16) for hh in range(2)], axis=0) for p in range(HB // 2)]
        sc2 = [_dot_nt(qq[p], kws[p]) for p in range(HB // 2)]
        scs = [sc2[p][qs * hh:qs * (hh + 1)] + bias_ref[sb, 2 * p + hh] for p, hh in heads]
        pes = [jnp.exp2(sc - jnp.max(sc, axis=-1, keepdims=True)) for sc in scs]
        pp = [jnp.concatenate([pes[2 * p + hh].astype(BF16) for hh in range(2)], axis=0) for p in range(HB // 2)]
        o2 = [_dot(pp[p], vws[p]) for p in range(HB // 2)]
        outs = [o2[p][qs * hh:qs * (hh + 1)] / jnp.sum(pes[2 * p + hh], axis=-1, keepdims=True) for p, hh in heads]
        for p, ls in enumerate(lanes):
            o_ref[rows, ls] = jnp.where(low, outs[2 * p], outs[2 * p + 1]).astype(BF16)


def _band_valid(g, pb, n_sub=None):
    rows, kw = CHUNK * g, pb + CHUNK * g
    r = np.arange(rows)[:, None]
    c = np.arange(kw)[None, :]
    dd = c // CHUNK - r // CHUNK
    band = (dd >= 0) & (dd <= pb // CHUNK)
    if n_sub is None:
        return band[None]
    return np.stack([band & (c >= pb - rows * s) for s in range(n_sub)])


def _band_bias(table, g, pb, valid):
    rows, kw = CHUNK * g, pb + CHUNK * g
    period = kw + rows
    m = np.arange(period)
    m = np.where(m < kw, m, m - period)
    ext = table[:, np.clip(m - pb, -MAX_REL, MAX_REL) + MAX_REL] * LOG2E
    flat = jnp.tile(ext, (1, rows))[:, :rows * (period - 1)]
    bias = flat.reshape(table.shape[0], rows, period - 1)[:, :, :kw]
    return jnp.where(valid[:, None], bias[None], -jnp.inf)


def _attn_call(kernel, q, kp, kc, vp, vc, extra, extra_specs, o_prev, *, width, kv_width, tq, pb,
               n_blocks, blk_map, prev_map, name):
    row = lambda i: (blk_map(i), 0)
    prev = lambda i: (prev_map(i), 0)
    in_specs = [pl.BlockSpec((tq, width), row),
                pl.BlockSpec((pb, kv_width), prev), pl.BlockSpec((tq, kv_width), row),
                pl.BlockSpec((pb, kv_width), prev), pl.BlockSpec((tq, kv_width), row)] + extra_specs
    args = [q, kp, kc, vp, vc] + extra
    aliases = {}
    if o_prev is not None:
        in_specs.append(pl.BlockSpec(memory_space=pl.ANY))
        args.append(o_prev)
        aliases = {len(args) - 1: 0}
        kernel = _drop_arg(kernel, len(args) - 1)
    return pl.pallas_call(
        kernel,
        grid=(n_blocks,),
        in_specs=in_specs,
        out_specs=pl.BlockSpec((tq, width), row),
        out_shape=jax.ShapeDtypeStruct((q.shape[0], width), BF16),
        input_output_aliases=aliases,
        compiler_params=_cparams(("parallel",)),
        name=name,
    )(*args)


def _attention(kernel_fn, q, k, v, cache_k, cache_v, masks, extra, extra_specs, *, width, kv_width, pb, tq, g,
               bp, lp, bs, name):
    bps = lp // tq
    n_sub = tq // (CHUNK * g)
    spec = lambda a: [pl.BlockSpec(a.shape, lambda i: (0,) * a.ndim)]
    kern = functools.partial(kernel_fn, g=g, n_sub=n_sub, pb=pb)
    common = dict(width=width, kv_width=kv_width, pb=pb)
    main = lambda i: (i // (bps - 1)) * bps + i % (bps - 1) + 1
    o = _attn_call(kern, q, k, k, v, v, [masks[0]] + extra, spec(masks[0]) + extra_specs, None, tq=tq,
                   n_blocks=bp * (bps - 1), blk_map=main, prev_map=lambda i: main(i) * (tq // pb) - 1,
                   name=name + "_main", **common)
    first = lambda i: i * bps
    o = _attn_call(kern, q, k, k, v, v, [masks[1]] + extra, spec(masks[1]) + extra_specs, o, tq=tq,
                   n_blocks=bp, blk_map=first, prev_map=lambda i: jnp.maximum(first(i) * (tq // pb) - 1, 0),
                   name=name + "_first", **common)
    samp = functools.partial(kernel_fn, g=1, n_sub=1, pb=pb)
    return _attn_call(samp, q, cache_k, k, cache_v, v, [masks[2]] + extra, spec(masks[2]) + extra_specs, o, tq=CHUNK,
                      n_blocks=bs, blk_map=lambda i: bp * lp // CHUNK + i, prev_map=lambda i: i,
                      name=name + "_sample", **common)


def _swa_kernel(q_ref, kp_ref, kc_ref, vp_ref, vc_ref, mask_ref, sink_ref, o_ref, *, g, n_sub, pb):
    qs = CHUNK * g
    kw_rows = pb + qs
    lane = lax.broadcasted_iota(jnp.int32, (1, 128), 1)
    low = lane < DHC
    hmask = [jnp.where(low, 1.0, 0.0), jnp.where(low, 0.0, 1.0)]
    pairs_per_kv = HC // KVC // 2
    for s in range(n_sub):
        msk = mask_ref[s if mask_ref.shape[0] > 1 else 0]
        rows = slice(qs * s, qs * (s + 1))
        kws = [_window(kp_ref, kc_ref, qs * s, qs * s + kw_rows, pb, slice(128 * kv, 128 * (kv + 1))) for kv in range(KVC)]
        vws = [_window(vp_ref, vc_ref, qs * s, qs * s + kw_rows, pb, slice(128 * kv, 128 * (kv + 1))) for kv in range(KVC)]
        heads = [(j, hh) for j in range(HC // 2) for hh in range(2)]
        qps = [q_ref[rows, 128 * j:128 * (j + 1)].astype(F32) for j in range(HC // 2)]
        per_kv = 2 * pairs_per_kv
        qq = [jnp.concatenate([(qps[j] * hmask[hh]).astype(BF16) for j, hh in heads[per_kv * kv:per_kv * (kv + 1)]],
                              axis=0) for kv in range(KVC)]
        sc2 = [_dot_nt(qq[kv], kws[kv]) for kv in range(KVC)]
        scs = [sc2[u // per_kv][qs * (u % per_kv):qs * (u % per_kv + 1)] + msk for u in range(len(heads))]
        sks = [sink_ref[0, 2 * j + hh] for j, hh in heads]
        ms = [jnp.maximum(jnp.max(sc, axis=-1, keepdims=True), sk) for sc, sk in zip(scs, sks)]
        pes = [jnp.exp2(sc - m) for sc, m in zip(scs, ms)]
        pp = [jnp.concatenate([pe.astype(BF16) for pe in pes[per_kv * kv:per_kv * (kv + 1)]], axis=0) for kv in range(KVC)]
        o2 = [_dot(pp[kv], vws[kv]) for kv in range(KVC)]
        outs = [o2[u // per_kv][qs * (u % per_kv):qs * (u % per_kv + 1)]
                / (jnp.sum(pes[u], axis=-1, keepdims=True) + jnp.exp2(sks[u] - ms[u])) for u in range(len(heads))]
        for j in range(HC // 2):
            o_ref[rows, 128 * j:128 * (j + 1)] = jnp.where(low, outs[2 * j], outs[2 * j + 1]).astype(BF16)


def _route(logits_t):
    a = [logits_t[4 * j:4 * j + 4] for j in range(EXP_PER_GROUP)]

    def first_argmax(vals, m):
        idx = jnp.full(m.shape, float(len(vals) - 1), F32)
        for j in reversed(range(len(vals) - 1)):
            idx = jnp.where(vals[j] == m, float(j), idx)
        return idx

    m1 = functools.reduce(jnp.maximum, a)
    i1 = first_argmax(a, m1)
    bsec = [jnp.where(i1 == float(j), -jnp.inf, a[j]) for j in range(EXP_PER_GROUP)]
    m2 = functools.reduce(jnp.maximum, bsec)
    i2 = first_argmax(bsec, m2)
    rows = lambda x: [x[gi:gi + 1] for gi in range(N_GROUPS)]
    gm = functools.reduce(jnp.maximum, rows(m1))
    gscore = jnp.exp(m1 - gm) + jnp.exp(m2 - gm)
    gs = rows(gscore)
    gsel = first_argmax(gs, functools.reduce(jnp.maximum, gs))

    def pick(x):
        xr = rows(x)
        out = xr[N_GROUPS - 1]
        for gi in reversed(range(N_GROUPS - 1)):
            out = jnp.where(gsel == float(gi), xr[gi], out)
        return out

    p1 = jnp.exp(pick(m1) - gm)
    p2 = jnp.exp(pick(m2) - gm)
    w1 = p1 / (p1 + p2)
    w2 = p2 / (p1 + p2)
    s1, s2 = pick(i1), pick(i2)
    lo, hi = jnp.minimum(s1, s2), jnp.maximum(s1, s2)
    pair = jnp.where(lo == 0.0, hi - 1.0, jnp.where(lo == 1.0, hi + 1.0, 5.0))
    bucket = gsel * float(N_PAIRS) + pair
    first_is_lo = s1 < s2
    return bucket, jnp.where(first_is_lo, w1, w2), jnp.where(first_is_lo, w2, w1)


def _outproj_kernel(*refs, n_x, n_o, n_prompt_tiles):
    x_refs = refs[:n_x]
    o_refs = refs[n_x:n_x + n_o]
    w_refs = refs[n_x + n_o:n_x + 2 * n_o]
    (gate_ref, nf_ref, sh_ref, sc_ref, wr_ref, br_ref, tri_ref,
     xn_ref, disp_ref, meta_ref, cnt_ref, run_ref, xbuf, xsem) = refs[n_x + 2 * n_o:]
    t = xn_ref.shape[0]

    @pl.when(pl.program_id(0) == 0)
    def _():
        run_ref[...] = jnp.zeros_like(run_ref)

    x_src = _ring_block(list(x_refs), xbuf, xsem, n_prompt_tiles)

    halves = [slice(0, t // 2), slice(t // 2, t)]
    grp = [slice(0, t // (2 * CHUNK)), slice(t // (2 * CHUNK), t // CHUNK)]
    ys = []
    for rs in halves:
        y = _dot(o_refs[0][rs, :], w_refs[0][...])
        for i in range(1, n_o):
            y = y + _dot(o_refs[i][rs, :], w_refs[i][...])
        ys.append(y)
    gate, shift, scale = gate_ref[...], sh_ref[...], sc_ref[...]
    gys = [_group_affine(y, gate[gs], None) for y, gs in zip(ys, grp)]

    for rs, gy in zip(halves, gys):
        xn_ref[rs, :] = x_src[rs, :] + gy
    hs = [_norm_mod(xn_ref[rs, :], nf_ref[...], shift[gs], scale[gs]) for rs, gs in zip(halves, grp)]
    for rs, h in zip(halves, hs):
        for s, slab in enumerate(_pack_pairs(h)):
            disp_ref[s, rs, :] = slab
    logits_t = [(_dot3_narrow(h, wr_ref[...]) + br_ref[...]).T[0:N_EXPERTS] for h in hs]
    bucket, w_lo, w_hi = _route(jnp.concatenate(logits_t, axis=1))
    r128 = lax.broadcasted_iota(jnp.int32, (128, t), 0)
    tok = (pl.program_id(0) * t + lax.broadcasted_iota(jnp.int32, (1, t), 1)).astype(F32)
    aux = jnp.where(r128 == 0, w_lo, jnp.where(r128 == 1, w_hi, jnp.where(r128 == 2, tok, 0.0))).T
    disp_ref[disp_ref.shape[0] - 1] = pltpu.bitcast(aux, jnp.int32)
    brow = lax.broadcasted_iota(jnp.int32, (BUCKET_ROWS, t), 0).astype(F32)
    onehot = jnp.where(brow == bucket, 1.0, 0.0)
    before = _dot(onehot.astype(BF16), tri_ref[...]) + run_ref[:, 0:1]
    rank = jnp.sum(onehot * before, axis=0, keepdims=True)
    run_ref[...] = run_ref[...] + jnp.sum(onehot, axis=1, keepdims=True)
    cnt_ref[...] = run_ref[...]
    r8 = lax.broadcasted_iota(jnp.int32, (8, t), 0)
    meta_ref[...] = jnp.where(r8 == 0, bucket, jnp.where(r8 == 1, rank, 0.0)).astype(jnp.int32)


def _outproj_call(xs_, os_, ws, mods, nf, wr, br, n_pad):
    d = xs_[0].shape[1]
    n = sum(a.shape[0] for a in xs_)
    npt = xs_[0].shape[0] // TM
    row = lambda i: (i, 0)
    const = lambda i: (0, 0)
    n_o = len(os_)
    in_specs = ([pl.BlockSpec(memory_space=pl.ANY) for _ in xs_]
                + [pl.BlockSpec((TM, o.shape[1]), row) for o in os_]
                + [pl.BlockSpec(w.shape, const) for w in ws]
                + [_mod_spec(GATE_MIX), pl.BlockSpec((1, d), const),
                   _mod_spec(SHIFT_FFN), _mod_spec(SCALE_FFN),
                   pl.BlockSpec(wr.shape, const), pl.BlockSpec(br.shape, const),
                   pl.BlockSpec((TM, TM), const)])
    tri = jnp.asarray(np.triu(np.ones((TM, TM), np.float32), k=1), dtype=BF16)
    return pl.pallas_call(
        functools.partial(_outproj_kernel, n_x=len(xs_), n_o=n_o, n_prompt_tiles=npt),
        grid=(n // TM,),
        in_specs=in_specs,
        out_specs=[pl.BlockSpec((TM, d), row), pl.BlockSpec((DISP_SLABS, TM, 128), lambda i: (0, i, 0)),
                   pl.BlockSpec((8, TM), lambda i: (0, i)), pl.BlockSpec((BUCKET_ROWS, 128), const)],
        out_shape=[jax.ShapeDtypeStruct((n, d), F32), jax.ShapeDtypeStruct((DISP_SLABS, n_pad, 128), jnp.int32),
                   jax.ShapeDtypeStruct((8, n), jnp.int32), jax.ShapeDtypeStruct((BUCKET_ROWS, 128), F32)],
        scratch_shapes=[pltpu.VMEM((BUCKET_ROWS, 128), F32)] + _ring_scratch(d),
        compiler_params=_cparams(("arbitrary",)),
        name="outproj_router",
    )(*xs_, *os_, *ws, mods, nf, mods, mods, wr, br, tri)


def _sc_mesh():
    return plsc.VectorSubcoreMesh(core_axis_name="core", subcore_axis_name="subcore")


def _sc_scatter_rows(src, idx, n_out):
    r = idx.shape[0]
    k = SC_GROUP
    w_per = r // (SC_WINDOW * SC_WORKERS)
    assert idx.shape == (src.shape[0],) and r % (SC_WINDOW * SC_WORKERS) == 0 and w_per % k == 0
    n_groups = w_per // k

    @functools.partial(
        pl.kernel, out_type=jax.ShapeDtypeStruct((n_out, 128), src.dtype), mesh=_sc_mesh(),
        scratch_types=[pltpu.VMEM((w_per, SC_WINDOW), jnp.int32),
                       pltpu.VMEM((2 * k, SC_WINDOW, 128), src.dtype),
                       pltpu.SemaphoreType.DMA((2,)), pltpu.SemaphoreType.DMA((2,))])
    def copy(x_hbm, i_hbm, o_hbm, ibuf, xbuf, in_sem, out_sem):
        wid = lax.axis_index("core") * (SC_WORKERS // 2) + lax.axis_index("subcore")
        pltpu.sync_copy(i_hbm.at[wid], ibuf)
        first = wid * w_per

        def start_in(g, slot):
            return [pltpu.async_copy(x_hbm.at[pl.ds((first + g * k + c) * SC_WINDOW, SC_WINDOW)],
                                     xbuf.at[slot * k + c], in_sem.at[slot]) for c in range(k)]

        def start_out(g, slot):
            return [pltpu.async_copy(xbuf.at[slot * k + c], o_hbm.at[ibuf.at[g * k + c]], out_sem.at[slot])
                    for c in range(k)]

        pending_in = start_in(0, 0)
        for g in range(n_groups):
            slot = g % 2
            for cp in pending_in:
                cp.wait()
            pending_out = start_out(g, slot)
            if g + 1 < n_groups:
                pending_in = start_in(g + 1, 1 - slot)
            for cp in pending_out:
                cp.wait()

    return copy(src, idx.reshape(SC_WORKERS, w_per, SC_WINDOW))


def _moe_kernel(elo_ref, ehi_ref, nvalid_ref, xs_ref, *refs, n_tok, dump_tiles):
    w_refs, (y_ref, tok_ref) = refs[:2 * MOE_TILES], refs[2 * MOE_TILES:]
    d_in = w_refs[0].shape[2] - D_FF
    step = pl.program_id(0)
    t = TMO
    tiles = range(MOE_TILES)
    rows = [slice(t * j, t * (j + 1)) for j in tiles]
    auxs = [pltpu.bitcast(xs_ref[Y_SLABS, rows[j], :], F32) for j in tiles]
    r = lax.broadcasted_iota(jnp.int32, (1, t), 1)
    for j in tiles:
        i = step * MOE_TILES + j
        spare = n_tok + (i % dump_tiles) * t + r
        tok = jnp.where(r < nvalid_ref[i], auxs[j].T[2:3, :].astype(jnp.int32), spare)
        for c in range(t // 128):
            tok_ref[j, c:c + 1, :] = tok[:, 128 * c:128 * (c + 1)]

    any_tokens = nvalid_ref[step * MOE_TILES] > 0
    for j in range(1, MOE_TILES):
        any_tokens = jnp.logical_or(any_tokens, nvalid_ref[step * MOE_TILES + j] > 0)

    @pl.when(any_tokens)
    def _():
        units = [(j, e) for j in tiles for e in range(2)]
        hs = [_unpack_pairs([xs_ref[s, rows[j], :] for s in range(Y_SLABS)], BF16) for j in tiles]
        abs_ = [_dot(hs[j], w_refs[2 * j + e][0, 0, 0:d_in, :]) for j, e in units]
        acts = [(ab[:, :D_FF] * _sigmoid(ab[:, :D_FF]) * ab[:, D_FF:]).astype(BF16) for ab in abs_]
        ys = [_dot(act, w_refs[2 * j + e][0, 0, d_in:d_in + D_FF, :]) for act, (j, e) in zip(acts, units)]
        for j in tiles:
            acc = auxs[j][:, 0:1] * ys[2 * j] + auxs[j][:, 1:2] * ys[2 * j + 1]
            for s, slab in enumerate(_pack_pairs(acc)):
                y_ref[s, rows[j], :] = slab

    @pl.when(jnp.logical_not(any_tokens))
    def _():
        y_ref[...] = jnp.zeros_like(y_ref)


def _moe_call(xs, elo, ehi, nvalid, wexp, n_tiles, n_tok, dump_tiles):
    m = MOE_TILES
    assert n_tiles % m == 0 and wexp.shape[3] == 2 * D_FF
    weight_specs = [pl.BlockSpec((1, 1) + wexp.shape[2:],
                                 lambda i, lo, hi, v, j=j, sel=sel: (0, (lo, hi)[sel][m * i + j], 0, 0))
                    for j in range(m) for sel in range(2)]
    weights = [wexp] * (2 * m)
    return pl.pallas_call(
        functools.partial(_moe_kernel, n_tok=n_tok, dump_tiles=dump_tiles),
        grid_spec=pltpu.PrefetchScalarGridSpec(
            num_scalar_prefetch=3,
            grid=(n_tiles // m,),
            in_specs=[pl.BlockSpec((DISP_SLABS, m * TMO, 128), lambda i, lo, hi, v: (0, i, 0))] + weight_specs,
            out_specs=[pl.BlockSpec((Y_SLABS, m * TMO, 128), lambda i, lo, hi, v: (0, i, 0)),
                       pl.BlockSpec((m, TMO // 128, 128), lambda i, lo, hi, v: (i, 0, 0))]),
        out_shape=[jax.ShapeDtypeStruct((Y_SLABS, n_tiles * TMO, 128), jnp.int32),
                   jax.ShapeDtypeStruct((n_tiles, TMO // 128, 128), jnp.int32)],
        compiler_params=_cparams(("arbitrary",), vmem_mb=VMEM_LIMIT_MOE_MB),
        name="moe_grouped",
    )(elo, ehi, nvalid, xs, *weights)


def _after(x, token):
    return lax.optimization_barrier((x, token))[0]


def _cast_kernel(after_ref, wgu_ref, wdn_ref, o_ref):
    k = wgu_ref.shape[2]
    o_ref[0, 0, 0:k, :] = wgu_ref[0, 0].astype(o_ref.dtype)
    o_ref[0, 0, k:, :] = wdn_ref[0, 0].astype(o_ref.dtype)


def _cast_call(w_gate_up, w_down, layer, after):
    _, e, k, n = w_gate_up.shape
    kd = w_down.shape[2]
    assert w_down.shape[3] == n
    return pl.pallas_call(
        _cast_kernel,
        grid=(e,),
        in_specs=[pl.BlockSpec(memory_space=pl.ANY),
                  pl.BlockSpec((1, 1, k, n), lambda i: (layer, i, 0, 0)),
                  pl.BlockSpec((1, 1, kd, n), lambda i: (layer, i, 0, 0))],
        out_specs=pl.BlockSpec((1, 1, k + kd, n), lambda i: (0, i, 0, 0)),
        out_shape=jax.ShapeDtypeStruct((1, e, k + kd, n), BF16),
        compiler_params=_cparams(("parallel",)),
        name="cast_weights",
    )(after, w_gate_up, w_down)


def _moe_layer(disp, meta, counts, w_gate_up, w_down, layer, n, n_pad, sort_rows):
    n_tiles = sort_rows // TMO
    wexp = _cast_call(w_gate_up, w_down, layer, counts)
    cnt = counts[:N_BUCKETS, 0].astype(jnp.int32)
    padded = ((cnt + TMO - 1) // TMO) * TMO
    ends = jnp.cumsum(padded)
    offs = ends - padded
    bucket, rank = meta[0], meta[1]
    pos = rank + jnp.sum(jnp.where(bucket[None, :] == jnp.arange(N_BUCKETS, dtype=jnp.int32)[:, None],
                                   offs[:, None], 0), axis=0)
    tile_start = jnp.arange(n_tiles, dtype=jnp.int32) * TMO
    tile_bucket = jnp.minimum(jnp.sum((tile_start[:, None] >= ends[None, :]).astype(jnp.int32), axis=1), N_BUCKETS - 1)
    pair_lo = np.array([0, 0, 0, 1, 1, 2], np.int32)
    pair_hi = np.array([1, 2, 3, 2, 3, 3], np.int32)
    b_lo = jnp.asarray(np.repeat(np.arange(N_GROUPS), N_PAIRS) * EXP_PER_GROUP + np.tile(pair_lo, N_GROUPS), jnp.int32)
    b_hi = jnp.asarray(np.repeat(np.arange(N_GROUPS), N_PAIRS) * EXP_PER_GROUP + np.tile(pair_hi, N_GROUPS), jnp.int32)
    onehot_tb = (tile_bucket[:, None] == jnp.arange(N_BUCKETS, dtype=jnp.int32)[None, :]).astype(jnp.int32)
    elo = jnp.sum(onehot_tb * b_lo[None, :], axis=1)
    ehi = jnp.sum(onehot_tb * b_hi[None, :], axis=1)
    bucket_end = jnp.sum(onehot_tb * (offs + cnt)[None, :], axis=1)
    nvalid = jnp.where(tile_start < ends[-1], jnp.clip(bucket_end - tile_start, 0, TMO), 0)
    dump = sort_rows + jnp.arange(n_pad - n, dtype=jnp.int32)
    pos_sc = jnp.concatenate([pos, dump])
    total = sort_rows + n_pad - n
    sc_idx = (pos_sc[None, :] + (jnp.arange(DISP_SLABS, dtype=jnp.int32) * total)[:, None]).reshape(-1)
    xs = _sc_scatter_rows(disp.reshape(DISP_SLABS * n_pad, 128), sc_idx, DISP_SLABS * total)
    ys, tok = _moe_call(xs.reshape(DISP_SLABS, total, 128), elo, ehi, nvalid, wexp, n_tiles,
                        n, (n_pad - n) // TMO)
    back_idx = (tok.reshape(1, sort_rows) + (jnp.arange(Y_SLABS, dtype=jnp.int32) * n_pad)[:, None]).reshape(-1)
    z = _sc_scatter_rows(ys.reshape(Y_SLABS * sort_rows, 128), back_idx, Y_SLABS * n_pad)
    return z.reshape(Y_SLABS, n_pad, 128), tok


def _final_kernel(xn_hbm, z_ref, gate_ref, g_ref, yp_ref, ys_ref, xbuf, xsem, *, n_prompt_tiles):
    x = _add_moe(_ring_block([xn_hbm], xbuf, xsem), z_ref, gate_ref)
    ms = jnp.mean(x * x, axis=-1, keepdims=True)
    y = x * lax.rsqrt(ms + EPS) * g_ref[...]
    i = pl.program_id(0)

    @pl.when(i < n_prompt_tiles)
    def _():
        yp_ref[...] = y

    @pl.when(i >= n_prompt_tiles)
    def _():
        ys_ref[...] = y


def _final_call(xn, z, mods, g, n_prompt):
    n, d = xn.shape
    npt = n_prompt // TM
    assert n - n_prompt == TM
    return pl.pallas_call(
        functools.partial(_final_kernel, n_prompt_tiles=npt),
        grid=(n // TM,),
        in_specs=[pl.BlockSpec(memory_space=pl.ANY), pl.BlockSpec((z.shape[0], TM, 128), lambda i: (0, i, 0)),
                  _mod_spec(GATE_FFN), pl.BlockSpec((1, d), lambda i: (0, 0))],
        out_specs=_token_specs(npt, d),
        out_shape=[jax.ShapeDtypeStruct((n_prompt, d), F32), jax.ShapeDtypeStruct((TM, d), F32)],
        scratch_shapes=_ring_scratch(d),
        compiler_params=_cparams(("arbitrary",)),
        name="final_norm",
    )(xn, z, mods, g)


def kernel(x_prompt, x_sample, c_prompt, c_sample, state_gla, cache_band_k, cache_band_v, cache_swa_k, cache_swa_v,
           w_ada, b_ada, norm_mix, norm_ffn, norm_final, w_in_even, w_gate_a, b_gate_a, gla_norm, rel_bias_b,
           w_out_even, w_in_odd, sinks_c, w_out_odd, w_router, b_router, w_gate_up, w_down):
    bp, lp, d = x_prompt.shape
    bs, ls_, _ = x_sample.shape
    n_p, n_s = bp * lp, bs * ls_
    n = n_p + n_s
    assert ls_ == CHUNK and n_s == TM and lp % TM == 0 and PAST_LEN % CHUNK == 0

    xp2, xs2 = x_prompt.reshape(n_p, d), x_sample.reshape(n_s, d)

    c16 = jnp.zeros((SEQ_ROWS, d), F32).at[:bp].set(c_prompt).at[bp:bp + bs].set(c_sample)
    mods = _ada_call(c16, w_ada, b_ada)
    seq_of_group = np.concatenate([np.repeat(np.arange(bp), lp // CHUNK), bp + np.arange(bs)])
    mods_g = [mods[l][seq_of_group] for l in range(DEPTH)]

    perm = np.array([4 * (c % 4) + c // 4 for c in range(N_EXPERTS)])
    wr = jnp.zeros((d, 128), F32).at[:, :N_EXPERTS].set(w_router[:, perm])
    br = jnp.zeros((1, 128), F32).at[0, :N_EXPERTS].set(b_router[perm])

    sc_unit = SC_WINDOW * SC_WORKERS * SC_GROUP
    n_pad = n + TMO
    while (DISP_SLABS * n_pad) % sc_unit or (Y_SLABS * n_pad) % TMO or (n_pad - n) % TMO:
        n_pad += TMO
    sort_rows = n + N_BUCKETS * TMO
    while (Y_SLABS * sort_rows) % sc_unit or sort_rows % (MOE_TILES * TMO):
        sort_rows += TMO

    gla_p = gla_s = bk_p = bv_p = bk_s = bv_s = sk_p = sv_p = sk_s = sv_s = None
    xn = z = tok = None
    for l in range(DEPTH):
        i = l // 2
        if l % 2 == 0:
            w = w_in_even[i]
            w_main = jnp.concatenate([w[:, :1536], w[:, 1552:]], axis=1).astype(BF16)
            w_la = jnp.zeros((d, 128), F32).at[:, :GATE_RANK].set(w[:, 1536:1552]).astype(BF16)
            w_gate = jnp.zeros((128, HA * DKA), F32).at[:GATE_RANK].set(w_gate_a[i])
            qa, ka, va, ra, qb, kb, vb, ga = _inproj_even_call(
                xp2, xs2, mods_g[l], norm_mix[l][None], w_main, w_la, w_gate, b_gate_a[i][None])
            xres = [xp2, xs2]
            gn = gla_norm[i][None]
            oa, s_p = _gla_call(qa, ka, va, ga, ra, jnp.zeros((bp, 256, 128), F32), gn, None,
                                n_seq=bp, seq_rows=lp, row0=0, nb=8)
            oa, s_s = _gla_call(qa, ka, va, ga, ra, state_gla[i].reshape(bs, 256, 128), gn, oa,
                                n_seq=bs, seq_rows=ls_, row0=n_p, nb=1)
            gla_p, gla_s = s_p.reshape(1, bp, HA, DKA, DVA), s_s.reshape(1, bs, HA, DKA, DVA)
            pb = N_PREV_B * CHUNK
            tq, g = 512, 2
            ck = cache_band_k[i].reshape(bs * pb, HB * DHB).astype(BF16)
            cv = cache_band_v[i].reshape(bs * pb, HB * DHB).astype(BF16)
            biases = (_band_bias(rel_bias_b[i], g, pb, _band_valid(g, pb)),
                      _band_bias(rel_bias_b[i], g, pb, _band_valid(g, pb, tq // (CHUNK * g))),
                      _band_bias(rel_bias_b[i], 1, pb, _band_valid(1, pb)))
            ob = _attention(_band_kernel, qb, kb, vb, ck, cv, biases, [], [], width=512, kv_width=512, pb=pb,
                            tq=tq, g=g, bp=bp, lp=lp, bs=bs, name="band")
            tail = lambda a: jnp.stack([a[(b + 1) * lp - pb:(b + 1) * lp] for b in range(bp)]).astype(F32).reshape(1, bp, pb, HB, DHB)
            new = lambda a: a[n_p:].astype(F32).reshape(bs, ls_, HB, DHB)
            bk_p, bv_p = tail(kb), tail(vb)
            bk_s = jnp.concatenate([cache_band_k[i][:, ls_:], new(kb)], axis=1)[None]
            bv_s = jnp.concatenate([cache_band_v[i][:, ls_:], new(vb)], axis=1)[None]
            wo = w_out_even[i].astype(BF16)
            os_, ws = [oa, ob], [wo[:HA * DVA], wo[HA * DVA:]]
        else:
            w = _after(w_in_odd[i], tok)
            w_out_l = _after(w_out_odd[i], tok)
            cache_k_l, cache_v_l = _after(cache_swa_k[i], tok), _after(cache_swa_v[i], tok)
            wk, wv = w[:, 1024:1152], w[:, 1152:1280]
            dup = lambda a: jnp.concatenate([a[:, :64], a[:, :64], a[:, 64:], a[:, 64:]], axis=1)
            w_all = jnp.concatenate([w[:, :1024], dup(wk), dup(wv)], axis=1).astype(BF16)
            cos, sin, rope_map = _rope_tables(lp, ls_, bp, bs)
            x, q, k, v = _inproj_odd_call(xn, z, mods_g[l - 1], mods_g[l], norm_mix[l][None], cos, sin, rope_map, w_all)
            xres = [x]
            pb = WINDOW
            tq, g = 512, 2
            sink = sinks_c[i][None] * LOG2E
            sink_spec = [pl.BlockSpec(memory_space=pltpu.SMEM)]
            dupc = lambda c: jnp.concatenate([c[:, :, 0], c[:, :, 0], c[:, :, 1], c[:, :, 1]], axis=-1).reshape(bs * pb, 256).astype(BF16)
            ck, cv = dupc(cache_k_l), dupc(cache_v_l)
            additive = lambda valid: jnp.asarray(np.where(valid, 0.0, -np.inf), F32)
            masks = (additive(_band_valid(g, pb)), additive(_band_valid(g, pb, tq // (CHUNK * g))),
                     additive(_band_valid(1, pb)))
            o = _attention(_swa_kernel, q, k, v, ck, cv, masks, [sink], sink_spec, width=1024, kv_width=256, pb=pb,
                           tq=tq, g=g, bp=bp, lp=lp, bs=bs, name="swa")
            undup = lambda a: jnp.concatenate([a[:, 0:64], a[:, 128:192]], axis=1).astype(F32)
            tail = lambda a: jnp.stack([undup(a[(b + 1) * lp - pb:(b + 1) * lp]) for b in range(bp)]).reshape(1, bp, pb, KVC, DHC)
            new = lambda a: undup(a[n_p:]).reshape(bs, ls_, KVC, DHC)
            sk_p, sv_p = tail(k), tail(v)
            sk_s = jnp.concatenate([cache_swa_k[i][:, ls_:], new(k)], axis=1)[None]
            sv_s = jnp.concatenate([cache_swa_v[i][:, ls_:], new(v)], axis=1)[None]
            os_, ws = [o], [w_out_l.astype(BF16)]
        xn, disp, meta, counts = _outproj_call(xres, os_, ws, mods_g[l], norm_ffn[l][None], wr, br, n_pad)
        z, tok = _moe_layer(disp, meta, counts, w_gate_up, w_down, l, n, n_pad, sort_rows)

    y_prompt, y_sample = _final_call(xn, z, mods_g[DEPTH - 1], norm_final[None], n_p)
    return (y_prompt.reshape(bp, lp, d), y_sample.reshape(bs, ls_, d),
            gla_p, gla_s, bk_p, bv_p, bk_s, bv_s, sk_p, sv_p, sk_s, sv_s)
```

```python
import functools

import numpy as np
import jax
import jax.numpy as jnp
from jax import lax
from jax.experimental import pallas as pl
from jax.experimental.pallas import tpu as pltpu
from jax.experimental.pallas import tpu_sc as plsc

F32 = jnp.float32
BF16 = jnp.bfloat16

D_MODEL = 1024
DEPTH = 2
CHUNK = 64
PAST_LEN = 4096
HA, DKA, DVA = 4, 64, 128
GATE_RANK = 16
GATE_TAU = 16.0
HB, DHB = 8, 64
N_PREV_B = 8
MAX_REL = 128
HC, KVC, DHC = 16, 2, 64
WINDOW = 128
ROPE_THETA = 10000.0
N_EXPERTS = 16
N_GROUPS = 4
EXP_PER_GROUP = 4
D_FF = 512
EPS = 1e-6

N_PAIRS = 6
N_BUCKETS = N_GROUPS * N_PAIRS
BUCKET_ROWS = 32
Y_SLABS = 4
DISP_SLABS = Y_SLABS + 1
TMO = 256
MOE_TILES = 2
SC_WINDOW = 128
SC_WORKERS = 32
SC_GROUP = 3

TM = 512
SEQ_ROWS = 16
SUB = 16
LOG2E = 1.4426950408889634
VMEM_LIMIT_MB = 48
VMEM_LIMIT_MOE_MB = 56


def _cparams(sem, vmem_mb=VMEM_LIMIT_MB):
    return pltpu.CompilerParams(dimension_semantics=sem, vmem_limit_bytes=vmem_mb * 1024 * 1024)


def _dot(a, b):
    return jnp.dot(a, b, preferred_element_type=F32)


def _dot_nt(a, b):
    return lax.dot_general(a, b, (((1,), (1,)), ((), ())), preferred_element_type=F32)


def _split(a):
    hi = a.astype(BF16)
    lo = (a - hi.astype(F32)).astype(BF16)
    return hi, lo


def _dot3(a, b):
    ah, al = _split(a)
    bh, bl = _split(b)
    return _dot(ah, bh) + _dot(ah, bl) + _dot(al, bh)


def _dot3_narrow(a, b):
    ah, al = _split(a)
    bh, bl = _split(b)
    n = b.shape[1]
    p = _dot(ah, jnp.concatenate([bh, bl], axis=1))
    return p[:, :n] + p[:, n:] + _dot(al, bh)


def _sigmoid(x):
    return 1.0 / (1.0 + jnp.exp(-x))


def _group_affine(y, mul, add):
    parts = []
    for gi in range(y.shape[0] // CHUNK):
        p = y[gi * CHUNK:(gi + 1) * CHUNK]
        if mul is not None:
            p = p * mul[gi:gi + 1]
        if add is not None:
            p = p + add[gi:gi + 1]
        parts.append(p)
    return jnp.concatenate(parts, axis=0)


def _norm_mod(x, g, shift, scale):
    ms = jnp.mean(x * x, axis=-1, keepdims=True)
    return _group_affine(x * lax.rsqrt(ms + EPS) * g, 1.0 + scale, shift)


def _mod_spec(part):
    return pl.BlockSpec((TM // CHUNK, D_MODEL), lambda i: (i, part))


SHIFT_MIX, SCALE_MIX, GATE_MIX, SHIFT_FFN, SCALE_FFN, GATE_FFN = range(6)


def _on_token_tile(xp_ref, xs_ref, n_prompt_tiles, body):
    @pl.when(pl.program_id(0) < n_prompt_tiles)
    def _():
        body(xp_ref)

    @pl.when(pl.program_id(0) >= n_prompt_tiles)
    def _():
        body(xs_ref)


RING = 3


def _ring_block(srcs, buf, sem, n_prompt_tiles=None):
    s = pl.program_id(0)
    n_steps = pl.num_programs(0)
    t = buf.shape[1]

    def copy(src, blk, slot):
        return pltpu.make_async_copy(src.at[pl.ds(pl.multiple_of(blk * t, t), t)], buf.at[slot], sem.at[slot])

    def start(step, slot):
        if len(srcs) == 1:
            copy(srcs[0], step, slot).start()
        elif isinstance(step, int):
            assert step < n_prompt_tiles
            copy(srcs[0], step, slot).start()
        else:
            @pl.when(step < n_prompt_tiles)
            def _():
                copy(srcs[0], step, slot).start()

            @pl.when(step >= n_prompt_tiles)
            def _():
                copy(srcs[1], step - n_prompt_tiles, slot).start()

    @pl.when(s == 0)
    def _():
        for k in range(RING - 1):
            start(k, k)

    ahead = s + (RING - 1)

    @pl.when(ahead < n_steps)
    def _():
        start(ahead, ahead % RING)

    slot = s % RING
    copy(srcs[0], 0, slot).wait()
    return buf.at[slot]


def _ring_scratch(d):
    return [pltpu.VMEM((RING, TM, d), F32), pltpu.SemaphoreType.DMA((RING,))]


def _token_specs(n_prompt_tiles, d):
    return [pl.BlockSpec((TM, d), lambda i: (jnp.minimum(i, n_prompt_tiles - 1), 0)),
            pl.BlockSpec((TM, d), lambda i: (0, 0))]


def _ada_kernel(c_ref, w_ref, b_ref, o_ref):
    c = c_ref[...]
    o_ref[0] = _dot3(c * _sigmoid(c), w_ref[0]) + b_ref[0]


def _ada_call(c16, w_ada, b_ada):
    d = D_MODEL
    tn = 1024
    return pl.pallas_call(
        _ada_kernel,
        grid=(DEPTH, 6 * d // tn),
        in_specs=[pl.BlockSpec((SEQ_ROWS, d), lambda l, j: (0, 0)),
                  pl.BlockSpec((1, d, tn), lambda l, j: (l, 0, j)),
                  pl.BlockSpec((1, 1, tn), lambda l, j: (l, 0, j))],
        out_specs=pl.BlockSpec((1, SEQ_ROWS, tn), lambda l, j: (l, 0, j)),
        out_shape=jax.ShapeDtypeStruct((DEPTH, SEQ_ROWS, 6 * d), F32),
        compiler_params=_cparams(("arbitrary", "arbitrary")),
        name="ada",
    )(c16, w_ada, b_ada.reshape(DEPTH, 1, 6 * d))


def _inproj_even_kernel(xp_ref, xs_ref, sh_ref, sc_ref, g_ref, w_ref, wla_ref, wg_ref, bg_ref,
                        qa_ref, ka_ref, va_ref, ra_ref, qb_ref, kb_ref, vb_ref, ga_ref, *, n_prompt_tiles):
    def body(x_ref):
        t = x_ref.shape[0]
        outs = ((qa_ref, 0, 256, DKA ** -0.5), (ka_ref, 256, 512, None), (va_ref, 512, 1024, None),
                (ra_ref, 1024, 1536, None), (qb_ref, 1536, 2048, DHB ** -0.5 * LOG2E), (kb_ref, 2048, 2560, None),
                (vb_ref, 2560, 3072, None))
        shift, scale_ = sh_ref[...], sc_ref[...]
        halves = [slice(0, t // 2), slice(t // 2, t)]
        grp = [slice(0, t // (2 * CHUNK)), slice(t // (2 * CHUNK), t // CHUNK)]
        hbs = [_norm_mod(x_ref[rs, :], g_ref[...], shift[gs], scale_[gs]).astype(BF16) for rs, gs in zip(halves, grp)]
        for rs, hb in zip(halves, hbs):
            zs = [_dot(hb, w_ref[:, lo:hi]) for _, lo, hi, _ in outs]
            la = _dot(hb, wla_ref[...])
            for z, (o_ref, _, _, scale) in zip(zs, outs):
                o_ref[rs, :] = (z if scale is None else z * scale).astype(BF16)
            gl = _dot3(la, wg_ref[...]) + bg_ref[...]
            ga_ref[rs, :] = -(jnp.maximum(-gl, 0.0) + jnp.log(1.0 + jnp.exp(-jnp.abs(gl)))) * (1.0 / GATE_TAU)

    _on_token_tile(xp_ref, xs_ref, n_prompt_tiles, body)


def _inproj_even_call(xp, xs, mods, g, w_main, w_la, w_gate, b_gate):
    d = xp.shape[1]
    npt = xp.shape[0] // TM
    n = xp.shape[0] + xs.shape[0]
    row = lambda i: (i, 0)
    const = lambda i: (0, 0)
    widths = (256, 256, 512, 512, 512, 512, 512)
    out_shape = [jax.ShapeDtypeStruct((n, w), BF16) for w in widths] + [jax.ShapeDtypeStruct((n, 256), F32)]
    out_specs = [pl.BlockSpec((TM, w), row) for w in widths] + [pl.BlockSpec((TM, 256), row)]
    return pl.pallas_call(
        functools.partial(_inproj_even_kernel, n_prompt_tiles=npt),
        grid=(n // TM,),
        in_specs=_token_specs(npt, d) + [
            _mod_spec(SHIFT_MIX), _mod_spec(SCALE_MIX),
            pl.BlockSpec((1, d), const),
            pl.BlockSpec(w_main.shape, const), pl.BlockSpec(w_la.shape, const),
            pl.BlockSpec(w_gate.shape, const), pl.BlockSpec(b_gate.shape, const)],
        out_specs=out_specs, out_shape=out_shape,
        compiler_params=_cparams(("parallel",)),
        name="inproj_even",
    )(xp, xs, mods, mods, g, w_main, w_la, w_gate, b_gate)


def _rope(x, cos, sin_signed):
    t, w = x.shape
    lane = lax.broadcasted_iota(jnp.int32, (1, w), 1)
    first_half = (lane & 63) < 32
    rot = jnp.where(first_half, pltpu.roll(x, w - 32, 1), pltpu.roll(x, 32, 1))
    reps = w // 128
    return x * jnp.tile(cos, (1, reps)) + rot * jnp.tile(sin_signed, (1, reps))


def _unpack_pairs(slabs, dtype):
    lo = [pltpu.bitcast(s << 16, F32) for s in slabs]
    hi = [pltpu.bitcast(s & jnp.int32(-65536), F32) for s in slabs]
    return jnp.concatenate(lo + hi, axis=1).astype(dtype)


def _pack_pairs(x):
    bits = pltpu.bitcast(x.astype(BF16).astype(F32), jnp.int32)
    half = x.shape[1] // 2
    packed = ((bits[:, :half] >> 16) & jnp.int32(0xFFFF)) | (bits[:, half:] & jnp.int32(-65536))
    return [packed[:, 128 * s:128 * (s + 1)] for s in range(half // 128)]


def _add_moe(xn_ref, z_ref, gate_ref):
    y = _unpack_pairs([z_ref[s] for s in range(z_ref.shape[0])], F32)
    return xn_ref[...] + _group_affine(y, gate_ref[...], None)


def _rope_tables(lp, ls_, bp, bs):
    assert PAST_LEN + ls_ <= lp and lp % 128 == 0 and bs * ls_ == TM
    half = DHC // 2
    inv = ROPE_THETA ** (-jnp.arange(half, dtype=F32) / half)
    inv = jnp.tile(inv, 128 // half)
    sign = jnp.asarray(np.tile(np.repeat([-1.0, 1.0], half), 128 // DHC), F32)
    a = jnp.asarray(np.arange(lp // 128) * 128, F32)[:, None] * inv[None, :]
    b = jnp.asarray(np.arange(128), F32)[:, None] * inv[None, :]
    ca, sa, cb, sb = jnp.cos(a)[:, None], jnp.sin(a)[:, None], jnp.cos(b)[None], jnp.sin(b)[None]
    cos = (ca * cb - sa * sb).reshape(lp, 128)
    sin = ((sa * cb + ca * sb) * sign).reshape(lp, 128)
    with_sample = lambda t: jnp.concatenate([t, jnp.tile(t[PAST_LEN:PAST_LEN + ls_], (bs, 1))], axis=0)
    tiles = lp // TM
    return with_sample(cos), with_sample(sin), lambda i: (jnp.where(i < bp * tiles, i % tiles, tiles), 0)


def _inproj_odd_kernel(xn_hbm, z_ref, gate_ref, sh_ref, sc_ref, g_ref, cos_ref, sin_ref, w_ref,
                       q_ref, k_ref, v_ref, xbuf, xsem):
    xn_ref = _ring_block([xn_hbm], xbuf, xsem)
    t = xn_ref.shape[0]
    halves = [slice(0, t // 2), slice(t // 2, t)]
    grp = [slice(0, t // (2 * CHUNK)), slice(t // (2 * CHUNK), t // CHUNK)]
    gate, shift, scale = gate_ref[...], sh_ref[...], sc_ref[...]
    xs = []
    for rs, gs in zip(halves, grp):
        y = _unpack_pairs([z_ref[s, rs, :] for s in range(z_ref.shape[0])], F32)
        xs.append(xn_ref[rs, :] + _group_affine(y, gate[gs], None))
    hbs =[_norm_mod(x, g_ref[...], shift[gs], scale[gs]).astype(BF16) for x, gs in zip(xs, grp)]
    qs = [_dot(hb, w_ref[:, 0:1024]) for hb in hbs]
    ks = [_dot(hb, w_ref[:, 1024:1280]) for hb in hbs]
    vs = [_dot(hb, w_ref[:, 1280:1536]) for hb in hbs]
    for rs, q, k, v in zip(halves, qs, ks, vs):
        cos, sin = cos_ref[rs, :], sin_ref[rs, :]
        q_ref[rs, :] = (_rope(q, cos, sin) * (DHC ** -0.5 * LOG2E)).astype(BF16)
        k_ref[rs, :] = _rope(k, cos, sin).astype(BF16)
        v_ref[rs, :] = v.astype(BF16)


def _inproj_odd_call(xn, z, mods_prev, mods, g, cos, sin, rope_map, w):
    n, d = xn.shape
    row = lambda i: (i, 0)
    const = lambda i: (0, 0)
    widths = (1024, 256, 256)
    return pl.pallas_call(
        _inproj_odd_kernel,
        grid=(n // TM,),
        in_specs=[pl.BlockSpec(memory_space=pl.ANY), pl.BlockSpec((z.shape[0], TM, 128), lambda i: (0, i, 0)),
                  _mod_spec(GATE_FFN), _mod_spec(SHIFT_MIX), _mod_spec(SCALE_MIX),
                  pl.BlockSpec((1, d), const),
                  pl.BlockSpec((TM, 128), rope_map), pl.BlockSpec((TM, 128), rope_map),
                  pl.BlockSpec(w.shape, const)],
        out_specs=[pl.BlockSpec((TM, wd), row) for wd in widths],
        out_shape=[jax.ShapeDtypeStruct((n, wd), BF16) for wd in widths],
        scratch_shapes=_ring_scratch(d),
        compiler_params=_cparams(("arbitrary",)),
        name="inproj_odd",
    )(xn, z, mods_prev, mods, mods, g, cos, sin, w)


def _gla_tri():
    t = np.arange(CHUNK)[:, None]
    s = np.arange(CHUNK)[None, :]
    cum = s <= t
    start = s < (t // SUB) * SUB
    end = s < (t // SUB + 1) * SUB
    return jnp.asarray(np.concatenate([cum, start, end], axis=0).astype(np.float32), dtype=BF16)


def _gla_kernel(q_ref, k_ref, v_ref, g_ref, r_ref, s0_ref, gn_ref, tri_ref, o_ref, sout_ref, s_ref, *, nb):
    c_ = CHUNK
    nsub = c_ // SUB

    @pl.when(pl.program_id(1) == 0)
    def _():
        s_ref[...] = s0_ref[0]

    tri = tri_ref[...]
    lane = lax.broadcasted_iota(jnp.int32, (1, 128), 1)
    hmask = [jnp.where(lane < DKA, 1.0, 0.0), jnp.where(lane >= DKA, 1.0, 0.0)]
    ti = lax.broadcasted_iota(jnp.int32, (c_, c_), 0)
    si = lax.broadcasted_iota(jnp.int32, (c_, c_), 1)
    rb, cb = ti >> 4, si >> 4
    m_diag = (rb == cb) & (si <= ti)
    m_off = [(cb == j) & (rb > j) for j in range(nsub - 1)]
    hk = HA * DKA
    gn = gn_ref[...]

    chunks = range(nb)
    heads = [(p, hh) for p in range(HA // 2) for hh in range(2)]
    rows = [slice(c * c_, (c + 1) * c_) for c in chunks]
    pair = [slice(128 * p, 128 * (p + 1)) for p in range(HA // 2)]
    css = []
    for c in chunks:
        g_hi, g_lo = _split(g_ref[rows[c], :])
        css.append(_dot(tri, g_hi) + _dot(tri, g_lo))
    lhs1, lhs2, kds, kes, q_inter, klts, dcols = [], [], [], [], [], [], []
    for c in chunks:
        b, rs, re = css[c][0:c_], css[c][c_:2 * c_], css[c][2 * c_:3 * c_]
        q = q_ref[rows[c], :].astype(F32)
        k = k_ref[rows[c], :].astype(F32)
        bl = b[c_ - 1:c_, :]
        qd = q * jnp.exp(b - rs)
        kd = k * jnp.exp(rs - b)
        ke = k * jnp.exp(re - b)
        qi = q * jnp.exp(b)
        kl = k * jnp.exp(bl - b)
        ql = [q * jnp.exp(jnp.minimum(b - b[SUB * (j + 1) - 1:SUB * (j + 1), :], 0.0)) for j in range(nsub - 1)]
        dcols.append(jnp.broadcast_to(jnp.exp(bl), (8, hk)).T[:, 0:1])
        kds.append([(kd[:, pair[p]] * hmask[hh]).astype(BF16) for p, hh in heads])
        kes.append([(ke[:, pair[p]] * hmask[hh]).astype(BF16) for p, hh in heads])
        klts.append([kl[:, ls].T.astype(BF16) for ls in pair])
        lhs1.append([qd[:, ls].astype(BF16) for ls in pair])
        lhs2.append([jnp.concatenate([ql[j][:, ls] for j in range(nsub - 1)], axis=0).astype(BF16) for ls in pair])
        q_inter.append([(qi[:, pair[p]] * hmask[hh]).astype(BF16) for p, hh in heads])
    a1s = [[_dot_nt(lhs1[c][p], kds[c][h]) for h, (p, hh) in enumerate(heads)] for c in chunks]
    a2s = [[_dot_nt(lhs2[c][p], kes[c][h]) for h, (p, hh) in enumerate(heads)] for c in chunks]
    atts = []
    for c in chunks:
        per_head = []
        for h in range(HA):
            att = jnp.zeros((c_, c_), F32)
            for j in reversed(range(nsub - 1)):
                att = jnp.where(m_off[j], a2s[c][h][j * c_:(j + 1) * c_], att)
            per_head.append(jnp.where(m_diag, a1s[c][h], att).astype(BF16))
        atts.append(per_head)
    vs_ = [[v_ref[rows[c], DVA * h:DVA * (h + 1)] for h in range(HA)] for c in chunks]
    o_intra = [[_dot(atts[c][h], vs_[c][h]) for h in range(HA)] for c in chunks]
    upds = [jnp.concatenate([_dot(klts[c][p][DKA * hh:DKA * (hh + 1)], vs_[c][2 * p + hh]) for p, hh in heads], axis=0)
            for c in chunks]

    s_cur = s_ref[...]
    s_in = []
    for c in chunks:
        s_in.append(s_cur.astype(BF16))
        s_cur = dcols[c] * s_cur + upds[c]
    s_ref[...] = s_cur
    sout_ref[0] = s_cur

    for c in chunks:
        for h in range(HA):
            o = o_intra[c][h] + _dot(q_inter[c][h], s_in[c][pair[h // 2], :])
            ms = jnp.mean(o * o, axis=-1, keepdims=True)
            vs = slice(DVA * h, DVA * (h + 1))
            rr = r_ref[rows[c], vs].astype(F32)
            o_ref[rows[c], vs] = (o * lax.rsqrt(ms + EPS) * gn * (rr * _sigmoid(rr))).astype(BF16)


def _gla_call(q, k, v, g, r, s0, gn, o_prev, *, n_seq, seq_rows, row0, nb):
    tq = nb * CHUNK
    steps = seq_rows // tq
    blk0 = row0 // tq
    row = lambda b, j: (blk0 + b * steps + j, 0)
    const = lambda b, j: (0, 0)
    tri = _gla_tri()
    in_specs = [pl.BlockSpec((tq, 256), row), pl.BlockSpec((tq, 256), row), pl.BlockSpec((tq, 512), row),
                pl.BlockSpec((tq, 256), row), pl.BlockSpec((tq, 512), row),
                pl.BlockSpec((1, 256, 128), lambda b, j: (b, 0, 0)),
                pl.BlockSpec((1, 128), const), pl.BlockSpec(tri.shape, const)]
    args = [q, k, v, g, r, s0, gn, tri]
    aliases = {}
    if o_prev is not None:
        in_specs.append(pl.BlockSpec(memory_space=pl.ANY))
        args.append(o_prev)
        aliases = {len(args) - 1: 0}
    kern = functools.partial(_gla_kernel, nb=nb)
    if o_prev is not None:
        kern = _drop_arg(kern, 8)
    return pl.pallas_call(
        kern,
        grid=(n_seq, steps),
        in_specs=in_specs,
        out_specs=[pl.BlockSpec((tq, 512), row), pl.BlockSpec((1, 256, 128), lambda b, j: (b, 0, 0))],
        out_shape=[jax.ShapeDtypeStruct((q.shape[0], 512), BF16), jax.ShapeDtypeStruct((n_seq, 256, 128), F32)],
        scratch_shapes=[pltpu.VMEM((256, 128), F32)],
        input_output_aliases=aliases,
        compiler_params=_cparams(("arbitrary", "arbitrary")),
        name="gla",
    )(*args)


def _drop_arg(fn, idx):
    def wrapped(*refs):
        return fn(*refs[:idx], *refs[idx + 1:])
    return wrapped


def _window(prev_ref, cur_ref, lo, hi, pb, ls):
    if lo < pb:
        return jnp.concatenate([prev_ref[lo:pb, ls], cur_ref[0:hi - pb, ls]], axis=0)
    return cur_ref[lo - pb:hi - pb, ls]


def _band_kernel(q_ref, kp_ref, kc_ref, vp_ref, vc_ref, bias_ref, o_ref, *, g, n_sub, pb):
    qs = CHUNK * g
    kw_rows = pb + qs
    lane = lax.broadcasted_iota(jnp.int32, (1, 128), 1)
    low = lane < DHB
    hmask = [jnp.where(low, 1.0, 0.0), jnp.where(low, 0.0, 1.0)]
    for s in range(n_sub):
        sb = s if bias_ref.shape[0] > 1 else 0
        rows = slice(qs * s, qs * (s + 1))
        lanes = [slice(128 * p, 128 * (p + 1)) for p in range(HB // 2)]
        heads = [(p, hh) for p in range(HB // 2) for hh in range(2)]
        qps = [q_ref[rows, ls].astype(F32) for ls in lanes]
        kws = [_window(kp_ref, kc_ref, qs * s, qs * s + kw_rows, pb, ls) for ls in lanes]
        vws = [_window(vp_ref, vc_ref, qs * s, qs * s + kw_rows, pb, ls) for ls in lanes]
        qq = [jnp.concatenate([(qps[p] * hmask[hh]).astype(BF16) for hh in range(2)], axis=0) for p in range(HB // 2)]
        sc2 = [_dot_nt(qq[p], kws[p]) for p in range(HB // 2)]
        scs = [sc2[p][qs * hh:qs * (hh + 1)] + bias_ref[sb, 2 * p + hh] for p, hh in heads]
        pes = [jnp.exp2(sc - jnp.max(sc, axis=-1, keepdims=True)) for sc in scs]
        pp = [jnp.concatenate([pes[2 * p + hh].astype(BF16) for hh in range(2)], axis=0) for p in range(HB // 2)]
        o2 = [_dot(pp[p], vws[p]) for p in range(HB // 2)]
        outs = [o2[p][qs * hh:qs * (hh + 1)] / jnp.sum(pes[2 * p + hh], axis=-1, keepdims=True) for p, hh in heads]
        for p, ls in enumerate(lanes):
            o_ref[rows, ls] = jnp.where(low, outs[2 * p], outs[2 * p + 1]).astype(BF16)


def _band_valid(g, pb, n_sub=None):
    rows, kw = CHUNK * g, pb + CHUNK * g
    r = np.arange(rows)[:, None]
    c = np.arange(kw)[None, :]
    dd = c // CHUNK - r // CHUNK
    band = (dd >= 0) & (dd <= pb // CHUNK)
    if n_sub is None:
        return band[None]
    return np.stack([band & (c >= pb - rows * s) for s in range(n_sub)])


def _band_bias(table, g, pb, valid):
    rows, kw = CHUNK * g, pb + CHUNK * g
    period = kw + rows
    m = np.arange(period)
    m = np.where(m < kw, m, m - period)
    ext = table[:, np.clip(m - pb, -MAX_REL, MAX_REL) + MAX_REL] * LOG2E
    flat = jnp.tile(ext, (1, rows))[:, :rows * (period - 1)]
    bias = flat.reshape(table.shape[0], rows, period - 1)[:, :, :kw]
    return jnp.where(valid[:, None], bias[None], -jnp.inf)


def _attn_call(kernel, q, kp, kc, vp, vc, extra, extra_specs, o_prev, *, width, kv_width, tq, pb,
               n_blocks, blk_map, prev_map, name):
    row = lambda i: (blk_map(i), 0)
    prev = lambda i: (prev_map(i), 0)
    in_specs = [pl.BlockSpec((tq, width), row),
                pl.BlockSpec((pb, kv_width), prev), pl.BlockSpec((tq, kv_width), row),
                pl.BlockSpec((pb, kv_width), prev), pl.BlockSpec((tq, kv_width), row)] + extra_specs
    args = [q, kp, kc, vp, vc] + extra
    aliases = {}
    if o_prev is not None:
        in_specs.append(pl.BlockSpec(memory_space=pl.ANY))
        args.append(o_prev)
        aliases = {len(args) - 1: 0}
        kernel = _drop_arg(kernel, len(args) - 1)
    return pl.pallas_call(
        kernel,
        grid=(n_blocks,),
        in_specs=in_specs,
        out_specs=pl.BlockSpec((tq, width), row),
        out_shape=jax.ShapeDtypeStruct((q.shape[0], width), BF16),
        input_output_aliases=aliases,
        compiler_params=_cparams(("parallel",)),
        name=name,
    )(*args)


def _attention(kernel_fn, q, k, v, cache_k, cache_v, masks, extra, extra_specs, *, width, kv_width, pb, tq, g,
               bp, lp, bs, name):
    bps = lp // tq
    n_sub = tq // (CHUNK * g)
    spec = lambda a: [pl.BlockSpec(a.shape, lambda i: (0,) * a.ndim)]
    kern = functools.partial(kernel_fn, g=g, n_sub=n_sub, pb=pb)
    common = dict(width=width, kv_width=kv_width, pb=pb)
    main = lambda i: (i // (bps - 1)) * bps + i % (bps - 1) + 1
    o = _attn_call(kern, q, k, k, v, v, [masks[0]] + extra, spec(masks[0]) + extra_specs, None, tq=tq,
                   n_blocks=bp * (bps - 1), blk_map=main, prev_map=lambda i: main(i) * (tq // pb) - 1,
                   name=name + "_main", **common)
    first = lambda i: i * bps
    o = _attn_call(kern, q, k, k, v, v, [masks[1]] + extra, spec(masks[1]) + extra_specs, o, tq=tq,
                   n_blocks=bp, blk_map=first, prev_map=lambda i: jnp.maximum(first(i) * (tq // pb) - 1, 0),
                   name=name + "_first", **common)
    samp = functools.partial(kernel_fn, g=1, n_sub=1, pb=pb)
    return _attn_call(samp, q, cache_k, k, cache_v, v, [masks[2]] + extra, spec(masks[2]) + extra_specs, o, tq=CHUNK,
                      n_blocks=bs, blk_map=lambda i: bp * lp // CHUNK + i, prev_map=lambda i: i,
                      name=name + "_sample", **common)


def _swa_kernel(q_ref, kp_ref, kc_ref, vp_ref, vc_ref, mask_ref, sink_ref, o_ref, *, g, n_sub, pb):
    qs = CHUNK * g
    kw_rows = pb + qs
    lane = lax.broadcasted_iota(jnp.int32, (1, 128), 1)
    low = lane < DHC
    hmask = [jnp.where(low, 1.0, 0.0), jnp.where(low, 0.0, 1.0)]
    pairs_per_kv = HC // KVC // 2
    for s in range(n_sub):
        msk = mask_ref[s if mask_ref.shape[0] > 1 else 0]
        rows = slice(qs * s, qs * (s + 1))
        kws = [_window(kp_ref, kc_ref, qs * s, qs * s + kw_rows, pb, slice(128 * kv, 128 * (kv + 1))) for kv in range(KVC)]
        vws = [_window(vp_ref, vc_ref, qs * s, qs * s + kw_rows, pb, slice(128 * kv, 128 * (kv + 1))) for kv in range(KVC)]
        heads = [(j, hh) for j in range(HC // 2) for hh in range(2)]
        qps = [q_ref[rows, 128 * j:128 * (j + 1)].astype(F32) for j in range(HC // 2)]
        per_kv = 2 * pairs_per_kv
        qq = [jnp.concatenate([(qps[j] * hmask[hh]).astype(BF16) for j, hh in heads[per_kv * kv:per_kv * (kv + 1)]],
                              axis=0) for kv in range(KVC)]
        sc2 = [_dot_nt(qq[kv], kws[kv]) for kv in range(KVC)]
        scs = [sc2[u // per_kv][qs * (u % per_kv):qs * (u % per_kv + 1)] + msk for u in range(len(heads))]
        sks = [sink_ref[0, 2 * j + hh] for j, hh in heads]
        ms = [jnp.maximum(jnp.max(sc, axis=-1, keepdims=True), sk) for sc, sk in zip(scs, sks)]
        pes = [jnp.exp2(sc - m) for sc, m in zip(scs, ms)]
        pp = [jnp.concatenate([pe.astype(BF16) for pe in pes[per_kv * kv:per_kv * (kv + 1)]], axis=0) for kv in range(KVC)]
        o2 = [_dot(pp[kv], vws[kv]) for kv in range(KVC)]
        outs = [o2[u // per_kv][qs * (u % per_kv):qs * (u % per_kv + 1)]
                / (jnp.sum(pes[u], axis=-1, keepdims=True) + jnp.exp2(sks[u] - ms[u])) for u in range(len(heads))]
        for j in range(HC // 2):
            o_ref[rows, 128 * j:128 * (j + 1)] = jnp.where(low, outs[2 * j], outs[2 * j + 1]).astype(BF16)


def _route(logits_t):
    a = [logits_t[4 * j:4 * j + 4] for j in range(EXP_PER_GROUP)]

    def first_argmax(vals, m):
        idx = jnp.full(m.shape, float(len(vals) - 1), F32)
        for j in reversed(range(len(vals) - 1)):
            idx = jnp.where(vals[j] == m, float(j), idx)
        return idx

    m1 = functools.reduce(jnp.maximum, a)
    i1 = first_argmax(a, m1)
    bsec = [jnp.where(i1 == float(j), -jnp.inf, a[j]) for j in range(EXP_PER_GROUP)]
    m2 = functools.reduce(jnp.maximum, bsec)
    i2 = first_argmax(bsec, m2)
    rows = lambda x: [x[gi:gi + 1] for gi in range(N_GROUPS)]
    gm = functools.reduce(jnp.maximum, rows(m1))
    gscore = jnp.exp(m1 - gm) + jnp.exp(m2 - gm)
    gs = rows(gscore)
    gsel = first_argmax(gs, functools.reduce(jnp.maximum, gs))

    def pick(x):
        xr = rows(x)
        out = xr[N_GROUPS - 1]
        for gi in reversed(range(N_GROUPS - 1)):
            out = jnp.where(gsel == float(gi), xr[gi], out)
        return out

    p1 = jnp.exp(pick(m1) - gm)
    p2 = jnp.exp(pick(m2) - gm)
    w1 = p1 / (p1 + p2)
    w2 = p2 / (p1 + p2)
    s1, s2 = pick(i1), pick(i2)
    lo, hi = jnp.minimum(s1, s2), jnp.maximum(s1, s2)
    pair = jnp.where(lo == 0.0, hi - 1.0, jnp.where(lo == 1.0, hi + 1.0, 5.0))
    bucket = gsel * float(N_PAIRS) + pair
    first_is_lo = s1 < s2
    return bucket, jnp.where(first_is_lo, w1, w2), jnp.where(first_is_lo, w2, w1)


def _outproj_kernel(*refs, n_x, n_o, n_prompt_tiles, with_moe):
    x_refs = refs[:n_x]
    o_refs = refs[n_x:n_x + n_o]
    w_refs = refs[n_x + n_o:n_x + 2 * n_o]
    rest = refs[n_x + 2 * n_o:]
    if with_moe:
        z_ref, gate_prev_ref, rest = rest[0], rest[1], rest[2:]
    (gate_ref, nf_ref, sh_ref, sc_ref, wr_ref, br_ref, tri_ref,
     xn_ref, disp_ref, meta_ref, cnt_ref, run_ref, xbuf, xsem) = rest
    t = xn_ref.shape[0]

    @pl.when(pl.program_id(0) == 0)
    def _():
        run_ref[...] = jnp.zeros_like(run_ref)

    x_src = _ring_block(list(x_refs), xbuf, xsem, n_prompt_tiles)

    halves = [slice(0, t // 2), slice(t // 2, t)]
    grp = [slice(0, t // (2 * CHUNK)), slice(t // (2 * CHUNK), t // CHUNK)]
    ys = []
    for rs in halves:
        y = _dot(o_refs[0][rs, :], w_refs[0][...])
        for i in range(1, n_o):
            y = y + _dot(o_refs[i][rs, :], w_refs[i][...])
        ys.append(y)
    gate, shift, scale = gate_ref[...], sh_ref[...], sc_ref[...]
    gys = [_group_affine(y, gate[gs], None) for y, gs in zip(ys, grp)]

    if with_moe:
        gate_prev = gate_prev_ref[...]
        xs = [x_src[rs, :] + _group_affine(_unpack_pairs([z_ref[s, rs, :] for s in range(z_ref.shape[0])], F32),
                                           gate_prev[gs], None) for rs, gs in zip(halves, grp)]
    else:
        xs = [x_src[rs, :] for rs in halves]
    for rs, x, gy in zip(halves, xs, gys):
        xn_ref[rs, :] = x + gy
    hs = [_norm_mod(xn_ref[rs, :], nf_ref[...], shift[gs], scale[gs]) for rs, gs in zip(halves, grp)]
    for rs, h in zip(halves, hs):
        for s, slab in enumerate(_pack_pairs(h)):
            disp_ref[s, rs, :] = slab
    logits_t = [(_dot3_narrow(h, wr_ref[...]) + br_ref[...]).T[0:N_EXPERTS] for h in hs]
    bucket, w_lo, w_hi = _route(jnp.concatenate(logits_t, axis=1))
    r128 = lax.broadcasted_iota(jnp.int32, (128, t), 0)
    tok = (pl.program_id(0) * t + lax.broadcasted_iota(jnp.int32, (1, t), 1)).astype(F32)
    aux = jnp.where(r128 == 0, w_lo, jnp.where(r128 == 1, w_hi, jnp.where(r128 == 2, tok, 0.0))).T
    disp_ref[disp_ref.shape[0] - 1] = pltpu.bitcast(aux, jnp.int32)
    brow = lax.broadcasted_iota(jnp.int32, (BUCKET_ROWS, t), 0).astype(F32)
    onehot = jnp.where(brow == bucket, 1.0, 0.0)
    before = _dot(onehot.astype(BF16), tri_ref[...]) + run_ref[:, 0:1]
    rank = jnp.sum(onehot * before, axis=0, keepdims=True)
    run_ref[...] = run_ref[...] + jnp.sum(onehot, axis=1, keepdims=True)
    cnt_ref[...] = run_ref[...]
    r8 = lax.broadcasted_iota(jnp.int32, (8, t), 0)
    meta_ref[...] = jnp.where(r8 == 0, bucket, jnp.where(r8 == 1, rank, 0.0)).astype(jnp.int32)


def _outproj_call(xs_, os_, ws, mods, nf, wr, br, n_pad, moe=None):
    d = xs_[0].shape[1]
    n = sum(a.shape[0] for a in xs_)
    npt = xs_[0].shape[0] // TM
    row = lambda i: (i, 0)
    const = lambda i: (0, 0)
    n_o = len(os_)
    moe_specs = [] if moe is None else [pl.BlockSpec((moe[0].shape[0], TM, 128), lambda i: (0, i, 0)),
                                        _mod_spec(GATE_FFN)]
    in_specs = ([pl.BlockSpec(memory_space=pl.ANY) for _ in xs_]
                + [pl.BlockSpec((TM, o.shape[1]), row) for o in os_]
                + [pl.BlockSpec(w.shape, const) for w in ws]
                + moe_specs
                + [_mod_spec(GATE_MIX), pl.BlockSpec((1, d), const),
                   _mod_spec(SHIFT_FFN), _mod_spec(SCALE_FFN),
                   pl.BlockSpec(wr.shape, const), pl.BlockSpec(br.shape, const),
                   pl.BlockSpec((TM, TM), const)])
    tri = jnp.asarray(np.triu(np.ones((TM, TM), np.float32), k=1), dtype=BF16)
    return pl.pallas_call(
        functools.partial(_outproj_kernel, n_x=len(xs_), n_o=n_o, n_prompt_tiles=npt, with_moe=moe is not None),
        grid=(n // TM,),
        in_specs=in_specs,
        out_specs=[pl.BlockSpec((TM, d), row), pl.BlockSpec((DISP_SLABS, TM, 128), lambda i: (0, i, 0)),
                   pl.BlockSpec((8, TM), lambda i: (0, i)), pl.BlockSpec((BUCKET_ROWS, 128), const)],
        out_shape=[jax.ShapeDtypeStruct((n, d), F32), jax.ShapeDtypeStruct((DISP_SLABS, n_pad, 128), jnp.int32),
                   jax.ShapeDtypeStruct((8, n), jnp.int32), jax.ShapeDtypeStruct((BUCKET_ROWS, 128), F32)],
        scratch_shapes=[pltpu.VMEM((BUCKET_ROWS, 128), F32)] + _ring_scratch(d),
        compiler_params=_cparams(("arbitrary",)),
        name="outproj_router",
    )(*xs_, *os_, *ws, *(() if moe is None else moe), mods, nf, mods, mods, wr, br, tri)


def _sc_mesh():
    return plsc.VectorSubcoreMesh(core_axis_name="core", subcore_axis_name="subcore")


def _sc_scatter_rows(src, idx, n_out):
    r = idx.shape[0]
    k = SC_GROUP
    w_per = r // (SC_WINDOW * SC_WORKERS)
    assert idx.shape == (src.shape[0],) and r % (SC_WINDOW * SC_WORKERS) == 0 and w_per % k == 0
    n_groups = w_per // k

    @functools.partial(
        pl.kernel, out_type=jax.ShapeDtypeStruct((n_out, 128), src.dtype), mesh=_sc_mesh(),
        scratch_types=[pltpu.VMEM((w_per, SC_WINDOW), jnp.int32),
                       pltpu.VMEM((2 * k, SC_WINDOW, 128), src.dtype),
                       pltpu.SemaphoreType.DMA((2,)), pltpu.SemaphoreType.DMA((2,))])
    def copy(x_hbm, i_hbm, o_hbm, ibuf, xbuf, in_sem, out_sem):
        wid = lax.axis_index("core") * (SC_WORKERS // 2) + lax.axis_index("subcore")
        pltpu.sync_copy(i_hbm.at[wid], ibuf)
        first = wid * w_per

        def start_in(g, slot):
            return [pltpu.async_copy(x_hbm.at[pl.ds((first + g * k + c) * SC_WINDOW, SC_WINDOW)],
                                     xbuf.at[slot * k + c], in_sem.at[slot]) for c in range(k)]

        def start_out(g, slot):
            return [pltpu.async_copy(xbuf.at[slot * k + c], o_hbm.at[ibuf.at[g * k + c]], out_sem.at[slot])
                    for c in range(k)]

        pending_in = start_in(0, 0)
        for g in range(n_groups):
            slot = g % 2
            for cp in pending_in:
                cp.wait()
            pending_out = start_out(g, slot)
            if g + 1 < n_groups:
                pending_in = start_in(g + 1, 1 - slot)
            for cp in pending_out:
                cp.wait()

    return copy(src, idx.reshape(SC_WORKERS, w_per, SC_WINDOW))


def _moe_kernel(elo_ref, ehi_ref, nvalid_ref, xs_ref, *refs, n_tok, dump_tiles):
    w_refs, (y_ref, tok_ref) = refs[:2 * MOE_TILES], refs[2 * MOE_TILES:]
    d_in = w_refs[0].shape[2] - D_FF
    step = pl.program_id(0)
    t = TMO
    tiles = range(MOE_TILES)
    rows = [slice(t * j, t * (j + 1)) for j in tiles]
    auxs = [pltpu.bitcast(xs_ref[Y_SLABS, rows[j], :], F32) for j in tiles]
    r = lax.broadcasted_iota(jnp.int32, (1, t), 1)
    for j in tiles:
        i = step * MOE_TILES + j
        spare = n_tok + (i % dump_tiles) * t + r
        tok = jnp.where(r < nvalid_ref[i], auxs[j].T[2:3, :].astype(jnp.int32), spare)
        for c in range(t // 128):
            tok_ref[j, c:c + 1, :] = tok[:, 128 * c:128 * (c + 1)]

    any_tokens = nvalid_ref[step * MOE_TILES] > 0
    for j in range(1, MOE_TILES):
        any_tokens = jnp.logical_or(any_tokens, nvalid_ref[step * MOE_TILES + j] > 0)

    @pl.when(any_tokens)
    def _():
        units = [(j, e) for j in tiles for e in range(2)]
        hs = [_unpack_pairs([xs_ref[s, rows[j], :] for s in range(Y_SLABS)], BF16) for j in tiles]
        abs_ = [_dot(hs[j], w_refs[2 * j + e][0, 0, 0:d_in, :]) for j, e in units]
        acts = [(ab[:, :D_FF] * _sigmoid(ab[:, :D_FF]) * ab[:, D_FF:]).astype(BF16) for ab in abs_]
        ys = [_dot(act, w_refs[2 * j + e][0, 0, d_in:d_in + D_FF, :]) for act, (j, e) in zip(acts, units)]
        for j in tiles:
            acc = auxs[j][:, 0:1] * ys[2 * j] + auxs[j][:, 1:2] * ys[2 * j + 1]
            for s, slab in enumerate(_pack_pairs(acc)):
                y_ref[s, rows[j], :] = slab

    @pl.when(jnp.logical_not(any_tokens))
    def _():
        y_ref[...] = jnp.zeros_like(y_ref)


def _moe_call(xs, elo, ehi, nvalid, wexp, n_tiles, n_tok, dump_tiles):
    m = MOE_TILES
    assert n_tiles % m == 0 and wexp.shape[3] == 2 * D_FF
    weight_specs = [pl.BlockSpec((1, 1) + wexp.shape[2:],
                                 lambda i, lo, hi, v, j=j, sel=sel: (0, (lo, hi)[sel][m * i + j], 0, 0))
                    for j in range(m) for sel in range(2)]
    weights = [wexp] * (2 * m)
    return pl.pallas_call(
        functools.partial(_moe_kernel, n_tok=n_tok, dump_tiles=dump_tiles),
        grid_spec=pltpu.PrefetchScalarGridSpec(
            num_scalar_prefetch=3,
            grid=(n_tiles // m,),
            in_specs=[pl.BlockSpec((DISP_SLABS, m * TMO, 128), lambda i, lo, hi, v: (0, i, 0))] + weight_specs,
            out_specs=[pl.BlockSpec((Y_SLABS, m * TMO, 128), lambda i, lo, hi, v: (0, i, 0)),
                       pl.BlockSpec((m, TMO // 128, 128), lambda i, lo, hi, v: (i, 0, 0))]),
        out_shape=[jax.ShapeDtypeStruct((Y_SLABS, n_tiles * TMO, 128), jnp.int32),
                   jax.ShapeDtypeStruct((n_tiles, TMO // 128, 128), jnp.int32)],
        compiler_params=_cparams(("arbitrary",), vmem_mb=VMEM_LIMIT_MOE_MB),
        name="moe_grouped",
    )(elo, ehi, nvalid, xs, *weights)


def _after(x, token):
    return lax.optimization_barrier((x, token))[0]


def _cast_kernel(after_ref, wgu_ref, wdn_ref, o_ref):
    k = wgu_ref.shape[2]
    o_ref[0, 0, 0:k, :] = wgu_ref[0, 0].astype(o_ref.dtype)
    o_ref[0, 0, k:, :] = wdn_ref[0, 0].astype(o_ref.dtype)


def _cast_call(w_gate_up, w_down, layer, after):
    _, e, k, n = w_gate_up.shape
    kd = w_down.shape[2]
    assert w_down.shape[3] == n
    return pl.pallas_call(
        _cast_kernel,
        grid=(e,),
        in_specs=[pl.BlockSpec(memory_space=pl.ANY),
                  pl.BlockSpec((1, 1, k, n), lambda i: (layer, i, 0, 0)),
                  pl.BlockSpec((1, 1, kd, n), lambda i: (layer, i, 0, 0))],
        out_specs=pl.BlockSpec((1, 1, k + kd, n), lambda i: (0, i, 0, 0)),
        out_shape=jax.ShapeDtypeStruct((1, e, k + kd, n), BF16),
        compiler_params=_cparams(("parallel",)),
        name="cast_weights",
    )(after, w_gate_up, w_down)


def _moe_layer(disp, meta, counts, w_gate_up, w_down, layer, n, n_pad, sort_rows):
    n_tiles = sort_rows // TMO
    wexp = _cast_call(w_gate_up, w_down, layer, counts)
    cnt = counts[:N_BUCKETS, 0].astype(jnp.int32)
    padded = ((cnt + TMO - 1) // TMO) * TMO
    ends = jnp.cumsum(padded)
    offs = ends - padded
    bucket, rank = meta[0], meta[1]
    pos = rank + jnp.sum(jnp.where(bucket[None, :] == jnp.arange(N_BUCKETS, dtype=jnp.int32)[:, None],
                                   offs[:, None], 0), axis=0)
    tile_start = jnp.arange(n_tiles, dtype=jnp.int32) * TMO
    tile_bucket = jnp.minimum(jnp.sum((tile_start[:, None] >= ends[None, :]).astype(jnp.int32), axis=1), N_BUCKETS - 1)
    pair_lo = np.array([0, 0, 0, 1, 1, 2], np.int32)
    pair_hi = np.array([1, 2, 3, 2, 3, 3], np.int32)
    b_lo = jnp.asarray(np.repeat(np.arange(N_GROUPS), N_PAIRS) * EXP_PER_GROUP + np.tile(pair_lo, N_GROUPS), jnp.int32)
    b_hi = jnp.asarray(np.repeat(np.arange(N_GROUPS), N_PAIRS) * EXP_PER_GROUP + np.tile(pair_hi, N_GROUPS), jnp.int32)
    onehot_tb = (tile_bucket[:, None] == jnp.arange(N_BUCKETS, dtype=jnp.int32)[None, :]).astype(jnp.int32)
    elo = jnp.sum(onehot_tb * b_lo[None, :], axis=1)
    ehi = jnp.sum(onehot_tb * b_hi[None, :], axis=1)
    bucket_end = jnp.sum(onehot_tb * (offs + cnt)[None, :], axis=1)
    nvalid = jnp.where(tile_start < ends[-1], jnp.clip(bucket_end - tile_start, 0, TMO), 0)
    dump = sort_rows + jnp.arange(n_pad - n, dtype=jnp.int32)
    pos_sc = jnp.concatenate([pos, dump])
    total = sort_rows + n_pad - n
    sc_idx = (pos_sc[None, :] + (jnp.arange(DISP_SLABS, dtype=jnp.int32) * total)[:, None]).reshape(-1)
    xs = _sc_scatter_rows(disp.reshape(DISP_SLABS * n_pad, 128), sc_idx, DISP_SLABS * total)
    ys, tok = _moe_call(xs.reshape(DISP_SLABS, total, 128), elo, ehi, nvalid, wexp, n_tiles,
                        n, (n_pad - n) // TMO)
    back_idx = (tok.reshape(1, sort_rows) + (jnp.arange(Y_SLABS, dtype=jnp.int32) * n_pad)[:, None]).reshape(-1)
    z = _sc_scatter_rows(ys.reshape(Y_SLABS * sort_rows, 128), back_idx, Y_SLABS * n_pad)
    return z.reshape(Y_SLABS, n_pad, 128), tok


def _final_kernel(xn_hbm, z_ref, gate_ref, g_ref, yp_ref, ys_ref, xbuf, xsem, *, n_prompt_tiles):
    x = _add_moe(_ring_block([xn_hbm], xbuf, xsem), z_ref, gate_ref)
    ms = jnp.mean(x * x, axis=-1, keepdims=True)
    y = x * lax.rsqrt(ms + EPS) * g_ref[...]
    i = pl.program_id(0)

    @pl.when(i < n_prompt_tiles)
    def _():
        yp_ref[...] = y

    @pl.when(i >= n_prompt_tiles)
    def _():
        ys_ref[...] = y


def _final_call(xn, z, mods, g, n_prompt):
    n, d = xn.shape
    npt = n_prompt // TM
    assert n - n_prompt == TM
    return pl.pallas_call(
        functools.partial(_final_kernel, n_prompt_tiles=npt),
        grid=(n // TM,),
        in_specs=[pl.BlockSpec(memory_space=pl.ANY), pl.BlockSpec((z.shape[0], TM, 128), lambda i: (0, i, 0)),
                  _mod_spec(GATE_FFN), pl.BlockSpec((1, d), lambda i: (0, 0))],
        out_specs=_token_specs(npt, d),
        out_shape=[jax.ShapeDtypeStruct((n_prompt, d), F32), jax.ShapeDtypeStruct((TM, d), F32)],
        scratch_shapes=_ring_scratch(d),
        compiler_params=_cparams(("arbitrary",)),
        name="final_norm",
    )(xn, z, mods, g)


def kernel(x_prompt, x_sample, c_prompt, c_sample, state_gla, cache_band_k, cache_band_v, cache_swa_k, cache_swa_v,
           w_ada, b_ada, norm_mix, norm_ffn, norm_final, w_in_even, w_gate_a, b_gate_a, gla_norm, rel_bias_b,
           w_out_even, w_in_odd, sinks_c, w_out_odd, w_router, b_router, w_gate_up, w_down):
    bp, lp, d = x_prompt.shape
    bs, ls_, _ = x_sample.shape
    n_p, n_s = bp * lp, bs * ls_
    n = n_p + n_s
    assert ls_ == CHUNK and n_s == TM and lp % TM == 0 and PAST_LEN % CHUNK == 0

    xp2, xs2 = x_prompt.reshape(n_p, d), x_sample.reshape(n_s, d)

    c16 = jnp.zeros((SEQ_ROWS, d), F32).at[:bp].set(c_prompt).at[bp:bp + bs].set(c_sample)
    mods = _ada_call(c16, w_ada, b_ada)
    seq_of_group = np.concatenate([np.repeat(np.arange(bp), lp // CHUNK), bp + np.arange(bs)])
    mods_g = [mods[l][seq_of_group] for l in range(DEPTH)]

    perm = np.array([4 * (c % 4) + c // 4 for c in range(N_EXPERTS)])
    wr = jnp.zeros((d, 128), F32).at[:, :N_EXPERTS].set(w_router[:, perm])
    br = jnp.zeros((1, 128), F32).at[0, :N_EXPERTS].set(b_router[perm])

    sc_unit = SC_WINDOW * SC_WORKERS * SC_GROUP
    n_pad = n + TMO
    while (DISP_SLABS * n_pad) % sc_unit or (Y_SLABS * n_pad) % TMO or (n_pad - n) % TMO:
        n_pad += TMO
    sort_rows = n + N_BUCKETS * TMO
    while (Y_SLABS * sort_rows) % sc_unit or sort_rows % (MOE_TILES * TMO):
        sort_rows += TMO

    gla_p = gla_s = bk_p = bv_p = bk_s = bv_s = sk_p = sv_p = sk_s = sv_s = None
    xn = z = tok = None
    for l in range(DEPTH):
        i = l // 2
        if l % 2 == 0:
            w = w_in_even[i]
            w_main = jnp.concatenate([w[:, :1536], w[:, 1552:]], axis=1).astype(BF16)
            w_la = jnp.zeros((d, 128), F32).at[:, :GATE_RANK].set(w[:, 1536:1552]).astype(BF16)
            w_gate = jnp.zeros((128, HA * DKA), F32).at[:GATE_RANK].set(w_gate_a[i])
            qa, ka, va, ra, qb, kb, vb, ga = _inproj_even_call(
                xp2, xs2, mods_g[l], norm_mix[l][None], w_main, w_la, w_gate, b_gate_a[i][None])
            xres, moe_prev = [xp2, xs2], None
            gn = gla_norm[i][None]
            oa, s_p = _gla_call(qa, ka, va, ga, ra, jnp.zeros((bp, 256, 128), F32), gn, None,
                                n_seq=bp, seq_rows=lp, row0=0, nb=8)
            oa, s_s = _gla_call(qa, ka, va, ga, ra, state_gla[i].reshape(bs, 256, 128), gn, oa,
                                n_seq=bs, seq_rows=ls_, row0=n_p, nb=1)
            gla_p, gla_s = s_p.reshape(1, bp, HA, DKA, DVA), s_s.reshape(1, bs, HA, DKA, DVA)
            pb = N_PREV_B * CHUNK
            tq, g = 512, 2
            ck = cache_band_k[i].reshape(bs * pb, HB * DHB).astype(BF16)
            cv = cache_band_v[i].reshape(bs * pb, HB * DHB).astype(BF16)
            biases = (_band_bias(rel_bias_b[i], g, pb, _band_valid(g, pb)),
                      _band_bias(rel_bias_b[i], g, pb, _band_valid(g, pb, tq // (CHUNK * g))),
                      _band_bias(rel_bias_b[i], 1, pb, _band_valid(1, pb)))
            ob = _attention(_band_kernel, qb, kb, vb, ck, cv, biases, [], [], width=512, kv_width=512, pb=pb,
                            tq=tq, g=g, bp=bp, lp=lp, bs=bs, name="band")
            tail = lambda a: jnp.stack([a[(b + 1) * lp - pb:(b + 1) * lp] for b in range(bp)]).astype(F32).reshape(1, bp, pb, HB, DHB)
            new = lambda a: a[n_p:].astype(F32).reshape(bs, ls_, HB, DHB)
            bk_p, bv_p = tail(kb), tail(vb)
            bk_s = jnp.concatenate([cache_band_k[i][:, ls_:], new(kb)], axis=1)[None]
            bv_s = jnp.concatenate([cache_band_v[i][:, ls_:], new(vb)], axis=1)[None]
            wo = w_out_even[i].astype(BF16)
            os_, ws = [oa, ob], [wo[:HA * DVA], wo[HA * DVA:]]
        else:
            w = _after(w_in_odd[i], tok)
            w_out_l = _after(w_out_odd[i], tok)
            cache_k_l, cache_v_l = _after(cache_swa_k[i], tok), _after(cache_swa_v[i], tok)
            wk, wv = w[:, 1024:1152], w[:, 1152:1280]
            dup = lambda a: jnp.concatenate([a[:, :64], a[:, :64], a[:, 64:], a[:, 64:]], axis=1)
            w_all = jnp.concatenate([w[:, :1024], dup(wk), dup(wv)], axis=1).astype(BF16)
            cos, sin, rope_map = _rope_tables(lp, ls_, bp, bs)
            q, k, v = _inproj_odd_call(xn, z, mods_g[l - 1], mods_g[l], norm_mix[l][None], cos, sin, rope_map, w_all)
            xres, moe_prev = [xn], (z, mods_g[l - 1])
            pb = WINDOW
            tq, g = 512, 2
            sink = sinks_c[i][None] * LOG2E
            sink_spec = [pl.BlockSpec(memory_space=pltpu.SMEM)]
            dupc = lambda c: jnp.concatenate([c[:, :, 0], c[:, :, 0], c[:, :, 1], c[:, :, 1]], axis=-1).reshape(bs * pb, 256).astype(BF16)
            ck, cv = dupc(cache_k_l), dupc(cache_v_l)
            additive = lambda valid: jnp.asarray(np.where(valid, 0.0, -np.inf), F32)
            masks = (additive(_band_valid(g, pb)), additive(_band_valid(g, pb, tq // (CHUNK * g))),
                     additive(_band_valid(1, pb)))
            o = _attention(_swa_kernel, q, k, v, ck, cv, masks, [sink], sink_spec, width=1024, kv_width=256, pb=pb,
                           tq=tq, g=g, bp=bp, lp=lp, bs=bs, name="swa")
            undup = lambda a: jnp.concatenate([a[:, 0:64], a[:, 128:192]], axis=1).astype(F32)
            tail = lambda a: jnp.stack([undup(a[(b + 1) * lp - pb:(b + 1) * lp]) for b in range(bp)]).reshape(1, bp, pb, KVC, DHC)
            new = lambda a: undup(a[n_p:]).reshape(bs, ls_, KVC, DHC)
            sk_p, sv_p = tail(k), tail(v)
            sk_s = jnp.concatenate([cache_swa_k[i][:, ls_:], new(k)], axis=1)[None]
            sv_s = jnp.concatenate([cache_swa_v[i][:, ls_:], new(v)], axis=1)[None]
            os_, ws = [o], [w_out_l.astype(BF16)]
        xn, disp, meta, counts = _outproj_call(xres, os_, ws, mods_g[l], norm_ffn[l][None], wr, br, n_pad, moe_prev)
        z, tok = _moe_layer(disp, meta, counts, w_gate_up, w_down, l, n, n_pad, sort_rows)

    y_prompt, y_sample = _final_call(xn, z, mods_g[DEPTH - 1], norm_final[None], n_p)
    return (y_prompt.reshape(bp, lp, d), y_sample.reshape(bs, ls_, d),
            gla_p, gla_s, bk_p, bv_p, bk_s, bv_s, sk_p, sv_p, sk_s, sv_s)
```

```python
import functools

import numpy as np
import jax
import jax.numpy as jnp
from jax import lax
from jax.experimental import pallas as pl
from jax.experimental.pallas import tpu as pltpu
from jax.experimental.pallas import tpu_sc as plsc

F32 = jnp.float32
BF16 = jnp.bfloat16

D_MODEL = 1024
DEPTH = 2
CHUNK = 64
PAST_LEN = 4096
HA, DKA, DVA = 4, 64, 128
GATE_RANK = 16
GATE_TAU = 16.0
HB, DHB = 8, 64
N_PREV_B = 8
MAX_REL = 128
HC, KVC, DHC = 16, 2, 64
WINDOW = 128
ROPE_THETA = 10000.0
N_EXPERTS = 16
N_GROUPS = 4
EXP_PER_GROUP = 4
D_FF = 512
EPS = 1e-6

N_PAIRS = 6
N_BUCKETS = N_GROUPS * N_PAIRS
BUCKET_ROWS = 32
Y_SLABS = 4
DISP_SLABS = Y_SLABS + 1
TMO = 256
MOE_TILES = 2
SC_WINDOW = 128
SC_WORKERS = 32
SC_GROUP = 3

TM = 512
SEQ_ROWS = 16
SUB = 16
LOG2E = 1.4426950408889634
VMEM_LIMIT_MB = 48
VMEM_LIMIT_MOE_MB = 56


def _cparams(sem, vmem_mb=VMEM_LIMIT_MB):
    return pltpu.CompilerParams(dimension_semantics=sem, vmem_limit_bytes=vmem_mb * 1024 * 1024)


def _dot(a, b):
    return jnp.dot(a, b, preferred_element_type=F32)


def _dot_nt(a, b):
    return lax.dot_general(a, b, (((1,), (1,)), ((), ())), preferred_element_type=F32)


def _split(a):
    hi = a.astype(BF16)
    lo = (a - hi.astype(F32)).astype(BF16)
    return hi, lo


def _dot3(a, b):
    ah, al = _split(a)
    bh, bl = _split(b)
    return _dot(ah, bh) + _dot(ah, bl) + _dot(al, bh)


def _dot3_narrow(a, b):
    ah, al = _split(a)
    bh, bl = _split(b)
    n = b.shape[1]
    p = _dot(ah, jnp.concatenate([bh, bl], axis=1))
    return p[:, :n] + p[:, n:] + _dot(al, bh)


def _sigmoid(x):
    return 1.0 / (1.0 + jnp.exp(-x))


def _group_affine(y, mul, add):
    parts = []
    for gi in range(y.shape[0] // CHUNK):
        p = y[gi * CHUNK:(gi + 1) * CHUNK]
        if mul is not None:
            p = p * mul[gi:gi + 1]
        if add is not None:
            p = p + add[gi:gi + 1]
        parts.append(p)
    return jnp.concatenate(parts, axis=0)


def _norm_mod(x, g, shift, scale):
    ms = jnp.mean(x * x, axis=-1, keepdims=True)
    return _group_affine(x * lax.rsqrt(ms + EPS) * g, 1.0 + scale, shift)


def _mod_spec(part):
    return pl.BlockSpec((TM // CHUNK, D_MODEL), lambda i: (i, part))


SHIFT_MIX, SCALE_MIX, GATE_MIX, SHIFT_FFN, SCALE_FFN, GATE_FFN = range(6)


def _on_token_tile(xp_ref, xs_ref, n_prompt_tiles, body):
    @pl.when(pl.program_id(0) < n_prompt_tiles)
    def _():
        body(xp_ref)

    @pl.when(pl.program_id(0) >= n_prompt_tiles)
    def _():
        body(xs_ref)


RING = 3


def _ring_block(srcs, buf, sem, n_prompt_tiles=None):
    s = pl.program_id(0)
    n_steps = pl.num_programs(0)
    t = buf.shape[1]

    def copy(src, blk, slot):
        return pltpu.make_async_copy(src.at[pl.ds(pl.multiple_of(blk * t, t), t)], buf.at[slot], sem.at[slot])

    def start(step, slot):
        if len(srcs) == 1:
            copy(srcs[0], step, slot).start()
        elif isinstance(step, int):
            assert step < n_prompt_tiles
            copy(srcs[0], step, slot).start()
        else:
            @pl.when(step < n_prompt_tiles)
            def _():
                copy(srcs[0], step, slot).start()

            @pl.when(step >= n_prompt_tiles)
            def _():
                copy(srcs[1], step - n_prompt_tiles, slot).start()

    @pl.when(s == 0)
    def _():
        for k in range(RING - 1):
            start(k, k)

    ahead = s + (RING - 1)

    @pl.when(ahead < n_steps)
    def _():
        start(ahead, ahead % RING)

    slot = s % RING
    copy(srcs[0], 0, slot).wait()
    return buf.at[slot]


def _ring_scratch(d):
    return [pltpu.VMEM((RING, TM, d), F32), pltpu.SemaphoreType.DMA((RING,))]


def _token_specs(n_prompt_tiles, d):
    return [pl.BlockSpec((TM, d), lambda i: (jnp.minimum(i, n_prompt_tiles - 1), 0)),
            pl.BlockSpec((TM, d), lambda i: (0, 0))]


def _ada_kernel(c_ref, w_ref, b_ref, o_ref):
    c = c_ref[...]
    o_ref[0] = _dot3(c * _sigmoid(c), w_ref[0]) + b_ref[0]


def _ada_call(c16, w_ada, b_ada):
    d = D_MODEL
    tn = 1024
    return pl.pallas_call(
        _ada_kernel,
        grid=(DEPTH, 6 * d // tn),
        in_specs=[pl.BlockSpec((SEQ_ROWS, d), lambda l, j: (0, 0)),
                  pl.BlockSpec((1, d, tn), lambda l, j: (l, 0, j)),
                  pl.BlockSpec((1, 1, tn), lambda l, j: (l, 0, j))],
        out_specs=pl.BlockSpec((1, SEQ_ROWS, tn), lambda l, j: (l, 0, j)),
        out_shape=jax.ShapeDtypeStruct((DEPTH, SEQ_ROWS, 6 * d), F32),
        compiler_params=_cparams(("arbitrary", "arbitrary")),
        name="ada",
    )(c16, w_ada, b_ada.reshape(DEPTH, 1, 6 * d))


def _inproj_even_kernel(xp_ref, xs_ref, sh_ref, sc_ref, g_ref, w_ref, wla_ref, wg_ref, bg_ref,
                        qa_ref, ka_ref, va_ref, ra_ref, qb_ref, kb_ref, vb_ref, ga_ref, *, n_prompt_tiles):
    def body(x_ref):
        t = x_ref.shape[0]
        outs = ((qa_ref, 0, 256, DKA ** -0.5), (ka_ref, 256, 512, None), (va_ref, 512, 1024, None),
                (ra_ref, 1024, 1536, None), (qb_ref, 1536, 2048, DHB ** -0.5 * LOG2E), (kb_ref, 2048, 2560, None),
                (vb_ref, 2560, 3072, None))
        shift, scale_ = sh_ref[...], sc_ref[...]
        halves = [slice(0, t // 2), slice(t // 2, t)]
        grp = [slice(0, t // (2 * CHUNK)), slice(t // (2 * CHUNK), t // CHUNK)]
        hbs = [_norm_mod(x_ref[rs, :], g_ref[...], shift[gs], scale_[gs]).astype(BF16) for rs, gs in zip(halves, grp)]
        for rs, hb in zip(halves, hbs):
            zs = [_dot(hb, w_ref[:, lo:hi]) for _, lo, hi, _ in outs]
            la = _dot(hb, wla_ref[...])
            for z, (o_ref, _, _, scale) in zip(zs, outs):
                o_ref[rs, :] = (z if scale is None else z * scale).astype(BF16)
            gl = _dot3(la, wg_ref[...]) + bg_ref[...]
            ga_ref[rs, :] = -(jnp.maximum(-gl, 0.0) + jnp.log(1.0 + jnp.exp(-jnp.abs(gl)))) * (1.0 / GATE_TAU)

    _on_token_tile(xp_ref, xs_ref, n_prompt_tiles, body)


def _inproj_even_call(xp, xs, mods, g, w_main, w_la, w_gate, b_gate):
    d = xp.shape[1]
    npt = xp.shape[0] // TM
    n = xp.shape[0] + xs.shape[0]
    row = lambda i: (i, 0)
    const = lambda i: (0, 0)
    widths = (256, 256, 512, 512, 512, 512, 512)
    out_shape = [jax.ShapeDtypeStruct((n, w), BF16) for w in widths] + [jax.ShapeDtypeStruct((n, 256), F32)]
    out_specs = [pl.BlockSpec((TM, w), row) for w in widths] + [pl.BlockSpec((TM, 256), row)]
    return pl.pallas_call(
        functools.partial(_inproj_even_kernel, n_prompt_tiles=npt),
        grid=(n // TM,),
        in_specs=_token_specs(npt, d) + [
            _mod_spec(SHIFT_MIX), _mod_spec(SCALE_MIX),
            pl.BlockSpec((1, d), const),
            pl.BlockSpec(w_main.shape, const), pl.BlockSpec(w_la.shape, const),
            pl.BlockSpec(w_gate.shape, const), pl.BlockSpec(b_gate.shape, const)],
        out_specs=out_specs, out_shape=out_shape,
        compiler_params=_cparams(("parallel",)),
        name="inproj_even",
    )(xp, xs, mods, mods, g, w_main, w_la, w_gate, b_gate)


def _rope(x, cos, sin_signed):
    t, w = x.shape
    lane = lax.broadcasted_iota(jnp.int32, (1, w), 1)
    first_half = (lane & 63) < 32
    rot = jnp.where(first_half, pltpu.roll(x, w - 32, 1), pltpu.roll(x, 32, 1))
    reps = w // 128
    return x * jnp.tile(cos, (1, reps)) + rot * jnp.tile(sin_signed, (1, reps))


def _unpack_pairs(slabs, dtype):
    lo = [pltpu.bitcast(s << 16, F32) for s in slabs]
    hi = [pltpu.bitcast(s & jnp.int32(-65536), F32) for s in slabs]
    return jnp.concatenate(lo + hi, axis=1).astype(dtype)


def _pack_pairs(x):
    bits = pltpu.bitcast(x.astype(BF16).astype(F32), jnp.int32)
    half = x.shape[1] // 2
    packed = ((bits[:, :half] >> 16) & jnp.int32(0xFFFF)) | (bits[:, half:] & jnp.int32(-65536))
    return [packed[:, 128 * s:128 * (s + 1)] for s in range(half // 128)]


def _add_moe(xn_ref, z_ref, gate_ref):
    y = _unpack_pairs([z_ref[s] for s in range(z_ref.shape[0])], F32)
    return xn_ref[...] + _group_affine(y, gate_ref[...], None)


def _rope_tables(lp, ls_, bp, bs):
    assert PAST_LEN + ls_ <= lp and lp % 128 == 0 and bs * ls_ == TM
    half = DHC // 2
    inv = ROPE_THETA ** (-jnp.arange(half, dtype=F32) / half)
    inv = jnp.tile(inv, 128 // half)
    sign = jnp.asarray(np.tile(np.repeat([-1.0, 1.0], half), 128 // DHC), F32)
    a = jnp.asarray(np.arange(lp // 128) * 128, F32)[:, None] * inv[None, :]
    b = jnp.asarray(np.arange(128), F32)[:, None] * inv[None, :]
    ca, sa, cb, sb = jnp.cos(a)[:, None], jnp.sin(a)[:, None], jnp.cos(b)[None], jnp.sin(b)[None]
    cos = (ca * cb - sa * sb).reshape(lp, 128)
    sin = ((sa * cb + ca * sb) * sign).reshape(lp, 128)
    with_sample = lambda t: jnp.concatenate([t, jnp.tile(t[PAST_LEN:PAST_LEN + ls_], (bs, 1))], axis=0)
    tiles = lp // TM
    return with_sample(cos), with_sample(sin), lambda i: (jnp.where(i < bp * tiles, i % tiles, tiles), 0)


def _inproj_odd_kernel(xn_hbm, z_ref, gate_ref, sh_ref, sc_ref, g_ref, cos_ref, sin_ref, w_ref,
                       x_ref, q_ref, kv_ref, xbuf, xsem):
    xn_ref = _ring_block([xn_hbm], xbuf, xsem)
    t = xn_ref.shape[0]
    halves = [slice(0, t // 2), slice(t // 2, t)]
    grp = [slice(0, t // (2 * CHUNK)), slice(t // (2 * CHUNK), t // CHUNK)]
    gate, shift, scale = gate_ref[...], sh_ref[...], sc_ref[...]
    xs = []
    for rs, gs in zip(halves, grp):
        y = _unpack_pairs([z_ref[s, rs, :] for s in range(z_ref.shape[0])], F32)
        xs.append(xn_ref[rs, :] + _group_affine(y, gate[gs], None))
    for rs, x in zip(halves, xs):
        x_ref[rs, :] = x
    hbs = [_norm_mod(x, g_ref[...], shift[gs], scale[gs]).astype(BF16) for x, gs in zip(xs, grp)]
    qs = [_dot(hb, w_ref[:, 0:1024]) for hb in hbs]
    ks = [_dot(hb, w_ref[:, 1024:1280]) for hb in hbs]
    vs = [_dot(hb, w_ref[:, 1280:1536]) for hb in hbs]
    for rs, q, k, v in zip(halves, qs, ks, vs):
        cos, sin = cos_ref[rs, :], sin_ref[rs, :]
        q_ref[rs, :] = (_rope(q, cos, sin) * (DHC ** -0.5 * LOG2E)).astype(BF16)
        kv_ref[rs, 0:256] = _rope(k, cos, sin).astype(BF16)
        kv_ref[rs, 256:512] = v.astype(BF16)


def _inproj_odd_call(xn, z, mods_prev, mods, g, cos, sin, rope_map, w):
    n, d = xn.shape
    row = lambda i: (i, 0)
    const = lambda i: (0, 0)
    widths = (1024, 512)
    return pl.pallas_call(
        _inproj_odd_kernel,
        grid=(n // TM,),
        in_specs=[pl.BlockSpec(memory_space=pl.ANY), pl.BlockSpec((z.shape[0], TM, 128), lambda i: (0, i, 0)),
                  _mod_spec(GATE_FFN), _mod_spec(SHIFT_MIX), _mod_spec(SCALE_MIX),
                  pl.BlockSpec((1, d), const),
                  pl.BlockSpec((TM, 128), rope_map), pl.BlockSpec((TM, 128), rope_map),
                  pl.BlockSpec(w.shape, const)],
        out_specs=[pl.BlockSpec((TM, d), row)] + [pl.BlockSpec((TM, wd), row) for wd in widths],
        out_shape=[jax.ShapeDtypeStruct((n, d), F32)] + [jax.ShapeDtypeStruct((n, wd), BF16) for wd in widths],
        scratch_shapes=_ring_scratch(d),
        compiler_params=_cparams(("arbitrary",)),
        name="inproj_odd",
    )(xn, z, mods_prev, mods, mods, g, cos, sin, w)


def _gla_tri():
    t = np.arange(CHUNK)[:, None]
    s = np.arange(CHUNK)[None, :]
    cum = s <= t
    start = s < (t // SUB) * SUB
    end = s < (t // SUB + 1) * SUB
    return jnp.asarray(np.concatenate([cum, start, end], axis=0).astype(np.float32), dtype=BF16)


def _gla_kernel(q_ref, k_ref, v_ref, g_ref, r_ref, s0_ref, gn_ref, tri_ref, o_ref, sout_ref, s_ref, *, nb):
    c_ = CHUNK
    nsub = c_ // SUB

    @pl.when(pl.program_id(1) == 0)
    def _():
        s_ref[...] = s0_ref[0]

    tri = tri_ref[...]
    lane = lax.broadcasted_iota(jnp.int32, (1, 128), 1)
    hmask = [jnp.where(lane < DKA, 1.0, 0.0), jnp.where(lane >= DKA, 1.0, 0.0)]
    ti = lax.broadcasted_iota(jnp.int32, (c_, c_), 0)
    si = lax.broadcasted_iota(jnp.int32, (c_, c_), 1)
    rb, cb = ti >> 4, si >> 4
    m_diag = (rb == cb) & (si <= ti)
    m_off = [(cb == j) & (rb > j) for j in range(nsub - 1)]
    hk = HA * DKA
    gn = gn_ref[...]

    chunks = range(nb)
    heads = [(p, hh) for p in range(HA // 2) for hh in range(2)]
    rows = [slice(c * c_, (c + 1) * c_) for c in chunks]
    pair = [slice(128 * p, 128 * (p + 1)) for p in range(HA // 2)]
    css = []
    for c in chunks:
        g_hi, g_lo = _split(g_ref[rows[c], :])
        css.append(_dot(tri, g_hi) + _dot(tri, g_lo))
    lhs1, lhs2, kds, kes, q_inter, klts, dcols = [], [], [], [], [], [], []
    for c in chunks:
        b, rs, re = css[c][0:c_], css[c][c_:2 * c_], css[c][2 * c_:3 * c_]
        q = q_ref[rows[c], :].astype(F32)
        k = k_ref[rows[c], :].astype(F32)
        bl = b[c_ - 1:c_, :]
        qd = q * jnp.exp(b - rs)
        kd = k * jnp.exp(rs - b)
        ke = k * jnp.exp(re - b)
        qi = q * jnp.exp(b)
        kl = k * jnp.exp(bl - b)
        ql = [q * jnp.exp(jnp.minimum(b - b[SUB * (j + 1) - 1:SUB * (j + 1), :], 0.0)) for j in range(nsub - 1)]
        dcols.append(jnp.broadcast_to(jnp.exp(bl), (8, hk)).T[:, 0:1])
        kds.append([(kd[:, pair[p]] * hmask[hh]).astype(BF16) for p, hh in heads])
        kes.append([(ke[:, pair[p]] * hmask[hh]).astype(BF16) for p, hh in heads])
        klts.append([kl[:, ls].T.astype(BF16) for ls in pair])
        lhs1.append([qd[:, ls].astype(BF16) for ls in pair])
        lhs2.append([jnp.concatenate([ql[j][:, ls] for j in range(nsub - 1)], axis=0).astype(BF16) for ls in pair])
        q_inter.append([(qi[:, pair[p]] * hmask[hh]).astype(BF16) for p, hh in heads])
    a1s = [[_dot_nt(lhs1[c][p], kds[c][h]) for h, (p, hh) in enumerate(heads)] for c in chunks]
    a2s = [[_dot_nt(lhs2[c][p], kes[c][h]) for h, (p, hh) in enumerate(heads)] for c in chunks]
    atts = []
    for c in chunks:
        per_head = []
        for h in range(HA):
            att = jnp.zeros((c_, c_), F32)
            for j in reversed(range(nsub - 1)):
                att = jnp.where(m_off[j], a2s[c][h][j * c_:(j + 1) * c_], att)
            per_head.append(jnp.where(m_diag, a1s[c][h], att).astype(BF16))
        atts.append(per_head)
    vs_ = [[v_ref[rows[c], DVA * h:DVA * (h + 1)] for h in range(HA)] for c in chunks]
    o_intra = [[_dot(atts[c][h], vs_[c][h]) for h in range(HA)] for c in chunks]
    upds = [jnp.concatenate([_dot(klts[c][p][DKA * hh:DKA * (hh + 1)], vs_[c][2 * p + hh]) for p, hh in heads], axis=0)
            for c in chunks]

    s_cur = s_ref[...]
    s_in = []
    for c in chunks:
        s_in.append(s_cur.astype(BF16))
        s_cur = dcols[c] * s_cur + upds[c]
    s_ref[...] = s_cur
    sout_ref[0] = s_cur

    for c in chunks:
        for h in range(HA):
            o = o_intra[c][h] + _dot(q_inter[c][h], s_in[c][pair[h // 2], :])
            ms = jnp.mean(o * o, axis=-1, keepdims=True)
            vs = slice(DVA * h, DVA * (h + 1))
            rr = r_ref[rows[c], vs].astype(F32)
            o_ref[rows[c], vs] = (o * lax.rsqrt(ms + EPS) * gn * (rr * _sigmoid(rr))).astype(BF16)


def _gla_call(q, k, v, g, r, s0, gn, o_prev, *, n_seq, seq_rows, row0, nb):
    tq = nb * CHUNK
    steps = seq_rows // tq
    blk0 = row0 // tq
    row = lambda b, j: (blk0 + b * steps + j, 0)
    const = lambda b, j: (0, 0)
    tri = _gla_tri()
    in_specs = [pl.BlockSpec((tq, 256), row), pl.BlockSpec((tq, 256), row), pl.BlockSpec((tq, 512), row),
                pl.BlockSpec((tq, 256), row), pl.BlockSpec((tq, 512), row),
                pl.BlockSpec((1, 256, 128), lambda b, j: (b, 0, 0)),
                pl.BlockSpec((1, 128), const), pl.BlockSpec(tri.shape, const)]
    args = [q, k, v, g, r, s0, gn, tri]
    aliases = {}
    if o_prev is not None:
        in_specs.append(pl.BlockSpec(memory_space=pl.ANY))
        args.append(o_prev)
        aliases = {len(args) - 1: 0}
    kern = functools.partial(_gla_kernel, nb=nb)
    if o_prev is not None:
        kern = _drop_arg(kern, 8)
    return pl.pallas_call(
        kern,
        grid=(n_seq, steps),
        in_specs=in_specs,
        out_specs=[pl.BlockSpec((tq, 512), row), pl.BlockSpec((1, 256, 128), lambda b, j: (b, 0, 0))],
        out_shape=[jax.ShapeDtypeStruct((q.shape[0], 512), BF16), jax.ShapeDtypeStruct((n_seq, 256, 128), F32)],
        scratch_shapes=[pltpu.VMEM((256, 128), F32)],
        input_output_aliases=aliases,
        compiler_params=_cparams(("arbitrary", "arbitrary")),
        name="gla",
    )(*args)


def _drop_arg(fn, idx):
    def wrapped(*refs):
        return fn(*refs[:idx], *refs[idx + 1:])
    return wrapped


def _window(prev_ref, cur_ref, lo, hi, pb, ls):
    if lo < pb:
        return jnp.concatenate([prev_ref[lo:pb, ls], cur_ref[0:hi - pb, ls]], axis=0)
    return cur_ref[lo - pb:hi - pb, ls]


def _band_kernel(q_ref, kp_ref, kc_ref, vp_ref, vc_ref, bias_ref, o_ref, *, g, n_sub, pb):
    qs = CHUNK * g
    kw_rows = pb + qs
    lane = lax.broadcasted_iota(jnp.int32, (1, 128), 1)
    low = lane < DHB
    hmask = [jnp.where(low, 1.0, 0.0), jnp.where(low, 0.0, 1.0)]
    for s in range(n_sub):
        sb = s if bias_ref.shape[0] > 1 else 0
        rows = slice(qs * s, qs * (s + 1))
        lanes = [slice(128 * p, 128 * (p + 1)) for p in range(HB // 2)]
        heads = [(p, hh) for p in range(HB // 2) for hh in range(2)]
        qps = [q_ref[rows, ls].astype(F32) for ls in lanes]
        kws = [_window(kp_ref, kc_ref, qs * s, qs * s + kw_rows, pb, ls) for ls in lanes]
        vws = [_window(vp_ref, vc_ref, qs * s, qs * s + kw_rows, pb, ls) for ls in lanes]
        qq = [jnp.concatenate([(qps[p] * hmask[hh]).astype(BF16) for hh in range(2)], axis=0) for p in range(HB // 2)]
        sc2 = [_dot_nt(qq[p], kws[p]) for p in range(HB // 2)]
        scs = [sc2[p][qs * hh:qs * (hh + 1)] + bias_ref[sb, 2 * p + hh] for p, hh in heads]
        pes = [jnp.exp2(sc - jnp.max(sc, axis=-1, keepdims=True)) for sc in scs]
        pp = [jnp.concatenate([pes[2 * p + hh].astype(BF16) for hh in range(2)], axis=0) for p in range(HB // 2)]
        o2 = [_dot(pp[p], vws[p]) for p in range(HB // 2)]
        outs = [o2[p][qs * hh:qs * (hh + 1)] / jnp.sum(pes[2 * p + hh], axis=-1, keepdims=True) for p, hh in heads]
        for p, ls in enumerate(lanes):
            o_ref[rows, ls] = jnp.where(low, outs[2 * p], outs[2 * p + 1]).astype(BF16)


def _band_valid(g, pb, n_sub=None):
    rows, kw = CHUNK * g, pb + CHUNK * g
    r = np.arange(rows)[:, None]
    c = np.arange(kw)[None, :]
    dd = c // CHUNK - r // CHUNK
    band = (dd >= 0) & (dd <= pb // CHUNK)
    if n_sub is None:
        return band[None]
    return np.stack([band & (c >= pb - rows * s) for s in range(n_sub)])


def _band_bias(table, g, pb, valid):
    rows, kw = CHUNK * g, pb + CHUNK * g
    period = kw + rows
    m = np.arange(period)
    m = np.where(m < kw, m, m - period)
    ext = table[:, np.clip(m - pb, -MAX_REL, MAX_REL) + MAX_REL] * LOG2E
    flat = jnp.tile(ext, (1, rows))[:, :rows * (period - 1)]
    bias = flat.reshape(table.shape[0], rows, period - 1)[:, :, :kw]
    return jnp.where(valid[:, None], bias[None], -jnp.inf)


def _split_kv(fn, kv_width):
    def wrapped(q_ref, kvp_ref, kvc_ref, *rest):
        kl, vl = slice(0, kv_width), slice(kv_width, 2 * kv_width)
        return fn(q_ref, kvp_ref.at[:, kl], kvc_ref.at[:, kl], kvp_ref.at[:, vl], kvc_ref.at[:, vl], *rest)
    return wrapped


def _attn_call(kernel, q, kp, kc, vp, vc, extra, extra_specs, o_prev, *, width, kv_width, tq, pb,
               n_blocks, blk_map, prev_map, name):
    row = lambda i: (blk_map(i), 0)
    prev = lambda i: (prev_map(i), 0)
    if vp is None:
        in_specs = [pl.BlockSpec((tq, width), row),
                    pl.BlockSpec((pb, 2 * kv_width), prev), pl.BlockSpec((tq, 2 * kv_width), row)] + extra_specs
        args = [q, kp, kc] + extra
        kernel = _split_kv(kernel, kv_width)
    else:
        in_specs = [pl.BlockSpec((tq, width), row),
                    pl.BlockSpec((pb, kv_width), prev), pl.BlockSpec((tq, kv_width), row),
                    pl.BlockSpec((pb, kv_width), prev), pl.BlockSpec((tq, kv_width), row)] + extra_specs
        args = [q, kp, kc, vp, vc] + extra
    aliases = {}
    if o_prev is not None:
        in_specs.append(pl.BlockSpec(memory_space=pl.ANY))
        args.append(o_prev)
        aliases = {len(args) - 1: 0}
        kernel = _drop_arg(kernel, len(args) - 1)
    return pl.pallas_call(
        kernel,
        grid=(n_blocks,),
        in_specs=in_specs,
        out_specs=pl.BlockSpec((tq, width), row),
        out_shape=jax.ShapeDtypeStruct((q.shape[0], width), BF16),
        input_output_aliases=aliases,
        compiler_params=_cparams(("parallel",)),
        name=name,
    )(*args)


def _attention(kernel_fn, q, k, v, cache_k, cache_v, masks, extra, extra_specs, *, width, kv_width, pb, tq, g,
               bp, lp, bs, name):
    bps = lp // tq
    n_sub = tq // (CHUNK * g)
    spec = lambda a: [pl.BlockSpec(a.shape, lambda i: (0,) * a.ndim)]
    kern = functools.partial(kernel_fn, g=g, n_sub=n_sub, pb=pb)
    common = dict(width=width, kv_width=kv_width, pb=pb)
    main = lambda i: (i // (bps - 1)) * bps + i % (bps - 1) + 1
    o = _attn_call(kern, q, k, k, v, v, [masks[0]] + extra, spec(masks[0]) + extra_specs, None, tq=tq,
                   n_blocks=bp * (bps - 1), blk_map=main, prev_map=lambda i: main(i) * (tq // pb) - 1,
                   name=name + "_main", **common)
    first = lambda i: i * bps
    o = _attn_call(kern, q, k, k, v, v, [masks[1]] + extra, spec(masks[1]) + extra_specs, o, tq=tq,
                   n_blocks=bp, blk_map=first, prev_map=lambda i: jnp.maximum(first(i) * (tq // pb) - 1, 0),
                   name=name + "_first", **common)
    samp = functools.partial(kernel_fn, g=1, n_sub=1, pb=pb)
    return _attn_call(samp, q, cache_k, k, cache_v, v, [masks[2]] + extra, spec(masks[2]) + extra_specs, o, tq=CHUNK,
                      n_blocks=bs, blk_map=lambda i: bp * lp // CHUNK + i, prev_map=lambda i: i,
                      name=name + "_sample", **common)


def _swa_kernel(q_ref, kp_ref, kc_ref, vp_ref, vc_ref, mask_ref, sink_ref, o_ref, *, g, n_sub, pb):
    qs = CHUNK * g
    kw_rows = pb + qs
    lane = lax.broadcasted_iota(jnp.int32, (1, 128), 1)
    low = lane < DHC
    hmask = [jnp.where(low, 1.0, 0.0), jnp.where(low, 0.0, 1.0)]
    pairs_per_kv = HC // KVC // 2
    for s in range(n_sub):
        msk = mask_ref[s if mask_ref.shape[0] > 1 else 0]
        rows = slice(qs * s, qs * (s + 1))
        kws = [_window(kp_ref, kc_ref, qs * s, qs * s + kw_rows, pb, slice(128 * kv, 128 * (kv + 1))) for kv in range(KVC)]
        vws = [_window(vp_ref, vc_ref, qs * s, qs * s + kw_rows, pb, slice(128 * kv, 128 * (kv + 1))) for kv in range(KVC)]
        heads = [(j, hh) for j in range(HC // 2) for hh in range(2)]
        qps = [q_ref[rows, 128 * j:128 * (j + 1)].astype(F32) for j in range(HC // 2)]
        per_kv = 2 * pairs_per_kv
        qq = [jnp.concatenate([(qps[j] * hmask[hh]).astype(BF16) for j, hh in heads[per_kv * kv:per_kv * (kv + 1)]],
                              axis=0) for kv in range(KVC)]
        sc2 = [_dot_nt(qq[kv], kws[kv]) for kv in range(KVC)]
        scs = [sc2[u // per_kv][qs * (u % per_kv):qs * (u % per_kv + 1)] + msk for u in range(len(heads))]
        sks = [sink_ref[0, 2 * j + hh] for j, hh in heads]
        ms = [jnp.maximum(jnp.max(sc, axis=-1, keepdims=True), sk) for sc, sk in zip(scs, sks)]
        pes = [jnp.exp2(sc - m) for sc, m in zip(scs, ms)]
        pp = [jnp.concatenate([pe.astype(BF16) for pe in pes[per_kv * kv:per_kv * (kv + 1)]], axis=0) for kv in range(KVC)]
        o2 = [_dot(pp[kv], vws[kv]) for kv in range(KVC)]
        outs = [o2[u // per_kv][qs * (u % per_kv):qs * (u % per_kv + 1)]
                / (jnp.sum(pes[u], axis=-1, keepdims=True) + jnp.exp2(sks[u] - ms[u])) for u in range(len(heads))]
        for j in range(HC // 2):
            o_ref[rows, 128 * j:128 * (j + 1)] = jnp.where(low, outs[2 * j], outs[2 * j + 1]).astype(BF16)


def _route(logits_t):
    a = [logits_t[4 * j:4 * j + 4] for j in range(EXP_PER_GROUP)]

    def first_argmax(vals, m):
        idx = jnp.full(m.shape, float(len(vals) - 1), F32)
        for j in reversed(range(len(vals) - 1)):
            idx = jnp.where(vals[j] == m, float(j), idx)
        return idx

    m1 = functools.reduce(jnp.maximum, a)
    i1 = first_argmax(a, m1)
    bsec = [jnp.where(i1 == float(j), -jnp.inf, a[j]) for j in range(EXP_PER_GROUP)]
    m2 = functools.reduce(jnp.maximum, bsec)
    i2 = first_argmax(bsec, m2)
    rows = lambda x: [x[gi:gi + 1] for gi in range(N_GROUPS)]
    gm = functools.reduce(jnp.maximum, rows(m1))
    gscore = jnp.exp(m1 - gm) + jnp.exp(m2 - gm)
    gs = rows(gscore)
    gsel = first_argmax(gs, functools.reduce(jnp.maximum, gs))

    def pick(x):
        xr = rows(x)
        out = xr[N_GROUPS - 1]
        for gi in reversed(range(N_GROUPS - 1)):
            out = jnp.where(gsel == float(gi), xr[gi], out)
        return out

    p1 = jnp.exp(pick(m1) - gm)
    p2 = jnp.exp(pick(m2) - gm)
    w1 = p1 / (p1 + p2)
    w2 = p2 / (p1 + p2)
    s1, s2 = pick(i1), pick(i2)
    lo, hi = jnp.minimum(s1, s2), jnp.maximum(s1, s2)
    pair = jnp.where(lo == 0.0, hi - 1.0, jnp.where(lo == 1.0, hi + 1.0, 5.0))
    bucket = gsel * float(N_PAIRS) + pair
    first_is_lo = s1 < s2
    return bucket, jnp.where(first_is_lo, w1, w2), jnp.where(first_is_lo, w2, w1)


def _outproj_kernel(*refs, n_x, n_o, n_prompt_tiles):
    x_refs = refs[:n_x]
    o_refs = refs[n_x:n_x + n_o]
    w_refs = refs[n_x + n_o:n_x + 2 * n_o]
    (gate_ref, nf_ref, sh_ref, sc_ref, wr_ref, br_ref, tri_ref,
     xn_ref, disp_ref, meta_ref, cnt_ref, run_ref, xbuf, xsem) = refs[n_x + 2 * n_o:]
    t = xn_ref.shape[0]

    @pl.when(pl.program_id(0) == 0)
    def _():
        run_ref[...] = jnp.zeros_like(run_ref)

    x_src = _ring_block(list(x_refs), xbuf, xsem, n_prompt_tiles)

    halves = [slice(0, t // 2), slice(t // 2, t)]
    grp = [slice(0, t // (2 * CHUNK)), slice(t // (2 * CHUNK), t // CHUNK)]
    ys = []
    for rs in halves:
        y = _dot(o_refs[0][rs, :], w_refs[0][...])
        for i in range(1, n_o):
            y = y + _dot(o_refs[i][rs, :], w_refs[i][...])
        ys.append(y)
    gate, shift, scale = gate_ref[...], sh_ref[...], sc_ref[...]
    gys = [_group_affine(y, gate[gs], None) for y, gs in zip(ys, grp)]

    for rs, gy in zip(halves, gys):
        xn_ref[rs, :] = x_src[rs, :] + gy
    hs = [_norm_mod(xn_ref[rs, :], nf_ref[...], shift[gs], scale[gs]) for rs, gs in zip(halves, grp)]
    for rs, h in zip(halves, hs):
        for s, slab in enumerate(_pack_pairs(h)):
            disp_ref[s, rs, :] = slab
    logits_t = [(_dot3_narrow(h, wr_ref[...]) + br_ref[...]).T[0:N_EXPERTS] for h in hs]
    bucket, w_lo, w_hi = _route(jnp.concatenate(logits_t, axis=1))
    r128 = lax.broadcasted_iota(jnp.int32, (128, t), 0)
    tok = (pl.program_id(0) * t + lax.broadcasted_iota(jnp.int32, (1, t), 1)).astype(F32)
    aux = jnp.where(r128 == 0, w_lo, jnp.where(r128 == 1, w_hi, jnp.where(r128 == 2, tok, 0.0))).T
    disp_ref[disp_ref.shape[0] - 1] = pltpu.bitcast(aux, jnp.int32)
    brow = lax.broadcasted_iota(jnp.int32, (BUCKET_ROWS, t), 0).astype(F32)
    onehot = jnp.where(brow == bucket, 1.0, 0.0)
    before = _dot(onehot.astype(BF16), tri_ref[...]) + run_ref[:, 0:1]
    rank = jnp.sum(onehot * before, axis=0, keepdims=True)
    run_ref[...] = run_ref[...] + jnp.sum(onehot, axis=1, keepdims=True)
    cnt_ref[...] = run_ref[...]
    r8 = lax.broadcasted_iota(jnp.int32, (8, t), 0)
    meta_ref[...] = jnp.where(r8 == 0, bucket, jnp.where(r8 == 1, rank, 0.0)).astype(jnp.int32)


def _outproj_call(xs_, os_, ws, mods, nf, wr, br, n_pad):
    d = xs_[0].shape[1]
    n = sum(a.shape[0] for a in xs_)
    npt = xs_[0].shape[0] // TM
    row = lambda i: (i, 0)
    const = lambda i: (0, 0)
    n_o = len(os_)
    in_specs = ([pl.BlockSpec(memory_space=pl.ANY) for _ in xs_]
                + [pl.BlockSpec((TM, o.shape[1]), row) for o in os_]
                + [pl.BlockSpec(w.shape, const) for w in ws]
                + [_mod_spec(GATE_MIX), pl.BlockSpec((1, d), const),
                   _mod_spec(SHIFT_FFN), _mod_spec(SCALE_FFN),
                   pl.BlockSpec(wr.shape, const), pl.BlockSpec(br.shape, const),
                   pl.BlockSpec((TM, TM), const)])
    tri = jnp.asarray(np.triu(np.ones((TM, TM), np.float32), k=1), dtype=BF16)
    return pl.pallas_call(
        functools.partial(_outproj_kernel, n_x=len(xs_), n_o=n_o, n_prompt_tiles=npt),
        grid=(n // TM,),
        in_specs=in_specs,
        out_specs=[pl.BlockSpec((TM, d), row), pl.BlockSpec((DISP_SLABS, TM, 128), lambda i: (0, i, 0)),
                   pl.BlockSpec((8, TM), lambda i: (0, i)), pl.BlockSpec((BUCKET_ROWS, 128), const)],
        out_shape=[jax.ShapeDtypeStruct((n, d), F32), jax.ShapeDtypeStruct((DISP_SLABS, n_pad, 128), jnp.int32),
                   jax.ShapeDtypeStruct((8, n), jnp.int32), jax.ShapeDtypeStruct((BUCKET_ROWS, 128), F32)],
        scratch_shapes=[pltpu.VMEM((BUCKET_ROWS, 128), F32)] + _ring_scratch(d),
        compiler_params=_cparams(("arbitrary",)),
        name="outproj_router",
    )(*xs_, *os_, *ws, mods, nf, mods, mods, wr, br, tri)


def _sc_mesh():
    return plsc.VectorSubcoreMesh(core_axis_name="core", subcore_axis_name="subcore")


def _sc_scatter_rows(src, idx, n_out):
    r = idx.shape[0]
    k = SC_GROUP
    w_per = r // (SC_WINDOW * SC_WORKERS)
    assert idx.shape == (src.shape[0],) and r % (SC_WINDOW * SC_WORKERS) == 0 and w_per % k == 0
    n_groups = w_per // k

    @functools.partial(
        pl.kernel, out_type=jax.ShapeDtypeStruct((n_out, 128), src.dtype), mesh=_sc_mesh(),
        scratch_types=[pltpu.VMEM((w_per, SC_WINDOW), jnp.int32),
                       pltpu.VMEM((2 * k, SC_WINDOW, 128), src.dtype),
                       pltpu.SemaphoreType.DMA((2,)), pltpu.SemaphoreType.DMA((2,))])
    def copy(x_hbm, i_hbm, o_hbm, ibuf, xbuf, in_sem, out_sem):
        wid = lax.axis_index("core") * (SC_WORKERS // 2) + lax.axis_index("subcore")
        pltpu.sync_copy(i_hbm.at[wid], ibuf)
        first = wid * w_per

        def start_in(g, slot):
            return [pltpu.async_copy(x_hbm.at[pl.ds((first + g * k + c) * SC_WINDOW, SC_WINDOW)],
                                     xbuf.at[slot * k + c], in_sem.at[slot]) for c in range(k)]

        def start_out(g, slot):
            return [pltpu.async_copy(xbuf.at[slot * k + c], o_hbm.at[ibuf.at[g * k + c]], out_sem.at[slot])
                    for c in range(k)]

        pending_in = start_in(0, 0)
        for g in range(n_groups):
            slot = g % 2
            for cp in pending_in:
                cp.wait()
            pending_out = start_out(g, slot)
            if g + 1 < n_groups:
                pending_in = start_in(g + 1, 1 - slot)
            for cp in pending_out:
                cp.wait()

    return copy(src, idx.reshape(SC_WORKERS, w_per, SC_WINDOW))


def _moe_kernel(elo_ref, ehi_ref, nvalid_ref, xs_ref, *refs, n_tok, dump_tiles):
    w_refs, (y_ref, tok_ref) = refs[:2 * MOE_TILES], refs[2 * MOE_TILES:]
    d_in = w_refs[0].shape[2] - D_FF
    step = pl.program_id(0)
    t = TMO
    tiles = range(MOE_TILES)
    rows = [slice(t * j, t * (j + 1)) for j in tiles]
    auxs = [pltpu.bitcast(xs_ref[Y_SLABS, rows[j], :], F32) for j in tiles]
    r = lax.broadcasted_iota(jnp.int32, (1, t), 1)
    for j in tiles:
        i = step * MOE_TILES + j
        spare = n_tok + (i % dump_tiles) * t + r
        tok = jnp.where(r < nvalid_ref[i], auxs[j].T[2:3, :].astype(jnp.int32), spare)
        for c in range(t // 128):
            tok_ref[j, c:c + 1, :] = tok[:, 128 * c:128 * (c + 1)]

    any_tokens = nvalid_ref[step * MOE_TILES] > 0
    for j in range(1, MOE_TILES):
        any_tokens = jnp.logical_or(any_tokens, nvalid_ref[step * MOE_TILES + j] > 0)

    @pl.when(any_tokens)
    def _():
        units = [(j, e) for j in tiles for e in range(2)]
        hs = [_unpack_pairs([xs_ref[s, rows[j], :] for s in range(Y_SLABS)], BF16) for j in tiles]
        abs_ = [_dot(hs[j], w_refs[2 * j + e][0, 0, 0:d_in, :]) for j, e in units]
        acts = [(ab[:, :D_FF] * _sigmoid(ab[:, :D_FF]) * ab[:, D_FF:]).astype(BF16) for ab in abs_]
        ys = [_dot(act, w_refs[2 * j + e][0, 0, d_in:d_in + D_FF, :]) for act, (j, e) in zip(acts, units)]
        for j in tiles:
            acc = auxs[j][:, 0:1] * ys[2 * j] + auxs[j][:, 1:2] * ys[2 * j + 1]
            for s, slab in enumerate(_pack_pairs(acc)):
                y_ref[s, rows[j], :] = slab

    @pl.when(jnp.logical_not(any_tokens))
    def _():
        y_ref[...] = jnp.zeros_like(y_ref)


def _moe_call(xs, elo, ehi, nvalid, wexp, n_tiles, n_tok, dump_tiles):
    m = MOE_TILES
    assert n_tiles % m == 0 and wexp.shape[3] == 2 * D_FF
    weight_specs = [pl.BlockSpec((1, 1) + wexp.shape[2:],
                                 lambda i, lo, hi, v, j=j, sel=sel: (0, (lo, hi)[sel][m * i + j], 0, 0))
                    for j in range(m) for sel in range(2)]
    weights = [wexp] * (2 * m)
    return pl.pallas_call(
        functools.partial(_moe_kernel, n_tok=n_tok, dump_tiles=dump_tiles),
        grid_spec=pltpu.PrefetchScalarGridSpec(
            num_scalar_prefetch=3,
            grid=(n_tiles // m,),
            in_specs=[pl.BlockSpec((DISP_SLABS, m * TMO, 128), lambda i, lo, hi, v: (0, i, 0))] + weight_specs,
            out_specs=[pl.BlockSpec((Y_SLABS, m * TMO, 128), lambda i, lo, hi, v: (0, i, 0)),
                       pl.BlockSpec((m, TMO // 128, 128), lambda i, lo, hi, v: (i, 0, 0))]),
        out_shape=[jax.ShapeDtypeStruct((Y_SLABS, n_tiles * TMO, 128), jnp.int32),
                   jax.ShapeDtypeStruct((n_tiles, TMO // 128, 128), jnp.int32)],
        compiler_params=_cparams(("arbitrary",), vmem_mb=VMEM_LIMIT_MOE_MB),
        name="moe_grouped",
    )(elo, ehi, nvalid, xs, *weights)


def _after(x, token):
    return lax.optimization_barrier((x, token))[0]


def _cast_kernel(after_ref, wgu_ref, wdn_ref, o_ref):
    k = wgu_ref.shape[2]
    o_ref[0, 0, 0:k, :] = wgu_ref[0, 0].astype(o_ref.dtype)
    o_ref[0, 0, k:, :] = wdn_ref[0, 0].astype(o_ref.dtype)


def _cast_call(w_gate_up, w_down, layer, after):
    _, e, k, n = w_gate_up.shape
    kd = w_down.shape[2]
    assert w_down.shape[3] == n
    return pl.pallas_call(
        _cast_kernel,
        grid=(e,),
        in_specs=[pl.BlockSpec(memory_space=pl.ANY),
                  pl.BlockSpec((1, 1, k, n), lambda i: (layer, i, 0, 0)),
                  pl.BlockSpec((1, 1, kd, n), lambda i: (layer, i, 0, 0))],
        out_specs=pl.BlockSpec((1, 1, k + kd, n), lambda i: (0, i, 0, 0)),
        out_shape=jax.ShapeDtypeStruct((1, e, k + kd, n), BF16),
        compiler_params=_cparams(("parallel",)),
        name="cast_weights",
    )(after, w_gate_up, w_down)


def _moe_layer(disp, meta, counts, w_gate_up, w_down, layer, n, n_pad, sort_rows):
    n_tiles = sort_rows // TMO
    wexp = _cast_call(w_gate_up, w_down, layer, counts)
    cnt = counts[:N_BUCKETS, 0].astype(jnp.int32)
    padded = ((cnt + TMO - 1) // TMO) * TMO
    ends = jnp.cumsum(padded)
    offs = ends - padded
    bucket, rank = meta[0], meta[1]
    pos = rank + jnp.sum(jnp.where(bucket[None, :] == jnp.arange(N_BUCKETS, dtype=jnp.int32)[:, None],
                                   offs[:, None], 0), axis=0)
    tile_start = jnp.arange(n_tiles, dtype=jnp.int32) * TMO
    tile_bucket = jnp.minimum(jnp.sum((tile_start[:, None] >= ends[None, :]).astype(jnp.int32), axis=1), N_BUCKETS - 1)
    pair_lo = np.array([0, 0, 0, 1, 1, 2], np.int32)
    pair_hi = np.array([1, 2, 3, 2, 3, 3], np.int32)
    b_lo = jnp.asarray(np.repeat(np.arange(N_GROUPS), N_PAIRS) * EXP_PER_GROUP + np.tile(pair_lo, N_GROUPS), jnp.int32)
    b_hi = jnp.asarray(np.repeat(np.arange(N_GROUPS), N_PAIRS) * EXP_PER_GROUP + np.tile(pair_hi, N_GROUPS), jnp.int32)
    onehot_tb = (tile_bucket[:, None] == jnp.arange(N_BUCKETS, dtype=jnp.int32)[None, :]).astype(jnp.int32)
    elo = jnp.sum(onehot_tb * b_lo[None, :], axis=1)
    ehi = jnp.sum(onehot_tb * b_hi[None, :], axis=1)
    bucket_end = jnp.sum(onehot_tb * (offs + cnt)[None, :], axis=1)
    nvalid = jnp.where(tile_start < ends[-1], jnp.clip(bucket_end - tile_start, 0, TMO), 0)
    dump = sort_rows + jnp.arange(n_pad - n, dtype=jnp.int32)
    pos_sc = jnp.concatenate([pos, dump])
    total = sort_rows + n_pad - n
    sc_idx = (pos_sc[None, :] + (jnp.arange(DISP_SLABS, dtype=jnp.int32) * total)[:, None]).reshape(-1)
    xs = _sc_scatter_rows(disp.reshape(DISP_SLABS * n_pad, 128), sc_idx, DISP_SLABS * total)
    ys, tok = _moe_call(xs.reshape(DISP_SLABS, total, 128), elo, ehi, nvalid, wexp, n_tiles,
                        n, (n_pad - n) // TMO)
    back_idx = (tok.reshape(1, sort_rows) + (jnp.arange(Y_SLABS, dtype=jnp.int32) * n_pad)[:, None]).reshape(-1)
    z = _sc_scatter_rows(ys.reshape(Y_SLABS * sort_rows, 128), back_idx, Y_SLABS * n_pad)
    return z.reshape(Y_SLABS, n_pad, 128), tok


def _final_kernel(xn_hbm, z_ref, gate_ref, g_ref, yp_ref, ys_ref, xbuf, xsem, *, n_prompt_tiles):
    x = _add_moe(_ring_block([xn_hbm], xbuf, xsem), z_ref, gate_ref)
    ms = jnp.mean(x * x, axis=-1, keepdims=True)
    y = x * lax.rsqrt(ms + EPS) * g_ref[...]
    i = pl.program_id(0)

    @pl.when(i < n_prompt_tiles)
    def _():
        yp_ref[...] = y

    @pl.when(i >= n_prompt_tiles)
    def _():
        ys_ref[...] = y


def _final_call(xn, z, mods, g, n_prompt):
    n, d = xn.shape
    npt = n_prompt // TM
    assert n - n_prompt == TM
    return pl.pallas_call(
        functools.partial(_final_kernel, n_prompt_tiles=npt),
        grid=(n // TM,),
        in_specs=[pl.BlockSpec(memory_space=pl.ANY), pl.BlockSpec((z.shape[0], TM, 128), lambda i: (0, i, 0)),
                  _mod_spec(GATE_FFN), pl.BlockSpec((1, d), lambda i: (0, 0))],
        out_specs=_token_specs(npt, d),
        out_shape=[jax.ShapeDtypeStruct((n_prompt, d), F32), jax.ShapeDtypeStruct((TM, d), F32)],
        scratch_shapes=_ring_scratch(d),
        compiler_params=_cparams(("arbitrary",)),
        name="final_norm",
    )(xn, z, mods, g)


def kernel(x_prompt, x_sample, c_prompt, c_sample, state_gla, cache_band_k, cache_band_v, cache_swa_k, cache_swa_v,
           w_ada, b_ada, norm_mix, norm_ffn, norm_final, w_in_even, w_gate_a, b_gate_a, gla_norm, rel_bias_b,
           w_out_even, w_in_odd, sinks_c, w_out_odd, w_router, b_router, w_gate_up, w_down):
    bp, lp, d = x_prompt.shape
    bs, ls_, _ = x_sample.shape
    n_p, n_s = bp * lp, bs * ls_
    n = n_p + n_s
    assert ls_ == CHUNK and n_s == TM and lp % TM == 0 and PAST_LEN % CHUNK == 0

    xp2, xs2 = x_prompt.reshape(n_p, d), x_sample.reshape(n_s, d)

    c16 = jnp.zeros((SEQ_ROWS, d), F32).at[:bp].set(c_prompt).at[bp:bp + bs].set(c_sample)
    mods = _ada_call(c16, w_ada, b_ada)
    seq_of_group = np.concatenate([np.repeat(np.arange(bp), lp // CHUNK), bp + np.arange(bs)])
    mods_g = [mods[l][seq_of_group] for l in range(DEPTH)]

    perm = np.array([4 * (c % 4) + c // 4 for c in range(N_EXPERTS)])
    wr = jnp.zeros((d, 128), F32).at[:, :N_EXPERTS].set(w_router[:, perm])
    br = jnp.zeros((1, 128), F32).at[0, :N_EXPERTS].set(b_router[perm])

    sc_unit = SC_WINDOW * SC_WORKERS * SC_GROUP
    n_pad = n + TMO
    while (DISP_SLABS * n_pad) % sc_unit or (Y_SLABS * n_pad) % TMO or (n_pad - n) % TMO:
        n_pad += TMO
    sort_rows = n + N_BUCKETS * TMO
    while (Y_SLABS * sort_rows) % sc_unit or sort_rows % (MOE_TILES * TMO):
        sort_rows += TMO

    gla_p = gla_s = bk_p = bv_p = bk_s = bv_s = sk_p = sv_p = sk_s = sv_s = None
    xn = z = tok = None
    for l in range(DEPTH):
        i = l // 2
        if l % 2 == 0:
            w = w_in_even[i]
            w_main = jnp.concatenate([w[:, :1536], w[:, 1552:]], axis=1).astype(BF16)
            w_la = jnp.zeros((d, 128), F32).at[:, :GATE_RANK].set(w[:, 1536:1552]).astype(BF16)
            w_gate = jnp.zeros((128, HA * DKA), F32).at[:GATE_RANK].set(w_gate_a[i])
            qa, ka, va, ra, qb, kb, vb, ga = _inproj_even_call(
                xp2, xs2, mods_g[l], norm_mix[l][None], w_main, w_la, w_gate, b_gate_a[i][None])
            xres = [xp2, xs2]
            gn = gla_norm[i][None]
            oa, s_p = _gla_call(qa, ka, va, ga, ra, jnp.zeros((bp, 256, 128), F32), gn, None,
                                n_seq=bp, seq_rows=lp, row0=0, nb=8)
            oa, s_s = _gla_call(qa, ka, va, ga, ra, state_gla[i].reshape(bs, 256, 128), gn, oa,
                                n_seq=bs, seq_rows=ls_, row0=n_p, nb=1)
            gla_p, gla_s = s_p.reshape(1, bp, HA, DKA, DVA), s_s.reshape(1, bs, HA, DKA, DVA)
            pb = N_PREV_B * CHUNK
            tq, g = 512, 2
            ck = cache_band_k[i].reshape(bs * pb, HB * DHB).astype(BF16)
            cv = cache_band_v[i].reshape(bs * pb, HB * DHB).astype(BF16)
            biases = (_band_bias(rel_bias_b[i], g, pb, _band_valid(g, pb)),
                      _band_bias(rel_bias_b[i], g, pb, _band_valid(g, pb, tq // (CHUNK * g))),
                      _band_bias(rel_bias_b[i], 1, pb, _band_valid(1, pb)))
            ob = _attention(_band_kernel, qb, kb, vb, ck, cv, biases, [], [], width=512, kv_width=512, pb=pb,
                            tq=tq, g=g, bp=bp, lp=lp, bs=bs, name="band")
            tail = lambda a: jnp.stack([a[(b + 1) * lp - pb:(b + 1) * lp] for b in range(bp)]).astype(F32).reshape(1, bp, pb, HB, DHB)
            new = lambda a: a[n_p:].astype(F32).reshape(bs, ls_, HB, DHB)
            bk_p, bv_p = tail(kb), tail(vb)
            bk_s = jnp.concatenate([cache_band_k[i][:, ls_:], new(kb)], axis=1)[None]
            bv_s = jnp.concatenate([cache_band_v[i][:, ls_:], new(vb)], axis=1)[None]
            wo = w_out_even[i].astype(BF16)
            os_, ws = [oa, ob], [wo[:HA * DVA], wo[HA * DVA:]]
        else:
            w = _after(w_in_odd[i], tok)
            w_out_l = _after(w_out_odd[i], tok)
            cache_k_l, cache_v_l = _after(cache_swa_k[i], tok), _after(cache_swa_v[i], tok)
            wk, wv = w[:, 1024:1152], w[:, 1152:1280]
            dup = lambda a: jnp.concatenate([a[:, :64], a[:, :64], a[:, 64:], a[:, 64:]], axis=1)
            w_all = jnp.concatenate([w[:, :1024], dup(wk), dup(wv)], axis=1).astype(BF16)
            cos, sin, rope_map = _rope_tables(lp, ls_, bp, bs)
            x, q, kv = _inproj_odd_call(xn, z, mods_g[l - 1], mods_g[l], norm_mix[l][None], cos, sin, rope_map, w_all)
            k, v = kv[:, :256], kv[:, 256:]
            xres = [x]
            pb = WINDOW
            tq, g = 512, 2
            sink = sinks_c[i][None] * LOG2E
            sink_spec = [pl.BlockSpec(memory_space=pltpu.SMEM)]
            dupc = lambda c: jnp.concatenate([c[:, :, 0], c[:, :, 0], c[:, :, 1], c[:, :, 1]], axis=-1).reshape(bs * pb, 256).astype(BF16)
            ck, cv = dupc(cache_k_l), dupc(cache_v_l)
            additive = lambda valid: jnp.asarray(np.where(valid, 0.0, -np.inf), F32)
            masks = (additive(_band_valid(g, pb)), additive(_band_valid(g, pb, tq // (CHUNK * g))),
                     additive(_band_valid(1, pb)))
            o = _attention(_swa_kernel, q, kv, None, jnp.concatenate([ck, cv], axis=1), None, masks, [sink], sink_spec,
                           width=1024, kv_width=256, pb=pb,
                           tq=tq, g=g, bp=bp, lp=lp, bs=bs, name="swa")
            undup = lambda a: jnp.concatenate([a[:, 0:64], a[:, 128:192]], axis=1).astype(F32)
            tail = lambda a: jnp.stack([undup(a[(b + 1) * lp - pb:(b + 1) * lp]) for b in range(bp)]).reshape(1, bp, pb, KVC, DHC)
            new = lambda a: undup(a[n_p:]).reshape(bs, ls_, KVC, DHC)
            sk_p, sv_p = tail(k), tail(v)
            sk_s = jnp.concatenate([cache_swa_k[i][:, ls_:], new(k)], axis=1)[None]
            sv_s = jnp.concatenate([cache_swa_v[i][:, ls_:], new(v)], axis=1)[None]
            os_, ws = [o], [w_out_l.astype(BF16)]
        xn, disp, meta, counts = _outproj_call(xres, os_, ws, mods_g[l], norm_ffn[l][None], wr, br, n_pad)
        z, tok = _moe_layer(disp, meta, counts, w_gate_up, w_down, l, n, n_pad, sort_rows)

    y_prompt, y_sample = _final_call(xn, z, mods_g[DEPTH - 1], norm_final[None], n_p)
    return (y_prompt.reshape(bp, lp, d), y_sample.reshape(bs, ls_, d),
            gla_p, gla_s, bk_p, bv_p, bk_s, bv_s, sk_p, sv_p, sk_s, sv_s)
```

```python
import functools

import numpy as np
import jax
import jax.numpy as jnp
from jax import lax
from jax.experimental import pallas as pl
from jax.experimental.pallas import tpu as pltpu
from jax.experimental.pallas import tpu_sc as plsc

F32 = jnp.float32
BF16 = jnp.bfloat16

D_MODEL = 1024
DEPTH = 2
CHUNK = 64
PAST_LEN = 4096
HA, DKA, DVA = 4, 64, 128
GATE_RANK = 16
GATE_TAU = 16.0
HB, DHB = 8, 64
N_PREV_B = 8
MAX_REL = 128
HC, KVC, DHC = 16, 2, 64
WINDOW = 128
ROPE_THETA = 10000.0
N_EXPERTS = 16
N_GROUPS = 4
EXP_PER_GROUP = 4
D_FF = 512
EPS = 1e-6

N_PAIRS = 6
N_BUCKETS = N_GROUPS * N_PAIRS
BUCKET_ROWS = 32
Y_SLABS = 4
DISP_SLABS = Y_SLABS + 1
TMO = 256
MOE_TILES = 2
SC_WINDOW = 128
SC_WORKERS = 32
SC_GROUP = 3

TM = 512
SEQ_ROWS = 16
SUB = 16
LOG2E = 1.4426950408889634
VMEM_LIMIT_MB = 48
VMEM_LIMIT_MOE_MB = 56


def _cparams(sem, vmem_mb=VMEM_LIMIT_MB):
    return pltpu.CompilerParams(dimension_semantics=sem, vmem_limit_bytes=vmem_mb * 1024 * 1024)


def _dot(a, b):
    return jnp.dot(a, b, preferred_element_type=F32)


def _dot_nt(a, b):
    return lax.dot_general(a, b, (((1,), (1,)), ((), ())), preferred_element_type=F32)


def _split(a):
    hi = a.astype(BF16)
    lo = (a - hi.astype(F32)).astype(BF16)
    return hi, lo


def _dot3(a, b):
    ah, al = _split(a)
    bh, bl = _split(b)
    return _dot(ah, bh) + _dot(ah, bl) + _dot(al, bh)


def _dot3_narrow(a, b):
    ah, al = _split(a)
    bh, bl = _split(b)
    n = b.shape[1]
    p = _dot(ah, jnp.concatenate([bh, bl], axis=1))
    return p[:, :n] + p[:, n:] + _dot(al, bh)


def _sigmoid(x):
    return 1.0 / (1.0 + jnp.exp(-x))


def _group_affine(y, mul, add):
    parts = []
    for gi in range(y.shape[0] // CHUNK):
        p = y[gi * CHUNK:(gi + 1) * CHUNK]
        if mul is not None:
            p = p * mul[gi:gi + 1]
        if add is not None:
            p = p + add[gi:gi + 1]
        parts.append(p)
    return jnp.concatenate(parts, axis=0)


def _norm_mod(x, g, shift, scale):
    ms = jnp.mean(x * x, axis=-1, keepdims=True)
    return _group_affine(x * lax.rsqrt(ms + EPS) * g, 1.0 + scale, shift)


def _mod_spec(part):
    return pl.BlockSpec((TM // CHUNK, D_MODEL), lambda i: (i, part))


SHIFT_MIX, SCALE_MIX, GATE_MIX, SHIFT_FFN, SCALE_FFN, GATE_FFN = range(6)


def _on_token_tile(xp_ref, xs_ref, n_prompt_tiles, body):
    @pl.when(pl.program_id(0) < n_prompt_tiles)
    def _():
        body(xp_ref)

    @pl.when(pl.program_id(0) >= n_prompt_tiles)
    def _():
        body(xs_ref)


RING = 4


def _ring_block(srcs, buf, sem, n_prompt_tiles=None):
    s = pl.program_id(0)
    n_steps = pl.num_programs(0)
    t = buf.shape[1]

    def copy(src, blk, slot):
        return pltpu.make_async_copy(src.at[pl.ds(pl.multiple_of(blk * t, t), t)], buf.at[slot], sem.at[slot])

    def start(step, slot):
        if len(srcs) == 1:
            copy(srcs[0], step, slot).start()
        elif isinstance(step, int):
            assert step < n_prompt_tiles
            copy(srcs[0], step, slot).start()
        else:
            @pl.when(step < n_prompt_tiles)
            def _():
                copy(srcs[0], step, slot).start()

            @pl.when(step >= n_prompt_tiles)
            def _():
                copy(srcs[1], step - n_prompt_tiles, slot).start()

    @pl.when(s == 0)
    def _():
        for k in range(RING - 1):
            start(k, k)

    ahead = s + (RING - 1)

    @pl.when(ahead < n_steps)
    def _():
        start(ahead, ahead % RING)

    slot = s % RING
    copy(srcs[0], 0, slot).wait()
    return buf.at[slot]


def _ring_scratch(d):
    return [pltpu.VMEM((RING, TM, d), F32), pltpu.SemaphoreType.DMA((RING,))]


def _token_specs(n_prompt_tiles, d):
    return [pl.BlockSpec((TM, d), lambda i: (jnp.minimum(i, n_prompt_tiles - 1), 0)),
            pl.BlockSpec((TM, d), lambda i: (0, 0))]


def _ada_kernel(c_ref, w_ref, b_ref, o_ref):
    c = c_ref[...]
    o_ref[0] = _dot3(c * _sigmoid(c), w_ref[0]) + b_ref[0]


def _ada_call(c16, w_ada, b_ada):
    d = D_MODEL
    tn = 1024
    return pl.pallas_call(
        _ada_kernel,
        grid=(DEPTH, 6 * d // tn),
        in_specs=[pl.BlockSpec((SEQ_ROWS, d), lambda l, j: (0, 0)),
                  pl.BlockSpec((1, d, tn), lambda l, j: (l, 0, j)),
                  pl.BlockSpec((1, 1, tn), lambda l, j: (l, 0, j))],
        out_specs=pl.BlockSpec((1, SEQ_ROWS, tn), lambda l, j: (l, 0, j)),
        out_shape=jax.ShapeDtypeStruct((DEPTH, SEQ_ROWS, 6 * d), F32),
        compiler_params=_cparams(("arbitrary", "arbitrary")),
        name="ada",
    )(c16, w_ada, b_ada.reshape(DEPTH, 1, 6 * d))


def _inproj_even_kernel(xp_ref, xs_ref, sh_ref, sc_ref, g_ref, w_ref, wla_ref, wg_ref, bg_ref,
                        qa_ref, ka_ref, va_ref, ra_ref, qb_ref, kb_ref, vb_ref, ga_ref, *, n_prompt_tiles):
    def body(x_ref):
        t = x_ref.shape[0]
        outs = ((qa_ref, 0, 256, DKA ** -0.5), (ka_ref, 256, 512, None), (va_ref, 512, 1024, None),
                (ra_ref, 1024, 1536, None), (qb_ref, 1536, 2048, DHB ** -0.5 * LOG2E), (kb_ref, 2048, 2560, None),
                (vb_ref, 2560, 3072, None))
        shift, scale_ = sh_ref[...], sc_ref[...]
        halves = [slice(0, t // 2), slice(t // 2, t)]
        grp = [slice(0, t // (2 * CHUNK)), slice(t // (2 * CHUNK), t // CHUNK)]
        hbs = [_norm_mod(x_ref[rs, :], g_ref[...], shift[gs], scale_[gs]).astype(BF16) for rs, gs in zip(halves, grp)]
        for rs, hb in zip(halves, hbs):
            zs = [_dot(hb, w_ref[:, lo:hi]) for _, lo, hi, _ in outs]
            la = _dot(hb, wla_ref[...])
            for z, (o_ref, _, _, scale) in zip(zs, outs):
                o_ref[rs, :] = (z if scale is None else z * scale).astype(BF16)
            gl = _dot3(la, wg_ref[...]) + bg_ref[...]
            ga_ref[rs, :] = -(jnp.maximum(-gl, 0.0) + jnp.log(1.0 + jnp.exp(-jnp.abs(gl)))) * (1.0 / GATE_TAU)

    _on_token_tile(xp_ref, xs_ref, n_prompt_tiles, body)


def _inproj_even_call(xp, xs, mods, g, w_main, w_la, w_gate, b_gate):
    d = xp.shape[1]
    npt = xp.shape[0] // TM
    n = xp.shape[0] + xs.shape[0]
    row = lambda i: (i, 0)
    const = lambda i: (0, 0)
    widths = (256, 256, 512, 512, 512, 512, 512)
    out_shape = [jax.ShapeDtypeStruct((n, w), BF16) for w in widths] + [jax.ShapeDtypeStruct((n, 256), F32)]
    out_specs = [pl.BlockSpec((TM, w), row) for w in widths] + [pl.BlockSpec((TM, 256), row)]
    return pl.pallas_call(
        functools.partial(_inproj_even_kernel, n_prompt_tiles=npt),
        grid=(n // TM,),
        in_specs=_token_specs(npt, d) + [
            _mod_spec(SHIFT_MIX), _mod_spec(SCALE_MIX),
            pl.BlockSpec((1, d), const),
            pl.BlockSpec(w_main.shape, const), pl.BlockSpec(w_la.shape, const),
            pl.BlockSpec(w_gate.shape, const), pl.BlockSpec(b_gate.shape, const)],
        out_specs=out_specs, out_shape=out_shape,
        compiler_params=_cparams(("parallel",)),
        name="inproj_even",
    )(xp, xs, mods, mods, g, w_main, w_la, w_gate, b_gate)


def _rope(x, cos, sin_signed):
    t, w = x.shape
    lane = lax.broadcasted_iota(jnp.int32, (1, w), 1)
    first_half = (lane & 63) < 32
    rot = jnp.where(first_half, pltpu.roll(x, w - 32, 1), pltpu.roll(x, 32, 1))
    reps = w // 128
    return x * jnp.tile(cos, (1, reps)) + rot * jnp.tile(sin_signed, (1, reps))


def _unpack_pairs(slabs, dtype):
    lo = [pltpu.bitcast(s << 16, F32) for s in slabs]
    hi = [pltpu.bitcast(s & jnp.int32(-65536), F32) for s in slabs]
    return jnp.concatenate(lo + hi, axis=1).astype(dtype)


def _pack_pairs(x):
    bits = pltpu.bitcast(x.astype(BF16).astype(F32), jnp.int32)
    half = x.shape[1] // 2
    packed = ((bits[:, :half] >> 16) & jnp.int32(0xFFFF)) | (bits[:, half:] & jnp.int32(-65536))
    return [packed[:, 128 * s:128 * (s + 1)] for s in range(half // 128)]


def _add_moe(xn_ref, z_ref, gate_ref):
    y = _unpack_pairs([z_ref[s] for s in range(z_ref.shape[0])], F32)
    return xn_ref[...] + _group_affine(y, gate_ref[...], None)


def _rope_tables(lp, ls_, bp, bs):
    assert PAST_LEN + ls_ <= lp and lp % 128 == 0 and bs * ls_ == TM
    half = DHC // 2
    inv = ROPE_THETA ** (-jnp.arange(half, dtype=F32) / half)
    inv = jnp.tile(inv, 128 // half)
    sign = jnp.asarray(np.tile(np.repeat([-1.0, 1.0], half), 128 // DHC), F32)
    a = jnp.asarray(np.arange(lp // 128) * 128, F32)[:, None] * inv[None, :]
    b = jnp.asarray(np.arange(128), F32)[:, None] * inv[None, :]
    ca, sa, cb, sb = jnp.cos(a)[:, None], jnp.sin(a)[:, None], jnp.cos(b)[None], jnp.sin(b)[None]
    cos = (ca * cb - sa * sb).reshape(lp, 128)
    sin = ((sa * cb + ca * sb) * sign).reshape(lp, 128)
    with_sample = lambda t: jnp.concatenate([t, jnp.tile(t[PAST_LEN:PAST_LEN + ls_], (bs, 1))], axis=0)
    tiles = lp // TM
    return with_sample(cos), with_sample(sin), lambda i: (jnp.where(i < bp * tiles, i % tiles, tiles), 0)


def _inproj_odd_kernel(xn_hbm, z_ref, gate_ref, sh_ref, sc_ref, g_ref, cos_ref, sin_ref, w_ref,
                       x_ref, q_ref, k_ref, v_ref, xbuf, xsem):
    xn_ref = _ring_block([xn_hbm], xbuf, xsem)
    t = xn_ref.shape[0]
    halves = [slice(0, t // 2), slice(t // 2, t)]
    grp = [slice(0, t // (2 * CHUNK)), slice(t // (2 * CHUNK), t // CHUNK)]
    gate, shift, scale = gate_ref[...], sh_ref[...], sc_ref[...]
    xs = []
    for rs, gs in zip(halves, grp):
        y = _unpack_pairs([z_ref[s, rs, :] for s in range(z_ref.shape[0])], F32)
        xs.append(xn_ref[rs, :] + _group_affine(y, gate[gs], None))
    for rs, x in zip(halves, xs):
        x_ref[rs, :] = x
    hbs = [_norm_mod(x, g_ref[...], shift[gs], scale[gs]).astype(BF16) for x, gs in zip(xs, grp)]
    qs = [_dot(hb, w_ref[:, 0:1024]) for hb in hbs]
    ks = [_dot(hb, w_ref[:, 1024:1280]) for hb in hbs]
    vs = [_dot(hb, w_ref[:, 1280:1536]) for hb in hbs]
    for rs, q, k, v in zip(halves, qs, ks, vs):
        cos, sin = cos_ref[rs, :], sin_ref[rs, :]
        q_ref[rs, :] = (_rope(q, cos, sin) * (DHC ** -0.5 * LOG2E)).astype(BF16)
        k_ref[rs, :] = _rope(k, cos, sin).astype(BF16)
        v_ref[rs, :] = v.astype(BF16)


def _inproj_odd_call(xn, z, mods_prev, mods, g, cos, sin, rope_map, w):
    n, d = xn.shape
    row = lambda i: (i, 0)
    const = lambda i: (0, 0)
    widths = (1024, 256, 256)
    return pl.pallas_call(
        _inproj_odd_kernel,
        grid=(n // TM,),
        in_specs=[pl.BlockSpec(memory_space=pl.ANY), pl.BlockSpec((z.shape[0], TM, 128), lambda i: (0, i, 0)),
                  _mod_spec(GATE_FFN), _mod_spec(SHIFT_MIX), _mod_spec(SCALE_MIX),
                  pl.BlockSpec((1, d), const),
                  pl.BlockSpec((TM, 128), rope_map), pl.BlockSpec((TM, 128), rope_map),
                  pl.BlockSpec(w.shape, const)],
        out_specs=[pl.BlockSpec((TM, d), row)] + [pl.BlockSpec((TM, wd), row) for wd in widths],
        out_shape=[jax.ShapeDtypeStruct((n, d), F32)] + [jax.ShapeDtypeStruct((n, wd), BF16) for wd in widths],
        scratch_shapes=_ring_scratch(d),
        compiler_params=_cparams(("arbitrary",)),
        name="inproj_odd",
    )(xn, z, mods_prev, mods, mods, g, cos, sin, w)


def _gla_tri():
    t = np.arange(CHUNK)[:, None]
    s = np.arange(CHUNK)[None, :]
    cum = s <= t
    start = s < (t // SUB) * SUB
    end = s < (t // SUB + 1) * SUB
    return jnp.asarray(np.concatenate([cum, start, end], axis=0).astype(np.float32), dtype=BF16)


def _gla_kernel(q_ref, k_ref, v_ref, g_ref, r_ref, s0_ref, gn_ref, tri_ref, o_ref, sout_ref, s_ref, *, nb):
    c_ = CHUNK
    nsub = c_ // SUB

    @pl.when(pl.program_id(1) == 0)
    def _():
        s_ref[...] = s0_ref[0]

    tri = tri_ref[...]
    lane = lax.broadcasted_iota(jnp.int32, (1, 128), 1)
    hmask = [jnp.where(lane < DKA, 1.0, 0.0), jnp.where(lane >= DKA, 1.0, 0.0)]
    ti = lax.broadcasted_iota(jnp.int32, (c_, c_), 0)
    si = lax.broadcasted_iota(jnp.int32, (c_, c_), 1)
    rb, cb = ti >> 4, si >> 4
    m_diag = (rb == cb) & (si <= ti)
    m_off = [(cb == j) & (rb > j) for j in range(nsub - 1)]
    hk = HA * DKA
    gn = gn_ref[...]

    chunks = range(nb)
    heads = [(p, hh) for p in range(HA // 2) for hh in range(2)]
    rows = [slice(c * c_, (c + 1) * c_) for c in chunks]
    pair = [slice(128 * p, 128 * (p + 1)) for p in range(HA // 2)]
    css = []
    for c in chunks:
        g_hi, g_lo = _split(g_ref[rows[c], :])
        css.append(_dot(tri, g_hi) + _dot(tri, g_lo))
    lhs1, lhs2, kds, kes, q_inter, klts, dcols = [], [], [], [], [], [], []
    for c in chunks:
        b, rs, re = css[c][0:c_], css[c][c_:2 * c_], css[c][2 * c_:3 * c_]
        q = q_ref[rows[c], :].astype(F32)
        k = k_ref[rows[c], :].astype(F32)
        bl = b[c_ - 1:c_, :]
        qd = q * jnp.exp(b - rs)
        kd = k * jnp.exp(rs - b)
        ke = k * jnp.exp(re - b)
        qi = q * jnp.exp(b)
        kl = k * jnp.exp(bl - b)
        ql = [q * jnp.exp(jnp.minimum(b - b[SUB * (j + 1) - 1:SUB * (j + 1), :], 0.0)) for j in range(nsub - 1)]
        dcols.append(jnp.broadcast_to(jnp.exp(bl), (8, hk)).T[:, 0:1])
        kds.append([(kd[:, pair[p]] * hmask[hh]).astype(BF16) for p, hh in heads])
        kes.append([(ke[:, pair[p]] * hmask[hh]).astype(BF16) for p, hh in heads])
        klts.append([kl[:, ls].T.astype(BF16) for ls in pair])
        lhs1.append([qd[:, ls].astype(BF16) for ls in pair])
        lhs2.append([jnp.concatenate([ql[j][:, ls] for j in range(nsub - 1)], axis=0).astype(BF16) for ls in pair])
        q_inter.append([(qi[:, pair[p]] * hmask[hh]).astype(BF16) for p, hh in heads])
    a1s = [[_dot_nt(lhs1[c][p], kds[c][h]) for h, (p, hh) in enumerate(heads)] for c in chunks]
    a2s = [[_dot_nt(lhs2[c][p], kes[c][h]) for h, (p, hh) in enumerate(heads)] for c in chunks]
    atts = []
    for c in chunks:
        per_head = []
        for h in range(HA):
            att = jnp.zeros((c_, c_), F32)
            for j in reversed(range(nsub - 1)):
                att = jnp.where(m_off[j], a2s[c][h][j * c_:(j + 1) * c_], att)
            per_head.append(jnp.where(m_diag, a1s[c][h], att).astype(BF16))
        atts.append(per_head)
    vs_ = [[v_ref[rows[c], DVA * h:DVA * (h + 1)] for h in range(HA)] for c in chunks]
    o_intra = [[_dot(atts[c][h], vs_[c][h]) for h in range(HA)] for c in chunks]
    upds = [jnp.concatenate([_dot(klts[c][p][DKA * hh:DKA * (hh + 1)], vs_[c][2 * p + hh]) for p, hh in heads], axis=0)
            for c in chunks]

    s_cur = s_ref[...]
    s_in = []
    for c in chunks:
        s_in.append(s_cur.astype(BF16))
        s_cur = dcols[c] * s_cur + upds[c]
    s_ref[...] = s_cur
    sout_ref[0] = s_cur

    for c in chunks:
        for h in range(HA):
            o = o_intra[c][h] + _dot(q_inter[c][h], s_in[c][pair[h // 2], :])
            ms = jnp.mean(o * o, axis=-1, keepdims=True)
            vs = slice(DVA * h, DVA * (h + 1))
            rr = r_ref[rows[c], vs].astype(F32)
            o_ref[rows[c], vs] = (o * lax.rsqrt(ms + EPS) * gn * (rr * _sigmoid(rr))).astype(BF16)


def _gla_call(q, k, v, g, r, s0, gn, o_prev, *, n_seq, seq_rows, row0, nb):
    tq = nb * CHUNK
    steps = seq_rows // tq
    blk0 = row0 // tq
    row = lambda b, j: (blk0 + b * steps + j, 0)
    const = lambda b, j: (0, 0)
    tri = _gla_tri()
    in_specs = [pl.BlockSpec((tq, 256), row), pl.BlockSpec((tq, 256), row), pl.BlockSpec((tq, 512), row),
                pl.BlockSpec((tq, 256), row), pl.BlockSpec((tq, 512), row),
                pl.BlockSpec((1, 256, 128), lambda b, j: (b, 0, 0)),
                pl.BlockSpec((1, 128), const), pl.BlockSpec(tri.shape, const)]
    args = [q, k, v, g, r, s0, gn, tri]
    aliases = {}
    if o_prev is not None:
        in_specs.append(pl.BlockSpec(memory_space=pl.ANY))
        args.append(o_prev)
        aliases = {len(args) - 1: 0}
    kern = functools.partial(_gla_kernel, nb=nb)
    if o_prev is not None:
        kern = _drop_arg(kern, 8)
    return pl.pallas_call(
        kern,
        grid=(n_seq, steps),
        in_specs=in_specs,
        out_specs=[pl.BlockSpec((tq, 512), row), pl.BlockSpec((1, 256, 128), lambda b, j: (b, 0, 0))],
        out_shape=[jax.ShapeDtypeStruct((q.shape[0], 512), BF16), jax.ShapeDtypeStruct((n_seq, 256, 128), F32)],
        scratch_shapes=[pltpu.VMEM((256, 128), F32)],
        input_output_aliases=aliases,
        compiler_params=_cparams(("arbitrary", "arbitrary")),
        name="gla",
    )(*args)


def _drop_arg(fn, idx):
    def wrapped(*refs):
        return fn(*refs[:idx], *refs[idx + 1:])
    return wrapped


def _window(prev_ref, cur_ref, lo, hi, pb, ls):
    if lo < pb:
        return jnp.concatenate([prev_ref[lo:pb, ls], cur_ref[0:hi - pb, ls]], axis=0)
    return cur_ref[lo - pb:hi - pb, ls]


def _band_kernel(q_ref, kp_ref, kc_ref, vp_ref, vc_ref, bias_ref, o_ref, *, g, n_sub, pb):
    qs = CHUNK * g
    kw_rows = pb + qs
    lane = lax.broadcasted_iota(jnp.int32, (1, 128), 1)
    low = lane < DHB
    hmask = [jnp.where(low, 1.0, 0.0), jnp.where(low, 0.0, 1.0)]
    for s in range(n_sub):
        sb = s if bias_ref.shape[0] > 1 else 0
        rows = slice(qs * s, qs * (s + 1))
        lanes = [slice(128 * p, 128 * (p + 1)) for p in range(HB // 2)]
        heads = [(p, hh) for p in range(HB // 2) for hh in range(2)]
        qps = [q_ref[rows, ls].astype(F32) for ls in lanes]
        kws = [_window(kp_ref, kc_ref, qs * s, qs * s + kw_rows, pb, ls) for ls in lanes]
        vws = [_window(vp_ref, vc_ref, qs * s, qs * s + kw_rows, pb, ls) for ls in lanes]
        qq = [jnp.concatenate([(qps[p] * hmask[hh]).astype(BF16) for hh in range(2)], axis=0) for p in range(HB // 2)]
        sc2 = [_dot_nt(qq[p], kws[p]) for p in range(HB // 2)]
        scs = [sc2[p][qs * hh:qs * (hh + 1)] + bias_ref[sb, 2 * p + hh] for p, hh in heads]
        pes = [jnp.exp2(sc - jnp.max(sc, axis=-1, keepdims=True)) for sc in scs]
        pp = [jnp.concatenate([pes[2 * p + hh].astype(BF16) for hh in range(2)], axis=0) for p in range(HB // 2)]
        o2 = [_dot(pp[p], vws[p]) for p in range(HB // 2)]
        outs = [o2[p][qs * hh:qs * (hh + 1)] / jnp.sum(pes[2 * p + hh], axis=-1, keepdims=True) for p, hh in heads]
        for p, ls in enumerate(lanes):
            o_ref[rows, ls] = jnp.where(low, outs[2 * p], outs[2 * p + 1]).astype(BF16)


def _band_valid(g, pb, n_sub=None):
    rows, kw = CHUNK * g, pb + CHUNK * g
    r = np.arange(rows)[:, None]
    c = np.arange(kw)[None, :]
    dd = c // CHUNK - r // CHUNK
    band = (dd >= 0) & (dd <= pb // CHUNK)
    if n_sub is None:
        return band[None]
    return np.stack([band & (c >= pb - rows * s) for s in range(n_sub)])


def _band_bias(table, g, pb, valid):
    rows, kw = CHUNK * g, pb + CHUNK * g
    period = kw + rows
    m = np.arange(period)
    m = np.where(m < kw, m, m - period)
    ext = table[:, np.clip(m - pb, -MAX_REL, MAX_REL) + MAX_REL] * LOG2E
    flat = jnp.tile(ext, (1, rows))[:, :rows * (period - 1)]
    bias = flat.reshape(table.shape[0], rows, period - 1)[:, :, :kw]
    return jnp.where(valid[:, None], bias[None], -jnp.inf)


def _attn_call(kernel, q, kp, kc, vp, vc, extra, extra_specs, o_prev, *, width, kv_width, tq, pb,
               n_blocks, blk_map, prev_map, name):
    row = lambda i: (blk_map(i), 0)
    prev = lambda i: (prev_map(i), 0)
    in_specs = [pl.BlockSpec((tq, width), row),
                pl.BlockSpec((pb, kv_width), prev), pl.BlockSpec((tq, kv_width), row),
                pl.BlockSpec((pb, kv_width), prev), pl.BlockSpec((tq, kv_width), row)] + extra_specs
    args = [q, kp, kc, vp, vc] + extra
    aliases = {}
    if o_prev is not None:
        in_specs.append(pl.BlockSpec(memory_space=pl.ANY))
        args.append(o_prev)
        aliases = {len(args) - 1: 0}
        kernel = _drop_arg(kernel, len(args) - 1)
    return pl.pallas_call(
        kernel,
        grid=(n_blocks,),
        in_specs=in_specs,
        out_specs=pl.BlockSpec((tq, width), row),
        out_shape=jax.ShapeDtypeStruct((q.shape[0], width), BF16),
        input_output_aliases=aliases,
        compiler_params=_cparams(("parallel",)),
        name=name,
    )(*args)


def _attention(kernel_fn, q, k, v, cache_k, cache_v, masks, extra, extra_specs, *, width, kv_width, pb, tq, g,
               bp, lp, bs, name):
    bps = lp // tq
    n_sub = tq // (CHUNK * g)
    spec = lambda a: [pl.BlockSpec(a.shape, lambda i: (0,) * a.ndim)]
    kern = functools.partial(kernel_fn, g=g, n_sub=n_sub, pb=pb)
    common = dict(width=width, kv_width=kv_width, pb=pb)
    main = lambda i: (i // (bps - 1)) * bps + i % (bps - 1) + 1
    o = _attn_call(kern, q, k, k, v, v, [masks[0]] + extra, spec(masks[0]) + extra_specs, None, tq=tq,
                   n_blocks=bp * (bps - 1), blk_map=main, prev_map=lambda i: main(i) * (tq // pb) - 1,
                   name=name + "_main", **common)
    first = lambda i: i * bps
    o = _attn_call(kern, q, k, k, v, v, [masks[1]] + extra, spec(masks[1]) + extra_specs, o, tq=tq,
                   n_blocks=bp, blk_map=first, prev_map=lambda i: jnp.maximum(first(i) * (tq // pb) - 1, 0),
                   name=name + "_first", **common)
    samp = functools.partial(kernel_fn, g=1, n_sub=1, pb=pb)
    return _attn_call(samp, q, cache_k, k, cache_v, v, [masks[2]] + extra, spec(masks[2]) + extra_specs, o, tq=CHUNK,
                      n_blocks=bs, blk_map=lambda i: bp * lp // CHUNK + i, prev_map=lambda i: i,
                      name=name + "_sample", **common)


def _swa_kernel(q_ref, kp_ref, kc_ref, vp_ref, vc_ref, mask_ref, sink_ref, o_ref, *, g, n_sub, pb):
    qs = CHUNK * g
    kw_rows = pb + qs
    lane = lax.broadcasted_iota(jnp.int32, (1, 128), 1)
    low = lane < DHC
    hmask = [jnp.where(low, 1.0, 0.0), jnp.where(low, 0.0, 1.0)]
    pairs_per_kv = HC // KVC // 2
    for s in range(n_sub):
        msk = mask_ref[s if mask_ref.shape[0] > 1 else 0]
        rows = slice(qs * s, qs * (s + 1))
        kws = [_window(kp_ref, kc_ref, qs * s, qs * s + kw_rows, pb, slice(128 * kv, 128 * (kv + 1))) for kv in range(KVC)]
        vws = [_window(vp_ref, vc_ref, qs * s, qs * s + kw_rows, pb, slice(128 * kv, 128 * (kv + 1))) for kv in range(KVC)]
        heads = [(j, hh) for j in range(HC // 2) for hh in range(2)]
        qps = [q_ref[rows, 128 * j:128 * (j + 1)].astype(F32) for j in range(HC // 2)]
        per_kv = 2 * pairs_per_kv
        qq = [jnp.concatenate([(qps[j] * hmask[hh]).astype(BF16) for j, hh in heads[per_kv * kv:per_kv * (kv + 1)]],
                              axis=0) for kv in range(KVC)]
        sc2 = [_dot_nt(qq[kv], kws[kv]) for kv in range(KVC)]
        scs = [sc2[u // per_kv][qs * (u % per_kv):qs * (u % per_kv + 1)] + msk for u in range(len(heads))]
        sks = [sink_ref[0, 2 * j + hh] for j, hh in heads]
        ms = [jnp.maximum(jnp.max(sc, axis=-1, keepdims=True), sk) for sc, sk in zip(scs, sks)]
        pes = [jnp.exp2(sc - m) for sc, m in zip(scs, ms)]
        pp = [jnp.concatenate([pe.astype(BF16) for pe in pes[per_kv * kv:per_kv * (kv + 1)]], axis=0) for kv in range(KVC)]
        o2 = [_dot(pp[kv], vws[kv]) for kv in range(KVC)]
        outs = [o2[u // per_kv][qs * (u % per_kv):qs * (u % per_kv + 1)]
                / (jnp.sum(pes[u], axis=-1, keepdims=True) + jnp.exp2(sks[u] - ms[u])) for u in range(len(heads))]
        for j in range(HC // 2):
            o_ref[rows, 128 * j:128 * (j + 1)] = jnp.where(low, outs[2 * j], outs[2 * j + 1]).astype(BF16)


def _route(logits_t):
    a = [logits_t[4 * j:4 * j + 4] for j in range(EXP_PER_GROUP)]

    def first_argmax(vals, m):
        idx = jnp.full(m.shape, float(len(vals) - 1), F32)
        for j in reversed(range(len(vals) - 1)):
            idx = jnp.where(vals[j] == m, float(j), idx)
        return idx

    m1 = functools.reduce(jnp.maximum, a)
    i1 = first_argmax(a, m1)
    bsec = [jnp.where(i1 == float(j), -jnp.inf, a[j]) for j in range(EXP_PER_GROUP)]
    m2 = functools.reduce(jnp.maximum, bsec)
    i2 = first_argmax(bsec, m2)
    rows = lambda x: [x[gi:gi + 1] for gi in range(N_GROUPS)]
    gm = functools.reduce(jnp.maximum, rows(m1))
    gscore = jnp.exp(m1 - gm) + jnp.exp(m2 - gm)
    gs = rows(gscore)
    gsel = first_argmax(gs, functools.reduce(jnp.maximum, gs))

    def pick(x):
        xr = rows(x)
        out = xr[N_GROUPS - 1]
        for gi in reversed(range(N_GROUPS - 1)):
            out = jnp.where(gsel == float(gi), xr[gi], out)
        return out

    p1 = jnp.exp(pick(m1) - gm)
    p2 = jnp.exp(pick(m2) - gm)
    w1 = p1 / (p1 + p2)
    w2 = p2 / (p1 + p2)
    s1, s2 = pick(i1), pick(i2)
    lo, hi = jnp.minimum(s1, s2), jnp.maximum(s1, s2)
    pair = jnp.where(lo == 0.0, hi - 1.0, jnp.where(lo == 1.0, hi + 1.0, 5.0))
    bucket = gsel * float(N_PAIRS) + pair
    first_is_lo = s1 < s2
    return bucket, jnp.where(first_is_lo, w1, w2), jnp.where(first_is_lo, w2, w1)


def _outproj_kernel(*refs, n_x, n_o, n_prompt_tiles):
    x_refs = refs[:n_x]
    o_refs = refs[n_x:n_x + n_o]
    w_refs = refs[n_x + n_o:n_x + 2 * n_o]
    (gate_ref, nf_ref, sh_ref, sc_ref, wr_ref, br_ref, tri_ref,
     xn_ref, disp_ref, meta_ref, cnt_ref, run_ref, xbuf, xsem) = refs[n_x + 2 * n_o:]
    t = xn_ref.shape[0]

    @pl.when(pl.program_id(0) == 0)
    def _():
        run_ref[...] = jnp.zeros_like(run_ref)

    x_src = _ring_block(list(x_refs), xbuf, xsem, n_prompt_tiles)

    halves = [slice(0, t // 2), slice(t // 2, t)]
    grp = [slice(0, t // (2 * CHUNK)), slice(t // (2 * CHUNK), t // CHUNK)]
    ys = []
    for rs in halves:
        y = _dot(o_refs[0][rs, :], w_refs[0][...])
        for i in range(1, n_o):
            y = y + _dot(o_refs[i][rs, :], w_refs[i][...])
        ys.append(y)
    gate, shift, scale = gate_ref[...], sh_ref[...], sc_ref[...]
    gys = [_group_affine(y, gate[gs], None) for y, gs in zip(ys, grp)]

    for rs, gy in zip(halves, gys):
        xn_ref[rs, :] = x_src[rs, :] + gy
    hs = [_norm_mod(xn_ref[rs, :], nf_ref[...], shift[gs], scale[gs]) for rs, gs in zip(halves, grp)]
    for rs, h in zip(halves, hs):
        for s, slab in enumerate(_pack_pairs(h)):
            disp_ref[s, rs, :] = slab
    logits_t = [(_dot3_narrow(h, wr_ref[...]) + br_ref[...]).T[0:N_EXPERTS] for h in hs]
    bucket, w_lo, w_hi = _route(jnp.concatenate(logits_t, axis=1))
    r128 = lax.broadcasted_iota(jnp.int32, (128, t), 0)
    tok = (pl.program_id(0) * t + lax.broadcasted_iota(jnp.int32, (1, t), 1)).astype(F32)
    aux = jnp.where(r128 == 0, w_lo, jnp.where(r128 == 1, w_hi, jnp.where(r128 == 2, tok, 0.0))).T
    disp_ref[disp_ref.shape[0] - 1] = pltpu.bitcast(aux, jnp.int32)
    brow = lax.broadcasted_iota(jnp.int32, (BUCKET_ROWS, t), 0).astype(F32)
    onehot = jnp.where(brow == bucket, 1.0, 0.0)
    before = _dot(onehot.astype(BF16), tri_ref[...]) + run_ref[:, 0:1]
    rank = jnp.sum(onehot * before, axis=0, keepdims=True)
    run_ref[...] = run_ref[...] + jnp.sum(onehot, axis=1, keepdims=True)
    cnt_ref[...] = run_ref[...]
    r8 = lax.broadcasted_iota(jnp.int32, (8, t), 0)
    meta_ref[...] = jnp.where(r8 == 0, bucket, jnp.where(r8 == 1, rank, 0.0)).astype(jnp.int32)


def _outproj_call(xs_, os_, ws, mods, nf, wr, br, n_pad):
    d = xs_[0].shape[1]
    n = sum(a.shape[0] for a in xs_)
    npt = xs_[0].shape[0] // TM
    row = lambda i: (i, 0)
    const = lambda i: (0, 0)
    n_o = len(os_)
    in_specs = ([pl.BlockSpec(memory_space=pl.ANY) for _ in xs_]
                + [pl.BlockSpec((TM, o.shape[1]), row) for o in os_]
                + [pl.BlockSpec(w.shape, const) for w in ws]
                + [_mod_spec(GATE_MIX), pl.BlockSpec((1, d), const),
                   _mod_spec(SHIFT_FFN), _mod_spec(SCALE_FFN),
                   pl.BlockSpec(wr.shape, const), pl.BlockSpec(br.shape, const),
                   pl.BlockSpec((TM, TM), const)])
    tri = jnp.asarray(np.triu(np.ones((TM, TM), np.float32), k=1), dtype=BF16)
    return pl.pallas_call(
        functools.partial(_outproj_kernel, n_x=len(xs_), n_o=n_o, n_prompt_tiles=npt),
        grid=(n // TM,),
        in_specs=in_specs,
        out_specs=[pl.BlockSpec((TM, d), row), pl.BlockSpec((DISP_SLABS, TM, 128), lambda i: (0, i, 0)),
                   pl.BlockSpec((8, TM), lambda i: (0, i)), pl.BlockSpec((BUCKET_ROWS, 128), const)],
        out_shape=[jax.ShapeDtypeStruct((n, d), F32), jax.ShapeDtypeStruct((DISP_SLABS, n_pad, 128), jnp.int32),
                   jax.ShapeDtypeStruct((8, n), jnp.int32), jax.ShapeDtypeStruct((BUCKET_ROWS, 128), F32)],
        scratch_shapes=[pltpu.VMEM((BUCKET_ROWS, 128), F32)] + _ring_scratch(d),
        compiler_params=_cparams(("arbitrary",)),
        name="outproj_router",
    )(*xs_, *os_, *ws, mods, nf, mods, mods, wr, br, tri)


def _sc_mesh():
    return plsc.VectorSubcoreMesh(core_axis_name="core", subcore_axis_name="subcore")


def _sc_scatter_rows(src, idx, n_out):
    r = idx.shape[0]
    k = SC_GROUP
    w_per = r // (SC_WINDOW * SC_WORKERS)
    assert idx.shape == (src.shape[0],) and r % (SC_WINDOW * SC_WORKERS) == 0 and w_per % k == 0
    n_groups = w_per // k

    @functools.partial(
        pl.kernel, out_type=jax.ShapeDtypeStruct((n_out, 128), src.dtype), mesh=_sc_mesh(),
        scratch_types=[pltpu.VMEM((w_per, SC_WINDOW), jnp.int32),
                       pltpu.VMEM((2 * k, SC_WINDOW, 128), src.dtype),
                       pltpu.SemaphoreType.DMA((2,)), pltpu.SemaphoreType.DMA((2,))])
    def copy(x_hbm, i_hbm, o_hbm, ibuf, xbuf, in_sem, out_sem):
        wid = lax.axis_index("core") * (SC_WORKERS // 2) + lax.axis_index("subcore")
        pltpu.sync_copy(i_hbm.at[wid], ibuf)
        first = wid * w_per

        def start_in(g, slot):
            return [pltpu.async_copy(x_hbm.at[pl.ds((first + g * k + c) * SC_WINDOW, SC_WINDOW)],
                                     xbuf.at[slot * k + c], in_sem.at[slot]) for c in range(k)]

        def start_out(g, slot):
            return [pltpu.async_copy(xbuf.at[slot * k + c], o_hbm.at[ibuf.at[g * k + c]], out_sem.at[slot])
                    for c in range(k)]

        pending_in = start_in(0, 0)
        for g in range(n_groups):
            slot = g % 2
            for cp in pending_in:
                cp.wait()
            pending_out = start_out(g, slot)
            if g + 1 < n_groups:
                pending_in = start_in(g + 1, 1 - slot)
            for cp in pending_out:
                cp.wait()

    return copy(src, idx.reshape(SC_WORKERS, w_per, SC_WINDOW))


def _moe_kernel(elo_ref, ehi_ref, nvalid_ref, xs_ref, *refs, n_tok, dump_tiles):
    w_refs, (y_ref, tok_ref) = refs[:2 * MOE_TILES], refs[2 * MOE_TILES:]
    d_in = w_refs[0].shape[2] - D_FF
    step = pl.program_id(0)
    t = TMO
    tiles = range(MOE_TILES)
    rows = [slice(t * j, t * (j + 1)) for j in tiles]
    auxs = [pltpu.bitcast(xs_ref[Y_SLABS, rows[j], :], F32) for j in tiles]
    r = lax.broadcasted_iota(jnp.int32, (1, t), 1)
    for j in tiles:
        i = step * MOE_TILES + j
        spare = n_tok + (i % dump_tiles) * t + r
        tok = jnp.where(r < nvalid_ref[i], auxs[j].T[2:3, :].astype(jnp.int32), spare)
        for c in range(t // 128):
            tok_ref[j, c:c + 1, :] = tok[:, 128 * c:128 * (c + 1)]

    any_tokens = nvalid_ref[step * MOE_TILES] > 0
    for j in range(1, MOE_TILES):
        any_tokens = jnp.logical_or(any_tokens, nvalid_ref[step * MOE_TILES + j] > 0)

    @pl.when(any_tokens)
    def _():
        units = [(j, e) for j in tiles for e in range(2)]
        hs = [_unpack_pairs([xs_ref[s, rows[j], :] for s in range(Y_SLABS)], BF16) for j in tiles]
        abs_ = [_dot(hs[j], w_refs[2 * j + e][0, 0, 0:d_in, :]) for j, e in units]
        acts = [(ab[:, :D_FF] * _sigmoid(ab[:, :D_FF]) * ab[:, D_FF:]).astype(BF16) for ab in abs_]
        ys = [_dot(act, w_refs[2 * j + e][0, 0, d_in:d_in + D_FF, :]) for act, (j, e) in zip(acts, units)]
        for j in tiles:
            acc = auxs[j][:, 0:1] * ys[2 * j] + auxs[j][:, 1:2] * ys[2 * j + 1]
            for s, slab in enumerate(_pack_pairs(acc)):
                y_ref[s, rows[j], :] = slab

    @pl.when(jnp.logical_not(any_tokens))
    def _():
        y_ref[...] = jnp.zeros_like(y_ref)


def _moe_call(xs, elo, ehi, nvalid, wexp, n_tiles, n_tok, dump_tiles):
    m = MOE_TILES
    assert n_tiles % m == 0 and wexp.shape[3] == 2 * D_FF
    weight_specs = [pl.BlockSpec((1, 1) + wexp.shape[2:],
                                 lambda i, lo, hi, v, j=j, sel=sel: (0, (lo, hi)[sel][m * i + j], 0, 0))
                    for j in range(m) for sel in range(2)]
    weights = [wexp] * (2 * m)
    return pl.pallas_call(
        functools.partial(_moe_kernel, n_tok=n_tok, dump_tiles=dump_tiles),
        grid_spec=pltpu.PrefetchScalarGridSpec(
            num_scalar_prefetch=3,
            grid=(n_tiles // m,),
            in_specs=[pl.BlockSpec((DISP_SLABS, m * TMO, 128), lambda i, lo, hi, v: (0, i, 0))] + weight_specs,
            out_specs=[pl.BlockSpec((Y_SLABS, m * TMO, 128), lambda i, lo, hi, v: (0, i, 0)),
                       pl.BlockSpec((m, TMO // 128, 128), lambda i, lo, hi, v: (i, 0, 0))]),
        out_shape=[jax.ShapeDtypeStruct((Y_SLABS, n_tiles * TMO, 128), jnp.int32),
                   jax.ShapeDtypeStruct((n_tiles, TMO // 128, 128), jnp.int32)],
        compiler_params=_cparams(("arbitrary",), vmem_mb=VMEM_LIMIT_MOE_MB),
        name="moe_grouped",
    )(elo, ehi, nvalid, xs, *weights)


def _after(x, token):
    return lax.optimization_barrier((x, token))[0]


def _cast_kernel(after_ref, wgu_ref, wdn_ref, o_ref):
    k = wgu_ref.shape[2]
    o_ref[0, 0, 0:k, :] = wgu_ref[0, 0].astype(o_ref.dtype)
    o_ref[0, 0, k:, :] = wdn_ref[0, 0].astype(o_ref.dtype)


def _cast_call(w_gate_up, w_down, layer, after):
    _, e, k, n = w_gate_up.shape
    kd = w_down.shape[2]
    assert w_down.shape[3] == n
    return pl.pallas_call(
        _cast_kernel,
        grid=(e,),
        in_specs=[pl.BlockSpec(memory_space=pl.ANY),
                  pl.BlockSpec((1, 1, k, n), lambda i: (layer, i, 0, 0)),
                  pl.BlockSpec((1, 1, kd, n), lambda i: (layer, i, 0, 0))],
        out_specs=pl.BlockSpec((1, 1, k + kd, n), lambda i: (0, i, 0, 0)),
        out_shape=jax.ShapeDtypeStruct((1, e, k + kd, n), BF16),
        compiler_params=_cparams(("parallel",)),
        name="cast_weights",
    )(after, w_gate_up, w_down)


def _moe_layer(disp, meta, counts, w_gate_up, w_down, layer, n, n_pad, sort_rows):
    n_tiles = sort_rows // TMO
    wexp = _cast_call(w_gate_up, w_down, layer, counts)
    cnt = counts[:N_BUCKETS, 0].astype(jnp.int32)
    padded = ((cnt + TMO - 1) // TMO) * TMO
    ends = jnp.cumsum(padded)
    offs = ends - padded
    bucket, rank = meta[0], meta[1]
    pos = rank + jnp.sum(jnp.where(bucket[None, :] == jnp.arange(N_BUCKETS, dtype=jnp.int32)[:, None],
                                   offs[:, None], 0), axis=0)
    tile_start = jnp.arange(n_tiles, dtype=jnp.int32) * TMO
    tile_bucket = jnp.minimum(jnp.sum((tile_start[:, None] >= ends[None, :]).astype(jnp.int32), axis=1), N_BUCKETS - 1)
    pair_lo = np.array([0, 0, 0, 1, 1, 2], np.int32)
    pair_hi = np.array([1, 2, 3, 2, 3, 3], np.int32)
    b_lo = jnp.asarray(np.repeat(np.arange(N_GROUPS), N_PAIRS) * EXP_PER_GROUP + np.tile(pair_lo, N_GROUPS), jnp.int32)
    b_hi = jnp.asarray(np.repeat(np.arange(N_GROUPS), N_PAIRS) * EXP_PER_GROUP + np.tile(pair_hi, N_GROUPS), jnp.int32)
    onehot_tb = (tile_bucket[:, None] == jnp.arange(N_BUCKETS, dtype=jnp.int32)[None, :]).astype(jnp.int32)
    elo = jnp.sum(onehot_tb * b_lo[None, :], axis=1)
    ehi = jnp.sum(onehot_tb * b_hi[None, :], axis=1)
    bucket_end = jnp.sum(onehot_tb * (offs + cnt)[None, :], axis=1)
    nvalid = jnp.where(tile_start < ends[-1], jnp.clip(bucket_end - tile_start, 0, TMO), 0)
    dump = sort_rows + jnp.arange(n_pad - n, dtype=jnp.int32)
    pos_sc = jnp.concatenate([pos, dump])
    total = sort_rows + n_pad - n
    sc_idx = (pos_sc[None, :] + (jnp.arange(DISP_SLABS, dtype=jnp.int32) * total)[:, None]).reshape(-1)
    xs = _sc_scatter_rows(disp.reshape(DISP_SLABS * n_pad, 128), sc_idx, DISP_SLABS * total)
    ys, tok = _moe_call(xs.reshape(DISP_SLABS, total, 128), elo, ehi, nvalid, wexp, n_tiles,
                        n, (n_pad - n) // TMO)
    back_idx = (tok.reshape(1, sort_rows) + (jnp.arange(Y_SLABS, dtype=jnp.int32) * n_pad)[:, None]).reshape(-1)
    z = _sc_scatter_rows(ys.reshape(Y_SLABS * sort_rows, 128), back_idx, Y_SLABS * n_pad)
    return z.reshape(Y_SLABS, n_pad, 128), tok


def _final_kernel(xn_hbm, z_ref, gate_ref, g_ref, yp_ref, ys_ref, xbuf, xsem, *, n_prompt_tiles):
    x = _add_moe(_ring_block([xn_hbm], xbuf, xsem), z_ref, gate_ref)
    ms = jnp.mean(x * x, axis=-1, keepdims=True)
    y = x * lax.rsqrt(ms + EPS) * g_ref[...]
    i = pl.program_id(0)

    @pl.when(i < n_prompt_tiles)
    def _():
        yp_ref[...] = y

    @pl.when(i >= n_prompt_tiles)
    def _():
        ys_ref[...] = y


def _final_call(xn, z, mods, g, n_prompt):
    n, d = xn.shape
    npt = n_prompt // TM
    assert n - n_prompt == TM
    return pl.pallas_call(
        functools.partial(_final_kernel, n_prompt_tiles=npt),
        grid=(n // TM,),
        in_specs=[pl.BlockSpec(memory_space=pl.ANY), pl.BlockSpec((z.shape[0], TM, 128), lambda i: (0, i, 0)),
                  _mod_spec(GATE_FFN), pl.BlockSpec((1, d), lambda i: (0, 0))],
        out_specs=_token_specs(npt, d),
        out_shape=[jax.ShapeDtypeStruct((n_prompt, d), F32), jax.ShapeDtypeStruct((TM, d), F32)],
        scratch_shapes=_ring_scratch(d),
        compiler_params=_cparams(("arbitrary",)),
        name="final_norm",
    )(xn, z, mods, g)


def kernel(x_prompt, x_sample, c_prompt, c_sample, state_gla, cache_band_k, cache_band_v, cache_swa_k, cache_swa_v,
           w_ada, b_ada, norm_mix, norm_ffn, norm_final, w_in_even, w_gate_a, b_gate_a, gla_norm, rel_bias_b,
           w_out_even, w_in_odd, sinks_c, w_out_odd, w_router, b_router, w_gate_up, w_down):
    bp, lp, d = x_prompt.shape
    bs, ls_, _ = x_sample.shape
    n_p, n_s = bp * lp, bs * ls_
    n = n_p + n_s
    assert ls_ == CHUNK and n_s == TM and lp % TM == 0 and PAST_LEN % CHUNK == 0

    xp2, xs2 = x_prompt.reshape(n_p, d), x_sample.reshape(n_s, d)

    c16 = jnp.zeros((SEQ_ROWS, d), F32).at[:bp].set(c_prompt).at[bp:bp + bs].set(c_sample)
    mods = _ada_call(c16, w_ada, b_ada)
    seq_of_group = np.concatenate([np.repeat(np.arange(bp), lp // CHUNK), bp + np.arange(bs)])
    mods_g = [mods[l][seq_of_group] for l in range(DEPTH)]

    perm = np.array([4 * (c % 4) + c // 4 for c in range(N_EXPERTS)])
    wr = jnp.zeros((d, 128), F32).at[:, :N_EXPERTS].set(w_router[:, perm])
    br = jnp.zeros((1, 128), F32).at[0, :N_EXPERTS].set(b_router[perm])

    sc_unit = SC_WINDOW * SC_WORKERS * SC_GROUP
    n_pad = n + TMO
    while (DISP_SLABS * n_pad) % sc_unit or (Y_SLABS * n_pad) % TMO or (n_pad - n) % TMO:
        n_pad += TMO
    sort_rows = n + N_BUCKETS * TMO
    while (Y_SLABS * sort_rows) % sc_unit or sort_rows % (MOE_TILES * TMO):
        sort_rows += TMO

    gla_p = gla_s = bk_p = bv_p = bk_s = bv_s = sk_p = sv_p = sk_s = sv_s = None
    xn = z = tok = None
    for l in range(DEPTH):
        i = l // 2
        if l % 2 == 0:
            w = w_in_even[i]
            w_main = jnp.concatenate([w[:, :1536], w[:, 1552:]], axis=1).astype(BF16)
            w_la = jnp.zeros((d, 128), F32).at[:, :GATE_RANK].set(w[:, 1536:1552]).astype(BF16)
            w_gate = jnp.zeros((128, HA * DKA), F32).at[:GATE_RANK].set(w_gate_a[i])
            qa, ka, va, ra, qb, kb, vb, ga = _inproj_even_call(
                xp2, xs2, mods_g[l], norm_mix[l][None], w_main, w_la, w_gate, b_gate_a[i][None])
            xres = [xp2, xs2]
            gn = gla_norm[i][None]
            oa, s_p = _gla_call(qa, ka, va, ga, ra, jnp.zeros((bp, 256, 128), F32), gn, None,
                                n_seq=bp, seq_rows=lp, row0=0, nb=8)
            oa, s_s = _gla_call(qa, ka, va, ga, ra, state_gla[i].reshape(bs, 256, 128), gn, oa,
                                n_seq=bs, seq_rows=ls_, row0=n_p, nb=1)
            gla_p, gla_s = s_p.reshape(1, bp, HA, DKA, DVA), s_s.reshape(1, bs, HA, DKA, DVA)
            pb = N_PREV_B * CHUNK
            tq, g = 512, 2
            ck = cache_band_k[i].reshape(bs * pb, HB * DHB).astype(BF16)
            cv = cache_band_v[i].reshape(bs * pb, HB * DHB).astype(BF16)
            biases = (_band_bias(rel_bias_b[i], g, pb, _band_valid(g, pb)),
                      _band_bias(rel_bias_b[i], g, pb, _band_valid(g, pb, tq // (CHUNK * g))),
                      _band_bias(rel_bias_b[i], 1, pb, _band_valid(1, pb)))
            ob = _attention(_band_kernel, qb, kb, vb, ck, cv, biases, [], [], width=512, kv_width=512, pb=pb,
                            tq=tq, g=g, bp=bp, lp=lp, bs=bs, name="band")
            tail = lambda a: jnp.stack([a[(b + 1) * lp - pb:(b + 1) * lp] for b in range(bp)]).astype(F32).reshape(1, bp, pb, HB, DHB)
            new = lambda a: a[n_p:].astype(F32).reshape(bs, ls_, HB, DHB)
            bk_p, bv_p = tail(kb), tail(vb)
            bk_s = jnp.concatenate([cache_band_k[i][:, ls_:], new(kb)], axis=1)[None]
            bv_s = jnp.concatenate([cache_band_v[i][:, ls_:], new(vb)], axis=1)[None]
            wo = w_out_even[i].astype(BF16)
            os_, ws = [oa, ob], [wo[:HA * DVA], wo[HA * DVA:]]
        else:
            w = _after(w_in_odd[i], tok)
            w_out_l = _after(w_out_odd[i], tok)
            cache_k_l, cache_v_l = _after(cache_swa_k[i], tok), _after(cache_swa_v[i], tok)
            wk, wv = w[:, 1024:1152], w[:, 1152:1280]
            dup = lambda a: jnp.concatenate([a[:, :64], a[:, :64], a[:, 64:], a[:, 64:]], axis=1)
            w_all = jnp.concatenate([w[:, :1024], dup(wk), dup(wv)], axis=1).astype(BF16)
            cos, sin, rope_map = _rope_tables(lp, ls_, bp, bs)
            x, q, k, v = _inproj_odd_call(xn, z, mods_g[l - 1], mods_g[l], norm_mix[l][None], cos, sin, rope_map, w_all)
            xres = [x]
            pb = WINDOW
            tq, g = 512, 2
            sink = sinks_c[i][None] * LOG2E
            sink_spec = [pl.BlockSpec(memory_space=pltpu.SMEM)]
            dupc = lambda c: jnp.concatenate([c[:, :, 0], c[:, :, 0], c[:, :, 1], c[:, :, 1]], axis=-1).reshape(bs * pb, 256).astype(BF16)
            ck, cv = dupc(cache_k_l), dupc(cache_v_l)
            additive = lambda valid: jnp.asarray(np.where(valid, 0.0, -np.inf), F32)
            masks = (additive(_band_valid(g, pb)), additive(_band_valid(g, pb, tq // (CHUNK * g))),
                     additive(_band_valid(1, pb)))
            o = _attention(_swa_kernel, q, k, v, ck, cv, masks, [sink], sink_spec, width=1024, kv_width=256, pb=pb,
                           tq=tq, g=g, bp=bp, lp=lp, bs=bs, name="swa")
            undup = lambda a: jnp.concatenate([a[:, 0:64], a[:, 128:192]], axis=1).astype(F32)
            tail = lambda a: jnp.stack([undup(a[(b + 1) * lp - pb:(b + 1) * lp]) for b in range(bp)]).reshape(1, bp, pb, KVC, DHC)
            new = lambda a: undup(a[n_p:]).reshape(bs, ls_, KVC, DHC)
            sk_p, sv_p = tail(k), tail(v)
            sk_s = jnp.concatenate([cache_swa_k[i][:, ls_:], new(k)], axis=1)[None]
            sv_s = jnp.concatenate([cache_swa_v[i][:, ls_:], new(v)], axis=1)[None]
            os_, ws = [o], [w_out_l.astype(BF16)]
        xn, disp, meta, counts = _outproj_call(xres, os_, ws, mods_g[l], norm_ffn[l][None], wr, br, n_pad)
        z, tok = _moe_layer(disp, meta, counts, w_gate_up, w_down, l, n, n_pad, sort_rows)

    y_prompt, y_sample = _final_call(xn, z, mods_g[DEPTH - 1], norm_final[None], n_p)
    return (y_prompt.reshape(bp, lp, d), y_sample.reshape(bs, ls_, d),
            gla_p, gla_s, bk_p, bv_p, bk_s, bv_s, sk_p, sv_p, sk_s, sv_s)
```

```python
import functools

import numpy as np
import jax
import jax.numpy as jnp
from jax import lax
from jax.experimental import pallas as pl
from jax.experimental.pallas import tpu as pltpu
from jax.experimental.pallas import tpu_sc as plsc

F32 = jnp.float32
BF16 = jnp.bfloat16

D_MODEL = 1024
DEPTH = 2
CHUNK = 64
PAST_LEN = 4096
HA, DKA, DVA = 4, 64, 128
GATE_RANK = 16
GATE_TAU = 16.0
HB, DHB = 8, 64
N_PREV_B = 8
MAX_REL = 128
HC, KVC, DHC = 16, 2, 64
WINDOW = 128
ROPE_THETA = 10000.0
N_EXPERTS = 16
N_GROUPS = 4
EXP_PER_GROUP = 4
D_FF = 512
EPS = 1e-6

N_PAIRS = 6
N_BUCKETS = N_GROUPS * N_PAIRS
BUCKET_ROWS = 32
Y_SLABS = 4
DISP_SLABS = Y_SLABS + 1
TMO = 256
MOE_TILES = 2
SC_WINDOW = 128
SC_WORKERS = 32
SC_GROUP = 3

TM = 512
SEQ_ROWS = 16
SUB = 16
LOG2E = 1.4426950408889634
VMEM_LIMIT_MB = 48
VMEM_LIMIT_MOE_MB = 56


def _cparams(sem, vmem_mb=VMEM_LIMIT_MB):
    return pltpu.CompilerParams(dimension_semantics=sem, vmem_limit_bytes=vmem_mb * 1024 * 1024)


def _dot(a, b):
    return jnp.dot(a, b, preferred_element_type=F32)


def _dot_nt(a, b):
    return lax.dot_general(a, b, (((1,), (1,)), ((), ())), preferred_element_type=F32)


def _split(a):
    hi = a.astype(BF16)
    lo = (a - hi.astype(F32)).astype(BF16)
    return hi, lo


def _dot3(a, b):
    ah, al = _split(a)
    bh, bl = _split(b)
    return _dot(ah, bh) + _dot(ah, bl) + _dot(al, bh)


def _dot3_narrow(a, b):
    ah, al = _split(a)
    bh, bl = _split(b)
    n = b.shape[1]
    p = _dot(ah, jnp.concatenate([bh, bl], axis=1))
    return p[:, :n] + p[:, n:] + _dot(al, bh)


def _sigmoid(x):
    return 1.0 / (1.0 + jnp.exp(-x))


def _group_affine(y, mul, add):
    parts = []
    for gi in range(y.shape[0] // CHUNK):
        p = y[gi * CHUNK:(gi + 1) * CHUNK]
        if mul is not None:
            p = p * mul[gi:gi + 1]
        if add is not None:
            p = p + add[gi:gi + 1]
        parts.append(p)
    return jnp.concatenate(parts, axis=0)


def _norm_mod(x, g, shift, scale):
    ms = jnp.mean(x * x, axis=-1, keepdims=True)
    return _group_affine(x * lax.rsqrt(ms + EPS) * g, 1.0 + scale, shift)


def _mod_spec(part):
    return pl.BlockSpec((TM // CHUNK, D_MODEL), lambda i: (i, part))


SHIFT_MIX, SCALE_MIX, GATE_MIX, SHIFT_FFN, SCALE_FFN, GATE_FFN = range(6)


def _on_token_tile(xp_ref, xs_ref, n_prompt_tiles, body):
    @pl.when(pl.program_id(0) < n_prompt_tiles)
    def _():
        body(xp_ref)

    @pl.when(pl.program_id(0) >= n_prompt_tiles)
    def _():
        body(xs_ref)


RING = 3


def _ring_block(srcs, buf, sem, n_prompt_tiles=None):
    s = pl.program_id(0)
    n_steps = pl.num_programs(0)
    t = buf.shape[1]

    def copy(src, blk, slot):
        return pltpu.make_async_copy(src.at[pl.ds(pl.multiple_of(blk * t, t), t)], buf.at[slot], sem.at[slot])

    def start(step, slot):
        if len(srcs) == 1:
            copy(srcs[0], step, slot).start()
        elif isinstance(step, int):
            assert step < n_prompt_tiles
            copy(srcs[0], step, slot).start()
        else:
            @pl.when(step < n_prompt_tiles)
            def _():
                copy(srcs[0], step, slot).start()

            @pl.when(step >= n_prompt_tiles)
            def _():
                copy(srcs[1], step - n_prompt_tiles, slot).start()

    @pl.when(s == 0)
    def _():
        for k in range(RING - 1):
            start(k, k)

    ahead = s + (RING - 1)

    @pl.when(ahead < n_steps)
    def _():
        start(ahead, ahead % RING)

    slot = s % RING
    copy(srcs[0], 0, slot).wait()
    return buf.at[slot]


def _ring_scratch(d):
    return [pltpu.VMEM((RING, TM, d), F32), pltpu.SemaphoreType.DMA((RING,))]


def _token_specs(n_prompt_tiles, d):
    return [pl.BlockSpec((TM, d), lambda i: (jnp.minimum(i, n_prompt_tiles - 1), 0)),
            pl.BlockSpec((TM, d), lambda i: (0, 0))]


def _ada_kernel(c_ref, w_ref, b_ref, o_ref):
    c = c_ref[...]
    o_ref[0] = _dot3(c * _sigmoid(c), w_ref[0]) + b_ref[0]


def _ada_call(c16, w_ada, b_ada):
    d = D_MODEL
    tn = 1024
    return pl.pallas_call(
        _ada_kernel,
        grid=(DEPTH, 6 * d // tn),
        in_specs=[pl.BlockSpec((SEQ_ROWS, d), lambda l, j: (0, 0)),
                  pl.BlockSpec((1, d, tn), lambda l, j: (l, 0, j)),
                  pl.BlockSpec((1, 1, tn), lambda l, j: (l, 0, j))],
        out_specs=pl.BlockSpec((1, SEQ_ROWS, tn), lambda l, j: (l, 0, j)),
        out_shape=jax.ShapeDtypeStruct((DEPTH, SEQ_ROWS, 6 * d), F32),
        compiler_params=_cparams(("arbitrary", "arbitrary")),
        name="ada",
    )(c16, w_ada, b_ada.reshape(DEPTH, 1, 6 * d))


def _inproj_even_kernel(xp_ref, xs_ref, sh_ref, sc_ref, g_ref, w_ref, wla_ref, wg_ref, bg_ref,
                        qa_ref, ka_ref, va_ref, ra_ref, qb_ref, kb_ref, vb_ref, ga_ref, *, n_prompt_tiles):
    def body(x_ref):
        t = x_ref.shape[0]
        outs = ((qa_ref, 0, 256, DKA ** -0.5), (ka_ref, 256, 512, None), (va_ref, 512, 1024, None),
                (ra_ref, 1024, 1536, None), (qb_ref, 1536, 2048, DHB ** -0.5 * LOG2E), (kb_ref, 2048, 2560, None),
                (vb_ref, 2560, 3072, None))
        shift, scale_ = sh_ref[...], sc_ref[...]
        halves = [slice(0, t // 2), slice(t // 2, t)]
        grp = [slice(0, t // (2 * CHUNK)), slice(t // (2 * CHUNK), t // CHUNK)]
        hbs = [_norm_mod(x_ref[rs, :], g_ref[...], shift[gs], scale_[gs]).astype(BF16) for rs, gs in zip(halves, grp)]
        for rs, hb in zip(halves, hbs):
            zs = [_dot(hb, w_ref[:, lo:hi]) for _, lo, hi, _ in outs]
            la = _dot(hb, wla_ref[...])
            for z, (o_ref, _, _, scale) in zip(zs, outs):
                o_ref[rs, :] = (z if scale is None else z * scale).astype(BF16)
            gl = _dot3(la, wg_ref[...]) + bg_ref[...]
            ga_ref[rs, :] = -(jnp.maximum(-gl, 0.0) + jnp.log(1.0 + jnp.exp(-jnp.abs(gl)))) * (1.0 / GATE_TAU)

    _on_token_tile(xp_ref, xs_ref, n_prompt_tiles, body)


def _inproj_even_call(xp, xs, mods, g, w_main, w_la, w_gate, b_gate):
    d = xp.shape[1]
    npt = xp.shape[0] // TM
    n = xp.shape[0] + xs.shape[0]
    row = lambda i: (i, 0)
    const = lambda i: (0, 0)
    widths = (256, 256, 512, 512, 512, 512, 512)
    out_shape = [jax.ShapeDtypeStruct((n, w), BF16) for w in widths] + [jax.ShapeDtypeStruct((n, 256), F32)]
    out_specs = [pl.BlockSpec((TM, w), row) for w in widths] + [pl.BlockSpec((TM, 256), row)]
    return pl.pallas_call(
        functools.partial(_inproj_even_kernel, n_prompt_tiles=npt),
        grid=(n // TM,),
        in_specs=_token_specs(npt, d) + [
            _mod_spec(SHIFT_MIX), _mod_spec(SCALE_MIX),
            pl.BlockSpec((1, d), const),
            pl.BlockSpec(w_main.shape, const), pl.BlockSpec(w_la.shape, const),
            pl.BlockSpec(w_gate.shape, const), pl.BlockSpec(b_gate.shape, const)],
        out_specs=out_specs, out_shape=out_shape,
        compiler_params=_cparams(("parallel",)),
        name="inproj_even",
    )(xp, xs, mods, mods, g, w_main, w_la, w_gate, b_gate)


def _rope(x, cos, sin_signed):
    t, w = x.shape
    lane = lax.broadcasted_iota(jnp.int32, (1, w), 1)
    first_half = (lane & 63) < 32
    rot = jnp.where(first_half, pltpu.roll(x, w - 32, 1), pltpu.roll(x, 32, 1))
    reps = w // 128
    return x * jnp.tile(cos, (1, reps)) + rot * jnp.tile(sin_signed, (1, reps))


def _unpack_pairs(slabs, dtype):
    lo = [pltpu.bitcast(s << 16, F32) for s in slabs]
    hi = [pltpu.bitcast(s & jnp.int32(-65536), F32) for s in slabs]
    return jnp.concatenate(lo + hi, axis=1).astype(dtype)


def _pack_pairs(x):
    bits = pltpu.bitcast(x.astype(BF16).astype(F32), jnp.int32)
    half = x.shape[1] // 2
    packed = ((bits[:, :half] >> 16) & jnp.int32(0xFFFF)) | (bits[:, half:] & jnp.int32(-65536))
    return [packed[:, 128 * s:128 * (s + 1)] for s in range(half // 128)]


def _add_moe(xn_ref, z_ref, gate_ref):
    y = _unpack_pairs([z_ref[s] for s in range(z_ref.shape[0])], F32)
    return xn_ref[...] + _group_affine(y, gate_ref[...], None)


def _rope_tables(lp, ls_, bp, bs):
    assert PAST_LEN + ls_ <= lp and lp % 128 == 0 and bs * ls_ == TM
    half = DHC // 2
    inv = ROPE_THETA ** (-jnp.arange(half, dtype=F32) / half)
    inv = jnp.tile(inv, 128 // half)
    sign = jnp.asarray(np.tile(np.repeat([-1.0, 1.0], half), 128 // DHC), F32)
    a = jnp.asarray(np.arange(lp // 128) * 128, F32)[:, None] * inv[None, :]
    b = jnp.asarray(np.arange(128), F32)[:, None] * inv[None, :]
    ca, sa, cb, sb = jnp.cos(a)[:, None], jnp.sin(a)[:, None], jnp.cos(b)[None], jnp.sin(b)[None]
    cos = (ca * cb - sa * sb).reshape(lp, 128)
    sin = ((sa * cb + ca * sb) * sign).reshape(lp, 128)
    with_sample = lambda t: jnp.concatenate([t, jnp.tile(t[PAST_LEN:PAST_LEN + ls_], (bs, 1))], axis=0)
    tiles = lp // TM
    return with_sample(cos), with_sample(sin), lambda i: (jnp.where(i < bp * tiles, i % tiles, tiles), 0)


def _inproj_odd_kernel(xn_hbm, z_ref, gate_ref, sh_ref, sc_ref, g_ref, cos_ref, sin_ref, w_ref,
                       x_ref, q_ref, k_ref, v_ref, xbuf, xsem):
    xn_ref = _ring_block([xn_hbm], xbuf, xsem)
    t = xn_ref.shape[0]
    halves = [slice(0, t // 2), slice(t // 2, t)]
    grp = [slice(0, t // (2 * CHUNK)), slice(t // (2 * CHUNK), t // CHUNK)]
    gate, shift, scale = gate_ref[...], sh_ref[...], sc_ref[...]
    xs = []
    for rs, gs in zip(halves, grp):
        y = _unpack_pairs([z_ref[s, rs, :] for s in range(z_ref.shape[0])], F32)
        xs.append(xn_ref[rs, :] + _group_affine(y, gate[gs], None))
    for rs, x in zip(halves, xs):
        x_ref[rs, :] = x
    hbs = [_norm_mod(x, g_ref[...], shift[gs], scale[gs]).astype(BF16) for x, gs in zip(xs, grp)]
    qs = [_dot(hb, w_ref[:, 0:1024]) for hb in hbs]
    ks = [_dot(hb, w_ref[:, 1024:1280]) for hb in hbs]
    vs = [_dot(hb, w_ref[:, 1280:1536]) for hb in hbs]
    for rs, q, k, v in zip(halves, qs, ks, vs):
        cos, sin = cos_ref[rs, :], sin_ref[rs, :]
        q_ref[rs, :] = (_rope(q, cos, sin) * (DHC ** -0.5 * LOG2E)).astype(BF16)
        k_ref[rs, :] = _rope(k, cos, sin).astype(BF16)
        v_ref[rs, :] = v.astype(BF16)


def _inproj_odd_call(xn, z, mods_prev, mods, g, cos, sin, rope_map, w):
    n, d = xn.shape
    row = lambda i: (i, 0)
    const = lambda i: (0, 0)
    widths = (1024, 256, 256)
    return pl.pallas_call(
        _inproj_odd_kernel,
        grid=(n // TM,),
        in_specs=[pl.BlockSpec(memory_space=pl.ANY), pl.BlockSpec((z.shape[0], TM, 128), lambda i: (0, i, 0)),
                  _mod_spec(GATE_FFN), _mod_spec(SHIFT_MIX), _mod_spec(SCALE_MIX),
                  pl.BlockSpec((1, d), const),
                  pl.BlockSpec((TM, 128), rope_map), pl.BlockSpec((TM, 128), rope_map),
                  pl.BlockSpec(w.shape, const)],
        out_specs=[pl.BlockSpec((TM, d), row)] + [pl.BlockSpec((TM, wd), row) for wd in widths],
        out_shape=[jax.ShapeDtypeStruct((n, d), F32)] + [jax.ShapeDtypeStruct((n, wd), BF16) for wd in widths],
        scratch_shapes=_ring_scratch(d),
        compiler_params=_cparams(("arbitrary",)),
        name="inproj_odd",
    )(xn, z, mods_prev, mods, mods, g, cos, sin, w)


def _gla_tri():
    t = np.arange(CHUNK)[:, None]
    s = np.arange(CHUNK)[None, :]
    cum = s <= t
    start = s < (t // SUB) * SUB
    end = s < (t // SUB + 1) * SUB
    return jnp.asarray(np.concatenate([cum, start, end], axis=0).astype(np.float32), dtype=BF16)


def _gla_kernel(q_ref, k_ref, v_ref, g_ref, r_ref, s0_ref, gn_ref, tri_ref, o_ref, sout_ref, s_ref, *, nb):
    c_ = CHUNK
    nsub = c_ // SUB

    @pl.when(pl.program_id(1) == 0)
    def _():
        s_ref[...] = s0_ref[0]

    tri = tri_ref[...]
    lane = lax.broadcasted_iota(jnp.int32, (1, 128), 1)
    hmask = [jnp.where(lane < DKA, 1.0, 0.0), jnp.where(lane >= DKA, 1.0, 0.0)]
    ti = lax.broadcasted_iota(jnp.int32, (c_, c_), 0)
    si = lax.broadcasted_iota(jnp.int32, (c_, c_), 1)
    rb, cb = ti >> 4, si >> 4
    m_diag = (rb == cb) & (si <= ti)
    m_off = [(cb == j) & (rb > j) for j in range(nsub - 1)]
    hk = HA * DKA
    gn = gn_ref[...]

    chunks = range(nb)
    heads = [(p, hh) for p in range(HA // 2) for hh in range(2)]
    rows = [slice(c * c_, (c + 1) * c_) for c in chunks]
    pair = [slice(128 * p, 128 * (p + 1)) for p in range(HA // 2)]
    css = []
    for c in chunks:
        g_hi, g_lo = _split(g_ref[rows[c], :])
        css.append(_dot(tri, g_hi) + _dot(tri, g_lo))
    lhs1, lhs2, kds, kes, q_inter, klts, dcols = [], [], [], [], [], [], []
    for c in chunks:
        b, rs, re = css[c][0:c_], css[c][c_:2 * c_], css[c][2 * c_:3 * c_]
        q = q_ref[rows[c], :].astype(F32)
        k = k_ref[rows[c], :].astype(F32)
        bl = b[c_ - 1:c_, :]
        qd = q * jnp.exp(b - rs)
        kd = k * jnp.exp(rs - b)
        ke = k * jnp.exp(re - b)
        qi = q * jnp.exp(b)
        kl = k * jnp.exp(bl - b)
        ql = [q * jnp.exp(jnp.minimum(b - b[SUB * (j + 1) - 1:SUB * (j + 1), :], 0.0)) for j in range(nsub - 1)]
        dcols.append(jnp.broadcast_to(jnp.exp(bl), (8, hk)).T[:, 0:1])
        kds.append([(kd[:, pair[p]] * hmask[hh]).astype(BF16) for p, hh in heads])
        kes.append([(ke[:, pair[p]] * hmask[hh]).astype(BF16) for p, hh in heads])
        klts.append([kl[:, ls].T.astype(BF16) for ls in pair])
        lhs1.append([qd[:, ls].astype(BF16) for ls in pair])
        lhs2.append([jnp.concatenate([ql[j][:, ls] for j in range(nsub - 1)], axis=0).astype(BF16) for ls in pair])
        q_inter.append([(qi[:, pair[p]] * hmask[hh]).astype(BF16) for p, hh in heads])
    a1s = [[_dot_nt(lhs1[c][p], kds[c][h]) for h, (p, hh) in enumerate(heads)] for c in chunks]
    a2s = [[_dot_nt(lhs2[c][p], kes[c][h]) for h, (p, hh) in enumerate(heads)] for c in chunks]
    atts = []
    for c in chunks:
        per_head = []
        for h in range(HA):
            att = jnp.zeros((c_, c_), F32)
            for j in reversed(range(nsub - 1)):
                att = jnp.where(m_off[j], a2s[c][h][j * c_:(j + 1) * c_], att)
            per_head.append(jnp.where(m_diag, a1s[c][h], att).astype(BF16))
        atts.append(per_head)
    vs_ = [[v_ref[rows[c], DVA * h:DVA * (h + 1)] for h in range(HA)] for c in chunks]
    o_intra = [[_dot(atts[c][h], vs_[c][h]) for h in range(HA)] for c in chunks]
    upds = [jnp.concatenate([_dot(klts[c][p][DKA * hh:DKA * (hh + 1)], vs_[c][2 * p + hh]) for p, hh in heads], axis=0)
            for c in chunks]

    s_cur = s_ref[...]
    s_in = []
    for c in chunks:
        s_in.append(s_cur.astype(BF16))
        s_cur = dcols[c] * s_cur + upds[c]
    s_ref[...] = s_cur
    sout_ref[0] = s_cur

    for c in chunks:
        for h in range(HA):
            o = o_intra[c][h] + _dot(q_inter[c][h], s_in[c][pair[h // 2], :])
            ms = jnp.mean(o * o, axis=-1, keepdims=True)
            vs = slice(DVA * h, DVA * (h + 1))
            rr = r_ref[rows[c], vs].astype(F32)
            o_ref[rows[c], vs] = (o * lax.rsqrt(ms + EPS) * gn * (rr * _sigmoid(rr))).astype(BF16)


def _gla_call(q, k, v, g, r, s0, gn, o_prev, *, n_seq, seq_rows, row0, nb):
    tq = nb * CHUNK
    steps = seq_rows // tq
    blk0 = row0 // tq
    row = lambda b, j: (blk0 + b * steps + j, 0)
    const = lambda b, j: (0, 0)
    tri = _gla_tri()
    in_specs = [pl.BlockSpec((tq, 256), row), pl.BlockSpec((tq, 256), row), pl.BlockSpec((tq, 512), row),
                pl.BlockSpec((tq, 256), row), pl.BlockSpec((tq, 512), row),
                pl.BlockSpec((1, 256, 128), lambda b, j: (b, 0, 0)),
                pl.BlockSpec((1, 128), const), pl.BlockSpec(tri.shape, const)]
    args = [q, k, v, g, r, s0, gn, tri]
    aliases = {}
    if o_prev is not None:
        in_specs.append(pl.BlockSpec(memory_space=pl.ANY))
        args.append(o_prev)
        aliases = {len(args) - 1: 0}
    kern = functools.partial(_gla_kernel, nb=nb)
    if o_prev is not None:
        kern = _drop_arg(kern, 8)
    return pl.pallas_call(
        kern,
        grid=(n_seq, steps),
        in_specs=in_specs,
        out_specs=[pl.BlockSpec((tq, 512), row), pl.BlockSpec((1, 256, 128), lambda b, j: (b, 0, 0))],
        out_shape=[jax.ShapeDtypeStruct((q.shape[0], 512), BF16), jax.ShapeDtypeStruct((n_seq, 256, 128), F32)],
        scratch_shapes=[pltpu.VMEM((256, 128), F32)],
        input_output_aliases=aliases,
        compiler_params=_cparams(("arbitrary", "arbitrary")),
        name="gla",
    )(*args)


def _drop_arg(fn, idx):
    def wrapped(*refs):
        return fn(*refs[:idx], *refs[idx + 1:])
    return wrapped


def _window(prev_ref, cur_ref, lo, hi, pb, ls):
    if lo < pb:
        return jnp.concatenate([prev_ref[lo:pb, ls], cur_ref[0:hi - pb, ls]], axis=0)
    return cur_ref[lo - pb:hi - pb, ls]


def _band_kernel(q_ref, kp_ref, kc_ref, vp_ref, vc_ref, bias_ref, o_ref, *, g, n_sub, pb):
    qs = CHUNK * g
    kw_rows = pb + qs
    lane = lax.broadcasted_iota(jnp.int32, (1, 128), 1)
    low = lane < DHB
    hmask = [jnp.where(low, 1.0, 0.0), jnp.where(low, 0.0, 1.0)]
    for s in range(n_sub):
        sb = s if bias_ref.shape[0] > 1 else 0
        rows = slice(qs * s, qs * (s + 1))
        lanes = [slice(128 * p, 128 * (p + 1)) for p in range(HB // 2)]
        heads = [(p, hh) for p in range(HB // 2) for hh in range(2)]
        qps = [q_ref[rows, ls].astype(F32) for ls in lanes]
        kws = [_window(kp_ref, kc_ref, qs * s, qs * s + kw_rows, pb, ls) for ls in lanes]
        vws = [_window(vp_ref, vc_ref, qs * s, qs * s + kw_rows, pb, ls) for ls in lanes]
        qq = [jnp.concatenate([(qps[p] * hmask[hh]).astype(BF16) for hh in range(2)], axis=0) for p in range(HB // 2)]
        sc2 = [_dot_nt(qq[p], kws[p]) for p in range(HB // 2)]
        scs = [sc2[p][qs * hh:qs * (hh + 1)] + bias_ref[sb, 2 * p + hh] for p, hh in heads]
        pes = [jnp.exp2(sc - jnp.max(sc, axis=-1, keepdims=True)) for sc in scs]
        pp = [jnp.concatenate([pes[2 * p + hh].astype(BF16) for hh in range(2)], axis=0) for p in range(HB // 2)]
        o2 = [_dot(pp[p], vws[p]) for p in range(HB // 2)]
        outs = [o2[p][qs * hh:qs * (hh + 1)] / jnp.sum(pes[2 * p + hh], axis=-1, keepdims=True) for p, hh in heads]
        for p, ls in enumerate(lanes):
            o_ref[rows, ls] = jnp.where(low, outs[2 * p], outs[2 * p + 1]).astype(BF16)


def _band_valid(g, pb, n_sub=None):
    rows, kw = CHUNK * g, pb + CHUNK * g
    r = np.arange(rows)[:, None]
    c = np.arange(kw)[None, :]
    dd = c // CHUNK - r // CHUNK
    band = (dd >= 0) & (dd <= pb // CHUNK)
    if n_sub is None:
        return band[None]
    return np.stack([band & (c >= pb - rows * s) for s in range(n_sub)])


def _band_bias(table, g, pb, valid):
    rows, kw = CHUNK * g, pb + CHUNK * g
    period = kw + rows
    m = np.arange(period)
    m = np.where(m < kw, m, m - period)
    ext = table[:, np.clip(m - pb, -MAX_REL, MAX_REL) + MAX_REL] * LOG2E
    flat = jnp.tile(ext, (1, rows))[:, :rows * (period - 1)]
    bias = flat.reshape(table.shape[0], rows, period - 1)[:, :, :kw]
    return jnp.where(valid[:, None], bias[None], -jnp.inf)


def _attn_call(kernel, q, kp, kc, vp, vc, extra, extra_specs, o_prev, *, width, kv_width, tq, pb,
               n_blocks, blk_map, prev_map, name):
    row = lambda i: (blk_map(i), 0)
    prev = lambda i: (prev_map(i), 0)
    in_specs = [pl.BlockSpec((tq, width), row),
                pl.BlockSpec((pb, kv_width), prev), pl.BlockSpec((tq, kv_width), row),
                pl.BlockSpec((pb, kv_width), prev), pl.BlockSpec((tq, kv_width), row)] + extra_specs
    args = [q, kp, kc, vp, vc] + extra
    aliases = {}
    if o_prev is not None:
        in_specs.append(pl.BlockSpec(memory_space=pl.ANY))
        args.append(o_prev)
        aliases = {len(args) - 1: 0}
        kernel = _drop_arg(kernel, len(args) - 1)
    return pl.pallas_call(
        kernel,
        grid=(n_blocks,),
        in_specs=in_specs,
        out_specs=pl.BlockSpec((tq, width), row),
        out_shape=jax.ShapeDtypeStruct((q.shape[0], width), BF16),
        input_output_aliases=aliases,
        compiler_params=_cparams(("parallel",)),
        name=name,
    )(*args)


def _attention(kernel_fn, q, k, v, cache_k, cache_v, masks, extra, extra_specs, *, width, kv_width, pb, tq, g,
               bp, lp, bs, name):
    bps = lp // tq
    n_sub = tq // (CHUNK * g)
    spec = lambda a: [pl.BlockSpec(a.shape, lambda i: (0,) * a.ndim)]
    kern = functools.partial(kernel_fn, g=g, n_sub=n_sub, pb=pb)
    common = dict(width=width, kv_width=kv_width, pb=pb)
    main = lambda i: (i // (bps - 1)) * bps + i % (bps - 1) + 1
    o = _attn_call(kern, q, k, k, v, v, [masks[0]] + extra, spec(masks[0]) + extra_specs, None, tq=tq,
                   n_blocks=bp * (bps - 1), blk_map=main, prev_map=lambda i: main(i) * (tq // pb) - 1,
                   name=name + "_main", **common)
    first = lambda i: i * bps
    o = _attn_call(kern, q, k, k, v, v, [masks[1]] + extra, spec(masks[1]) + extra_specs, o, tq=tq,
                   n_blocks=bp, blk_map=first, prev_map=lambda i: jnp.maximum(first(i) * (tq // pb) - 1, 0),
                   name=name + "_first", **common)
    samp = functools.partial(kernel_fn, g=1, n_sub=1, pb=pb)
    return _attn_call(samp, q, cache_k, k, cache_v, v, [masks[2]] + extra, spec(masks[2]) + extra_specs, o, tq=CHUNK,
                      n_blocks=bs, blk_map=lambda i: bp * lp // CHUNK + i, prev_map=lambda i: i,
                      name=name + "_sample", **common)


def _swa_kernel(q_ref, kp_ref, kc_ref, vp_ref, vc_ref, mask_ref, sink_ref, o_ref, *, g, n_sub, pb):
    qs = CHUNK * g
    kw_rows = pb + qs
    lane = lax.broadcasted_iota(jnp.int32, (1, 128), 1)
    low = lane < DHC
    hmask = [jnp.where(low, 1.0, 0.0), jnp.where(low, 0.0, 1.0)]
    pairs_per_kv = HC // KVC // 2
    for s in range(n_sub):
        msk = mask_ref[s if mask_ref.shape[0] > 1 else 0]
        rows = slice(qs * s, qs * (s + 1))
        kws = [_window(kp_ref, kc_ref, qs * s, qs * s + kw_rows, pb, slice(128 * kv, 128 * (kv + 1))) for kv in range(KVC)]
        vws = [_window(vp_ref, vc_ref, qs * s, qs * s + kw_rows, pb, slice(128 * kv, 128 * (kv + 1))) for kv in range(KVC)]
        heads = [(j, hh) for j in range(HC // 2) for hh in range(2)]
        qps = [q_ref[rows, 128 * j:128 * (j + 1)].astype(F32) for j in range(HC // 2)]
        per_kv = 2 * pairs_per_kv
        qq = [jnp.concatenate([(qps[j] * hmask[hh]).astype(BF16) for j, hh in heads[per_kv * kv:per_kv * (kv + 1)]],
                              axis=0) for kv in range(KVC)]
        sc2 = [_dot_nt(qq[kv], kws[kv]) for kv in range(KVC)]
        scs = [sc2[u // per_kv][qs * (u % per_kv):qs * (u % per_kv + 1)] + msk for u in range(len(heads))]
        sks = [sink_ref[0, 2 * j + hh] for j, hh in heads]
        ms = [jnp.maximum(jnp.max(sc, axis=-1, keepdims=True), sk) for sc, sk in zip(scs, sks)]
        pes = [jnp.exp2(sc - m) for sc, m in zip(scs, ms)]
        pp = [jnp.concatenate([pe.astype(BF16) for pe in pes[per_kv * kv:per_kv * (kv + 1)]], axis=0) for kv in range(KVC)]
        o2 = [_dot(pp[kv], vws[kv]) for kv in range(KVC)]
        outs = [o2[u // per_kv][qs * (u % per_kv):qs * (u % per_kv + 1)]
                / (jnp.sum(pes[u], axis=-1, keepdims=True) + jnp.exp2(sks[u] - ms[u])) for u in range(len(heads))]
        for j in range(HC // 2):
            o_ref[rows, 128 * j:128 * (j + 1)] = jnp.where(low, outs[2 * j], outs[2 * j + 1]).astype(BF16)


def _route(logits_t):
    a = [logits_t[4 * j:4 * j + 4] for j in range(EXP_PER_GROUP)]

    def first_argmax(vals, m):
        idx = jnp.full(m.shape, float(len(vals) - 1), F32)
        for j in reversed(range(len(vals) - 1)):
            idx = jnp.where(vals[j] == m, float(j), idx)
        return idx

    m1 = functools.reduce(jnp.maximum, a)
    i1 = first_argmax(a, m1)
    bsec = [jnp.where(i1 == float(j), -jnp.inf, a[j]) for j in range(EXP_PER_GROUP)]
    m2 = functools.reduce(jnp.maximum, bsec)
    i2 = first_argmax(bsec, m2)
    rows = lambda x: [x[gi:gi + 1] for gi in range(N_GROUPS)]
    gm = functools.reduce(jnp.maximum, rows(m1))
    gscore = jnp.exp(m1 - gm) + jnp.exp(m2 - gm)
    gs = rows(gscore)
    gsel = first_argmax(gs, functools.reduce(jnp.maximum, gs))

    def pick(x):
        xr = rows(x)
        out = xr[N_GROUPS - 1]
        for gi in reversed(range(N_GROUPS - 1)):
            out = jnp.where(gsel == float(gi), xr[gi], out)
        return out

    p1 = jnp.exp(pick(m1) - gm)
    p2 = jnp.exp(pick(m2) - gm)
    w1 = p1 / (p1 + p2)
    w2 = p2 / (p1 + p2)
    s1, s2 = pick(i1), pick(i2)
    lo, hi = jnp.minimum(s1, s2), jnp.maximum(s1, s2)
    pair = jnp.where(lo == 0.0, hi - 1.0, jnp.where(lo == 1.0, hi + 1.0, 5.0))
    bucket = gsel * float(N_PAIRS) + pair
    first_is_lo = s1 < s2
    return bucket, jnp.where(first_is_lo, w1, w2), jnp.where(first_is_lo, w2, w1)


def _outproj_kernel(*refs, n_x, n_o, n_prompt_tiles):
    x_refs = refs[:n_x]
    o_refs = refs[n_x:n_x + n_o]
    w_refs = refs[n_x + n_o:n_x + 2 * n_o]
    (gate_ref, nf_ref, sh_ref, sc_ref, wr_ref, br_ref, tri_ref,
     xn_ref, disp_ref, meta_ref, cnt_ref, run_ref, xbuf, xsem) = refs[n_x + 2 * n_o:]
    t = xn_ref.shape[0]

    @pl.when(pl.program_id(0) == 0)
    def _():
        run_ref[...] = jnp.zeros_like(run_ref)

    x_src = _ring_block(list(x_refs), xbuf, xsem, n_prompt_tiles)

    halves = [slice(0, t // 2), slice(t // 2, t)]
    grp = [slice(0, t // (2 * CHUNK)), slice(t // (2 * CHUNK), t // CHUNK)]
    ys = []
    for rs in halves:
        y = _dot(o_refs[0][rs, :], w_refs[0][...])
        for i in range(1, n_o):
            y = y + _dot(o_refs[i][rs, :], w_refs[i][...])
        ys.append(y)
    gate, shift, scale = gate_ref[...], sh_ref[...], sc_ref[...]
    gys = [_group_affine(y, gate[gs], None) for y, gs in zip(ys, grp)]

    for rs, gy in zip(halves, gys):
        xn_ref[rs, :] = x_src[rs, :] + gy
    hs = [_norm_mod(xn_ref[rs, :], nf_ref[...], shift[gs], scale[gs]) for rs, gs in zip(halves, grp)]
    for rs, h in zip(halves, hs):
        for s, slab in enumerate(_pack_pairs(h)):
            disp_ref[s, rs, :] = slab
    logits_t = [(_dot3_narrow(h, wr_ref[...]) + br_ref[...]).T[0:N_EXPERTS] for h in hs]
    bucket, w_lo, w_hi = _route(jnp.concatenate(logits_t, axis=1))
    r128 = lax.broadcasted_iota(jnp.int32, (128, t), 0)
    tok = (pl.program_id(0) * t + lax.broadcasted_iota(jnp.int32, (1, t), 1)).astype(F32)
    aux = jnp.where(r128 == 0, w_lo, jnp.where(r128 == 1, w_hi, jnp.where(r128 == 2, tok, 0.0))).T
    disp_ref[disp_ref.shape[0] - 1] = pltpu.bitcast(aux, jnp.int32)
    brow = lax.broadcasted_iota(jnp.int32, (BUCKET_ROWS, t), 0).astype(F32)
    onehot = jnp.where(brow == bucket, 1.0, 0.0)
    before = _dot(onehot.astype(BF16), tri_ref[...]) + run_ref[:, 0:1]
    rank = jnp.sum(onehot * before, axis=0, keepdims=True)
    run_ref[...] = run_ref[...] + jnp.sum(onehot, axis=1, keepdims=True)
    cnt_ref[...] = run_ref[...]
    r8 = lax.broadcasted_iota(jnp.int32, (8, t), 0)
    meta_ref[...] = jnp.where(r8 == 0, bucket, jnp.where(r8 == 1, rank, 0.0)).astype(jnp.int32)


def _outproj_call(xs_, os_, ws, mods, nf, wr, br, n_pad):
    d = xs_[0].shape[1]
    n = sum(a.shape[0] for a in xs_)
    npt = xs_[0].shape[0] // TM
    row = lambda i: (i, 0)
    const = lambda i: (0, 0)
    n_o = len(os_)
    in_specs = ([pl.BlockSpec(memory_space=pl.ANY) for _ in xs_]
                + [pl.BlockSpec((TM, o.shape[1]), row) for o in os_]
                + [pl.BlockSpec(w.shape, const) for w in ws]
                + [_mod_spec(GATE_MIX), pl.BlockSpec((1, d), const),
                   _mod_spec(SHIFT_FFN), _mod_spec(SCALE_FFN),
                   pl.BlockSpec(wr.shape, const), pl.BlockSpec(br.shape, const),
                   pl.BlockSpec((TM, TM), const)])
    tri = jnp.asarray(np.triu(np.ones((TM, TM), np.float32), k=1), dtype=BF16)
    return pl.pallas_call(
        functools.partial(_outproj_kernel, n_x=len(xs_), n_o=n_o, n_prompt_tiles=npt),
        grid=(n // TM,),
        in_specs=in_specs,
        out_specs=[pl.BlockSpec((TM, d), row), pl.BlockSpec((DISP_SLABS, TM, 128), lambda i: (0, i, 0)),
                   pl.BlockSpec((8, TM), lambda i: (0, i)), pl.BlockSpec((BUCKET_ROWS, 128), const)],
        out_shape=[jax.ShapeDtypeStruct((n, d), F32), jax.ShapeDtypeStruct((DISP_SLABS, n_pad, 128), jnp.int32),
                   jax.ShapeDtypeStruct((8, n), jnp.int32), jax.ShapeDtypeStruct((BUCKET_ROWS, 128), F32)],
        scratch_shapes=[pltpu.VMEM((BUCKET_ROWS, 128), F32)] + _ring_scratch(d),
        compiler_params=_cparams(("arbitrary",)),
        name="outproj_router",
    )(*xs_, *os_, *ws, mods, nf, mods, mods, wr, br, tri)


def _sc_mesh():
    return plsc.VectorSubcoreMesh(core_axis_name="core", subcore_axis_name="subcore")


def _sc_scatter_rows(src, idx, n_out):
    r = idx.shape[0]
    k = SC_GROUP
    w_per = r // (SC_WINDOW * SC_WORKERS)
    assert idx.shape == (src.shape[0],) and r % (SC_WINDOW * SC_WORKERS) == 0 and w_per % k == 0
    n_groups = w_per // k

    @functools.partial(
        pl.kernel, out_type=jax.ShapeDtypeStruct((n_out, 128), src.dtype), mesh=_sc_mesh(),
        scratch_types=[pltpu.VMEM((w_per, SC_WINDOW), jnp.int32),
                       pltpu.VMEM((2 * k, SC_WINDOW, 128), src.dtype),
                       pltpu.SemaphoreType.DMA((2,)), pltpu.SemaphoreType.DMA((2,))])
    def copy(x_hbm, i_hbm, o_hbm, ibuf, xbuf, in_sem, out_sem):
        wid = lax.axis_index("core") * (SC_WORKERS // 2) + lax.axis_index("subcore")
        pltpu.sync_copy(i_hbm.at[wid], ibuf)
        first = wid * w_per

        def start_in(g, slot):
            return [pltpu.async_copy(x_hbm.at[pl.ds((first + g * k + c) * SC_WINDOW, SC_WINDOW)],
                                     xbuf.at[slot * k + c], in_sem.at[slot]) for c in range(k)]

        def start_out(g, slot):
            return [pltpu.async_copy(xbuf.at[slot * k + c], o_hbm.at[ibuf.at[g * k + c]], out_sem.at[slot])
                    for c in range(k)]

        pending_in = start_in(0, 0)
        for g in range(n_groups):
            slot = g % 2
            for cp in pending_in:
                cp.wait()
            pending_out = start_out(g, slot)
            if g + 1 < n_groups:
                pending_in = start_in(g + 1, 1 - slot)
            for cp in pending_out:
                cp.wait()

    return copy(src, idx.reshape(SC_WORKERS, w_per, SC_WINDOW))


def _moe_kernel(elo_ref, ehi_ref, nvalid_ref, xs_ref, *refs, n_tok, dump_tiles):
    w_refs, (y_ref, tok_ref) = refs[:2 * MOE_TILES], refs[2 * MOE_TILES:]
    d_in = w_refs[0].shape[2] - D_FF
    step = pl.program_id(0)
    t = TMO
    tiles = range(MOE_TILES)
    rows = [slice(t * j, t * (j + 1)) for j in tiles]
    auxs = [pltpu.bitcast(xs_ref[Y_SLABS, rows[j], :], F32) for j in tiles]
    r = lax.broadcasted_iota(jnp.int32, (1, t), 1)
    for j in tiles:
        i = step * MOE_TILES + j
        spare = n_tok + (i % dump_tiles) * t + r
        tok = jnp.where(r < nvalid_ref[i], auxs[j].T[2:3, :].astype(jnp.int32), spare)
        for c in range(t // 128):
            tok_ref[j, c:c + 1, :] = tok[:, 128 * c:128 * (c + 1)]

    any_tokens = nvalid_ref[step * MOE_TILES] > 0
    for j in range(1, MOE_TILES):
        any_tokens = jnp.logical_or(any_tokens, nvalid_ref[step * MOE_TILES + j] > 0)

    @pl.when(any_tokens)
    def _():
        units = [(j, e) for j in tiles for e in range(2)]
        hs = [_unpack_pairs([xs_ref[s, rows[j], :] for s in range(Y_SLABS)], BF16) for j in tiles]
        abs_ = [_dot(hs[j], w_refs[2 * j + e][0, 0, 0:d_in, :]) for j, e in units]
        acts = [(ab[:, :D_FF] * _sigmoid(ab[:, :D_FF]) * ab[:, D_FF:]).astype(BF16) for ab in abs_]
        ys = [_dot(act, w_refs[2 * j + e][0, 0, d_in:d_in + D_FF, :]) for act, (j, e) in zip(acts, units)]
        for j in tiles:
            acc = auxs[j][:, 0:1] * ys[2 * j] + auxs[j][:, 1:2] * ys[2 * j + 1]
            for s, slab in enumerate(_pack_pairs(acc)):
                y_ref[s, rows[j], :] = slab

    @pl.when(jnp.logical_not(any_tokens))
    def _():
        y_ref[...] = jnp.zeros_like(y_ref)


def _moe_call(xs, elo, ehi, nvalid, wexp, n_tiles, n_tok, dump_tiles):
    m = MOE_TILES
    assert n_tiles % m == 0 and wexp.shape[3] == 2 * D_FF
    weight_specs = [pl.BlockSpec((1, 1) + wexp.shape[2:],
                                 lambda i, lo, hi, v, j=j, sel=sel: (0, (lo, hi)[sel][m * i + j], 0, 0))
                    for j in range(m) for sel in range(2)]
    weights = [wexp] * (2 * m)
    return pl.pallas_call(
        functools.partial(_moe_kernel, n_tok=n_tok, dump_tiles=dump_tiles),
        grid_spec=pltpu.PrefetchScalarGridSpec(
            num_scalar_prefetch=3,
            grid=(n_tiles // m,),
            in_specs=[pl.BlockSpec((DISP_SLABS, m * TMO, 128), lambda i, lo, hi, v: (0, i, 0))] + weight_specs,
            out_specs=[pl.BlockSpec((Y_SLABS, m * TMO, 128), lambda i, lo, hi, v: (0, i, 0)),
                       pl.BlockSpec((m, TMO // 128, 128), lambda i, lo, hi, v: (i, 0, 0))]),
        out_shape=[jax.ShapeDtypeStruct((Y_SLABS, n_tiles * TMO, 128), jnp.int32),
                   jax.ShapeDtypeStruct((n_tiles, TMO // 128, 128), jnp.int32)],
        compiler_params=_cparams(("arbitrary",), vmem_mb=VMEM_LIMIT_MOE_MB),
        name="moe_grouped",
    )(elo, ehi, nvalid, xs, *weights)


def _after(x, token):
    return lax.optimization_barrier((x, token))[0]


def _cast_kernel(after_ref, wgu_ref, wdn_ref, o_ref):
    k = wgu_ref.shape[2]
    o_ref[0, 0, 0:k, :] = wgu_ref[0, 0].astype(o_ref.dtype)
    o_ref[0, 0, k:, :] = wdn_ref[0, 0].astype(o_ref.dtype)


def _cast_call(w_gate_up, w_down, layer, after):
    _, e, k, n = w_gate_up.shape
    kd = w_down.shape[2]
    assert w_down.shape[3] == n
    return pl.pallas_call(
        _cast_kernel,
        grid=(e,),
        in_specs=[pl.BlockSpec(memory_space=pl.ANY),
                  pl.BlockSpec((1, 1, k, n), lambda i: (layer, i, 0, 0)),
                  pl.BlockSpec((1, 1, kd, n), lambda i: (layer, i, 0, 0))],
        out_specs=pl.BlockSpec((1, 1, k + kd, n), lambda i: (0, i, 0, 0)),
        out_shape=jax.ShapeDtypeStruct((1, e, k + kd, n), BF16),
        compiler_params=_cparams(("parallel",)),
        name="cast_weights",
    )(after, w_gate_up, w_down)


def _moe_layer(disp, meta, counts, w_gate_up, w_down, layer, n, n_pad, sort_rows, cast_after=None):
    n_tiles = sort_rows // TMO
    wexp = _cast_call(w_gate_up, w_down, layer, counts if cast_after is None else cast_after)
    cnt = counts[:N_BUCKETS, 0].astype(jnp.int32)
    padded = ((cnt + TMO - 1) // TMO) * TMO
    ends = jnp.cumsum(padded)
    offs = ends - padded
    bucket, rank = meta[0], meta[1]
    pos = rank + jnp.sum(jnp.where(bucket[None, :] == jnp.arange(N_BUCKETS, dtype=jnp.int32)[:, None],
                                   offs[:, None], 0), axis=0)
    tile_start = jnp.arange(n_tiles, dtype=jnp.int32) * TMO
    tile_bucket = jnp.minimum(jnp.sum((tile_start[:, None] >= ends[None, :]).astype(jnp.int32), axis=1), N_BUCKETS - 1)
    pair_lo = np.array([0, 0, 0, 1, 1, 2], np.int32)
    pair_hi = np.array([1, 2, 3, 2, 3, 3], np.int32)
    b_lo = jnp.asarray(np.repeat(np.arange(N_GROUPS), N_PAIRS) * EXP_PER_GROUP + np.tile(pair_lo, N_GROUPS), jnp.int32)
    b_hi = jnp.asarray(np.repeat(np.arange(N_GROUPS), N_PAIRS) * EXP_PER_GROUP + np.tile(pair_hi, N_GROUPS), jnp.int32)
    onehot_tb = (tile_bucket[:, None] == jnp.arange(N_BUCKETS, dtype=jnp.int32)[None, :]).astype(jnp.int32)
    elo = jnp.sum(onehot_tb * b_lo[None, :], axis=1)
    ehi = jnp.sum(onehot_tb * b_hi[None, :], axis=1)
    bucket_end = jnp.sum(onehot_tb * (offs + cnt)[None, :], axis=1)
    nvalid = jnp.where(tile_start < ends[-1], jnp.clip(bucket_end - tile_start, 0, TMO), 0)
    dump = sort_rows + jnp.arange(n_pad - n, dtype=jnp.int32)
    pos_sc = jnp.concatenate([pos, dump])
    total = sort_rows + n_pad - n
    sc_idx = (pos_sc[None, :] + (jnp.arange(DISP_SLABS, dtype=jnp.int32) * total)[:, None]).reshape(-1)
    xs = _sc_scatter_rows(disp.reshape(DISP_SLABS * n_pad, 128), sc_idx, DISP_SLABS * total)
    ys, tok = _moe_call(xs.reshape(DISP_SLABS, total, 128), elo, ehi, nvalid, wexp, n_tiles,
                        n, (n_pad - n) // TMO)
    back_idx = (tok.reshape(1, sort_rows) + (jnp.arange(Y_SLABS, dtype=jnp.int32) * n_pad)[:, None]).reshape(-1)
    z = _sc_scatter_rows(ys.reshape(Y_SLABS * sort_rows, 128), back_idx, Y_SLABS * n_pad)
    return z.reshape(Y_SLABS, n_pad, 128), tok


def _final_kernel(xn_hbm, z_ref, gate_ref, g_ref, yp_ref, ys_ref, xbuf, xsem, *, n_prompt_tiles):
    x = _add_moe(_ring_block([xn_hbm], xbuf, xsem), z_ref, gate_ref)
    ms = jnp.mean(x * x, axis=-1, keepdims=True)
    y = x * lax.rsqrt(ms + EPS) * g_ref[...]
    i = pl.program_id(0)

    @pl.when(i < n_prompt_tiles)
    def _():
        yp_ref[...] = y

    @pl.when(i >= n_prompt_tiles)
    def _():
        ys_ref[...] = y


def _final_call(xn, z, mods, g, n_prompt):
    n, d = xn.shape
    npt = n_prompt // TM
    assert n - n_prompt == TM
    return pl.pallas_call(
        functools.partial(_final_kernel, n_prompt_tiles=npt),
        grid=(n // TM,),
        in_specs=[pl.BlockSpec(memory_space=pl.ANY), pl.BlockSpec((z.shape[0], TM, 128), lambda i: (0, i, 0)),
                  _mod_spec(GATE_FFN), pl.BlockSpec((1, d), lambda i: (0, 0))],
        out_specs=_token_specs(npt, d),
        out_shape=[jax.ShapeDtypeStruct((n_prompt, d), F32), jax.ShapeDtypeStruct((TM, d), F32)],
        scratch_shapes=_ring_scratch(d),
        compiler_params=_cparams(("arbitrary",)),
        name="final_norm",
    )(xn, z, mods, g)


def kernel(x_prompt, x_sample, c_prompt, c_sample, state_gla, cache_band_k, cache_band_v, cache_swa_k, cache_swa_v,
           w_ada, b_ada, norm_mix, norm_ffn, norm_final, w_in_even, w_gate_a, b_gate_a, gla_norm, rel_bias_b,
           w_out_even, w_in_odd, sinks_c, w_out_odd, w_router, b_router, w_gate_up, w_down):
    bp, lp, d = x_prompt.shape
    bs, ls_, _ = x_sample.shape
    n_p, n_s = bp * lp, bs * ls_
    n = n_p + n_s
    assert ls_ == CHUNK and n_s == TM and lp % TM == 0 and PAST_LEN % CHUNK == 0

    xp2, xs2 = x_prompt.reshape(n_p, d), x_sample.reshape(n_s, d)

    c16 = jnp.zeros((SEQ_ROWS, d), F32).at[:bp].set(c_prompt).at[bp:bp + bs].set(c_sample)
    mods = _ada_call(c16, w_ada, b_ada)
    seq_of_group = np.concatenate([np.repeat(np.arange(bp), lp // CHUNK), bp + np.arange(bs)])
    mods_g = [mods[l][seq_of_group] for l in range(DEPTH)]

    perm = np.array([4 * (c % 4) + c // 4 for c in range(N_EXPERTS)])
    wr = jnp.zeros((d, 128), F32).at[:, :N_EXPERTS].set(w_router[:, perm])
    br = jnp.zeros((1, 128), F32).at[0, :N_EXPERTS].set(b_router[perm])

    sc_unit = SC_WINDOW * SC_WORKERS * SC_GROUP
    n_pad = n + TMO
    while (DISP_SLABS * n_pad) % sc_unit or (Y_SLABS * n_pad) % TMO or (n_pad - n) % TMO:
        n_pad += TMO
    sort_rows = n + N_BUCKETS * TMO
    while (Y_SLABS * sort_rows) % sc_unit or sort_rows % (MOE_TILES * TMO):
        sort_rows += TMO

    gla_p = gla_s = bk_p = bv_p = bk_s = bv_s = sk_p = sv_p = sk_s = sv_s = None
    xn = z = tok = None
    for l in range(DEPTH):
        i = l // 2
        if l % 2 == 0:
            w = w_in_even[i]
            w_main = jnp.concatenate([w[:, :1536], w[:, 1552:]], axis=1).astype(BF16)
            w_la = jnp.zeros((d, 128), F32).at[:, :GATE_RANK].set(w[:, 1536:1552]).astype(BF16)
            w_gate = jnp.zeros((128, HA * DKA), F32).at[:GATE_RANK].set(w_gate_a[i])
            qa, ka, va, ra, qb, kb, vb, ga = _inproj_even_call(
                xp2, xs2, mods_g[l], norm_mix[l][None], w_main, w_la, w_gate, b_gate_a[i][None])
            xres = [xp2, xs2]
            gn = gla_norm[i][None]
            oa, s_p = _gla_call(qa, ka, va, ga, ra, jnp.zeros((bp, 256, 128), F32), gn, None,
                                n_seq=bp, seq_rows=lp, row0=0, nb=8)
            oa, s_s = _gla_call(qa, ka, va, ga, ra, state_gla[i].reshape(bs, 256, 128), gn, oa,
                                n_seq=bs, seq_rows=ls_, row0=n_p, nb=1)
            gla_p, gla_s = s_p.reshape(1, bp, HA, DKA, DVA), s_s.reshape(1, bs, HA, DKA, DVA)
            pb = N_PREV_B * CHUNK
            tq, g = 512, 2
            ck = cache_band_k[i].reshape(bs * pb, HB * DHB).astype(BF16)
            cv = cache_band_v[i].reshape(bs * pb, HB * DHB).astype(BF16)
            biases = (_band_bias(rel_bias_b[i], g, pb, _band_valid(g, pb)),
                      _band_bias(rel_bias_b[i], g, pb, _band_valid(g, pb, tq // (CHUNK * g))),
                      _band_bias(rel_bias_b[i], 1, pb, _band_valid(1, pb)))
            ob = _attention(_band_kernel, qb, kb, vb, ck, cv, biases, [], [], width=512, kv_width=512, pb=pb,
                            tq=tq, g=g, bp=bp, lp=lp, bs=bs, name="band")
            tail = lambda a: jnp.stack([a[(b + 1) * lp - pb:(b + 1) * lp] for b in range(bp)]).astype(F32).reshape(1, bp, pb, HB, DHB)
            new = lambda a: a[n_p:].astype(F32).reshape(bs, ls_, HB, DHB)
            bk_p, bv_p = tail(kb), tail(vb)
            bk_s = jnp.concatenate([cache_band_k[i][:, ls_:], new(kb)], axis=1)[None]
            bv_s = jnp.concatenate([cache_band_v[i][:, ls_:], new(vb)], axis=1)[None]
            wo = w_out_even[i].astype(BF16)
            os_, ws = [oa, ob], [wo[:HA * DVA], wo[HA * DVA:]]
        else:
            w = _after(w_in_odd[i], tok)
            w_out_l = _after(w_out_odd[i], tok)
            cache_k_l, cache_v_l = _after(cache_swa_k[i], tok), _after(cache_swa_v[i], tok)
            wk, wv = w[:, 1024:1152], w[:, 1152:1280]
            dup = lambda a: jnp.concatenate([a[:, :64], a[:, :64], a[:, 64:], a[:, 64:]], axis=1)
            w_all = jnp.concatenate([w[:, :1024], dup(wk), dup(wv)], axis=1).astype(BF16)
            cos, sin, rope_map = _rope_tables(lp, ls_, bp, bs)
            x, q, k, v = _inproj_odd_call(xn, z, mods_g[l - 1], mods_g[l], norm_mix[l][None], cos, sin, rope_map, w_all)
            xres = [x]
            pb = WINDOW
            tq, g = 512, 2
            sink = sinks_c[i][None] * LOG2E
            sink_spec = [pl.BlockSpec(memory_space=pltpu.SMEM)]
            dupc = lambda c: jnp.concatenate([c[:, :, 0], c[:, :, 0], c[:, :, 1], c[:, :, 1]], axis=-1).reshape(bs * pb, 256).astype(BF16)
            ck, cv = dupc(cache_k_l), dupc(cache_v_l)
            additive = lambda valid: jnp.asarray(np.where(valid, 0.0, -np.inf), F32)
            masks = (additive(_band_valid(g, pb)), additive(_band_valid(g, pb, tq // (CHUNK * g))),
                     additive(_band_valid(1, pb)))
            o = _attention(_swa_kernel, q, k, v, ck, cv, masks, [sink], sink_spec, width=1024, kv_width=256, pb=pb,
                           tq=tq, g=g, bp=bp, lp=lp, bs=bs, name="swa")
            undup = lambda a: jnp.concatenate([a[:, 0:64], a[:, 128:192]], axis=1).astype(F32)
            tail = lambda a: jnp.stack([undup(a[(b + 1) * lp - pb:(b + 1) * lp]) for b in range(bp)]).reshape(1, bp, pb, KVC, DHC)
            new = lambda a: undup(a[n_p:]).reshape(bs, ls_, KVC, DHC)
            sk_p, sv_p = tail(k), tail(v)
            sk_s = jnp.concatenate([cache_swa_k[i][:, ls_:], new(k)], axis=1)[None]
            sv_s = jnp.concatenate([cache_swa_v[i][:, ls_:], new(v)], axis=1)[None]
            os_, ws = [o], [w_out_l.astype(BF16)]
        xn, disp, meta, counts = _outproj_call(xres, os_, ws, mods_g[l], norm_ffn[l][None], wr, br, n_pad)
        z, tok = _moe_layer(disp, meta, counts, w_gate_up, w_down, l, n, n_pad, sort_rows, tok if l else None)

    y_prompt, y_sample = _final_call(xn, z, mods_g[DEPTH - 1], norm_final[None], n_p)
    return (y_prompt.reshape(bp, lp, d), y_sample.reshape(bs, ls_, d),
            gla_p, gla_s, bk_p, bv_p, bk_s, bv_s, sk_p, sv_p, sk_s, sv_s)
```
